```python
import math
import jax, jax.numpy as jnp
from jax import lax
import numpy as np

D_MODEL = 1024
BATCH = 8
SEQ = 8192
DEPTH = 2

EXPAND = 2
D_INNER = EXPAND * D_MODEL
S5_WIDTH = D_INNER // 2
S5_GROUP = 16
S5_GROUPS = S5_WIDTH // S5_GROUP
S5_STATE = 64
RET_WIDTH = D_INNER - S5_WIDTH
RET_HEADS = 4
RET_DK = RET_WIDTH // RET_HEADS
RET_DV = RET_WIDTH // RET_HEADS
RET_CHUNK = 128
ROPE_BASE = 10000.0
EVEN_SPLITS = [S5_WIDTH, S5_WIDTH, RET_HEADS * RET_DK, RET_HEADS * RET_DK, RET_HEADS * RET_DV, RET_HEADS * RET_DV]
EVEN_IN = sum(EVEN_SPLITS)
SGU_WIDTH = D_INNER
SGU_GROUPS = 4
SGU_GROUP_DIM = SGU_WIDTH // SGU_GROUPS
SGU_CHUNK = 128
ODD_IN = 3 * SGU_WIDTH
N_EVEN = (DEPTH + 1) // 2
N_ODD = DEPTH // 2
NORM_EPS = 1e-6

kernel_name = "hybrid_s5_retention_sgu_block"

F32 = jnp.float32


def rms_norm(x, g):
    xf = x.astype(F32)
    y = xf * lax.rsqrt(jnp.mean(xf * xf, axis=-1, keepdims=True) + NORM_EPS)
    return (y * g.astype(F32)).astype(x.dtype)


def rotary(x, pos):
    half = x.shape[-1] // 2
    inv = ROPE_BASE ** (-jnp.arange(half, dtype=F32) / half)
    ang = pos[:, None] * inv[None, :]
    cos = jnp.cos(ang)[None, :, None, :]
    sin = jnp.sin(ang)[None, :, None, :]
    x1, x2 = x[..., :half], x[..., half:]
    return jnp.concatenate([x1 * cos - x2 * sin, x1 * sin + x2 * cos], axis=-1)


def s5_branch(u, lam_re, lam_im, log_dt, b_re, b_im, c_re, c_im, d_skip, w_glu, b_glu):
    bsz, seq, _ = u.shape
    uf = u.astype(F32).reshape(bsz, seq, S5_GROUPS, S5_GROUP)
    lr = jnp.minimum(lam_re.astype(F32), -1e-4)
    li = lam_im.astype(F32)
    dt = jnp.exp(log_dt.astype(F32))[:, None]
    mag = jnp.exp(lr * dt)
    ab_re = mag * jnp.cos(li * dt)
    ab_im = mag * jnp.sin(li * dt)
    den = lr * lr + li * li
    n_re = ab_re - 1.0
    n_im = ab_im
    z_re = (n_re * lr + n_im * li) / den
    z_im = (n_im * lr - n_re * li) / den
    br = b_re.astype(F32)
    bi = b_im.astype(F32)
    bb_re = z_re[..., None] * br - z_im[..., None] * bi
    bb_im = z_re[..., None] * bi + z_im[..., None] * br
    bu_re = jnp.einsum('gph,blgh->blgp', bb_re, uf)
    bu_im = jnp.einsum('gph,blgh->blgp', bb_im, uf)
    a_re = jnp.broadcast_to(ab_re, bu_re.shape)
    a_im = jnp.broadcast_to(ab_im, bu_im.shape)

    def combine(left, right):
        a1r, a1i, b1r, b1i = left
        a2r, a2i, b2r, b2i = right
        return (a2r * a1r - a2i * a1i,
                a2r * a1i + a2i * a1r,
                a2r * b1r - a2i * b1i + b2r,
                a2r * b1i + a2i * b1r + b2i)

    _, _, s_re, s_im = lax.associative_scan(combine, (a_re, a_im, bu_re, bu_im), axis=1)
    y = (jnp.einsum('ghp,blgp->blgh', c_re.astype(F32), s_re)
         - jnp.einsum('ghp,blgp->blgh', c_im.astype(F32), s_im))
    y = y + d_skip.astype(F32).reshape(S5_GROUPS, S5_GROUP) * uf
    y = jax.nn.gelu(y.reshape(bsz, seq, S5_WIDTH))
    y = y * jax.nn.sigmoid(y @ w_glu.astype(F32) + b_glu.astype(F32))
    return y.astype(u.dtype)


def retention_branch(q, k, v, gn_gain):
    bsz, seq, _ = q.shape
    nc = seq // RET_CHUNK
    pos = jnp.arange(seq, dtype=F32)
    qh = rotary(q.astype(F32).reshape(bsz, seq, RET_HEADS, RET_DK), pos)
    kh = rotary(k.astype(F32).reshape(bsz, seq, RET_HEADS, RET_DK), pos) * (RET_DK ** -0.5)
    vh = v.astype(F32).reshape(bsz, seq, RET_HEADS, RET_DV)
    log_g = jnp.log1p(-jnp.exp2(-5.0 - jnp.arange(RET_HEADS, dtype=F32)))
    idx = jnp.arange(RET_CHUNK, dtype=F32)
    diff = idx[:, None] - idx[None, :]
    decay = jnp.where(diff >= 0, jnp.exp(log_g[:, None, None] * jnp.maximum(diff, 0.0)), 0.0)
    xi = jnp.exp(log_g[None, :] * (idx[:, None] + 1.0))
    zeta = jnp.exp(log_g[None, :] * (RET_CHUNK - 1.0 - idx[:, None]))
    chunk_decay = jnp.exp(log_g * RET_CHUNK)
    qc = qh.reshape(bsz, nc, RET_CHUNK, RET_HEADS, RET_DK)
    kc = kh.reshape(bsz, nc, RET_CHUNK, RET_HEADS, RET_DK)
    vc = vh.reshape(bsz, nc, RET_CHUNK, RET_HEADS, RET_DV)
    scores = jnp.einsum('bcnhk,bcmhk->bchnm', qc, kc) * decay[None, None]
    inner = jnp.einsum('bchnm,bcmhv->bcnhv', scores, vc)
    local = jnp.einsum('bcmhk,bcmhv->bchkv', kc * zeta[None, None, :, :, None], vc)

    def step(state, s_chunk):
        return state * chunk_decay[None, :, None, None] + s_chunk, state

    init = jnp.zeros((bsz, RET_HEADS, RET_DK, RET_DV), F32)
    _, prev = lax.scan(step, init, jnp.moveaxis(local, 1, 0))
    prev = jnp.moveaxis(prev, 0, 1)
    cross = jnp.einsum('bcnhk,bchkv->bcnhv', qc * xi[None, None, :, :, None], prev)
    o = (inner + cross).reshape(bsz, seq, RET_HEADS, RET_DV)
    mu = jnp.mean(o, axis=-1, keepdims=True)
    var = jnp.mean(jnp.square(o - mu), axis=-1, keepdims=True)
    o = (o - mu) * lax.rsqrt(var + NORM_EPS)
    o = o.reshape(bsz, seq, RET_HEADS * RET_DV) * gn_gain.astype(F32)
    return o.astype(q.dtype)


def spatial_gating_branch(h, v_gain, w_s, b_s):
    bsz, seq, _ = h.shape
    nc = seq // SGU_CHUNK
    u, v = h[..., :SGU_WIDTH], h[..., SGU_WIDTH:]
    vf = v.astype(F32)
    mu = jnp.mean(vf, axis=-1, keepdims=True)
    var = jnp.mean(jnp.square(vf - mu), axis=-1, keepdims=True)
    vf = (vf - mu) * lax.rsqrt(var + NORM_EPS) * v_gain.astype(F32)
    vc = vf.reshape(bsz, nc, SGU_CHUNK, SGU_GROUPS, SGU_GROUP_DIM)
    mask = jnp.tril(jnp.ones((SGU_CHUNK, SGU_CHUNK), dtype=bool))
    w = jnp.where(mask[None], w_s.astype(F32), 0.0)
    s = jnp.einsum('gts,bcsgd->bctgd', w, vc) + b_s.astype(F32).T[None, None, :, :, None]
    return (u.astype(F32) * s.reshape(bsz, seq, SGU_WIDTH)).astype(h.dtype)


def _fwd_setup_inputs(seed: int = 0) -> dict:
    key = jax.random.key(seed)
    ks = jax.random.split(key, 24)
    nrm = lambda k, shape, scale: jax.random.normal(k, shape, F32) * scale
    x = jax.random.normal(ks[0], (BATCH, SEQ, D_MODEL), F32)
    norm_even = 1.0 + nrm(ks[1], (N_EVEN, D_MODEL), 0.02)
    w_in_even = nrm(ks[2], (N_EVEN, D_MODEL, EVEN_IN), D_MODEL ** -0.5)
    s5_lam_re = -0.5 + nrm(ks[3], (N_EVEN, S5_GROUPS, S5_STATE), 0.01)
    s5_lam_im = math.pi * jnp.arange(S5_STATE, dtype=F32)[None, None, :] + nrm(ks[4], (N_EVEN, S5_GROUPS, S5_STATE), 0.01)
    s5_log_dt = jax.random.uniform(ks[5], (N_EVEN, S5_GROUPS), F32, math.log(0.001), math.log(0.1))
    s5_b_re = nrm(ks[6], (N_EVEN, S5_GROUPS, S5_STATE, S5_GROUP), (2 * S5_GROUP) ** -0.5)
    s5_b_im = nrm(ks[7], (N_EVEN, S5_GROUPS, S5_STATE, S5_GROUP), (2 * S5_GROUP) ** -0.5)
    s5_c_re = nrm(ks[8], (N_EVEN, S5_GROUPS, S5_GROUP, S5_STATE), (2 * S5_STATE) ** -0.5)
    s5_c_im = nrm(ks[9], (N_EVEN, S5_GROUPS, S5_GROUP, S5_STATE), (2 * S5_STATE) ** -0.5)
    s5_d = nrm(ks[10], (N_EVEN, S5_WIDTH), 1.0)
    s5_w_glu = nrm(ks[11], (N_EVEN, S5_WIDTH, S5_WIDTH), S5_WIDTH ** -0.5)
    s5_b_glu = nrm(ks[12], (N_EVEN, S5_WIDTH), 0.01)
    ret_gn_gain = 1.0 + nrm(ks[13], (N_EVEN, RET_HEADS * RET_DV), 0.02)
    w_out_even = nrm(ks[14], (N_EVEN, D_INNER, D_MODEL), D_INNER ** -0.5)
    norm_odd = 1.0 + nrm(ks[15], (N_ODD, D_MODEL), 0.02)
    w_in_odd = nrm(ks[16], (N_ODD, D_MODEL, ODD_IN), D_MODEL ** -0.5)
    sgu_norm_gain = 1.0 + nrm(ks[17], (N_ODD, SGU_WIDTH), 0.02)
    sgu_w_spatial = nrm(ks[18], (N_ODD, SGU_GROUPS, SGU_CHUNK, SGU_CHUNK), SGU_CHUNK ** -0.5)
    sgu_b_spatial = 1.0 + nrm(ks[19], (N_ODD, SGU_GROUPS, SGU_CHUNK), 0.02)
    w_out_odd = nrm(ks[20], (N_ODD, SGU_WIDTH, D_MODEL), SGU_WIDTH ** -0.5)
    final_norm = 1.0 + nrm(ks[21], (D_MODEL,), 0.02)
    return {"x": x, "norm_even": norm_even, "w_in_even": w_in_even,
            "s5_lam_re": s5_lam_re, "s5_lam_im": s5_lam_im, "s5_log_dt": s5_log_dt,
            "s5_b_re": s5_b_re, "s5_b_im": s5_b_im, "s5_c_re": s5_c_re, "s5_c_im": s5_c_im,
            "s5_d": s5_d, "s5_w_glu": s5_w_glu, "s5_b_glu": s5_b_glu,
            "ret_gn_gain": ret_gn_gain, "w_out_even": w_out_even,
            "norm_odd": norm_odd, "w_in_odd": w_in_odd, "sgu_norm_gain": sgu_norm_gain,
            "sgu_w_spatial": sgu_w_spatial, "sgu_b_spatial": sgu_b_spatial,
            "w_out_odd": w_out_odd, "final_norm": final_norm}


def _fwd_reference(x, norm_even, w_in_even, s5_lam_re, s5_lam_im, s5_log_dt, s5_b_re, s5_b_im,
              s5_c_re, s5_c_im, s5_d, s5_w_glu, s5_b_glu, ret_gn_gain, w_out_even,
              norm_odd, w_in_odd, sgu_norm_gain, sgu_w_spatial, sgu_b_spatial, w_out_odd,
              final_norm):
    split_pts = [int(p) for p in np.cumsum(EVEN_SPLITS)[:-1]]
    for layer in range(DEPTH):
        i = layer // 2
        if layer % 2 == 0:
            h = rms_norm(x, norm_even[i])
            p = h @ w_in_even[i]
            a_u, a_z, q, k, v, b_z = jnp.split(p, split_pts, axis=-1)
            ya = s5_branch(a_u, s5_lam_re[i], s5_lam_im[i], s5_log_dt[i], s5_b_re[i], s5_b_im[i],
                           s5_c_re[i], s5_c_im[i], s5_d[i], s5_w_glu[i], s5_b_glu[i]) * jax.nn.silu(a_z)
            yb = retention_branch(q, k, v, ret_gn_gain[i]) * jax.nn.silu(b_z)
            x = x + jnp.concatenate([ya, yb], axis=-1) @ w_out_even[i]
        else:
            h = rms_norm(x, norm_odd[i])
            p = h @ w_in_odd[i]
            hz = jax.nn.gelu(p[..., :2 * SGU_WIDTH])
            z = p[..., 2 * SGU_WIDTH:]
            y = spatial_gating_branch(hz, sgu_norm_gain[i], sgu_w_spatial[i], sgu_b_spatial[i]) * jax.nn.silu(z)
            x = x + y @ w_out_odd[i]
    return rms_norm(x, final_norm)


import jax as _jax
import jax.numpy as _jnp

TWIN_FORMAT = 'train_step'
FWD_PARAMS = ['x', 'norm_even', 'w_in_even', 's5_lam_re', 's5_lam_im', 's5_log_dt', 's5_b_re', 's5_b_im', 's5_c_re', 's5_c_im', 's5_d', 's5_w_glu', 's5_b_glu', 'ret_gn_gain', 'w_out_even', 'norm_odd', 'w_in_odd', 'sgu_norm_gain', 'sgu_w_spatial', 'sgu_b_spatial', 'w_out_odd', 'final_norm']
TWIN_WEIGHTS = ['norm_even', 'w_in_even', 's5_lam_re', 's5_lam_im', 's5_log_dt', 's5_b_re', 's5_b_im', 's5_c_re', 's5_c_im', 's5_d', 's5_w_glu', 's5_b_glu', 'ret_gn_gain', 'w_out_even', 'norm_odd', 'w_in_odd', 'sgu_norm_gain', 'sgu_w_spatial', 'sgu_b_spatial', 'w_out_odd', 'final_norm']
TWIN_DIFF_INPUT = 'x'
TWIN_INPUTS = ['x', 'norm_even', 'w_in_even', 's5_lam_re', 's5_lam_im', 's5_log_dt', 's5_b_re', 's5_b_im', 's5_c_re', 's5_c_im', 's5_d', 's5_w_glu', 's5_b_glu', 'ret_gn_gain', 'w_out_even', 'norm_odd', 'w_in_odd', 'sgu_norm_gain', 'sgu_w_spatial', 'sgu_b_spatial', 'w_out_odd', 'final_norm', 'loss_target', 'm_norm_even', 'm_w_in_even', 'm_s5_lam_re', 'm_s5_lam_im', 'm_s5_log_dt', 'm_s5_b_re', 'm_s5_b_im', 'm_s5_c_re', 'm_s5_c_im', 'm_s5_d', 'm_s5_w_glu', 'm_s5_b_glu', 'm_ret_gn_gain', 'm_w_out_even', 'm_norm_odd', 'm_w_in_odd', 'm_sgu_norm_gain', 'm_sgu_w_spatial', 'm_sgu_b_spatial', 'm_w_out_odd', 'm_final_norm', 'v_norm_even', 'v_w_in_even', 'v_s5_lam_re', 'v_s5_lam_im', 'v_s5_log_dt', 'v_s5_b_re', 'v_s5_b_im', 'v_s5_c_re', 'v_s5_c_im', 'v_s5_d', 'v_s5_w_glu', 'v_s5_b_glu', 'v_ret_gn_gain', 'v_w_out_even', 'v_norm_odd', 'v_w_in_odd', 'v_sgu_norm_gain', 'v_sgu_w_spatial', 'v_sgu_b_spatial', 'v_w_out_odd', 'v_final_norm']
TWIN_OUTPUTS = ['loss', 'grad_x', 'grad_norm_even', 'grad_w_in_even', 'grad_s5_lam_re', 'grad_s5_lam_im', 'grad_s5_log_dt', 'grad_s5_b_re', 'grad_s5_b_im', 'grad_s5_c_re', 'grad_s5_c_im', 'grad_s5_d', 'grad_s5_w_glu', 'grad_s5_b_glu', 'grad_ret_gn_gain', 'grad_w_out_even', 'grad_norm_odd', 'grad_w_in_odd', 'grad_sgu_norm_gain', 'grad_sgu_w_spatial', 'grad_sgu_b_spatial', 'grad_w_out_odd', 'grad_final_norm', 'delta_norm_even', 'delta_w_in_even', 'delta_s5_lam_re', 'delta_s5_lam_im', 'delta_s5_log_dt', 'delta_s5_b_re', 'delta_s5_b_im', 'delta_s5_c_re', 'delta_s5_c_im', 'delta_s5_d', 'delta_s5_w_glu', 'delta_s5_b_glu', 'delta_ret_gn_gain', 'delta_w_out_even', 'delta_norm_odd', 'delta_w_in_odd', 'delta_sgu_norm_gain', 'delta_sgu_w_spatial', 'delta_sgu_b_spatial', 'delta_w_out_odd', 'delta_final_norm', 'new_m_norm_even', 'new_m_w_in_even', 'new_m_s5_lam_re', 'new_m_s5_lam_im', 'new_m_s5_log_dt', 'new_m_s5_b_re', 'new_m_s5_b_im', 'new_m_s5_c_re', 'new_m_s5_c_im', 'new_m_s5_d', 'new_m_s5_w_glu', 'new_m_s5_b_glu', 'new_m_ret_gn_gain', 'new_m_w_out_even', 'new_m_norm_odd', 'new_m_w_in_odd', 'new_m_sgu_norm_gain', 'new_m_sgu_w_spatial', 'new_m_sgu_b_spatial', 'new_m_w_out_odd', 'new_m_final_norm', 'new_v_norm_even', 'new_v_w_in_even', 'new_v_s5_lam_re', 'new_v_s5_lam_im', 'new_v_s5_log_dt', 'new_v_s5_b_re', 'new_v_s5_b_im', 'new_v_s5_c_re', 'new_v_s5_c_im', 'new_v_s5_d', 'new_v_s5_w_glu', 'new_v_s5_b_glu', 'new_v_ret_gn_gain', 'new_v_w_out_even', 'new_v_norm_odd', 'new_v_w_in_odd', 'new_v_sgu_norm_gain', 'new_v_sgu_w_spatial', 'new_v_sgu_b_spatial', 'new_v_w_out_odd', 'new_v_final_norm']
TWIN_LEAF_KINDS = {'loss': 'loss', 'grad_x': 'grad_x', 'grad_norm_even': 'grad_w', 'grad_w_in_even': 'grad_w', 'grad_s5_lam_re': 'grad_w', 'grad_s5_lam_im': 'grad_w', 'grad_s5_log_dt': 'grad_w', 'grad_s5_b_re': 'grad_w', 'grad_s5_b_im': 'grad_w', 'grad_s5_c_re': 'grad_w', 'grad_s5_c_im': 'grad_w', 'grad_s5_d': 'grad_w', 'grad_s5_w_glu': 'grad_w', 'grad_s5_b_glu': 'grad_w', 'grad_ret_gn_gain': 'grad_w', 'grad_w_out_even': 'grad_w', 'grad_norm_odd': 'grad_w', 'grad_w_in_odd': 'grad_w', 'grad_sgu_norm_gain': 'grad_w', 'grad_sgu_w_spatial': 'grad_w', 'grad_sgu_b_spatial': 'grad_w', 'grad_w_out_odd': 'grad_w', 'grad_final_norm': 'grad_w', 'delta_norm_even': 'delta_w', 'delta_w_in_even': 'delta_w', 'delta_s5_lam_re': 'delta_w', 'delta_s5_lam_im': 'delta_w', 'delta_s5_log_dt': 'delta_w', 'delta_s5_b_re': 'delta_w', 'delta_s5_b_im': 'delta_w', 'delta_s5_c_re': 'delta_w', 'delta_s5_c_im': 'delta_w', 'delta_s5_d': 'delta_w', 'delta_s5_w_glu': 'delta_w', 'delta_s5_b_glu': 'delta_w', 'delta_ret_gn_gain': 'delta_w', 'delta_w_out_even': 'delta_w', 'delta_norm_odd': 'delta_w', 'delta_w_in_odd': 'delta_w', 'delta_sgu_norm_gain': 'delta_w', 'delta_sgu_w_spatial': 'delta_w', 'delta_sgu_b_spatial': 'delta_w', 'delta_w_out_odd': 'delta_w', 'delta_final_norm': 'delta_w', 'new_m_norm_even': 'new_m', 'new_m_w_in_even': 'new_m', 'new_m_s5_lam_re': 'new_m', 'new_m_s5_lam_im': 'new_m', 'new_m_s5_log_dt': 'new_m', 'new_m_s5_b_re': 'new_m', 'new_m_s5_b_im': 'new_m', 'new_m_s5_c_re': 'new_m', 'new_m_s5_c_im': 'new_m', 'new_m_s5_d': 'new_m', 'new_m_s5_w_glu': 'new_m', 'new_m_s5_b_glu': 'new_m', 'new_m_ret_gn_gain': 'new_m', 'new_m_w_out_even': 'new_m', 'new_m_norm_odd': 'new_m', 'new_m_w_in_odd': 'new_m', 'new_m_sgu_norm_gain': 'new_m', 'new_m_sgu_w_spatial': 'new_m', 'new_m_sgu_b_spatial': 'new_m', 'new_m_w_out_odd': 'new_m', 'new_m_final_norm': 'new_m', 'new_v_norm_even': 'new_v', 'new_v_w_in_even': 'new_v', 'new_v_s5_lam_re': 'new_v', 'new_v_s5_lam_im': 'new_v', 'new_v_s5_log_dt': 'new_v', 'new_v_s5_b_re': 'new_v', 'new_v_s5_b_im': 'new_v', 'new_v_s5_c_re': 'new_v', 'new_v_s5_c_im': 'new_v', 'new_v_s5_d': 'new_v', 'new_v_s5_w_glu': 'new_v', 'new_v_s5_b_glu': 'new_v', 'new_v_ret_gn_gain': 'new_v', 'new_v_w_out_even': 'new_v', 'new_v_norm_odd': 'new_v', 'new_v_w_in_odd': 'new_v', 'new_v_sgu_norm_gain': 'new_v', 'new_v_sgu_w_spatial': 'new_v', 'new_v_sgu_b_spatial': 'new_v', 'new_v_w_out_odd': 'new_v', 'new_v_final_norm': 'new_v'}


def _forward(args):
    return _fwd_reference(*[args[k] for k in FWD_PARAMS])


def _output_shape():
    def fwd():
        inp = _fwd_setup_inputs(0)
        return _fwd_reference(*[inp[k] for k in FWD_PARAMS])
    out = _jax.eval_shape(fwd)
    return out.shape, out.dtype

N_MICROBATCH = 1
ADAM_LR = 0.001
ADAM_B1 = 0.9
ADAM_B2 = 0.999
ADAM_EPS = 1e-08
ADAM_WD = 0.01
ADAM_STEP = 10
PER_EXAMPLE_BATCH_AXIS = {'x': 0, 'loss_target': 0}
SHARED_INPUTS = []
_WEIGHT_DTYPES = {'norm_even': _jnp.float32, 'w_in_even': _jnp.float32, 's5_lam_re': _jnp.float32, 's5_lam_im': _jnp.float32, 's5_log_dt': _jnp.float32, 's5_b_re': _jnp.float32, 's5_b_im': _jnp.float32, 's5_c_re': _jnp.float32, 's5_c_im': _jnp.float32, 's5_d': _jnp.float32, 's5_w_glu': _jnp.float32, 's5_b_glu': _jnp.float32, 'ret_gn_gain': _jnp.float32, 'w_out_even': _jnp.float32, 'norm_odd': _jnp.float32, 'w_in_odd': _jnp.float32, 'sgu_norm_gain': _jnp.float32, 'sgu_w_spatial': _jnp.float32, 'sgu_b_spatial': _jnp.float32, 'w_out_odd': _jnp.float32, 'final_norm': _jnp.float32}
MOMENT_SCALE = {'norm_even': 2.259773e-01, 'w_in_even': 9.222764e-02, 's5_lam_re': 2.002740e-03, 's5_lam_im': 2.184232e-03, 's5_log_dt': 1.515308e+00, 's5_b_re': 1.380504e-03, 's5_b_im': 1.366363e-03, 's5_c_re': 2.773142e-03, 's5_c_im': 2.761168e-03, 's5_d': 4.150550e-02, 's5_w_glu': 1.197589e-02, 's5_b_glu': 2.134862e-02, 'ret_gn_gain': 1.100142e-01, 'w_out_even': 1.138304e-01, 'norm_odd': 1.609735e-01, 'w_in_odd': 6.524314e-02, 'sgu_norm_gain': 4.458601e-02, 'sgu_w_spatial': 8.107757e-02, 'sgu_b_spatial': 1.141138e-01, 'w_out_odd': 9.998618e-02, 'final_norm': 6.400533e+01}


def _to_microbatches(a, axis):
    t = _jnp.moveaxis(a, axis, 0)
    t = t.reshape((N_MICROBATCH, t.shape[0] // N_MICROBATCH) + t.shape[1:])
    return _jnp.moveaxis(t, 1, axis + 1)


def setup_inputs(seed: int = 0) -> dict:
    inp = _fwd_setup_inputs(seed)
    key = _jax.random.fold_in(_jax.random.key(seed), 7919)
    shape, _ = _output_shape()
    out = dict(inp)
    out["loss_target"] = _jax.random.normal(_jax.random.fold_in(key, 0), shape, _jnp.float32)
    for i, name in enumerate(TWIN_WEIGHTS):
        w = inp[name].astype(_jnp.float32)
        if MOMENT_SCALE is None:
            s = _jnp.sqrt(_jnp.mean(_jnp.square(w)) + 1e-30)
        else:
            s = MOMENT_SCALE[name]
        km, kv = _jax.random.split(_jax.random.fold_in(key, i + 1))
        out[name] = w
        out["m_" + name] = s * _jax.random.normal(km, w.shape, _jnp.float32)
        out["v_" + name] = (s * s) * _jax.random.uniform(kv, w.shape, _jnp.float32, 0.5, 1.5)
    if N_MICROBATCH > 1:
        for name, axis in PER_EXAMPLE_BATCH_AXIS.items():
            out[name] = _to_microbatches(out[name], axis)
    return {'x': out['x'], 'norm_even': out['norm_even'], 'w_in_even': out['w_in_even'], 's5_lam_re': out['s5_lam_re'], 's5_lam_im': out['s5_lam_im'], 's5_log_dt': out['s5_log_dt'], 's5_b_re': out['s5_b_re'], 's5_b_im': out['s5_b_im'], 's5_c_re': out['s5_c_re'], 's5_c_im': out['s5_c_im'], 's5_d': out['s5_d'], 's5_w_glu': out['s5_w_glu'], 's5_b_glu': out['s5_b_glu'], 'ret_gn_gain': out['ret_gn_gain'], 'w_out_even': out['w_out_even'], 'norm_odd': out['norm_odd'], 'w_in_odd': out['w_in_odd'], 'sgu_norm_gain': out['sgu_norm_gain'], 'sgu_w_spatial': out['sgu_w_spatial'], 'sgu_b_spatial': out['sgu_b_spatial'], 'w_out_odd': out['w_out_odd'], 'final_norm': out['final_norm'], 'loss_target': out['loss_target'], 'm_norm_even': out['m_norm_even'], 'm_w_in_even': out['m_w_in_even'], 'm_s5_lam_re': out['m_s5_lam_re'], 'm_s5_lam_im': out['m_s5_lam_im'], 'm_s5_log_dt': out['m_s5_log_dt'], 'm_s5_b_re': out['m_s5_b_re'], 'm_s5_b_im': out['m_s5_b_im'], 'm_s5_c_re': out['m_s5_c_re'], 'm_s5_c_im': out['m_s5_c_im'], 'm_s5_d': out['m_s5_d'], 'm_s5_w_glu': out['m_s5_w_glu'], 'm_s5_b_glu': out['m_s5_b_glu'], 'm_ret_gn_gain': out['m_ret_gn_gain'], 'm_w_out_even': out['m_w_out_even'], 'm_norm_odd': out['m_norm_odd'], 'm_w_in_odd': out['m_w_in_odd'], 'm_sgu_norm_gain': out['m_sgu_norm_gain'], 'm_sgu_w_spatial': out['m_sgu_w_spatial'], 'm_sgu_b_spatial': out['m_sgu_b_spatial'], 'm_w_out_odd': out['m_w_out_odd'], 'm_final_norm': out['m_final_norm'], 'v_norm_even': out['v_norm_even'], 'v_w_in_even': out['v_w_in_even'], 'v_s5_lam_re': out['v_s5_lam_re'], 'v_s5_lam_im': out['v_s5_lam_im'], 'v_s5_log_dt': out['v_s5_log_dt'], 'v_s5_b_re': out['v_s5_b_re'], 'v_s5_b_im': out['v_s5_b_im'], 'v_s5_c_re': out['v_s5_c_re'], 'v_s5_c_im': out['v_s5_c_im'], 'v_s5_d': out['v_s5_d'], 'v_s5_w_glu': out['v_s5_w_glu'], 'v_s5_b_glu': out['v_s5_b_glu'], 'v_ret_gn_gain': out['v_ret_gn_gain'], 'v_w_out_even': out['v_w_out_even'], 'v_norm_odd': out['v_norm_odd'], 'v_w_in_odd': out['v_w_in_odd'], 'v_sgu_norm_gain': out['v_sgu_norm_gain'], 'v_sgu_w_spatial': out['v_sgu_w_spatial'], 'v_sgu_b_spatial': out['v_sgu_b_spatial'], 'v_w_out_odd': out['v_w_out_odd'], 'v_final_norm': out['v_final_norm']}


def _loss(weights, diff, rest, loss_target):
    with _jax.named_scope("forward"):
        args = {**rest, TWIN_DIFF_INPUT: diff, **{k: w.astype(_WEIGHT_DTYPES[k]) for k, w in weights.items()}}
        y = _forward(args)
    with _jax.named_scope("loss_head"):
        err = _jnp.square(y.astype(_jnp.float32) - loss_target)
        return 0.5 * _jnp.sum(_jnp.mean(err, axis=-1)) if err.ndim else 0.5 * err


def _adamw(w, g, m, v):
    m = ADAM_B1 * m + (1.0 - ADAM_B1) * g
    v = ADAM_B2 * v + (1.0 - ADAM_B2) * _jnp.square(g)
    m_hat = m / (1.0 - ADAM_B1 ** ADAM_STEP)
    v_hat = v / (1.0 - ADAM_B2 ** ADAM_STEP)
    delta = -ADAM_LR * (m_hat / (_jnp.sqrt(v_hat) + ADAM_EPS) + ADAM_WD * w)
    return delta, m, v


def reference(x, norm_even, w_in_even, s5_lam_re, s5_lam_im, s5_log_dt, s5_b_re, s5_b_im, s5_c_re, s5_c_im, s5_d, s5_w_glu, s5_b_glu, ret_gn_gain, w_out_even, norm_odd, w_in_odd, sgu_norm_gain, sgu_w_spatial, sgu_b_spatial, w_out_odd, final_norm, loss_target, m_norm_even, m_w_in_even, m_s5_lam_re, m_s5_lam_im, m_s5_log_dt, m_s5_b_re, m_s5_b_im, m_s5_c_re, m_s5_c_im, m_s5_d, m_s5_w_glu, m_s5_b_glu, m_ret_gn_gain, m_w_out_even, m_norm_odd, m_w_in_odd, m_sgu_norm_gain, m_sgu_w_spatial, m_sgu_b_spatial, m_w_out_odd, m_final_norm, v_norm_even, v_w_in_even, v_s5_lam_re, v_s5_lam_im, v_s5_log_dt, v_s5_b_re, v_s5_b_im, v_s5_c_re, v_s5_c_im, v_s5_d, v_s5_w_glu, v_s5_b_glu, v_ret_gn_gain, v_w_out_even, v_norm_odd, v_w_in_odd, v_sgu_norm_gain, v_sgu_w_spatial, v_sgu_b_spatial, v_w_out_odd, v_final_norm):
    given = dict(x=x, norm_even=norm_even, w_in_even=w_in_even, s5_lam_re=s5_lam_re, s5_lam_im=s5_lam_im, s5_log_dt=s5_log_dt, s5_b_re=s5_b_re, s5_b_im=s5_b_im, s5_c_re=s5_c_re, s5_c_im=s5_c_im, s5_d=s5_d, s5_w_glu=s5_w_glu, s5_b_glu=s5_b_glu, ret_gn_gain=ret_gn_gain, w_out_even=w_out_even, norm_odd=norm_odd, w_in_odd=w_in_odd, sgu_norm_gain=sgu_norm_gain, sgu_w_spatial=sgu_w_spatial, sgu_b_spatial=sgu_b_spatial, w_out_odd=w_out_odd, final_norm=final_norm, loss_target=loss_target, m_norm_even=m_norm_even, m_w_in_even=m_w_in_even, m_s5_lam_re=m_s5_lam_re, m_s5_lam_im=m_s5_lam_im, m_s5_log_dt=m_s5_log_dt, m_s5_b_re=m_s5_b_re, m_s5_b_im=m_s5_b_im, m_s5_c_re=m_s5_c_re, m_s5_c_im=m_s5_c_im, m_s5_d=m_s5_d, m_s5_w_glu=m_s5_w_glu, m_s5_b_glu=m_s5_b_glu, m_ret_gn_gain=m_ret_gn_gain, m_w_out_even=m_w_out_even, m_norm_odd=m_norm_odd, m_w_in_odd=m_w_in_odd, m_sgu_norm_gain=m_sgu_norm_gain, m_sgu_w_spatial=m_sgu_w_spatial, m_sgu_b_spatial=m_sgu_b_spatial, m_w_out_odd=m_w_out_odd, m_final_norm=m_final_norm, v_norm_even=v_norm_even, v_w_in_even=v_w_in_even, v_s5_lam_re=v_s5_lam_re, v_s5_lam_im=v_s5_lam_im, v_s5_log_dt=v_s5_log_dt, v_s5_b_re=v_s5_b_re, v_s5_b_im=v_s5_b_im, v_s5_c_re=v_s5_c_re, v_s5_c_im=v_s5_c_im, v_s5_d=v_s5_d, v_s5_w_glu=v_s5_w_glu, v_s5_b_glu=v_s5_b_glu, v_ret_gn_gain=v_ret_gn_gain, v_w_out_even=v_w_out_even, v_norm_odd=v_norm_odd, v_w_in_odd=v_w_in_odd, v_sgu_norm_gain=v_sgu_norm_gain, v_sgu_w_spatial=v_sgu_w_spatial, v_sgu_b_spatial=v_sgu_b_spatial, v_w_out_odd=v_w_out_odd, v_final_norm=v_final_norm)
    weights = {n: given[n] for n in TWIN_WEIGHTS}
    shared = {n: given[n] for n in SHARED_INPUTS}
    per_example = {n: given[n] for n in ['x']}
    grad_fn = _jax.value_and_grad(_loss, argnums=(0, 1))

    def one_microbatch(ex, loss_target):
        ex = dict(ex)
        diff = ex.pop(TWIN_DIFF_INPUT)
        return grad_fn(weights, diff, {**shared, **ex}, loss_target)

    if N_MICROBATCH == 1:
        loss, (grad_w, grad_x) = one_microbatch(per_example, given["loss_target"])
    else:
        def body(carry, xs):
            loss_sum, grad_sum = carry
            l_k, (gw_k, gx_k) = one_microbatch(xs[0], xs[1])
            with _jax.named_scope("update"):
                return (loss_sum + l_k, _jax.tree.map(_jnp.add, grad_sum, gw_k)), gx_k

        init = (_jnp.zeros((), _jnp.float32), _jax.tree.map(_jnp.zeros_like, weights))
        (loss, grad_w), grad_x = _jax.lax.scan(body, init, (per_example, given["loss_target"]))
    with _jax.named_scope("update"):
        delta_w, new_m, new_v = {}, {}, {}
        for n in TWIN_WEIGHTS:
            delta_w[n], new_m[n], new_v[n] = _adamw(weights[n], grad_w[n], given["m_" + n], given["v_" + n])
    return (loss, grad_x, *[grad_w[n] for n in TWIN_WEIGHTS], *[delta_w[n] for n in TWIN_WEIGHTS],
            *[new_m[n] for n in TWIN_WEIGHTS], *[new_v[n] for n in TWIN_WEIGHTS])
```

```python
import functools
import math

import numpy as np
import jax
import jax.numpy as jnp
from jax import lax
from jax.experimental import pallas as pl
from jax.experimental.pallas import tpu as pltpu

F32 = jnp.float32
MXU_DTYPE = jnp.bfloat16
NORM_EPS = 1e-6
D_MODEL = 1024
S5_WIDTH = 1024
S5_GROUP = 16
S5_GROUPS = 64
S5_STATE = 64
S5_LANES = S5_GROUPS * S5_STATE
S5_KBLK = 8
RET_HEADS = 4
RET_DK = 256
RET_CHUNK = 128
ROPE_BASE = 10000.0
SGU_WIDTH = 2048
SGU_GROUPS = 4
SGU_GDIM = 512
SGU_CHUNK = 128
EVEN_IN = 6144
ODD_IN = 6144
ADAM_LR = 0.001
ADAM_B1 = 0.9
ADAM_B2 = 0.999
ADAM_EPS = 1e-08
ADAM_WD = 0.01
ADAM_STEP = 10
N_CHIPS = 4
VMEM_LIMIT = 56 * 1024 * 1024

TL_PROJ = 512
TL_S5 = 128
TL_SGU = 128


def _cparams(sem, **kw):
    return pltpu.CompilerParams(dimension_semantics=sem, vmem_limit_bytes=VMEM_LIMIT, **kw)


def _mm(a, b):
    return jnp.dot(a.astype(MXU_DTYPE), b.astype(MXU_DTYPE), preferred_element_type=F32)


def _mm_nt(a, b):
    return lax.dot_general(a.astype(MXU_DTYPE), b.astype(MXU_DTYPE),
                           (((1,), (1,)), ((), ())), preferred_element_type=F32)


def _mm_tn(a, b):
    return lax.dot_general(a.astype(MXU_DTYPE), b.astype(MXU_DTYPE),
                           (((0,), (0,)), ((), ())), preferred_element_type=F32)


_GELU_C = math.sqrt(2.0 / math.pi)


def _gelu(x):
    return 0.5 * x * (1.0 + jnp.tanh(_GELU_C * (x + 0.044715 * x * x * x)))


def _gelu_grad(x):
    th = jnp.tanh(_GELU_C * (x + 0.044715 * x * x * x))
    return 0.5 * (1.0 + th) + 0.5 * x * (1.0 - th * th) * _GELU_C * (1.0 + 3.0 * 0.044715 * x * x)


def _sigmoid(x):
    return 1.0 / (1.0 + jnp.exp(-x))


def _silu_and_grad(x):
    s = _sigmoid(x)
    return x * s, s * (1.0 + x * (1.0 - s))


def _rms(x):
    return lax.rsqrt(jnp.mean(x * x, axis=-1, keepdims=True) + NORM_EPS)


def _full(shape):
    nd = len(shape)
    return pl.BlockSpec(shape, lambda *_: (0,) * nd)


def norm_matmul(x, g, w, name):
    L, D = x.shape
    nb, _, tn = w.shape
    tl = min(TL_PROJ, L)

    def body(x_ref, g_ref, w_ref, o_ref):
        xv = x_ref[...]
        o_ref[...] = _mm(xv * _rms(xv) * g_ref[...], w_ref[0])

    return pl.pallas_call(
        body, name=name, grid=(nb, L // tl),
        in_specs=[pl.BlockSpec((tl, D), lambda n, i: (i, 0)), _full((1, D)),
                  pl.BlockSpec((1, D, tn), lambda n, i: (n, 0, 0))],
        out_specs=pl.BlockSpec((tl, tn), lambda n, i: (i, n)),
        out_shape=jax.ShapeDtypeStruct((L, nb * tn), F32),
        compiler_params=_cparams(("arbitrary", "arbitrary")),
    )(x, g, w)


def matmul_residual(y, w, x, name):
    L, K = y.shape
    D = w.shape[1]
    tl = min(TL_PROJ, L)

    def body(y_ref, w_ref, x_ref, o_ref):
        o_ref[...] = x_ref[...] + _mm(y_ref[...], w_ref[...])

    return pl.pallas_call(
        body, name=name, grid=(L // tl,),
        in_specs=[pl.BlockSpec((tl, K), lambda i: (i, 0)), _full((K, D)),
                  pl.BlockSpec((tl, D), lambda i: (i, 0))],
        out_specs=pl.BlockSpec((tl, D), lambda i: (i, 0)),
        out_shape=jax.ShapeDtypeStruct((L, D), F32),
        compiler_params=_cparams(("arbitrary",)),
    )(y, w, x)


def out_proj_loss(y, w, x, gf, tgt, name):
    L, K = y.shape
    D = w.shape[1]
    tl = min(TL_PROJ, L)

    def body(y_ref, w_ref, x_ref, gf_ref, t_ref, dx_ref, loss_ref, dg_ref):
        @pl.when(pl.program_id(0) == 0)
        def _():
            loss_ref[...] = jnp.zeros_like(loss_ref)
            dg_ref[...] = jnp.zeros_like(dg_ref)

        x2 = x_ref[...] + _mm(y_ref[...], w_ref[...])
        r = _rms(x2)
        xn = x2 * r
        e = xn * gf_ref[...] - t_ref[...]
        loss_ref[...] += (0.5 / D) * jnp.sum(e * e)
        dout = e * (1.0 / D)
        dg_ref[...] += jnp.sum(dout * xn, axis=0, keepdims=True)
        dxn = dout * gf_ref[...]
        dx_ref[...] = r * (dxn - xn * jnp.mean(dxn * xn, axis=-1, keepdims=True))

    return pl.pallas_call(
        body, name=name, grid=(L // tl,),
        in_specs=[pl.BlockSpec((tl, K), lambda i: (i, 0)), _full((K, D)),
                  pl.BlockSpec((tl, D), lambda i: (i, 0)), _full((1, D)),
                  pl.BlockSpec((tl, D), lambda i: (i, 0))],
        out_specs=[pl.BlockSpec((tl, D), lambda i: (i, 0)), _full((8, 128)), _full((1, D))],
        out_shape=[jax.ShapeDtypeStruct((L, D), F32), jax.ShapeDtypeStruct((8, 128), F32),
                   jax.ShapeDtypeStruct((1, D), F32)],
        compiler_params=_cparams(("arbitrary",)),
    )(y, w, x, gf, tgt)


def out_proj_bwd(dx, w, y, name):
    L, D = dx.shape
    K = w.shape[0]
    tl = min(TL_PROJ, L)

    def body(dx_ref, w_ref, y_ref, dy_ref, dw_ref):
        @pl.when(pl.program_id(0) == 0)
        def _():
            dw_ref[...] = jnp.zeros_like(dw_ref)

        dxv = dx_ref[...]
        dy_ref[...] = _mm_nt(dxv, w_ref[...])
        dw_ref[...] += _mm_tn(y_ref[...], dxv)

    return pl.pallas_call(
        body, name=name, grid=(L // tl,),
        in_specs=[pl.BlockSpec((tl, D), lambda i: (i, 0)), _full((K, D)),
                  pl.BlockSpec((tl, K), lambda i: (i, 0))],
        out_specs=[pl.BlockSpec((tl, K), lambda i: (i, 0)), _full((K, D))],
        out_shape=[jax.ShapeDtypeStruct((L, K), F32), jax.ShapeDtypeStruct((K, D), F32)],
        compiler_params=_cparams(("arbitrary",)),
    )(dx, w, y)


def in_proj_bwd_dx(x, g, dp, w, dres, name):
    L, D = x.shape
    nb, _, tn = w.shape
    tl = min(TL_PROJ, L)

    def body(x_ref, g_ref, dres_ref, dp_ref, w_ref, dx_ref, dg_ref):
        @pl.when(pl.program_id(0) == 0)
        def _():
            dg_ref[...] = jnp.zeros_like(dg_ref)

        dh = _mm_nt(dp_ref[:, :tn], w_ref[0])
        for k in range(1, nb):
            dh = dh + _mm_nt(dp_ref[:, tn * k:tn * (k + 1)], w_ref[k])
        xv = x_ref[...]
        r = _rms(xv)
        xn = xv * r
        dg_ref[...] += jnp.sum(dh * xn, axis=0, keepdims=True)
        dxn = dh * g_ref[...]
        dx_ref[...] = dres_ref[...] + r * (dxn - xn * jnp.mean(dxn * xn, axis=-1, keepdims=True))

    return pl.pallas_call(
        body, name=name, grid=(L // tl,),
        in_specs=[pl.BlockSpec((tl, D), lambda i: (i, 0)), _full((1, D)), pl.BlockSpec((tl, D), lambda i: (i, 0)),
                  pl.BlockSpec((tl, nb * tn), lambda i: (i, 0)), _full(w.shape)],
        out_specs=[pl.BlockSpec((tl, D), lambda i: (i, 0)), _full((1, D))],
        out_shape=[jax.ShapeDtypeStruct((L, D), F32), jax.ShapeDtypeStruct((1, D), F32)],
        compiler_params=_cparams(("arbitrary",)),
    )(x, g, dres, dp, w)


def in_proj_bwd_dw(x, g, dp, name):
    L, D = x.shape
    tn = dp.shape[1] // N_CHIPS
    tl = min(TL_PROJ, L)

    def body(x_ref, g_ref, dp_ref, dw_ref):
        @pl.when(pl.program_id(1) == 0)
        def _():
            dw_ref[...] = jnp.zeros_like(dw_ref)

        xv = x_ref[...]
        dw_ref[0] += _mm_tn(xv * _rms(xv) * g_ref[...], dp_ref[...])

    return pl.pallas_call(
        body, name=name, grid=(N_CHIPS, L // tl),
        in_specs=[pl.BlockSpec((tl, D), lambda n, i: (i, 0)), _full((1, D)),
                  pl.BlockSpec((tl, tn), lambda n, i: (i, n))],
        out_specs=pl.BlockSpec((1, D, tn), lambda n, i: (n, 0, 0)),
        out_shape=jax.ShapeDtypeStruct((N_CHIPS, D, tn), F32),
        compiler_params=_cparams(("arbitrary", "arbitrary")),
    )(x, g, dp)


def _s5_param_fn(lam_re, lam_im, log_dt, b_re, b_im):
    lr = jnp.minimum(lam_re, -1e-4)
    li = lam_im
    dt = jnp.exp(log_dt)
    mag = jnp.exp(lr * dt)
    ab_re = mag * jnp.cos(li * dt)
    ab_im = mag * jnp.sin(li * dt)
    den = lr * lr + li * li
    n_re = ab_re - 1.0
    n_im = ab_im
    z_re = (n_re * lr + n_im * li) / den
    z_im = (n_im * lr - n_re * li) / den
    bb_re = z_re[None] * b_re - z_im[None] * b_im
    bb_im = z_re[None] * b_im + z_im[None] * b_re
    return ab_re, ab_im, bb_re, bb_im


def s5_params_fwd(lam_re, lam_im, log_dt, b_re, b_im):
    G, P = lam_re.shape
    H = b_re.shape[0]

    def body(lr_ref, li_ref, dt_ref, br_ref, bi_ref, abr_ref, abi_ref, bbr_ref, bbi_ref, pr_ref, pi_ref):
        ab_re, ab_im, bb_re, bb_im = _s5_param_fn(lr_ref[...], li_ref[...], dt_ref[...], br_ref[...], bi_ref[...])
        abr_ref[...] = ab_re
        abi_ref[...] = ab_im
        bbr_ref[...] = bb_re
        bbi_ref[...] = bb_im
        cr, ci = ab_re, ab_im
        pr_ref[0] = cr
        pi_ref[0] = ci
        for k in range(1, 8):
            cr, ci = cr * ab_re - ci * ab_im, cr * ab_im + ci * ab_re
            pr_ref[k] = cr
            pi_ref[k] = ci

    shp = lambda *s: jax.ShapeDtypeStruct(s, F32)
    return pl.pallas_call(
        body, name="s5_params_fwd",
        out_shape=[shp(G, P), shp(G, P), shp(H, G, P), shp(H, G, P), shp(8, G, P), shp(8, G, P)],
    )(lam_re, lam_im, log_dt, b_re, b_im)


def s5_params_bwd(lam_re, lam_im, log_dt, b_re, b_im, d_ab_re, d_ab_im, d_bb_re, d_bb_im):
    G, P = lam_re.shape
    H = b_re.shape[0]

    def body(lr_ref, li_ref, dt_ref, br_ref, bi_ref, g0, g1, g2, g3, o0, o1, o2, o3, o4):
        prim = (lr_ref[...], li_ref[...], dt_ref[...], br_ref[...], bi_ref[...])
        _, vjp = jax.vjp(_s5_param_fn, *prim)
        d = vjp((jnp.sum(g0[...], axis=0), jnp.sum(g1[...], axis=0), g2[...], g3[...]))
        o0[...], o1[...], o2[...], o3[...], o4[...] = d

    shp = lambda *s: jax.ShapeDtypeStruct(s, F32)
    return pl.pallas_call(
        body, name="s5_params_bwd",
        out_shape=[shp(G, P), shp(G, P), shp(G, 1), shp(H, G, P), shp(H, G, P)],
    )(lam_re, lam_im, log_dt, b_re, b_im, d_ab_re, d_ab_im, d_bb_re, d_bb_im)


def _s5_scan_tables(pw_re, pw_im):
    t = np.arange(8)[:, None]
    f_re, f_im, b_re, b_im = [pw_re], [pw_im], [pw_re[::-1]], [pw_im[::-1]]
    for k in (1, 2, 4):
        mf = jnp.asarray((t >= k).astype(np.float32))
        mb = jnp.asarray((t <= 7 - k).astype(np.float32))
        f_re.append(mf * pw_re[k - 1][None])
        f_im.append(mf * pw_im[k - 1][None])
        b_re.append(mb * pw_re[k - 1][None])
        b_im.append(mb * pw_im[k - 1][None])
    return jnp.stack(f_re), jnp.stack(f_im), jnp.stack(b_re), jnp.stack(b_im)


def _scan_fwd_tile(br, bi, tf_re, tf_im, lanes, cr, ci):
    for j, k in enumerate((1, 2, 4)):
        pr, pi = tf_re[j + 1, :, lanes], tf_im[j + 1, :, lanes]
        sr, si = pltpu.roll(br, k, 0), pltpu.roll(bi, k, 0)
        br, bi = br + (pr * sr - pi * si), bi + (pr * si + pi * sr)
    ar, ai = tf_re[0, :, lanes], tf_im[0, :, lanes]
    br, bi = br + (ar * cr - ai * ci), bi + (ar * ci + ai * cr)
    return br, bi


def _scan_bwd_tile(gr, gi, tb_re, tb_im, lanes, cr, ci):
    for j, k in enumerate((1, 2, 4)):
        pr, pi = tb_re[j + 1, :, lanes], tb_im[j + 1, :, lanes]
        sr, si = pltpu.roll(gr, 8 - k, 0), pltpu.roll(gi, 8 - k, 0)
        gr, gi = gr + (pr * sr + pi * si), gi + (pr * si - pi * sr)
    ar, ai = tb_re[0, :, lanes], tb_im[0, :, lanes]
    gr, gi = gr + (ar * cr + ai * ci), gi + (ar * ci - ai * cr)
    return gr, gi


_LANE_BLK = 512


def _s5_forward_chunk(u, wbd_ref, tf_re, tf_im, s_re, s_im, c0_re, c0_im, tl):
    for k in range(S5_KBLK):
        bu = _mm(u[:, 128 * k:128 * (k + 1)], wbd_ref[k])
        s_re[pl.ds(8, tl), 512 * k:512 * (k + 1)] = bu[:, :512]
        s_im[pl.ds(8, tl), 512 * k:512 * (k + 1)] = bu[:, 512:]
    s_re[pl.ds(0, 8), :] = jnp.broadcast_to(c0_re, (8, S5_LANES))
    s_im[pl.ds(0, 8), :] = jnp.broadcast_to(c0_im, (8, S5_LANES))
    outs_re, outs_im = [], []
    for b in range(S5_LANES // _LANE_BLK):
        lanes = slice(_LANE_BLK * b, _LANE_BLK * (b + 1))

        def tile(i, carry, lanes=lanes):
            r = pl.multiple_of(8 + i * 8, 8)
            br, bi = _scan_fwd_tile(s_re[pl.ds(r, 8), lanes], s_im[pl.ds(r, 8), lanes],
                                    tf_re, tf_im, lanes, carry[0], carry[1])
            s_re[pl.ds(r, 8), lanes] = br
            s_im[pl.ds(r, 8), lanes] = bi
            return br[7:8], bi[7:8]

        cr, ci = lax.fori_loop(0, tl // 8, tile, (c0_re[:, lanes], c0_im[:, lanes]))
        outs_re.append(cr)
        outs_im.append(ci)
    return jnp.concatenate(outs_re, axis=1), jnp.concatenate(outs_im, axis=1)


def _s5_readout(s_re, s_im, cre_ref, cim_ref, tl):
    ys = []
    for k in range(S5_KBLK):
        lanes = slice(512 * k, 512 * (k + 1))
        ys.append(_mm(s_re[pl.ds(8, tl), lanes], cre_ref[k]) - _mm(s_im[pl.ds(8, tl), lanes], cim_ref[k]))
    return jnp.concatenate(ys, axis=1)


def s5_forward(p, wbd, cre, cim, tf_re, tf_im, d_skip, w_glu, b_glu):
    L = p.shape[0]
    tl = min(TL_S5, L)
    nch = L // tl

    def body(u_ref, z_ref, wbd_ref, cre_ref, cim_ref, tfr_ref, tfi_ref, d_ref, wg_ref, bg_ref,
             ya_ref, st_re_ref, st_im_ref, s_re, s_im, car_re, car_im):
        @pl.when(pl.program_id(0) == 0)
        def _():
            car_re[...] = jnp.zeros_like(car_re)
            car_im[...] = jnp.zeros_like(car_im)

        c0_re, c0_im = car_re[...], car_im[...]
        st_re_ref[0] = c0_re
        st_im_ref[0] = c0_im
        u = u_ref[...]
        e_re, e_im = _s5_forward_chunk(u, wbd_ref, tfr_ref, tfi_ref, s_re, s_im, c0_re, c0_im, tl)
        car_re[...] = e_re
        car_im[...] = e_im
        y = _s5_readout(s_re, s_im, cre_ref, cim_ref, tl) + d_ref[...] * u
        yg = _gelu(y)
        gate = _sigmoid(_mm(yg, wg_ref[...]) + bg_ref[...])
        sz, _ = _silu_and_grad(z_ref[...])
        ya_ref[...] = (yg * gate * sz).astype(ya_ref.dtype)

    return pl.pallas_call(
        body, name="s5_forward", grid=(nch,),
        in_specs=[pl.BlockSpec((tl, 1024), lambda i: (i, 0)), pl.BlockSpec((tl, 1024), lambda i: (i, 1)),
                  _full(wbd.shape), _full(cre.shape), _full(cim.shape), _full(tf_re.shape), _full(tf_im.shape),
                  _full((1, 1024)), _full((1024, 1024)), _full((1, 1024))],
        out_specs=[pl.BlockSpec((tl, 1024), lambda i: (i, 0)),
                   pl.BlockSpec((1, 1, S5_LANES), lambda i: (i, 0, 0)),
                   pl.BlockSpec((1, 1, S5_LANES), lambda i: (i, 0, 0))],
        out_shape=[jax.ShapeDtypeStruct((L, 1024), MXU_DTYPE),
                   jax.ShapeDtypeStruct((nch, 1, S5_LANES), F32), jax.ShapeDtypeStruct((nch, 1, S5_LANES), F32)],
        scratch_shapes=[pltpu.VMEM((tl + 8, S5_LANES), F32), pltpu.VMEM((tl + 8, S5_LANES), F32),
                        pltpu.VMEM((1, S5_LANES), F32), pltpu.VMEM((1, S5_LANES), F32)],
        compiler_params=_cparams(("arbitrary",)),
    )(p, p, wbd, cre, cim, tf_re, tf_im, d_skip, w_glu, b_glu)


def s5_backward(p, dy, st_re, st_im, wbd, cre, cim, tf_re, tf_im, tb_re, tb_im, d_skip, w_glu, b_glu):
    L = p.shape[0]
    tl = min(TL_S5, L)
    nch = L // tl
    rev = lambda i: (nch - 1 - i, 0)
    rev1 = lambda i: (nch - 1 - i, 1)
    rev3 = lambda i: (nch - 1 - i, 0, 0)

    def body(u_ref, z_ref, dya_ref, str_ref, sti_ref, wbd_ref, cre_ref, cim_ref, tfr_ref, tfi_ref, tbr_ref, tbi_ref,
             d_ref, wg_ref, bg_ref,
             dp_ref, dwbd_ref, dcre_ref, dcim_ref, dabr_ref, dabi_ref, dd_ref, dwg_ref, dbg_ref,
             s_re, s_im, g_re, g_im, car_re, car_im):
        @pl.when(pl.program_id(0) == 0)
        def _():
            car_re[...] = jnp.zeros_like(car_re)
            car_im[...] = jnp.zeros_like(car_im)
            for r in (dwbd_ref, dcre_ref, dcim_ref, dabr_ref, dabi_ref, dd_ref, dwg_ref, dbg_ref):
                r[...] = jnp.zeros_like(r)

        u = u_ref[...]
        _s5_forward_chunk(u, wbd_ref, tfr_ref, tfi_ref, s_re, s_im, str_ref[0], sti_ref[0], tl)
        y = _s5_readout(s_re, s_im, cre_ref, cim_ref, tl) + d_ref[...] * u
        yg = _gelu(y)
        gate = _sigmoid(_mm(yg, wg_ref[...]) + bg_ref[...])
        sz, dsz = _silu_and_grad(z_ref[...])
        dya = dya_ref[...]
        s5out = yg * gate
        dp_ref[:, 1024:] = (dya * s5out * dsz).astype(dp_ref.dtype)
        ds5 = dya * sz
        dt = ds5 * yg * gate * (1.0 - gate)
        dwg_ref[...] += _mm_tn(yg, dt)
        dbg_ref[...] += jnp.sum(dt, axis=0, keepdims=True)
        dyv = (ds5 * gate + _mm_nt(dt, wg_ref[...])) * _gelu_grad(y)
        dd_ref[...] += jnp.sum(dyv * u, axis=0, keepdims=True)

        for k in range(S5_KBLK):
            lanes = slice(512 * k, 512 * (k + 1))
            dyk = dyv[:, 128 * k:128 * (k + 1)]
            g_re[:, lanes] = _mm_nt(dyk, cre_ref[k])
            g_im[:, lanes] = -_mm_nt(dyk, cim_ref[k])
            dcre_ref[k] += _mm_tn(s_re[pl.ds(8, tl), lanes], dyk)
            dcim_ref[k] -= _mm_tn(s_im[pl.ds(8, tl), lanes], dyk)

        row0 = lax.broadcasted_iota(jnp.int32, (8, _LANE_BLK), 0) == 0
        for b in range(S5_LANES // _LANE_BLK):
            lanes = slice(_LANE_BLK * b, _LANE_BLK * (b + 1))

            def tile(j, carry, lanes=lanes):
                cr, ci, ar, ai = carry
                i = tl // 8 - 1 - j
                r = pl.multiple_of(i * 8, 8)
                gr, gi = _scan_bwd_tile(g_re[pl.ds(r, 8), lanes], g_im[pl.ds(r, 8), lanes],
                                        tbr_ref, tbi_ref, lanes, cr, ci)
                g_re[pl.ds(r, 8), lanes] = gr
                g_im[pl.ds(r, 8), lanes] = gi
                pr, pi = s_re[pl.ds(r, 8), lanes], s_im[pl.ds(r, 8), lanes]
                qr, qi = s_re[pl.ds(r + 8, 8), lanes], s_im[pl.ds(r + 8, 8), lanes]
                sr = jnp.where(row0, jnp.broadcast_to(pr[7:8], qr.shape), pltpu.roll(qr, 1, 0))
                si = jnp.where(row0, jnp.broadcast_to(pi[7:8], qi.shape), pltpu.roll(qi, 1, 0))
                return gr[0:1], gi[0:1], ar + (sr * gr + si * gi), ai + (sr * gi - si * gr)

            z8 = jnp.zeros((8, _LANE_BLK), F32)
            cr, ci, ar, ai = lax.fori_loop(0, tl // 8, tile, (car_re[:, lanes], car_im[:, lanes], z8, z8))
            car_re[:, lanes] = cr
            car_im[:, lanes] = ci
            dabr_ref[:, lanes] += ar
            dabi_ref[:, lanes] += ai

        dus = []
        for k in range(S5_KBLK):
            lanes = slice(512 * k, 512 * (k + 1))
            g = jnp.concatenate([g_re[:, lanes], g_im[:, lanes]], axis=1)
            dwbd_ref[k] += _mm_tn(u[:, 128 * k:128 * (k + 1)], g)
            dus.append(_mm_nt(g, wbd_ref[k]))
        du = jnp.concatenate(dus, axis=1) + dyv * d_ref[...]
        dp_ref[:, :1024] = du.astype(dp_ref.dtype)

    shp = lambda *s: jax.ShapeDtypeStruct(s, F32)
    return pl.pallas_call(
        body, name="s5_backward", grid=(nch,),
        in_specs=[pl.BlockSpec((tl, 1024), rev), pl.BlockSpec((tl, 1024), rev1), pl.BlockSpec((tl, 1024), rev),
                  pl.BlockSpec((1, 1, S5_LANES), rev3), pl.BlockSpec((1, 1, S5_LANES), rev3),
                  _full(wbd.shape), _full(cre.shape), _full(cim.shape), _full(tf_re.shape), _full(tf_im.shape),
                  _full(tb_re.shape), _full(tb_im.shape), _full((1, 1024)), _full((1024, 1024)), _full((1, 1024))],
        out_specs=[pl.BlockSpec((tl, 2048), rev), _full(wbd.shape), _full(cre.shape), _full(cim.shape),
                   _full((8, S5_LANES)), _full((8, S5_LANES)), _full((1, 1024)), _full((1024, 1024)), _full((1, 1024))],
        out_shape=[jax.ShapeDtypeStruct((L, 2048), MXU_DTYPE), shp(*wbd.shape), shp(*cre.shape), shp(*cim.shape),
                   shp(8, S5_LANES), shp(8, S5_LANES), shp(1, 1024), shp(1024, 1024), shp(1, 1024)],
        scratch_shapes=[pltpu.VMEM((tl + 8, S5_LANES), F32), pltpu.VMEM((tl + 8, S5_LANES), F32),
                        pltpu.VMEM((tl, S5_LANES), F32), pltpu.VMEM((tl, S5_LANES), F32),
                        pltpu.VMEM((1, S5_LANES), F32), pltpu.VMEM((1, S5_LANES), F32)],
        compiler_params=_cparams(("arbitrary",)),
    )(p, p, dy, st_re, st_im, wbd, cre, cim, tf_re, tf_im, tb_re, tb_im, d_skip, w_glu, b_glu)


def _block_diag(w, rows_first):
    g8 = w.reshape(S5_KBLK, 8, w.shape[1], w.shape[2])
    eye = jnp.eye(8, dtype=w.dtype)
    out = jnp.einsum('kgab,fg->kfagb', g8, eye)
    return out.reshape(S5_KBLK, 8 * w.shape[1], 8 * w.shape[2])


def _block_diag_extract(wbd, a, b):
    w5 = wbd.reshape(S5_KBLK, 8, a, 8, b)
    idx = jnp.arange(8)
    return w5[:, idx, :, idx, :].transpose(1, 0, 2, 3).reshape(S5_GROUPS, a, b)


def _ret_constants():
    log_g = np.log1p(-np.exp2(-5.0 - np.arange(RET_HEADS, dtype=np.float32))).astype(np.float32)
    idx = np.arange(RET_CHUNK, dtype=np.float32)
    diff = idx[:, None] - idx[None, :]
    decay = np.where(diff >= 0, np.exp(log_g[:, None, None] * np.maximum(diff, 0.0)), 0.0).astype(np.float32)
    xi = np.exp(log_g[None, :] * (idx[:, None] + 1.0)).astype(np.float32)
    zeta = np.exp(log_g[None, :] * (RET_CHUNK - 1.0 - idx[:, None])).astype(np.float32)
    chunk_decay = np.exp(log_g * RET_CHUNK).astype(np.float32)
    return decay, xi, zeta, chunk_decay


def _rope_tables(L):
    half = RET_DK // 2
    inv = ROPE_BASE ** (-jnp.arange(half, dtype=F32) / half)
    ang = jnp.arange(L, dtype=F32)[:, None] * inv[None, :]
    return jnp.cos(ang), jnp.sin(ang)


def _rot(xh, cos, sin):
    x1, x2 = xh[:, :128], xh[:, 128:]
    return jnp.concatenate([x1 * cos - x2 * sin, x1 * sin + x2 * cos], axis=1)


def _rot_t(dh, cos, sin):
    d1, d2 = dh[:, :128], dh[:, 128:]
    return jnp.concatenate([d1 * cos + d2 * sin, d2 * cos - d1 * sin], axis=1)


def retention_forward(p, cos, sin, gain):
    L = p.shape[0]
    nc = L // RET_CHUNK
    decay_np, xi_np, zeta_np, cd_np = _ret_constants()
    decay, xi, zeta = jnp.asarray(decay_np), jnp.asarray(xi_np), jnp.asarray(zeta_np)
    scale = RET_DK ** -0.5

    def body(q_ref, k_ref, v_ref, z_ref, cos_ref, sin_ref, dec_ref, xi_ref, zeta_ref, gain_ref,
             yb_ref, prev_ref, state):
        @pl.when(pl.program_id(0) == 0)
        def _():
            state[...] = jnp.zeros_like(state)

        cs, sn = cos_ref[...], sin_ref[...]
        sz, _ = _silu_and_grad(z_ref[...])
        for h in range(RET_HEADS):
            hs = slice(RET_DK * h, RET_DK * (h + 1))
            qh = _rot(q_ref[:, hs], cs, sn)
            kh = _rot(k_ref[:, hs], cs, sn) * scale
            vh = v_ref[:, hs]
            prev = state[h]
            prev_ref[0, h] = prev.astype(prev_ref.dtype)
            sc = _mm_nt(qh, kh) * dec_ref[h]
            o = _mm(sc, vh) + _mm(qh * xi_ref[:, h:h + 1], prev)
            state[h] = prev * float(cd_np[h]) + _mm_tn(kh * zeta_ref[:, h:h + 1], vh)
            mu = jnp.mean(o, axis=-1, keepdims=True)
            oc = o - mu
            on = oc * lax.rsqrt(jnp.mean(oc * oc, axis=-1, keepdims=True) + NORM_EPS)
            yb_ref[:, hs] = (on * gain_ref[:, hs] * sz[:, hs]).astype(yb_ref.dtype)

    blk = lambda c: pl.BlockSpec((RET_CHUNK, 1024), lambda i, c=c: (i, c))
    return pl.pallas_call(
        body, name="retention_forward", grid=(nc,),
        in_specs=[blk(2), blk(3), blk(4), blk(5),
                  pl.BlockSpec((RET_CHUNK, 128), lambda i: (i, 0)), pl.BlockSpec((RET_CHUNK, 128), lambda i: (i, 0)),
                  _full(decay.shape), _full(xi.shape), _full(zeta.shape), _full((1, 1024))],
        out_specs=[pl.BlockSpec((RET_CHUNK, 1024), lambda i: (i, 0)),
                   pl.BlockSpec((1, RET_HEADS, RET_DK, RET_DK), lambda i: (i, 0, 0, 0))],
        out_shape=[jax.ShapeDtypeStruct((L, 1024), MXU_DTYPE),
                   jax.ShapeDtypeStruct((nc, RET_HEADS, RET_DK, RET_DK), MXU_DTYPE)],
        scratch_shapes=[pltpu.VMEM((RET_HEADS, RET_DK, RET_DK), F32)],
        compiler_params=_cparams(("arbitrary",)),
    )(p, p, p, p, cos, sin, decay, xi, zeta, gain)


def retention_backward(p, dy, prevs, cos, sin, gain):
    L = p.shape[0]
    nc = L // RET_CHUNK
    decay_np, xi_np, zeta_np, cd_np = _ret_constants()
    decay, xi, zeta = jnp.asarray(decay_np), jnp.asarray(xi_np), jnp.asarray(zeta_np)
    scale = RET_DK ** -0.5

    def body(q_ref, k_ref, v_ref, z_ref, dyb_ref, prev_ref, cos_ref, sin_ref, dec_ref, xi_ref, zeta_ref, gain_ref,
             dp_ref, dgain_ref, dstate):
        @pl.when(pl.program_id(0) == 0)
        def _():
            dstate[...] = jnp.zeros_like(dstate)
            dgain_ref[...] = jnp.zeros_like(dgain_ref)

        cs, sn = cos_ref[...], sin_ref[...]
        sz, dsz = _silu_and_grad(z_ref[...])
        dyb = dyb_ref[...]
        for h in range(RET_HEADS):
            hs = slice(RET_DK * h, RET_DK * (h + 1))
            qh = _rot(q_ref[:, hs], cs, sn)
            kh = _rot(k_ref[:, hs], cs, sn) * scale
            vh = v_ref[:, hs]
            prev = prev_ref[0, h]
            xih, zth = xi_ref[:, h:h + 1], zeta_ref[:, h:h + 1]
            sc = _mm_nt(qh, kh) * dec_ref[h]
            o = _mm(sc, vh) + _mm(qh * xih, prev)
            mu = jnp.mean(o, axis=-1, keepdims=True)
            oc = o - mu
            rstd = lax.rsqrt(jnp.mean(oc * oc, axis=-1, keepdims=True) + NORM_EPS)
            on = oc * rstd
            gh = gain_ref[:, hs]
            dyh = dyb[:, hs]
            dp_ref[:, 3072 + RET_DK * h:3072 + RET_DK * (h + 1)] = (dyh * on * gh * dsz[:, hs]).astype(dp_ref.dtype)
            dong = dyh * sz[:, hs]
            dgain_ref[:, hs] += jnp.sum(dong * on, axis=0, keepdims=True)
            don = dong * gh
            do = rstd * (don - jnp.mean(don, axis=-1, keepdims=True)
                         - on * jnp.mean(don * on, axis=-1, keepdims=True))
            dst = dstate[h]
            dsc = _mm_nt(do, vh) * dec_ref[h]
            dqh = _mm(dsc, kh) + _mm_nt(do, prev) * xih
            dkh = _mm_tn(dsc, qh) + _mm_nt(vh, dst) * zth
            dvh = _mm_tn(sc, do) + _mm(kh * zth, dst)
            dstate[h] = dst * float(cd_np[h]) + _mm_tn(qh * xih, do)
            dp_ref[:, hs] = _rot_t(dqh, cs, sn).astype(dp_ref.dtype)
            dp_ref[:, 1024 + RET_DK * h:1024 + RET_DK * (h + 1)] = (_rot_t(dkh, cs, sn) * scale).astype(dp_ref.dtype)
            dp_ref[:, 2048 + RET_DK * h:2048 + RET_DK * (h + 1)] = dvh.astype(dp_ref.dtype)

    blk = lambda c: pl.BlockSpec((RET_CHUNK, 1024), lambda i, c=c: (nc - 1 - i, c))
    tab = pl.BlockSpec((RET_CHUNK, 128), lambda i: (nc - 1 - i, 0))
    return pl.pallas_call(
        body, name="retention_backward", grid=(nc,),
        in_specs=[blk(2), blk(3), blk(4), blk(5), blk(1),
                  pl.BlockSpec((1, RET_HEADS, RET_DK, RET_DK), lambda i: (nc - 1 - i, 0, 0, 0)),
                  tab, tab, _full(decay.shape), _full(xi.shape), _full(zeta.shape), _full((1, 1024))],
        out_specs=[pl.BlockSpec((RET_CHUNK, 4096), lambda i: (nc - 1 - i, 0)), _full((1, 1024))],
        out_shape=[jax.ShapeDtypeStruct((L, 4096), MXU_DTYPE), jax.ShapeDtypeStruct((1, 1024), F32)],
        scratch_shapes=[pltpu.VMEM((RET_HEADS, RET_DK, RET_DK), F32)],
        compiler_params=_cparams(("arbitrary",)),
    )(p, p, p, p, dy, prevs, cos, sin, decay, xi, zeta, gain)


def _sgu_mix(p_ref, gain_ref, wm_ref, bt_ref, tl):
    pu, pv, z = p_ref[:, :2048], p_ref[:, 2048:4096], p_ref[:, 4096:]
    u, v = _gelu(pu), _gelu(pv)
    mu = jnp.mean(v, axis=-1, keepdims=True)
    vc = v - mu
    rstd = lax.rsqrt(jnp.mean(vc * vc, axis=-1, keepdims=True) + NORM_EPS)
    vn = vc * rstd
    vg = vn * gain_ref[...]
    mask = (lax.broadcasted_iota(jnp.int32, (SGU_CHUNK, SGU_CHUNK), 0)
            >= lax.broadcasted_iota(jnp.int32, (SGU_CHUNK, SGU_CHUNK), 1))
    wms = [jnp.where(mask, wm_ref[g], 0.0) for g in range(SGU_GROUPS)]
    rows = []
    for c in range(tl // SGU_CHUNK):
        rs = slice(SGU_CHUNK * c, SGU_CHUNK * (c + 1))
        cols = []
        for g in range(SGU_GROUPS):
            gs = slice(SGU_GDIM * g, SGU_GDIM * (g + 1))
            cols.append(_mm(wms[g], vg[rs, gs]) + bt_ref[:, g:g + 1])
        rows.append(jnp.concatenate(cols, axis=1))
    s = rows[0] if len(rows) == 1 else jnp.concatenate(rows, axis=0)
    return pu, pv, z, u, vn, rstd, vg, wms, mask, s


def sgu_forward(p, gain, wm, bt):
    L = p.shape[0]
    tl = min(TL_SGU, L)

    def body(p_ref, gain_ref, wm_ref, bt_ref, y_ref):
        _, _, z, u, _, _, _, _, _, s = _sgu_mix(p_ref, gain_ref, wm_ref, bt_ref, tl)
        sz, _ = _silu_and_grad(z)
        y_ref[...] = (u * s * sz).astype(y_ref.dtype)

    return pl.pallas_call(
        body, name="sgu_forward", grid=(L // tl,),
        in_specs=[pl.BlockSpec((tl, ODD_IN), lambda i: (i, 0)), _full((1, 2048)), _full(wm.shape), _full(bt.shape)],
        out_specs=pl.BlockSpec((tl, 2048), lambda i: (i, 0)),
        out_shape=jax.ShapeDtypeStruct((L, 2048), MXU_DTYPE),
        compiler_params=_cparams(("arbitrary",)),
    )(p, gain, wm, bt)


def sgu_backward(p, dy, gain, wm, bt):
    L = p.shape[0]
    tl = min(TL_SGU, L)

    def body(p_ref, dy_ref, gain_ref, wm_ref, bt_ref, dp_ref, dgain_ref, dwm_ref, dbt_ref):
        @pl.when(pl.program_id(0) == 0)
        def _():
            dgain_ref[...] = jnp.zeros_like(dgain_ref)
            dwm_ref[...] = jnp.zeros_like(dwm_ref)
            dbt_ref[...] = jnp.zeros_like(dbt_ref)

        pu, pv, z, u, vn, rstd, vg, wms, mask, s = _sgu_mix(p_ref, gain_ref, wm_ref, bt_ref, tl)
        sz, dsz = _silu_and_grad(z)
        dyv = dy_ref[...]
        dp_ref[:, 4096:] = (dyv * u * s * dsz).astype(dp_ref.dtype)
        dsg = dyv * sz
        dp_ref[:, :2048] = (dsg * s * _gelu_grad(pu)).astype(dp_ref.dtype)
        ds = dsg * u
        rows = []
        dbs = [jnp.zeros((SGU_CHUNK, 1), F32) for _ in range(SGU_GROUPS)]
        for c in range(tl // SGU_CHUNK):
            rs = slice(SGU_CHUNK * c, SGU_CHUNK * (c + 1))
            cols = []
            for g in range(SGU_GROUPS):
                gs = slice(SGU_GDIM * g, SGU_GDIM * (g + 1))
                dsg_c = ds[rs, gs]
                dbs[g] = dbs[g] + jnp.sum(dsg_c, axis=1, keepdims=True)
                dwm_ref[g] += jnp.where(mask, _mm_nt(dsg_c, vg[rs, gs]), 0.0)
                cols.append(_mm_tn(wms[g], dsg_c))
            rows.append(jnp.concatenate(cols, axis=1))
        dbt_ref[...] += jnp.concatenate(dbs, axis=1)
        dvg = rows[0] if len(rows) == 1 else jnp.concatenate(rows, axis=0)
        dgain_ref[...] += jnp.sum(dvg * vn, axis=0, keepdims=True)
        dvn = dvg * gain_ref[...]
        dv = rstd * (dvn - jnp.mean(dvn, axis=-1, keepdims=True) - vn * jnp.mean(dvn * vn, axis=-1, keepdims=True))
        dp_ref[:, 2048:4096] = (dv * _gelu_grad(pv)).astype(dp_ref.dtype)

    return pl.pallas_call(
        body, name="sgu_backward", grid=(L // tl,),
        in_specs=[pl.BlockSpec((tl, ODD_IN), lambda i: (i, 0)), pl.BlockSpec((tl, 2048), lambda i: (i, 0)),
                  _full((1, 2048)), _full(wm.shape), _full(bt.shape)],
        out_specs=[pl.BlockSpec((tl, ODD_IN), lambda i: (i, 0)), _full((1, 2048)), _full(wm.shape), _full(bt.shape)],
        out_shape=[jax.ShapeDtypeStruct((L, ODD_IN), MXU_DTYPE), jax.ShapeDtypeStruct((1, 2048), F32),
                   jax.ShapeDtypeStruct(wm.shape, F32), jax.ShapeDtypeStruct(bt.shape, F32)],
        compiler_params=_cparams(("arbitrary",)),
    )(p, dy, gain, wm, bt)


def local_grads(x, tgt, w):
    L = x.shape[0]
    ne, no, gf = w["norm_even"], w["norm_odd"], w["final_norm"].reshape(1, D_MODEL)
    w_in_e, w_out_e = w["w_in_even"], w["w_out_even"]
    w_in_o, w_out_o = w["w_in_odd"], w["w_out_odd"]
    w_glu = w["s5_w_glu"]
    lam_re, lam_im = w["s5_lam_re"][0], w["s5_lam_im"][0]
    log_dt = w["s5_log_dt"].reshape(S5_GROUPS, 1)
    bt_re = jnp.transpose(w["s5_b_re"][0], (2, 0, 1))
    bt_im = jnp.transpose(w["s5_b_im"][0], (2, 0, 1))
    c_re, c_im = w["s5_c_re"][0], w["s5_c_im"][0]
    sg_gain = w["sgu_norm_gain"]
    wm = w["sgu_w_spatial"][0]
    bt = jnp.transpose(w["sgu_b_spatial"][0])

    ab_re, ab_im, bb_re, bb_im, pw_re, pw_im = s5_params_fwd(lam_re, lam_im, log_dt, bt_re, bt_im)
    tf_re, tf_im, tb_re, tb_im = _s5_scan_tables(pw_re.reshape(8, S5_LANES), pw_im.reshape(8, S5_LANES))
    wbd = jnp.concatenate([_block_diag(jnp.transpose(bb_re, (1, 0, 2)), True),
                           _block_diag(jnp.transpose(bb_im, (1, 0, 2)), True)], axis=2).astype(MXU_DTYPE)
    cre = _block_diag(jnp.transpose(c_re, (0, 2, 1)), True).astype(MXU_DTYPE)
    cim = _block_diag(jnp.transpose(c_im, (0, 2, 1)), True).astype(MXU_DTYPE)
    cos, sin = _rope_tables(L)

    p1 = norm_matmul(x, ne, w_in_e, "even_in")
    ya, st_re, st_im = s5_forward(p1, wbd, cre, cim, tf_re, tf_im, w["s5_d"], w_glu, w["s5_b_glu"])
    yb, prevs = retention_forward(p1, cos, sin, w["ret_gn_gain"])
    ycat = jnp.concatenate([ya, yb], axis=1)
    x1 = matmul_residual(ycat, w_out_e, x, "even_out")
    p2 = norm_matmul(x1, no, w_in_o, "odd_in")
    y2 = sgu_forward(p2, sg_gain, wm, bt)
    dx2, loss, dgf = out_proj_loss(y2, w_out_o, x1, gf, tgt, "odd_out_loss")

    g = {}
    dy2, g["w_out_odd"] = out_proj_bwd(dx2, w_out_o, y2, "odd_out_bwd")
    dp2, g["sgu_norm_gain"], dwm, dbt = sgu_backward(p2, dy2, sg_gain, wm, bt)
    g["w_in_odd"] = in_proj_bwd_dw(x1, no, dp2, "odd_in_dw")
    dx1, g["norm_odd"] = in_proj_bwd_dx(x1, no, dp2, w_in_o, dx2, "odd_in_dx")
    dycat, g["w_out_even"] = out_proj_bwd(dx1, w_out_e, ycat, "even_out_bwd")
    dpb, g["ret_gn_gain"] = retention_backward(p1, dycat, prevs, cos, sin, w["ret_gn_gain"])
    (dpa, dwbd, dcre, dcim, dab_re, dab_im, g["s5_d"], g["s5_w_glu"], g["s5_b_glu"]) = s5_backward(
        p1, dycat, st_re, st_im, wbd, cre, cim, tf_re, tf_im, tb_re, tb_im, w["s5_d"], w_glu, w["s5_b_glu"])
    dp1 = jnp.concatenate([dpa, dpb], axis=1)
    g["w_in_even"] = in_proj_bwd_dw(x, ne, dp1, "even_in_dw")
    dx0, g["norm_even"] = in_proj_bwd_dx(x, ne, dp1, w_in_e, dx1, "even_in_dx")

    dbb_re = jnp.transpose(_block_diag_extract(dwbd[:, :, :512], S5_GROUP, S5_STATE), (1, 0, 2))
    dbb_im = jnp.transpose(_block_diag_extract(dwbd[:, :, 512:], S5_GROUP, S5_STATE), (1, 0, 2))
    dlr, dli, ddt, dbt_re, dbt_im = s5_params_bwd(
        lam_re, lam_im, log_dt, bt_re, bt_im, dab_re.reshape(8, S5_GROUPS, S5_STATE),
        dab_im.reshape(8, S5_GROUPS, S5_STATE), dbb_re, dbb_im)
    g["s5_lam_re"], g["s5_lam_im"] = dlr[None], dli[None]
    g["s5_log_dt"] = ddt.reshape(1, S5_GROUPS)
    g["s5_b_re"] = jnp.transpose(dbt_re, (1, 2, 0))[None]
    g["s5_b_im"] = jnp.transpose(dbt_im, (1, 2, 0))[None]
    g["s5_c_re"] = jnp.transpose(_block_diag_extract(dcre, S5_STATE, S5_GROUP), (0, 2, 1))[None]
    g["s5_c_im"] = jnp.transpose(_block_diag_extract(dcim, S5_STATE, S5_GROUP), (0, 2, 1))[None]
    g["sgu_w_spatial"] = dwm[None]
    g["sgu_b_spatial"] = jnp.transpose(dbt)[None]
    g["final_norm"] = dgf.reshape(D_MODEL)
    return loss, dx0, g


MESH = pl.DeviceIdType.MESH
ANY = pl.BlockSpec(memory_space=pl.ANY)


def _place():
    return lax.axis_index("x"), lax.axis_index("y"), lax.axis_index("c")


def _chip_peer(x, y, c, d):
    return (1 - x if d >= 2 else x, 1 - y if d % 2 else y, c)


def gather_weights(mats, vecs):
    n_m, n = len(mats), len(mats) + len(vecs)

    def body(*refs):
        in_refs, out_refs = refs[:n], refs[n:2 * n]
        stage = refs[2 * n:2 * n + n_m]
        send_sems, recv_sems, loc_sems = refs[2 * n + n_m:]
        x, y, c = _place()
        me = 2 * x + y
        for p in range(n_m):
            stage[p][...] = in_refs[p][...].astype(MXU_DTYPE)
        srcs = list(stage) + list(in_refs[n_m:])

        def remote(p, d, slab):
            return pltpu.make_async_remote_copy(
                src_ref=srcs[p], dst_ref=out_refs[p].at[slab], send_sem=send_sems.at[p, d - 1],
                recv_sem=recv_sems.at[p, d - 1], device_id=_chip_peer(x, y, c, d), device_id_type=MESH)

        local = [pltpu.make_async_copy(srcs[p], out_refs[p].at[me], loc_sems.at[p]) for p in range(n)]
        for p in range(n):
            local[p].start()
            for d in (1, 2, 3):
                remote(p, d, me).start()
        for p in range(n):
            for d in (1, 2, 3):
                px, py, _ = _chip_peer(x, y, c, d)
                remote(p, d, 2 * px + py).wait_recv()
        for p in range(n):
            for d in (1, 2, 3):
                remote(p, d, me).wait_send()
            local[p].wait()

    outs = [jax.ShapeDtypeStruct((N_CHIPS,) + m.shape, MXU_DTYPE) for m in mats]
    outs += [jax.ShapeDtypeStruct((N_CHIPS,) + v.shape, F32) for v in vecs]
    return pl.pallas_call(
        body, name="gather_weights",
        in_specs=[pl.BlockSpec(memory_space=pltpu.VMEM)] * n,
        out_specs=[ANY] * n, out_shape=outs,
        scratch_shapes=[pltpu.VMEM(m.shape, MXU_DTYPE) for m in mats]
        + [pltpu.SemaphoreType.DMA((n, 3)), pltpu.SemaphoreType.DMA((n, 3)), pltpu.SemaphoreType.DMA((n,))],
        compiler_params=pltpu.CompilerParams(vmem_limit_bytes=VMEM_LIMIT),
    )(*mats, *vecs)


def exchange_grads(shards, whole):
    n_s, n = len(shards), len(shards) + len(whole)

    def body(*refs):
        in_refs, out_refs = refs[:n], refs[n:2 * n]
        send_sems, recv_sems, loc_sems = refs[2 * n:]
        x, y, c = _place()
        me = 2 * x + y

        def src(p, slab):
            return in_refs[p].at[slab] if p < n_s else in_refs[p]

        def remote(p, d):
            px, py, _ = _chip_peer(x, y, c, d)
            return pltpu.make_async_remote_copy(
                src_ref=src(p, 2 * px + py), dst_ref=out_refs[p].at[d], send_sem=send_sems.at[p, d - 1],
                recv_sem=recv_sems.at[p, d - 1], device_id=(px, py, c), device_id_type=MESH)

        local = [pltpu.make_async_copy(src(p, me), out_refs[p].at[0], loc_sems.at[p]) for p in range(n)]
        for p in range(n):
            local[p].start()
            for d in (1, 2, 3):
                remote(p, d).start()
        for p in range(n):
            for d in (1, 2, 3):
                remote(p, d).wait_recv()
        for p in range(n):
            for d in (1, 2, 3):
                remote(p, d).wait_send()
            local[p].wait()

    outs = [jax.ShapeDtypeStruct(s.shape, F32) for s in shards]
    outs += [jax.ShapeDtypeStruct((N_CHIPS,) + a.shape, F32) for a in whole]
    return pl.pallas_call(
        body, name="exchange_grads", in_specs=[ANY] * n, out_specs=[ANY] * n, out_shape=outs,
        scratch_shapes=[pltpu.SemaphoreType.DMA((n, 3)), pltpu.SemaphoreType.DMA((n, 3)),
                        pltpu.SemaphoreType.DMA((n,))],
    )(*shards, *whole)


def sibling_exchange(arrs):
    n = len(arrs)

    def body(*refs):
        in_refs, out_refs = refs[:n], refs[n:2 * n]
        send_sems, recv_sems = refs[2 * n:]
        x, y, c = _place()
        copies = [pltpu.make_async_remote_copy(
            src_ref=in_refs[p], dst_ref=out_refs[p], send_sem=send_sems.at[p], recv_sem=recv_sems.at[p],
            device_id=(x, y, 1 - c), device_id_type=MESH) for p in range(n)]
        for cp in copies:
            cp.start()
        for cp in copies:
            cp.wait_recv()
        for cp in copies:
            cp.wait_send()

    return pl.pallas_call(
        body, name="sibling_exchange", in_specs=[ANY] * n, out_specs=[ANY] * n,
        out_shape=[jax.ShapeDtypeStruct(a.shape, a.dtype) for a in arrs],
        scratch_shapes=[pltpu.SemaphoreType.DMA((n,)), pltpu.SemaphoreType.DMA((n,))],
    )(*arrs)


def _row_block(rows):
    return 128 if rows % 128 == 0 else rows


def sum_slabs(r, name):
    _, R, C = r.shape
    tr = _row_block(R)

    def body(r_ref, o_ref):
        o_ref[...] = (r_ref[0] + r_ref[1]) + (r_ref[2] + r_ref[3])

    return pl.pallas_call(
        body, name=name, grid=(R // tr,),
        in_specs=[pl.BlockSpec((N_CHIPS, tr, C), lambda i: (0, i, 0))],
        out_specs=pl.BlockSpec((tr, C), lambda i: (i, 0)),
        out_shape=jax.ShapeDtypeStruct((R, C), F32),
        compiler_params=_cparams(("arbitrary",)),
    )(r)


def adam_update(w, m, v, ga, gb, name):
    R, C = w.shape
    tr = _row_block(R)
    gs = [ga] if gb is None else [ga, gb]

    def body(*refs):
        w_ref, m_ref, v_ref = refs[:3]
        g_refs = refs[3:3 + len(gs)]
        g_out, d_out, m_out, v_out = refs[3 + len(gs):]
        g = g_refs[0][...]
        if len(gs) == 2:
            g = g + g_refs[1][...]
        mn = ADAM_B1 * m_ref[...] + (1.0 - ADAM_B1) * g
        vn = ADAM_B2 * v_ref[...] + (1.0 - ADAM_B2) * (g * g)
        m_hat = mn / (1.0 - ADAM_B1 ** ADAM_STEP)
        v_hat = vn / (1.0 - ADAM_B2 ** ADAM_STEP)
        g_out[...] = g
        d_out[...] = -ADAM_LR * (m_hat / (jnp.sqrt(v_hat) + ADAM_EPS) + ADAM_WD * w_ref[...])
        m_out[...] = mn
        v_out[...] = vn

    blk = pl.BlockSpec((tr, C), lambda i: (i, 0))
    return pl.pallas_call(
        body, name=name, grid=(R // tr,),
        in_specs=[blk] * (3 + len(gs)), out_specs=[blk] * 4,
        out_shape=[jax.ShapeDtypeStruct((R, C), F32)] * 4,
        compiler_params=_cparams(("arbitrary",)),
    )(w, m, v, *gs)


WEIGHTS = ("norm_even", "w_in_even", "s5_lam_re", "s5_lam_im", "s5_log_dt", "s5_b_re", "s5_b_im", "s5_c_re",
           "s5_c_im", "s5_d", "s5_w_glu", "s5_b_glu", "ret_gn_gain", "w_out_even", "norm_odd", "w_in_odd",
           "sgu_norm_gain", "sgu_w_spatial", "sgu_b_spatial", "w_out_odd", "final_norm")
MATRICES = ("w_in_even", "s5_w_glu", "w_out_even", "w_in_odd", "w_out_odd")
SHARDED_VECS = ("norm_odd", "sgu_norm_gain")
REPLICATED = tuple(n for n in WEIGHTS if n not in MATRICES and n not in SHARDED_VECS)
PACKED = REPLICATED + SHARDED_VECS
LANES = 128


def _pack(parts, rows):
    flat = jnp.concatenate([p.reshape(-1) for p in parts])
    return jnp.pad(flat, (0, rows * LANES - flat.shape[0])).reshape(rows, LANES)


def _packed_rows(n_elems):
    return -(-n_elems // (8 * LANES)) * 8


def kernel(x, norm_even, w_in_even, s5_lam_re, s5_lam_im, s5_log_dt, s5_b_re, s5_b_im, s5_c_re, s5_c_im, s5_d, s5_w_glu, s5_b_glu, ret_gn_gain, w_out_even, norm_odd, w_in_odd, sgu_norm_gain, sgu_w_spatial, sgu_b_spatial, w_out_odd, final_norm, loss_target, m_norm_even, m_w_in_even, m_s5_lam_re, m_s5_lam_im, m_s5_log_dt, m_s5_b_re, m_s5_b_im, m_s5_c_re, m_s5_c_im, m_s5_d, m_s5_w_glu, m_s5_b_glu, m_ret_gn_gain, m_w_out_even, m_norm_odd, m_w_in_odd, m_sgu_norm_gain, m_sgu_w_spatial, m_sgu_b_spatial, m_w_out_odd, m_final_norm, v_norm_even, v_w_in_even, v_s5_lam_re, v_s5_lam_im, v_s5_log_dt, v_s5_b_re, v_s5_b_im, v_s5_c_re, v_s5_c_im, v_s5_d, v_s5_w_glu, v_s5_b_glu, v_ret_gn_gain, v_w_out_even, v_norm_odd, v_w_in_odd, v_sgu_norm_gain, v_sgu_w_spatial, v_sgu_b_spatial, v_w_out_odd, v_final_norm):
    w = dict(norm_even=norm_even, w_in_even=w_in_even, s5_lam_re=s5_lam_re, s5_lam_im=s5_lam_im, s5_log_dt=s5_log_dt, s5_b_re=s5_b_re, s5_b_im=s5_b_im, s5_c_re=s5_c_re, s5_c_im=s5_c_im, s5_d=s5_d, s5_w_glu=s5_w_glu, s5_b_glu=s5_b_glu, ret_gn_gain=ret_gn_gain, w_out_even=w_out_even, norm_odd=norm_odd, w_in_odd=w_in_odd, sgu_norm_gain=sgu_norm_gain, sgu_w_spatial=sgu_w_spatial, sgu_b_spatial=sgu_b_spatial, w_out_odd=w_out_odd, final_norm=final_norm)
    m = dict(norm_even=m_norm_even, w_in_even=m_w_in_even, s5_lam_re=m_s5_lam_re, s5_lam_im=m_s5_lam_im, s5_log_dt=m_s5_log_dt, s5_b_re=m_s5_b_re, s5_b_im=m_s5_b_im, s5_c_re=m_s5_c_re, s5_c_im=m_s5_c_im, s5_d=m_s5_d, s5_w_glu=m_s5_w_glu, s5_b_glu=m_s5_b_glu, ret_gn_gain=m_ret_gn_gain, w_out_even=m_w_out_even, norm_odd=m_norm_odd, w_in_odd=m_w_in_odd, sgu_norm_gain=m_sgu_norm_gain, sgu_w_spatial=m_sgu_w_spatial, sgu_b_spatial=m_sgu_b_spatial, w_out_odd=m_w_out_odd, final_norm=m_final_norm)
    v = dict(norm_even=v_norm_even, w_in_even=v_w_in_even, s5_lam_re=v_s5_lam_re, s5_lam_im=v_s5_lam_im, s5_log_dt=v_s5_log_dt, s5_b_re=v_s5_b_re, s5_b_im=v_s5_b_im, s5_c_re=v_s5_c_re, s5_c_im=v_s5_c_im, s5_d=v_s5_d, s5_w_glu=v_s5_w_glu, s5_b_glu=v_s5_b_glu, ret_gn_gain=v_ret_gn_gain, w_out_even=v_w_out_even, norm_odd=v_norm_odd, w_in_odd=v_w_in_odd, sgu_norm_gain=v_sgu_norm_gain, sgu_w_spatial=v_sgu_w_spatial, sgu_b_spatial=v_sgu_b_spatial, w_out_odd=v_w_out_odd, final_norm=v_final_norm)
    me = 2 * lax.axis_index("x") + lax.axis_index("y")

    gathered = gather_weights([w[n][0] for n in MATRICES], [w[n] for n in SHARDED_VECS])
    full = dict(w)
    full["w_in_even"], full["w_in_odd"] = gathered[0], gathered[3]
    full["s5_w_glu"] = gathered[1].reshape(S5_WIDTH, S5_WIDTH)
    full["w_out_even"] = gathered[2].reshape(2 * S5_WIDTH, D_MODEL)
    full["w_out_odd"] = gathered[4].reshape(SGU_WIDTH, D_MODEL)
    full["norm_odd"] = gathered[5].reshape(1, D_MODEL)
    full["sgu_norm_gain"] = gathered[6].reshape(1, SGU_WIDTH)

    loss, grad_x, g = local_grads(x[0], loss_target[0], full)

    mat_shards = [g[n].reshape((N_CHIPS,) + w[n].shape[1:]) for n in MATRICES]
    n_small = sum(int(np.prod(g[n].shape)) for n in PACKED) + 1
    rows = _packed_rows(n_small)
    packed = _pack([g[n] for n in PACKED] + [loss[0, :1]], rows)
    recv = exchange_grads(mat_shards, [packed])
    part = [sum_slabs(r, "sum_" + n) for r, n in zip(recv, MATRICES + ("packed",))]
    other = sibling_exchange(part)

    out_g, out_d, out_m, out_v = {}, {}, {}, {}
    for k, n in enumerate(MATRICES):
        res = adam_update(w[n][0], m[n][0], v[n][0], part[k], other[k], "adam_" + n)
        out_g[n], out_d[n], out_m[n], out_v[n] = (r[None] for r in res)
    zeros = {n: jnp.zeros_like(g[n]) for n in SHARDED_VECS}
    res = adam_update(_pack([w[n] for n in REPLICATED] + [zeros[n] for n in SHARDED_VECS], rows),
                      _pack([m[n] for n in REPLICATED] + [zeros[n] for n in SHARDED_VECS], rows),
                      _pack([v[n] for n in REPLICATED] + [zeros[n] for n in SHARDED_VECS], rows),
                      part[-1], other[-1], "adam_packed")
    flat = [r.reshape(-1) for r in res]
    off = 0
    vec_grads = {}
    for n in PACKED:
        size = int(np.prod(g[n].shape))
        if n in REPLICATED:
            for dst, f in zip((out_g, out_d, out_m, out_v), flat):
                dst[n] = f[off:off + size].reshape(w[n].shape)
        else:
            vec_grads[n] = lax.dynamic_slice(flat[0], (off + me * w[n].shape[1],), (w[n].shape[1],))
        off += size
    total_loss = flat[0][off]
    vrows = _packed_rows(sum(w[n].shape[1] for n in SHARDED_VECS))
    res = adam_update(_pack([w[n] for n in SHARDED_VECS], vrows), _pack([m[n] for n in SHARDED_VECS], vrows),
                      _pack([v[n] for n in SHARDED_VECS], vrows), _pack([vec_grads[n] for n in SHARDED_VECS], vrows),
                      None, "adam_vecs")
    flat = [r.reshape(-1) for r in res]
    off = 0
    for n in SHARDED_VECS:
        size = w[n].shape[1]
        for dst, f in zip((out_g, out_d, out_m, out_v), flat):
            dst[n] = f[off:off + size].reshape(w[n].shape)
        off += size

    return (total_loss, grad_x[None], *[out_g[n] for n in WEIGHTS], *[out_d[n] for n in WEIGHTS],
            *[out_m[n] for n in WEIGHTS], *[out_v[n] for n in WEIGHTS])
```

```python
import functools
import math

import numpy as np
import jax
import jax.numpy as jnp
from jax import lax
from jax.experimental import pallas as pl
from jax.experimental.pallas import tpu as pltpu

F32 = jnp.float32
MXU_DTYPE = jnp.bfloat16
NORM_EPS = 1e-6
D_MODEL = 1024
S5_WIDTH = 1024
S5_GROUP = 16
S5_GROUPS = 64
S5_STATE = 64
S5_LANES = S5_GROUPS * S5_STATE
S5_KBLK = 8
RET_HEADS = 4
RET_DK = 256
RET_CHUNK = 128
ROPE_BASE = 10000.0
SGU_WIDTH = 2048
SGU_GROUPS = 4
SGU_GDIM = 512
SGU_CHUNK = 128
EVEN_IN = 6144
ODD_IN = 6144
ADAM_LR = 0.001
ADAM_B1 = 0.9
ADAM_B2 = 0.999
ADAM_EPS = 1e-08
ADAM_WD = 0.01
ADAM_STEP = 10
N_CHIPS = 4
VMEM_LIMIT = 56 * 1024 * 1024

TL_PROJ = 512
TL_S5 = 128
TL_SGU = 128


def _cparams(sem, **kw):
    return pltpu.CompilerParams(dimension_semantics=sem, vmem_limit_bytes=VMEM_LIMIT, **kw)


def _mm(a, b):
    return jnp.dot(a.astype(MXU_DTYPE), b.astype(MXU_DTYPE), preferred_element_type=F32)


def _mm_nt(a, b):
    return lax.dot_general(a.astype(MXU_DTYPE), b.astype(MXU_DTYPE),
                           (((1,), (1,)), ((), ())), preferred_element_type=F32)


def _mm_tn(a, b):
    return lax.dot_general(a.astype(MXU_DTYPE), b.astype(MXU_DTYPE),
                           (((0,), (0,)), ((), ())), preferred_element_type=F32)


_GELU_C = math.sqrt(2.0 / math.pi)


def _gelu(x):
    return 0.5 * x * (1.0 + jnp.tanh(_GELU_C * (x + 0.044715 * x * x * x)))


def _gelu_grad(x):
    th = jnp.tanh(_GELU_C * (x + 0.044715 * x * x * x))
    return 0.5 * (1.0 + th) + 0.5 * x * (1.0 - th * th) * _GELU_C * (1.0 + 3.0 * 0.044715 * x * x)


def _sigmoid(x):
    return 1.0 / (1.0 + jnp.exp(-x))


def _silu_and_grad(x):
    s = _sigmoid(x)
    return x * s, s * (1.0 + x * (1.0 - s))


def _rms(x):
    return lax.rsqrt(jnp.mean(x * x, axis=-1, keepdims=True) + NORM_EPS)


def _full(shape):
    nd = len(shape)
    return pl.BlockSpec(shape, lambda *_: (0,) * nd)


MESH = pl.DeviceIdType.MESH
ANY = pl.BlockSpec(memory_space=pl.ANY)


def _place():
    return lax.axis_index("x"), lax.axis_index("y"), lax.axis_index("c")


def _chip_peer(x, y, c, d):
    return (1 - x if d >= 2 else x, 1 - y if d % 2 else y, c)


class _Plan:
    def __init__(self, inputs, out_shape, build):
        self.inputs, self.out_shape, self._build = list(inputs), list(out_shape), build
        n = len(self.inputs)
        self.sems = [pltpu.SemaphoreType.DMA((n, 3)), pltpu.SemaphoreType.DMA((n, 3)), pltpu.SemaphoreType.DMA((n,))]

    def start(self, in_refs, out_refs, sems):
        send, recv, local = self._build(in_refs, out_refs, sems)
        for p in range(len(self.inputs)):
            local[p].start()
            for cp in send[p]:
                cp.start()

    def wait(self, in_refs, out_refs, sems):
        send, recv, local = self._build(in_refs, out_refs, sems)
        for p in range(len(self.inputs)):
            for cp in recv[p]:
                cp.wait_recv()
        for p in range(len(self.inputs)):
            for cp in send[p]:
                cp.wait_send()
            local[p].wait()


def gather_plan(shards):
    def build(in_refs, out_refs, sems):
        send_sems, recv_sems, loc_sems = sems
        x, y, c = _place()
        me = 2 * x + y

        def remote(p, d, slab):
            return pltpu.make_async_remote_copy(
                src_ref=in_refs[p], dst_ref=out_refs[p].at[slab], send_sem=send_sems.at[p, d - 1],
                recv_sem=recv_sems.at[p, d - 1], device_id=_chip_peer(x, y, c, d), device_id_type=MESH)

        n = len(in_refs)
        send = [[remote(p, d, me) for d in (1, 2, 3)] for p in range(n)]
        recv = [[remote(p, d, me ^ d) for d in (1, 2, 3)] for p in range(n)]
        local = [pltpu.make_async_copy(in_refs[p], out_refs[p].at[me], loc_sems.at[p]) for p in range(n)]
        return send, recv, local

    return _Plan(shards, [jax.ShapeDtypeStruct((N_CHIPS,) + s.shape, s.dtype) for s in shards], build)


def reduce_plan(shards, whole=()):
    n_s = len(shards)

    def build(in_refs, out_refs, sems):
        send_sems, recv_sems, loc_sems = sems
        x, y, c = _place()
        me = 2 * x + y

        def src(p, slab):
            return in_refs[p].at[slab] if p < n_s else in_refs[p]

        def remote(p, d):
            return pltpu.make_async_remote_copy(
                src_ref=src(p, me ^ d), dst_ref=out_refs[p].at[d], send_sem=send_sems.at[p, d - 1],
                recv_sem=recv_sems.at[p, d - 1], device_id=_chip_peer(x, y, c, d), device_id_type=MESH)

        n = len(in_refs)
        send = [[remote(p, d) for d in (1, 2, 3)] for p in range(n)]
        local = [pltpu.make_async_copy(src(p, me), out_refs[p].at[0], loc_sems.at[p]) for p in range(n)]
        return send, send, local

    outs = [jax.ShapeDtypeStruct(s.shape, s.dtype) for s in shards]
    outs += [jax.ShapeDtypeStruct((N_CHIPS,) + a.shape, a.dtype) for a in whole]
    return _Plan(list(shards) + list(whole), outs, build)


def run_plan(plan, name):
    n = len(plan.inputs)

    def body(*refs):
        plan.start(refs[:n], refs[n:2 * n], refs[2 * n:])
        plan.wait(refs[:n], refs[n:2 * n], refs[2 * n:])

    return pl.pallas_call(body, name=name, in_specs=[ANY] * n, out_specs=[ANY] * n, out_shape=plan.out_shape,
                          scratch_shapes=plan.sems)(*plan.inputs)


def _call(body, plan, *, name, grid, in_specs, out_specs, out_shape, sem, scratch_shapes=()):
    single = not isinstance(out_shape, (list, tuple))
    out_specs = [out_specs] if single else list(out_specs)
    out_shape = [out_shape] if single else list(out_shape)
    n_in, n_out, n_scr = len(in_specs), len(out_specs), len(scratch_shapes)
    ci = 0 if plan is None else len(plan.inputs)

    def hosted(*refs):
        ins, cins = refs[:n_in], refs[n_in:n_in + ci]
        k = n_in + ci
        outs, couts = refs[k:k + n_out], refs[k + n_out:k + n_out + ci]
        k += n_out + ci
        scr, sems = refs[k:k + n_scr], refs[k + n_scr:]
        ids = [pl.program_id(a) for a in range(len(grid))]
        first = functools.reduce(jnp.logical_and, [i == 0 for i in ids])
        last = functools.reduce(jnp.logical_and, [i == g - 1 for i, g in zip(ids, grid)])

        @pl.when(first)
        def _():
            plan.start(cins, couts, sems)

        body(*ins, *outs, *scr)

        @pl.when(last)
        def _():
            plan.wait(cins, couts, sems)

    def run(*args):
        if plan is None:
            res = pl.pallas_call(body, name=name, grid=grid, in_specs=list(in_specs), out_specs=out_specs,
                                 out_shape=out_shape, scratch_shapes=list(scratch_shapes),
                                 compiler_params=_cparams(sem))(*args)
            return (res[0] if single else res), []
        res = pl.pallas_call(hosted, name=name, grid=grid, in_specs=list(in_specs) + [ANY] * ci,
                             out_specs=out_specs + [ANY] * ci, out_shape=out_shape + plan.out_shape,
                             scratch_shapes=list(scratch_shapes) + plan.sems,
                             compiler_params=_cparams(sem))(*args, *plan.inputs)
        return (res[0] if single else res[:n_out]), list(res[n_out:])

    return run


def norm_matmul(x, g, w, name, plan=None):
    L, D = x.shape
    nb, _, tn = w.shape
    tl = min(TL_PROJ, L)

    def body(x_ref, g_ref, w_ref, o_ref):
        xv = x_ref[...]
        o_ref[...] = _mm(xv * _rms(xv) * g_ref[...], w_ref[0])

    return _call(
        body, plan, name=name, grid=(nb, L // tl),
        in_specs=[pl.BlockSpec((tl, D), lambda n, i: (i, 0)), _full((1, D)),
                  pl.BlockSpec((1, D, tn), lambda n, i: (n, 0, 0))],
        out_specs=pl.BlockSpec((tl, tn), lambda n, i: (i, n)),
        out_shape=jax.ShapeDtypeStruct((L, nb * tn), F32),
        sem=("arbitrary", "arbitrary"),
    )(x, g, w)


def matmul_residual(y, w, x, name):
    L, K = y.shape
    D = w.shape[1]
    tl = min(TL_PROJ, L)

    def body(y_ref, w_ref, x_ref, o_ref):
        o_ref[...] = x_ref[...] + _mm(y_ref[...], w_ref[...])

    return pl.pallas_call(
        body, name=name, grid=(L // tl,),
        in_specs=[pl.BlockSpec((tl, K), lambda i: (i, 0)), _full((K, D)),
                  pl.BlockSpec((tl, D), lambda i: (i, 0))],
        out_specs=pl.BlockSpec((tl, D), lambda i: (i, 0)),
        out_shape=jax.ShapeDtypeStruct((L, D), F32),
        compiler_params=_cparams(("arbitrary",)),
    )(y, w, x)


def out_proj_loss(y, w, x, gf, tgt, name):
    L, K = y.shape
    D = w.shape[1]
    tl = min(TL_PROJ, L)

    def body(y_ref, w_ref, x_ref, gf_ref, t_ref, dx_ref, loss_ref, dg_ref):
        @pl.when(pl.program_id(0) == 0)
        def _():
            loss_ref[...] = jnp.zeros_like(loss_ref)
            dg_ref[...] = jnp.zeros_like(dg_ref)

        x2 = x_ref[...] + _mm(y_ref[...], w_ref[...])
        r = _rms(x2)
        xn = x2 * r
        e = xn * gf_ref[...] - t_ref[...]
        loss_ref[...] += (0.5 / D) * jnp.sum(e * e)
        dout = e * (1.0 / D)
        dg_ref[...] += jnp.sum(dout * xn, axis=0, keepdims=True)
        dxn = dout * gf_ref[...]
        dx_ref[...] = r * (dxn - xn * jnp.mean(dxn * xn, axis=-1, keepdims=True))

    return pl.pallas_call(
        body, name=name, grid=(L // tl,),
        in_specs=[pl.BlockSpec((tl, K), lambda i: (i, 0)), _full((K, D)),
                  pl.BlockSpec((tl, D), lambda i: (i, 0)), _full((1, D)),
                  pl.BlockSpec((tl, D), lambda i: (i, 0))],
        out_specs=[pl.BlockSpec((tl, D), lambda i: (i, 0)), _full((8, 128)), _full((1, D))],
        out_shape=[jax.ShapeDtypeStruct((L, D), F32), jax.ShapeDtypeStruct((8, 128), F32),
                   jax.ShapeDtypeStruct((1, D), F32)],
        compiler_params=_cparams(("arbitrary",)),
    )(y, w, x, gf, tgt)


def out_proj_bwd(dx, w, y, name):
    L, D = dx.shape
    K = w.shape[0]
    tl = min(TL_PROJ, L)

    def body(dx_ref, w_ref, y_ref, dy_ref, dw_ref):
        @pl.when(pl.program_id(0) == 0)
        def _():
            dw_ref[...] = jnp.zeros_like(dw_ref)

        dxv = dx_ref[...]
        dy_ref[...] = _mm_nt(dxv, w_ref[...])
        dw_ref[...] += _mm_tn(y_ref[...], dxv)

    return pl.pallas_call(
        body, name=name, grid=(L // tl,),
        in_specs=[pl.BlockSpec((tl, D), lambda i: (i, 0)), _full((K, D)),
                  pl.BlockSpec((tl, K), lambda i: (i, 0))],
        out_specs=[pl.BlockSpec((tl, K), lambda i: (i, 0)), _full((K, D))],
        out_shape=[jax.ShapeDtypeStruct((L, K), F32), jax.ShapeDtypeStruct((K, D), F32)],
        compiler_params=_cparams(("arbitrary",)),
    )(dx, w, y)


def in_proj_bwd_dx(x, g, dp, w, dres, name, plan=None):
    L, D = x.shape
    nb, _, tn = w.shape
    tl = min(TL_PROJ, L)

    def body(x_ref, g_ref, dres_ref, dp_ref, w_ref, dx_ref, dg_ref):
        @pl.when(pl.program_id(0) == 0)
        def _():
            dg_ref[...] = jnp.zeros_like(dg_ref)

        dh = _mm_nt(dp_ref[:, :tn], w_ref[0])
        for k in range(1, nb):
            dh = dh + _mm_nt(dp_ref[:, tn * k:tn * (k + 1)], w_ref[k])
        xv = x_ref[...]
        r = _rms(xv)
        xn = xv * r
        dg_ref[...] += jnp.sum(dh * xn, axis=0, keepdims=True)
        dxn = dh * g_ref[...]
        dx_ref[...] = dres_ref[...] + r * (dxn - xn * jnp.mean(dxn * xn, axis=-1, keepdims=True))

    return _call(
        body, plan, name=name, grid=(L // tl,),
        in_specs=[pl.BlockSpec((tl, D), lambda i: (i, 0)), _full((1, D)), pl.BlockSpec((tl, D), lambda i: (i, 0)),
                  pl.BlockSpec((tl, nb * tn), lambda i: (i, 0)), _full(w.shape)],
        out_specs=[pl.BlockSpec((tl, D), lambda i: (i, 0)), _full((1, D))],
        out_shape=[jax.ShapeDtypeStruct((L, D), F32), jax.ShapeDtypeStruct((1, D), F32)],
        sem=("arbitrary",),
    )(x, g, dres, dp, w)


def in_proj_bwd_dw(x, g, dp, name):
    L, D = x.shape
    tn = dp.shape[1] // N_CHIPS
    tl = min(TL_PROJ, L)

    def body(x_ref, g_ref, dp_ref, dw_ref):
        @pl.when(pl.program_id(1) == 0)
        def _():
            dw_ref[...] = jnp.zeros_like(dw_ref)

        xv = x_ref[...]
        dw_ref[0] += _mm_tn(xv * _rms(xv) * g_ref[...], dp_ref[...])

    return pl.pallas_call(
        body, name=name, grid=(N_CHIPS, L // tl),
        in_specs=[pl.BlockSpec((tl, D), lambda n, i: (i, 0)), _full((1, D)),
                  pl.BlockSpec((tl, tn), lambda n, i: (i, n))],
        out_specs=pl.BlockSpec((1, D, tn), lambda n, i: (n, 0, 0)),
        out_shape=jax.ShapeDtypeStruct((N_CHIPS, D, tn), F32),
        compiler_params=_cparams(("arbitrary", "arbitrary")),
    )(x, g, dp)


def _s5_param_fn(lam_re, lam_im, log_dt, b_re, b_im):
    lr = jnp.minimum(lam_re, -1e-4)
    li = lam_im
    dt = jnp.exp(log_dt)
    mag = jnp.exp(lr * dt)
    ab_re = mag * jnp.cos(li * dt)
    ab_im = mag * jnp.sin(li * dt)
    den = lr * lr + li * li
    n_re = ab_re - 1.0
    n_im = ab_im
    z_re = (n_re * lr + n_im * li) / den
    z_im = (n_im * lr - n_re * li) / den
    bb_re = z_re[None] * b_re - z_im[None] * b_im
    bb_im = z_re[None] * b_im + z_im[None] * b_re
    return ab_re, ab_im, bb_re, bb_im


def s5_params_fwd(lam_re, lam_im, log_dt, b_re, b_im):
    G, P = lam_re.shape
    H = b_re.shape[0]

    def body(lr_ref, li_ref, dt_ref, br_ref, bi_ref, abr_ref, abi_ref, bbr_ref, bbi_ref, pr_ref, pi_ref):
        ab_re, ab_im, bb_re, bb_im = _s5_param_fn(lr_ref[...], li_ref[...], dt_ref[...], br_ref[...], bi_ref[...])
        abr_ref[...] = ab_re
        abi_ref[...] = ab_im
        bbr_ref[...] = bb_re
        bbi_ref[...] = bb_im
        cr, ci = ab_re, ab_im
        pr_ref[0] = cr
        pi_ref[0] = ci
        for k in range(1, 8):
            cr, ci = cr * ab_re - ci * ab_im, cr * ab_im + ci * ab_re
            pr_ref[k] = cr
            pi_ref[k] = ci

    shp = lambda *s: jax.ShapeDtypeStruct(s, F32)
    return pl.pallas_call(
        body, name="s5_params_fwd",
        out_shape=[shp(G, P), shp(G, P), shp(H, G, P), shp(H, G, P), shp(8, G, P), shp(8, G, P)],
    )(lam_re, lam_im, log_dt, b_re, b_im)


def s5_params_bwd(lam_re, lam_im, log_dt, b_re, b_im, d_ab_re, d_ab_im, d_bb_re, d_bb_im):
    G, P = lam_re.shape
    H = b_re.shape[0]

    def body(lr_ref, li_ref, dt_ref, br_ref, bi_ref, g0, g1, g2, g3, o0, o1, o2, o3, o4):
        prim = (lr_ref[...], li_ref[...], dt_ref[...], br_ref[...], bi_ref[...])
        _, vjp = jax.vjp(_s5_param_fn, *prim)
        d = vjp((jnp.sum(g0[...], axis=0), jnp.sum(g1[...], axis=0), g2[...], g3[...]))
        o0[...], o1[...], o2[...], o3[...], o4[...] = d

    shp = lambda *s: jax.ShapeDtypeStruct(s, F32)
    return pl.pallas_call(
        body, name="s5_params_bwd",
        out_shape=[shp(G, P), shp(G, P), shp(G, 1), shp(H, G, P), shp(H, G, P)],
    )(lam_re, lam_im, log_dt, b_re, b_im, d_ab_re, d_ab_im, d_bb_re, d_bb_im)


def _s5_scan_tables(pw_re, pw_im):
    t = np.arange(8)[:, None]
    f_re, f_im, b_re, b_im = [pw_re], [pw_im], [pw_re[::-1]], [pw_im[::-1]]
    for k in (1, 2, 4):
        mf = jnp.asarray((t >= k).astype(np.float32))
        mb = jnp.asarray((t <= 7 - k).astype(np.float32))
        f_re.append(mf * pw_re[k - 1][None])
        f_im.append(mf * pw_im[k - 1][None])
        b_re.append(mb * pw_re[k - 1][None])
        b_im.append(mb * pw_im[k - 1][None])
    return jnp.stack(f_re), jnp.stack(f_im), jnp.stack(b_re), jnp.stack(b_im)


def _scan_fwd_tile(br, bi, tf_re, tf_im, lanes, cr, ci):
    for j, k in enumerate((1, 2, 4)):
        pr, pi = tf_re[j + 1, :, lanes], tf_im[j + 1, :, lanes]
        sr, si = pltpu.roll(br, k, 0), pltpu.roll(bi, k, 0)
        br, bi = br + (pr * sr - pi * si), bi + (pr * si + pi * sr)
    ar, ai = tf_re[0, :, lanes], tf_im[0, :, lanes]
    br, bi = br + (ar * cr - ai * ci), bi + (ar * ci + ai * cr)
    return br, bi


def _scan_bwd_tile(gr, gi, tb_re, tb_im, lanes, cr, ci):
    for j, k in enumerate((1, 2, 4)):
        pr, pi = tb_re[j + 1, :, lanes], tb_im[j + 1, :, lanes]
        sr, si = pltpu.roll(gr, 8 - k, 0), pltpu.roll(gi, 8 - k, 0)
        gr, gi = gr + (pr * sr + pi * si), gi + (pr * si - pi * sr)
    ar, ai = tb_re[0, :, lanes], tb_im[0, :, lanes]
    gr, gi = gr + (ar * cr + ai * ci), gi + (ar * ci - ai * cr)
    return gr, gi


_LANE_BLK = 512


def _s5_forward_chunk(u, wbd_ref, tf_re, tf_im, s_re, s_im, c0_re, c0_im, tl):
    for k in range(S5_KBLK):
        bu = _mm(u[:, 128 * k:128 * (k + 1)], wbd_ref[k])
        s_re[pl.ds(8, tl), 512 * k:512 * (k + 1)] = bu[:, :512]
        s_im[pl.ds(8, tl), 512 * k:512 * (k + 1)] = bu[:, 512:]
    s_re[pl.ds(0, 8), :] = jnp.broadcast_to(c0_re, (8, S5_LANES))
    s_im[pl.ds(0, 8), :] = jnp.broadcast_to(c0_im, (8, S5_LANES))
    outs_re, outs_im = [], []
    for b in range(S5_LANES // _LANE_BLK):
        lanes = slice(_LANE_BLK * b, _LANE_BLK * (b + 1))

        def tile(i, carry, lanes=lanes):
            r = pl.multiple_of(8 + i * 8, 8)
            br, bi = _scan_fwd_tile(s_re[pl.ds(r, 8), lanes], s_im[pl.ds(r, 8), lanes],
                                    tf_re, tf_im, lanes, carry[0], carry[1])
            s_re[pl.ds(r, 8), lanes] = br
            s_im[pl.ds(r, 8), lanes] = bi
            return br[7:8], bi[7:8]

        cr, ci = lax.fori_loop(0, tl // 8, tile, (c0_re[:, lanes], c0_im[:, lanes]))
        outs_re.append(cr)
        outs_im.append(ci)
    return jnp.concatenate(outs_re, axis=1), jnp.concatenate(outs_im, axis=1)


def _s5_readout(s_re, s_im, cre_ref, cim_ref, tl):
    ys = []
    for k in range(S5_KBLK):
        lanes = slice(512 * k, 512 * (k + 1))
        ys.append(_mm(s_re[pl.ds(8, tl), lanes], cre_ref[k]) - _mm(s_im[pl.ds(8, tl), lanes], cim_ref[k]))
    return jnp.concatenate(ys, axis=1)


def s5_forward(p, wbd, cre, cim, tf_re, tf_im, d_skip, w_glu, b_glu, plan=None):
    L = p.shape[0]
    tl = min(TL_S5, L)
    nch = L // tl

    def body(u_ref, z_ref, wbd_ref, cre_ref, cim_ref, tfr_ref, tfi_ref, d_ref, wg_ref, bg_ref,
             ya_ref, st_re_ref, st_im_ref, s_re, s_im, car_re, car_im):
        @pl.when(pl.program_id(0) == 0)
        def _():
            car_re[...] = jnp.zeros_like(car_re)
            car_im[...] = jnp.zeros_like(car_im)

        c0_re, c0_im = car_re[...], car_im[...]
        st_re_ref[0] = c0_re
        st_im_ref[0] = c0_im
        u = u_ref[...]
        e_re, e_im = _s5_forward_chunk(u, wbd_ref, tfr_ref, tfi_ref, s_re, s_im, c0_re, c0_im, tl)
        car_re[...] = e_re
        car_im[...] = e_im
        y = _s5_readout(s_re, s_im, cre_ref, cim_ref, tl) + d_ref[...] * u
        yg = _gelu(y)
        gate = _sigmoid(_mm(yg, wg_ref[...]) + bg_ref[...])
        sz, _ = _silu_and_grad(z_ref[...])
        ya_ref[...] = (yg * gate * sz).astype(ya_ref.dtype)

    return _call(
        body, plan, name="s5_forward", grid=(nch,),
        in_specs=[pl.BlockSpec((tl, 1024), lambda i: (i, 0)), pl.BlockSpec((tl, 1024), lambda i: (i, 1)),
                  _full(wbd.shape), _full(cre.shape), _full(cim.shape), _full(tf_re.shape), _full(tf_im.shape),
                  _full((1, 1024)), _full((1024, 1024)), _full((1, 1024))],
        out_specs=[pl.BlockSpec((tl, 1024), lambda i: (i, 0)),
                   pl.BlockSpec((1, 1, S5_LANES), lambda i: (i, 0, 0)),
                   pl.BlockSpec((1, 1, S5_LANES), lambda i: (i, 0, 0))],
        out_shape=[jax.ShapeDtypeStruct((L, 1024), MXU_DTYPE),
                   jax.ShapeDtypeStruct((nch, 1, S5_LANES), F32), jax.ShapeDtypeStruct((nch, 1, S5_LANES), F32)],
        scratch_shapes=[pltpu.VMEM((tl + 8, S5_LANES), F32), pltpu.VMEM((tl + 8, S5_LANES), F32),
                        pltpu.VMEM((1, S5_LANES), F32), pltpu.VMEM((1, S5_LANES), F32)],
        sem=("arbitrary",),
    )(p, p, wbd, cre, cim, tf_re, tf_im, d_skip, w_glu, b_glu)


def s5_backward(p, dy, st_re, st_im, wbd, cre, cim, tf_re, tf_im, tb_re, tb_im, d_skip, w_glu, b_glu, plan=None):
    L = p.shape[0]
    tl = min(TL_S5, L)
    nch = L // tl
    rev = lambda i: (nch - 1 - i, 0)
    rev1 = lambda i: (nch - 1 - i, 1)
    rev3 = lambda i: (nch - 1 - i, 0, 0)

    def body(u_ref, z_ref, dya_ref, str_ref, sti_ref, wbd_ref, cre_ref, cim_ref, tfr_ref, tfi_ref, tbr_ref, tbi_ref,
             d_ref, wg_ref, bg_ref,
             dp_ref, dwbd_ref, dcre_ref, dcim_ref, dabr_ref, dabi_ref, dd_ref, dwg_ref, dbg_ref,
             s_re, s_im, g_re, g_im, car_re, car_im):
        @pl.when(pl.program_id(0) == 0)
        def _():
            car_re[...] = jnp.zeros_like(car_re)
            car_im[...] = jnp.zeros_like(car_im)
            for r in (dwbd_ref, dcre_ref, dcim_ref, dabr_ref, dabi_ref, dd_ref, dwg_ref, dbg_ref):
                r[...] = jnp.zeros_like(r)

        u = u_ref[...]
        _s5_forward_chunk(u, wbd_ref, tfr_ref, tfi_ref, s_re, s_im, str_ref[0], sti_ref[0], tl)
        y = _s5_readout(s_re, s_im, cre_ref, cim_ref, tl) + d_ref[...] * u
        yg = _gelu(y)
        gate = _sigmoid(_mm(yg, wg_ref[...]) + bg_ref[...])
        sz, dsz = _silu_and_grad(z_ref[...])
        dya = dya_ref[...]
        s5out = yg * gate
        dp_ref[:, 1024:] = (dya * s5out * dsz).astype(dp_ref.dtype)
        ds5 = dya * sz
        dt = ds5 * yg * gate * (1.0 - gate)
        dwg_ref[...] += _mm_tn(yg, dt)
        dbg_ref[...] += jnp.sum(dt, axis=0, keepdims=True)
        dyv = (ds5 * gate + _mm_nt(dt, wg_ref[...])) * _gelu_grad(y)
        dd_ref[...] += jnp.sum(dyv * u, axis=0, keepdims=True)

        for k in range(S5_KBLK):
            lanes = slice(512 * k, 512 * (k + 1))
            dyk = dyv[:, 128 * k:128 * (k + 1)]
            g_re[:, lanes] = _mm_nt(dyk, cre_ref[k])
            g_im[:, lanes] = -_mm_nt(dyk, cim_ref[k])
            dcre_ref[k] += _mm_tn(s_re[pl.ds(8, tl), lanes], dyk)
            dcim_ref[k] -= _mm_tn(s_im[pl.ds(8, tl), lanes], dyk)

        row0 = lax.broadcasted_iota(jnp.int32, (8, _LANE_BLK), 0) == 0
        for b in range(S5_LANES // _LANE_BLK):
            lanes = slice(_LANE_BLK * b, _LANE_BLK * (b + 1))

            def tile(j, carry, lanes=lanes):
                cr, ci, ar, ai = carry
                i = tl // 8 - 1 - j
                r = pl.multiple_of(i * 8, 8)
                gr, gi = _scan_bwd_tile(g_re[pl.ds(r, 8), lanes], g_im[pl.ds(r, 8), lanes],
                                        tbr_ref, tbi_ref, lanes, cr, ci)
                g_re[pl.ds(r, 8), lanes] = gr
                g_im[pl.ds(r, 8), lanes] = gi
                pr, pi = s_re[pl.ds(r, 8), lanes], s_im[pl.ds(r, 8), lanes]
                qr, qi = s_re[pl.ds(r + 8, 8), lanes], s_im[pl.ds(r + 8, 8), lanes]
                sr = jnp.where(row0, jnp.broadcast_to(pr[7:8], qr.shape), pltpu.roll(qr, 1, 0))
                si = jnp.where(row0, jnp.broadcast_to(pi[7:8], qi.shape), pltpu.roll(qi, 1, 0))
                return gr[0:1], gi[0:1], ar + (sr * gr + si * gi), ai + (sr * gi - si * gr)

            z8 = jnp.zeros((8, _LANE_BLK), F32)
            cr, ci, ar, ai = lax.fori_loop(0, tl // 8, tile, (car_re[:, lanes], car_im[:, lanes], z8, z8))
            car_re[:, lanes] = cr
            car_im[:, lanes] = ci
            dabr_ref[:, lanes] += ar
            dabi_ref[:, lanes] += ai

        dus = []
        for k in range(S5_KBLK):
            lanes = slice(512 * k, 512 * (k + 1))
            g = jnp.concatenate([g_re[:, lanes], g_im[:, lanes]], axis=1)
            dwbd_ref[k] += _mm_tn(u[:, 128 * k:128 * (k + 1)], g)
            dus.append(_mm_nt(g, wbd_ref[k]))
        du = jnp.concatenate(dus, axis=1) + dyv * d_ref[...]
        dp_ref[:, :1024] = du.astype(dp_ref.dtype)

    shp = lambda *s: jax.ShapeDtypeStruct(s, F32)
    return _call(
        body, plan, name="s5_backward", grid=(nch,),
        in_specs=[pl.BlockSpec((tl, 1024), rev), pl.BlockSpec((tl, 1024), rev1), pl.BlockSpec((tl, 1024), rev),
                  pl.BlockSpec((1, 1, S5_LANES), rev3), pl.BlockSpec((1, 1, S5_LANES), rev3),
                  _full(wbd.shape), _full(cre.shape), _full(cim.shape), _full(tf_re.shape), _full(tf_im.shape),
                  _full(tb_re.shape), _full(tb_im.shape), _full((1, 1024)), _full((1024, 1024)), _full((1, 1024))],
        out_specs=[pl.BlockSpec((tl, 2048), rev), _full(wbd.shape), _full(cre.shape), _full(cim.shape),
                   _full((8, S5_LANES)), _full((8, S5_LANES)), _full((1, 1024)), _full((1024, 1024)), _full((1, 1024))],
        out_shape=[jax.ShapeDtypeStruct((L, 2048), MXU_DTYPE), shp(*wbd.shape), shp(*cre.shape), shp(*cim.shape),
                   shp(8, S5_LANES), shp(8, S5_LANES), shp(1, 1024), shp(1024, 1024), shp(1, 1024)],
        scratch_shapes=[pltpu.VMEM((tl + 8, S5_LANES), F32), pltpu.VMEM((tl + 8, S5_LANES), F32),
                        pltpu.VMEM((tl, S5_LANES), F32), pltpu.VMEM((tl, S5_LANES), F32),
                        pltpu.VMEM((1, S5_LANES), F32), pltpu.VMEM((1, S5_LANES), F32)],
        sem=("arbitrary",),
    )(p, p, dy, st_re, st_im, wbd, cre, cim, tf_re, tf_im, tb_re, tb_im, d_skip, w_glu, b_glu)


def _block_diag(w, rows_first):
    g8 = w.reshape(S5_KBLK, 8, w.shape[1], w.shape[2])
    eye = jnp.eye(8, dtype=w.dtype)
    out = jnp.einsum('kgab,fg->kfagb', g8, eye)
    return out.reshape(S5_KBLK, 8 * w.shape[1], 8 * w.shape[2])


def _block_diag_extract(wbd, a, b):
    w5 = wbd.reshape(S5_KBLK, 8, a, 8, b)
    idx = jnp.arange(8)
    return w5[:, idx, :, idx, :].transpose(1, 0, 2, 3).reshape(S5_GROUPS, a, b)


def _ret_constants():
    log_g = np.log1p(-np.exp2(-5.0 - np.arange(RET_HEADS, dtype=np.float32))).astype(np.float32)
    idx = np.arange(RET_CHUNK, dtype=np.float32)
    diff = idx[:, None] - idx[None, :]
    decay = np.where(diff >= 0, np.exp(log_g[:, None, None] * np.maximum(diff, 0.0)), 0.0).astype(np.float32)
    xi = np.exp(log_g[None, :] * (idx[:, None] + 1.0)).astype(np.float32)
    zeta = np.exp(log_g[None, :] * (RET_CHUNK - 1.0 - idx[:, None])).astype(np.float32)
    chunk_decay = np.exp(log_g * RET_CHUNK).astype(np.float32)
    return decay, xi, zeta, chunk_decay


def _rope_tables(L):
    half = RET_DK // 2
    inv = ROPE_BASE ** (-jnp.arange(half, dtype=F32) / half)
    ang = jnp.arange(L, dtype=F32)[:, None] * inv[None, :]
    return jnp.cos(ang), jnp.sin(ang)


def _rot(xh, cos, sin):
    x1, x2 = xh[:, :128], xh[:, 128:]
    return jnp.concatenate([x1 * cos - x2 * sin, x1 * sin + x2 * cos], axis=1)


def _rot_t(dh, cos, sin):
    d1, d2 = dh[:, :128], dh[:, 128:]
    return jnp.concatenate([d1 * cos + d2 * sin, d2 * cos - d1 * sin], axis=1)


def retention_forward(p, cos, sin, gain):
    L = p.shape[0]
    nc = L // RET_CHUNK
    decay_np, xi_np, zeta_np, cd_np = _ret_constants()
    decay, xi, zeta = jnp.asarray(decay_np), jnp.asarray(xi_np), jnp.asarray(zeta_np)
    scale = RET_DK ** -0.5

    def body(q_ref, k_ref, v_ref, z_ref, cos_ref, sin_ref, dec_ref, xi_ref, zeta_ref, gain_ref,
             yb_ref, prev_ref, state):
        @pl.when(pl.program_id(0) == 0)
        def _():
            state[...] = jnp.zeros_like(state)

        cs, sn = cos_ref[...], sin_ref[...]
        sz, _ = _silu_and_grad(z_ref[...])
        for h in range(RET_HEADS):
            hs = slice(RET_DK * h, RET_DK * (h + 1))
            qh = _rot(q_ref[:, hs], cs, sn)
            kh = _rot(k_ref[:, hs], cs, sn) * scale
            vh = v_ref[:, hs]
            prev = state[h]
            prev_ref[0, h] = prev.astype(prev_ref.dtype)
            sc = _mm_nt(qh, kh) * dec_ref[h]
            o = _mm(sc, vh) + _mm(qh * xi_ref[:, h:h + 1], prev)
            state[h] = prev * float(cd_np[h]) + _mm_tn(kh * zeta_ref[:, h:h + 1], vh)
            mu = jnp.mean(o, axis=-1, keepdims=True)
            oc = o - mu
            on = oc * lax.rsqrt(jnp.mean(oc * oc, axis=-1, keepdims=True) + NORM_EPS)
            yb_ref[:, hs] = (on * gain_ref[:, hs] * sz[:, hs]).astype(yb_ref.dtype)

    blk = lambda c: pl.BlockSpec((RET_CHUNK, 1024), lambda i, c=c: (i, c))
    return pl.pallas_call(
        body, name="retention_forward", grid=(nc,),
        in_specs=[blk(2), blk(3), blk(4), blk(5),
                  pl.BlockSpec((RET_CHUNK, 128), lambda i: (i, 0)), pl.BlockSpec((RET_CHUNK, 128), lambda i: (i, 0)),
                  _full(decay.shape), _full(xi.shape), _full(zeta.shape), _full((1, 1024))],
        out_specs=[pl.BlockSpec((RET_CHUNK, 1024), lambda i: (i, 0)),
                   pl.BlockSpec((1, RET_HEADS, RET_DK, RET_DK), lambda i: (i, 0, 0, 0))],
        out_shape=[jax.ShapeDtypeStruct((L, 1024), MXU_DTYPE),
                   jax.ShapeDtypeStruct((nc, RET_HEADS, RET_DK, RET_DK), MXU_DTYPE)],
        scratch_shapes=[pltpu.VMEM((RET_HEADS, RET_DK, RET_DK), F32)],
        compiler_params=_cparams(("arbitrary",)),
    )(p, p, p, p, cos, sin, decay, xi, zeta, gain)


def retention_backward(p, dy, prevs, cos, sin, gain, plan=None):
    L = p.shape[0]
    nc = L // RET_CHUNK
    decay_np, xi_np, zeta_np, cd_np = _ret_constants()
    decay, xi, zeta = jnp.asarray(decay_np), jnp.asarray(xi_np), jnp.asarray(zeta_np)
    scale = RET_DK ** -0.5

    def body(q_ref, k_ref, v_ref, z_ref, dyb_ref, prev_ref, cos_ref, sin_ref, dec_ref, xi_ref, zeta_ref, gain_ref,
             dp_ref, dgain_ref, dstate):
        @pl.when(pl.program_id(0) == 0)
        def _():
            dstate[...] = jnp.zeros_like(dstate)
            dgain_ref[...] = jnp.zeros_like(dgain_ref)

        cs, sn = cos_ref[...], sin_ref[...]
        sz, dsz = _silu_and_grad(z_ref[...])
        dyb = dyb_ref[...]
        for h in range(RET_HEADS):
            hs = slice(RET_DK * h, RET_DK * (h + 1))
            qh = _rot(q_ref[:, hs], cs, sn)
            kh = _rot(k_ref[:, hs], cs, sn) * scale
            vh = v_ref[:, hs]
            prev = prev_ref[0, h]
            xih, zth = xi_ref[:, h:h + 1], zeta_ref[:, h:h + 1]
            sc = _mm_nt(qh, kh) * dec_ref[h]
            o = _mm(sc, vh) + _mm(qh * xih, prev)
            mu = jnp.mean(o, axis=-1, keepdims=True)
            oc = o - mu
            rstd = lax.rsqrt(jnp.mean(oc * oc, axis=-1, keepdims=True) + NORM_EPS)
            on = oc * rstd
            gh = gain_ref[:, hs]
            dyh = dyb[:, hs]
            dp_ref[:, 3072 + RET_DK * h:3072 + RET_DK * (h + 1)] = (dyh * on * gh * dsz[:, hs]).astype(dp_ref.dtype)
            dong = dyh * sz[:, hs]
            dgain_ref[:, hs] += jnp.sum(dong * on, axis=0, keepdims=True)
            don = dong * gh
            do = rstd * (don - jnp.mean(don, axis=-1, keepdims=True)
                         - on * jnp.mean(don * on, axis=-1, keepdims=True))
            dst = dstate[h]
            dsc = _mm_nt(do, vh) * dec_ref[h]
            dqh = _mm(dsc, kh) + _mm_nt(do, prev) * xih
            dkh = _mm_tn(dsc, qh) + _mm_nt(vh, dst) * zth
            dvh = _mm_tn(sc, do) + _mm(kh * zth, dst)
            dstate[h] = dst * float(cd_np[h]) + _mm_tn(qh * xih, do)
            dp_ref[:, hs] = _rot_t(dqh, cs, sn).astype(dp_ref.dtype)
            dp_ref[:, 1024 + RET_DK * h:1024 + RET_DK * (h + 1)] = (_rot_t(dkh, cs, sn) * scale).astype(dp_ref.dtype)
            dp_ref[:, 2048 + RET_DK * h:2048 + RET_DK * (h + 1)] = dvh.astype(dp_ref.dtype)

    blk = lambda c: pl.BlockSpec((RET_CHUNK, 1024), lambda i, c=c: (nc - 1 - i, c))
    tab = pl.BlockSpec((RET_CHUNK, 128), lambda i: (nc - 1 - i, 0))
    return _call(
        body, plan, name="retention_backward", grid=(nc,),
        in_specs=[blk(2), blk(3), blk(4), blk(5), blk(1),
                  pl.BlockSpec((1, RET_HEADS, RET_DK, RET_DK), lambda i: (nc - 1 - i, 0, 0, 0)),
                  tab, tab, _full(decay.shape), _full(xi.shape), _full(zeta.shape), _full((1, 1024))],
        out_specs=[pl.BlockSpec((RET_CHUNK, 4096), lambda i: (nc - 1 - i, 0)), _full((1, 1024))],
        out_shape=[jax.ShapeDtypeStruct((L, 4096), MXU_DTYPE), jax.ShapeDtypeStruct((1, 1024), F32)],
        scratch_shapes=[pltpu.VMEM((RET_HEADS, RET_DK, RET_DK), F32)],
        sem=("arbitrary",),
    )(p, p, p, p, dy, prevs, cos, sin, decay, xi, zeta, gain)


def _sgu_mix(p_ref, gain_ref, wm_ref, bt_ref, tl):
    pu, pv, z = p_ref[:, :2048], p_ref[:, 2048:4096], p_ref[:, 4096:]
    u, v = _gelu(pu), _gelu(pv)
    mu = jnp.mean(v, axis=-1, keepdims=True)
    vc = v - mu
    rstd = lax.rsqrt(jnp.mean(vc * vc, axis=-1, keepdims=True) + NORM_EPS)
    vn = vc * rstd
    vg = vn * gain_ref[...]
    mask = (lax.broadcasted_iota(jnp.int32, (SGU_CHUNK, SGU_CHUNK), 0)
            >= lax.broadcasted_iota(jnp.int32, (SGU_CHUNK, SGU_CHUNK), 1))
    wms = [jnp.where(mask, wm_ref[g], 0.0) for g in range(SGU_GROUPS)]
    rows = []
    for c in range(tl // SGU_CHUNK):
        rs = slice(SGU_CHUNK * c, SGU_CHUNK * (c + 1))
        cols = []
        for g in range(SGU_GROUPS):
            gs = slice(SGU_GDIM * g, SGU_GDIM * (g + 1))
            cols.append(_mm(wms[g], vg[rs, gs]) + bt_ref[:, g:g + 1])
        rows.append(jnp.concatenate(cols, axis=1))
    s = rows[0] if len(rows) == 1 else jnp.concatenate(rows, axis=0)
    return pu, pv, z, u, vn, rstd, vg, wms, mask, s


def sgu_forward(p, gain, wm, bt):
    L = p.shape[0]
    tl = min(TL_SGU, L)

    def body(p_ref, gain_ref, wm_ref, bt_ref, y_ref):
        _, _, z, u, _, _, _, _, _, s = _sgu_mix(p_ref, gain_ref, wm_ref, bt_ref, tl)
        sz, _ = _silu_and_grad(z)
        y_ref[...] = (u * s * sz).astype(y_ref.dtype)

    return pl.pallas_call(
        body, name="sgu_forward", grid=(L // tl,),
        in_specs=[pl.BlockSpec((tl, ODD_IN), lambda i: (i, 0)), _full((1, 2048)), _full(wm.shape), _full(bt.shape)],
        out_specs=pl.BlockSpec((tl, 2048), lambda i: (i, 0)),
        out_shape=jax.ShapeDtypeStruct((L, 2048), MXU_DTYPE),
        compiler_params=_cparams(("arbitrary",)),
    )(p, gain, wm, bt)


def sgu_backward(p, dy, gain, wm, bt, plan=None):
    L = p.shape[0]
    tl = min(TL_SGU, L)

    def body(p_ref, dy_ref, gain_ref, wm_ref, bt_ref, dp_ref, dgain_ref, dwm_ref, dbt_ref):
        @pl.when(pl.program_id(0) == 0)
        def _():
            dgain_ref[...] = jnp.zeros_like(dgain_ref)
            dwm_ref[...] = jnp.zeros_like(dwm_ref)
            dbt_ref[...] = jnp.zeros_like(dbt_ref)

        pu, pv, z, u, vn, rstd, vg, wms, mask, s = _sgu_mix(p_ref, gain_ref, wm_ref, bt_ref, tl)
        sz, dsz = _silu_and_grad(z)
        dyv = dy_ref[...]
        dp_ref[:, 4096:] = (dyv * u * s * dsz).astype(dp_ref.dtype)
        dsg = dyv * sz
        dp_ref[:, :2048] = (dsg * s * _gelu_grad(pu)).astype(dp_ref.dtype)
        ds = dsg * u
        rows = []
        dbs = [jnp.zeros((SGU_CHUNK, 1), F32) for _ in range(SGU_GROUPS)]
        for c in range(tl // SGU_CHUNK):
            rs = slice(SGU_CHUNK * c, SGU_CHUNK * (c + 1))
            cols = []
            for g in range(SGU_GROUPS):
                gs = slice(SGU_GDIM * g, SGU_GDIM * (g + 1))
                dsg_c = ds[rs, gs]
                dbs[g] = dbs[g] + jnp.sum(dsg_c, axis=1, keepdims=True)
                dwm_ref[g] += jnp.where(mask, _mm_nt(dsg_c, vg[rs, gs]), 0.0)
                cols.append(_mm_tn(wms[g], dsg_c))
            rows.append(jnp.concatenate(cols, axis=1))
        dbt_ref[...] += jnp.concatenate(dbs, axis=1)
        dvg = rows[0] if len(rows) == 1 else jnp.concatenate(rows, axis=0)
        dgain_ref[...] += jnp.sum(dvg * vn, axis=0, keepdims=True)
        dvn = dvg * gain_ref[...]
        dv = rstd * (dvn - jnp.mean(dvn, axis=-1, keepdims=True) - vn * jnp.mean(dvn * vn, axis=-1, keepdims=True))
        dp_ref[:, 2048:4096] = (dv * _gelu_grad(pv)).astype(dp_ref.dtype)

    return _call(
        body, plan, name="sgu_backward", grid=(L // tl,),
        in_specs=[pl.BlockSpec((tl, ODD_IN), lambda i: (i, 0)), pl.BlockSpec((tl, 2048), lambda i: (i, 0)),
                  _full((1, 2048)), _full(wm.shape), _full(bt.shape)],
        out_specs=[pl.BlockSpec((tl, ODD_IN), lambda i: (i, 0)), _full((1, 2048)), _full(wm.shape), _full(bt.shape)],
        out_shape=[jax.ShapeDtypeStruct((L, ODD_IN), MXU_DTYPE), jax.ShapeDtypeStruct((1, 2048), F32),
                   jax.ShapeDtypeStruct(wm.shape, F32), jax.ShapeDtypeStruct(bt.shape, F32)],
        sem=("arbitrary",),
    )(p, dy, gain, wm, bt)


def cast_shards(mats):
    n = len(mats)

    def body(*refs):
        for p in range(n):
            refs[n + p][...] = refs[p][...].astype(MXU_DTYPE)

    return pl.pallas_call(
        body, name="cast_shards", out_shape=[jax.ShapeDtypeStruct(m.shape, MXU_DTYPE) for m in mats],
        compiler_params=pltpu.CompilerParams(vmem_limit_bytes=VMEM_LIMIT),
    )(*mats)


def local_grads(x, tgt, w):
    L = x.shape[0]
    ne, gf = w["norm_even"], w["final_norm"].reshape(1, D_MODEL)
    sh = dict(zip(MATRICES, cast_shards([w[n][0] for n in MATRICES])))
    (w_in_e,) = run_plan(gather_plan([sh["w_in_even"]]), "gather_w_in_even")
    lam_re, lam_im = w["s5_lam_re"][0], w["s5_lam_im"][0]
    log_dt = w["s5_log_dt"].reshape(S5_GROUPS, 1)
    bt_re = jnp.transpose(w["s5_b_re"][0], (2, 0, 1))
    bt_im = jnp.transpose(w["s5_b_im"][0], (2, 0, 1))
    c_re, c_im = w["s5_c_re"][0], w["s5_c_im"][0]
    wm = w["sgu_w_spatial"][0]
    bt = jnp.transpose(w["sgu_b_spatial"][0])

    ab_re, ab_im, bb_re, bb_im, pw_re, pw_im = s5_params_fwd(lam_re, lam_im, log_dt, bt_re, bt_im)
    tf_re, tf_im, tb_re, tb_im = _s5_scan_tables(pw_re.reshape(8, S5_LANES), pw_im.reshape(8, S5_LANES))
    wbd = jnp.concatenate([_block_diag(jnp.transpose(bb_re, (1, 0, 2)), True),
                           _block_diag(jnp.transpose(bb_im, (1, 0, 2)), True)], axis=2).astype(MXU_DTYPE)
    cre = _block_diag(jnp.transpose(c_re, (0, 2, 1)), True).astype(MXU_DTYPE)
    cim = _block_diag(jnp.transpose(c_im, (0, 2, 1)), True).astype(MXU_DTYPE)
    cos, sin = _rope_tables(L)

    p1, (w_glu, w_out_e) = norm_matmul(x, ne, w_in_e, "even_in", gather_plan([sh["s5_w_glu"], sh["w_out_even"]]))
    w_glu = w_glu.reshape(S5_WIDTH, S5_WIDTH)
    w_out_e = w_out_e.reshape(2 * S5_WIDTH, D_MODEL)
    (ya, st_re, st_im), (w_in_o, w_out_o, no, sg_gain) = s5_forward(
        p1, wbd, cre, cim, tf_re, tf_im, w["s5_d"], w_glu, w["s5_b_glu"],
        gather_plan([sh["w_in_odd"], sh["w_out_odd"], w["norm_odd"], w["sgu_norm_gain"]]))
    w_out_o = w_out_o.reshape(SGU_WIDTH, D_MODEL)
    no, sg_gain = no.reshape(1, D_MODEL), sg_gain.reshape(1, SGU_WIDTH)
    yb, prevs = retention_forward(p1, cos, sin, w["ret_gn_gain"])
    ycat = jnp.concatenate([ya, yb], axis=1)
    x1 = matmul_residual(ycat, w_out_e, x, "even_out")
    p2, _ = norm_matmul(x1, no, w_in_o, "odd_in")
    y2 = sgu_forward(p2, sg_gain, wm, bt)
    dx2, loss, dgf = out_proj_loss(y2, w_out_o, x1, gf, tgt, "odd_out_loss")

    g, landed = {}, {}
    shard_major = lambda a, n: a.reshape((N_CHIPS,) + w[n].shape[1:])
    dy2, g_w_out_o = out_proj_bwd(dx2, w_out_o, y2, "odd_out_bwd")
    (dp2, g["sgu_norm_gain"], dwm, dbt), (landed["w_out_odd"],) = sgu_backward(
        p2, dy2, sg_gain, wm, bt, reduce_plan([shard_major(g_w_out_o, "w_out_odd")]))
    g_w_in_o = in_proj_bwd_dw(x1, no, dp2, "odd_in_dw")
    (dx1, g["norm_odd"]), _ = in_proj_bwd_dx(x1, no, dp2, w_in_o, dx2, "odd_in_dx")
    dycat, g_w_out_e = out_proj_bwd(dx1, w_out_e, ycat, "even_out_bwd")
    ((dpa, dwbd, dcre, dcim, dab_re, dab_im, g["s5_d"], g_w_glu, g["s5_b_glu"]),
     (landed["w_in_odd"], landed["w_out_even"])) = s5_backward(
        p1, dycat, st_re, st_im, wbd, cre, cim, tf_re, tf_im, tb_re, tb_im, w["s5_d"], w_glu, w["s5_b_glu"],
        reduce_plan([g_w_in_o, shard_major(g_w_out_e, "w_out_even")]))
    (dpb, g["ret_gn_gain"]), (landed["s5_w_glu"],) = retention_backward(
        p1, dycat, prevs, cos, sin, w["ret_gn_gain"], reduce_plan([shard_major(g_w_glu, "s5_w_glu")]))
    dp1 = jnp.concatenate([dpa, dpb], axis=1)
    g_w_in_e = in_proj_bwd_dw(x, ne, dp1, "even_in_dw")
    (dx0, g["norm_even"]), (landed["w_in_even"],) = in_proj_bwd_dx(
        x, ne, dp1, w_in_e, dx1, "even_in_dx", reduce_plan([g_w_in_e]))

    dbb_re = jnp.transpose(_block_diag_extract(dwbd[:, :, :512], S5_GROUP, S5_STATE), (1, 0, 2))
    dbb_im = jnp.transpose(_block_diag_extract(dwbd[:, :, 512:], S5_GROUP, S5_STATE), (1, 0, 2))
    dlr, dli, ddt, dbt_re, dbt_im = s5_params_bwd(
        lam_re, lam_im, log_dt, bt_re, bt_im, dab_re.reshape(8, S5_GROUPS, S5_STATE),
        dab_im.reshape(8, S5_GROUPS, S5_STATE), dbb_re, dbb_im)
    g["s5_lam_re"], g["s5_lam_im"] = dlr[None], dli[None]
    g["s5_log_dt"] = ddt.reshape(1, S5_GROUPS)
    g["s5_b_re"] = jnp.transpose(dbt_re, (1, 2, 0))[None]
    g["s5_b_im"] = jnp.transpose(dbt_im, (1, 2, 0))[None]
    g["s5_c_re"] = jnp.transpose(_block_diag_extract(dcre, S5_STATE, S5_GROUP), (0, 2, 1))[None]
    g["s5_c_im"] = jnp.transpose(_block_diag_extract(dcim, S5_STATE, S5_GROUP), (0, 2, 1))[None]
    g["sgu_w_spatial"] = dwm[None]
    g["sgu_b_spatial"] = jnp.transpose(dbt)[None]
    g["final_norm"] = dgf.reshape(D_MODEL)
    return loss, dx0, g, landed


def sibling_exchange(arrs):
    n = len(arrs)

    def body(*refs):
        in_refs, out_refs = refs[:n], refs[n:2 * n]
        send_sems, recv_sems = refs[2 * n:]
        x, y, c = _place()
        copies = [pltpu.make_async_remote_copy(
            src_ref=in_refs[p], dst_ref=out_refs[p], send_sem=send_sems.at[p], recv_sem=recv_sems.at[p],
            device_id=(x, y, 1 - c), device_id_type=MESH) for p in range(n)]
        for cp in copies:
            cp.start()
        for cp in copies:
            cp.wait_recv()
        for cp in copies:
            cp.wait_send()

    return pl.pallas_call(
        body, name="sibling_exchange", in_specs=[ANY] * n, out_specs=[ANY] * n,
        out_shape=[jax.ShapeDtypeStruct(a.shape, a.dtype) for a in arrs],
        scratch_shapes=[pltpu.SemaphoreType.DMA((n,)), pltpu.SemaphoreType.DMA((n,))],
    )(*arrs)


def _row_block(rows):
    return 128 if rows % 128 == 0 else rows


def sum_slabs(r, name):
    _, R, C = r.shape
    tr = _row_block(R)

    def body(r_ref, o_ref):
        o_ref[...] = (r_ref[0] + r_ref[1]) + (r_ref[2] + r_ref[3])

    return pl.pallas_call(
        body, name=name, grid=(R // tr,),
        in_specs=[pl.BlockSpec((N_CHIPS, tr, C), lambda i: (0, i, 0))],
        out_specs=pl.BlockSpec((tr, C), lambda i: (i, 0)),
        out_shape=jax.ShapeDtypeStruct((R, C), F32),
        compiler_params=_cparams(("arbitrary",)),
    )(r)


def adam_update(w, m, v, ga, gb, name):
    R, C = w.shape
    tr = _row_block(R)
    gs = [ga] if gb is None else [ga, gb]

    def body(*refs):
        w_ref, m_ref, v_ref = refs[:3]
        g_refs = refs[3:3 + len(gs)]
        g_out, d_out, m_out, v_out = refs[3 + len(gs):]
        g = g_refs[0][...]
        if len(gs) == 2:
            g = g + g_refs[1][...]
        mn = ADAM_B1 * m_ref[...] + (1.0 - ADAM_B1) * g
        vn = ADAM_B2 * v_ref[...] + (1.0 - ADAM_B2) * (g * g)
        m_hat = mn / (1.0 - ADAM_B1 ** ADAM_STEP)
        v_hat = vn / (1.0 - ADAM_B2 ** ADAM_STEP)
        g_out[...] = g
        d_out[...] = -ADAM_LR * (m_hat / (jnp.sqrt(v_hat) + ADAM_EPS) + ADAM_WD * w_ref[...])
        m_out[...] = mn
        v_out[...] = vn

    blk = pl.BlockSpec((tr, C), lambda i: (i, 0))
    return pl.pallas_call(
        body, name=name, grid=(R // tr,),
        in_specs=[blk] * (3 + len(gs)), out_specs=[blk] * 4,
        out_shape=[jax.ShapeDtypeStruct((R, C), F32)] * 4,
        compiler_params=_cparams(("arbitrary",)),
    )(w, m, v, *gs)


WEIGHTS = ("norm_even", "w_in_even", "s5_lam_re", "s5_lam_im", "s5_log_dt", "s5_b_re", "s5_b_im", "s5_c_re",
           "s5_c_im", "s5_d", "s5_w_glu", "s5_b_glu", "ret_gn_gain", "w_out_even", "norm_odd", "w_in_odd",
           "sgu_norm_gain", "sgu_w_spatial", "sgu_b_spatial", "w_out_odd", "final_norm")
MATRICES = ("w_in_even", "s5_w_glu", "w_out_even", "w_in_odd", "w_out_odd")
SHARDED_VECS = ("norm_odd", "sgu_norm_gain")
REPLICATED = tuple(n for n in WEIGHTS if n not in MATRICES and n not in SHARDED_VECS)
PACKED = REPLICATED + SHARDED_VECS
LANES = 128


def _pack(parts, rows):
    flat = jnp.concatenate([p.reshape(-1) for p in parts])
    return jnp.pad(flat, (0, rows * LANES - flat.shape[0])).reshape(rows, LANES)


def _packed_rows(n_elems):
    return -(-n_elems // (8 * LANES)) * 8


def kernel(x, norm_even, w_in_even, s5_lam_re, s5_lam_im, s5_log_dt, s5_b_re, s5_b_im, s5_c_re, s5_c_im, s5_d, s5_w_glu, s5_b_glu, ret_gn_gain, w_out_even, norm_odd, w_in_odd, sgu_norm_gain, sgu_w_spatial, sgu_b_spatial, w_out_odd, final_norm, loss_target, m_norm_even, m_w_in_even, m_s5_lam_re, m_s5_lam_im, m_s5_log_dt, m_s5_b_re, m_s5_b_im, m_s5_c_re, m_s5_c_im, m_s5_d, m_s5_w_glu, m_s5_b_glu, m_ret_gn_gain, m_w_out_even, m_norm_odd, m_w_in_odd, m_sgu_norm_gain, m_sgu_w_spatial, m_sgu_b_spatial, m_w_out_odd, m_final_norm, v_norm_even, v_w_in_even, v_s5_lam_re, v_s5_lam_im, v_s5_log_dt, v_s5_b_re, v_s5_b_im, v_s5_c_re, v_s5_c_im, v_s5_d, v_s5_w_glu, v_s5_b_glu, v_ret_gn_gain, v_w_out_even, v_norm_odd, v_w_in_odd, v_sgu_norm_gain, v_sgu_w_spatial, v_sgu_b_spatial, v_w_out_odd, v_final_norm):
    w = dict(norm_even=norm_even, w_in_even=w_in_even, s5_lam_re=s5_lam_re, s5_lam_im=s5_lam_im, s5_log_dt=s5_log_dt, s5_b_re=s5_b_re, s5_b_im=s5_b_im, s5_c_re=s5_c_re, s5_c_im=s5_c_im, s5_d=s5_d, s5_w_glu=s5_w_glu, s5_b_glu=s5_b_glu, ret_gn_gain=ret_gn_gain, w_out_even=w_out_even, norm_odd=norm_odd, w_in_odd=w_in_odd, sgu_norm_gain=sgu_norm_gain, sgu_w_spatial=sgu_w_spatial, sgu_b_spatial=sgu_b_spatial, w_out_odd=w_out_odd, final_norm=final_norm)
    m = dict(norm_even=m_norm_even, w_in_even=m_w_in_even, s5_lam_re=m_s5_lam_re, s5_lam_im=m_s5_lam_im, s5_log_dt=m_s5_log_dt, s5_b_re=m_s5_b_re, s5_b_im=m_s5_b_im, s5_c_re=m_s5_c_re, s5_c_im=m_s5_c_im, s5_d=m_s5_d, s5_w_glu=m_s5_w_glu, s5_b_glu=m_s5_b_glu, ret_gn_gain=m_ret_gn_gain, w_out_even=m_w_out_even, norm_odd=m_norm_odd, w_in_odd=m_w_in_odd, sgu_norm_gain=m_sgu_norm_gain, sgu_w_spatial=m_sgu_w_spatial, sgu_b_spatial=m_sgu_b_spatial, w_out_odd=m_w_out_odd, final_norm=m_final_norm)
    v = dict(norm_even=v_norm_even, w_in_even=v_w_in_even, s5_lam_re=v_s5_lam_re, s5_lam_im=v_s5_lam_im, s5_log_dt=v_s5_log_dt, s5_b_re=v_s5_b_re, s5_b_im=v_s5_b_im, s5_c_re=v_s5_c_re, s5_c_im=v_s5_c_im, s5_d=v_s5_d, s5_w_glu=v_s5_w_glu, s5_b_glu=v_s5_b_glu, ret_gn_gain=v_ret_gn_gain, w_out_even=v_w_out_even, norm_odd=v_norm_odd, w_in_odd=v_w_in_odd, sgu_norm_gain=v_sgu_norm_gain, sgu_w_spatial=v_sgu_w_spatial, sgu_b_spatial=v_sgu_b_spatial, w_out_odd=v_w_out_odd, final_norm=v_final_norm)
    me = 2 * lax.axis_index("x") + lax.axis_index("y")

    loss, grad_x, g, landed = local_grads(x[0], loss_target[0], w)

    n_small = sum(int(np.prod(g[n].shape)) for n in PACKED) + 1
    rows = _packed_rows(n_small)
    packed = _pack([g[n] for n in PACKED] + [loss[0, :1]], rows)
    (landed["packed"],) = run_plan(reduce_plan([], [packed]), "exchange_packed")
    part = [sum_slabs(landed[n], "sum_" + n) for n in MATRICES + ("packed",)]
    other = sibling_exchange(part)

    out_g, out_d, out_m, out_v = {}, {}, {}, {}
    for k, n in enumerate(MATRICES):
        res = adam_update(w[n][0], m[n][0], v[n][0], part[k], other[k], "adam_" + n)
        out_g[n], out_d[n], out_m[n], out_v[n] = (r[None] for r in res)
    zeros = {n: jnp.zeros_like(g[n]) for n in SHARDED_VECS}
    res = adam_update(_pack([w[n] for n in REPLICATED] + [zeros[n] for n in SHARDED_VECS], rows),
                      _pack([m[n] for n in REPLICATED] + [zeros[n] for n in SHARDED_VECS], rows),
                      _pack([v[n] for n in REPLICATED] + [zeros[n] for n in SHARDED_VECS], rows),
                      part[-1], other[-1], "adam_packed")
    flat = [r.reshape(-1) for r in res]
    off = 0
    vec_grads = {}
    for n in PACKED:
        size = int(np.prod(g[n].shape))
        if n in REPLICATED:
            for dst, f in zip((out_g, out_d, out_m, out_v), flat):
                dst[n] = f[off:off + size].reshape(w[n].shape)
        else:
            vec_grads[n] = lax.dynamic_slice(flat[0], (off + me * w[n].shape[1],), (w[n].shape[1],))
        off += size
    total_loss = flat[0][off]
    vrows = _packed_rows(sum(w[n].shape[1] for n in SHARDED_VECS))
    res = adam_update(_pack([w[n] for n in SHARDED_VECS], vrows), _pack([m[n] for n in SHARDED_VECS], vrows),
                      _pack([v[n] for n in SHARDED_VECS], vrows), _pack([vec_grads[n] for n in SHARDED_VECS], vrows),
                      None, "adam_vecs")
    flat = [r.reshape(-1) for r in res]
    off = 0
    for n in SHARDED_VECS:
        size = w[n].shape[1]
        for dst, f in zip((out_g, out_d, out_m, out_v), flat):
            dst[n] = f[off:off + size].reshape(w[n].shape)
        off += size

    return (total_loss, grad_x[None], *[out_g[n] for n in WEIGHTS], *[out_d[n] for n in WEIGHTS],
            *[out_m[n] for n in WEIGHTS], *[out_v[n] for n in WEIGHTS])
```

```python
import functools
import math

import numpy as np
import jax
import jax.numpy as jnp
from jax import lax
from jax.experimental import pallas as pl
from jax.experimental.pallas import tpu as pltpu

F32 = jnp.float32
MXU_DTYPE = jnp.bfloat16
NORM_EPS = 1e-6
D_MODEL = 1024
S5_WIDTH = 1024
S5_GROUP = 16
S5_GROUPS = 64
S5_STATE = 64
S5_LANES = S5_GROUPS * S5_STATE
S5_KBLK = 8
RET_HEADS = 4
RET_DK = 256
RET_CHUNK = 128
ROPE_BASE = 10000.0
SGU_WIDTH = 2048
SGU_GROUPS = 4
SGU_GDIM = 512
SGU_CHUNK = 128
EVEN_IN = 6144
ODD_IN = 6144
ADAM_LR = 0.001
ADAM_B1 = 0.9
ADAM_B2 = 0.999
ADAM_EPS = 1e-08
ADAM_WD = 0.01
ADAM_STEP = 10
N_CHIPS = 4
VMEM_LIMIT = 56 * 1024 * 1024

TL_PROJ = 512
TL_S5 = 128
TL_SGU = 128


def _cparams(sem, **kw):
    return pltpu.CompilerParams(dimension_semantics=sem, vmem_limit_bytes=VMEM_LIMIT, **kw)


def _mm(a, b):
    return jnp.dot(a.astype(MXU_DTYPE), b.astype(MXU_DTYPE), preferred_element_type=F32)


def _mm_nt(a, b):
    return lax.dot_general(a.astype(MXU_DTYPE), b.astype(MXU_DTYPE),
                           (((1,), (1,)), ((), ())), preferred_element_type=F32)


def _mm_tn(a, b):
    return lax.dot_general(a.astype(MXU_DTYPE), b.astype(MXU_DTYPE),
                           (((0,), (0,)), ((), ())), preferred_element_type=F32)


_GELU_C = math.sqrt(2.0 / math.pi)


def _gelu(x):
    return 0.5 * x * (1.0 + jnp.tanh(_GELU_C * (x + 0.044715 * x * x * x)))


def _gelu_grad(x):
    th = jnp.tanh(_GELU_C * (x + 0.044715 * x * x * x))
    return 0.5 * (1.0 + th) + 0.5 * x * (1.0 - th * th) * _GELU_C * (1.0 + 3.0 * 0.044715 * x * x)


def _sigmoid(x):
    return 1.0 / (1.0 + jnp.exp(-x))


def _silu_and_grad(x):
    s = _sigmoid(x)
    return x * s, s * (1.0 + x * (1.0 - s))


def _rms(x):
    return lax.rsqrt(jnp.mean(x * x, axis=-1, keepdims=True) + NORM_EPS)


def _full(shape):
    nd = len(shape)
    return pl.BlockSpec(shape, lambda *_: (0,) * nd)


MESH = pl.DeviceIdType.MESH
ANY = pl.BlockSpec(memory_space=pl.ANY)


def _place():
    return lax.axis_index("x"), lax.axis_index("y"), lax.axis_index("c")


def _chip_peer(x, y, c, d):
    return (1 - x if d >= 2 else x, 1 - y if d % 2 else y, c)


class _Plan:
    def __init__(self, inputs, out_shape, build):
        self.inputs, self.out_shape, self._build = list(inputs), list(out_shape), build
        n = len(self.inputs)
        self.sems = [pltpu.SemaphoreType.DMA((n, 3)), pltpu.SemaphoreType.DMA((n, 3)), pltpu.SemaphoreType.DMA((n,))]

    def start(self, in_refs, out_refs, sems):
        send, recv, local = self._build(in_refs, out_refs, sems)
        for p in range(len(self.inputs)):
            local[p].start()
            for cp in send[p]:
                cp.start()

    def wait(self, in_refs, out_refs, sems):
        send, recv, local = self._build(in_refs, out_refs, sems)
        for p in range(len(self.inputs)):
            for cp in recv[p]:
                cp.wait_recv()
        for p in range(len(self.inputs)):
            for cp in send[p]:
                cp.wait_send()
            local[p].wait()


def gather_plan(shards):
    def build(in_refs, out_refs, sems):
        send_sems, recv_sems, loc_sems = sems
        x, y, c = _place()
        me = 2 * x + y

        def remote(p, d, slab):
            return pltpu.make_async_remote_copy(
                src_ref=in_refs[p], dst_ref=out_refs[p].at[slab], send_sem=send_sems.at[p, d - 1],
                recv_sem=recv_sems.at[p, d - 1], device_id=_chip_peer(x, y, c, d), device_id_type=MESH)

        n = len(in_refs)
        send = [[remote(p, d, me) for d in (1, 2, 3)] for p in range(n)]
        recv = [[remote(p, d, me ^ d) for d in (1, 2, 3)] for p in range(n)]
        local = [pltpu.make_async_copy(in_refs[p], out_refs[p].at[me], loc_sems.at[p]) for p in range(n)]
        return send, recv, local

    return _Plan(shards, [jax.ShapeDtypeStruct((N_CHIPS,) + s.shape, s.dtype) for s in shards], build)


def reduce_plan(shards, whole=()):
    n_s = len(shards)

    def build(in_refs, out_refs, sems):
        send_sems, recv_sems, loc_sems = sems
        x, y, c = _place()
        me = 2 * x + y

        def src(p, slab):
            return in_refs[p].at[slab] if p < n_s else in_refs[p]

        def remote(p, d):
            return pltpu.make_async_remote_copy(
                src_ref=src(p, me ^ d), dst_ref=out_refs[p].at[d], send_sem=send_sems.at[p, d - 1],
                recv_sem=recv_sems.at[p, d - 1], device_id=_chip_peer(x, y, c, d), device_id_type=MESH)

        n = len(in_refs)
        send = [[remote(p, d) for d in (1, 2, 3)] for p in range(n)]
        local = [pltpu.make_async_copy(src(p, me), out_refs[p].at[0], loc_sems.at[p]) for p in range(n)]
        return send, send, local

    outs = [jax.ShapeDtypeStruct(s.shape, s.dtype) for s in shards]
    outs += [jax.ShapeDtypeStruct((N_CHIPS,) + a.shape, a.dtype) for a in whole]
    return _Plan(list(shards) + list(whole), outs, build)


def run_plan(plan, name):
    n = len(plan.inputs)

    def body(*refs):
        plan.start(refs[:n], refs[n:2 * n], refs[2 * n:])
        plan.wait(refs[:n], refs[n:2 * n], refs[2 * n:])

    return pl.pallas_call(body, name=name, in_specs=[ANY] * n, out_specs=[ANY] * n, out_shape=plan.out_shape,
                          scratch_shapes=plan.sems)(*plan.inputs)


def _call(body, plan, *, name, grid, in_specs, out_specs, out_shape, sem, scratch_shapes=()):
    single = not isinstance(out_shape, (list, tuple))
    out_specs = [out_specs] if single else list(out_specs)
    out_shape = [out_shape] if single else list(out_shape)
    n_in, n_out, n_scr = len(in_specs), len(out_specs), len(scratch_shapes)
    ci = 0 if plan is None else len(plan.inputs)

    def hosted(*refs):
        ins, cins = refs[:n_in], refs[n_in:n_in + ci]
        k = n_in + ci
        outs, couts = refs[k:k + n_out], refs[k + n_out:k + n_out + ci]
        k += n_out + ci
        scr, sems = refs[k:k + n_scr], refs[k + n_scr:]
        ids = [pl.program_id(a) for a in range(len(grid))]
        first = functools.reduce(jnp.logical_and, [i == 0 for i in ids])
        last = functools.reduce(jnp.logical_and, [i == g - 1 for i, g in zip(ids, grid)])

        @pl.when(first)
        def _():
            plan.start(cins, couts, sems)

        body(*ins, *outs, *scr)

        @pl.when(last)
        def _():
            plan.wait(cins, couts, sems)

    def run(*args):
        if plan is None:
            res = pl.pallas_call(body, name=name, grid=grid, in_specs=list(in_specs), out_specs=out_specs,
                                 out_shape=out_shape, scratch_shapes=list(scratch_shapes),
                                 compiler_params=_cparams(sem))(*args)
            return (res[0] if single else res), []
        res = pl.pallas_call(hosted, name=name, grid=grid, in_specs=list(in_specs) + [ANY] * ci,
                             out_specs=out_specs + [ANY] * ci, out_shape=out_shape + plan.out_shape,
                             scratch_shapes=list(scratch_shapes) + plan.sems,
                             compiler_params=_cparams(sem))(*args, *plan.inputs)
        return (res[0] if single else res[:n_out]), list(res[n_out:])

    return run


def norm_matmul(x, g, w, name, plan=None, tn=None, tiles=None):
    L, D = x.shape
    nb, _, wb = w.shape
    tn = wb if tn is None else tn
    per = wb // tn
    first, count = (0, nb * per) if tiles is None else tiles
    tl = min(TL_PROJ, L)

    def body(x_ref, g_ref, w_ref, o_ref):
        xv = x_ref[...]
        o_ref[...] = _mm(xv * _rms(xv) * g_ref[...], w_ref[0])

    return _call(
        body, plan, name=name, grid=(count, L // tl),
        in_specs=[pl.BlockSpec((tl, D), lambda n, i: (i, 0)), _full((1, D)),
                  pl.BlockSpec((1, D, tn), lambda n, i: ((n + first) // per, 0, (n + first) % per))],
        out_specs=pl.BlockSpec((tl, tn), lambda n, i: (i, n)),
        out_shape=jax.ShapeDtypeStruct((L, count * tn), F32),
        sem=("arbitrary", "arbitrary"),
    )(x, g, w)


def matmul_residual(y, w, x, name):
    L, K = y.shape
    D = w.shape[1]
    tl = min(TL_PROJ, L)

    def body(y_ref, w_ref, x_ref, o_ref):
        o_ref[...] = x_ref[...] + _mm(y_ref[...], w_ref[...])

    return pl.pallas_call(
        body, name=name, grid=(L // tl,),
        in_specs=[pl.BlockSpec((tl, K), lambda i: (i, 0)), _full((K, D)),
                  pl.BlockSpec((tl, D), lambda i: (i, 0))],
        out_specs=pl.BlockSpec((tl, D), lambda i: (i, 0)),
        out_shape=jax.ShapeDtypeStruct((L, D), F32),
        compiler_params=_cparams(("arbitrary",)),
    )(y, w, x)


def out_proj_loss(y, w, x, gf, tgt, name):
    L, K = y.shape
    D = w.shape[1]
    tl = min(TL_PROJ, L)

    def body(y_ref, w_ref, x_ref, gf_ref, t_ref, dx_ref, loss_ref, dg_ref):
        @pl.when(pl.program_id(0) == 0)
        def _():
            loss_ref[...] = jnp.zeros_like(loss_ref)
            dg_ref[...] = jnp.zeros_like(dg_ref)

        x2 = x_ref[...] + _mm(y_ref[...], w_ref[...])
        r = _rms(x2)
        xn = x2 * r
        e = xn * gf_ref[...] - t_ref[...]
        loss_ref[...] += (0.5 / D) * jnp.sum(e * e)
        dout = e * (1.0 / D)
        dg_ref[...] += jnp.sum(dout * xn, axis=0, keepdims=True)
        dxn = dout * gf_ref[...]
        dx_ref[...] = r * (dxn - xn * jnp.mean(dxn * xn, axis=-1, keepdims=True))

    return pl.pallas_call(
        body, name=name, grid=(L // tl,),
        in_specs=[pl.BlockSpec((tl, K), lambda i: (i, 0)), _full((K, D)),
                  pl.BlockSpec((tl, D), lambda i: (i, 0)), _full((1, D)),
                  pl.BlockSpec((tl, D), lambda i: (i, 0))],
        out_specs=[pl.BlockSpec((tl, D), lambda i: (i, 0)), _full((8, 128)), _full((1, D))],
        out_shape=[jax.ShapeDtypeStruct((L, D), F32), jax.ShapeDtypeStruct((8, 128), F32),
                   jax.ShapeDtypeStruct((1, D), F32)],
        compiler_params=_cparams(("arbitrary",)),
    )(y, w, x, gf, tgt)


def out_proj_bwd(dx, w, y, name):
    L, D = dx.shape
    K = w.shape[0]
    tl = min(TL_PROJ, L)

    def body(dx_ref, w_ref, y_ref, dy_ref, dw_ref):
        @pl.when(pl.program_id(0) == 0)
        def _():
            dw_ref[...] = jnp.zeros_like(dw_ref)

        dxv = dx_ref[...]
        dy_ref[...] = _mm_nt(dxv, w_ref[...])
        dw_ref[...] += _mm_tn(y_ref[...], dxv)

    return pl.pallas_call(
        body, name=name, grid=(L // tl,),
        in_specs=[pl.BlockSpec((tl, D), lambda i: (i, 0)), _full((K, D)),
                  pl.BlockSpec((tl, K), lambda i: (i, 0))],
        out_specs=[pl.BlockSpec((tl, K), lambda i: (i, 0)), _full((K, D))],
        out_shape=[jax.ShapeDtypeStruct((L, K), F32), jax.ShapeDtypeStruct((K, D), F32)],
        compiler_params=_cparams(("arbitrary",)),
    )(dx, w, y)


def in_proj_bwd_dx(x, g, dp, w, dres, name, plan=None):
    L, D = x.shape
    nb, _, tn = w.shape
    tl = min(TL_PROJ, L)

    def body(x_ref, g_ref, dres_ref, dp_ref, w_ref, dx_ref, dg_ref):
        @pl.when(pl.program_id(0) == 0)
        def _():
            dg_ref[...] = jnp.zeros_like(dg_ref)

        dh = _mm_nt(dp_ref[:, :tn], w_ref[0])
        for k in range(1, nb):
            dh = dh + _mm_nt(dp_ref[:, tn * k:tn * (k + 1)], w_ref[k])
        xv = x_ref[...]
        r = _rms(xv)
        xn = xv * r
        dg_ref[...] += jnp.sum(dh * xn, axis=0, keepdims=True)
        dxn = dh * g_ref[...]
        dx_ref[...] = dres_ref[...] + r * (dxn - xn * jnp.mean(dxn * xn, axis=-1, keepdims=True))

    return _call(
        body, plan, name=name, grid=(L // tl,),
        in_specs=[pl.BlockSpec((tl, D), lambda i: (i, 0)), _full((1, D)), pl.BlockSpec((tl, D), lambda i: (i, 0)),
                  pl.BlockSpec((tl, nb * tn), lambda i: (i, 0)), _full(w.shape)],
        out_specs=[pl.BlockSpec((tl, D), lambda i: (i, 0)), _full((1, D))],
        out_shape=[jax.ShapeDtypeStruct((L, D), F32), jax.ShapeDtypeStruct((1, D), F32)],
        sem=("arbitrary",),
    )(x, g, dres, dp, w)


def in_proj_bwd_dw(x, g, dp, name):
    L, D = x.shape
    tn = dp.shape[1] // N_CHIPS
    tl = min(TL_PROJ, L)

    def body(x_ref, g_ref, dp_ref, dw_ref):
        @pl.when(pl.program_id(1) == 0)
        def _():
            dw_ref[...] = jnp.zeros_like(dw_ref)

        xv = x_ref[...]
        dw_ref[0] += _mm_tn(xv * _rms(xv) * g_ref[...], dp_ref[...])

    return pl.pallas_call(
        body, name=name, grid=(N_CHIPS, L // tl),
        in_specs=[pl.BlockSpec((tl, D), lambda n, i: (i, 0)), _full((1, D)),
                  pl.BlockSpec((tl, tn), lambda n, i: (i, n))],
        out_specs=pl.BlockSpec((1, D, tn), lambda n, i: (n, 0, 0)),
        out_shape=jax.ShapeDtypeStruct((N_CHIPS, D, tn), F32),
        compiler_params=_cparams(("arbitrary", "arbitrary")),
    )(x, g, dp)


def _s5_param_fn(lam_re, lam_im, log_dt, b_re, b_im):
    lr = jnp.minimum(lam_re, -1e-4)
    li = lam_im
    dt = jnp.exp(log_dt)
    mag = jnp.exp(lr * dt)
    ab_re = mag * jnp.cos(li * dt)
    ab_im = mag * jnp.sin(li * dt)
    den = lr * lr + li * li
    n_re = ab_re - 1.0
    n_im = ab_im
    z_re = (n_re * lr + n_im * li) / den
    z_im = (n_im * lr - n_re * li) / den
    bb_re = z_re[None] * b_re - z_im[None] * b_im
    bb_im = z_re[None] * b_im + z_im[None] * b_re
    return ab_re, ab_im, bb_re, bb_im


def s5_params_fwd(lam_re, lam_im, log_dt, b_re, b_im, span):
    G, P = lam_re.shape
    H = b_re.shape[0]
    assert span & (span - 1) == 0

    def body(lr_ref, li_ref, dt_ref, br_ref, bi_ref, abr_ref, abi_ref, bbr_ref, bbi_ref, pr_ref, pi_ref):
        ab_re, ab_im, bb_re, bb_im = _s5_param_fn(lr_ref[...], li_ref[...], dt_ref[...], br_ref[...], bi_ref[...])
        abr_ref[...] = ab_re
        abi_ref[...] = ab_im
        bbr_ref[...] = bb_re
        bbi_ref[...] = bb_im
        cr, ci = ab_re, ab_im
        for _ in range(span.bit_length() - 1):
            cr, ci = cr * cr - ci * ci, 2.0 * cr * ci
        pr_ref[...] = cr
        pi_ref[...] = ci

    shp = lambda *s: jax.ShapeDtypeStruct(s, F32)
    return pl.pallas_call(
        body, name="s5_params_fwd",
        out_shape=[shp(G, P), shp(G, P), shp(H, G, P), shp(H, G, P), shp(G, P), shp(G, P)],
    )(lam_re, lam_im, log_dt, b_re, b_im)


def s5_params_bwd(lam_re, lam_im, log_dt, b_re, b_im, d_ab_re, d_ab_im, d_bb_re, d_bb_im):
    G, P = lam_re.shape
    H = b_re.shape[0]

    def body(lr_ref, li_ref, dt_ref, br_ref, bi_ref, g0, g1, g2, g3, o0, o1, o2, o3, o4):
        prim = (lr_ref[...], li_ref[...], dt_ref[...], br_ref[...], bi_ref[...])
        _, vjp = jax.vjp(_s5_param_fn, *prim)
        d = vjp((jnp.sum(g0[...], axis=0), jnp.sum(g1[...], axis=0), g2[...], g3[...]))
        o0[...], o1[...], o2[...], o3[...], o4[...] = d

    shp = lambda *s: jax.ShapeDtypeStruct(s, F32)
    return pl.pallas_call(
        body, name="s5_params_bwd",
        out_shape=[shp(G, P), shp(G, P), shp(G, 1), shp(H, G, P), shp(H, G, P)],
    )(lam_re, lam_im, log_dt, b_re, b_im, d_ab_re, d_ab_im, d_bb_re, d_bb_im)


def stream_order(a, tl):
    L, C = a.shape
    return a.reshape(L // tl, 8, tl // 8, C).transpose(0, 2, 1, 3).reshape(L, C)


def token_order(a, tl):
    L, C = a.shape
    return a.reshape(L // tl, tl // 8, 8, C).transpose(0, 2, 1, 3).reshape(L, C)


_LANE_BLK = 1024


def _cmul_add(ar, ai, xr, xi, br, bi):
    return br + (ar * xr - ai * xi), bi + (ar * xi + ai * xr)


def _cmulc_add(ar, ai, xr, xi, br, bi):
    return br + (ar * xr + ai * xi), bi + (ar * xi - ai * xr)


def _s5_states(u, wbd_ref, a_re, a_im, at_re, at_im, s_re, s_im, e_re, e_im, c0_re, c0_im, tl):
    t8 = tl // 8
    for k in range(S5_KBLK):
        bu = _mm(u[:, 128 * k:128 * (k + 1)], wbd_ref[k])
        s_re[:, 512 * k:512 * (k + 1)] = bu[:, :512]
        s_im[:, 512 * k:512 * (k + 1)] = bu[:, 512:]
    outs_re, outs_im = [], []
    for b in range(S5_LANES // _LANE_BLK):
        lanes = slice(_LANE_BLK * b, _LANE_BLK * (b + 1))
        ar = jnp.broadcast_to(a_re[:, lanes], (8, _LANE_BLK))
        ai = jnp.broadcast_to(a_im[:, lanes], (8, _LANE_BLK))

        def local(i, carry, lanes=lanes, ar=ar, ai=ai):
            r = pl.multiple_of(i * 8, 8)
            sr, si = _cmul_add(ar, ai, carry[0], carry[1], s_re[pl.ds(r, 8), lanes], s_im[pl.ds(r, 8), lanes])
            s_re[pl.ds(r, 8), lanes] = sr
            s_im[pl.ds(r, 8), lanes] = si
            return sr, si

        zero = jnp.zeros((8, _LANE_BLK), F32)
        fr, fi = lax.fori_loop(0, t8, local, (zero, zero), unroll=True)
        tr, ti = at_re[:, lanes], at_im[:, lanes]
        er, ei = c0_re[:, lanes], c0_im[:, lanes]
        ers, eis = [er], [ei]
        for j in range(8):
            er, ei = _cmul_add(tr, ti, er, ei, fr[j:j + 1], fi[j:j + 1])
            ers.append(er)
            eis.append(ei)
        outs_re.append(ers[8])
        outs_im.append(eis[8])
        ent_r, ent_i = jnp.concatenate(ers[:8], axis=0), jnp.concatenate(eis[:8], axis=0)
        e_re[:, lanes] = ent_r
        e_im[:, lanes] = ent_i

        def fix(i, carry, lanes=lanes, ar=ar, ai=ai):
            r = pl.multiple_of(i * 8, 8)
            zr, zi = ar * carry[0] - ai * carry[1], ar * carry[1] + ai * carry[0]
            s_re[pl.ds(r, 8), lanes] = s_re[pl.ds(r, 8), lanes] + zr
            s_im[pl.ds(r, 8), lanes] = s_im[pl.ds(r, 8), lanes] + zi
            return zr, zi

        lax.fori_loop(0, t8, fix, (ent_r, ent_i), unroll=True)
    return jnp.concatenate(outs_re, axis=1), jnp.concatenate(outs_im, axis=1)


def _s5_readout(s_re, s_im, cre_ref, cim_ref):
    ys = []
    for k in range(S5_KBLK):
        lanes = slice(512 * k, 512 * (k + 1))
        ys.append(_mm(s_re[:, lanes], cre_ref[k]) - _mm(s_im[:, lanes], cim_ref[k]))
    return jnp.concatenate(ys, axis=1)


def s5_forward(p, wbd, cre, cim, atab, d_skip, w_glu, b_glu, plan=None):
    L = p.shape[0]
    tl = min(TL_S5, L)
    nch = L // tl

    def body(u_ref, z_ref, wbd_ref, cre_ref, cim_ref, at_ref, d_ref, wg_ref, bg_ref,
             ya_ref, st_re_ref, st_im_ref, s_re, s_im, e_re, e_im, car_re, car_im):
        @pl.when(pl.program_id(0) == 0)
        def _():
            car_re[...] = jnp.zeros_like(car_re)
            car_im[...] = jnp.zeros_like(car_im)

        c0_re, c0_im = car_re[...], car_im[...]
        st_re_ref[0] = c0_re
        st_im_ref[0] = c0_im
        u = u_ref[...]
        x_re, x_im = _s5_states(u, wbd_ref, at_ref[0:1], at_ref[1:2], at_ref[2:3], at_ref[3:4],
                                s_re, s_im, e_re, e_im, c0_re, c0_im, tl)
        car_re[...] = x_re
        car_im[...] = x_im
        y = _s5_readout(s_re, s_im, cre_ref, cim_ref) + d_ref[...] * u
        yg = _gelu(y)
        gate = _sigmoid(_mm(yg, wg_ref[...]) + bg_ref[...])
        sz, _ = _silu_and_grad(z_ref[...])
        ya_ref[...] = (yg * gate * sz).astype(ya_ref.dtype)

    return _call(
        body, plan, name="s5_forward", grid=(nch,),
        in_specs=[pl.BlockSpec((tl, 1024), lambda i: (i, 0)), pl.BlockSpec((tl, 1024), lambda i: (i, 1)),
                  _full(wbd.shape), _full(cre.shape), _full(cim.shape), _full(atab.shape),
                  _full((1, 1024)), _full((1024, 1024)), _full((1, 1024))],
        out_specs=[pl.BlockSpec((tl, 1024), lambda i: (i, 0)),
                   pl.BlockSpec((1, 1, S5_LANES), lambda i: (i, 0, 0)),
                   pl.BlockSpec((1, 1, S5_LANES), lambda i: (i, 0, 0))],
        out_shape=[jax.ShapeDtypeStruct((L, 1024), MXU_DTYPE),
                   jax.ShapeDtypeStruct((nch, 1, S5_LANES), F32), jax.ShapeDtypeStruct((nch, 1, S5_LANES), F32)],
        scratch_shapes=[pltpu.VMEM((tl, S5_LANES), F32), pltpu.VMEM((tl, S5_LANES), F32),
                        pltpu.VMEM((8, S5_LANES), F32), pltpu.VMEM((8, S5_LANES), F32),
                        pltpu.VMEM((1, S5_LANES), F32), pltpu.VMEM((1, S5_LANES), F32)],
        sem=("arbitrary",),
    )(p, p, wbd, cre, cim, atab, d_skip, w_glu, b_glu)


def s5_backward(p, dya, st_re, st_im, wbd, cre, cim, atab, d_skip, w_glu, b_glu, plan=None):
    L = p.shape[0]
    tl = min(TL_S5, L)
    t8 = tl // 8
    nch = L // tl
    rev = lambda i: (nch - 1 - i, 0)
    rev1 = lambda i: (nch - 1 - i, 1)
    rev3 = lambda i: (nch - 1 - i, 0, 0)

    def body(u_ref, z_ref, dya_ref, str_ref, sti_ref, wbd_ref, cre_ref, cim_ref, at_ref,
             d_ref, wg_ref, bg_ref,
             dp_ref, dwbd_ref, dcre_ref, dcim_ref, dabr_ref, dabi_ref, dd_ref, dwg_ref, dbg_ref,
             s_re, s_im, g_re, g_im, e_re, e_im, car_re, car_im):
        @pl.when(pl.program_id(0) == 0)
        def _():
            car_re[...] = jnp.zeros_like(car_re)
            car_im[...] = jnp.zeros_like(car_im)
            for r in (dwbd_ref, dcre_ref, dcim_ref, dabr_ref, dabi_ref, dd_ref, dwg_ref, dbg_ref):
                r[...] = jnp.zeros_like(r)

        u = u_ref[...]
        a_re, a_im, at_re, at_im = at_ref[0:1], at_ref[1:2], at_ref[2:3], at_ref[3:4]
        _s5_states(u, wbd_ref, a_re, a_im, at_re, at_im, s_re, s_im, e_re, e_im, str_ref[0], sti_ref[0], tl)
        y = _s5_readout(s_re, s_im, cre_ref, cim_ref) + d_ref[...] * u
        yg = _gelu(y)
        gate = _sigmoid(_mm(yg, wg_ref[...]) + bg_ref[...])
        sz, dsz = _silu_and_grad(z_ref[...])
        dya = dya_ref[...]
        s5out = yg * gate
        dp_ref[:, 1024:] = (dya * s5out * dsz).astype(dp_ref.dtype)
        ds5 = dya * sz
        dt = ds5 * yg * gate * (1.0 - gate)
        dwg_ref[...] += _mm_tn(yg, dt)
        dbg_ref[...] += jnp.sum(dt, axis=0, keepdims=True)
        dyv = (ds5 * gate + _mm_nt(dt, wg_ref[...])) * _gelu_grad(y)
        dd_ref[...] += jnp.sum(dyv * u, axis=0, keepdims=True)

        for k in range(S5_KBLK):
            lanes = slice(512 * k, 512 * (k + 1))
            dyk = dyv[:, 128 * k:128 * (k + 1)]
            g_re[:, lanes] = _mm_nt(dyk, cre_ref[k])
            g_im[:, lanes] = -_mm_nt(dyk, cim_ref[k])
            dcre_ref[k] += _mm_tn(s_re[:, lanes], dyk)
            dcim_ref[k] -= _mm_tn(s_im[:, lanes], dyk)

        for b in range(S5_LANES // _LANE_BLK):
            lanes = slice(_LANE_BLK * b, _LANE_BLK * (b + 1))
            ar = jnp.broadcast_to(a_re[:, lanes], (8, _LANE_BLK))
            ai = jnp.broadcast_to(a_im[:, lanes], (8, _LANE_BLK))

            def local(j, carry, lanes=lanes, ar=ar, ai=ai):
                r = pl.multiple_of((t8 - 1 - j) * 8, 8)
                gr, gi = _cmulc_add(ar, ai, carry[0], carry[1], g_re[pl.ds(r, 8), lanes], g_im[pl.ds(r, 8), lanes])
                g_re[pl.ds(r, 8), lanes] = gr
                g_im[pl.ds(r, 8), lanes] = gi
                return gr, gi

            zero = jnp.zeros((8, _LANE_BLK), F32)
            fr, fi = lax.fori_loop(0, t8, local, (zero, zero), unroll=True)
            tr, ti = at_re[:, lanes], at_im[:, lanes]
            hr, hi = car_re[:, lanes], car_im[:, lanes]
            hrs, his = [hr], [hi]
            for j in range(7, -1, -1):
                hr, hi = _cmulc_add(tr, ti, hr, hi, fr[j:j + 1], fi[j:j + 1])
                hrs.append(hr)
                his.append(hi)
            car_re[:, lanes] = hrs[8]
            car_im[:, lanes] = his[8]
            in_r = jnp.concatenate(hrs[7::-1], axis=0)
            in_i = jnp.concatenate(his[7::-1], axis=0)

            def fix(j, carry, lanes=lanes, ar=ar, ai=ai):
                wr, wi, accr, acci = carry
                r = pl.multiple_of((t8 - 1 - j) * 8, 8)
                wr, wi = ar * wr + ai * wi, ar * wi - ai * wr
                gr, gi = g_re[pl.ds(r, 8), lanes] + wr, g_im[pl.ds(r, 8), lanes] + wi
                g_re[pl.ds(r, 8), lanes] = gr
                g_im[pl.ds(r, 8), lanes] = gi
                sr, si = s_re[pl.ds(r - 8, 8), lanes], s_im[pl.ds(r - 8, 8), lanes]
                return wr, wi, accr + (sr * gr + si * gi), acci + (sr * gi - si * gr)

            wr, wi, accr, acci = lax.fori_loop(0, t8 - 1, fix, (in_r, in_i, zero, zero), unroll=True)
            wr, wi = ar * wr + ai * wi, ar * wi - ai * wr
            gr, gi = g_re[pl.ds(0, 8), lanes] + wr, g_im[pl.ds(0, 8), lanes] + wi
            g_re[pl.ds(0, 8), lanes] = gr
            g_im[pl.ds(0, 8), lanes] = gi
            sr, si = e_re[:, lanes], e_im[:, lanes]
            dabr_ref[:, lanes] += accr + (sr * gr + si * gi)
            dabi_ref[:, lanes] += acci + (sr * gi - si * gr)

        dus = []
        for k in range(S5_KBLK):
            lanes = slice(512 * k, 512 * (k + 1))
            g = jnp.concatenate([g_re[:, lanes], g_im[:, lanes]], axis=1)
            dwbd_ref[k] += _mm_tn(u[:, 128 * k:128 * (k + 1)], g)
            dus.append(_mm_nt(g, wbd_ref[k]))
        du = jnp.concatenate(dus, axis=1) + dyv * d_ref[...]
        dp_ref[:, :1024] = du.astype(dp_ref.dtype)

    shp = lambda *s: jax.ShapeDtypeStruct(s, F32)
    return _call(
        body, plan, name="s5_backward", grid=(nch,),
        in_specs=[pl.BlockSpec((tl, 1024), rev), pl.BlockSpec((tl, 1024), rev1), pl.BlockSpec((tl, 1024), rev),
                  pl.BlockSpec((1, 1, S5_LANES), rev3), pl.BlockSpec((1, 1, S5_LANES), rev3),
                  _full(wbd.shape), _full(cre.shape), _full(cim.shape), _full(atab.shape),
                  _full((1, 1024)), _full((1024, 1024)), _full((1, 1024))],
        out_specs=[pl.BlockSpec((tl, 2048), rev), _full(wbd.shape), _full(cre.shape), _full(cim.shape),
                   _full((8, S5_LANES)), _full((8, S5_LANES)), _full((1, 1024)), _full((1024, 1024)), _full((1, 1024))],
        out_shape=[jax.ShapeDtypeStruct((L, 2048), MXU_DTYPE), shp(*wbd.shape), shp(*cre.shape), shp(*cim.shape),
                   shp(8, S5_LANES), shp(8, S5_LANES), shp(1, 1024), shp(1024, 1024), shp(1, 1024)],
        scratch_shapes=[pltpu.VMEM((tl, S5_LANES), F32), pltpu.VMEM((tl, S5_LANES), F32),
                        pltpu.VMEM((tl, S5_LANES), F32), pltpu.VMEM((tl, S5_LANES), F32),
                        pltpu.VMEM((8, S5_LANES), F32), pltpu.VMEM((8, S5_LANES), F32),
                        pltpu.VMEM((1, S5_LANES), F32), pltpu.VMEM((1, S5_LANES), F32)],
        sem=("arbitrary",),
    )(p, p, dya, st_re, st_im, wbd, cre, cim, atab, d_skip, w_glu, b_glu)


def _block_diag(w, rows_first):
    g8 = w.reshape(S5_KBLK, 8, w.shape[1], w.shape[2])
    eye = jnp.eye(8, dtype=w.dtype)
    out = jnp.einsum('kgab,fg->kfagb', g8, eye)
    return out.reshape(S5_KBLK, 8 * w.shape[1], 8 * w.shape[2])


def _block_diag_extract(wbd, a, b):
    w5 = wbd.reshape(S5_KBLK, 8, a, 8, b)
    idx = jnp.arange(8)
    return w5[:, idx, :, idx, :].transpose(1, 0, 2, 3).reshape(S5_GROUPS, a, b)


def _ret_constants():
    log_g = np.log1p(-np.exp2(-5.0 - np.arange(RET_HEADS, dtype=np.float32))).astype(np.float32)
    idx = np.arange(RET_CHUNK, dtype=np.float32)
    diff = idx[:, None] - idx[None, :]
    decay = np.where(diff >= 0, np.exp(log_g[:, None, None] * np.maximum(diff, 0.0)), 0.0).astype(np.float32)
    xi = np.exp(log_g[None, :] * (idx[:, None] + 1.0)).astype(np.float32)
    zeta = np.exp(log_g[None, :] * (RET_CHUNK - 1.0 - idx[:, None])).astype(np.float32)
    chunk_decay = np.exp(log_g * RET_CHUNK).astype(np.float32)
    return decay, xi, zeta, chunk_decay


def _rope_tables(L):
    half = RET_DK // 2
    inv = ROPE_BASE ** (-jnp.arange(half, dtype=F32) / half)
    ang = jnp.arange(L, dtype=F32)[:, None] * inv[None, :]
    return jnp.cos(ang), jnp.sin(ang)


def _rot(xh, cos, sin):
    x1, x2 = xh[:, :128], xh[:, 128:]
    return jnp.concatenate([x1 * cos - x2 * sin, x1 * sin + x2 * cos], axis=1)


def _rot_t(dh, cos, sin):
    d1, d2 = dh[:, :128], dh[:, 128:]
    return jnp.concatenate([d1 * cos + d2 * sin, d2 * cos - d1 * sin], axis=1)


def retention_forward(p, cos, sin, gain):
    L = p.shape[0]
    nc = L // RET_CHUNK
    decay_np, xi_np, zeta_np, cd_np = _ret_constants()
    decay, xi, zeta = jnp.asarray(decay_np), jnp.asarray(xi_np), jnp.asarray(zeta_np)
    scale = RET_DK ** -0.5

    def body(q_ref, k_ref, v_ref, z_ref, cos_ref, sin_ref, dec_ref, xi_ref, zeta_ref, gain_ref,
             yb_ref, prev_ref, state):
        @pl.when(pl.program_id(0) == 0)
        def _():
            state[...] = jnp.zeros_like(state)

        cs, sn = cos_ref[...], sin_ref[...]
        sz, _ = _silu_and_grad(z_ref[...])
        for h in range(RET_HEADS):
            hs = slice(RET_DK * h, RET_DK * (h + 1))
            qh = _rot(q_ref[:, hs], cs, sn)
            kh = _rot(k_ref[:, hs], cs, sn) * scale
            vh = v_ref[:, hs]
            prev = state[h]
            prev_ref[0, h] = prev.astype(prev_ref.dtype)
            sc = _mm_nt(qh, kh) * dec_ref[h]
            o = _mm(sc, vh) + _mm(qh * xi_ref[:, h:h + 1], prev)
            state[h] = prev * float(cd_np[h]) + _mm_tn(kh * zeta_ref[:, h:h + 1], vh)
            mu = jnp.mean(o, axis=-1, keepdims=True)
            oc = o - mu
            on = oc * lax.rsqrt(jnp.mean(oc * oc, axis=-1, keepdims=True) + NORM_EPS)
            yb_ref[:, hs] = (on * gain_ref[:, hs] * sz[:, hs]).astype(yb_ref.dtype)

    blk = lambda c: pl.BlockSpec((RET_CHUNK, 1024), lambda i, c=c: (i, c))
    return pl.pallas_call(
        body, name="retention_forward", grid=(nc,),
        in_specs=[blk(0), blk(1), blk(2), blk(3),
                  pl.BlockSpec((RET_CHUNK, 128), lambda i: (i, 0)), pl.BlockSpec((RET_CHUNK, 128), lambda i: (i, 0)),
                  _full(decay.shape), _full(xi.shape), _full(zeta.shape), _full((1, 1024))],
        out_specs=[pl.BlockSpec((RET_CHUNK, 1024), lambda i: (i, 0)),
                   pl.BlockSpec((1, RET_HEADS, RET_DK, RET_DK), lambda i: (i, 0, 0, 0))],
        out_shape=[jax.ShapeDtypeStruct((L, 1024), MXU_DTYPE),
                   jax.ShapeDtypeStruct((nc, RET_HEADS, RET_DK, RET_DK), MXU_DTYPE)],
        scratch_shapes=[pltpu.VMEM((RET_HEADS, RET_DK, RET_DK), F32)],
        compiler_params=_cparams(("arbitrary",)),
    )(p, p, p, p, cos, sin, decay, xi, zeta, gain)


def retention_backward(p, dy, prevs, cos, sin, gain, plan=None):
    L = p.shape[0]
    nc = L // RET_CHUNK
    decay_np, xi_np, zeta_np, cd_np = _ret_constants()
    decay, xi, zeta = jnp.asarray(decay_np), jnp.asarray(xi_np), jnp.asarray(zeta_np)
    scale = RET_DK ** -0.5

    def body(q_ref, k_ref, v_ref, z_ref, dyb_ref, prev_ref, cos_ref, sin_ref, dec_ref, xi_ref, zeta_ref, gain_ref,
             dp_ref, dgain_ref, dstate):
        @pl.when(pl.program_id(0) == 0)
        def _():
            dstate[...] = jnp.zeros_like(dstate)
            dgain_ref[...] = jnp.zeros_like(dgain_ref)

        cs, sn = cos_ref[...], sin_ref[...]
        sz, dsz = _silu_and_grad(z_ref[...])
        dyb = dyb_ref[...]
        for h in range(RET_HEADS):
            hs = slice(RET_DK * h, RET_DK * (h + 1))
            qh = _rot(q_ref[:, hs], cs, sn)
            kh = _rot(k_ref[:, hs], cs, sn) * scale
            vh = v_ref[:, hs]
            prev = prev_ref[0, h]
            xih, zth = xi_ref[:, h:h + 1], zeta_ref[:, h:h + 1]
            sc = _mm_nt(qh, kh) * dec_ref[h]
            o = _mm(sc, vh) + _mm(qh * xih, prev)
            mu = jnp.mean(o, axis=-1, keepdims=True)
            oc = o - mu
            rstd = lax.rsqrt(jnp.mean(oc * oc, axis=-1, keepdims=True) + NORM_EPS)
            on = oc * rstd
            gh = gain_ref[:, hs]
            dyh = dyb[:, hs]
            dp_ref[:, 3072 + RET_DK * h:3072 + RET_DK * (h + 1)] = (dyh * on * gh * dsz[:, hs]).astype(dp_ref.dtype)
            dong = dyh * sz[:, hs]
            dgain_ref[:, hs] += jnp.sum(dong * on, axis=0, keepdims=True)
            don = dong * gh
            do = rstd * (don - jnp.mean(don, axis=-1, keepdims=True)
                         - on * jnp.mean(don * on, axis=-1, keepdims=True))
            dst = dstate[h]
            dsc = _mm_nt(do, vh) * dec_ref[h]
            dqh = _mm(dsc, kh) + _mm_nt(do, prev) * xih
            dkh = _mm_tn(dsc, qh) + _mm_nt(vh, dst) * zth
            dvh = _mm_tn(sc, do) + _mm(kh * zth, dst)
            dstate[h] = dst * float(cd_np[h]) + _mm_tn(qh * xih, do)
            dp_ref[:, hs] = _rot_t(dqh, cs, sn).astype(dp_ref.dtype)
            dp_ref[:, 1024 + RET_DK * h:1024 + RET_DK * (h + 1)] = (_rot_t(dkh, cs, sn) * scale).astype(dp_ref.dtype)
            dp_ref[:, 2048 + RET_DK * h:2048 + RET_DK * (h + 1)] = dvh.astype(dp_ref.dtype)

    blk = lambda c: pl.BlockSpec((RET_CHUNK, 1024), lambda i, c=c: (nc - 1 - i, c))
    tab = pl.BlockSpec((RET_CHUNK, 128), lambda i: (nc - 1 - i, 0))
    return _call(
        body, plan, name="retention_backward", grid=(nc,),
        in_specs=[blk(0), blk(1), blk(2), blk(3), blk(1),
                  pl.BlockSpec((1, RET_HEADS, RET_DK, RET_DK), lambda i: (nc - 1 - i, 0, 0, 0)),
                  tab, tab, _full(decay.shape), _full(xi.shape), _full(zeta.shape), _full((1, 1024))],
        out_specs=[pl.BlockSpec((RET_CHUNK, 4096), lambda i: (nc - 1 - i, 0)), _full((1, 1024))],
        out_shape=[jax.ShapeDtypeStruct((L, 4096), MXU_DTYPE), jax.ShapeDtypeStruct((1, 1024), F32)],
        scratch_shapes=[pltpu.VMEM((RET_HEADS, RET_DK, RET_DK), F32)],
        sem=("arbitrary",),
    )(p, p, p, p, dy, prevs, cos, sin, decay, xi, zeta, gain)


def _sgu_mix(p_ref, gain_ref, wm_ref, bt_ref, tl):
    pu, pv, z = p_ref[:, :2048], p_ref[:, 2048:4096], p_ref[:, 4096:]
    u, v = _gelu(pu), _gelu(pv)
    mu = jnp.mean(v, axis=-1, keepdims=True)
    vc = v - mu
    rstd = lax.rsqrt(jnp.mean(vc * vc, axis=-1, keepdims=True) + NORM_EPS)
    vn = vc * rstd
    vg = vn * gain_ref[...]
    mask = (lax.broadcasted_iota(jnp.int32, (SGU_CHUNK, SGU_CHUNK), 0)
            >= lax.broadcasted_iota(jnp.int32, (SGU_CHUNK, SGU_CHUNK), 1))
    wms = [jnp.where(mask, wm_ref[g], 0.0) for g in range(SGU_GROUPS)]
    rows = []
    for c in range(tl // SGU_CHUNK):
        rs = slice(SGU_CHUNK * c, SGU_CHUNK * (c + 1))
        cols = []
        for g in range(SGU_GROUPS):
            gs = slice(SGU_GDIM * g, SGU_GDIM * (g + 1))
            cols.append(_mm(wms[g], vg[rs, gs]) + bt_ref[:, g:g + 1])
        rows.append(jnp.concatenate(cols, axis=1))
    s = rows[0] if len(rows) == 1 else jnp.concatenate(rows, axis=0)
    return pu, pv, z, u, vn, rstd, vg, wms, mask, s


def sgu_forward(p, gain, wm, bt):
    L = p.shape[0]
    tl = min(TL_SGU, L)

    def body(p_ref, gain_ref, wm_ref, bt_ref, y_ref):
        _, _, z, u, _, _, _, _, _, s = _sgu_mix(p_ref, gain_ref, wm_ref, bt_ref, tl)
        sz, _ = _silu_and_grad(z)
        y_ref[...] = (u * s * sz).astype(y_ref.dtype)

    return pl.pallas_call(
        body, name="sgu_forward", grid=(L // tl,),
        in_specs=[pl.BlockSpec((tl, ODD_IN), lambda i: (i, 0)), _full((1, 2048)), _full(wm.shape), _full(bt.shape)],
        out_specs=pl.BlockSpec((tl, 2048), lambda i: (i, 0)),
        out_shape=jax.ShapeDtypeStruct((L, 2048), MXU_DTYPE),
        compiler_params=_cparams(("arbitrary",)),
    )(p, gain, wm, bt)


def sgu_backward(p, dy, gain, wm, bt, plan=None):
    L = p.shape[0]
    tl = min(TL_SGU, L)

    def body(p_ref, dy_ref, gain_ref, wm_ref, bt_ref, dp_ref, dgain_ref, dwm_ref, dbt_ref):
        @pl.when(pl.program_id(0) == 0)
        def _():
            dgain_ref[...] = jnp.zeros_like(dgain_ref)
            dwm_ref[...] = jnp.zeros_like(dwm_ref)
            dbt_ref[...] = jnp.zeros_like(dbt_ref)

        pu, pv, z, u, vn, rstd, vg, wms, mask, s = _sgu_mix(p_ref, gain_ref, wm_ref, bt_ref, tl)
        sz, dsz = _silu_and_grad(z)
        dyv = dy_ref[...]
        dp_ref[:, 4096:] = (dyv * u * s * dsz).astype(dp_ref.dtype)
        dsg = dyv * sz
        dp_ref[:, :2048] = (dsg * s * _gelu_grad(pu)).astype(dp_ref.dtype)
        ds = dsg * u
        rows = []
        dbs = [jnp.zeros((SGU_CHUNK, 1), F32) for _ in range(SGU_GROUPS)]
        for c in range(tl // SGU_CHUNK):
            rs = slice(SGU_CHUNK * c, SGU_CHUNK * (c + 1))
            cols = []
            for g in range(SGU_GROUPS):
                gs = slice(SGU_GDIM * g, SGU_GDIM * (g + 1))
                dsg_c = ds[rs, gs]
                dbs[g] = dbs[g] + jnp.sum(dsg_c, axis=1, keepdims=True)
                dwm_ref[g] += jnp.where(mask, _mm_nt(dsg_c, vg[rs, gs]), 0.0)
                cols.append(_mm_tn(wms[g], dsg_c))
            rows.append(jnp.concatenate(cols, axis=1))
        dbt_ref[...] += jnp.concatenate(dbs, axis=1)
        dvg = rows[0] if len(rows) == 1 else jnp.concatenate(rows, axis=0)
        dgain_ref[...] += jnp.sum(dvg * vn, axis=0, keepdims=True)
        dvn = dvg * gain_ref[...]
        dv = rstd * (dvn - jnp.mean(dvn, axis=-1, keepdims=True) - vn * jnp.mean(dvn * vn, axis=-1, keepdims=True))
        dp_ref[:, 2048:4096] = (dv * _gelu_grad(pv)).astype(dp_ref.dtype)

    return _call(
        body, plan, name="sgu_backward", grid=(L // tl,),
        in_specs=[pl.BlockSpec((tl, ODD_IN), lambda i: (i, 0)), pl.BlockSpec((tl, 2048), lambda i: (i, 0)),
                  _full((1, 2048)), _full(wm.shape), _full(bt.shape)],
        out_specs=[pl.BlockSpec((tl, ODD_IN), lambda i: (i, 0)), _full((1, 2048)), _full(wm.shape), _full(bt.shape)],
        out_shape=[jax.ShapeDtypeStruct((L, ODD_IN), MXU_DTYPE), jax.ShapeDtypeStruct((1, 2048), F32),
                   jax.ShapeDtypeStruct(wm.shape, F32), jax.ShapeDtypeStruct(bt.shape, F32)],
        sem=("arbitrary",),
    )(p, dy, gain, wm, bt)


def cast_shards(mats):
    n = len(mats)

    def body(*refs):
        for p in range(n):
            refs[n + p][...] = refs[p][...].astype(MXU_DTYPE)

    return pl.pallas_call(
        body, name="cast_shards", out_shape=[jax.ShapeDtypeStruct(m.shape, MXU_DTYPE) for m in mats],
        compiler_params=pltpu.CompilerParams(vmem_limit_bytes=VMEM_LIMIT),
    )(*mats)


def local_grads(x, tgt, w):
    L = x.shape[0]
    ne, gf = w["norm_even"], w["final_norm"].reshape(1, D_MODEL)
    sh = dict(zip(MATRICES, cast_shards([w[n][0] for n in MATRICES])))
    (w_in_e,) = run_plan(gather_plan([sh["w_in_even"]]), "gather_w_in_even")
    lam_re, lam_im = w["s5_lam_re"][0], w["s5_lam_im"][0]
    log_dt = w["s5_log_dt"].reshape(S5_GROUPS, 1)
    bt_re = jnp.transpose(w["s5_b_re"][0], (2, 0, 1))
    bt_im = jnp.transpose(w["s5_b_im"][0], (2, 0, 1))
    c_re, c_im = w["s5_c_re"][0], w["s5_c_im"][0]
    wm = w["sgu_w_spatial"][0]
    bt = jnp.transpose(w["sgu_b_spatial"][0])

    tl5 = min(TL_S5, L)
    ab_re, ab_im, bb_re, bb_im, at_re, at_im = s5_params_fwd(lam_re, lam_im, log_dt, bt_re, bt_im, tl5 // 8)
    atab = jnp.stack([ab_re.reshape(S5_LANES), ab_im.reshape(S5_LANES),
                      at_re.reshape(S5_LANES), at_im.reshape(S5_LANES)])
    wbd = jnp.concatenate([_block_diag(jnp.transpose(bb_re, (1, 0, 2)), True),
                           _block_diag(jnp.transpose(bb_im, (1, 0, 2)), True)], axis=2).astype(MXU_DTYPE)
    cre = _block_diag(jnp.transpose(c_re, (0, 2, 1)), True).astype(MXU_DTYPE)
    cim = _block_diag(jnp.transpose(c_im, (0, 2, 1)), True).astype(MXU_DTYPE)
    cos, sin = _rope_tables(L)

    p1a, (w_glu,) = norm_matmul(stream_order(x, tl5), ne, w_in_e, "even_in_s5", gather_plan([sh["s5_w_glu"]]),
                                tn=512, tiles=(0, 4))
    p1b, (w_out_e,) = norm_matmul(x, ne, w_in_e, "even_in_ret", gather_plan([sh["w_out_even"]]),
                                  tn=512, tiles=(4, 8))
    w_glu = w_glu.reshape(S5_WIDTH, S5_WIDTH)
    w_out_e = w_out_e.reshape(2 * S5_WIDTH, D_MODEL)
    (ya, st_re, st_im), (w_in_o, w_out_o, no, sg_gain) = s5_forward(
        p1a, wbd, cre, cim, atab, w["s5_d"], w_glu, w["s5_b_glu"],
        gather_plan([sh["w_in_odd"], sh["w_out_odd"], w["norm_odd"], w["sgu_norm_gain"]]))
    w_out_o = w_out_o.reshape(SGU_WIDTH, D_MODEL)
    no, sg_gain = no.reshape(1, D_MODEL), sg_gain.reshape(1, SGU_WIDTH)
    yb, prevs = retention_forward(p1b, cos, sin, w["ret_gn_gain"])
    ycat = jnp.concatenate([token_order(ya, tl5), yb], axis=1)
    x1 = matmul_residual(ycat, w_out_e, x, "even_out")
    p2, _ = norm_matmul(x1, no, w_in_o, "odd_in")
    y2 = sgu_forward(p2, sg_gain, wm, bt)
    dx2, loss, dgf = out_proj_loss(y2, w_out_o, x1, gf, tgt, "odd_out_loss")

    g, landed = {}, {}
    shard_major = lambda a, n: a.reshape((N_CHIPS,) + w[n].shape[1:])
    dy2, g_w_out_o = out_proj_bwd(dx2, w_out_o, y2, "odd_out_bwd")
    (dp2, g["sgu_norm_gain"], dwm, dbt), (landed["w_out_odd"],) = sgu_backward(
        p2, dy2, sg_gain, wm, bt, reduce_plan([shard_major(g_w_out_o, "w_out_odd")]))
    g_w_in_o = in_proj_bwd_dw(x1, no, dp2, "odd_in_dw")
    (dx1, g["norm_odd"]), _ = in_proj_bwd_dx(x1, no, dp2, w_in_o, dx2, "odd_in_dx")
    dycat, g_w_out_e = out_proj_bwd(dx1, w_out_e, ycat, "even_out_bwd")
    ((dpa, dwbd, dcre, dcim, dab_re, dab_im, g["s5_d"], g_w_glu, g["s5_b_glu"]),
     (landed["w_in_odd"], landed["w_out_even"])) = s5_backward(
        p1a, stream_order(dycat[:, :S5_WIDTH], tl5), st_re, st_im, wbd, cre, cim, atab, w["s5_d"], w_glu,
        w["s5_b_glu"], reduce_plan([g_w_in_o, shard_major(g_w_out_e, "w_out_even")]))
    (dpb, g["ret_gn_gain"]), (landed["s5_w_glu"],) = retention_backward(
        p1b, dycat, prevs, cos, sin, w["ret_gn_gain"], reduce_plan([shard_major(g_w_glu, "s5_w_glu")]))
    dp1 = jnp.concatenate([token_order(dpa, tl5), dpb], axis=1)
    g_w_in_e = in_proj_bwd_dw(x, ne, dp1, "even_in_dw")
    (dx0, g["norm_even"]), (landed["w_in_even"],) = in_proj_bwd_dx(
        x, ne, dp1, w_in_e, dx1, "even_in_dx", reduce_plan([g_w_in_e]))

    dbb_re = jnp.transpose(_block_diag_extract(dwbd[:, :, :512], S5_GROUP, S5_STATE), (1, 0, 2))
    dbb_im = jnp.transpose(_block_diag_extract(dwbd[:, :, 512:], S5_GROUP, S5_STATE), (1, 0, 2))
    dlr, dli, ddt, dbt_re, dbt_im = s5_params_bwd(
        lam_re, lam_im, log_dt, bt_re, bt_im, dab_re.reshape(8, S5_GROUPS, S5_STATE),
        dab_im.reshape(8, S5_GROUPS, S5_STATE), dbb_re, dbb_im)
    g["s5_lam_re"], g["s5_lam_im"] = dlr[None], dli[None]
    g["s5_log_dt"] = ddt.reshape(1, S5_GROUPS)
    g["s5_b_re"] = jnp.transpose(dbt_re, (1, 2, 0))[None]
    g["s5_b_im"] = jnp.transpose(dbt_im, (1, 2, 0))[None]
    g["s5_c_re"] = jnp.transpose(_block_diag_extract(dcre, S5_STATE, S5_GROUP), (0, 2, 1))[None]
    g["s5_c_im"] = jnp.transpose(_block_diag_extract(dcim, S5_STATE, S5_GROUP), (0, 2, 1))[None]
    g["sgu_w_spatial"] = dwm[None]
    g["sgu_b_spatial"] = jnp.transpose(dbt)[None]
    g["final_norm"] = dgf.reshape(D_MODEL)
    return loss, dx0, g, landed


def sibling_exchange(arrs):
    n = len(arrs)

    def body(*refs):
        in_refs, out_refs = refs[:n], refs[n:2 * n]
        send_sems, recv_sems = refs[2 * n:]
        x, y, c = _place()
        copies = [pltpu.make_async_remote_copy(
            src_ref=in_refs[p], dst_ref=out_refs[p], send_sem=send_sems.at[p], recv_sem=recv_sems.at[p],
            device_id=(x, y, 1 - c), device_id_type=MESH) for p in range(n)]
        for cp in copies:
            cp.start()
        for cp in copies:
            cp.wait_recv()
        for cp in copies:
            cp.wait_send()

    return pl.pallas_call(
        body, name="sibling_exchange", in_specs=[ANY] * n, out_specs=[ANY] * n,
        out_shape=[jax.ShapeDtypeStruct(a.shape, a.dtype) for a in arrs],
        scratch_shapes=[pltpu.SemaphoreType.DMA((n,)), pltpu.SemaphoreType.DMA((n,))],
    )(*arrs)


def _row_block(rows):
    return 128 if rows % 128 == 0 else rows


def sum_slabs(r, name):
    _, R, C = r.shape
    tr = _row_block(R)

    def body(r_ref, o_ref):
        o_ref[...] = (r_ref[0] + r_ref[1]) + (r_ref[2] + r_ref[3])

    return pl.pallas_call(
        body, name=name, grid=(R // tr,),
        in_specs=[pl.BlockSpec((N_CHIPS, tr, C), lambda i: (0, i, 0))],
        out_specs=pl.BlockSpec((tr, C), lambda i: (i, 0)),
        out_shape=jax.ShapeDtypeStruct((R, C), F32),
        compiler_params=_cparams(("arbitrary",)),
    )(r)


def adam_update(w, m, v, ga, gb, name):
    R, C = w.shape
    tr = _row_block(R)
    gs = [ga] if gb is None else [ga, gb]

    def body(*refs):
        w_ref, m_ref, v_ref = refs[:3]
        g_refs = refs[3:3 + len(gs)]
        g_out, d_out, m_out, v_out = refs[3 + len(gs):]
        g = g_refs[0][...]
        if len(gs) == 2:
            g = g + g_refs[1][...]
        mn = ADAM_B1 * m_ref[...] + (1.0 - ADAM_B1) * g
        vn = ADAM_B2 * v_ref[...] + (1.0 - ADAM_B2) * (g * g)
        m_hat = mn / (1.0 - ADAM_B1 ** ADAM_STEP)
        v_hat = vn / (1.0 - ADAM_B2 ** ADAM_STEP)
        g_out[...] = g
        d_out[...] = -ADAM_LR * (m_hat / (jnp.sqrt(v_hat) + ADAM_EPS) + ADAM_WD * w_ref[...])
        m_out[...] = mn
        v_out[...] = vn

    blk = pl.BlockSpec((tr, C), lambda i: (i, 0))
    return pl.pallas_call(
        body, name=name, grid=(R // tr,),
        in_specs=[blk] * (3 + len(gs)), out_specs=[blk] * 4,
        out_shape=[jax.ShapeDtypeStruct((R, C), F32)] * 4,
        compiler_params=_cparams(("arbitrary",)),
    )(w, m, v, *gs)


WEIGHTS = ("norm_even", "w_in_even", "s5_lam_re", "s5_lam_im", "s5_log_dt", "s5_b_re", "s5_b_im", "s5_c_re",
           "s5_c_im", "s5_d", "s5_w_glu", "s5_b_glu", "ret_gn_gain", "w_out_even", "norm_odd", "w_in_odd",
           "sgu_norm_gain", "sgu_w_spatial", "sgu_b_spatial", "w_out_odd", "final_norm")
MATRICES = ("w_in_even", "s5_w_glu", "w_out_even", "w_in_odd", "w_out_odd")
SHARDED_VECS = ("norm_odd", "sgu_norm_gain")
REPLICATED = tuple(n for n in WEIGHTS if n not in MATRICES and n not in SHARDED_VECS)
PACKED = REPLICATED + SHARDED_VECS
LANES = 128


def _pack(parts, rows):
    flat = jnp.concatenate([p.reshape(-1) for p in parts])
    return jnp.pad(flat, (0, rows * LANES - flat.shape[0])).reshape(rows, LANES)


def _packed_rows(n_elems):
    return -(-n_elems // (8 * LANES)) * 8


def kernel(x, norm_even, w_in_even, s5_lam_re, s5_lam_im, s5_log_dt, s5_b_re, s5_b_im, s5_c_re, s5_c_im, s5_d, s5_w_glu, s5_b_glu, ret_gn_gain, w_out_even, norm_odd, w_in_odd, sgu_norm_gain, sgu_w_spatial, sgu_b_spatial, w_out_odd, final_norm, loss_target, m_norm_even, m_w_in_even, m_s5_lam_re, m_s5_lam_im, m_s5_log_dt, m_s5_b_re, m_s5_b_im, m_s5_c_re, m_s5_c_im, m_s5_d, m_s5_w_glu, m_s5_b_glu, m_ret_gn_gain, m_w_out_even, m_norm_odd, m_w_in_odd, m_sgu_norm_gain, m_sgu_w_spatial, m_sgu_b_spatial, m_w_out_odd, m_final_norm, v_norm_even, v_w_in_even, v_s5_lam_re, v_s5_lam_im, v_s5_log_dt, v_s5_b_re, v_s5_b_im, v_s5_c_re, v_s5_c_im, v_s5_d, v_s5_w_glu, v_s5_b_glu, v_ret_gn_gain, v_w_out_even, v_norm_odd, v_w_in_odd, v_sgu_norm_gain, v_sgu_w_spatial, v_sgu_b_spatial, v_w_out_odd, v_final_norm):
    w = dict(norm_even=norm_even, w_in_even=w_in_even, s5_lam_re=s5_lam_re, s5_lam_im=s5_lam_im, s5_log_dt=s5_log_dt, s5_b_re=s5_b_re, s5_b_im=s5_b_im, s5_c_re=s5_c_re, s5_c_im=s5_c_im, s5_d=s5_d, s5_w_glu=s5_w_glu, s5_b_glu=s5_b_glu, ret_gn_gain=ret_gn_gain, w_out_even=w_out_even, norm_odd=norm_odd, w_in_odd=w_in_odd, sgu_norm_gain=sgu_norm_gain, sgu_w_spatial=sgu_w_spatial, sgu_b_spatial=sgu_b_spatial, w_out_odd=w_out_odd, final_norm=final_norm)
    m = dict(norm_even=m_norm_even, w_in_even=m_w_in_even, s5_lam_re=m_s5_lam_re, s5_lam_im=m_s5_lam_im, s5_log_dt=m_s5_log_dt, s5_b_re=m_s5_b_re, s5_b_im=m_s5_b_im, s5_c_re=m_s5_c_re, s5_c_im=m_s5_c_im, s5_d=m_s5_d, s5_w_glu=m_s5_w_glu, s5_b_glu=m_s5_b_glu, ret_gn_gain=m_ret_gn_gain, w_out_even=m_w_out_even, norm_odd=m_norm_odd, w_in_odd=m_w_in_odd, sgu_norm_gain=m_sgu_norm_gain, sgu_w_spatial=m_sgu_w_spatial, sgu_b_spatial=m_sgu_b_spatial, w_out_odd=m_w_out_odd, final_norm=m_final_norm)
    v = dict(norm_even=v_norm_even, w_in_even=v_w_in_even, s5_lam_re=v_s5_lam_re, s5_lam_im=v_s5_lam_im, s5_log_dt=v_s5_log_dt, s5_b_re=v_s5_b_re, s5_b_im=v_s5_b_im, s5_c_re=v_s5_c_re, s5_c_im=v_s5_c_im, s5_d=v_s5_d, s5_w_glu=v_s5_w_glu, s5_b_glu=v_s5_b_glu, ret_gn_gain=v_ret_gn_gain, w_out_even=v_w_out_even, norm_odd=v_norm_odd, w_in_odd=v_w_in_odd, sgu_norm_gain=v_sgu_norm_gain, sgu_w_spatial=v_sgu_w_spatial, sgu_b_spatial=v_sgu_b_spatial, w_out_odd=v_w_out_odd, final_norm=v_final_norm)
    me = 2 * lax.axis_index("x") + lax.axis_index("y")

    loss, grad_x, g, landed = local_grads(x[0], loss_target[0], w)

    n_small = sum(int(np.prod(g[n].shape)) for n in PACKED) + 1
    rows = _packed_rows(n_small)
    packed = _pack([g[n] for n in PACKED] + [loss[0, :1]], rows)
    (landed["packed"],) = run_plan(reduce_plan([], [packed]), "exchange_packed")
    part = [sum_slabs(landed[n], "sum_" + n) for n in MATRICES + ("packed",)]
    other = sibling_exchange(part)

    out_g, out_d, out_m, out_v = {}, {}, {}, {}
    for k, n in enumerate(MATRICES):
        res = adam_update(w[n][0], m[n][0], v[n][0], part[k], other[k], "adam_" + n)
        out_g[n], out_d[n], out_m[n], out_v[n] = (r[None] for r in res)
    zeros = {n: jnp.zeros_like(g[n]) for n in SHARDED_VECS}
    res = adam_update(_pack([w[n] for n in REPLICATED] + [zeros[n] for n in SHARDED_VECS], rows),
                      _pack([m[n] for n in REPLICATED] + [zeros[n] for n in SHARDED_VECS], rows),
                      _pack([v[n] for n in REPLICATED] + [zeros[n] for n in SHARDED_VECS], rows),
                      part[-1], other[-1], "adam_packed")
    flat = [r.reshape(-1) for r in res]
    off = 0
    vec_grads = {}
    for n in PACKED:
        size = int(np.prod(g[n].shape))
        if n in REPLICATED:
            for dst, f in zip((out_g, out_d, out_m, out_v), flat):
                dst[n] = f[off:off + size].reshape(w[n].shape)
        else:
            vec_grads[n] = lax.dynamic_slice(flat[0], (off + me * w[n].shape[1],), (w[n].shape[1],))
        off += size
    total_loss = flat[0][off]
    vrows = _packed_rows(sum(w[n].shape[1] for n in SHARDED_VECS))
    res = adam_update(_pack([w[n] for n in SHARDED_VECS], vrows), _pack([m[n] for n in SHARDED_VECS], vrows),
                      _pack([v[n] for n in SHARDED_VECS], vrows), _pack([vec_grads[n] for n in SHARDED_VECS], vrows),
                      None, "adam_vecs")
    flat = [r.reshape(-1) for r in res]
    off = 0
    for n in SHARDED_VECS:
        size = w[n].shape[1]
        for dst, f in zip((out_g, out_d, out_m, out_v), flat):
            dst[n] = f[off:off + size].reshape(w[n].shape)
        off += size

    return (total_loss, grad_x[None], *[out_g[n] for n in WEIGHTS], *[out_d[n] for n in WEIGHTS],
            *[out_m[n] for n in WEIGHTS], *[out_v[n] for n in WEIGHTS])
```

```python
import functools
import math

import numpy as np
import jax
import jax.numpy as jnp
from jax import lax
from jax.experimental import pallas as pl
from jax.experimental.pallas import tpu as pltpu

F32 = jnp.float32
MXU_DTYPE = jnp.bfloat16
NORM_EPS = 1e-6
D_MODEL = 1024
S5_WIDTH = 1024
S5_GROUP = 16
S5_GROUPS = 64
S5_STATE = 64
S5_LANES = S5_GROUPS * S5_STATE
S5_KBLK = 8
RET_HEADS = 4
RET_DK = 256
RET_CHUNK = 128
ROPE_BASE = 10000.0
SGU_WIDTH = 2048
SGU_GROUPS = 4
SGU_GDIM = 512
SGU_CHUNK = 128
EVEN_IN = 6144
ODD_IN = 6144
ADAM_LR = 0.001
ADAM_B1 = 0.9
ADAM_B2 = 0.999
ADAM_EPS = 1e-08
ADAM_WD = 0.01
ADAM_STEP = 10
N_CHIPS = 4
VMEM_LIMIT = 56 * 1024 * 1024

TL_PROJ = 512
TL_DW = 1024
TL_S5 = 128
TL_SGU = 128


def _cparams(sem, **kw):
    return pltpu.CompilerParams(dimension_semantics=sem, vmem_limit_bytes=VMEM_LIMIT, **kw)


def _mm(a, b):
    return jnp.dot(a.astype(MXU_DTYPE), b.astype(MXU_DTYPE), preferred_element_type=F32)


def _mm_nt(a, b):
    return lax.dot_general(a.astype(MXU_DTYPE), b.astype(MXU_DTYPE),
                           (((1,), (1,)), ((), ())), preferred_element_type=F32)


def _mm_tn(a, b):
    return lax.dot_general(a.astype(MXU_DTYPE), b.astype(MXU_DTYPE),
                           (((0,), (0,)), ((), ())), preferred_element_type=F32)


_GELU_C = math.sqrt(2.0 / math.pi)


def _gelu(x):
    return 0.5 * x * (1.0 + jnp.tanh(_GELU_C * (x + 0.044715 * x * x * x)))


def _gelu_grad(x):
    th = jnp.tanh(_GELU_C * (x + 0.044715 * x * x * x))
    return 0.5 * (1.0 + th) + 0.5 * x * (1.0 - th * th) * _GELU_C * (1.0 + 3.0 * 0.044715 * x * x)


def _sigmoid(x):
    return 1.0 / (1.0 + jnp.exp(-x))


def _silu_and_grad(x):
    s = _sigmoid(x)
    return x * s, s * (1.0 + x * (1.0 - s))


def _rms(x):
    return lax.rsqrt(jnp.mean(x * x, axis=-1, keepdims=True) + NORM_EPS)


def _full(shape):
    nd = len(shape)
    return pl.BlockSpec(shape, lambda *_: (0,) * nd)


MESH = pl.DeviceIdType.MESH
ANY = pl.BlockSpec(memory_space=pl.ANY)


def _place():
    return lax.axis_index("x"), lax.axis_index("y"), lax.axis_index("c")


def _chip_peer(x, y, c, d):
    return (1 - x if d >= 2 else x, 1 - y if d % 2 else y, c)


class _Plan:
    def __init__(self, inputs, out_shape, build):
        self.inputs, self.out_shape, self._build = list(inputs), list(out_shape), build
        n = len(self.inputs)
        self.sems = [pltpu.SemaphoreType.DMA((n, 3)), pltpu.SemaphoreType.DMA((n, 3)), pltpu.SemaphoreType.DMA((n,))]

    def start(self, in_refs, out_refs, sems):
        send, recv, local = self._build(in_refs, out_refs, sems)
        for p in range(len(self.inputs)):
            local[p].start()
            for cp in send[p]:
                cp.start()

    def wait(self, in_refs, out_refs, sems):
        send, recv, local = self._build(in_refs, out_refs, sems)
        for p in range(len(self.inputs)):
            for cp in recv[p]:
                cp.wait_recv()
        for p in range(len(self.inputs)):
            for cp in send[p]:
                cp.wait_send()
            local[p].wait()


def gather_plan(shards):
    def build(in_refs, out_refs, sems):
        send_sems, recv_sems, loc_sems = sems
        x, y, c = _place()
        me = 2 * x + y

        def remote(p, d, slab):
            return pltpu.make_async_remote_copy(
                src_ref=in_refs[p], dst_ref=out_refs[p].at[slab], send_sem=send_sems.at[p, d - 1],
                recv_sem=recv_sems.at[p, d - 1], device_id=_chip_peer(x, y, c, d), device_id_type=MESH)

        n = len(in_refs)
        send = [[remote(p, d, me) for d in (1, 2, 3)] for p in range(n)]
        recv = [[remote(p, d, me ^ d) for d in (1, 2, 3)] for p in range(n)]
        local = [pltpu.make_async_copy(in_refs[p], out_refs[p].at[me], loc_sems.at[p]) for p in range(n)]
        return send, recv, local

    return _Plan(shards, [jax.ShapeDtypeStruct((N_CHIPS,) + s.shape, s.dtype) for s in shards], build)


def reduce_plan(shards, whole=()):
    n_s = len(shards)

    def build(in_refs, out_refs, sems):
        send_sems, recv_sems, loc_sems = sems
        x, y, c = _place()
        me = 2 * x + y

        def src(p, slab):
            return in_refs[p].at[slab] if p < n_s else in_refs[p]

        def remote(p, d):
            return pltpu.make_async_remote_copy(
                src_ref=src(p, me ^ d), dst_ref=out_refs[p].at[d], send_sem=send_sems.at[p, d - 1],
                recv_sem=recv_sems.at[p, d - 1], device_id=_chip_peer(x, y, c, d), device_id_type=MESH)

        n = len(in_refs)
        send = [[remote(p, d) for d in (1, 2, 3)] for p in range(n)]
        local = [pltpu.make_async_copy(src(p, me), out_refs[p].at[0], loc_sems.at[p]) for p in range(n)]
        return send, send, local

    outs = [jax.ShapeDtypeStruct(s.shape, s.dtype) for s in shards]
    outs += [jax.ShapeDtypeStruct((N_CHIPS,) + a.shape, a.dtype) for a in whole]
    return _Plan(list(shards) + list(whole), outs, build)


def run_plan(plan, name):
    n = len(plan.inputs)

    def body(*refs):
        plan.start(refs[:n], refs[n:2 * n], refs[2 * n:])
        plan.wait(refs[:n], refs[n:2 * n], refs[2 * n:])

    return pl.pallas_call(body, name=name, in_specs=[ANY] * n, out_specs=[ANY] * n, out_shape=plan.out_shape,
                          scratch_shapes=plan.sems)(*plan.inputs)


def _call(body, plan, *, name, grid, in_specs, out_specs, out_shape, sem, scratch_shapes=()):
    single = not isinstance(out_shape, (list, tuple))
    out_specs = [out_specs] if single else list(out_specs)
    out_shape = [out_shape] if single else list(out_shape)
    n_in, n_out, n_scr = len(in_specs), len(out_specs), len(scratch_shapes)
    ci = 0 if plan is None else len(plan.inputs)

    def hosted(*refs):
        ins, cins = refs[:n_in], refs[n_in:n_in + ci]
        k = n_in + ci
        outs, couts = refs[k:k + n_out], refs[k + n_out:k + n_out + ci]
        k += n_out + ci
        scr, sems = refs[k:k + n_scr], refs[k + n_scr:]
        ids = [pl.program_id(a) for a in range(len(grid))]
        first = functools.reduce(jnp.logical_and, [i == 0 for i in ids])
        last = functools.reduce(jnp.logical_and, [i == g - 1 for i, g in zip(ids, grid)])

        @pl.when(first)
        def _():
            plan.start(cins, couts, sems)

        body(*ins, *outs, *scr)

        @pl.when(last)
        def _():
            plan.wait(cins, couts, sems)

    def run(*args):
        if plan is None:
            res = pl.pallas_call(body, name=name, grid=grid, in_specs=list(in_specs), out_specs=out_specs,
                                 out_shape=out_shape, scratch_shapes=list(scratch_shapes),
                                 compiler_params=_cparams(sem))(*args)
            return (res[0] if single else res), []
        res = pl.pallas_call(hosted, name=name, grid=grid, in_specs=list(in_specs) + [ANY] * ci,
                             out_specs=out_specs + [ANY] * ci, out_shape=out_shape + plan.out_shape,
                             scratch_shapes=list(scratch_shapes) + plan.sems,
                             compiler_params=_cparams(sem))(*args, *plan.inputs)
        return (res[0] if single else res[:n_out]), list(res[n_out:])

    return run


def norm_matmul(x, g, w, name, plan=None, tn=None):
    L, D = x.shape
    tl = min(TL_PROJ, L)
    if w.ndim == 3:
        nt, _, tn = w.shape
        w_spec = pl.BlockSpec((1, D, tn), lambda i, n: (n, 0, 0))
    else:
        nt = w.shape[1] // tn
        w_spec = pl.BlockSpec((D, tn), lambda i, n: (0, n))

    def body(x_ref, g_ref, w_ref, o_ref, h_ref):
        @pl.when(pl.program_id(1) == 0)
        def _():
            xv = x_ref[...]
            h_ref[...] = (xv * _rms(xv) * g_ref[...]).astype(h_ref.dtype)

        wv = w_ref[0] if w.ndim == 3 else w_ref[...]
        o_ref[...] = _mm(h_ref[...], wv)

    return _call(
        body, plan, name=name, grid=(L // tl, nt),
        in_specs=[pl.BlockSpec((tl, D), lambda i, n: (i, 0)), _full((1, D)), w_spec],
        out_specs=[pl.BlockSpec((tl, tn), lambda i, n: (i, n)), pl.BlockSpec((tl, D), lambda i, n: (i, 0))],
        out_shape=[jax.ShapeDtypeStruct((L, nt * tn), F32), jax.ShapeDtypeStruct((L, D), MXU_DTYPE)],
        sem=("arbitrary", "arbitrary"),
    )(x, g, w)


def matmul_residual(ys, w, x, name):
    L, D = x.shape
    tl = min(TL_PROJ, L)
    n = len(ys)
    offs = np.cumsum([0] + [y.shape[1] for y in ys])

    def body(*refs):
        y_refs, w_ref, x_ref, o_ref = refs[:n], refs[n], refs[n + 1], refs[n + 2]
        acc = x_ref[...]
        for k in range(n):
            acc = acc + _mm(y_refs[k][...], w_ref[offs[k]:offs[k + 1], :])
        o_ref[...] = acc

    return pl.pallas_call(
        body, name=name, grid=(L // tl,),
        in_specs=[pl.BlockSpec((tl, y.shape[1]), lambda i: (i, 0)) for y in ys]
        + [_full(w.shape), pl.BlockSpec((tl, D), lambda i: (i, 0))],
        out_specs=pl.BlockSpec((tl, D), lambda i: (i, 0)),
        out_shape=jax.ShapeDtypeStruct((L, D), F32),
        compiler_params=_cparams(("arbitrary",)),
    )(*ys, w, x)


def out_proj_loss(y, w, x, gf, tgt, name):
    L, K = y.shape
    D = w.shape[1]
    tl = min(TL_PROJ, L)

    def body(y_ref, w_ref, x_ref, gf_ref, t_ref, dx_ref, loss_ref, dg_ref):
        @pl.when(pl.program_id(0) == 0)
        def _():
            loss_ref[...] = jnp.zeros_like(loss_ref)
            dg_ref[...] = jnp.zeros_like(dg_ref)

        x2 = x_ref[...] + _mm(y_ref[...], w_ref[...])
        r = _rms(x2)
        xn = x2 * r
        e = xn * gf_ref[...] - t_ref[...]
        loss_ref[...] += (0.5 / D) * jnp.sum(e * e)
        dout = e * (1.0 / D)
        dg_ref[...] += jnp.sum(dout * xn, axis=0, keepdims=True)
        dxn = dout * gf_ref[...]
        dx_ref[...] = r * (dxn - xn * jnp.mean(dxn * xn, axis=-1, keepdims=True))

    return pl.pallas_call(
        body, name=name, grid=(L // tl,),
        in_specs=[pl.BlockSpec((tl, K), lambda i: (i, 0)), _full((K, D)),
                  pl.BlockSpec((tl, D), lambda i: (i, 0)), _full((1, D)),
                  pl.BlockSpec((tl, D), lambda i: (i, 0))],
        out_specs=[pl.BlockSpec((tl, D), lambda i: (i, 0)), _full((8, 128)), _full((1, D))],
        out_shape=[jax.ShapeDtypeStruct((L, D), F32), jax.ShapeDtypeStruct((8, 128), F32),
                   jax.ShapeDtypeStruct((1, D), F32)],
        compiler_params=_cparams(("arbitrary",)),
    )(y, w, x, gf, tgt)


def out_proj_bwd(dx, w, ys, name):
    L, D = dx.shape
    K = w.shape[0]
    tl = min(TL_PROJ, L)
    n = len(ys)
    offs = np.cumsum([0] + [y.shape[1] for y in ys])

    def body(*refs):
        dx_ref, w_ref, y_refs = refs[0], refs[1], refs[2:2 + n]
        dy_refs, dw_ref = refs[2 + n:2 + 2 * n], refs[2 + 2 * n]

        @pl.when(pl.program_id(0) == 0)
        def _():
            dw_ref[...] = jnp.zeros_like(dw_ref)

        dxv = dx_ref[...]
        for k in range(n):
            dy_refs[k][...] = _mm_nt(dxv, w_ref[offs[k]:offs[k + 1], :])
            dw_ref[offs[k]:offs[k + 1], :] += _mm_tn(y_refs[k][...], dxv)

    y_specs = [pl.BlockSpec((tl, y.shape[1]), lambda i: (i, 0)) for y in ys]
    return pl.pallas_call(
        body, name=name, grid=(L // tl,),
        in_specs=[pl.BlockSpec((tl, D), lambda i: (i, 0)), _full((K, D))] + y_specs,
        out_specs=y_specs + [_full((K, D))],
        out_shape=[jax.ShapeDtypeStruct(y.shape, F32) for y in ys] + [jax.ShapeDtypeStruct((K, D), F32)],
        compiler_params=_cparams(("arbitrary",)),
    )(dx, w, *ys)


def in_proj_bwd_dx(x, g, dps, ws, dres, name, plan=None):
    L, D = x.shape
    tl = min(TL_PROJ, L)
    n = len(dps)

    def body(*refs):
        x_ref, g_ref, dres_ref = refs[:3]
        dp_refs, w_refs = refs[3:3 + n], refs[3 + n:3 + 2 * n]
        dx_ref, dg_ref = refs[3 + 2 * n:]

        @pl.when(pl.program_id(0) == 0)
        def _():
            dg_ref[...] = jnp.zeros_like(dg_ref)

        dh = None
        for dp_ref, w_ref, w in zip(dp_refs, w_refs, ws):
            if w.ndim == 3:
                tn = w.shape[2]
                parts = [_mm_nt(dp_ref[:, tn * k:tn * (k + 1)], w_ref[k]) for k in range(w.shape[0])]
            else:
                parts = [_mm_nt(dp_ref[...], w_ref[...])]
            for part in parts:
                dh = part if dh is None else dh + part
        xv = x_ref[...]
        r = _rms(xv)
        xn = xv * r
        dg_ref[...] += jnp.sum(dh * xn, axis=0, keepdims=True)
        dxn = dh * g_ref[...]
        dx_ref[...] = dres_ref[...] + r * (dxn - xn * jnp.mean(dxn * xn, axis=-1, keepdims=True))

    return _call(
        body, plan, name=name, grid=(L // tl,),
        in_specs=[pl.BlockSpec((tl, D), lambda i: (i, 0)), _full((1, D)), pl.BlockSpec((tl, D), lambda i: (i, 0))]
        + [pl.BlockSpec((tl, dp.shape[1]), lambda i: (i, 0)) for dp in dps] + [_full(w.shape) for w in ws],
        out_specs=[pl.BlockSpec((tl, D), lambda i: (i, 0)), _full((1, D))],
        out_shape=[jax.ShapeDtypeStruct((L, D), F32), jax.ShapeDtypeStruct((1, D), F32)],
        sem=("arbitrary",),
    )(x, g, dres, *dps, *ws)


def in_proj_bwd_dw(h, dp, name, tn, first=0, into=None, dtype=F32):
    L, D = h.shape
    tl = min(TL_DW, L)
    wb = EVEN_IN // N_CHIPS
    per = wb // tn
    count = dp.shape[1] // tn
    last = L // tl - 1

    def body(*refs):
        h_ref, dp_ref, dw_ref, acc = refs[0], refs[1], refs[-2], refs[-1]

        @pl.when(pl.program_id(1) == 0)
        def _():
            acc[...] = jnp.zeros_like(acc)

        acc[...] += _mm_tn(h_ref[...], dp_ref[...])

        @pl.when(pl.program_id(1) == last)
        def _():
            dw_ref[0] = acc[...].astype(dw_ref.dtype)

    ins = [h, dp] + ([] if into is None else [into])
    return pl.pallas_call(
        body, name=name, grid=(count, L // tl),
        in_specs=[pl.BlockSpec((tl, D), lambda n, i: (i, 0)), pl.BlockSpec((tl, tn), lambda n, i: (i, n))]
        + ([] if into is None else [ANY]),
        out_specs=pl.BlockSpec((1, D, tn), lambda n, i: ((n + first) // per, 0, (n + first) % per)),
        out_shape=jax.ShapeDtypeStruct((N_CHIPS, D, wb), dtype),
        scratch_shapes=[pltpu.VMEM((D, tn), F32)],
        input_output_aliases={} if into is None else {2: 0},
        compiler_params=_cparams(("arbitrary", "arbitrary")),
    )(*ins)


def _s5_param_fn(lam_re, lam_im, log_dt, b_re, b_im):
    lr = jnp.minimum(lam_re, -1e-4)
    li = lam_im
    dt = jnp.exp(log_dt)
    mag = jnp.exp(lr * dt)
    ab_re = mag * jnp.cos(li * dt)
    ab_im = mag * jnp.sin(li * dt)
    den = lr * lr + li * li
    n_re = ab_re - 1.0
    n_im = ab_im
    z_re = (n_re * lr + n_im * li) / den
    z_im = (n_im * lr - n_re * li) / den
    bb_re = z_re[None] * b_re - z_im[None] * b_im
    bb_im = z_re[None] * b_im + z_im[None] * b_re
    return ab_re, ab_im, bb_re, bb_im


def s5_params_fwd(lam_re, lam_im, log_dt, b_re, b_im, span):
    G, P = lam_re.shape
    H = b_re.shape[0]
    assert span & (span - 1) == 0

    def body(lr_ref, li_ref, dt_ref, br_ref, bi_ref, abr_ref, abi_ref, bbr_ref, bbi_ref, pr_ref, pi_ref):
        ab_re, ab_im, bb_re, bb_im = _s5_param_fn(lr_ref[...], li_ref[...], dt_ref[...], br_ref[...], bi_ref[...])
        abr_ref[...] = ab_re
        abi_ref[...] = ab_im
        bbr_ref[...] = bb_re
        bbi_ref[...] = bb_im
        cr, ci = ab_re, ab_im
        for _ in range(span.bit_length() - 1):
            cr, ci = cr * cr - ci * ci, 2.0 * cr * ci
        pr_ref[...] = cr
        pi_ref[...] = ci

    shp = lambda *s: jax.ShapeDtypeStruct(s, F32)
    return pl.pallas_call(
        body, name="s5_params_fwd",
        out_shape=[shp(G, P), shp(G, P), shp(H, G, P), shp(H, G, P), shp(G, P), shp(G, P)],
    )(lam_re, lam_im, log_dt, b_re, b_im)


def s5_params_bwd(lam_re, lam_im, log_dt, b_re, b_im, d_ab_re, d_ab_im, d_bb_re, d_bb_im):
    G, P = lam_re.shape
    H = b_re.shape[0]

    def body(lr_ref, li_ref, dt_ref, br_ref, bi_ref, g0, g1, g2, g3, o0, o1, o2, o3, o4):
        prim = (lr_ref[...], li_ref[...], dt_ref[...], br_ref[...], bi_ref[...])
        _, vjp = jax.vjp(_s5_param_fn, *prim)
        d = vjp((jnp.sum(g0[...], axis=0), jnp.sum(g1[...], axis=0), g2[...], g3[...]))
        o0[...], o1[...], o2[...], o3[...], o4[...] = d

    shp = lambda *s: jax.ShapeDtypeStruct(s, F32)
    return pl.pallas_call(
        body, name="s5_params_bwd",
        out_shape=[shp(G, P), shp(G, P), shp(G, 1), shp(H, G, P), shp(H, G, P)],
    )(lam_re, lam_im, log_dt, b_re, b_im, d_ab_re, d_ab_im, d_bb_re, d_bb_im)


def stream_order(a, tl):
    L, C = a.shape
    return a.reshape(L // tl, 8, tl // 8, C).transpose(0, 2, 1, 3).reshape(L, C)


def token_order(a, tl):
    L, C = a.shape
    return a.reshape(L // tl, tl // 8, 8, C).transpose(0, 2, 1, 3).reshape(L, C)


_LANE_BLK = 1024


def _cmul_add(ar, ai, xr, xi, br, bi):
    return br + (ar * xr - ai * xi), bi + (ar * xi + ai * xr)


def _cmulc_add(ar, ai, xr, xi, br, bi):
    return br + (ar * xr + ai * xi), bi + (ar * xi - ai * xr)


def _s5_states(u, wbd_ref, a_re, a_im, at_re, at_im, s_re, s_im, e_re, e_im, c0_re, c0_im, tl):
    t8 = tl // 8
    for k in range(S5_KBLK):
        bu = _mm(u[:, 128 * k:128 * (k + 1)], wbd_ref[k])
        s_re[:, 512 * k:512 * (k + 1)] = bu[:, :512]
        s_im[:, 512 * k:512 * (k + 1)] = bu[:, 512:]
    outs_re, outs_im = [], []
    for b in range(S5_LANES // _LANE_BLK):
        lanes = slice(_LANE_BLK * b, _LANE_BLK * (b + 1))
        ar = jnp.broadcast_to(a_re[:, lanes], (8, _LANE_BLK))
        ai = jnp.broadcast_to(a_im[:, lanes], (8, _LANE_BLK))

        def local(i, carry, lanes=lanes, ar=ar, ai=ai):
            r = pl.multiple_of(i * 8, 8)
            sr, si = _cmul_add(ar, ai, carry[0], carry[1], s_re[pl.ds(r, 8), lanes], s_im[pl.ds(r, 8), lanes])
            s_re[pl.ds(r, 8), lanes] = sr
            s_im[pl.ds(r, 8), lanes] = si
            return sr, si

        zero = jnp.zeros((8, _LANE_BLK), F32)
        fr, fi = lax.fori_loop(0, t8, local, (zero, zero), unroll=True)
        tr, ti = at_re[:, lanes], at_im[:, lanes]
        er, ei = c0_re[:, lanes], c0_im[:, lanes]
        ers, eis = [er], [ei]
        for j in range(8):
            er, ei = _cmul_add(tr, ti, er, ei, fr[j:j + 1], fi[j:j + 1])
            ers.append(er)
            eis.append(ei)
        outs_re.append(ers[8])
        outs_im.append(eis[8])
        ent_r, ent_i = jnp.concatenate(ers[:8], axis=0), jnp.concatenate(eis[:8], axis=0)
        e_re[:, lanes] = ent_r
        e_im[:, lanes] = ent_i

        def fix(i, carry, lanes=lanes, ar=ar, ai=ai):
            r = pl.multiple_of(i * 8, 8)
            zr, zi = ar * carry[0] - ai * carry[1], ar * carry[1] + ai * carry[0]
            s_re[pl.ds(r, 8), lanes] = s_re[pl.ds(r, 8), lanes] + zr
            s_im[pl.ds(r, 8), lanes] = s_im[pl.ds(r, 8), lanes] + zi
            return zr, zi

        lax.fori_loop(0, t8, fix, (ent_r, ent_i), unroll=True)
    return jnp.concatenate(outs_re, axis=1), jnp.concatenate(outs_im, axis=1)


def _s5_readout(s_re, s_im, cre_ref, cim_ref):
    ys = []
    for k in range(S5_KBLK):
        lanes = slice(512 * k, 512 * (k + 1))
        ys.append(_mm(s_re[:, lanes], cre_ref[k]) - _mm(s_im[:, lanes], cim_ref[k]))
    return jnp.concatenate(ys, axis=1)


def s5_forward(p, wbd, cre, cim, atab, d_skip, w_glu, b_glu, plan=None):
    L = p.shape[0]
    tl = min(TL_S5, L)
    nch = L // tl

    def body(u_ref, z_ref, wbd_ref, cre_ref, cim_ref, at_ref, d_ref, wg_ref, bg_ref,
             ya_ref, st_re_ref, st_im_ref, s_re, s_im, e_re, e_im, car_re, car_im):
        @pl.when(pl.program_id(0) == 0)
        def _():
            car_re[...] = jnp.zeros_like(car_re)
            car_im[...] = jnp.zeros_like(car_im)

        c0_re, c0_im = car_re[...], car_im[...]
        st_re_ref[0] = c0_re
        st_im_ref[0] = c0_im
        u = u_ref[...]
        x_re, x_im = _s5_states(u, wbd_ref, at_ref[0:1], at_ref[1:2], at_ref[2:3], at_ref[3:4],
                                s_re, s_im, e_re, e_im, c0_re, c0_im, tl)
        car_re[...] = x_re
        car_im[...] = x_im
        y = _s5_readout(s_re, s_im, cre_ref, cim_ref) + d_ref[...] * u
        yg = _gelu(y)
        gate = _sigmoid(_mm(yg, wg_ref[...]) + bg_ref[...])
        sz, _ = _silu_and_grad(z_ref[...])
        ya_ref[...] = (yg * gate * sz).astype(ya_ref.dtype)

    return _call(
        body, plan, name="s5_forward", grid=(nch,),
        in_specs=[pl.BlockSpec((tl, 1024), lambda i: (i, 0)), pl.BlockSpec((tl, 1024), lambda i: (i, 1)),
                  _full(wbd.shape), _full(cre.shape), _full(cim.shape), _full(atab.shape),
                  _full((1, 1024)), _full((1024, 1024)), _full((1, 1024))],
        out_specs=[pl.BlockSpec((tl, 1024), lambda i: (i, 0)),
                   pl.BlockSpec((1, 1, S5_LANES), lambda i: (i, 0, 0)),
                   pl.BlockSpec((1, 1, S5_LANES), lambda i: (i, 0, 0))],
        out_shape=[jax.ShapeDtypeStruct((L, 1024), MXU_DTYPE),
                   jax.ShapeDtypeStruct((nch, 1, S5_LANES), F32), jax.ShapeDtypeStruct((nch, 1, S5_LANES), F32)],
        scratch_shapes=[pltpu.VMEM((tl, S5_LANES), F32), pltpu.VMEM((tl, S5_LANES), F32),
                        pltpu.VMEM((8, S5_LANES), F32), pltpu.VMEM((8, S5_LANES), F32),
                        pltpu.VMEM((1, S5_LANES), F32), pltpu.VMEM((1, S5_LANES), F32)],
        sem=("arbitrary",),
    )(p, p, wbd, cre, cim, atab, d_skip, w_glu, b_glu)


def s5_backward(p, dya, st_re, st_im, wbd, cre, cim, atab, d_skip, w_glu, b_glu, plan=None):
    L = p.shape[0]
    tl = min(TL_S5, L)
    t8 = tl // 8
    nch = L // tl
    rev = lambda i: (nch - 1 - i, 0)
    rev1 = lambda i: (nch - 1 - i, 1)
    rev3 = lambda i: (nch - 1 - i, 0, 0)

    def body(u_ref, z_ref, dya_ref, str_ref, sti_ref, wbd_ref, cre_ref, cim_ref, at_ref,
             d_ref, wg_ref, bg_ref,
             dp_ref, dwbd_ref, dcre_ref, dcim_ref, dabr_ref, dabi_ref, dd_ref, dwg_ref, dbg_ref,
             s_re, s_im, g_re, g_im, e_re, e_im, car_re, car_im):
        @pl.when(pl.program_id(0) == 0)
        def _():
            car_re[...] = jnp.zeros_like(car_re)
            car_im[...] = jnp.zeros_like(car_im)
            for r in (dwbd_ref, dcre_ref, dcim_ref, dabr_ref, dabi_ref, dd_ref, dwg_ref, dbg_ref):
                r[...] = jnp.zeros_like(r)

        u = u_ref[...]
        a_re, a_im, at_re, at_im = at_ref[0:1], at_ref[1:2], at_ref[2:3], at_ref[3:4]
        _s5_states(u, wbd_ref, a_re, a_im, at_re, at_im, s_re, s_im, e_re, e_im, str_ref[0], sti_ref[0], tl)
        y = _s5_readout(s_re, s_im, cre_ref, cim_ref) + d_ref[...] * u
        yg = _gelu(y)
        gate = _sigmoid(_mm(yg, wg_ref[...]) + bg_ref[...])
        sz, dsz = _silu_and_grad(z_ref[...])
        dya = dya_ref[...]
        s5out = yg * gate
        dp_ref[:, 1024:] = (dya * s5out * dsz).astype(dp_ref.dtype)
        ds5 = dya * sz
        dt = ds5 * yg * gate * (1.0 - gate)
        dwg_ref[...] += _mm_tn(yg, dt)
        dbg_ref[...] += jnp.sum(dt, axis=0, keepdims=True)
        dyv = (ds5 * gate + _mm_nt(dt, wg_ref[...])) * _gelu_grad(y)
        dd_ref[...] += jnp.sum(dyv * u, axis=0, keepdims=True)

        for k in range(S5_KBLK):
            lanes = slice(512 * k, 512 * (k + 1))
            dyk = dyv[:, 128 * k:128 * (k + 1)]
            g_re[:, lanes] = _mm_nt(dyk, cre_ref[k])
            g_im[:, lanes] = -_mm_nt(dyk, cim_ref[k])
            dcre_ref[k] += _mm_tn(s_re[:, lanes], dyk)
            dcim_ref[k] -= _mm_tn(s_im[:, lanes], dyk)

        for b in range(S5_LANES // _LANE_BLK):
            lanes = slice(_LANE_BLK * b, _LANE_BLK * (b + 1))
            ar = jnp.broadcast_to(a_re[:, lanes], (8, _LANE_BLK))
            ai = jnp.broadcast_to(a_im[:, lanes], (8, _LANE_BLK))

            def local(j, carry, lanes=lanes, ar=ar, ai=ai):
                r = pl.multiple_of((t8 - 1 - j) * 8, 8)
                gr, gi = _cmulc_add(ar, ai, carry[0], carry[1], g_re[pl.ds(r, 8), lanes], g_im[pl.ds(r, 8), lanes])
                g_re[pl.ds(r, 8), lanes] = gr
                g_im[pl.ds(r, 8), lanes] = gi
                return gr, gi

            zero = jnp.zeros((8, _LANE_BLK), F32)
            fr, fi = lax.fori_loop(0, t8, local, (zero, zero), unroll=True)
            tr, ti = at_re[:, lanes], at_im[:, lanes]
            hr, hi = car_re[:, lanes], car_im[:, lanes]
            hrs, his = [hr], [hi]
            for j in range(7, -1, -1):
                hr, hi = _cmulc_add(tr, ti, hr, hi, fr[j:j + 1], fi[j:j + 1])
                hrs.append(hr)
                his.append(hi)
            car_re[:, lanes] = hrs[8]
            car_im[:, lanes] = his[8]
            in_r = jnp.concatenate(hrs[7::-1], axis=0)
            in_i = jnp.concatenate(his[7::-1], axis=0)

            def fix(j, carry, lanes=lanes, ar=ar, ai=ai):
                wr, wi, accr, acci = carry
                r = pl.multiple_of((t8 - 1 - j) * 8, 8)
                wr, wi = ar * wr + ai * wi, ar * wi - ai * wr
                gr, gi = g_re[pl.ds(r, 8), lanes] + wr, g_im[pl.ds(r, 8), lanes] + wi
                g_re[pl.ds(r, 8), lanes] = gr
                g_im[pl.ds(r, 8), lanes] = gi
                sr, si = s_re[pl.ds(r - 8, 8), lanes], s_im[pl.ds(r - 8, 8), lanes]
                return wr, wi, accr + (sr * gr + si * gi), acci + (sr * gi - si * gr)

            wr, wi, accr, acci = lax.fori_loop(0, t8 - 1, fix, (in_r, in_i, zero, zero), unroll=True)
            wr, wi = ar * wr + ai * wi, ar * wi - ai * wr
            gr, gi = g_re[pl.ds(0, 8), lanes] + wr, g_im[pl.ds(0, 8), lanes] + wi
            g_re[pl.ds(0, 8), lanes] = gr
            g_im[pl.ds(0, 8), lanes] = gi
            sr, si = e_re[:, lanes], e_im[:, lanes]
            dabr_ref[:, lanes] += accr + (sr * gr + si * gi)
            dabi_ref[:, lanes] += acci + (sr * gi - si * gr)

        dus = []
        for k in range(S5_KBLK):
            lanes = slice(512 * k, 512 * (k + 1))
            g = jnp.concatenate([g_re[:, lanes], g_im[:, lanes]], axis=1)
            dwbd_ref[k] += _mm_tn(u[:, 128 * k:128 * (k + 1)], g)
            dus.append(_mm_nt(g, wbd_ref[k]))
        du = jnp.concatenate(dus, axis=1) + dyv * d_ref[...]
        dp_ref[:, :1024] = du.astype(dp_ref.dtype)

    shp = lambda *s: jax.ShapeDtypeStruct(s, F32)
    return _call(
        body, plan, name="s5_backward", grid=(nch,),
        in_specs=[pl.BlockSpec((tl, 1024), rev), pl.BlockSpec((tl, 1024), rev1), pl.BlockSpec((tl, 1024), rev),
                  pl.BlockSpec((1, 1, S5_LANES), rev3), pl.BlockSpec((1, 1, S5_LANES), rev3),
                  _full(wbd.shape), _full(cre.shape), _full(cim.shape), _full(atab.shape),
                  _full((1, 1024)), _full((1024, 1024)), _full((1, 1024))],
        out_specs=[pl.BlockSpec((tl, 2048), rev), _full(wbd.shape), _full(cre.shape), _full(cim.shape),
                   _full((8, S5_LANES)), _full((8, S5_LANES)), _full((1, 1024)), _full((1024, 1024)), _full((1, 1024))],
        out_shape=[jax.ShapeDtypeStruct((L, 2048), MXU_DTYPE), shp(*wbd.shape), shp(*cre.shape), shp(*cim.shape),
                   shp(8, S5_LANES), shp(8, S5_LANES), shp(1, 1024), shp(1024, 1024), shp(1, 1024)],
        scratch_shapes=[pltpu.VMEM((tl, S5_LANES), F32), pltpu.VMEM((tl, S5_LANES), F32),
                        pltpu.VMEM((tl, S5_LANES), F32), pltpu.VMEM((tl, S5_LANES), F32),
                        pltpu.VMEM((8, S5_LANES), F32), pltpu.VMEM((8, S5_LANES), F32),
                        pltpu.VMEM((1, S5_LANES), F32), pltpu.VMEM((1, S5_LANES), F32)],
        sem=("arbitrary",),
    )(p, p, dya, st_re, st_im, wbd, cre, cim, atab, d_skip, w_glu, b_glu)


def _block_diag(w, rows_first):
    g8 = w.reshape(S5_KBLK, 8, w.shape[1], w.shape[2])
    eye = jnp.eye(8, dtype=w.dtype)
    out = jnp.einsum('kgab,fg->kfagb', g8, eye)
    return out.reshape(S5_KBLK, 8 * w.shape[1], 8 * w.shape[2])


def _block_diag_extract(wbd, a, b):
    w5 = wbd.reshape(S5_KBLK, 8, a, 8, b)
    idx = jnp.arange(8)
    return w5[:, idx, :, idx, :].transpose(1, 0, 2, 3).reshape(S5_GROUPS, a, b)


def _ret_constants():
    log_g = np.log1p(-np.exp2(-5.0 - np.arange(RET_HEADS, dtype=np.float32))).astype(np.float32)
    idx = np.arange(RET_CHUNK, dtype=np.float32)
    diff = idx[:, None] - idx[None, :]
    decay = np.where(diff >= 0, np.exp(log_g[:, None, None] * np.maximum(diff, 0.0)), 0.0).astype(np.float32)
    xi = np.exp(log_g[None, :] * (idx[:, None] + 1.0)).astype(np.float32)
    zeta = np.exp(log_g[None, :] * (RET_CHUNK - 1.0 - idx[:, None])).astype(np.float32)
    chunk_decay = np.exp(log_g * RET_CHUNK).astype(np.float32)
    return decay, xi, zeta, chunk_decay


def _rope_tables(L):
    half = RET_DK // 2
    inv = ROPE_BASE ** (-jnp.arange(half, dtype=F32) / half)
    ang = jnp.arange(L, dtype=F32)[:, None] * inv[None, :]
    return jnp.cos(ang), jnp.sin(ang)


def _rot(xh, cos, sin):
    x1, x2 = xh[:, :128], xh[:, 128:]
    return jnp.concatenate([x1 * cos - x2 * sin, x1 * sin + x2 * cos], axis=1)


def _rot_t(dh, cos, sin):
    d1, d2 = dh[:, :128], dh[:, 128:]
    return jnp.concatenate([d1 * cos + d2 * sin, d2 * cos - d1 * sin], axis=1)


def retention_forward(p, cos, sin, gain):
    L = p.shape[0]
    nc = L // RET_CHUNK
    decay_np, xi_np, zeta_np, cd_np = _ret_constants()
    decay, xi, zeta = jnp.asarray(decay_np), jnp.asarray(xi_np), jnp.asarray(zeta_np)
    scale = RET_DK ** -0.5

    def body(q_ref, k_ref, v_ref, z_ref, cos_ref, sin_ref, dec_ref, xi_ref, zeta_ref, gain_ref,
             yb_ref, prev_ref, state):
        @pl.when(pl.program_id(0) == 0)
        def _():
            state[...] = jnp.zeros_like(state)

        cs, sn = cos_ref[...], sin_ref[...]
        sz, _ = _silu_and_grad(z_ref[...])
        for h in range(RET_HEADS):
            hs = slice(RET_DK * h, RET_DK * (h + 1))
            qh = _rot(q_ref[:, hs], cs, sn)
            kh = _rot(k_ref[:, hs], cs, sn) * scale
            vh = v_ref[:, hs]
            prev = state[h]
            prev_ref[0, h] = prev.astype(prev_ref.dtype)
            sc = _mm_nt(qh, kh) * dec_ref[h]
            o = _mm(sc, vh) + _mm(qh * xi_ref[:, h:h + 1], prev)
            state[h] = prev * float(cd_np[h]) + _mm_tn(kh * zeta_ref[:, h:h + 1], vh)
            mu = jnp.mean(o, axis=-1, keepdims=True)
            oc = o - mu
            on = oc * lax.rsqrt(jnp.mean(oc * oc, axis=-1, keepdims=True) + NORM_EPS)
            yb_ref[:, hs] = (on * gain_ref[:, hs] * sz[:, hs]).astype(yb_ref.dtype)

    blk = lambda c: pl.BlockSpec((RET_CHUNK, 1024), lambda i, c=c: (i, c))
    return pl.pallas_call(
        body, name="retention_forward", grid=(nc,),
        in_specs=[blk(0), blk(1), blk(2), blk(3),
                  pl.BlockSpec((RET_CHUNK, 128), lambda i: (i, 0)), pl.BlockSpec((RET_CHUNK, 128), lambda i: (i, 0)),
                  _full(decay.shape), _full(xi.shape), _full(zeta.shape), _full((1, 1024))],
        out_specs=[pl.BlockSpec((RET_CHUNK, 1024), lambda i: (i, 0)),
                   pl.BlockSpec((1, RET_HEADS, RET_DK, RET_DK), lambda i: (i, 0, 0, 0))],
        out_shape=[jax.ShapeDtypeStruct((L, 1024), MXU_DTYPE),
                   jax.ShapeDtypeStruct((nc, RET_HEADS, RET_DK, RET_DK), MXU_DTYPE)],
        scratch_shapes=[pltpu.VMEM((RET_HEADS, RET_DK, RET_DK), F32)],
        compiler_params=_cparams(("arbitrary",)),
    )(p, p, p, p, cos, sin, decay, xi, zeta, gain)


def retention_backward(p, dy, prevs, cos, sin, gain, plan=None):
    L = p.shape[0]
    nc = L // RET_CHUNK
    decay_np, xi_np, zeta_np, cd_np = _ret_constants()
    decay, xi, zeta = jnp.asarray(decay_np), jnp.asarray(xi_np), jnp.asarray(zeta_np)
    scale = RET_DK ** -0.5

    def body(q_ref, k_ref, v_ref, z_ref, dyb_ref, prev_ref, cos_ref, sin_ref, dec_ref, xi_ref, zeta_ref, gain_ref,
             dp_ref, dgain_ref, dstate):
        @pl.when(pl.program_id(0) == 0)
        def _():
            dstate[...] = jnp.zeros_like(dstate)
            dgain_ref[...] = jnp.zeros_like(dgain_ref)

        cs, sn = cos_ref[...], sin_ref[...]
        sz, dsz = _silu_and_grad(z_ref[...])
        dyb = dyb_ref[...]
        for h in range(RET_HEADS):
            hs = slice(RET_DK * h, RET_DK * (h + 1))
            qh = _rot(q_ref[:, hs], cs, sn)
            kh = _rot(k_ref[:, hs], cs, sn) * scale
            vh = v_ref[:, hs]
            prev = prev_ref[0, h]
            xih, zth = xi_ref[:, h:h + 1], zeta_ref[:, h:h + 1]
            sc = _mm_nt(qh, kh) * dec_ref[h]
            o = _mm(sc, vh) + _mm(qh * xih, prev)
            mu = jnp.mean(o, axis=-1, keepdims=True)
            oc = o - mu
            rstd = lax.rsqrt(jnp.mean(oc * oc, axis=-1, keepdims=True) + NORM_EPS)
            on = oc * rstd
            gh = gain_ref[:, hs]
            dyh = dyb[:, hs]
            dp_ref[:, 3072 + RET_DK * h:3072 + RET_DK * (h + 1)] = (dyh * on * gh * dsz[:, hs]).astype(dp_ref.dtype)
            dong = dyh * sz[:, hs]
            dgain_ref[:, hs] += jnp.sum(dong * on, axis=0, keepdims=True)
            don = dong * gh
            do = rstd * (don - jnp.mean(don, axis=-1, keepdims=True)
                         - on * jnp.mean(don * on, axis=-1, keepdims=True))
            dst = dstate[h]
            dsc = _mm_nt(do, vh) * dec_ref[h]
            dqh = _mm(dsc, kh) + _mm_nt(do, prev) * xih
            dkh = _mm_tn(dsc, qh) + _mm_nt(vh, dst) * zth
            dvh = _mm_tn(sc, do) + _mm(kh * zth, dst)
            dstate[h] = dst * float(cd_np[h]) + _mm_tn(qh * xih, do)
            dp_ref[:, hs] = _rot_t(dqh, cs, sn).astype(dp_ref.dtype)
            dp_ref[:, 1024 + RET_DK * h:1024 + RET_DK * (h + 1)] = (_rot_t(dkh, cs, sn) * scale).astype(dp_ref.dtype)
            dp_ref[:, 2048 + RET_DK * h:2048 + RET_DK * (h + 1)] = dvh.astype(dp_ref.dtype)

    blk = lambda c: pl.BlockSpec((RET_CHUNK, 1024), lambda i, c=c: (nc - 1 - i, c))
    tab = pl.BlockSpec((RET_CHUNK, 128), lambda i: (nc - 1 - i, 0))
    return _call(
        body, plan, name="retention_backward", grid=(nc,),
        in_specs=[blk(0), blk(1), blk(2), blk(3), blk(0),
                  pl.BlockSpec((1, RET_HEADS, RET_DK, RET_DK), lambda i: (nc - 1 - i, 0, 0, 0)),
                  tab, tab, _full(decay.shape), _full(xi.shape), _full(zeta.shape), _full((1, 1024))],
        out_specs=[pl.BlockSpec((RET_CHUNK, 4096), lambda i: (nc - 1 - i, 0)), _full((1, 1024))],
        out_shape=[jax.ShapeDtypeStruct((L, 4096), MXU_DTYPE), jax.ShapeDtypeStruct((1, 1024), F32)],
        scratch_shapes=[pltpu.VMEM((RET_HEADS, RET_DK, RET_DK), F32)],
        sem=("arbitrary",),
    )(p, p, p, p, dy, prevs, cos, sin, decay, xi, zeta, gain)


def _sgu_mix(p_ref, gain_ref, wm_ref, bt_ref, tl):
    pu, pv, z = p_ref[:, :2048], p_ref[:, 2048:4096], p_ref[:, 4096:]
    u, v = _gelu(pu), _gelu(pv)
    mu = jnp.mean(v, axis=-1, keepdims=True)
    vc = v - mu
    rstd = lax.rsqrt(jnp.mean(vc * vc, axis=-1, keepdims=True) + NORM_EPS)
    vn = vc * rstd
    vg = vn * gain_ref[...]
    mask = (lax.broadcasted_iota(jnp.int32, (SGU_CHUNK, SGU_CHUNK), 0)
            >= lax.broadcasted_iota(jnp.int32, (SGU_CHUNK, SGU_CHUNK), 1))
    wms = [jnp.where(mask, wm_ref[g], 0.0) for g in range(SGU_GROUPS)]
    rows = []
    for c in range(tl // SGU_CHUNK):
        rs = slice(SGU_CHUNK * c, SGU_CHUNK * (c + 1))
        cols = []
        for g in range(SGU_GROUPS):
            gs = slice(SGU_GDIM * g, SGU_GDIM * (g + 1))
            cols.append(_mm(wms[g], vg[rs, gs]) + bt_ref[:, g:g + 1])
        rows.append(jnp.concatenate(cols, axis=1))
    s = rows[0] if len(rows) == 1 else jnp.concatenate(rows, axis=0)
    return pu, pv, z, u, vn, rstd, vg, wms, mask, s


def sgu_forward(p, gain, wm, bt):
    L = p.shape[0]
    tl = min(TL_SGU, L)

    def body(p_ref, gain_ref, wm_ref, bt_ref, y_ref):
        _, _, z, u, _, _, _, _, _, s = _sgu_mix(p_ref, gain_ref, wm_ref, bt_ref, tl)
        sz, _ = _silu_and_grad(z)
        y_ref[...] = (u * s * sz).astype(y_ref.dtype)

    return pl.pallas_call(
        body, name="sgu_forward", grid=(L // tl,),
        in_specs=[pl.BlockSpec((tl, ODD_IN), lambda i: (i, 0)), _full((1, 2048)), _full(wm.shape), _full(bt.shape)],
        out_specs=pl.BlockSpec((tl, 2048), lambda i: (i, 0)),
        out_shape=jax.ShapeDtypeStruct((L, 2048), MXU_DTYPE),
        compiler_params=_cparams(("arbitrary",)),
    )(p, gain, wm, bt)


def sgu_backward(p, dy, gain, wm, bt, plan=None):
    L = p.shape[0]
    tl = min(TL_SGU, L)

    def body(p_ref, dy_ref, gain_ref, wm_ref, bt_ref, dp_ref, dgain_ref, dwm_ref, dbt_ref):
        @pl.when(pl.program_id(0) == 0)
        def _():
            dgain_ref[...] = jnp.zeros_like(dgain_ref)
            dwm_ref[...] = jnp.zeros_like(dwm_ref)
            dbt_ref[...] = jnp.zeros_like(dbt_ref)

        pu, pv, z, u, vn, rstd, vg, wms, mask, s = _sgu_mix(p_ref, gain_ref, wm_ref, bt_ref, tl)
        sz, dsz = _silu_and_grad(z)
        dyv = dy_ref[...]
        dp_ref[:, 4096:] = (dyv * u * s * dsz).astype(dp_ref.dtype)
        dsg = dyv * sz
        dp_ref[:, :2048] = (dsg * s * _gelu_grad(pu)).astype(dp_ref.dtype)
        ds = dsg * u
        rows = []
        dbs = [jnp.zeros((SGU_CHUNK, 1), F32) for _ in range(SGU_GROUPS)]
        for c in range(tl // SGU_CHUNK):
            rs = slice(SGU_CHUNK * c, SGU_CHUNK * (c + 1))
            cols = []
            for g in range(SGU_GROUPS):
                gs = slice(SGU_GDIM * g, SGU_GDIM * (g + 1))
                dsg_c = ds[rs, gs]
                dbs[g] = dbs[g] + jnp.sum(dsg_c, axis=1, keepdims=True)
                dwm_ref[g] += jnp.where(mask, _mm_nt(dsg_c, vg[rs, gs]), 0.0)
                cols.append(_mm_tn(wms[g], dsg_c))
            rows.append(jnp.concatenate(cols, axis=1))
        dbt_ref[...] += jnp.concatenate(dbs, axis=1)
        dvg = rows[0] if len(rows) == 1 else jnp.concatenate(rows, axis=0)
        dgain_ref[...] += jnp.sum(dvg * vn, axis=0, keepdims=True)
        dvn = dvg * gain_ref[...]
        dv = rstd * (dvn - jnp.mean(dvn, axis=-1, keepdims=True) - vn * jnp.mean(dvn * vn, axis=-1, keepdims=True))
        dp_ref[:, 2048:4096] = (dv * _gelu_grad(pv)).astype(dp_ref.dtype)

    return _call(
        body, plan, name="sgu_backward", grid=(L // tl,),
        in_specs=[pl.BlockSpec((tl, ODD_IN), lambda i: (i, 0)), pl.BlockSpec((tl, 2048), lambda i: (i, 0)),
                  _full((1, 2048)), _full(wm.shape), _full(bt.shape)],
        out_specs=[pl.BlockSpec((tl, ODD_IN), lambda i: (i, 0)), _full((1, 2048)), _full(wm.shape), _full(bt.shape)],
        out_shape=[jax.ShapeDtypeStruct((L, ODD_IN), MXU_DTYPE), jax.ShapeDtypeStruct((1, 2048), F32),
                   jax.ShapeDtypeStruct(wm.shape, F32), jax.ShapeDtypeStruct(bt.shape, F32)],
        sem=("arbitrary",),
    )(p, dy, gain, wm, bt)


def cast_shards(mats):
    n = len(mats)

    def body(*refs):
        for p in range(n):
            refs[n + p][...] = refs[p][...].astype(MXU_DTYPE)

    return pl.pallas_call(
        body, name="cast_shards", out_shape=[jax.ShapeDtypeStruct(m.shape, MXU_DTYPE) for m in mats],
        compiler_params=pltpu.CompilerParams(vmem_limit_bytes=VMEM_LIMIT),
    )(*mats)


def local_grads(x, tgt, w):
    L = x.shape[0]
    ne, gf = w["norm_even"], w["final_norm"].reshape(1, D_MODEL)
    sh = dict(zip(MATRICES, cast_shards([w[n][0] for n in MATRICES])))
    (w_in_e,) = run_plan(gather_plan([sh["w_in_even"]]), "gather_w_in_even")
    lam_re, lam_im = w["s5_lam_re"][0], w["s5_lam_im"][0]
    log_dt = w["s5_log_dt"].reshape(S5_GROUPS, 1)
    bt_re = jnp.transpose(w["s5_b_re"][0], (2, 0, 1))
    bt_im = jnp.transpose(w["s5_b_im"][0], (2, 0, 1))
    c_re, c_im = w["s5_c_re"][0], w["s5_c_im"][0]
    wm = w["sgu_w_spatial"][0]
    bt = jnp.transpose(w["sgu_b_spatial"][0])

    tl5 = min(TL_S5, L)
    ab_re, ab_im, bb_re, bb_im, at_re, at_im = s5_params_fwd(lam_re, lam_im, log_dt, bt_re, bt_im, tl5 // 8)
    atab = jnp.stack([ab_re.reshape(S5_LANES), ab_im.reshape(S5_LANES),
                      at_re.reshape(S5_LANES), at_im.reshape(S5_LANES)])
    wbd = jnp.concatenate([_block_diag(jnp.transpose(bb_re, (1, 0, 2)), True),
                           _block_diag(jnp.transpose(bb_im, (1, 0, 2)), True)], axis=2).astype(MXU_DTYPE)
    cre = _block_diag(jnp.transpose(c_re, (0, 2, 1)), True).astype(MXU_DTYPE)
    cim = _block_diag(jnp.transpose(c_im, (0, 2, 1)), True).astype(MXU_DTYPE)
    cos, sin = _rope_tables(L)

    s5_cols = 2 * S5_WIDTH
    w_s5 = jnp.concatenate([w_in_e[0], w_in_e[1][:, :s5_cols - EVEN_IN // N_CHIPS]], axis=1)
    w_ret = jnp.concatenate([w_in_e[1][:, s5_cols - EVEN_IN // N_CHIPS:], w_in_e[2], w_in_e[3]], axis=1)
    (p1a, h0s), (w_glu,) = norm_matmul(stream_order(x, tl5), ne, w_s5, "even_in_s5",
                                       gather_plan([sh["s5_w_glu"]]), tn=1024)
    (p1b, h0), (w_out_e,) = norm_matmul(x, ne, w_ret, "even_in_ret", gather_plan([sh["w_out_even"]]), tn=1024)
    w_glu = w_glu.reshape(S5_WIDTH, S5_WIDTH)
    w_out_e = w_out_e.reshape(2 * S5_WIDTH, D_MODEL)
    (ya, st_re, st_im), (w_in_o, w_out_o, no, sg_gain) = s5_forward(
        p1a, wbd, cre, cim, atab, w["s5_d"], w_glu, w["s5_b_glu"],
        gather_plan([sh["w_in_odd"], sh["w_out_odd"], w["norm_odd"], w["sgu_norm_gain"]]))
    w_out_o = w_out_o.reshape(SGU_WIDTH, D_MODEL)
    no, sg_gain = no.reshape(1, D_MODEL), sg_gain.reshape(1, SGU_WIDTH)
    yb, prevs = retention_forward(p1b, cos, sin, w["ret_gn_gain"])
    ya = token_order(ya, tl5)
    x1 = matmul_residual([ya, yb], w_out_e, x, "even_out")
    (p2, h1), _ = norm_matmul(x1, no, w_in_o, "odd_in")
    y2 = sgu_forward(p2, sg_gain, wm, bt)
    dx2, loss, dgf = out_proj_loss(y2, w_out_o, x1, gf, tgt, "odd_out_loss")

    g, landed = {}, {}
    shard_major = lambda a, n: a.reshape((N_CHIPS,) + w[n].shape[1:])
    dy2, g_w_out_o = out_proj_bwd(dx2, w_out_o, [y2], "odd_out_bwd")
    (dp2, g["sgu_norm_gain"], dwm, dbt), (landed["w_out_odd"],) = sgu_backward(
        p2, dy2, sg_gain, wm, bt, reduce_plan([shard_major(g_w_out_o, "w_out_odd")]))
    g_w_in_o = in_proj_bwd_dw(h1, dp2, "odd_in_dw", ODD_IN // N_CHIPS)
    (dx1, g["norm_odd"]), _ = in_proj_bwd_dx(x1, no, [dp2], [w_in_o], dx2, "odd_in_dx")
    dya, dyb, g_w_out_e = out_proj_bwd(dx1, w_out_e, [ya, yb], "even_out_bwd")
    ((dpa, dwbd, dcre, dcim, dab_re, dab_im, g["s5_d"], g_w_glu, g["s5_b_glu"]),
     (landed["w_in_odd"], landed["w_out_even"])) = s5_backward(
        p1a, stream_order(dya, tl5), st_re, st_im, wbd, cre, cim, atab, w["s5_d"], w_glu,
        w["s5_b_glu"], reduce_plan([g_w_in_o, shard_major(g_w_out_e, "w_out_even")]))
    (dpb, g["ret_gn_gain"]), (landed["s5_w_glu"],) = retention_backward(
        p1b, dyb, prevs, cos, sin, w["ret_gn_gain"], reduce_plan([shard_major(g_w_glu, "s5_w_glu")]))
    g_w_in_e = in_proj_bwd_dw(h0s, dpa, "even_in_dw_s5", 512, dtype=MXU_DTYPE)
    g_w_in_e = in_proj_bwd_dw(h0, dpb, "even_in_dw_ret", 512, first=s5_cols // 512, into=g_w_in_e, dtype=MXU_DTYPE)

    dbb_re = jnp.transpose(_block_diag_extract(dwbd[:, :, :512], S5_GROUP, S5_STATE), (1, 0, 2))
    dbb_im = jnp.transpose(_block_diag_extract(dwbd[:, :, 512:], S5_GROUP, S5_STATE), (1, 0, 2))
    dlr, dli, ddt, dbt_re, dbt_im = s5_params_bwd(
        lam_re, lam_im, log_dt, bt_re, bt_im, dab_re.reshape(8, S5_GROUPS, S5_STATE),
        dab_im.reshape(8, S5_GROUPS, S5_STATE), dbb_re, dbb_im)
    g["s5_lam_re"], g["s5_lam_im"] = dlr[None], dli[None]
    g["s5_log_dt"] = ddt.reshape(1, S5_GROUPS)
    g["s5_b_re"] = jnp.transpose(dbt_re, (1, 2, 0))[None]
    g["s5_b_im"] = jnp.transpose(dbt_im, (1, 2, 0))[None]
    g["s5_c_re"] = jnp.transpose(_block_diag_extract(dcre, S5_STATE, S5_GROUP), (0, 2, 1))[None]
    g["s5_c_im"] = jnp.transpose(_block_diag_extract(dcim, S5_STATE, S5_GROUP), (0, 2, 1))[None]
    g["sgu_w_spatial"] = dwm[None]
    g["sgu_b_spatial"] = jnp.transpose(dbt)[None]
    g["final_norm"] = dgf.reshape(D_MODEL)

    packed = _pack([g[n] for n in PACKED] + [loss[0, :1]])
    (dx0, g["norm_even"]), (landed["w_in_even"], landed["packed"]) = in_proj_bwd_dx(
        x, ne, [token_order(dpa, tl5), dpb], [w_s5, w_ret], dx1, "even_in_dx",
        reduce_plan([g_w_in_e], [packed]))
    return dx0, g, landed


def sibling_exchange(arrs):
    n = len(arrs)

    def body(*refs):
        in_refs, out_refs = refs[:n], refs[n:2 * n]
        send_sems, recv_sems = refs[2 * n:]
        x, y, c = _place()
        copies = [pltpu.make_async_remote_copy(
            src_ref=in_refs[p], dst_ref=out_refs[p], send_sem=send_sems.at[p], recv_sem=recv_sems.at[p],
            device_id=(x, y, 1 - c), device_id_type=MESH) for p in range(n)]
        for cp in copies:
            cp.start()
        for cp in copies:
            cp.wait_recv()
        for cp in copies:
            cp.wait_send()

    return pl.pallas_call(
        body, name="sibling_exchange", in_specs=[ANY] * n, out_specs=[ANY] * n,
        out_shape=[jax.ShapeDtypeStruct(a.shape, a.dtype) for a in arrs],
        scratch_shapes=[pltpu.SemaphoreType.DMA((n,)), pltpu.SemaphoreType.DMA((n,))],
    )(*arrs)


def _row_block(rows):
    return 128 if rows % 128 == 0 else rows


def sum_slabs(r, name):
    _, R, C = r.shape
    tr = _row_block(R)

    def body(r_ref, o_ref):
        a, b, c, d = (r_ref[k].astype(F32) for k in range(N_CHIPS))
        o_ref[...] = (a + b) + (c + d)

    return pl.pallas_call(
        body, name=name, grid=(R // tr,),
        in_specs=[pl.BlockSpec((N_CHIPS, tr, C), lambda i: (0, i, 0))],
        out_specs=pl.BlockSpec((tr, C), lambda i: (i, 0)),
        out_shape=jax.ShapeDtypeStruct((R, C), F32),
        compiler_params=_cparams(("arbitrary",)),
    )(r)


def adam_update(w, m, v, ga, gb, name):
    R, C = w.shape
    tr = _row_block(R)
    gs = [ga] if gb is None else [ga, gb]

    def body(*refs):
        w_ref, m_ref, v_ref = refs[:3]
        g_refs = refs[3:3 + len(gs)]
        g_out, d_out, m_out, v_out = refs[3 + len(gs):]
        g = g_refs[0][...]
        if len(gs) == 2:
            g = g + g_refs[1][...]
        mn = ADAM_B1 * m_ref[...] + (1.0 - ADAM_B1) * g
        vn = ADAM_B2 * v_ref[...] + (1.0 - ADAM_B2) * (g * g)
        m_hat = mn / (1.0 - ADAM_B1 ** ADAM_STEP)
        v_hat = vn / (1.0 - ADAM_B2 ** ADAM_STEP)
        g_out[...] = g
        d_out[...] = -ADAM_LR * (m_hat / (jnp.sqrt(v_hat) + ADAM_EPS) + ADAM_WD * w_ref[...])
        m_out[...] = mn
        v_out[...] = vn

    blk = pl.BlockSpec((tr, C), lambda i: (i, 0))
    return pl.pallas_call(
        body, name=name, grid=(R // tr,),
        in_specs=[blk] * (3 + len(gs)), out_specs=[blk] * 4,
        out_shape=[jax.ShapeDtypeStruct((R, C), F32)] * 4,
        compiler_params=_cparams(("arbitrary",)),
    )(w, m, v, *gs)


WEIGHTS = ("norm_even", "w_in_even", "s5_lam_re", "s5_lam_im", "s5_log_dt", "s5_b_re", "s5_b_im", "s5_c_re",
           "s5_c_im", "s5_d", "s5_w_glu", "s5_b_glu", "ret_gn_gain", "w_out_even", "norm_odd", "w_in_odd",
           "sgu_norm_gain", "sgu_w_spatial", "sgu_b_spatial", "w_out_odd", "final_norm")
MATRICES = ("w_in_even", "s5_w_glu", "w_out_even", "w_in_odd", "w_out_odd")
SHARDED_VECS = ("norm_odd", "sgu_norm_gain")
REPLICATED = tuple(n for n in WEIGHTS if n not in MATRICES and n not in SHARDED_VECS)
LATE = ("norm_even",)
EARLY = tuple(n for n in REPLICATED if n not in LATE)
PACKED = EARLY + SHARDED_VECS
LANES = 128


def _pack(parts):
    flat = jnp.concatenate([p.reshape(-1) for p in parts])
    rows = -(-flat.shape[0] // (8 * LANES)) * 8
    return jnp.pad(flat, (0, rows * LANES - flat.shape[0])).reshape(rows, LANES)


def kernel(x, norm_even, w_in_even, s5_lam_re, s5_lam_im, s5_log_dt, s5_b_re, s5_b_im, s5_c_re, s5_c_im, s5_d, s5_w_glu, s5_b_glu, ret_gn_gain, w_out_even, norm_odd, w_in_odd, sgu_norm_gain, sgu_w_spatial, sgu_b_spatial, w_out_odd, final_norm, loss_target, m_norm_even, m_w_in_even, m_s5_lam_re, m_s5_lam_im, m_s5_log_dt, m_s5_b_re, m_s5_b_im, m_s5_c_re, m_s5_c_im, m_s5_d, m_s5_w_glu, m_s5_b_glu, m_ret_gn_gain, m_w_out_even, m_norm_odd, m_w_in_odd, m_sgu_norm_gain, m_sgu_w_spatial, m_sgu_b_spatial, m_w_out_odd, m_final_norm, v_norm_even, v_w_in_even, v_s5_lam_re, v_s5_lam_im, v_s5_log_dt, v_s5_b_re, v_s5_b_im, v_s5_c_re, v_s5_c_im, v_s5_d, v_s5_w_glu, v_s5_b_glu, v_ret_gn_gain, v_w_out_even, v_norm_odd, v_w_in_odd, v_sgu_norm_gain, v_sgu_w_spatial, v_sgu_b_spatial, v_w_out_odd, v_final_norm):
    w = dict(norm_even=norm_even, w_in_even=w_in_even, s5_lam_re=s5_lam_re, s5_lam_im=s5_lam_im, s5_log_dt=s5_log_dt, s5_b_re=s5_b_re, s5_b_im=s5_b_im, s5_c_re=s5_c_re, s5_c_im=s5_c_im, s5_d=s5_d, s5_w_glu=s5_w_glu, s5_b_glu=s5_b_glu, ret_gn_gain=ret_gn_gain, w_out_even=w_out_even, norm_odd=norm_odd, w_in_odd=w_in_odd, sgu_norm_gain=sgu_norm_gain, sgu_w_spatial=sgu_w_spatial, sgu_b_spatial=sgu_b_spatial, w_out_odd=w_out_odd, final_norm=final_norm)
    m = dict(norm_even=m_norm_even, w_in_even=m_w_in_even, s5_lam_re=m_s5_lam_re, s5_lam_im=m_s5_lam_im, s5_log_dt=m_s5_log_dt, s5_b_re=m_s5_b_re, s5_b_im=m_s5_b_im, s5_c_re=m_s5_c_re, s5_c_im=m_s5_c_im, s5_d=m_s5_d, s5_w_glu=m_s5_w_glu, s5_b_glu=m_s5_b_glu, ret_gn_gain=m_ret_gn_gain, w_out_even=m_w_out_even, norm_odd=m_norm_odd, w_in_odd=m_w_in_odd, sgu_norm_gain=m_sgu_norm_gain, sgu_w_spatial=m_sgu_w_spatial, sgu_b_spatial=m_sgu_b_spatial, w_out_odd=m_w_out_odd, final_norm=m_final_norm)
    v = dict(norm_even=v_norm_even, w_in_even=v_w_in_even, s5_lam_re=v_s5_lam_re, s5_lam_im=v_s5_lam_im, s5_log_dt=v_s5_log_dt, s5_b_re=v_s5_b_re, s5_b_im=v_s5_b_im, s5_c_re=v_s5_c_re, s5_c_im=v_s5_c_im, s5_d=v_s5_d, s5_w_glu=v_s5_w_glu, s5_b_glu=v_s5_b_glu, ret_gn_gain=v_ret_gn_gain, w_out_even=v_w_out_even, norm_odd=v_norm_odd, w_in_odd=v_w_in_odd, sgu_norm_gain=v_sgu_norm_gain, sgu_w_spatial=v_sgu_w_spatial, sgu_b_spatial=v_sgu_b_spatial, w_out_odd=v_w_out_odd, final_norm=v_final_norm)
    me = 2 * lax.axis_index("x") + lax.axis_index("y")

    grad_x, g, landed = local_grads(x[0], loss_target[0], w)

    (landed["late"],) = run_plan(reduce_plan([], [_pack([g[n] for n in LATE])]), "exchange_late")
    part = [sum_slabs(landed[n], "sum_" + n) for n in MATRICES + ("packed", "late")]
    other = sibling_exchange(part)

    out_g, out_d, out_m, out_v = {}, {}, {}, {}
    for k, n in enumerate(MATRICES):
        res = adam_update(w[n][0], m[n][0], v[n][0], part[k], other[k], "adam_" + n)
        out_g[n], out_d[n], out_m[n], out_v[n] = (r[None] for r in res)
    zeros = [jnp.zeros_like(g[n]) for n in SHARDED_VECS] + [jnp.zeros((1,), F32)]
    res = adam_update(_pack([w[n] for n in EARLY] + zeros), _pack([m[n] for n in EARLY] + zeros),
                      _pack([v[n] for n in EARLY] + zeros), part[-2], other[-2], "adam_packed")
    flat = [r.reshape(-1) for r in res]
    off = 0
    vec_grads = []
    for n in PACKED:
        size = int(np.prod(g[n].shape))
        if n in EARLY:
            for dst, f in zip((out_g, out_d, out_m, out_v), flat):
                dst[n] = f[off:off + size].reshape(w[n].shape)
        else:
            vec_grads.append(lax.dynamic_slice(flat[0], (off + me * w[n].shape[1],), (w[n].shape[1],)))
        off += size
    total_loss = flat[0][off]
    late_size = sum(int(np.prod(g[n].shape)) for n in LATE)
    vecs = SHARDED_VECS + LATE
    res = adam_update(_pack([w[n] for n in vecs]), _pack([m[n] for n in vecs]), _pack([v[n] for n in vecs]),
                      _pack(vec_grads + [part[-1].reshape(-1)[:late_size]]),
                      _pack([jnp.zeros_like(vg) for vg in vec_grads] + [other[-1].reshape(-1)[:late_size]]),
                      "adam_vecs")
    flat = [r.reshape(-1) for r in res]
    off = 0
    for n in vecs:
        size = int(np.prod(w[n].shape))
        for dst, f in zip((out_g, out_d, out_m, out_v), flat):
            dst[n] = f[off:off + size].reshape(w[n].shape)
        off += size

    return (total_loss, grad_x[None], *[out_g[n] for n in WEIGHTS], *[out_d[n] for n in WEIGHTS],
            *[out_m[n] for n in WEIGHTS], *[out_v[n] for n in WEIGHTS])
```

```python
import functools
import math

import numpy as np
import jax
import jax.numpy as jnp
from jax import lax
from jax.experimental import pallas as pl
from jax.experimental.pallas import tpu as pltpu

F32 = jnp.float32
MXU_DTYPE = jnp.bfloat16
NORM_EPS = 1e-6
D_MODEL = 1024
S5_WIDTH = 1024
S5_GROUP = 16
S5_GROUPS = 64
S5_STATE = 64
S5_LANES = S5_GROUPS * S5_STATE
S5_KBLK = 8
RET_HEADS = 4
RET_DK = 256
RET_CHUNK = 128
ROPE_BASE = 10000.0
SGU_WIDTH = 2048
SGU_GROUPS = 4
SGU_GDIM = 512
SGU_CHUNK = 128
EVEN_IN = 6144
ODD_IN = 6144
ADAM_LR = 0.001
ADAM_B1 = 0.9
ADAM_B2 = 0.999
ADAM_EPS = 1e-08
ADAM_WD = 0.01
ADAM_STEP = 10
N_CHIPS = 4
VMEM_LIMIT = 56 * 1024 * 1024

TL_PROJ = 512
TL_DW = 1024
TL_S5 = 128
TL_SGU = 128


def _cparams(sem, **kw):
    return pltpu.CompilerParams(dimension_semantics=sem, vmem_limit_bytes=VMEM_LIMIT, **kw)


def _mm(a, b):
    return jnp.dot(a.astype(MXU_DTYPE), b.astype(MXU_DTYPE), preferred_element_type=F32)


def _mm_nt(a, b):
    return lax.dot_general(a.astype(MXU_DTYPE), b.astype(MXU_DTYPE),
                           (((1,), (1,)), ((), ())), preferred_element_type=F32)


def _mm_tn(a, b):
    return lax.dot_general(a.astype(MXU_DTYPE), b.astype(MXU_DTYPE),
                           (((0,), (0,)), ((), ())), preferred_element_type=F32)


_GELU_C = math.sqrt(2.0 / math.pi)


def _gelu(x):
    return 0.5 * x * (1.0 + jnp.tanh(_GELU_C * (x + 0.044715 * x * x * x)))


def _gelu_grad(x):
    th = jnp.tanh(_GELU_C * (x + 0.044715 * x * x * x))
    return 0.5 * (1.0 + th) + 0.5 * x * (1.0 - th * th) * _GELU_C * (1.0 + 3.0 * 0.044715 * x * x)


def _sigmoid(x):
    return 1.0 / (1.0 + jnp.exp(-x))


def _silu_and_grad(x):
    s = _sigmoid(x)
    return x * s, s * (1.0 + x * (1.0 - s))


def _rms(x):
    return lax.rsqrt(jnp.mean(x * x, axis=-1, keepdims=True) + NORM_EPS)


def _full(shape):
    nd = len(shape)
    return pl.BlockSpec(shape, lambda *_: (0,) * nd)


MESH = pl.DeviceIdType.MESH
ANY = pl.BlockSpec(memory_space=pl.ANY)


def _place():
    return lax.axis_index("x"), lax.axis_index("y"), lax.axis_index("c")


def _chip_peer(x, y, c, d):
    return (1 - x if d >= 2 else x, 1 - y if d % 2 else y, c)


class _Plan:
    def __init__(self, inputs, out_shape, build):
        self.inputs, self.out_shape, self._build = list(inputs), list(out_shape), build
        n = len(self.inputs)
        self.sems = [pltpu.SemaphoreType.DMA((n, 3)), pltpu.SemaphoreType.DMA((n, 3)), pltpu.SemaphoreType.DMA((n,))]

    def start(self, in_refs, out_refs, sems):
        send, recv, local = self._build(in_refs, out_refs, sems)
        for p in range(len(self.inputs)):
            local[p].start()
            for cp in send[p]:
                cp.start()

    def wait(self, in_refs, out_refs, sems):
        send, recv, local = self._build(in_refs, out_refs, sems)
        for p in range(len(self.inputs)):
            for cp in recv[p]:
                cp.wait_recv()
        for p in range(len(self.inputs)):
            for cp in send[p]:
                cp.wait_send()
            local[p].wait()


def gather_plan(shards):
    def build(in_refs, out_refs, sems):
        send_sems, recv_sems, loc_sems = sems
        x, y, c = _place()
        me = 2 * x + y

        def remote(p, d, slab):
            return pltpu.make_async_remote_copy(
                src_ref=in_refs[p], dst_ref=out_refs[p].at[slab], send_sem=send_sems.at[p, d - 1],
                recv_sem=recv_sems.at[p, d - 1], device_id=_chip_peer(x, y, c, d), device_id_type=MESH)

        n = len(in_refs)
        send = [[remote(p, d, me) for d in (1, 2, 3)] for p in range(n)]
        recv = [[remote(p, d, me ^ d) for d in (1, 2, 3)] for p in range(n)]
        local = [pltpu.make_async_copy(in_refs[p], out_refs[p].at[me], loc_sems.at[p]) for p in range(n)]
        return send, recv, local

    return _Plan(shards, [jax.ShapeDtypeStruct((N_CHIPS,) + s.shape, s.dtype) for s in shards], build)


def reduce_plan(shards, whole=()):
    n_s = len(shards)

    def build(in_refs, out_refs, sems):
        send_sems, recv_sems, loc_sems = sems
        x, y, c = _place()
        me = 2 * x + y

        def src(p, slab):
            return in_refs[p].at[slab] if p < n_s else in_refs[p]

        def remote(p, d):
            return pltpu.make_async_remote_copy(
                src_ref=src(p, me ^ d), dst_ref=out_refs[p].at[d], send_sem=send_sems.at[p, d - 1],
                recv_sem=recv_sems.at[p, d - 1], device_id=_chip_peer(x, y, c, d), device_id_type=MESH)

        n = len(in_refs)
        send = [[remote(p, d) for d in (1, 2, 3)] for p in range(n)]
        local = [pltpu.make_async_copy(src(p, me), out_refs[p].at[0], loc_sems.at[p]) for p in range(n)]
        return send, send, local

    outs = [jax.ShapeDtypeStruct(s.shape, s.dtype) for s in shards]
    outs += [jax.ShapeDtypeStruct((N_CHIPS,) + a.shape, a.dtype) for a in whole]
    return _Plan(list(shards) + list(whole), outs, build)


def run_plan(plan, name):
    n = len(plan.inputs)

    def body(*refs):
        plan.start(refs[:n], refs[n:2 * n], refs[2 * n:])
        plan.wait(refs[:n], refs[n:2 * n], refs[2 * n:])

    return pl.pallas_call(body, name=name, in_specs=[ANY] * n, out_specs=[ANY] * n, out_shape=plan.out_shape,
                          scratch_shapes=plan.sems)(*plan.inputs)


def _call(body, plan, *, name, grid, in_specs, out_specs, out_shape, sem, scratch_shapes=()):
    single = not isinstance(out_shape, (list, tuple))
    out_specs = [out_specs] if single else list(out_specs)
    out_shape = [out_shape] if single else list(out_shape)
    n_in, n_out, n_scr = len(in_specs), len(out_specs), len(scratch_shapes)
    ci = 0 if plan is None else len(plan.inputs)

    def hosted(*refs):
        ins, cins = refs[:n_in], refs[n_in:n_in + ci]
        k = n_in + ci
        outs, couts = refs[k:k + n_out], refs[k + n_out:k + n_out + ci]
        k += n_out + ci
        scr, sems = refs[k:k + n_scr], refs[k + n_scr:]
        ids = [pl.program_id(a) for a in range(len(grid))]
        first = functools.reduce(jnp.logical_and, [i == 0 for i in ids])
        last = functools.reduce(jnp.logical_and, [i == g - 1 for i, g in zip(ids, grid)])

        @pl.when(first)
        def _():
            plan.start(cins, couts, sems)

        body(*ins, *outs, *scr)

        @pl.when(last)
        def _():
            plan.wait(cins, couts, sems)

    def run(*args):
        if plan is None:
            res = pl.pallas_call(body, name=name, grid=grid, in_specs=list(in_specs), out_specs=out_specs,
                                 out_shape=out_shape, scratch_shapes=list(scratch_shapes),
                                 compiler_params=_cparams(sem))(*args)
            return (res[0] if single else res), []
        res = pl.pallas_call(hosted, name=name, grid=grid, in_specs=list(in_specs) + [ANY] * ci,
                             out_specs=out_specs + [ANY] * ci, out_shape=out_shape + plan.out_shape,
                             scratch_shapes=list(scratch_shapes) + plan.sems,
                             compiler_params=_cparams(sem))(*args, *plan.inputs)
        return (res[0] if single else res[:n_out]), list(res[n_out:])

    return run


def norm_matmul(x, g, w, name, plan=None, tn=None):
    L, D = x.shape
    tl = min(TL_PROJ, L)
    if w.ndim == 3:
        nt, _, tn = w.shape
        w_spec = pl.BlockSpec((1, D, tn), lambda i, n: (n, 0, 0))
    else:
        nt = w.shape[1] // tn
        w_spec = pl.BlockSpec((D, tn), lambda i, n: (0, n))

    def body(x_ref, g_ref, w_ref, o_ref, h_ref):
        xv = x_ref[...]
        h = (xv * _rms(xv) * g_ref[...]).astype(h_ref.dtype)
        h_ref[...] = h
        o_ref[...] = _mm(h, w_ref[0] if w.ndim == 3 else w_ref[...])

    return _call(
        body, plan, name=name, grid=(L // tl, nt),
        in_specs=[pl.BlockSpec((tl, D), lambda i, n: (i, 0)), _full((1, D)), w_spec],
        out_specs=[pl.BlockSpec((tl, tn), lambda i, n: (i, n)), pl.BlockSpec((tl, D), lambda i, n: (i, 0))],
        out_shape=[jax.ShapeDtypeStruct((L, nt * tn), F32), jax.ShapeDtypeStruct((L, D), MXU_DTYPE)],
        sem=("arbitrary", "arbitrary"),
    )(x, g, w)


def matmul_residual(ys, w, x, name):
    L, D = x.shape
    tl = min(TL_PROJ, L)
    n = len(ys)
    offs = np.cumsum([0] + [y.shape[1] for y in ys])

    def body(*refs):
        y_refs, w_ref, x_ref, o_ref = refs[:n], refs[n], refs[n + 1], refs[n + 2]
        acc = x_ref[...]
        for k in range(n):
            acc = acc + _mm(y_refs[k][...], w_ref[offs[k]:offs[k + 1], :])
        o_ref[...] = acc

    return pl.pallas_call(
        body, name=name, grid=(L // tl,),
        in_specs=[pl.BlockSpec((tl, y.shape[1]), lambda i: (i, 0)) for y in ys]
        + [_full(w.shape), pl.BlockSpec((tl, D), lambda i: (i, 0))],
        out_specs=pl.BlockSpec((tl, D), lambda i: (i, 0)),
        out_shape=jax.ShapeDtypeStruct((L, D), F32),
        compiler_params=_cparams(("arbitrary",)),
    )(*ys, w, x)


def out_proj_loss(y, w, x, gf, tgt, name):
    L, K = y.shape
    D = w.shape[1]
    tl = min(TL_PROJ, L)

    def body(y_ref, w_ref, x_ref, gf_ref, t_ref, dx_ref, loss_ref, dg_ref):
        @pl.when(pl.program_id(0) == 0)
        def _():
            loss_ref[...] = jnp.zeros_like(loss_ref)
            dg_ref[...] = jnp.zeros_like(dg_ref)

        x2 = x_ref[...] + _mm(y_ref[...], w_ref[...])
        r = _rms(x2)
        xn = x2 * r
        e = xn * gf_ref[...] - t_ref[...]
        loss_ref[...] += (0.5 / D) * jnp.sum(e * e)
        dout = e * (1.0 / D)
        dg_ref[...] += jnp.sum(dout * xn, axis=0, keepdims=True)
        dxn = dout * gf_ref[...]
        dx_ref[...] = r * (dxn - xn * jnp.mean(dxn * xn, axis=-1, keepdims=True))

    return pl.pallas_call(
        body, name=name, grid=(L // tl,),
        in_specs=[pl.BlockSpec((tl, K), lambda i: (i, 0)), _full((K, D)),
                  pl.BlockSpec((tl, D), lambda i: (i, 0)), _full((1, D)),
                  pl.BlockSpec((tl, D), lambda i: (i, 0))],
        out_specs=[pl.BlockSpec((tl, D), lambda i: (i, 0)), _full((8, 128)), _full((1, D))],
        out_shape=[jax.ShapeDtypeStruct((L, D), F32), jax.ShapeDtypeStruct((8, 128), F32),
                   jax.ShapeDtypeStruct((1, D), F32)],
        compiler_params=_cparams(("arbitrary",)),
    )(y, w, x, gf, tgt)


def out_proj_bwd(dx, w, ys, name):
    L, D = dx.shape
    K = w.shape[0]
    tl = min(TL_PROJ, L)
    n = len(ys)
    offs = np.cumsum([0] + [y.shape[1] for y in ys])

    def body(*refs):
        dx_ref, w_ref, y_refs = refs[0], refs[1], refs[2:2 + n]
        dy_refs, dw_ref = refs[2 + n:2 + 2 * n], refs[2 + 2 * n]

        @pl.when(pl.program_id(0) == 0)
        def _():
            dw_ref[...] = jnp.zeros_like(dw_ref)

        dxv = dx_ref[...]
        for k in range(n):
            dy_refs[k][...] = _mm_nt(dxv, w_ref[offs[k]:offs[k + 1], :])
            dw_ref[offs[k]:offs[k + 1], :] += _mm_tn(y_refs[k][...], dxv)

    y_specs = [pl.BlockSpec((tl, y.shape[1]), lambda i: (i, 0)) for y in ys]
    return pl.pallas_call(
        body, name=name, grid=(L // tl,),
        in_specs=[pl.BlockSpec((tl, D), lambda i: (i, 0)), _full((K, D))] + y_specs,
        out_specs=y_specs + [_full((K, D))],
        out_shape=[jax.ShapeDtypeStruct(y.shape, F32) for y in ys] + [jax.ShapeDtypeStruct((K, D), F32)],
        compiler_params=_cparams(("arbitrary",)),
    )(dx, w, *ys)


def in_proj_bwd_dx(x, g, dps, ws, dres, name, plan=None):
    L, D = x.shape
    tl = min(TL_PROJ, L)
    n = len(dps)

    def body(*refs):
        x_ref, g_ref, dres_ref = refs[:3]
        dp_refs, w_refs = refs[3:3 + n], refs[3 + n:3 + 2 * n]
        dx_ref, dg_ref = refs[3 + 2 * n:]

        @pl.when(pl.program_id(0) == 0)
        def _():
            dg_ref[...] = jnp.zeros_like(dg_ref)

        dh = None
        for dp_ref, w_ref, w in zip(dp_refs, w_refs, ws):
            if w.ndim == 3:
                tn = w.shape[2]
                parts = [_mm_nt(dp_ref[:, tn * k:tn * (k + 1)], w_ref[k]) for k in range(w.shape[0])]
            else:
                parts = [_mm_nt(dp_ref[...], w_ref[...])]
            for part in parts:
                dh = part if dh is None else dh + part
        xv = x_ref[...]
        r = _rms(xv)
        xn = xv * r
        dg_ref[...] += jnp.sum(dh * xn, axis=0, keepdims=True)
        dxn = dh * g_ref[...]
        dx_ref[...] = dres_ref[...] + r * (dxn - xn * jnp.mean(dxn * xn, axis=-1, keepdims=True))

    return _call(
        body, plan, name=name, grid=(L // tl,),
        in_specs=[pl.BlockSpec((tl, D), lambda i: (i, 0)), _full((1, D)), pl.BlockSpec((tl, D), lambda i: (i, 0))]
        + [pl.BlockSpec((tl, dp.shape[1]), lambda i: (i, 0)) for dp in dps] + [_full(w.shape) for w in ws],
        out_specs=[pl.BlockSpec((tl, D), lambda i: (i, 0)), _full((1, D))],
        out_shape=[jax.ShapeDtypeStruct((L, D), F32), jax.ShapeDtypeStruct((1, D), F32)],
        sem=("arbitrary",),
    )(x, g, dres, *dps, *ws)


def in_proj_bwd_dw(h, dp, name, tn, first=0, into=None, dtype=F32):
    L, D = h.shape
    tl = min(TL_DW, L)
    wb = EVEN_IN // N_CHIPS
    per = wb // tn
    count = dp.shape[1] // tn
    last = L // tl - 1

    def body(*refs):
        h_ref, dp_ref, dw_ref, acc = refs[0], refs[1], refs[-2], refs[-1]

        @pl.when(pl.program_id(1) == 0)
        def _():
            acc[...] = jnp.zeros_like(acc)

        acc[...] += _mm_tn(h_ref[...], dp_ref[...])

        @pl.when(pl.program_id(1) == last)
        def _():
            dw_ref[0] = acc[...].astype(dw_ref.dtype)

    ins = [h, dp] + ([] if into is None else [into])
    return pl.pallas_call(
        body, name=name, grid=(count, L // tl),
        in_specs=[pl.BlockSpec((tl, D), lambda n, i: (i, 0)), pl.BlockSpec((tl, tn), lambda n, i: (i, n))]
        + ([] if into is None else [ANY]),
        out_specs=pl.BlockSpec((1, D, tn), lambda n, i: ((n + first) // per, 0, (n + first) % per)),
        out_shape=jax.ShapeDtypeStruct((N_CHIPS, D, wb), dtype),
        scratch_shapes=[pltpu.VMEM((D, tn), F32)],
        input_output_aliases={} if into is None else {2: 0},
        compiler_params=_cparams(("arbitrary", "arbitrary")),
    )(*ins)


def _s5_param_fn(lam_re, lam_im, log_dt, b_re, b_im):
    lr = jnp.minimum(lam_re, -1e-4)
    li = lam_im
    dt = jnp.exp(log_dt)
    mag = jnp.exp(lr * dt)
    ab_re = mag * jnp.cos(li * dt)
    ab_im = mag * jnp.sin(li * dt)
    den = lr * lr + li * li
    n_re = ab_re - 1.0
    n_im = ab_im
    z_re = (n_re * lr + n_im * li) / den
    z_im = (n_im * lr - n_re * li) / den
    bb_re = z_re[None] * b_re - z_im[None] * b_im
    bb_im = z_re[None] * b_im + z_im[None] * b_re
    return ab_re, ab_im, bb_re, bb_im


def s5_params_fwd(lam_re, lam_im, log_dt, b_re, b_im, span):
    G, P = lam_re.shape
    H = b_re.shape[0]
    assert span & (span - 1) == 0

    def body(lr_ref, li_ref, dt_ref, br_ref, bi_ref, abr_ref, abi_ref, bbr_ref, bbi_ref, pr_ref, pi_ref):
        ab_re, ab_im, bb_re, bb_im = _s5_param_fn(lr_ref[...], li_ref[...], dt_ref[...], br_ref[...], bi_ref[...])
        abr_ref[...] = ab_re
        abi_ref[...] = ab_im
        bbr_ref[...] = bb_re
        bbi_ref[...] = bb_im
        cr, ci = ab_re, ab_im
        for _ in range(span.bit_length() - 1):
            cr, ci = cr * cr - ci * ci, 2.0 * cr * ci
        pr_ref[...] = cr
        pi_ref[...] = ci

    shp = lambda *s: jax.ShapeDtypeStruct(s, F32)
    return pl.pallas_call(
        body, name="s5_params_fwd",
        out_shape=[shp(G, P), shp(G, P), shp(H, G, P), shp(H, G, P), shp(G, P), shp(G, P)],
    )(lam_re, lam_im, log_dt, b_re, b_im)


def s5_params_bwd(lam_re, lam_im, log_dt, b_re, b_im, d_ab_re, d_ab_im, d_bb_re, d_bb_im):
    G, P = lam_re.shape
    H = b_re.shape[0]

    def body(lr_ref, li_ref, dt_ref, br_ref, bi_ref, g0, g1, g2, g3, o0, o1, o2, o3, o4):
        prim = (lr_ref[...], li_ref[...], dt_ref[...], br_ref[...], bi_ref[...])
        _, vjp = jax.vjp(_s5_param_fn, *prim)
        d = vjp((jnp.sum(g0[...], axis=0), jnp.sum(g1[...], axis=0), g2[...], g3[...]))
        o0[...], o1[...], o2[...], o3[...], o4[...] = d

    shp = lambda *s: jax.ShapeDtypeStruct(s, F32)
    return pl.pallas_call(
        body, name="s5_params_bwd",
        out_shape=[shp(G, P), shp(G, P), shp(G, 1), shp(H, G, P), shp(H, G, P)],
    )(lam_re, lam_im, log_dt, b_re, b_im, d_ab_re, d_ab_im, d_bb_re, d_bb_im)


def stream_order(a, tl):
    L, C = a.shape
    return a.reshape(L // tl, 8, tl // 8, C).transpose(0, 2, 1, 3).reshape(L, C)


def token_order(a, tl):
    L, C = a.shape
    return a.reshape(L // tl, tl // 8, 8, C).transpose(0, 2, 1, 3).reshape(L, C)


_LANE_BLK = 1024


def _cmul_add(ar, ai, xr, xi, br, bi):
    return br + (ar * xr - ai * xi), bi + (ar * xi + ai * xr)


def _cmulc_add(ar, ai, xr, xi, br, bi):
    return br + (ar * xr + ai * xi), bi + (ar * xi - ai * xr)


def _s5_states(u, wbd_ref, a_re, a_im, at_re, at_im, s_re, s_im, e_re, e_im, c0_re, c0_im, tl):
    t8 = tl // 8
    for k in range(S5_KBLK):
        bu = _mm(u[:, 128 * k:128 * (k + 1)], wbd_ref[k])
        s_re[:, 512 * k:512 * (k + 1)] = bu[:, :512]
        s_im[:, 512 * k:512 * (k + 1)] = bu[:, 512:]
    outs_re, outs_im = [], []
    for b in range(S5_LANES // _LANE_BLK):
        lanes = slice(_LANE_BLK * b, _LANE_BLK * (b + 1))
        ar = jnp.broadcast_to(a_re[:, lanes], (8, _LANE_BLK))
        ai = jnp.broadcast_to(a_im[:, lanes], (8, _LANE_BLK))

        def local(i, carry, lanes=lanes, ar=ar, ai=ai):
            r = pl.multiple_of(i * 8, 8)
            sr, si = _cmul_add(ar, ai, carry[0], carry[1], s_re[pl.ds(r, 8), lanes], s_im[pl.ds(r, 8), lanes])
            s_re[pl.ds(r, 8), lanes] = sr
            s_im[pl.ds(r, 8), lanes] = si
            return sr, si

        zero = jnp.zeros((8, _LANE_BLK), F32)
        fr, fi = lax.fori_loop(0, t8, local, (zero, zero), unroll=True)
        tr, ti = at_re[:, lanes], at_im[:, lanes]
        er, ei = c0_re[:, lanes], c0_im[:, lanes]
        ers, eis = [er], [ei]
        for j in range(8):
            er, ei = _cmul_add(tr, ti, er, ei, fr[j:j + 1], fi[j:j + 1])
            ers.append(er)
            eis.append(ei)
        outs_re.append(ers[8])
        outs_im.append(eis[8])
        ent_r, ent_i = jnp.concatenate(ers[:8], axis=0), jnp.concatenate(eis[:8], axis=0)
        e_re[:, lanes] = ent_r
        e_im[:, lanes] = ent_i

        def fix(i, carry, lanes=lanes, ar=ar, ai=ai):
            r = pl.multiple_of(i * 8, 8)
            zr, zi = ar * carry[0] - ai * carry[1], ar * carry[1] + ai * carry[0]
            s_re[pl.ds(r, 8), lanes] = s_re[pl.ds(r, 8), lanes] + zr
            s_im[pl.ds(r, 8), lanes] = s_im[pl.ds(r, 8), lanes] + zi
            return zr, zi

        lax.fori_loop(0, t8, fix, (ent_r, ent_i), unroll=True)
    return jnp.concatenate(outs_re, axis=1), jnp.concatenate(outs_im, axis=1)


def _s5_readout(s_re, s_im, cre_ref, cim_ref):
    ys = []
    for k in range(S5_KBLK):
        lanes = slice(512 * k, 512 * (k + 1))
        ys.append(_mm(s_re[:, lanes], cre_ref[k]) - _mm(s_im[:, lanes], cim_ref[k]))
    return jnp.concatenate(ys, axis=1)


def s5_forward(p, wbd, cre, cim, atab, d_skip, w_glu, b_glu, plan=None):
    L = p.shape[0]
    tl = min(TL_S5, L)
    nch = L // tl

    def body(u_ref, z_ref, wbd_ref, cre_ref, cim_ref, at_ref, d_ref, wg_ref, bg_ref,
             ya_ref, st_re_ref, st_im_ref, s_re, s_im, e_re, e_im, car_re, car_im):
        @pl.when(pl.program_id(0) == 0)
        def _():
            car_re[...] = jnp.zeros_like(car_re)
            car_im[...] = jnp.zeros_like(car_im)

        c0_re, c0_im = car_re[...], car_im[...]
        st_re_ref[0] = c0_re
        st_im_ref[0] = c0_im
        u = u_ref[...]
        x_re, x_im = _s5_states(u, wbd_ref, at_ref[0:1], at_ref[1:2], at_ref[2:3], at_ref[3:4],
                                s_re, s_im, e_re, e_im, c0_re, c0_im, tl)
        car_re[...] = x_re
        car_im[...] = x_im
        y = _s5_readout(s_re, s_im, cre_ref, cim_ref) + d_ref[...] * u
        yg = _gelu(y)
        gate = _sigmoid(_mm(yg, wg_ref[...]) + bg_ref[...])
        sz, _ = _silu_and_grad(z_ref[...])
        ya_ref[...] = (yg * gate * sz).astype(ya_ref.dtype)

    return _call(
        body, plan, name="s5_forward", grid=(nch,),
        in_specs=[pl.BlockSpec((tl, 1024), lambda i: (i, 0)), pl.BlockSpec((tl, 1024), lambda i: (i, 1)),
                  _full(wbd.shape), _full(cre.shape), _full(cim.shape), _full(atab.shape),
                  _full((1, 1024)), _full((1024, 1024)), _full((1, 1024))],
        out_specs=[pl.BlockSpec((tl, 1024), lambda i: (i, 0)),
                   pl.BlockSpec((1, 1, S5_LANES), lambda i: (i, 0, 0)),
                   pl.BlockSpec((1, 1, S5_LANES), lambda i: (i, 0, 0))],
        out_shape=[jax.ShapeDtypeStruct((L, 1024), MXU_DTYPE),
                   jax.ShapeDtypeStruct((nch, 1, S5_LANES), F32), jax.ShapeDtypeStruct((nch, 1, S5_LANES), F32)],
        scratch_shapes=[pltpu.VMEM((tl, S5_LANES), F32), pltpu.VMEM((tl, S5_LANES), F32),
                        pltpu.VMEM((8, S5_LANES), F32), pltpu.VMEM((8, S5_LANES), F32),
                        pltpu.VMEM((1, S5_LANES), F32), pltpu.VMEM((1, S5_LANES), F32)],
        sem=("arbitrary",),
    )(p, p, wbd, cre, cim, atab, d_skip, w_glu, b_glu)


def s5_backward(p, dya, st_re, st_im, wbd, cre, cim, atab, d_skip, w_glu, b_glu, plan=None):
    L = p.shape[0]
    tl = min(TL_S5, L)
    t8 = tl // 8
    nch = L // tl
    rev = lambda i: (nch - 1 - i, 0)
    rev1 = lambda i: (nch - 1 - i, 1)
    rev3 = lambda i: (nch - 1 - i, 0, 0)

    def body(u_ref, z_ref, dya_ref, str_ref, sti_ref, wbd_ref, cre_ref, cim_ref, at_ref,
             d_ref, wg_ref, bg_ref,
             dp_ref, dwbd_ref, dcre_ref, dcim_ref, dabr_ref, dabi_ref, dd_ref, dwg_ref, dbg_ref,
             s_re, s_im, g_re, g_im, e_re, e_im, car_re, car_im):
        @pl.when(pl.program_id(0) == 0)
        def _():
            car_re[...] = jnp.zeros_like(car_re)
            car_im[...] = jnp.zeros_like(car_im)
            for r in (dwbd_ref, dcre_ref, dcim_ref, dabr_ref, dabi_ref, dd_ref, dwg_ref, dbg_ref):
                r[...] = jnp.zeros_like(r)

        u = u_ref[...]
        a_re, a_im, at_re, at_im = at_ref[0:1], at_ref[1:2], at_ref[2:3], at_ref[3:4]
        _s5_states(u, wbd_ref, a_re, a_im, at_re, at_im, s_re, s_im, e_re, e_im, str_ref[0], sti_ref[0], tl)
        y = _s5_readout(s_re, s_im, cre_ref, cim_ref) + d_ref[...] * u
        yg = _gelu(y)
        gate = _sigmoid(_mm(yg, wg_ref[...]) + bg_ref[...])
        sz, dsz = _silu_and_grad(z_ref[...])
        dya = dya_ref[...]
        s5out = yg * gate
        dp_ref[:, 1024:] = (dya * s5out * dsz).astype(dp_ref.dtype)
        ds5 = dya * sz
        dt = ds5 * yg * gate * (1.0 - gate)
        dwg_ref[...] += _mm_tn(yg, dt)
        dbg_ref[...] += jnp.sum(dt, axis=0, keepdims=True)
        dyv = (ds5 * gate + _mm_nt(dt, wg_ref[...])) * _gelu_grad(y)
        dd_ref[...] += jnp.sum(dyv * u, axis=0, keepdims=True)

        for k in range(S5_KBLK):
            lanes = slice(512 * k, 512 * (k + 1))
            dyk = dyv[:, 128 * k:128 * (k + 1)]
            g_re[:, lanes] = _mm_nt(dyk, cre_ref[k])
            g_im[:, lanes] = -_mm_nt(dyk, cim_ref[k])
            dcre_ref[k] += _mm_tn(s_re[:, lanes], dyk)
            dcim_ref[k] -= _mm_tn(s_im[:, lanes], dyk)

        for b in range(S5_LANES // _LANE_BLK):
            lanes = slice(_LANE_BLK * b, _LANE_BLK * (b + 1))
            ar = jnp.broadcast_to(a_re[:, lanes], (8, _LANE_BLK))
            ai = jnp.broadcast_to(a_im[:, lanes], (8, _LANE_BLK))

            def local(j, carry, lanes=lanes, ar=ar, ai=ai):
                r = pl.multiple_of((t8 - 1 - j) * 8, 8)
                gr, gi = _cmulc_add(ar, ai, carry[0], carry[1], g_re[pl.ds(r, 8), lanes], g_im[pl.ds(r, 8), lanes])
                g_re[pl.ds(r, 8), lanes] = gr
                g_im[pl.ds(r, 8), lanes] = gi
                return gr, gi

            zero = jnp.zeros((8, _LANE_BLK), F32)
            fr, fi = lax.fori_loop(0, t8, local, (zero, zero), unroll=True)
            tr, ti = at_re[:, lanes], at_im[:, lanes]
            hr, hi = car_re[:, lanes], car_im[:, lanes]
            hrs, his = [hr], [hi]
            for j in range(7, -1, -1):
                hr, hi = _cmulc_add(tr, ti, hr, hi, fr[j:j + 1], fi[j:j + 1])
                hrs.append(hr)
                his.append(hi)
            car_re[:, lanes] = hrs[8]
            car_im[:, lanes] = his[8]
            in_r = jnp.concatenate(hrs[7::-1], axis=0)
            in_i = jnp.concatenate(his[7::-1], axis=0)

            def fix(j, carry, lanes=lanes, ar=ar, ai=ai):
                wr, wi, accr, acci = carry
                r = pl.multiple_of((t8 - 1 - j) * 8, 8)
                wr, wi = ar * wr + ai * wi, ar * wi - ai * wr
                gr, gi = g_re[pl.ds(r, 8), lanes] + wr, g_im[pl.ds(r, 8), lanes] + wi
                g_re[pl.ds(r, 8), lanes] = gr
                g_im[pl.ds(r, 8), lanes] = gi
                sr, si = s_re[pl.ds(r - 8, 8), lanes], s_im[pl.ds(r - 8, 8), lanes]
                return wr, wi, accr + (sr * gr + si * gi), acci + (sr * gi - si * gr)

            wr, wi, accr, acci = lax.fori_loop(0, t8 - 1, fix, (in_r, in_i, zero, zero), unroll=True)
            wr, wi = ar * wr + ai * wi, ar * wi - ai * wr
            gr, gi = g_re[pl.ds(0, 8), lanes] + wr, g_im[pl.ds(0, 8), lanes] + wi
            g_re[pl.ds(0, 8), lanes] = gr
            g_im[pl.ds(0, 8), lanes] = gi
            sr, si = e_re[:, lanes], e_im[:, lanes]
            dabr_ref[:, lanes] += accr + (sr * gr + si * gi)
            dabi_ref[:, lanes] += acci + (sr * gi - si * gr)

        dus = []
        for k in range(S5_KBLK):
            lanes = slice(512 * k, 512 * (k + 1))
            g = jnp.concatenate([g_re[:, lanes], g_im[:, lanes]], axis=1)
            dwbd_ref[k] += _mm_tn(u[:, 128 * k:128 * (k + 1)], g)
            dus.append(_mm_nt(g, wbd_ref[k]))
        du = jnp.concatenate(dus, axis=1) + dyv * d_ref[...]
        dp_ref[:, :1024] = du.astype(dp_ref.dtype)

    shp = lambda *s: jax.ShapeDtypeStruct(s, F32)
    return _call(
        body, plan, name="s5_backward", grid=(nch,),
        in_specs=[pl.BlockSpec((tl, 1024), rev), pl.BlockSpec((tl, 1024), rev1), pl.BlockSpec((tl, 1024), rev),
                  pl.BlockSpec((1, 1, S5_LANES), rev3), pl.BlockSpec((1, 1, S5_LANES), rev3),
                  _full(wbd.shape), _full(cre.shape), _full(cim.shape), _full(atab.shape),
                  _full((1, 1024)), _full((1024, 1024)), _full((1, 1024))],
        out_specs=[pl.BlockSpec((tl, 2048), rev), _full(wbd.shape), _full(cre.shape), _full(cim.shape),
                   _full((8, S5_LANES)), _full((8, S5_LANES)), _full((1, 1024)), _full((1024, 1024)), _full((1, 1024))],
        out_shape=[jax.ShapeDtypeStruct((L, 2048), MXU_DTYPE), shp(*wbd.shape), shp(*cre.shape), shp(*cim.shape),
                   shp(8, S5_LANES), shp(8, S5_LANES), shp(1, 1024), shp(1024, 1024), shp(1, 1024)],
        scratch_shapes=[pltpu.VMEM((tl, S5_LANES), F32), pltpu.VMEM((tl, S5_LANES), F32),
                        pltpu.VMEM((tl, S5_LANES), F32), pltpu.VMEM((tl, S5_LANES), F32),
                        pltpu.VMEM((8, S5_LANES), F32), pltpu.VMEM((8, S5_LANES), F32),
                        pltpu.VMEM((1, S5_LANES), F32), pltpu.VMEM((1, S5_LANES), F32)],
        sem=("arbitrary",),
    )(p, p, dya, st_re, st_im, wbd, cre, cim, atab, d_skip, w_glu, b_glu)


def _block_diag(w, rows_first):
    g8 = w.reshape(S5_KBLK, 8, w.shape[1], w.shape[2])
    eye = jnp.eye(8, dtype=w.dtype)
    out = jnp.einsum('kgab,fg->kfagb', g8, eye)
    return out.reshape(S5_KBLK, 8 * w.shape[1], 8 * w.shape[2])


def _block_diag_extract(wbd, a, b):
    w5 = wbd.reshape(S5_KBLK, 8, a, 8, b)
    idx = jnp.arange(8)
    return w5[:, idx, :, idx, :].transpose(1, 0, 2, 3).reshape(S5_GROUPS, a, b)


def _ret_constants():
    log_g = np.log1p(-np.exp2(-5.0 - np.arange(RET_HEADS, dtype=np.float32))).astype(np.float32)
    idx = np.arange(RET_CHUNK, dtype=np.float32)
    diff = idx[:, None] - idx[None, :]
    decay = np.where(diff >= 0, np.exp(log_g[:, None, None] * np.maximum(diff, 0.0)), 0.0).astype(np.float32)
    xi = np.exp(log_g[None, :] * (idx[:, None] + 1.0)).astype(np.float32)
    zeta = np.exp(log_g[None, :] * (RET_CHUNK - 1.0 - idx[:, None])).astype(np.float32)
    chunk_decay = np.exp(log_g * RET_CHUNK).astype(np.float32)
    return decay, xi, zeta, chunk_decay


def _rope_tables(L):
    half = RET_DK // 2
    inv = ROPE_BASE ** (-jnp.arange(half, dtype=F32) / half)
    ang = jnp.arange(L, dtype=F32)[:, None] * inv[None, :]
    return jnp.cos(ang), jnp.sin(ang)


def _rot(xh, cos, sin):
    x1, x2 = xh[:, :128], xh[:, 128:]
    return jnp.concatenate([x1 * cos - x2 * sin, x1 * sin + x2 * cos], axis=1)


def _rot_t(dh, cos, sin):
    d1, d2 = dh[:, :128], dh[:, 128:]
    return jnp.concatenate([d1 * cos + d2 * sin, d2 * cos - d1 * sin], axis=1)


def retention_forward(p, cos, sin, gain):
    L = p.shape[0]
    nc = L // RET_CHUNK
    decay_np, xi_np, zeta_np, cd_np = _ret_constants()
    decay, xi, zeta = jnp.asarray(decay_np), jnp.asarray(xi_np), jnp.asarray(zeta_np)
    scale = RET_DK ** -0.5

    def body(q_ref, k_ref, v_ref, z_ref, cos_ref, sin_ref, dec_ref, xi_ref, zeta_ref, gain_ref,
             yb_ref, prev_ref, state):
        @pl.when(pl.program_id(0) == 0)
        def _():
            state[...] = jnp.zeros_like(state)

        cs, sn = cos_ref[...], sin_ref[...]
        sz, _ = _silu_and_grad(z_ref[...])
        for h in range(RET_HEADS):
            hs = slice(RET_DK * h, RET_DK * (h + 1))
            qh = _rot(q_ref[:, hs], cs, sn)
            kh = _rot(k_ref[:, hs], cs, sn) * scale
            vh = v_ref[:, hs]
            prev = state[h]
            prev_ref[0, h] = prev.astype(prev_ref.dtype)
            sc = _mm_nt(qh, kh) * dec_ref[h]
            o = _mm(sc, vh) + _mm(qh * xi_ref[:, h:h + 1], prev)
            state[h] = prev * float(cd_np[h]) + _mm_tn(kh * zeta_ref[:, h:h + 1], vh)
            mu = jnp.mean(o, axis=-1, keepdims=True)
            oc = o - mu
            on = oc * lax.rsqrt(jnp.mean(oc * oc, axis=-1, keepdims=True) + NORM_EPS)
            yb_ref[:, hs] = (on * gain_ref[:, hs] * sz[:, hs]).astype(yb_ref.dtype)

    blk = lambda c: pl.BlockSpec((RET_CHUNK, 1024), lambda i, c=c: (i, c))
    return pl.pallas_call(
        body, name="retention_forward", grid=(nc,),
        in_specs=[blk(0), blk(1), blk(2), blk(3),
                  pl.BlockSpec((RET_CHUNK, 128), lambda i: (i, 0)), pl.BlockSpec((RET_CHUNK, 128), lambda i: (i, 0)),
                  _full(decay.shape), _full(xi.shape), _full(zeta.shape), _full((1, 1024))],
        out_specs=[pl.BlockSpec((RET_CHUNK, 1024), lambda i: (i, 0)),
                   pl.BlockSpec((1, RET_HEADS, RET_DK, RET_DK), lambda i: (i, 0, 0, 0))],
        out_shape=[jax.ShapeDtypeStruct((L, 1024), MXU_DTYPE),
                   jax.ShapeDtypeStruct((nc, RET_HEADS, RET_DK, RET_DK), MXU_DTYPE)],
        scratch_shapes=[pltpu.VMEM((RET_HEADS, RET_DK, RET_DK), F32)],
        compiler_params=_cparams(("arbitrary",)),
    )(p, p, p, p, cos, sin, decay, xi, zeta, gain)


def retention_backward(p, dy, prevs, cos, sin, gain, plan=None):
    L = p.shape[0]
    nc = L // RET_CHUNK
    decay_np, xi_np, zeta_np, cd_np = _ret_constants()
    decay, xi, zeta = jnp.asarray(decay_np), jnp.asarray(xi_np), jnp.asarray(zeta_np)
    scale = RET_DK ** -0.5

    def body(q_ref, k_ref, v_ref, z_ref, dyb_ref, prev_ref, cos_ref, sin_ref, dec_ref, xi_ref, zeta_ref, gain_ref,
             dp_ref, dgain_ref, dstate):
        @pl.when(pl.program_id(0) == 0)
        def _():
            dstate[...] = jnp.zeros_like(dstate)
            dgain_ref[...] = jnp.zeros_like(dgain_ref)

        cs, sn = cos_ref[...], sin_ref[...]
        sz, dsz = _silu_and_grad(z_ref[...])
        dyb = dyb_ref[...]
        for h in range(RET_HEADS):
            hs = slice(RET_DK * h, RET_DK * (h + 1))
            qh = _rot(q_ref[:, hs], cs, sn)
            kh = _rot(k_ref[:, hs], cs, sn) * scale
            vh = v_ref[:, hs]
            prev = prev_ref[0, h]
            xih, zth = xi_ref[:, h:h + 1], zeta_ref[:, h:h + 1]
            sc = _mm_nt(qh, kh) * dec_ref[h]
            o = _mm(sc, vh) + _mm(qh * xih, prev)
            mu = jnp.mean(o, axis=-1, keepdims=True)
            oc = o - mu
            rstd = lax.rsqrt(jnp.mean(oc * oc, axis=-1, keepdims=True) + NORM_EPS)
            on = oc * rstd
            gh = gain_ref[:, hs]
            dyh = dyb[:, hs]
            dp_ref[:, 3072 + RET_DK * h:3072 + RET_DK * (h + 1)] = (dyh * on * gh * dsz[:, hs]).astype(dp_ref.dtype)
            dong = dyh * sz[:, hs]
            dgain_ref[:, hs] += jnp.sum(dong * on, axis=0, keepdims=True)
            don = dong * gh
            do = rstd * (don - jnp.mean(don, axis=-1, keepdims=True)
                         - on * jnp.mean(don * on, axis=-1, keepdims=True))
            dst = dstate[h]
            dsc = _mm_nt(do, vh) * dec_ref[h]
            dqh = _mm(dsc, kh) + _mm_nt(do, prev) * xih
            dkh = _mm_tn(dsc, qh) + _mm_nt(vh, dst) * zth
            dvh = _mm_tn(sc, do) + _mm(kh * zth, dst)
            dstate[h] = dst * float(cd_np[h]) + _mm_tn(qh * xih, do)
            dp_ref[:, hs] = _rot_t(dqh, cs, sn).astype(dp_ref.dtype)
            dp_ref[:, 1024 + RET_DK * h:1024 + RET_DK * (h + 1)] = (_rot_t(dkh, cs, sn) * scale).astype(dp_ref.dtype)
            dp_ref[:, 2048 + RET_DK * h:2048 + RET_DK * (h + 1)] = dvh.astype(dp_ref.dtype)

    blk = lambda c: pl.BlockSpec((RET_CHUNK, 1024), lambda i, c=c: (nc - 1 - i, c))
    tab = pl.BlockSpec((RET_CHUNK, 128), lambda i: (nc - 1 - i, 0))
    return _call(
        body, plan, name="retention_backward", grid=(nc,),
        in_specs=[blk(0), blk(1), blk(2), blk(3), blk(0),
                  pl.BlockSpec((1, RET_HEADS, RET_DK, RET_DK), lambda i: (nc - 1 - i, 0, 0, 0)),
                  tab, tab, _full(decay.shape), _full(xi.shape), _full(zeta.shape), _full((1, 1024))],
        out_specs=[pl.BlockSpec((RET_CHUNK, 4096), lambda i: (nc - 1 - i, 0)), _full((1, 1024))],
        out_shape=[jax.ShapeDtypeStruct((L, 4096), MXU_DTYPE), jax.ShapeDtypeStruct((1, 1024), F32)],
        scratch_shapes=[pltpu.VMEM((RET_HEADS, RET_DK, RET_DK), F32)],
        sem=("arbitrary",),
    )(p, p, p, p, dy, prevs, cos, sin, decay, xi, zeta, gain)


def _sgu_mix(p_ref, gain_ref, wm_ref, bt_ref, tl):
    pu, pv, z = p_ref[:, :2048], p_ref[:, 2048:4096], p_ref[:, 4096:]
    u, v = _gelu(pu), _gelu(pv)
    mu = jnp.mean(v, axis=-1, keepdims=True)
    vc = v - mu
    rstd = lax.rsqrt(jnp.mean(vc * vc, axis=-1, keepdims=True) + NORM_EPS)
    vn = vc * rstd
    vg = vn * gain_ref[...]
    mask = (lax.broadcasted_iota(jnp.int32, (SGU_CHUNK, SGU_CHUNK), 0)
            >= lax.broadcasted_iota(jnp.int32, (SGU_CHUNK, SGU_CHUNK), 1))
    wms = [jnp.where(mask, wm_ref[g], 0.0) for g in range(SGU_GROUPS)]
    rows = []
    for c in range(tl // SGU_CHUNK):
        rs = slice(SGU_CHUNK * c, SGU_CHUNK * (c + 1))
        cols = []
        for g in range(SGU_GROUPS):
            gs = slice(SGU_GDIM * g, SGU_GDIM * (g + 1))
            cols.append(_mm(wms[g], vg[rs, gs]) + bt_ref[:, g:g + 1])
        rows.append(jnp.concatenate(cols, axis=1))
    s = rows[0] if len(rows) == 1 else jnp.concatenate(rows, axis=0)
    return pu, pv, z, u, vn, rstd, vg, wms, mask, s


def sgu_forward(p, gain, wm, bt):
    L = p.shape[0]
    tl = min(TL_SGU, L)

    def body(p_ref, gain_ref, wm_ref, bt_ref, y_ref):
        _, _, z, u, _, _, _, _, _, s = _sgu_mix(p_ref, gain_ref, wm_ref, bt_ref, tl)
        sz, _ = _silu_and_grad(z)
        y_ref[...] = (u * s * sz).astype(y_ref.dtype)

    return pl.pallas_call(
        body, name="sgu_forward", grid=(L // tl,),
        in_specs=[pl.BlockSpec((tl, ODD_IN), lambda i: (i, 0)), _full((1, 2048)), _full(wm.shape), _full(bt.shape)],
        out_specs=pl.BlockSpec((tl, 2048), lambda i: (i, 0)),
        out_shape=jax.ShapeDtypeStruct((L, 2048), MXU_DTYPE),
        compiler_params=_cparams(("arbitrary",)),
    )(p, gain, wm, bt)


def sgu_backward(p, dy, gain, wm, bt, plan=None):
    L = p.shape[0]
    tl = min(TL_SGU, L)

    def body(p_ref, dy_ref, gain_ref, wm_ref, bt_ref, dp_ref, dgain_ref, dwm_ref, dbt_ref):
        @pl.when(pl.program_id(0) == 0)
        def _():
            dgain_ref[...] = jnp.zeros_like(dgain_ref)
            dwm_ref[...] = jnp.zeros_like(dwm_ref)
            dbt_ref[...] = jnp.zeros_like(dbt_ref)

        pu, pv, z, u, vn, rstd, vg, wms, mask, s = _sgu_mix(p_ref, gain_ref, wm_ref, bt_ref, tl)
        sz, dsz = _silu_and_grad(z)
        dyv = dy_ref[...]
        dp_ref[:, 4096:] = (dyv * u * s * dsz).astype(dp_ref.dtype)
        dsg = dyv * sz
        dp_ref[:, :2048] = (dsg * s * _gelu_grad(pu)).astype(dp_ref.dtype)
        ds = dsg * u
        rows = []
        dbs = [jnp.zeros((SGU_CHUNK, 1), F32) for _ in range(SGU_GROUPS)]
        for c in range(tl // SGU_CHUNK):
            rs = slice(SGU_CHUNK * c, SGU_CHUNK * (c + 1))
            cols = []
            for g in range(SGU_GROUPS):
                gs = slice(SGU_GDIM * g, SGU_GDIM * (g + 1))
                dsg_c = ds[rs, gs]
                dbs[g] = dbs[g] + jnp.sum(dsg_c, axis=1, keepdims=True)
                dwm_ref[g] += jnp.where(mask, _mm_nt(dsg_c, vg[rs, gs]), 0.0)
                cols.append(_mm_tn(wms[g], dsg_c))
            rows.append(jnp.concatenate(cols, axis=1))
        dbt_ref[...] += jnp.concatenate(dbs, axis=1)
        dvg = rows[0] if len(rows) == 1 else jnp.concatenate(rows, axis=0)
        dgain_ref[...] += jnp.sum(dvg * vn, axis=0, keepdims=True)
        dvn = dvg * gain_ref[...]
        dv = rstd * (dvn - jnp.mean(dvn, axis=-1, keepdims=True) - vn * jnp.mean(dvn * vn, axis=-1, keepdims=True))
        dp_ref[:, 2048:4096] = (dv * _gelu_grad(pv)).astype(dp_ref.dtype)

    return _call(
        body, plan, name="sgu_backward", grid=(L // tl,),
        in_specs=[pl.BlockSpec((tl, ODD_IN), lambda i: (i, 0)), pl.BlockSpec((tl, 2048), lambda i: (i, 0)),
                  _full((1, 2048)), _full(wm.shape), _full(bt.shape)],
        out_specs=[pl.BlockSpec((tl, ODD_IN), lambda i: (i, 0)), _full((1, 2048)), _full(wm.shape), _full(bt.shape)],
        out_shape=[jax.ShapeDtypeStruct((L, ODD_IN), MXU_DTYPE), jax.ShapeDtypeStruct((1, 2048), F32),
                   jax.ShapeDtypeStruct(wm.shape, F32), jax.ShapeDtypeStruct(bt.shape, F32)],
        sem=("arbitrary",),
    )(p, dy, gain, wm, bt)


def cast_shards(mats):
    n = len(mats)

    def body(*refs):
        for p in range(n):
            refs[n + p][...] = refs[p][...].astype(MXU_DTYPE)

    return pl.pallas_call(
        body, name="cast_shards", out_shape=[jax.ShapeDtypeStruct(m.shape, MXU_DTYPE) for m in mats],
        compiler_params=pltpu.CompilerParams(vmem_limit_bytes=VMEM_LIMIT),
    )(*mats)


def local_grads(x, tgt, w):
    L = x.shape[0]
    ne, gf = w["norm_even"], w["final_norm"].reshape(1, D_MODEL)
    sh = dict(zip(MATRICES, cast_shards([w[n][0] for n in MATRICES])))
    (w_in_e,) = run_plan(gather_plan([sh["w_in_even"]]), "gather_w_in_even")
    lam_re, lam_im = w["s5_lam_re"][0], w["s5_lam_im"][0]
    log_dt = w["s5_log_dt"].reshape(S5_GROUPS, 1)
    bt_re = jnp.transpose(w["s5_b_re"][0], (2, 0, 1))
    bt_im = jnp.transpose(w["s5_b_im"][0], (2, 0, 1))
    c_re, c_im = w["s5_c_re"][0], w["s5_c_im"][0]
    wm = w["sgu_w_spatial"][0]
    bt = jnp.transpose(w["sgu_b_spatial"][0])

    tl5 = min(TL_S5, L)
    ab_re, ab_im, bb_re, bb_im, at_re, at_im = s5_params_fwd(lam_re, lam_im, log_dt, bt_re, bt_im, tl5 // 8)
    atab = jnp.stack([ab_re.reshape(S5_LANES), ab_im.reshape(S5_LANES),
                      at_re.reshape(S5_LANES), at_im.reshape(S5_LANES)])
    wbd = jnp.concatenate([_block_diag(jnp.transpose(bb_re, (1, 0, 2)), True),
                           _block_diag(jnp.transpose(bb_im, (1, 0, 2)), True)], axis=2).astype(MXU_DTYPE)
    cre = _block_diag(jnp.transpose(c_re, (0, 2, 1)), True).astype(MXU_DTYPE)
    cim = _block_diag(jnp.transpose(c_im, (0, 2, 1)), True).astype(MXU_DTYPE)
    cos, sin = _rope_tables(L)

    s5_cols = 2 * S5_WIDTH
    w_s5 = jnp.concatenate([w_in_e[0], w_in_e[1][:, :s5_cols - EVEN_IN // N_CHIPS]], axis=1)
    w_ret = jnp.concatenate([w_in_e[1][:, s5_cols - EVEN_IN // N_CHIPS:], w_in_e[2], w_in_e[3]], axis=1)
    (p1a, h0s), (w_glu,) = norm_matmul(stream_order(x, tl5), ne, w_s5, "even_in_s5",
                                       gather_plan([sh["s5_w_glu"]]), tn=1024)
    (p1b, h0), (w_out_e,) = norm_matmul(x, ne, w_ret, "even_in_ret", gather_plan([sh["w_out_even"]]), tn=1024)
    w_glu = w_glu.reshape(S5_WIDTH, S5_WIDTH)
    w_out_e = w_out_e.reshape(2 * S5_WIDTH, D_MODEL)
    (ya, st_re, st_im), (w_in_o, w_out_o, no, sg_gain) = s5_forward(
        p1a, wbd, cre, cim, atab, w["s5_d"], w_glu, w["s5_b_glu"],
        gather_plan([sh["w_in_odd"], sh["w_out_odd"], w["norm_odd"], w["sgu_norm_gain"]]))
    w_out_o = w_out_o.reshape(SGU_WIDTH, D_MODEL)
    no, sg_gain = no.reshape(1, D_MODEL), sg_gain.reshape(1, SGU_WIDTH)
    yb, prevs = retention_forward(p1b, cos, sin, w["ret_gn_gain"])
    ya = token_order(ya, tl5)
    x1 = matmul_residual([ya, yb], w_out_e, x, "even_out")
    (p2, h1), _ = norm_matmul(x1, no, w_in_o, "odd_in")
    y2 = sgu_forward(p2, sg_gain, wm, bt)
    dx2, loss, dgf = out_proj_loss(y2, w_out_o, x1, gf, tgt, "odd_out_loss")

    g, landed = {}, {}
    shard_major = lambda a, n: a.reshape((N_CHIPS,) + w[n].shape[1:])
    dy2, g_w_out_o = out_proj_bwd(dx2, w_out_o, [y2], "odd_out_bwd")
    (dp2, g["sgu_norm_gain"], dwm, dbt), (landed["w_out_odd"],) = sgu_backward(
        p2, dy2, sg_gain, wm, bt, reduce_plan([shard_major(g_w_out_o, "w_out_odd")]))
    g_w_in_o = in_proj_bwd_dw(h1, dp2, "odd_in_dw", ODD_IN // N_CHIPS)
    (dx1, g["norm_odd"]), _ = in_proj_bwd_dx(x1, no, [dp2], [w_in_o], dx2, "odd_in_dx")
    dya, dyb, g_w_out_e = out_proj_bwd(dx1, w_out_e, [ya, yb], "even_out_bwd")
    ((dpa, dwbd, dcre, dcim, dab_re, dab_im, g["s5_d"], g_w_glu, g["s5_b_glu"]),
     (landed["w_in_odd"], landed["w_out_even"])) = s5_backward(
        p1a, stream_order(dya, tl5), st_re, st_im, wbd, cre, cim, atab, w["s5_d"], w_glu,
        w["s5_b_glu"], reduce_plan([g_w_in_o, shard_major(g_w_out_e, "w_out_even")]))

    dbb_re = jnp.transpose(_block_diag_extract(dwbd[:, :, :512], S5_GROUP, S5_STATE), (1, 0, 2))
    dbb_im = jnp.transpose(_block_diag_extract(dwbd[:, :, 512:], S5_GROUP, S5_STATE), (1, 0, 2))
    dlr, dli, ddt, dbt_re, dbt_im = s5_params_bwd(
        lam_re, lam_im, log_dt, bt_re, bt_im, dab_re.reshape(8, S5_GROUPS, S5_STATE),
        dab_im.reshape(8, S5_GROUPS, S5_STATE), dbb_re, dbb_im)
    g["s5_lam_re"], g["s5_lam_im"] = dlr[None], dli[None]
    g["s5_log_dt"] = ddt.reshape(1, S5_GROUPS)
    g["s5_b_re"] = jnp.transpose(dbt_re, (1, 2, 0))[None]
    g["s5_b_im"] = jnp.transpose(dbt_im, (1, 2, 0))[None]
    g["s5_c_re"] = jnp.transpose(_block_diag_extract(dcre, S5_STATE, S5_GROUP), (0, 2, 1))[None]
    g["s5_c_im"] = jnp.transpose(_block_diag_extract(dcim, S5_STATE, S5_GROUP), (0, 2, 1))[None]
    g["sgu_w_spatial"] = dwm[None]
    g["sgu_b_spatial"] = jnp.transpose(dbt)[None]
    g["final_norm"] = dgf.reshape(D_MODEL)
    g["loss"] = loss

    early = BEHIND_RETENTION_BWD + ("loss",)
    (dpb, g["ret_gn_gain"]), recv = retention_backward(
        p1b, dyb, prevs, cos, sin, w["ret_gn_gain"],
        reduce_plan([shard_major(g_w_glu, "s5_w_glu")], [g[n] for n in early]))
    landed.update(zip(("s5_w_glu",) + early, recv))
    g_w_in_e = in_proj_bwd_dw(h0s, dpa, "even_in_dw_s5", 512, dtype=MXU_DTYPE)
    g_w_in_e = in_proj_bwd_dw(h0, dpb, "even_in_dw_ret", 512, first=s5_cols // 512, into=g_w_in_e, dtype=MXU_DTYPE)
    (dx0, g["norm_even"]), recv = in_proj_bwd_dx(
        x, ne, [token_order(dpa, tl5), dpb], [w_s5, w_ret], dx1, "even_in_dx",
        reduce_plan([g_w_in_e], [g[n] for n in BEHIND_EVEN_IN_DX]))
    landed.update(zip(("w_in_even",) + BEHIND_EVEN_IN_DX, recv))
    (landed["norm_even"],) = run_plan(reduce_plan([], [g["norm_even"]]), "exchange_norm_even")
    return dx0, landed


def sibling_exchange(arrs):
    n = len(arrs)

    def body(*refs):
        in_refs, out_refs = refs[:n], refs[n:2 * n]
        send_sems, recv_sems = refs[2 * n:]
        x, y, c = _place()
        copies = [pltpu.make_async_remote_copy(
            src_ref=in_refs[p], dst_ref=out_refs[p], send_sem=send_sems.at[p], recv_sem=recv_sems.at[p],
            device_id=(x, y, 1 - c), device_id_type=MESH) for p in range(n)]
        for cp in copies:
            cp.start()
        for cp in copies:
            cp.wait_recv()
        for cp in copies:
            cp.wait_send()

    return pl.pallas_call(
        body, name="sibling_exchange", in_specs=[ANY] * n, out_specs=[ANY] * n,
        out_shape=[jax.ShapeDtypeStruct(a.shape, a.dtype) for a in arrs],
        scratch_shapes=[pltpu.SemaphoreType.DMA((n,)), pltpu.SemaphoreType.DMA((n,))],
    )(*arrs)


def _row_block(rows):
    return 128 if rows % 128 == 0 else rows


def sum_slabs(r, name):
    _, R, C = r.shape
    tr = _row_block(R)

    def body(r_ref, o_ref):
        a, b, c, d = (r_ref[k].astype(F32) for k in range(N_CHIPS))
        o_ref[...] = (a + b) + (c + d)

    return pl.pallas_call(
        body, name=name, grid=(R // tr,),
        in_specs=[pl.BlockSpec((N_CHIPS, tr, C), lambda i: (0, i, 0))],
        out_specs=pl.BlockSpec((tr, C), lambda i: (i, 0)),
        out_shape=jax.ShapeDtypeStruct((R, C), F32),
        compiler_params=_cparams(("arbitrary",)),
    )(r)


def _adam(w, m, v, g):
    mn = ADAM_B1 * m + (1.0 - ADAM_B1) * g
    vn = ADAM_B2 * v + (1.0 - ADAM_B2) * (g * g)
    m_hat = mn / (1.0 - ADAM_B1 ** ADAM_STEP)
    v_hat = vn / (1.0 - ADAM_B2 ** ADAM_STEP)
    return -ADAM_LR * (m_hat / (jnp.sqrt(v_hat) + ADAM_EPS) + ADAM_WD * w), mn, vn


def adam_update(w, m, v, ga, gb, name):
    R, C = w.shape
    tr = _row_block(R)

    def body(w_ref, m_ref, v_ref, ga_ref, gb_ref, g_out, d_out, m_out, v_out):
        g = ga_ref[...] + gb_ref[...]
        g_out[...] = g
        d_out[...], m_out[...], v_out[...] = _adam(w_ref[...], m_ref[...], v_ref[...], g)

    blk = pl.BlockSpec((tr, C), lambda i: (i, 0))
    return pl.pallas_call(
        body, name=name, grid=(R // tr,),
        in_specs=[blk] * 5, out_specs=[blk] * 4,
        out_shape=[jax.ShapeDtypeStruct((R, C), F32)] * 4,
        compiler_params=_cparams(("arbitrary",)),
    )(w, m, v, ga, gb)


WIDE_ROWS = ("s5_b_re", "s5_b_im")
GROUP_BLK = 8


def _by_groups(arrs, lead):
    def spec(a):
        blk = a.shape[:lead] + (GROUP_BLK,) + a.shape[lead + 1:]
        nd = len(a.shape)
        return pl.BlockSpec(blk, lambda i: (0,) * lead + (i,) + (0,) * (nd - lead - 1))
    return [spec(a) for a in arrs]


def sum_small(landed):
    def body(*refs):
        k = len(refs) // 2
        for i in range(k):
            r = refs[i]
            refs[k + i][...] = (r[0] + r[1]) + (r[2] + r[3])

    out = {}
    plain = [n for n in landed if n not in WIDE_ROWS]
    res = pl.pallas_call(
        functools.partial(body), name="sum_small",
        out_shape=[jax.ShapeDtypeStruct(landed[n].shape[1:], F32) for n in plain],
        compiler_params=pltpu.CompilerParams(vmem_limit_bytes=VMEM_LIMIT),
    )(*[landed[n] for n in plain])
    out.update(zip(plain, res))
    wide = [n for n in landed if n in WIDE_ROWS]
    if wide:
        ins = [landed[n] for n in wide]
        outs = [jax.ShapeDtypeStruct(a.shape[1:], F32) for a in ins]
        res = pl.pallas_call(
            functools.partial(body), name="sum_small_wide", grid=(S5_GROUPS // GROUP_BLK,),
            in_specs=_by_groups(ins, 2),
            out_specs=_by_groups(outs, 1), out_shape=outs, compiler_params=_cparams(("arbitrary",)),
        )(*ins)
        out.update(zip(wide, res))
    return out


def adam_small(names, w, m, v, ga, gb):
    def body(*refs):
        k = len(refs) // 9
        me = 2 * lax.axis_index("x") + lax.axis_index("y")
        for i in range(k):
            w_ref, m_ref, v_ref, ga_ref, gb_ref = refs[i], refs[k + i], refs[2 * k + i], refs[3 * k + i], refs[4 * k + i]
            size = w_ref.shape[-1]
            if ga_ref.shape != w_ref.shape:
                part = pl.ds(pl.multiple_of(me * size, LANES), size)
                g = ga_ref[:, part] + gb_ref[:, part]
            else:
                g = ga_ref[...] + gb_ref[...]
            refs[5 * k + i][...] = g
            refs[6 * k + i][...], refs[7 * k + i][...], refs[8 * k + i][...] = _adam(w_ref[...], m_ref[...], v_ref[...], g)

    def run(group, **kw):
        ins = [d[n] for d in (w, m, v, ga, gb) for n in group]
        outs = [jax.ShapeDtypeStruct(w[n].shape, F32) for _ in range(4) for n in group]
        res = pl.pallas_call(functools.partial(body), out_shape=outs, **kw)(*ins)
        k = len(group)
        return [dict(zip(group, res[j * k:(j + 1) * k])) for j in range(4)]

    plain = [n for n in names if n not in WIDE_ROWS]
    wide = [n for n in names if n in WIDE_ROWS]
    res = run(plain, name="adam_small", compiler_params=pltpu.CompilerParams(vmem_limit_bytes=VMEM_LIMIT))
    if wide:
        specs = _by_groups([w[n] for n in wide], 1)
        res_w = run(wide, name="adam_small_wide", grid=(S5_GROUPS // GROUP_BLK,), in_specs=specs * 5,
                    out_specs=specs * 4, compiler_params=_cparams(("arbitrary",)))
        for d, dw in zip(res, res_w):
            d.update(dw)
    return res


WEIGHTS = ("norm_even", "w_in_even", "s5_lam_re", "s5_lam_im", "s5_log_dt", "s5_b_re", "s5_b_im", "s5_c_re",
           "s5_c_im", "s5_d", "s5_w_glu", "s5_b_glu", "ret_gn_gain", "w_out_even", "norm_odd", "w_in_odd",
           "sgu_norm_gain", "sgu_w_spatial", "sgu_b_spatial", "w_out_odd", "final_norm")
MATRICES = ("w_in_even", "s5_w_glu", "w_out_even", "w_in_odd", "w_out_odd")
SHARDED_VECS = ("norm_odd", "sgu_norm_gain")
REPLICATED = tuple(n for n in WEIGHTS if n not in MATRICES and n not in SHARDED_VECS)
SMALL = tuple(n for n in WEIGHTS if n not in MATRICES)
BEHIND_RETENTION_BWD = tuple(n for n in SMALL if n not in ("ret_gn_gain", "norm_even"))
BEHIND_EVEN_IN_DX = ("ret_gn_gain",)
LANES = 128


def kernel(x, norm_even, w_in_even, s5_lam_re, s5_lam_im, s5_log_dt, s5_b_re, s5_b_im, s5_c_re, s5_c_im, s5_d, s5_w_glu, s5_b_glu, ret_gn_gain, w_out_even, norm_odd, w_in_odd, sgu_norm_gain, sgu_w_spatial, sgu_b_spatial, w_out_odd, final_norm, loss_target, m_norm_even, m_w_in_even, m_s5_lam_re, m_s5_lam_im, m_s5_log_dt, m_s5_b_re, m_s5_b_im, m_s5_c_re, m_s5_c_im, m_s5_d, m_s5_w_glu, m_s5_b_glu, m_ret_gn_gain, m_w_out_even, m_norm_odd, m_w_in_odd, m_sgu_norm_gain, m_sgu_w_spatial, m_sgu_b_spatial, m_w_out_odd, m_final_norm, v_norm_even, v_w_in_even, v_s5_lam_re, v_s5_lam_im, v_s5_log_dt, v_s5_b_re, v_s5_b_im, v_s5_c_re, v_s5_c_im, v_s5_d, v_s5_w_glu, v_s5_b_glu, v_ret_gn_gain, v_w_out_even, v_norm_odd, v_w_in_odd, v_sgu_norm_gain, v_sgu_w_spatial, v_sgu_b_spatial, v_w_out_odd, v_final_norm):
    w = dict(norm_even=norm_even, w_in_even=w_in_even, s5_lam_re=s5_lam_re, s5_lam_im=s5_lam_im, s5_log_dt=s5_log_dt, s5_b_re=s5_b_re, s5_b_im=s5_b_im, s5_c_re=s5_c_re, s5_c_im=s5_c_im, s5_d=s5_d, s5_w_glu=s5_w_glu, s5_b_glu=s5_b_glu, ret_gn_gain=ret_gn_gain, w_out_even=w_out_even, norm_odd=norm_odd, w_in_odd=w_in_odd, sgu_norm_gain=sgu_norm_gain, sgu_w_spatial=sgu_w_spatial, sgu_b_spatial=sgu_b_spatial, w_out_odd=w_out_odd, final_norm=final_norm)
    m = dict(norm_even=m_norm_even, w_in_even=m_w_in_even, s5_lam_re=m_s5_lam_re, s5_lam_im=m_s5_lam_im, s5_log_dt=m_s5_log_dt, s5_b_re=m_s5_b_re, s5_b_im=m_s5_b_im, s5_c_re=m_s5_c_re, s5_c_im=m_s5_c_im, s5_d=m_s5_d, s5_w_glu=m_s5_w_glu, s5_b_glu=m_s5_b_glu, ret_gn_gain=m_ret_gn_gain, w_out_even=m_w_out_even, norm_odd=m_norm_odd, w_in_odd=m_w_in_odd, sgu_norm_gain=m_sgu_norm_gain, sgu_w_spatial=m_sgu_w_spatial, sgu_b_spatial=m_sgu_b_spatial, w_out_odd=m_w_out_odd, final_norm=m_final_norm)
    v = dict(norm_even=v_norm_even, w_in_even=v_w_in_even, s5_lam_re=v_s5_lam_re, s5_lam_im=v_s5_lam_im, s5_log_dt=v_s5_log_dt, s5_b_re=v_s5_b_re, s5_b_im=v_s5_b_im, s5_c_re=v_s5_c_re, s5_c_im=v_s5_c_im, s5_d=v_s5_d, s5_w_glu=v_s5_w_glu, s5_b_glu=v_s5_b_glu, ret_gn_gain=v_ret_gn_gain, w_out_even=v_w_out_even, norm_odd=v_norm_odd, w_in_odd=v_w_in_odd, sgu_norm_gain=v_sgu_norm_gain, sgu_w_spatial=v_sgu_w_spatial, sgu_b_spatial=v_sgu_b_spatial, w_out_odd=v_w_out_odd, final_norm=v_final_norm)

    grad_x, landed = local_grads(x[0], loss_target[0], w)

    small = SMALL + ("loss",)
    part = {n: sum_slabs(landed[n], "sum_" + n) for n in MATRICES}
    part.update(sum_small({n: landed[n] for n in small}))
    names = MATRICES + small
    other = dict(zip(names, sibling_exchange([part[n] for n in names])))

    out_g, out_d, out_m, out_v = adam_small(SMALL, w, m, v, part, other)
    for n in MATRICES:
        res = adam_update(w[n][0], m[n][0], v[n][0], part[n], other[n], "adam_" + n)
        out_g[n], out_d[n], out_m[n], out_v[n] = (r[None] for r in res)
    total_loss = (part["loss"] + other["loss"])[0, 0]

    return (total_loss, grad_x[None], *[out_g[n] for n in WEIGHTS], *[out_d[n] for n in WEIGHTS],
            *[out_m[n] for n in WEIGHTS], *[out_v[n] for n in WEIGHTS])
```

```python
import functools
import math

import numpy as np
import jax
import jax.numpy as jnp
from jax import lax
from jax.experimental import pallas as pl
from jax.experimental.pallas import tpu as pltpu

F32 = jnp.float32
MXU_DTYPE = jnp.bfloat16
NORM_EPS = 1e-6
D_MODEL = 1024
S5_WIDTH = 1024
S5_GROUP = 16
S5_GROUPS = 64
S5_STATE = 64
S5_LANES = S5_GROUPS * S5_STATE
S5_KBLK = 8
RET_HEADS = 4
RET_DK = 256
RET_CHUNK = 128
ROPE_BASE = 10000.0
SGU_WIDTH = 2048
SGU_GROUPS = 4
SGU_GDIM = 512
SGU_CHUNK = 128
EVEN_IN = 6144
ODD_IN = 6144
ADAM_LR = 0.001
ADAM_B1 = 0.9
ADAM_B2 = 0.999
ADAM_EPS = 1e-08
ADAM_WD = 0.01
ADAM_STEP = 10
N_CHIPS = 4
VMEM_LIMIT = 56 * 1024 * 1024

TL_PROJ = 512
TL_DW = 1024
TL_S5 = 128
TL_SGU = 128


def _cparams(sem, **kw):
    return pltpu.CompilerParams(dimension_semantics=sem, vmem_limit_bytes=VMEM_LIMIT, **kw)


def _mm(a, b):
    return jnp.dot(a.astype(MXU_DTYPE), b.astype(MXU_DTYPE), preferred_element_type=F32)


def _mm_nt(a, b):
    return lax.dot_general(a.astype(MXU_DTYPE), b.astype(MXU_DTYPE),
                           (((1,), (1,)), ((), ())), preferred_element_type=F32)


def _mm_tn(a, b):
    return lax.dot_general(a.astype(MXU_DTYPE), b.astype(MXU_DTYPE),
                           (((0,), (0,)), ((), ())), preferred_element_type=F32)


_GELU_C = math.sqrt(2.0 / math.pi)


def _gelu_parts(x):
    x2 = x * x
    th = jnp.tanh(x * (_GELU_C + (_GELU_C * 0.044715) * x2))
    hx = 0.5 * x
    return hx + hx * th, th, x2, hx


def _gelu(x):
    return _gelu_parts(x)[0]


def _gelu_and_grad(x):
    g, th, x2, hx = _gelu_parts(x)
    return g, (0.5 + 0.5 * th) + hx * (1.0 - th * th) * (_GELU_C + (3.0 * _GELU_C * 0.044715) * x2)


def _gelu_grad(x):
    return _gelu_and_grad(x)[1]


def _sigmoid(x):
    return 1.0 / (1.0 + jnp.exp(-x))


def _silu_and_grad(x):
    s = _sigmoid(x)
    return x * s, s * (1.0 + x * (1.0 - s))


def _rms(x):
    return lax.rsqrt(jnp.mean(x * x, axis=-1, keepdims=True) + NORM_EPS)


def _full(shape):
    nd = len(shape)
    return pl.BlockSpec(shape, lambda *_: (0,) * nd)


MESH = pl.DeviceIdType.MESH
ANY = pl.BlockSpec(memory_space=pl.ANY)


def _place():
    return lax.axis_index("x"), lax.axis_index("y"), lax.axis_index("c")


def _chip_peer(x, y, c, d):
    return (1 - x if d >= 2 else x, 1 - y if d % 2 else y, c)


class _Plan:
    def __init__(self, inputs, out_shape, build):
        self.inputs, self.out_shape, self._build = list(inputs), list(out_shape), build
        n = len(self.inputs)
        self.sems = [pltpu.SemaphoreType.DMA((n, 3)), pltpu.SemaphoreType.DMA((n, 3)), pltpu.SemaphoreType.DMA((n,))]

    def start(self, in_refs, out_refs, sems):
        send, recv, local = self._build(in_refs, out_refs, sems)
        for p in range(len(self.inputs)):
            local[p].start()
            for cp in send[p]:
                cp.start()

    def wait(self, in_refs, out_refs, sems):
        send, recv, local = self._build(in_refs, out_refs, sems)
        for p in range(len(self.inputs)):
            for cp in recv[p]:
                cp.wait_recv()
        for p in range(len(self.inputs)):
            for cp in send[p]:
                cp.wait_send()
            local[p].wait()


class _GatherPlan:
    def __init__(self, shards):
        self.inputs = list(shards)
        self.out_shape = [jax.ShapeDtypeStruct((N_CHIPS,) + s.shape, s.dtype) for s in shards]
        n = len(shards)
        self.halved = [s.shape[0] % 32 == 0 for s in shards]
        self.sems = [pltpu.SemaphoreType.DMA((n, 3)) for _ in range(4)] + [pltpu.SemaphoreType.DMA((n,))]

    def _copies(self, in_refs, out_refs, sems):
        ici_s, ici_r, d2d_s, d2d_r, loc = sems
        x, y, c = _place()
        me = 2 * x + y

        def rows(p, core):
            if not self.halved[p]:
                return slice(None)
            half = self.inputs[p].shape[0] // 2
            return pl.ds(pl.multiple_of(core * half, 16), half)

        def ici(p, d, slab, core, src=None):
            dst = out_refs[p].at[slab, rows(p, core)]
            return pltpu.make_async_remote_copy(
                src_ref=in_refs[p].at[rows(p, core)] if src is None else src, dst_ref=dst,
                send_sem=ici_s.at[p, d - 1], recv_sem=ici_r.at[p, d - 1],
                device_id=_chip_peer(x, y, c, d), device_id_type=MESH)

        def d2d(p, d, core):
            part = out_refs[p].at[me ^ d, rows(p, core)]
            return pltpu.make_async_remote_copy(
                src_ref=part, dst_ref=part, send_sem=d2d_s.at[p, d - 1], recv_sem=d2d_r.at[p, d - 1],
                device_id=(x, y, 1 - c), device_id_type=MESH)

        local = [pltpu.make_async_copy(in_refs[p], out_refs[p].at[me], loc.at[p]) for p in range(len(self.inputs))]
        return me, c, ici, d2d, local

    def start(self, in_refs, out_refs, sems):
        me, c, ici, d2d, local = self._copies(in_refs, out_refs, sems)
        for p in range(len(self.inputs)):
            local[p].start()
            for d in (1, 2, 3):
                ici(p, d, me, c).start()

    def wait(self, in_refs, out_refs, sems):
        me, c, ici, d2d, local = self._copies(in_refs, out_refs, sems)
        n = len(self.inputs)
        for p in range(n):
            for d in (1, 2, 3):
                ici(p, d, me ^ d, c).wait_recv()
                if self.halved[p]:
                    d2d(p, d, c).start()
        for p in range(n):
            for d in (1, 2, 3):
                if self.halved[p]:
                    d2d(p, d, 1 - c).wait_recv()
                    d2d(p, d, c).wait_send()
                ici(p, d, me, c).wait_send()
            local[p].wait()


def gather_plan(shards):
    return _GatherPlan(shards)


def reduce_plan(shards, whole=()):
    n_s = len(shards)

    def build(in_refs, out_refs, sems):
        send_sems, recv_sems, loc_sems = sems
        x, y, c = _place()
        me = 2 * x + y

        def src(p, slab):
            return in_refs[p].at[slab] if p < n_s else in_refs[p]

        def remote(p, d):
            return pltpu.make_async_remote_copy(
                src_ref=src(p, me ^ d), dst_ref=out_refs[p].at[d], send_sem=send_sems.at[p, d - 1],
                recv_sem=recv_sems.at[p, d - 1], device_id=_chip_peer(x, y, c, d), device_id_type=MESH)

        n = len(in_refs)
        send = [[remote(p, d) for d in (1, 2, 3)] for p in range(n)]
        local = [pltpu.make_async_copy(src(p, me), out_refs[p].at[0], loc_sems.at[p]) for p in range(n)]
        return send, send, local

    outs = [jax.ShapeDtypeStruct(s.shape, s.dtype) for s in shards]
    outs += [jax.ShapeDtypeStruct((N_CHIPS,) + a.shape, a.dtype) for a in whole]
    return _Plan(list(shards) + list(whole), outs, build)


def run_plan(plan, name):
    n = len(plan.inputs)

    def body(*refs):
        plan.start(refs[:n], refs[n:2 * n], refs[2 * n:])
        plan.wait(refs[:n], refs[n:2 * n], refs[2 * n:])

    return pl.pallas_call(body, name=name, in_specs=[ANY] * n, out_specs=[ANY] * n, out_shape=plan.out_shape,
                          scratch_shapes=plan.sems)(*plan.inputs)


def _call(body, plan, *, name, grid, in_specs, out_specs, out_shape, sem, scratch_shapes=()):
    single = not isinstance(out_shape, (list, tuple))
    out_specs = [out_specs] if single else list(out_specs)
    out_shape = [out_shape] if single else list(out_shape)
    n_in, n_out, n_scr = len(in_specs), len(out_specs), len(scratch_shapes)
    ci = 0 if plan is None else len(plan.inputs)

    def hosted(*refs):
        ins, cins = refs[:n_in], refs[n_in:n_in + ci]
        k = n_in + ci
        outs, couts = refs[k:k + n_out], refs[k + n_out:k + n_out + ci]
        k += n_out + ci
        scr, sems = refs[k:k + n_scr], refs[k + n_scr:]
        ids = [pl.program_id(a) for a in range(len(grid))]
        first = functools.reduce(jnp.logical_and, [i == 0 for i in ids])
        last = functools.reduce(jnp.logical_and, [i == g - 1 for i, g in zip(ids, grid)])

        @pl.when(first)
        def _():
            plan.start(cins, couts, sems)

        body(*ins, *outs, *scr)

        @pl.when(last)
        def _():
            plan.wait(cins, couts, sems)

    def run(*args):
        if plan is None:
            res = pl.pallas_call(body, name=name, grid=grid, in_specs=list(in_specs), out_specs=out_specs,
                                 out_shape=out_shape, scratch_shapes=list(scratch_shapes),
                                 compiler_params=_cparams(sem))(*args)
            return (res[0] if single else res), []
        res = pl.pallas_call(hosted, name=name, grid=grid, in_specs=list(in_specs) + [ANY] * ci,
                             out_specs=out_specs + [ANY] * ci, out_shape=out_shape + plan.out_shape,
                             scratch_shapes=list(scratch_shapes) + plan.sems,
                             compiler_params=_cparams(sem))(*args, *plan.inputs)
        return (res[0] if single else res[:n_out]), list(res[n_out:])

    return run


def norm_matmul(x, g, w, name, plan=None, tn=None):
    L, D = x.shape
    tl = min(TL_DW, L)
    if w.ndim == 3:
        nt, _, tn = w.shape
        w_spec = pl.BlockSpec((1, D, tn), lambda i, n: (n, 0, 0))
    else:
        nt = w.shape[1] // tn
        w_spec = pl.BlockSpec((D, tn), lambda i, n: (0, n))

    def body(x_ref, g_ref, w_ref, o_ref, h_ref):
        xv = x_ref[...]
        h = (xv * _rms(xv) * g_ref[...]).astype(h_ref.dtype)
        h_ref[...] = h
        o_ref[...] = _mm(h, w_ref[0] if w.ndim == 3 else w_ref[...])

    return _call(
        body, plan, name=name, grid=(L // tl, nt),
        in_specs=[pl.BlockSpec((tl, D), lambda i, n: (i, 0)), _full((1, D)), w_spec],
        out_specs=[pl.BlockSpec((tl, tn), lambda i, n: (i, n)), pl.BlockSpec((tl, D), lambda i, n: (i, 0))],
        out_shape=[jax.ShapeDtypeStruct((L, nt * tn), F32), jax.ShapeDtypeStruct((L, D), MXU_DTYPE)],
        sem=("arbitrary", "arbitrary"),
    )(x, g, w)


def matmul_residual(ys, w, x, name):
    L, D = x.shape
    tl = min(TL_PROJ, L)
    n = len(ys)
    offs = np.cumsum([0] + [y.shape[1] for y in ys])

    def body(*refs):
        y_refs, w_ref, x_ref, o_ref = refs[:n], refs[n], refs[n + 1], refs[n + 2]
        acc = x_ref[...]
        for k in range(n):
            acc = acc + _mm(y_refs[k][...], w_ref[offs[k]:offs[k + 1], :])
        o_ref[...] = acc

    return pl.pallas_call(
        body, name=name, grid=(L // tl,),
        in_specs=[pl.BlockSpec((tl, y.shape[1]), lambda i: (i, 0)) for y in ys]
        + [_full(w.shape), pl.BlockSpec((tl, D), lambda i: (i, 0))],
        out_specs=pl.BlockSpec((tl, D), lambda i: (i, 0)),
        out_shape=jax.ShapeDtypeStruct((L, D), F32),
        compiler_params=_cparams(("arbitrary",)),
    )(*ys, w, x)


def out_proj_loss(y, w, x, gf, tgt, name):
    L, K = y.shape
    D = w.shape[1]
    tl = min(TL_PROJ, L)

    def body(y_ref, w_ref, x_ref, gf_ref, t_ref, dx_ref, loss_ref, dg_ref):
        @pl.when(pl.program_id(0) == 0)
        def _():
            loss_ref[...] = jnp.zeros_like(loss_ref)
            dg_ref[...] = jnp.zeros_like(dg_ref)

        x2 = x_ref[...] + _mm(y_ref[...], w_ref[...])
        r = _rms(x2)
        xn = x2 * r
        e = xn * gf_ref[...] - t_ref[...]
        loss_ref[...] += (0.5 / D) * jnp.sum(e * e)
        dout = e * (1.0 / D)
        dg_ref[...] += jnp.sum(dout * xn, axis=0, keepdims=True)
        dxn = dout * gf_ref[...]
        dx_ref[...] = r * (dxn - xn * jnp.mean(dxn * xn, axis=-1, keepdims=True))

    return pl.pallas_call(
        body, name=name, grid=(L // tl,),
        in_specs=[pl.BlockSpec((tl, K), lambda i: (i, 0)), _full((K, D)),
                  pl.BlockSpec((tl, D), lambda i: (i, 0)), _full((1, D)),
                  pl.BlockSpec((tl, D), lambda i: (i, 0))],
        out_specs=[pl.BlockSpec((tl, D), lambda i: (i, 0)), _full((8, 128)), _full((1, D))],
        out_shape=[jax.ShapeDtypeStruct((L, D), F32), jax.ShapeDtypeStruct((8, 128), F32),
                   jax.ShapeDtypeStruct((1, D), F32)],
        compiler_params=_cparams(("arbitrary",)),
    )(y, w, x, gf, tgt)


def out_proj_bwd(dx, w, ys, name):
    L, D = dx.shape
    K = w.shape[0]
    tl = min(TL_PROJ, L)
    n = len(ys)
    offs = np.cumsum([0] + [y.shape[1] for y in ys])

    def body(*refs):
        dx_ref, w_ref, y_refs = refs[0], refs[1], refs[2:2 + n]
        dy_refs, dw_ref = refs[2 + n:2 + 2 * n], refs[2 + 2 * n]

        @pl.when(pl.program_id(0) == 0)
        def _():
            dw_ref[...] = jnp.zeros_like(dw_ref)

        dxv = dx_ref[...]
        for k in range(n):
            dy_refs[k][...] = _mm_nt(dxv, w_ref[offs[k]:offs[k + 1], :])
            dw_ref[offs[k]:offs[k + 1], :] += _mm_tn(y_refs[k][...], dxv)

    y_specs = [pl.BlockSpec((tl, y.shape[1]), lambda i: (i, 0)) for y in ys]
    return pl.pallas_call(
        body, name=name, grid=(L // tl,),
        in_specs=[pl.BlockSpec((tl, D), lambda i: (i, 0)), _full((K, D))] + y_specs,
        out_specs=y_specs + [_full((K, D))],
        out_shape=[jax.ShapeDtypeStruct(y.shape, F32) for y in ys] + [jax.ShapeDtypeStruct((K, D), F32)],
        compiler_params=_cparams(("arbitrary",)),
    )(dx, w, *ys)


def in_proj_bwd_dx(x, g, dps, ws, dres, name, plan=None):
    L, D = x.shape
    tl = min(TL_PROJ, L)
    n = len(dps)

    def body(*refs):
        x_ref, g_ref, dres_ref = refs[:3]
        dp_refs, w_refs = refs[3:3 + n], refs[3 + n:3 + 2 * n]
        dx_ref, dg_ref = refs[3 + 2 * n:]

        @pl.when(pl.program_id(0) == 0)
        def _():
            dg_ref[...] = jnp.zeros_like(dg_ref)

        dh = None
        for dp_ref, w_ref, w in zip(dp_refs, w_refs, ws):
            if w.ndim == 3:
                tn = w.shape[2]
                parts = [_mm_nt(dp_ref[:, tn * k:tn * (k + 1)], w_ref[k]) for k in range(w.shape[0])]
            else:
                parts = [_mm_nt(dp_ref[...], w_ref[...])]
            for part in parts:
                dh = part if dh is None else dh + part
        xv = x_ref[...]
        r = _rms(xv)
        xn = xv * r
        dg_ref[...] += jnp.sum(dh * xn, axis=0, keepdims=True)
        dxn = dh * g_ref[...]
        dx_ref[...] = dres_ref[...] + r * (dxn - xn * jnp.mean(dxn * xn, axis=-1, keepdims=True))

    return _call(
        body, plan, name=name, grid=(L // tl,),
        in_specs=[pl.BlockSpec((tl, D), lambda i: (i, 0)), _full((1, D)), pl.BlockSpec((tl, D), lambda i: (i, 0))]
        + [pl.BlockSpec((tl, dp.shape[1]), lambda i: (i, 0)) for dp in dps] + [_full(w.shape) for w in ws],
        out_specs=[pl.BlockSpec((tl, D), lambda i: (i, 0)), _full((1, D))],
        out_shape=[jax.ShapeDtypeStruct((L, D), F32), jax.ShapeDtypeStruct((1, D), F32)],
        sem=("arbitrary",),
    )(x, g, dres, *dps, *ws)


def in_proj_bwd_dw(h, dp, name, tn, first=0, into=None, dtype=F32):
    L, D = h.shape
    tl = min(TL_DW, L)
    wb = EVEN_IN // N_CHIPS
    per = wb // tn
    count = dp.shape[1] // tn
    last = L // tl - 1

    def body(*refs):
        h_ref, dp_ref, dw_ref, acc = refs[0], refs[1], refs[-2], refs[-1]

        @pl.when(pl.program_id(1) == 0)
        def _():
            acc[...] = jnp.zeros_like(acc)

        acc[...] += _mm_tn(h_ref[...], dp_ref[...])

        @pl.when(pl.program_id(1) == last)
        def _():
            dw_ref[0] = acc[...].astype(dw_ref.dtype)

    ins = [h, dp] + ([] if into is None else [into])
    return pl.pallas_call(
        body, name=name, grid=(count, L // tl),
        in_specs=[pl.BlockSpec((tl, D), lambda n, i: (i, 0)), pl.BlockSpec((tl, tn), lambda n, i: (i, n))]
        + ([] if into is None else [ANY]),
        out_specs=pl.BlockSpec((1, D, tn), lambda n, i: ((n + first) // per, 0, (n + first) % per)),
        out_shape=jax.ShapeDtypeStruct((N_CHIPS, D, wb), dtype),
        scratch_shapes=[pltpu.VMEM((D, tn), F32)],
        input_output_aliases={} if into is None else {2: 0},
        compiler_params=_cparams(("arbitrary", "arbitrary")),
    )(*ins)


def _s5_param_fn(lam_re, lam_im, log_dt, b_re, b_im):
    lr = jnp.minimum(lam_re, -1e-4)
    li = lam_im
    dt = jnp.exp(log_dt)
    mag = jnp.exp(lr * dt)
    ab_re = mag * jnp.cos(li * dt)
    ab_im = mag * jnp.sin(li * dt)
    den = lr * lr + li * li
    n_re = ab_re - 1.0
    n_im = ab_im
    z_re = (n_re * lr + n_im * li) / den
    z_im = (n_im * lr - n_re * li) / den
    bb_re = z_re[None] * b_re - z_im[None] * b_im
    bb_im = z_re[None] * b_im + z_im[None] * b_re
    return ab_re, ab_im, bb_re, bb_im


def s5_params_fwd(lam_re, lam_im, log_dt, b_re, b_im, span):
    G, P = lam_re.shape
    H = b_re.shape[0]
    assert span & (span - 1) == 0

    def body(lr_ref, li_ref, dt_ref, br_ref, bi_ref, abr_ref, abi_ref, bbr_ref, bbi_ref, pr_ref, pi_ref):
        ab_re, ab_im, bb_re, bb_im = _s5_param_fn(lr_ref[...], li_ref[...], dt_ref[...], br_ref[...], bi_ref[...])
        abr_ref[...] = ab_re
        abi_ref[...] = ab_im
        bbr_ref[...] = bb_re
        bbi_ref[...] = bb_im
        cr, ci = ab_re, ab_im
        for _ in range(span.bit_length() - 1):
            cr, ci = cr * cr - ci * ci, 2.0 * cr * ci
        pr_ref[...] = cr
        pi_ref[...] = ci

    shp = lambda *s: jax.ShapeDtypeStruct(s, F32)
    return pl.pallas_call(
        body, name="s5_params_fwd",
        out_shape=[shp(G, P), shp(G, P), shp(H, G, P), shp(H, G, P), shp(G, P), shp(G, P)],
    )(lam_re, lam_im, log_dt, b_re, b_im)


def s5_params_bwd(lam_re, lam_im, log_dt, b_re, b_im, d_ab_re, d_ab_im, d_bb_re, d_bb_im):
    G, P = lam_re.shape
    H = b_re.shape[0]

    def body(lr_ref, li_ref, dt_ref, br_ref, bi_ref, g0, g1, g2, g3, o0, o1, o2, o3, o4):
        prim = (lr_ref[...], li_ref[...], dt_ref[...], br_ref[...], bi_ref[...])
        _, vjp = jax.vjp(_s5_param_fn, *prim)
        d = vjp((jnp.sum(g0[...], axis=0), jnp.sum(g1[...], axis=0), g2[...], g3[...]))
        o0[...], o1[...], o2[...], o3[...], o4[...] = d

    shp = lambda *s: jax.ShapeDtypeStruct(s, F32)
    return pl.pallas_call(
        body, name="s5_params_bwd",
        out_shape=[shp(G, P), shp(G, P), shp(G, 1), shp(H, G, P), shp(H, G, P)],
    )(lam_re, lam_im, log_dt, b_re, b_im, d_ab_re, d_ab_im, d_bb_re, d_bb_im)


def stream_order(a, tl):
    L, C = a.shape
    return a.reshape(L // tl, 8, tl // 8, C).transpose(0, 2, 1, 3).reshape(L, C)


def token_order(a, tl):
    L, C = a.shape
    return a.reshape(L // tl, tl // 8, 8, C).transpose(0, 2, 1, 3).reshape(L, C)


_LANE_BLK = 1024


def _cmul_add(ar, ai, xr, xi, br, bi):
    return br + (ar * xr - ai * xi), bi + (ar * xi + ai * xr)


def _cmulc_add(ar, ai, xr, xi, br, bi):
    return br + (ar * xr + ai * xi), bi + (ar * xi - ai * xr)


def _s5_states(u, wbd_ref, a_re, a_im, at_re, at_im, s_re, s_im, e_re, e_im, c0_re, c0_im, tl):
    t8 = tl // 8
    for k in range(S5_KBLK):
        bu = _mm(u[:, 128 * k:128 * (k + 1)], wbd_ref[k])
        s_re[:, 512 * k:512 * (k + 1)] = bu[:, :512]
        s_im[:, 512 * k:512 * (k + 1)] = bu[:, 512:]
    outs_re, outs_im = [], []
    for b in range(S5_LANES // _LANE_BLK):
        lanes = slice(_LANE_BLK * b, _LANE_BLK * (b + 1))
        ar = jnp.broadcast_to(a_re[:, lanes], (8, _LANE_BLK))
        ai = jnp.broadcast_to(a_im[:, lanes], (8, _LANE_BLK))

        def local(i, carry, lanes=lanes, ar=ar, ai=ai):
            r = pl.multiple_of(i * 8, 8)
            sr, si = _cmul_add(ar, ai, carry[0], carry[1], s_re[pl.ds(r, 8), lanes], s_im[pl.ds(r, 8), lanes])
            s_re[pl.ds(r, 8), lanes] = sr
            s_im[pl.ds(r, 8), lanes] = si
            return sr, si

        zero = jnp.zeros((8, _LANE_BLK), F32)
        fr, fi = lax.fori_loop(0, t8, local, (zero, zero), unroll=True)
        tr, ti = at_re[:, lanes], at_im[:, lanes]
        er, ei = c0_re[:, lanes], c0_im[:, lanes]
        ers, eis = [er], [ei]
        for j in range(8):
            er, ei = _cmul_add(tr, ti, er, ei, fr[j:j + 1], fi[j:j + 1])
            ers.append(er)
            eis.append(ei)
        outs_re.append(ers[8])
        outs_im.append(eis[8])
        ent_r, ent_i = jnp.concatenate(ers[:8], axis=0), jnp.concatenate(eis[:8], axis=0)
        e_re[:, lanes] = ent_r
        e_im[:, lanes] = ent_i

        def fix(i, carry, lanes=lanes, ar=ar, ai=ai):
            r = pl.multiple_of(i * 8, 8)
            zr, zi = ar * carry[0] - ai * carry[1], ar * carry[1] + ai * carry[0]
            s_re[pl.ds(r, 8), lanes] = s_re[pl.ds(r, 8), lanes] + zr
            s_im[pl.ds(r, 8), lanes] = s_im[pl.ds(r, 8), lanes] + zi
            return zr, zi

        lax.fori_loop(0, t8, fix, (ent_r, ent_i), unroll=True)
    return jnp.concatenate(outs_re, axis=1), jnp.concatenate(outs_im, axis=1)


def _s5_readout(s_re, s_im, cre_ref, cim_ref):
    ys = []
    for k in range(S5_KBLK):
        lanes = slice(512 * k, 512 * (k + 1))
        ys.append(_mm(s_re[:, lanes], cre_ref[k]) - _mm(s_im[:, lanes], cim_ref[k]))
    return jnp.concatenate(ys, axis=1)


def s5_forward(p, wbd, cre, cim, atab, d_skip, w_glu, b_glu, plan=None):
    L = p.shape[0]
    tl = min(TL_S5, L)
    nch = L // tl

    def body(u_ref, z_ref, wbd_ref, cre_ref, cim_ref, at_ref, d_ref, wg_ref, bg_ref,
             ya_ref, st_re_ref, st_im_ref, s_re, s_im, e_re, e_im, car_re, car_im):
        @pl.when(pl.program_id(0) == 0)
        def _():
            car_re[...] = jnp.zeros_like(car_re)
            car_im[...] = jnp.zeros_like(car_im)

        c0_re, c0_im = car_re[...], car_im[...]
        st_re_ref[0] = c0_re
        st_im_ref[0] = c0_im
        u = u_ref[...]
        x_re, x_im = _s5_states(u, wbd_ref, at_ref[0:1], at_ref[1:2], at_ref[2:3], at_ref[3:4],
                                s_re, s_im, e_re, e_im, c0_re, c0_im, tl)
        car_re[...] = x_re
        car_im[...] = x_im
        y = _s5_readout(s_re, s_im, cre_ref, cim_ref) + d_ref[...] * u
        yg = _gelu(y)
        gate = _sigmoid(_mm(yg, wg_ref[...]) + bg_ref[...])
        sz, _ = _silu_and_grad(z_ref[...])
        ya_ref[...] = (yg * gate * sz).astype(ya_ref.dtype)

    return _call(
        body, plan, name="s5_forward", grid=(nch,),
        in_specs=[pl.BlockSpec((tl, 1024), lambda i: (i, 0)), pl.BlockSpec((tl, 1024), lambda i: (i, 1)),
                  _full(wbd.shape), _full(cre.shape), _full(cim.shape), _full(atab.shape),
                  _full((1, 1024)), _full((1024, 1024)), _full((1, 1024))],
        out_specs=[pl.BlockSpec((tl, 1024), lambda i: (i, 0)),
                   pl.BlockSpec((1, 1, S5_LANES), lambda i: (i, 0, 0)),
                   pl.BlockSpec((1, 1, S5_LANES), lambda i: (i, 0, 0))],
        out_shape=[jax.ShapeDtypeStruct((L, 1024), MXU_DTYPE),
                   jax.ShapeDtypeStruct((nch, 1, S5_LANES), F32), jax.ShapeDtypeStruct((nch, 1, S5_LANES), F32)],
        scratch_shapes=[pltpu.VMEM((tl, S5_LANES), F32), pltpu.VMEM((tl, S5_LANES), F32),
                        pltpu.VMEM((8, S5_LANES), F32), pltpu.VMEM((8, S5_LANES), F32),
                        pltpu.VMEM((1, S5_LANES), F32), pltpu.VMEM((1, S5_LANES), F32)],
        sem=("arbitrary",),
    )(p, p, wbd, cre, cim, atab, d_skip, w_glu, b_glu)


def s5_backward(p, dya, st_re, st_im, wbd, cre, cim, atab, d_skip, w_glu, b_glu, plan=None):
    L = p.shape[0]
    tl = min(TL_S5, L)
    t8 = tl // 8
    nch = L // tl
    rev = lambda i: (nch - 1 - i, 0)
    rev1 = lambda i: (nch - 1 - i, 1)
    rev3 = lambda i: (nch - 1 - i, 0, 0)

    def body(u_ref, z_ref, dya_ref, str_ref, sti_ref, wbd_ref, cre_ref, cim_ref, at_ref,
             d_ref, wg_ref, bg_ref,
             dp_ref, dwbd_ref, dcre_ref, dcim_ref, dabr_ref, dabi_ref, dd_ref, dwg_ref, dbg_ref,
             s_re, s_im, g_re, g_im, e_re, e_im, car_re, car_im):
        @pl.when(pl.program_id(0) == 0)
        def _():
            car_re[...] = jnp.zeros_like(car_re)
            car_im[...] = jnp.zeros_like(car_im)
            for r in (dwbd_ref, dcre_ref, dcim_ref, dabr_ref, dabi_ref, dd_ref, dwg_ref, dbg_ref):
                r[...] = jnp.zeros_like(r)

        u = u_ref[...]
        a_re, a_im, at_re, at_im = at_ref[0:1], at_ref[1:2], at_ref[2:3], at_ref[3:4]
        _s5_states(u, wbd_ref, a_re, a_im, at_re, at_im, s_re, s_im, e_re, e_im, str_ref[0], sti_ref[0], tl)
        y = _s5_readout(s_re, s_im, cre_ref, cim_ref) + d_ref[...] * u
        yg, dyg = _gelu_and_grad(y)
        gate = _sigmoid(_mm(yg, wg_ref[...]) + bg_ref[...])
        sz, dsz = _silu_and_grad(z_ref[...])
        dya = dya_ref[...]
        s5out = yg * gate
        dp_ref[:, 1024:] = (dya * s5out * dsz).astype(dp_ref.dtype)
        ds5 = dya * sz
        dt = ds5 * yg * gate * (1.0 - gate)
        dwg_ref[...] += _mm_tn(yg, dt)
        dbg_ref[...] += jnp.sum(dt, axis=0, keepdims=True)
        dyv = (ds5 * gate + _mm_nt(dt, wg_ref[...])) * dyg
        dd_ref[...] += jnp.sum(dyv * u, axis=0, keepdims=True)

        for k in range(S5_KBLK):
            lanes = slice(512 * k, 512 * (k + 1))
            dyk = dyv[:, 128 * k:128 * (k + 1)]
            g_re[:, lanes] = _mm_nt(dyk, cre_ref[k])
            g_im[:, lanes] = -_mm_nt(dyk, cim_ref[k])
            dcre_ref[k] += _mm_tn(s_re[:, lanes], dyk)
            dcim_ref[k] -= _mm_tn(s_im[:, lanes], dyk)

        for b in range(S5_LANES // _LANE_BLK):
            lanes = slice(_LANE_BLK * b, _LANE_BLK * (b + 1))
            ar = jnp.broadcast_to(a_re[:, lanes], (8, _LANE_BLK))
            ai = jnp.broadcast_to(a_im[:, lanes], (8, _LANE_BLK))

            def local(j, carry, lanes=lanes, ar=ar, ai=ai):
                r = pl.multiple_of((t8 - 1 - j) * 8, 8)
                gr, gi = _cmulc_add(ar, ai, carry[0], carry[1], g_re[pl.ds(r, 8), lanes], g_im[pl.ds(r, 8), lanes])
                g_re[pl.ds(r, 8), lanes] = gr
                g_im[pl.ds(r, 8), lanes] = gi
                return gr, gi

            zero = jnp.zeros((8, _LANE_BLK), F32)
            fr, fi = lax.fori_loop(0, t8, local, (zero, zero), unroll=True)
            tr, ti = at_re[:, lanes], at_im[:, lanes]
            hr, hi = car_re[:, lanes], car_im[:, lanes]
            hrs, his = [hr], [hi]
            for j in range(7, -1, -1):
                hr, hi = _cmulc_add(tr, ti, hr, hi, fr[j:j + 1], fi[j:j + 1])
                hrs.append(hr)
                his.append(hi)
            car_re[:, lanes] = hrs[8]
            car_im[:, lanes] = his[8]
            in_r = jnp.concatenate(hrs[7::-1], axis=0)
            in_i = jnp.concatenate(his[7::-1], axis=0)

            def fix(j, carry, lanes=lanes, ar=ar, ai=ai):
                wr, wi, accr, acci = carry
                r = pl.multiple_of((t8 - 1 - j) * 8, 8)
                wr, wi = ar * wr + ai * wi, ar * wi - ai * wr
                gr, gi = g_re[pl.ds(r, 8), lanes] + wr, g_im[pl.ds(r, 8), lanes] + wi
                g_re[pl.ds(r, 8), lanes] = gr
                g_im[pl.ds(r, 8), lanes] = gi
                sr, si = s_re[pl.ds(r - 8, 8), lanes], s_im[pl.ds(r - 8, 8), lanes]
                return wr, wi, accr + (sr * gr + si * gi), acci + (sr * gi - si * gr)

            wr, wi, accr, acci = lax.fori_loop(0, t8 - 1, fix, (in_r, in_i, zero, zero), unroll=True)
            wr, wi = ar * wr + ai * wi, ar * wi - ai * wr
            gr, gi = g_re[pl.ds(0, 8), lanes] + wr, g_im[pl.ds(0, 8), lanes] + wi
            g_re[pl.ds(0, 8), lanes] = gr
            g_im[pl.ds(0, 8), lanes] = gi
            sr, si = e_re[:, lanes], e_im[:, lanes]
            dabr_ref[:, lanes] += accr + (sr * gr + si * gi)
            dabi_ref[:, lanes] += acci + (sr * gi - si * gr)

        dus = []
        for k in range(S5_KBLK):
            lanes = slice(512 * k, 512 * (k + 1))
            g = jnp.concatenate([g_re[:, lanes], g_im[:, lanes]], axis=1)
            dwbd_ref[k] += _mm_tn(u[:, 128 * k:128 * (k + 1)], g)
            dus.append(_mm_nt(g, wbd_ref[k]))
        du = jnp.concatenate(dus, axis=1) + dyv * d_ref[...]
        dp_ref[:, :1024] = du.astype(dp_ref.dtype)

    shp = lambda *s: jax.ShapeDtypeStruct(s, F32)
    return _call(
        body, plan, name="s5_backward", grid=(nch,),
        in_specs=[pl.BlockSpec((tl, 1024), rev), pl.BlockSpec((tl, 1024), rev1), pl.BlockSpec((tl, 1024), rev),
                  pl.BlockSpec((1, 1, S5_LANES), rev3), pl.BlockSpec((1, 1, S5_LANES), rev3),
                  _full(wbd.shape), _full(cre.shape), _full(cim.shape), _full(atab.shape),
                  _full((1, 1024)), _full((1024, 1024)), _full((1, 1024))],
        out_specs=[pl.BlockSpec((tl, 2048), rev), _full(wbd.shape), _full(cre.shape), _full(cim.shape),
                   _full((8, S5_LANES)), _full((8, S5_LANES)), _full((1, 1024)), _full((1024, 1024)), _full((1, 1024))],
        out_shape=[jax.ShapeDtypeStruct((L, 2048), MXU_DTYPE), shp(*wbd.shape), shp(*cre.shape), shp(*cim.shape),
                   shp(8, S5_LANES), shp(8, S5_LANES), shp(1, 1024), shp(1024, 1024), shp(1, 1024)],
        scratch_shapes=[pltpu.VMEM((tl, S5_LANES), F32), pltpu.VMEM((tl, S5_LANES), F32),
                        pltpu.VMEM((tl, S5_LANES), F32), pltpu.VMEM((tl, S5_LANES), F32),
                        pltpu.VMEM((8, S5_LANES), F32), pltpu.VMEM((8, S5_LANES), F32),
                        pltpu.VMEM((1, S5_LANES), F32), pltpu.VMEM((1, S5_LANES), F32)],
        sem=("arbitrary",),
    )(p, p, dya, st_re, st_im, wbd, cre, cim, atab, d_skip, w_glu, b_glu)


def _block_diag(w, rows_first):
    g8 = w.reshape(S5_KBLK, 8, w.shape[1], w.shape[2])
    eye = jnp.eye(8, dtype=w.dtype)
    out = jnp.einsum('kgab,fg->kfagb', g8, eye)
    return out.reshape(S5_KBLK, 8 * w.shape[1], 8 * w.shape[2])


def _block_diag_extract(wbd, a, b):
    w5 = wbd.reshape(S5_KBLK, 8, a, 8, b)
    idx = jnp.arange(8)
    return w5[:, idx, :, idx, :].transpose(1, 0, 2, 3).reshape(S5_GROUPS, a, b)


def _ret_constants():
    log_g = np.log1p(-np.exp2(-5.0 - np.arange(RET_HEADS, dtype=np.float32))).astype(np.float32)
    idx = np.arange(RET_CHUNK, dtype=np.float32)
    diff = idx[:, None] - idx[None, :]
    decay = np.where(diff >= 0, np.exp(log_g[:, None, None] * np.maximum(diff, 0.0)), 0.0).astype(np.float32)
    xi = np.exp(log_g[None, :] * (idx[:, None] + 1.0)).astype(np.float32)
    zeta = np.exp(log_g[None, :] * (RET_CHUNK - 1.0 - idx[:, None])).astype(np.float32)
    chunk_decay = np.exp(log_g * RET_CHUNK).astype(np.float32)
    return decay, xi, zeta, chunk_decay


def _rope_tables(L):
    half = RET_DK // 2
    inv = ROPE_BASE ** (-jnp.arange(half, dtype=F32) / half)
    ang = jnp.arange(L, dtype=F32)[:, None] * inv[None, :]
    return jnp.cos(ang), jnp.sin(ang)


def _rot(xh, cos, sin):
    x1, x2 = xh[:, :128], xh[:, 128:]
    return jnp.concatenate([x1 * cos - x2 * sin, x1 * sin + x2 * cos], axis=1)


def _rot_t(dh, cos, sin):
    d1, d2 = dh[:, :128], dh[:, 128:]
    return jnp.concatenate([d1 * cos + d2 * sin, d2 * cos - d1 * sin], axis=1)


def retention_forward(p, cos, sin, gain):
    L = p.shape[0]
    nc = L // RET_CHUNK
    decay_np, xi_np, zeta_np, cd_np = _ret_constants()
    decay, xi, zeta = jnp.asarray(decay_np), jnp.asarray(xi_np), jnp.asarray(zeta_np)
    scale = RET_DK ** -0.5

    def body(q_ref, k_ref, v_ref, z_ref, cos_ref, sin_ref, dec_ref, xi_ref, zeta_ref, gain_ref,
             yb_ref, prev_ref, state):
        @pl.when(pl.program_id(0) == 0)
        def _():
            state[...] = jnp.zeros_like(state)

        H = range(RET_HEADS)
        hs = [slice(RET_DK * h, RET_DK * (h + 1)) for h in H]
        cs, sn = cos_ref[...], sin_ref[...]
        sz, _ = _silu_and_grad(z_ref[...])
        qh = [_rot(q_ref[:, hs[h]], cs, sn) for h in H]
        kh = [_rot(k_ref[:, hs[h]], cs, sn) * scale for h in H]
        vh = [v_ref[:, hs[h]] for h in H]
        prev = [state[h] for h in H]
        sc = [_mm_nt(qh[h], kh[h]) * dec_ref[h] for h in H]
        cross = [_mm(qh[h] * xi_ref[:, h:h + 1], prev[h]) for h in H]
        loc = [_mm_tn(kh[h] * zeta_ref[:, h:h + 1], vh[h]) for h in H]
        o = [_mm(sc[h], vh[h]) + cross[h] for h in H]
        oc = [o[h] - jnp.mean(o[h], axis=-1, keepdims=True) for h in H]
        on = [oc[h] * lax.rsqrt(jnp.mean(oc[h] * oc[h], axis=-1, keepdims=True) + NORM_EPS) for h in H]
        for h in H:
            prev_ref[0, h] = prev[h].astype(prev_ref.dtype)
            state[h] = prev[h] * float(cd_np[h]) + loc[h]
            yb_ref[:, hs[h]] = (on[h] * gain_ref[:, hs[h]] * sz[:, hs[h]]).astype(yb_ref.dtype)

    blk = lambda c: pl.BlockSpec((RET_CHUNK, 1024), lambda i, c=c: (i, c))
    return pl.pallas_call(
        body, name="retention_forward", grid=(nc,),
        in_specs=[blk(0), blk(1), blk(2), blk(3),
                  pl.BlockSpec((RET_CHUNK, 128), lambda i: (i, 0)), pl.BlockSpec((RET_CHUNK, 128), lambda i: (i, 0)),
                  _full(decay.shape), _full(xi.shape), _full(zeta.shape), _full((1, 1024))],
        out_specs=[pl.BlockSpec((RET_CHUNK, 1024), lambda i: (i, 0)),
                   pl.BlockSpec((1, RET_HEADS, RET_DK, RET_DK), lambda i: (i, 0, 0, 0))],
        out_shape=[jax.ShapeDtypeStruct((L, 1024), MXU_DTYPE),
                   jax.ShapeDtypeStruct((nc, RET_HEADS, RET_DK, RET_DK), MXU_DTYPE)],
        scratch_shapes=[pltpu.VMEM((RET_HEADS, RET_DK, RET_DK), F32)],
        compiler_params=_cparams(("arbitrary",)),
    )(p, p, p, p, cos, sin, decay, xi, zeta, gain)


def retention_backward(p, dy, prevs, cos, sin, gain, plan=None):
    L = p.shape[0]
    nc = L // RET_CHUNK
    decay_np, xi_np, zeta_np, cd_np = _ret_constants()
    decay, xi, zeta = jnp.asarray(decay_np), jnp.asarray(xi_np), jnp.asarray(zeta_np)
    scale = RET_DK ** -0.5

    def body(q_ref, k_ref, v_ref, z_ref, dyb_ref, prev_ref, cos_ref, sin_ref, dec_ref, xi_ref, zeta_ref, gain_ref,
             dp_ref, dgain_ref, dstate):
        @pl.when(pl.program_id(0) == 0)
        def _():
            dstate[...] = jnp.zeros_like(dstate)
            dgain_ref[...] = jnp.zeros_like(dgain_ref)

        H = range(RET_HEADS)
        hs = [slice(RET_DK * h, RET_DK * (h + 1)) for h in H]
        cs, sn = cos_ref[...], sin_ref[...]
        sz, dsz = _silu_and_grad(z_ref[...])
        dyb = dyb_ref[...]
        xih = [xi_ref[:, h:h + 1] for h in H]
        zth = [zeta_ref[:, h:h + 1] for h in H]
        qh = [_rot(q_ref[:, hs[h]], cs, sn) for h in H]
        kh = [_rot(k_ref[:, hs[h]], cs, sn) * scale for h in H]
        vh = [v_ref[:, hs[h]] for h in H]
        prev = [prev_ref[0, h] for h in H]
        dst = [dstate[h] for h in H]
        qx = [qh[h] * xih[h] for h in H]
        kz = [kh[h] * zth[h] for h in H]
        sc = [_mm_nt(qh[h], kh[h]) * dec_ref[h] for h in H]
        cross = [_mm(qx[h], prev[h]) for h in H]
        dk_st = [_mm_nt(vh[h], dst[h]) for h in H]
        dv_st = [_mm(kz[h], dst[h]) for h in H]
        o = [_mm(sc[h], vh[h]) + cross[h] for h in H]
        oc = [o[h] - jnp.mean(o[h], axis=-1, keepdims=True) for h in H]
        rstd = [lax.rsqrt(jnp.mean(oc[h] * oc[h], axis=-1, keepdims=True) + NORM_EPS) for h in H]
        on = [oc[h] * rstd[h] for h in H]
        dong = [dyb[:, hs[h]] * sz[:, hs[h]] for h in H]
        don = [dong[h] * gain_ref[:, hs[h]] for h in H]
        do = [rstd[h] * (don[h] - jnp.mean(don[h], axis=-1, keepdims=True)
                         - on[h] * jnp.mean(don[h] * on[h], axis=-1, keepdims=True)) for h in H]
        dsc = [_mm_nt(do[h], vh[h]) * dec_ref[h] for h in H]
        dq_st = [_mm_nt(do[h], prev[h]) for h in H]
        dnew = [_mm_tn(qx[h], do[h]) for h in H]
        dqh = [_mm(dsc[h], kh[h]) + dq_st[h] * xih[h] for h in H]
        dkh = [_mm_tn(dsc[h], qh[h]) + dk_st[h] * zth[h] for h in H]
        dvh = [_mm_tn(sc[h], do[h]) + dv_st[h] for h in H]
        for h in H:
            dstate[h] = dst[h] * float(cd_np[h]) + dnew[h]
            dgain_ref[:, hs[h]] += jnp.sum(dong[h] * on[h], axis=0, keepdims=True)
            dp_ref[:, hs[h]] = _rot_t(dqh[h], cs, sn).astype(dp_ref.dtype)
            dp_ref[:, 1024 + RET_DK * h:1024 + RET_DK * (h + 1)] = (_rot_t(dkh[h], cs, sn) * scale).astype(dp_ref.dtype)
            dp_ref[:, 2048 + RET_DK * h:2048 + RET_DK * (h + 1)] = dvh[h].astype(dp_ref.dtype)
            dp_ref[:, 3072 + RET_DK * h:3072 + RET_DK * (h + 1)] = (
                dyb[:, hs[h]] * on[h] * gain_ref[:, hs[h]] * dsz[:, hs[h]]).astype(dp_ref.dtype)

    blk = lambda c: pl.BlockSpec((RET_CHUNK, 1024), lambda i, c=c: (nc - 1 - i, c))
    tab = pl.BlockSpec((RET_CHUNK, 128), lambda i: (nc - 1 - i, 0))
    return _call(
        body, plan, name="retention_backward", grid=(nc,),
        in_specs=[blk(0), blk(1), blk(2), blk(3), blk(0),
                  pl.BlockSpec((1, RET_HEADS, RET_DK, RET_DK), lambda i: (nc - 1 - i, 0, 0, 0)),
                  tab, tab, _full(decay.shape), _full(xi.shape), _full(zeta.shape), _full((1, 1024))],
        out_specs=[pl.BlockSpec((RET_CHUNK, 4096), lambda i: (nc - 1 - i, 0)), _full((1, 1024))],
        out_shape=[jax.ShapeDtypeStruct((L, 4096), MXU_DTYPE), jax.ShapeDtypeStruct((1, 1024), F32)],
        scratch_shapes=[pltpu.VMEM((RET_HEADS, RET_DK, RET_DK), F32)],
        sem=("arbitrary",),
    )(p, p, p, p, dy, prevs, cos, sin, decay, xi, zeta, gain)


def _sgu_mix(p_ref, gain_ref, wm_ref, bt_ref, tl):
    pu, pv, z = p_ref[:, :2048], p_ref[:, 2048:4096], p_ref[:, 4096:]
    (u, du), (v, dv) = _gelu_and_grad(pu), _gelu_and_grad(pv)
    mu = jnp.mean(v, axis=-1, keepdims=True)
    vc = v - mu
    rstd = lax.rsqrt(jnp.mean(vc * vc, axis=-1, keepdims=True) + NORM_EPS)
    vn = vc * rstd
    vg = vn * gain_ref[...]
    mask = (lax.broadcasted_iota(jnp.int32, (SGU_CHUNK, SGU_CHUNK), 0)
            >= lax.broadcasted_iota(jnp.int32, (SGU_CHUNK, SGU_CHUNK), 1))
    wms = [jnp.where(mask, wm_ref[g], 0.0) for g in range(SGU_GROUPS)]
    rows = []
    for c in range(tl // SGU_CHUNK):
        rs = slice(SGU_CHUNK * c, SGU_CHUNK * (c + 1))
        cols = []
        for g in range(SGU_GROUPS):
            gs = slice(SGU_GDIM * g, SGU_GDIM * (g + 1))
            cols.append(_mm(wms[g], vg[rs, gs]) + bt_ref[:, g:g + 1])
        rows.append(jnp.concatenate(cols, axis=1))
    s = rows[0] if len(rows) == 1 else jnp.concatenate(rows, axis=0)
    return du, dv, z, u, vn, rstd, vg, wms, mask, s


def sgu_forward(p, gain, wm, bt):
    L = p.shape[0]
    tl = min(TL_SGU, L)

    def body(p_ref, gain_ref, wm_ref, bt_ref, y_ref):
        _, _, z, u, _, _, _, _, _, s = _sgu_mix(p_ref, gain_ref, wm_ref, bt_ref, tl)
        sz, _ = _silu_and_grad(z)
        y_ref[...] = (u * s * sz).astype(y_ref.dtype)

    return pl.pallas_call(
        body, name="sgu_forward", grid=(L // tl,),
        in_specs=[pl.BlockSpec((tl, ODD_IN), lambda i: (i, 0)), _full((1, 2048)), _full(wm.shape), _full(bt.shape)],
        out_specs=pl.BlockSpec((tl, 2048), lambda i: (i, 0)),
        out_shape=jax.ShapeDtypeStruct((L, 2048), MXU_DTYPE),
        compiler_params=_cparams(("arbitrary",)),
    )(p, gain, wm, bt)


def sgu_backward(p, dy, gain, wm, bt, plan=None):
    L = p.shape[0]
    tl = min(TL_SGU, L)

    def body(p_ref, dy_ref, gain_ref, wm_ref, bt_ref, dp_ref, dgain_ref, dwm_ref, dbt_ref):
        @pl.when(pl.program_id(0) == 0)
        def _():
            dgain_ref[...] = jnp.zeros_like(dgain_ref)
            dwm_ref[...] = jnp.zeros_like(dwm_ref)
            dbt_ref[...] = jnp.zeros_like(dbt_ref)

        gu, gv, z, u, vn, rstd, vg, wms, mask, s = _sgu_mix(p_ref, gain_ref, wm_ref, bt_ref, tl)
        sz, dsz = _silu_and_grad(z)
        dyv = dy_ref[...]
        dp_ref[:, 4096:] = (dyv * u * s * dsz).astype(dp_ref.dtype)
        dsg = dyv * sz
        dp_ref[:, :2048] = (dsg * s * gu).astype(dp_ref.dtype)
        ds = dsg * u
        rows = []
        dbs = [jnp.zeros((SGU_CHUNK, 1), F32) for _ in range(SGU_GROUPS)]
        for c in range(tl // SGU_CHUNK):
            rs = slice(SGU_CHUNK * c, SGU_CHUNK * (c + 1))
            cols = []
            for g in range(SGU_GROUPS):
                gs = slice(SGU_GDIM * g, SGU_GDIM * (g + 1))
                dsg_c = ds[rs, gs]
                dbs[g] = dbs[g] + jnp.sum(dsg_c, axis=1, keepdims=True)
                dwm_ref[g] += jnp.where(mask, _mm_nt(dsg_c, vg[rs, gs]), 0.0)
                cols.append(_mm_tn(wms[g], dsg_c))
            rows.append(jnp.concatenate(cols, axis=1))
        dbt_ref[...] += jnp.concatenate(dbs, axis=1)
        dvg = rows[0] if len(rows) == 1 else jnp.concatenate(rows, axis=0)
        dgain_ref[...] += jnp.sum(dvg * vn, axis=0, keepdims=True)
        dvn = dvg * gain_ref[...]
        dv = rstd * (dvn - jnp.mean(dvn, axis=-1, keepdims=True) - vn * jnp.mean(dvn * vn, axis=-1, keepdims=True))
        dp_ref[:, 2048:4096] = (dv * gv).astype(dp_ref.dtype)

    return _call(
        body, plan, name="sgu_backward", grid=(L // tl,),
        in_specs=[pl.BlockSpec((tl, ODD_IN), lambda i: (i, 0)), pl.BlockSpec((tl, 2048), lambda i: (i, 0)),
                  _full((1, 2048)), _full(wm.shape), _full(bt.shape)],
        out_specs=[pl.BlockSpec((tl, ODD_IN), lambda i: (i, 0)), _full((1, 2048)), _full(wm.shape), _full(bt.shape)],
        out_shape=[jax.ShapeDtypeStruct((L, ODD_IN), MXU_DTYPE), jax.ShapeDtypeStruct((1, 2048), F32),
                   jax.ShapeDtypeStruct(wm.shape, F32), jax.ShapeDtypeStruct(bt.shape, F32)],
        sem=("arbitrary",),
    )(p, dy, gain, wm, bt)


def cast_shards(mats):
    n = len(mats)

    def body(*refs):
        for p in range(n):
            refs[n + p][...] = refs[p][...].astype(MXU_DTYPE)

    return pl.pallas_call(
        body, name="cast_shards", out_shape=[jax.ShapeDtypeStruct(m.shape, MXU_DTYPE) for m in mats],
        compiler_params=pltpu.CompilerParams(vmem_limit_bytes=VMEM_LIMIT),
    )(*mats)


def local_grads(x, tgt, w):
    L = x.shape[0]
    ne, gf = w["norm_even"], w["final_norm"].reshape(1, D_MODEL)
    sh = dict(zip(MATRICES, cast_shards([w[n][0] for n in MATRICES])))
    (w_in_e,) = run_plan(gather_plan([sh["w_in_even"]]), "gather_w_in_even")
    lam_re, lam_im = w["s5_lam_re"][0], w["s5_lam_im"][0]
    log_dt = w["s5_log_dt"].reshape(S5_GROUPS, 1)
    bt_re = jnp.transpose(w["s5_b_re"][0], (2, 0, 1))
    bt_im = jnp.transpose(w["s5_b_im"][0], (2, 0, 1))
    c_re, c_im = w["s5_c_re"][0], w["s5_c_im"][0]
    wm = w["sgu_w_spatial"][0]
    bt = jnp.transpose(w["sgu_b_spatial"][0])

    tl5 = min(TL_S5, L)
    ab_re, ab_im, bb_re, bb_im, at_re, at_im = s5_params_fwd(lam_re, lam_im, log_dt, bt_re, bt_im, tl5 // 8)
    atab = jnp.stack([ab_re.reshape(S5_LANES), ab_im.reshape(S5_LANES),
                      at_re.reshape(S5_LANES), at_im.reshape(S5_LANES)])
    wbd = jnp.concatenate([_block_diag(jnp.transpose(bb_re, (1, 0, 2)), True),
                           _block_diag(jnp.transpose(bb_im, (1, 0, 2)), True)], axis=2).astype(MXU_DTYPE)
    cre = _block_diag(jnp.transpose(c_re, (0, 2, 1)), True).astype(MXU_DTYPE)
    cim = _block_diag(jnp.transpose(c_im, (0, 2, 1)), True).astype(MXU_DTYPE)
    cos, sin = _rope_tables(L)

    s5_cols = 2 * S5_WIDTH
    w_s5 = jnp.concatenate([w_in_e[0], w_in_e[1][:, :s5_cols - EVEN_IN // N_CHIPS]], axis=1)
    w_ret = jnp.concatenate([w_in_e[1][:, s5_cols - EVEN_IN // N_CHIPS:], w_in_e[2], w_in_e[3]], axis=1)
    (p1a, h0s), (w_glu,) = norm_matmul(stream_order(x, tl5), ne, w_s5, "even_in_s5",
                                       gather_plan([sh["s5_w_glu"]]), tn=1024)
    (p1b, h0), (w_out_e,) = norm_matmul(x, ne, w_ret, "even_in_ret", gather_plan([sh["w_out_even"]]), tn=1024)
    w_glu = w_glu.reshape(S5_WIDTH, S5_WIDTH)
    w_out_e = w_out_e.reshape(2 * S5_WIDTH, D_MODEL)
    (ya, st_re, st_im), (w_in_o, w_out_o, no, sg_gain) = s5_forward(
        p1a, wbd, cre, cim, atab, w["s5_d"], w_glu, w["s5_b_glu"],
        gather_plan([sh["w_in_odd"], sh["w_out_odd"], w["norm_odd"], w["sgu_norm_gain"]]))
    w_out_o = w_out_o.reshape(SGU_WIDTH, D_MODEL)
    no, sg_gain = no.reshape(1, D_MODEL), sg_gain.reshape(1, SGU_WIDTH)
    yb, prevs = retention_forward(p1b, cos, sin, w["ret_gn_gain"])
    ya = token_order(ya, tl5)
    x1 = matmul_residual([ya, yb], w_out_e, x, "even_out")
    (p2, h1), _ = norm_matmul(x1, no, w_in_o, "odd_in")
    y2 = sgu_forward(p2, sg_gain, wm, bt)
    dx2, loss, dgf = out_proj_loss(y2, w_out_o, x1, gf, tgt, "odd_out_loss")

    g, landed = {}, {}
    shard_major = lambda a, n: a.reshape((N_CHIPS,) + w[n].shape[1:])
    dy2, g_w_out_o = out_proj_bwd(dx2, w_out_o, [y2], "odd_out_bwd")
    (dp2, g["sgu_norm_gain"], dwm, dbt), (landed["w_out_odd"],) = sgu_backward(
        p2, dy2, sg_gain, wm, bt, reduce_plan([shard_major(g_w_out_o, "w_out_odd")]))
    g_w_in_o = in_proj_bwd_dw(h1, dp2, "odd_in_dw", ODD_IN // N_CHIPS)
    (dx1, g["norm_odd"]), _ = in_proj_bwd_dx(x1, no, [dp2], [w_in_o], dx2, "odd_in_dx")
    dya, dyb, g_w_out_e = out_proj_bwd(dx1, w_out_e, [ya, yb], "even_out_bwd")
    ((dpa, dwbd, dcre, dcim, dab_re, dab_im, g["s5_d"], g_w_glu, g["s5_b_glu"]),
     (landed["w_in_odd"], landed["w_out_even"])) = s5_backward(
        p1a, stream_order(dya, tl5), st_re, st_im, wbd, cre, cim, atab, w["s5_d"], w_glu,
        w["s5_b_glu"], reduce_plan([g_w_in_o, shard_major(g_w_out_e, "w_out_even")]))

    dbb_re = jnp.transpose(_block_diag_extract(dwbd[:, :, :512], S5_GROUP, S5_STATE), (1, 0, 2))
    dbb_im = jnp.transpose(_block_diag_extract(dwbd[:, :, 512:], S5_GROUP, S5_STATE), (1, 0, 2))
    dlr, dli, ddt, dbt_re, dbt_im = s5_params_bwd(
        lam_re, lam_im, log_dt, bt_re, bt_im, dab_re.reshape(8, S5_GROUPS, S5_STATE),
        dab_im.reshape(8, S5_GROUPS, S5_STATE), dbb_re, dbb_im)
    g["s5_lam_re"], g["s5_lam_im"] = dlr[None], dli[None]
    g["s5_log_dt"] = ddt.reshape(1, S5_GROUPS)
    g["s5_b_re"] = jnp.transpose(dbt_re, (1, 2, 0))[None]
    g["s5_b_im"] = jnp.transpose(dbt_im, (1, 2, 0))[None]
    g["s5_c_re"] = jnp.transpose(_block_diag_extract(dcre, S5_STATE, S5_GROUP), (0, 2, 1))[None]
    g["s5_c_im"] = jnp.transpose(_block_diag_extract(dcim, S5_STATE, S5_GROUP), (0, 2, 1))[None]
    g["sgu_w_spatial"] = dwm[None]
    g["sgu_b_spatial"] = jnp.transpose(dbt)[None]
    g["final_norm"] = dgf.reshape(D_MODEL)
    g["loss"] = loss

    early = BEHIND_RETENTION_BWD + ("loss",)
    (dpb, g["ret_gn_gain"]), recv = retention_backward(
        p1b, dyb, prevs, cos, sin, w["ret_gn_gain"],
        reduce_plan([shard_major(g_w_glu, "s5_w_glu")], [g[n] for n in early]))
    landed.update(zip(("s5_w_glu",) + early, recv))
    g_w_in_e = in_proj_bwd_dw(h0s, dpa, "even_in_dw_s5", 512, dtype=MXU_DTYPE)
    g_w_in_e = in_proj_bwd_dw(h0, dpb, "even_in_dw_ret", 512, first=s5_cols // 512, into=g_w_in_e, dtype=MXU_DTYPE)
    (dx0, g["norm_even"]), recv = in_proj_bwd_dx(
        x, ne, [token_order(dpa, tl5), dpb], [w_s5, w_ret], dx1, "even_in_dx",
        reduce_plan([g_w_in_e], [g[n] for n in BEHIND_EVEN_IN_DX]))
    landed.update(zip(("w_in_even",) + BEHIND_EVEN_IN_DX, recv))
    (landed["norm_even"],) = run_plan(reduce_plan([], [g["norm_even"]]), "exchange_norm_even")
    return dx0, landed


def sibling_exchange(arrs):
    n = len(arrs)

    def body(*refs):
        in_refs, out_refs = refs[:n], refs[n:2 * n]
        send_sems, recv_sems = refs[2 * n:]
        x, y, c = _place()
        copies = [pltpu.make_async_remote_copy(
            src_ref=in_refs[p], dst_ref=out_refs[p], send_sem=send_sems.at[p], recv_sem=recv_sems.at[p],
            device_id=(x, y, 1 - c), device_id_type=MESH) for p in range(n)]
        for cp in copies:
            cp.start()
        for cp in copies:
            cp.wait_recv()
        for cp in copies:
            cp.wait_send()

    return pl.pallas_call(
        body, name="sibling_exchange", in_specs=[ANY] * n, out_specs=[ANY] * n,
        out_shape=[jax.ShapeDtypeStruct(a.shape, a.dtype) for a in arrs],
        scratch_shapes=[pltpu.SemaphoreType.DMA((n,)), pltpu.SemaphoreType.DMA((n,))],
    )(*arrs)


def _row_block(rows):
    return 128 if rows % 128 == 0 else rows


def sum_slabs(r, name):
    _, R, C = r.shape
    tr = _row_block(R)

    def body(r_ref, o_ref):
        a, b, c, d = (r_ref[k].astype(F32) for k in range(N_CHIPS))
        o_ref[...] = (a + b) + (c + d)

    return pl.pallas_call(
        body, name=name, grid=(R // tr,),
        in_specs=[pl.BlockSpec((N_CHIPS, tr, C), lambda i: (0, i, 0))],
        out_specs=pl.BlockSpec((tr, C), lambda i: (i, 0)),
        out_shape=jax.ShapeDtypeStruct((R, C), F32),
        compiler_params=_cparams(("arbitrary",)),
    )(r)


def _adam(w, m, v, g):
    mn = ADAM_B1 * m + (1.0 - ADAM_B1) * g
    vn = ADAM_B2 * v + (1.0 - ADAM_B2) * (g * g)
    m_hat = mn / (1.0 - ADAM_B1 ** ADAM_STEP)
    v_hat = vn / (1.0 - ADAM_B2 ** ADAM_STEP)
    return -ADAM_LR * (m_hat / (jnp.sqrt(v_hat) + ADAM_EPS) + ADAM_WD * w), mn, vn


def adam_update(w, m, v, ga, gb, name):
    R, C = w.shape
    tr = _row_block(R)

    def body(w_ref, m_ref, v_ref, ga_ref, gb_ref, g_out, d_out, m_out, v_out):
        g = ga_ref[...] + gb_ref[...]
        g_out[...] = g
        d_out[...], m_out[...], v_out[...] = _adam(w_ref[...], m_ref[...], v_ref[...], g)

    blk = pl.BlockSpec((tr, C), lambda i: (i, 0))
    return pl.pallas_call(
        body, name=name, grid=(R // tr,),
        in_specs=[blk] * 5, out_specs=[blk] * 4,
        out_shape=[jax.ShapeDtypeStruct((R, C), F32)] * 4,
        compiler_params=_cparams(("arbitrary",)),
    )(w, m, v, ga, gb)


WIDE_ROWS = ("s5_b_re", "s5_b_im")
GROUP_BLK = 8


def _by_groups(arrs, lead):
    def spec(a):
        blk = a.shape[:lead] + (GROUP_BLK,) + a.shape[lead + 1:]
        nd = len(a.shape)
        return pl.BlockSpec(blk, lambda i: (0,) * lead + (i,) + (0,) * (nd - lead - 1))
    return [spec(a) for a in arrs]


def sum_small(landed):
    def body(*refs):
        k = len(refs) // 2
        for i in range(k):
            r = refs[i]
            refs[k + i][...] = (r[0] + r[1]) + (r[2] + r[3])

    out = {}
    plain = [n for n in landed if n not in WIDE_ROWS]
    res = pl.pallas_call(
        functools.partial(body), name="sum_small",
        out_shape=[jax.ShapeDtypeStruct(landed[n].shape[1:], F32) for n in plain],
        compiler_params=pltpu.CompilerParams(vmem_limit_bytes=VMEM_LIMIT),
    )(*[landed[n] for n in plain])
    out.update(zip(plain, res))
    wide = [n for n in landed if n in WIDE_ROWS]
    if wide:
        ins = [landed[n] for n in wide]
        outs = [jax.ShapeDtypeStruct(a.shape[1:], F32) for a in ins]
        res = pl.pallas_call(
            functools.partial(body), name="sum_small_wide", grid=(S5_GROUPS // GROUP_BLK,),
            in_specs=_by_groups(ins, 2),
            out_specs=_by_groups(outs, 1), out_shape=outs, compiler_params=_cparams(("arbitrary",)),
        )(*ins)
        out.update(zip(wide, res))
    return out


def adam_small(names, w, m, v, ga, gb):
    def body(*refs):
        k = len(refs) // 9
        me = 2 * lax.axis_index("x") + lax.axis_index("y")
        for i in range(k):
            w_ref, m_ref, v_ref, ga_ref, gb_ref = refs[i], refs[k + i], refs[2 * k + i], refs[3 * k + i], refs[4 * k + i]
            size = w_ref.shape[-1]
            if ga_ref.shape != w_ref.shape:
                part = pl.ds(pl.multiple_of(me * size, LANES), size)
                g = ga_ref[:, part] + gb_ref[:, part]
            else:
                g = ga_ref[...] + gb_ref[...]
            refs[5 * k + i][...] = g
            refs[6 * k + i][...], refs[7 * k + i][...], refs[8 * k + i][...] = _adam(w_ref[...], m_ref[...], v_ref[...], g)

    def run(group, **kw):
        ins = [d[n] for d in (w, m, v, ga, gb) for n in group]
        outs = [jax.ShapeDtypeStruct(w[n].shape, F32) for _ in range(4) for n in group]
        res = pl.pallas_call(functools.partial(body), out_shape=outs, **kw)(*ins)
        k = len(group)
        return [dict(zip(group, res[j * k:(j + 1) * k])) for j in range(4)]

    plain = [n for n in names if n not in WIDE_ROWS]
    wide = [n for n in names if n in WIDE_ROWS]
    res = run(plain, name="adam_small", compiler_params=pltpu.CompilerParams(vmem_limit_bytes=VMEM_LIMIT))
    if wide:
        specs = _by_groups([w[n] for n in wide], 1)
        res_w = run(wide, name="adam_small_wide", grid=(S5_GROUPS // GROUP_BLK,), in_specs=specs * 5,
                    out_specs=specs * 4, compiler_params=_cparams(("arbitrary",)))
        for d, dw in zip(res, res_w):
            d.update(dw)
    return res


WEIGHTS = ("norm_even", "w_in_even", "s5_lam_re", "s5_lam_im", "s5_log_dt", "s5_b_re", "s5_b_im", "s5_c_re",
           "s5_c_im", "s5_d", "s5_w_glu", "s5_b_glu", "ret_gn_gain", "w_out_even", "norm_odd", "w_in_odd",
           "sgu_norm_gain", "sgu_w_spatial", "sgu_b_spatial", "w_out_odd", "final_norm")
MATRICES = ("w_in_even", "s5_w_glu", "w_out_even", "w_in_odd", "w_out_odd")
SHARDED_VECS = ("norm_odd", "sgu_norm_gain")
REPLICATED = tuple(n for n in WEIGHTS if n not in MATRICES and n not in SHARDED_VECS)
SMALL = tuple(n for n in WEIGHTS if n not in MATRICES)
BEHIND_RETENTION_BWD = tuple(n for n in SMALL if n not in ("ret_gn_gain", "norm_even"))
BEHIND_EVEN_IN_DX = ("ret_gn_gain",)
LANES = 128


def kernel(x, norm_even, w_in_even, s5_lam_re, s5_lam_im, s5_log_dt, s5_b_re, s5_b_im, s5_c_re, s5_c_im, s5_d, s5_w_glu, s5_b_glu, ret_gn_gain, w_out_even, norm_odd, w_in_odd, sgu_norm_gain, sgu_w_spatial, sgu_b_spatial, w_out_odd, final_norm, loss_target, m_norm_even, m_w_in_even, m_s5_lam_re, m_s5_lam_im, m_s5_log_dt, m_s5_b_re, m_s5_b_im, m_s5_c_re, m_s5_c_im, m_s5_d, m_s5_w_glu, m_s5_b_glu, m_ret_gn_gain, m_w_out_even, m_norm_odd, m_w_in_odd, m_sgu_norm_gain, m_sgu_w_spatial, m_sgu_b_spatial, m_w_out_odd, m_final_norm, v_norm_even, v_w_in_even, v_s5_lam_re, v_s5_lam_im, v_s5_log_dt, v_s5_b_re, v_s5_b_im, v_s5_c_re, v_s5_c_im, v_s5_d, v_s5_w_glu, v_s5_b_glu, v_ret_gn_gain, v_w_out_even, v_norm_odd, v_w_in_odd, v_sgu_norm_gain, v_sgu_w_spatial, v_sgu_b_spatial, v_w_out_odd, v_final_norm):
    w = dict(norm_even=norm_even, w_in_even=w_in_even, s5_lam_re=s5_lam_re, s5_lam_im=s5_lam_im, s5_log_dt=s5_log_dt, s5_b_re=s5_b_re, s5_b_im=s5_b_im, s5_c_re=s5_c_re, s5_c_im=s5_c_im, s5_d=s5_d, s5_w_glu=s5_w_glu, s5_b_glu=s5_b_glu, ret_gn_gain=ret_gn_gain, w_out_even=w_out_even, norm_odd=norm_odd, w_in_odd=w_in_odd, sgu_norm_gain=sgu_norm_gain, sgu_w_spatial=sgu_w_spatial, sgu_b_spatial=sgu_b_spatial, w_out_odd=w_out_odd, final_norm=final_norm)
    m = dict(norm_even=m_norm_even, w_in_even=m_w_in_even, s5_lam_re=m_s5_lam_re, s5_lam_im=m_s5_lam_im, s5_log_dt=m_s5_log_dt, s5_b_re=m_s5_b_re, s5_b_im=m_s5_b_im, s5_c_re=m_s5_c_re, s5_c_im=m_s5_c_im, s5_d=m_s5_d, s5_w_glu=m_s5_w_glu, s5_b_glu=m_s5_b_glu, ret_gn_gain=m_ret_gn_gain, w_out_even=m_w_out_even, norm_odd=m_norm_odd, w_in_odd=m_w_in_odd, sgu_norm_gain=m_sgu_norm_gain, sgu_w_spatial=m_sgu_w_spatial, sgu_b_spatial=m_sgu_b_spatial, w_out_odd=m_w_out_odd, final_norm=m_final_norm)
    v = dict(norm_even=v_norm_even, w_in_even=v_w_in_even, s5_lam_re=v_s5_lam_re, s5_lam_im=v_s5_lam_im, s5_log_dt=v_s5_log_dt, s5_b_re=v_s5_b_re, s5_b_im=v_s5_b_im, s5_c_re=v_s5_c_re, s5_c_im=v_s5_c_im, s5_d=v_s5_d, s5_w_glu=v_s5_w_glu, s5_b_glu=v_s5_b_glu, ret_gn_gain=v_ret_gn_gain, w_out_even=v_w_out_even, norm_odd=v_norm_odd, w_in_odd=v_w_in_odd, sgu_norm_gain=v_sgu_norm_gain, sgu_w_spatial=v_sgu_w_spatial, sgu_b_spatial=v_sgu_b_spatial, w_out_odd=v_w_out_odd, final_norm=v_final_norm)

    grad_x, landed = local_grads(x[0], loss_target[0], w)

    small = SMALL + ("loss",)
    part = {n: sum_slabs(landed[n], "sum_" + n) for n in MATRICES}
    part.update(sum_small({n: landed[n] for n in small}))
    names = MATRICES + small
    other = dict(zip(names, sibling_exchange([part[n] for n in names])))

    out_g, out_d, out_m, out_v = adam_small(SMALL, w, m, v, part, other)
    for n in MATRICES:
        res = adam_update(w[n][0], m[n][0], v[n][0], part[n], other[n], "adam_" + n)
        out_g[n], out_d[n], out_m[n], out_v[n] = (r[None] for r in res)
    total_loss = (part["loss"] + other["loss"])[0, 0]

    return (total_loss, grad_x[None], *[out_g[n] for n in WEIGHTS], *[out_d[n] for n in WEIGHTS],
            *[out_m[n] for n in WEIGHTS], *[out_v[n] for n in WEIGHTS])
```

```python
import functools
import math

import numpy as np
import jax
import jax.numpy as jnp
from jax import lax
from jax.experimental import pallas as pl
from jax.experimental.pallas import tpu as pltpu

F32 = jnp.float32
MXU_DTYPE = jnp.bfloat16
NORM_EPS = 1e-6
D_MODEL = 1024
S5_WIDTH = 1024
S5_GROUP = 16
S5_GROUPS = 64
S5_STATE = 64
S5_LANES = S5_GROUPS * S5_STATE
S5_KBLK = 8
RET_HEADS = 4
RET_DK = 256
RET_CHUNK = 128
ROPE_BASE = 10000.0
SGU_WIDTH = 2048
SGU_GROUPS = 4
SGU_GDIM = 512
SGU_CHUNK = 128
EVEN_IN = 6144
ODD_IN = 6144
ADAM_LR = 0.001
ADAM_B1 = 0.9
ADAM_B2 = 0.999
ADAM_EPS = 1e-08
ADAM_WD = 0.01
ADAM_STEP = 10
N_CHIPS = 4
VMEM_LIMIT = 56 * 1024 * 1024

TL_PROJ = 512
TL_DW = 1024
TL_S5 = 256
TL_SGU = 128


def _cparams(sem, **kw):
    return pltpu.CompilerParams(dimension_semantics=sem, vmem_limit_bytes=VMEM_LIMIT, **kw)


def _mm(a, b):
    return jnp.dot(a.astype(MXU_DTYPE), b.astype(MXU_DTYPE), preferred_element_type=F32)


def _mm_nt(a, b):
    return lax.dot_general(a.astype(MXU_DTYPE), b.astype(MXU_DTYPE),
                           (((1,), (1,)), ((), ())), preferred_element_type=F32)


def _mm_tn(a, b):
    return lax.dot_general(a.astype(MXU_DTYPE), b.astype(MXU_DTYPE),
                           (((0,), (0,)), ((), ())), preferred_element_type=F32)


_GELU_C = math.sqrt(2.0 / math.pi)


def _gelu_parts(x):
    x2 = x * x
    th = jnp.tanh(x * (_GELU_C + (_GELU_C * 0.044715) * x2))
    hx = 0.5 * x
    return hx + hx * th, th, x2, hx


def _gelu(x):
    return _gelu_parts(x)[0]


def _gelu_and_grad(x):
    g, th, x2, hx = _gelu_parts(x)
    return g, (0.5 + 0.5 * th) + hx * (1.0 - th * th) * (_GELU_C + (3.0 * _GELU_C * 0.044715) * x2)


def _gelu_grad(x):
    return _gelu_and_grad(x)[1]


def _sigmoid(x):
    return 1.0 / (1.0 + jnp.exp(-x))


def _silu_and_grad(x):
    s = _sigmoid(x)
    return x * s, s * (1.0 + x * (1.0 - s))


def _rms(x):
    return lax.rsqrt(jnp.mean(x * x, axis=-1, keepdims=True) + NORM_EPS)


def _full(shape):
    nd = len(shape)
    return pl.BlockSpec(shape, lambda *_: (0,) * nd)


MESH = pl.DeviceIdType.MESH
ANY = pl.BlockSpec(memory_space=pl.ANY)


def _place():
    return lax.axis_index("x"), lax.axis_index("y"), lax.axis_index("c")


def _chip_peer(x, y, c, d):
    return (1 - x if d >= 2 else x, 1 - y if d % 2 else y, c)


class _Plan:
    def __init__(self, inputs, out_shape, build):
        self.inputs, self.out_shape, self._build = list(inputs), list(out_shape), build
        n = len(self.inputs)
        self.sems = [pltpu.SemaphoreType.DMA((n, 3)), pltpu.SemaphoreType.DMA((n, 3)), pltpu.SemaphoreType.DMA((n,))]

    def start(self, in_refs, out_refs, sems):
        send, recv, local = self._build(in_refs, out_refs, sems)
        for p in range(len(self.inputs)):
            local[p].start()
            for cp in send[p]:
                cp.start()

    def wait(self, in_refs, out_refs, sems):
        send, recv, local = self._build(in_refs, out_refs, sems)
        for p in range(len(self.inputs)):
            for cp in recv[p]:
                cp.wait_recv()
        for p in range(len(self.inputs)):
            for cp in send[p]:
                cp.wait_send()
            local[p].wait()


class _GatherPlan:
    def __init__(self, shards):
        self.inputs = list(shards)
        self.out_shape = [jax.ShapeDtypeStruct((N_CHIPS,) + s.shape, s.dtype) for s in shards]
        n = len(shards)
        self.halved = [s.shape[0] % 32 == 0 for s in shards]
        self.sems = [pltpu.SemaphoreType.DMA((n, 3)) for _ in range(4)] + [pltpu.SemaphoreType.DMA((n,))]

    def _copies(self, in_refs, out_refs, sems):
        ici_s, ici_r, d2d_s, d2d_r, loc = sems
        x, y, c = _place()
        me = 2 * x + y

        def rows(p, core):
            if not self.halved[p]:
                return slice(None)
            half = self.inputs[p].shape[0] // 2
            return pl.ds(pl.multiple_of(core * half, 16), half)

        def ici(p, d, slab, core, src=None):
            dst = out_refs[p].at[slab, rows(p, core)]
            return pltpu.make_async_remote_copy(
                src_ref=in_refs[p].at[rows(p, core)] if src is None else src, dst_ref=dst,
                send_sem=ici_s.at[p, d - 1], recv_sem=ici_r.at[p, d - 1],
                device_id=_chip_peer(x, y, c, d), device_id_type=MESH)

        def d2d(p, d, core):
            part = out_refs[p].at[me ^ d, rows(p, core)]
            return pltpu.make_async_remote_copy(
                src_ref=part, dst_ref=part, send_sem=d2d_s.at[p, d - 1], recv_sem=d2d_r.at[p, d - 1],
                device_id=(x, y, 1 - c), device_id_type=MESH)

        local = [pltpu.make_async_copy(in_refs[p], out_refs[p].at[me], loc.at[p]) for p in range(len(self.inputs))]
        return me, c, ici, d2d, local

    def start(self, in_refs, out_refs, sems):
        me, c, ici, d2d, local = self._copies(in_refs, out_refs, sems)
        for p in range(len(self.inputs)):
            local[p].start()
            for d in (1, 2, 3):
                ici(p, d, me, c).start()

    def wait(self, in_refs, out_refs, sems):
        me, c, ici, d2d, local = self._copies(in_refs, out_refs, sems)
        n = len(self.inputs)
        for p in range(n):
            for d in (1, 2, 3):
                ici(p, d, me ^ d, c).wait_recv()
                if self.halved[p]:
                    d2d(p, d, c).start()
        for p in range(n):
            for d in (1, 2, 3):
                if self.halved[p]:
                    d2d(p, d, 1 - c).wait_recv()
                    d2d(p, d, c).wait_send()
                ici(p, d, me, c).wait_send()
            local[p].wait()


def gather_plan(shards):
    return _GatherPlan(shards)


def reduce_plan(shards, whole=()):
    n_s = len(shards)

    def build(in_refs, out_refs, sems):
        send_sems, recv_sems, loc_sems = sems
        x, y, c = _place()
        me = 2 * x + y

        def src(p, slab):
            return in_refs[p].at[slab] if p < n_s else in_refs[p]

        def remote(p, d):
            return pltpu.make_async_remote_copy(
                src_ref=src(p, me ^ d), dst_ref=out_refs[p].at[d], send_sem=send_sems.at[p, d - 1],
                recv_sem=recv_sems.at[p, d - 1], device_id=_chip_peer(x, y, c, d), device_id_type=MESH)

        n = len(in_refs)
        send = [[remote(p, d) for d in (1, 2, 3)] for p in range(n)]
        local = [pltpu.make_async_copy(src(p, me), out_refs[p].at[0], loc_sems.at[p]) for p in range(n)]
        return send, send, local

    outs = [jax.ShapeDtypeStruct(s.shape, s.dtype) for s in shards]
    outs += [jax.ShapeDtypeStruct((N_CHIPS,) + a.shape, a.dtype) for a in whole]
    return _Plan(list(shards) + list(whole), outs, build)


def run_plan(plan, name):
    n = len(plan.inputs)

    def body(*refs):
        plan.start(refs[:n], refs[n:2 * n], refs[2 * n:])
        plan.wait(refs[:n], refs[n:2 * n], refs[2 * n:])

    return pl.pallas_call(body, name=name, in_specs=[ANY] * n, out_specs=[ANY] * n, out_shape=plan.out_shape,
                          scratch_shapes=plan.sems)(*plan.inputs)


def _call(body, plan, *, name, grid, in_specs, out_specs, out_shape, sem, scratch_shapes=()):
    single = not isinstance(out_shape, (list, tuple))
    out_specs = [out_specs] if single else list(out_specs)
    out_shape = [out_shape] if single else list(out_shape)
    n_in, n_out, n_scr = len(in_specs), len(out_specs), len(scratch_shapes)
    ci = 0 if plan is None else len(plan.inputs)

    def hosted(*refs):
        ins, cins = refs[:n_in], refs[n_in:n_in + ci]
        k = n_in + ci
        outs, couts = refs[k:k + n_out], refs[k + n_out:k + n_out + ci]
        k += n_out + ci
        scr, sems = refs[k:k + n_scr], refs[k + n_scr:]
        ids = [pl.program_id(a) for a in range(len(grid))]
        first = functools.reduce(jnp.logical_and, [i == 0 for i in ids])
        last = functools.reduce(jnp.logical_and, [i == g - 1 for i, g in zip(ids, grid)])

        @pl.when(first)
        def _():
            plan.start(cins, couts, sems)

        body(*ins, *outs, *scr)

        @pl.when(last)
        def _():
            plan.wait(cins, couts, sems)

    def run(*args):
        if plan is None:
            res = pl.pallas_call(body, name=name, grid=grid, in_specs=list(in_specs), out_specs=out_specs,
                                 out_shape=out_shape, scratch_shapes=list(scratch_shapes),
                                 compiler_params=_cparams(sem))(*args)
            return (res[0] if single else res), []
        res = pl.pallas_call(hosted, name=name, grid=grid, in_specs=list(in_specs) + [ANY] * ci,
                             out_specs=out_specs + [ANY] * ci, out_shape=out_shape + plan.out_shape,
                             scratch_shapes=list(scratch_shapes) + plan.sems,
                             compiler_params=_cparams(sem))(*args, *plan.inputs)
        return (res[0] if single else res[:n_out]), list(res[n_out:])

    return run


def norm_matmul(x, g, w, name, plan=None, tn=None):
    L, D = x.shape
    tl = min(TL_DW, L)
    if w.ndim == 3:
        nt, _, tn = w.shape
        w_spec = pl.BlockSpec((1, D, tn), lambda i, n: (n, 0, 0))
    else:
        nt = w.shape[1] // tn
        w_spec = pl.BlockSpec((D, tn), lambda i, n: (0, n))

    def body(x_ref, g_ref, w_ref, o_ref, h_ref):
        xv = x_ref[...]
        h = (xv * _rms(xv) * g_ref[...]).astype(h_ref.dtype)
        h_ref[...] = h
        o_ref[...] = _mm(h, w_ref[0] if w.ndim == 3 else w_ref[...])

    return _call(
        body, plan, name=name, grid=(L // tl, nt),
        in_specs=[pl.BlockSpec((tl, D), lambda i, n: (i, 0)), _full((1, D)), w_spec],
        out_specs=[pl.BlockSpec((tl, tn), lambda i, n: (i, n)), pl.BlockSpec((tl, D), lambda i, n: (i, 0))],
        out_shape=[jax.ShapeDtypeStruct((L, nt * tn), F32), jax.ShapeDtypeStruct((L, D), MXU_DTYPE)],
        sem=("arbitrary", "arbitrary"),
    )(x, g, w)


def matmul_residual(ys, w, x, name):
    L, D = x.shape
    tl = min(TL_PROJ, L)
    n = len(ys)
    offs = np.cumsum([0] + [y.shape[1] for y in ys])

    def body(*refs):
        y_refs, w_ref, x_ref, o_ref = refs[:n], refs[n], refs[n + 1], refs[n + 2]
        acc = x_ref[...]
        for k in range(n):
            acc = acc + _mm(y_refs[k][...], w_ref[offs[k]:offs[k + 1], :])
        o_ref[...] = acc

    return pl.pallas_call(
        body, name=name, grid=(L // tl,),
        in_specs=[pl.BlockSpec((tl, y.shape[1]), lambda i: (i, 0)) for y in ys]
        + [_full(w.shape), pl.BlockSpec((tl, D), lambda i: (i, 0))],
        out_specs=pl.BlockSpec((tl, D), lambda i: (i, 0)),
        out_shape=jax.ShapeDtypeStruct((L, D), F32),
        compiler_params=_cparams(("arbitrary",)),
    )(*ys, w, x)


def out_proj_loss(y, w, x, gf, tgt, name):
    L, K = y.shape
    D = w.shape[1]
    tl = min(TL_PROJ, L)

    def body(y_ref, w_ref, x_ref, gf_ref, t_ref, dx_ref, loss_ref, dg_ref):
        @pl.when(pl.program_id(0) == 0)
        def _():
            loss_ref[...] = jnp.zeros_like(loss_ref)
            dg_ref[...] = jnp.zeros_like(dg_ref)

        x2 = x_ref[...] + _mm(y_ref[...], w_ref[...])
        r = _rms(x2)
        xn = x2 * r
        e = xn * gf_ref[...] - t_ref[...]
        loss_ref[...] += (0.5 / D) * jnp.sum(e * e)
        dout = e * (1.0 / D)
        dg_ref[...] += jnp.sum(dout * xn, axis=0, keepdims=True)
        dxn = dout * gf_ref[...]
        dx_ref[...] = r * (dxn - xn * jnp.mean(dxn * xn, axis=-1, keepdims=True))

    return pl.pallas_call(
        body, name=name, grid=(L // tl,),
        in_specs=[pl.BlockSpec((tl, K), lambda i: (i, 0)), _full((K, D)),
                  pl.BlockSpec((tl, D), lambda i: (i, 0)), _full((1, D)),
                  pl.BlockSpec((tl, D), lambda i: (i, 0))],
        out_specs=[pl.BlockSpec((tl, D), lambda i: (i, 0)), _full((8, 128)), _full((1, D))],
        out_shape=[jax.ShapeDtypeStruct((L, D), F32), jax.ShapeDtypeStruct((8, 128), F32),
                   jax.ShapeDtypeStruct((1, D), F32)],
        compiler_params=_cparams(("arbitrary",)),
    )(y, w, x, gf, tgt)


def out_proj_bwd(dx, w, ys, name):
    L, D = dx.shape
    K = w.shape[0]
    tl = min(TL_PROJ, L)
    n = len(ys)
    offs = np.cumsum([0] + [y.shape[1] for y in ys])

    def body(*refs):
        dx_ref, w_ref, y_refs = refs[0], refs[1], refs[2:2 + n]
        dy_refs, dw_ref = refs[2 + n:2 + 2 * n], refs[2 + 2 * n]

        @pl.when(pl.program_id(0) == 0)
        def _():
            dw_ref[...] = jnp.zeros_like(dw_ref)

        dxv = dx_ref[...]
        for k in range(n):
            dy_refs[k][...] = _mm_nt(dxv, w_ref[offs[k]:offs[k + 1], :])
            dw_ref[offs[k]:offs[k + 1], :] += _mm_tn(y_refs[k][...], dxv)

    y_specs = [pl.BlockSpec((tl, y.shape[1]), lambda i: (i, 0)) for y in ys]
    return pl.pallas_call(
        body, name=name, grid=(L // tl,),
        in_specs=[pl.BlockSpec((tl, D), lambda i: (i, 0)), _full((K, D))] + y_specs,
        out_specs=y_specs + [_full((K, D))],
        out_shape=[jax.ShapeDtypeStruct(y.shape, F32) for y in ys] + [jax.ShapeDtypeStruct((K, D), F32)],
        compiler_params=_cparams(("arbitrary",)),
    )(dx, w, *ys)


def in_proj_bwd_dx(x, g, dps, ws, dres, name, plan=None):
    L, D = x.shape
    tl = min(TL_PROJ, L)
    n = len(dps)

    def body(*refs):
        x_ref, g_ref, dres_ref = refs[:3]
        dp_refs, w_refs = refs[3:3 + n], refs[3 + n:3 + 2 * n]
        dx_ref, dg_ref = refs[3 + 2 * n:]

        @pl.when(pl.program_id(0) == 0)
        def _():
            dg_ref[...] = jnp.zeros_like(dg_ref)

        dh = None
        for dp_ref, w_ref, w in zip(dp_refs, w_refs, ws):
            if w.ndim == 3:
                tn = w.shape[2]
                parts = [_mm_nt(dp_ref[:, tn * k:tn * (k + 1)], w_ref[k]) for k in range(w.shape[0])]
            else:
                parts = [_mm_nt(dp_ref[...], w_ref[...])]
            for part in parts:
                dh = part if dh is None else dh + part
        xv = x_ref[...]
        r = _rms(xv)
        xn = xv * r
        dg_ref[...] += jnp.sum(dh * xn, axis=0, keepdims=True)
        dxn = dh * g_ref[...]
        dx_ref[...] = dres_ref[...] + r * (dxn - xn * jnp.mean(dxn * xn, axis=-1, keepdims=True))

    return _call(
        body, plan, name=name, grid=(L // tl,),
        in_specs=[pl.BlockSpec((tl, D), lambda i: (i, 0)), _full((1, D)), pl.BlockSpec((tl, D), lambda i: (i, 0))]
        + [pl.BlockSpec((tl, dp.shape[1]), lambda i: (i, 0)) for dp in dps] + [_full(w.shape) for w in ws],
        out_specs=[pl.BlockSpec((tl, D), lambda i: (i, 0)), _full((1, D))],
        out_shape=[jax.ShapeDtypeStruct((L, D), F32), jax.ShapeDtypeStruct((1, D), F32)],
        sem=("arbitrary",),
    )(x, g, dres, *dps, *ws)


def in_proj_bwd_dw(h, dp, name, tn, first=0, into=None, dtype=F32):
    L, D = h.shape
    tl = min(TL_DW, L)
    wb = EVEN_IN // N_CHIPS
    per = wb // tn
    count = dp.shape[1] // tn
    last = L // tl - 1

    def body(*refs):
        h_ref, dp_ref, dw_ref, acc = refs[0], refs[1], refs[-2], refs[-1]

        @pl.when(pl.program_id(1) == 0)
        def _():
            acc[...] = jnp.zeros_like(acc)

        acc[...] += _mm_tn(h_ref[...], dp_ref[...])

        @pl.when(pl.program_id(1) == last)
        def _():
            dw_ref[0] = acc[...].astype(dw_ref.dtype)

    ins = [h, dp] + ([] if into is None else [into])
    return pl.pallas_call(
        body, name=name, grid=(count, L // tl),
        in_specs=[pl.BlockSpec((tl, D), lambda n, i: (i, 0)), pl.BlockSpec((tl, tn), lambda n, i: (i, n))]
        + ([] if into is None else [ANY]),
        out_specs=pl.BlockSpec((1, D, tn), lambda n, i: ((n + first) // per, 0, (n + first) % per)),
        out_shape=jax.ShapeDtypeStruct((N_CHIPS, D, wb), dtype),
        scratch_shapes=[pltpu.VMEM((D, tn), F32)],
        input_output_aliases={} if into is None else {2: 0},
        compiler_params=_cparams(("arbitrary", "arbitrary")),
    )(*ins)


def _s5_param_fn(lam_re, lam_im, log_dt, b_re, b_im):
    lr = jnp.minimum(lam_re, -1e-4)
    li = lam_im
    dt = jnp.exp(log_dt)
    mag = jnp.exp(lr * dt)
    ab_re = mag * jnp.cos(li * dt)
    ab_im = mag * jnp.sin(li * dt)
    den = lr * lr + li * li
    n_re = ab_re - 1.0
    n_im = ab_im
    z_re = (n_re * lr + n_im * li) / den
    z_im = (n_im * lr - n_re * li) / den
    bb_re = z_re[None] * b_re - z_im[None] * b_im
    bb_im = z_re[None] * b_im + z_im[None] * b_re
    return ab_re, ab_im, bb_re, bb_im


def s5_params_fwd(lam_re, lam_im, log_dt, b_re, b_im, span):
    G, P = lam_re.shape
    H = b_re.shape[0]
    assert span & (span - 1) == 0

    def body(lr_ref, li_ref, dt_ref, br_ref, bi_ref, abr_ref, abi_ref, bbr_ref, bbi_ref, pr_ref, pi_ref):
        ab_re, ab_im, bb_re, bb_im = _s5_param_fn(lr_ref[...], li_ref[...], dt_ref[...], br_ref[...], bi_ref[...])
        abr_ref[...] = ab_re
        abi_ref[...] = ab_im
        bbr_ref[...] = bb_re
        bbi_ref[...] = bb_im
        cr, ci = ab_re, ab_im
        for _ in range(span.bit_length() - 1):
            cr, ci = cr * cr - ci * ci, 2.0 * cr * ci
        pr_ref[...] = cr
        pi_ref[...] = ci

    shp = lambda *s: jax.ShapeDtypeStruct(s, F32)
    return pl.pallas_call(
        body, name="s5_params_fwd",
        out_shape=[shp(G, P), shp(G, P), shp(H, G, P), shp(H, G, P), shp(G, P), shp(G, P)],
    )(lam_re, lam_im, log_dt, b_re, b_im)


def s5_params_bwd(lam_re, lam_im, log_dt, b_re, b_im, d_ab_re, d_ab_im, d_bb_re, d_bb_im):
    G, P = lam_re.shape
    H = b_re.shape[0]

    def body(lr_ref, li_ref, dt_ref, br_ref, bi_ref, g0, g1, g2, g3, o0, o1, o2, o3, o4):
        prim = (lr_ref[...], li_ref[...], dt_ref[...], br_ref[...], bi_ref[...])
        _, vjp = jax.vjp(_s5_param_fn, *prim)
        d = vjp((jnp.sum(g0[...], axis=0), jnp.sum(g1[...], axis=0), g2[...], g3[...]))
        o0[...], o1[...], o2[...], o3[...], o4[...] = d

    shp = lambda *s: jax.ShapeDtypeStruct(s, F32)
    return pl.pallas_call(
        body, name="s5_params_bwd",
        out_shape=[shp(G, P), shp(G, P), shp(G, 1), shp(H, G, P), shp(H, G, P)],
    )(lam_re, lam_im, log_dt, b_re, b_im, d_ab_re, d_ab_im, d_bb_re, d_bb_im)


def stream_order(a, tl):
    L, C = a.shape
    return a.reshape(L // tl, 8, tl // 8, C).transpose(0, 2, 1, 3).reshape(L, C)


def token_order(a, tl):
    L, C = a.shape
    return a.reshape(L // tl, tl // 8, 8, C).transpose(0, 2, 1, 3).reshape(L, C)


_LANE_BLK = 1024


def _cmul_add(ar, ai, xr, xi, br, bi):
    return br + (ar * xr - ai * xi), bi + (ar * xi + ai * xr)


def _cmulc_add(ar, ai, xr, xi, br, bi):
    return br + (ar * xr + ai * xi), bi + (ar * xi - ai * xr)


def _s5_states(u, wbd_ref, a_re, a_im, at_re, at_im, s_re, s_im, e_re, e_im, c0_re, c0_im, tl):
    t8 = tl // 8
    for k in range(S5_KBLK):
        bu = _mm(u[:, 128 * k:128 * (k + 1)], wbd_ref[k])
        s_re[:, 512 * k:512 * (k + 1)] = bu[:, :512]
        s_im[:, 512 * k:512 * (k + 1)] = bu[:, 512:]
    outs_re, outs_im = [], []
    for b in range(S5_LANES // _LANE_BLK):
        lanes = slice(_LANE_BLK * b, _LANE_BLK * (b + 1))
        ar = jnp.broadcast_to(a_re[:, lanes], (8, _LANE_BLK))
        ai = jnp.broadcast_to(a_im[:, lanes], (8, _LANE_BLK))

        def local(i, carry, lanes=lanes, ar=ar, ai=ai):
            r = pl.multiple_of(i * 8, 8)
            sr, si = _cmul_add(ar, ai, carry[0], carry[1], s_re[pl.ds(r, 8), lanes], s_im[pl.ds(r, 8), lanes])
            s_re[pl.ds(r, 8), lanes] = sr
            s_im[pl.ds(r, 8), lanes] = si
            return sr, si

        zero = jnp.zeros((8, _LANE_BLK), F32)
        fr, fi = lax.fori_loop(0, t8, local, (zero, zero), unroll=True)
        tr, ti = at_re[:, lanes], at_im[:, lanes]
        er, ei = c0_re[:, lanes], c0_im[:, lanes]
        ers, eis = [er], [ei]
        for j in range(8):
            er, ei = _cmul_add(tr, ti, er, ei, fr[j:j + 1], fi[j:j + 1])
            ers.append(er)
            eis.append(ei)
        outs_re.append(ers[8])
        outs_im.append(eis[8])
        ent_r, ent_i = jnp.concatenate(ers[:8], axis=0), jnp.concatenate(eis[:8], axis=0)
        e_re[:, lanes] = ent_r
        e_im[:, lanes] = ent_i

        def fix(i, carry, lanes=lanes, ar=ar, ai=ai):
            r = pl.multiple_of(i * 8, 8)
            zr, zi = ar * carry[0] - ai * carry[1], ar * carry[1] + ai * carry[0]
            s_re[pl.ds(r, 8), lanes] = s_re[pl.ds(r, 8), lanes] + zr
            s_im[pl.ds(r, 8), lanes] = s_im[pl.ds(r, 8), lanes] + zi
            return zr, zi

        lax.fori_loop(0, t8, fix, (ent_r, ent_i), unroll=True)
    return jnp.concatenate(outs_re, axis=1), jnp.concatenate(outs_im, axis=1)


def _s5_readout(s_re, s_im, cre_ref, cim_ref):
    ys = []
    for k in range(S5_KBLK):
        lanes = slice(512 * k, 512 * (k + 1))
        ys.append(_mm(s_re[:, lanes], cre_ref[k]) - _mm(s_im[:, lanes], cim_ref[k]))
    return jnp.concatenate(ys, axis=1)


def s5_forward(p, wbd, cre, cim, atab, d_skip, w_glu, b_glu, plan=None):
    L = p.shape[0]
    tl = min(TL_S5, L)
    nch = L // tl

    def body(u_ref, z_ref, wbd_ref, cre_ref, cim_ref, at_ref, d_ref, wg_ref, bg_ref,
             ya_ref, st_re_ref, st_im_ref, s_re, s_im, e_re, e_im, car_re, car_im):
        @pl.when(pl.program_id(0) == 0)
        def _():
            car_re[...] = jnp.zeros_like(car_re)
            car_im[...] = jnp.zeros_like(car_im)

        c0_re, c0_im = car_re[...], car_im[...]
        st_re_ref[0] = c0_re
        st_im_ref[0] = c0_im
        u = u_ref[...]
        x_re, x_im = _s5_states(u, wbd_ref, at_ref[0:1], at_ref[1:2], at_ref[2:3], at_ref[3:4],
                                s_re, s_im, e_re, e_im, c0_re, c0_im, tl)
        car_re[...] = x_re
        car_im[...] = x_im
        y = _s5_readout(s_re, s_im, cre_ref, cim_ref) + d_ref[...] * u
        yg = _gelu(y)
        gate = _sigmoid(_mm(yg, wg_ref[...]) + bg_ref[...])
        sz, _ = _silu_and_grad(z_ref[...])
        ya_ref[...] = (yg * gate * sz).astype(ya_ref.dtype)

    return _call(
        body, plan, name="s5_forward", grid=(nch,),
        in_specs=[pl.BlockSpec((tl, 1024), lambda i: (i, 0)), pl.BlockSpec((tl, 1024), lambda i: (i, 1)),
                  _full(wbd.shape), _full(cre.shape), _full(cim.shape), _full(atab.shape),
                  _full((1, 1024)), _full((1024, 1024)), _full((1, 1024))],
        out_specs=[pl.BlockSpec((tl, 1024), lambda i: (i, 0)),
                   pl.BlockSpec((1, 1, S5_LANES), lambda i: (i, 0, 0)),
                   pl.BlockSpec((1, 1, S5_LANES), lambda i: (i, 0, 0))],
        out_shape=[jax.ShapeDtypeStruct((L, 1024), MXU_DTYPE),
                   jax.ShapeDtypeStruct((nch, 1, S5_LANES), F32), jax.ShapeDtypeStruct((nch, 1, S5_LANES), F32)],
        scratch_shapes=[pltpu.VMEM((tl, S5_LANES), F32), pltpu.VMEM((tl, S5_LANES), F32),
                        pltpu.VMEM((8, S5_LANES), F32), pltpu.VMEM((8, S5_LANES), F32),
                        pltpu.VMEM((1, S5_LANES), F32), pltpu.VMEM((1, S5_LANES), F32)],
        sem=("arbitrary",),
    )(p, p, wbd, cre, cim, atab, d_skip, w_glu, b_glu)


def s5_backward(p, dya, st_re, st_im, wbd, cre, cim, atab, d_skip, w_glu, b_glu, plan=None):
    L = p.shape[0]
    tl = min(TL_S5, L)
    t8 = tl // 8
    nch = L // tl
    rev = lambda i: (nch - 1 - i, 0)
    rev1 = lambda i: (nch - 1 - i, 1)
    rev3 = lambda i: (nch - 1 - i, 0, 0)

    def body(u_ref, z_ref, dya_ref, str_ref, sti_ref, wbd_ref, cre_ref, cim_ref, at_ref,
             d_ref, wg_ref, bg_ref,
             dp_ref, dwbd_ref, dcre_ref, dcim_ref, dabr_ref, dabi_ref, dd_ref, dwg_ref, dbg_ref,
             s_re, s_im, g_re, g_im, e_re, e_im, car_re, car_im):
        @pl.when(pl.program_id(0) == 0)
        def _():
            car_re[...] = jnp.zeros_like(car_re)
            car_im[...] = jnp.zeros_like(car_im)
            for r in (dwbd_ref, dcre_ref, dcim_ref, dabr_ref, dabi_ref, dd_ref, dwg_ref, dbg_ref):
                r[...] = jnp.zeros_like(r)

        u = u_ref[...]
        a_re, a_im, at_re, at_im = at_ref[0:1], at_ref[1:2], at_ref[2:3], at_ref[3:4]
        _s5_states(u, wbd_ref, a_re, a_im, at_re, at_im, s_re, s_im, e_re, e_im, str_ref[0], sti_ref[0], tl)
        y = _s5_readout(s_re, s_im, cre_ref, cim_ref) + d_ref[...] * u
        yg, dyg = _gelu_and_grad(y)
        gate = _sigmoid(_mm(yg, wg_ref[...]) + bg_ref[...])
        sz, dsz = _silu_and_grad(z_ref[...])
        dya = dya_ref[...]
        s5out = yg * gate
        dp_ref[:, 1024:] = (dya * s5out * dsz).astype(dp_ref.dtype)
        ds5 = dya * sz
        dt = ds5 * yg * gate * (1.0 - gate)
        dwg_ref[...] += _mm_tn(yg, dt)
        dbg_ref[...] += jnp.sum(dt, axis=0, keepdims=True)
        dyv = (ds5 * gate + _mm_nt(dt, wg_ref[...])) * dyg
        dd_ref[...] += jnp.sum(dyv * u, axis=0, keepdims=True)

        for k in range(S5_KBLK):
            lanes = slice(512 * k, 512 * (k + 1))
            dyk = dyv[:, 128 * k:128 * (k + 1)]
            g_re[:, lanes] = _mm_nt(dyk, cre_ref[k])
            g_im[:, lanes] = -_mm_nt(dyk, cim_ref[k])
            dcre_ref[k] += _mm_tn(s_re[:, lanes], dyk)
            dcim_ref[k] -= _mm_tn(s_im[:, lanes], dyk)

        for b in range(S5_LANES // _LANE_BLK):
            lanes = slice(_LANE_BLK * b, _LANE_BLK * (b + 1))
            ar = jnp.broadcast_to(a_re[:, lanes], (8, _LANE_BLK))
            ai = jnp.broadcast_to(a_im[:, lanes], (8, _LANE_BLK))

            def local(j, carry, lanes=lanes, ar=ar, ai=ai):
                r = pl.multiple_of((t8 - 1 - j) * 8, 8)
                gr, gi = _cmulc_add(ar, ai, carry[0], carry[1], g_re[pl.ds(r, 8), lanes], g_im[pl.ds(r, 8), lanes])
                g_re[pl.ds(r, 8), lanes] = gr
                g_im[pl.ds(r, 8), lanes] = gi
                return gr, gi

            zero = jnp.zeros((8, _LANE_BLK), F32)
            fr, fi = lax.fori_loop(0, t8, local, (zero, zero), unroll=True)
            tr, ti = at_re[:, lanes], at_im[:, lanes]
            hr, hi = car_re[:, lanes], car_im[:, lanes]
            hrs, his = [hr], [hi]
            for j in range(7, -1, -1):
                hr, hi = _cmulc_add(tr, ti, hr, hi, fr[j:j + 1], fi[j:j + 1])
                hrs.append(hr)
                his.append(hi)
            car_re[:, lanes] = hrs[8]
            car_im[:, lanes] = his[8]
            in_r = jnp.concatenate(hrs[7::-1], axis=0)
            in_i = jnp.concatenate(his[7::-1], axis=0)

            def fix(j, carry, lanes=lanes, ar=ar, ai=ai):
                wr, wi, accr, acci = carry
                r = pl.multiple_of((t8 - 1 - j) * 8, 8)
                wr, wi = ar * wr + ai * wi, ar * wi - ai * wr
                gr, gi = g_re[pl.ds(r, 8), lanes] + wr, g_im[pl.ds(r, 8), lanes] + wi
                g_re[pl.ds(r, 8), lanes] = gr
                g_im[pl.ds(r, 8), lanes] = gi
                sr, si = s_re[pl.ds(r - 8, 8), lanes], s_im[pl.ds(r - 8, 8), lanes]
                return wr, wi, accr + (sr * gr + si * gi), acci + (sr * gi - si * gr)

            wr, wi, accr, acci = lax.fori_loop(0, t8 - 1, fix, (in_r, in_i, zero, zero), unroll=True)
            wr, wi = ar * wr + ai * wi, ar * wi - ai * wr
            gr, gi = g_re[pl.ds(0, 8), lanes] + wr, g_im[pl.ds(0, 8), lanes] + wi
            g_re[pl.ds(0, 8), lanes] = gr
            g_im[pl.ds(0, 8), lanes] = gi
            sr, si = e_re[:, lanes], e_im[:, lanes]
            dabr_ref[:, lanes] += accr + (sr * gr + si * gi)
            dabi_ref[:, lanes] += acci + (sr * gi - si * gr)

        dus = []
        for k in range(S5_KBLK):
            lanes = slice(512 * k, 512 * (k + 1))
            g = jnp.concatenate([g_re[:, lanes], g_im[:, lanes]], axis=1)
            dwbd_ref[k] += _mm_tn(u[:, 128 * k:128 * (k + 1)], g)
            dus.append(_mm_nt(g, wbd_ref[k]))
        du = jnp.concatenate(dus, axis=1) + dyv * d_ref[...]
        dp_ref[:, :1024] = du.astype(dp_ref.dtype)

    shp = lambda *s: jax.ShapeDtypeStruct(s, F32)
    return _call(
        body, plan, name="s5_backward", grid=(nch,),
        in_specs=[pl.BlockSpec((tl, 1024), rev), pl.BlockSpec((tl, 1024), rev1), pl.BlockSpec((tl, 1024), rev),
                  pl.BlockSpec((1, 1, S5_LANES), rev3), pl.BlockSpec((1, 1, S5_LANES), rev3),
                  _full(wbd.shape), _full(cre.shape), _full(cim.shape), _full(atab.shape),
                  _full((1, 1024)), _full((1024, 1024)), _full((1, 1024))],
        out_specs=[pl.BlockSpec((tl, 2048), rev), _full(wbd.shape), _full(cre.shape), _full(cim.shape),
                   _full((8, S5_LANES)), _full((8, S5_LANES)), _full((1, 1024)), _full((1024, 1024)), _full((1, 1024))],
        out_shape=[jax.ShapeDtypeStruct((L, 2048), MXU_DTYPE), shp(*wbd.shape), shp(*cre.shape), shp(*cim.shape),
                   shp(8, S5_LANES), shp(8, S5_LANES), shp(1, 1024), shp(1024, 1024), shp(1, 1024)],
        scratch_shapes=[pltpu.VMEM((tl, S5_LANES), F32), pltpu.VMEM((tl, S5_LANES), F32),
                        pltpu.VMEM((tl, S5_LANES), F32), pltpu.VMEM((tl, S5_LANES), F32),
                        pltpu.VMEM((8, S5_LANES), F32), pltpu.VMEM((8, S5_LANES), F32),
                        pltpu.VMEM((1, S5_LANES), F32), pltpu.VMEM((1, S5_LANES), F32)],
        sem=("arbitrary",),
    )(p, p, dya, st_re, st_im, wbd, cre, cim, atab, d_skip, w_glu, b_glu)


def _block_diag(w, rows_first):
    g8 = w.reshape(S5_KBLK, 8, w.shape[1], w.shape[2])
    eye = jnp.eye(8, dtype=w.dtype)
    out = jnp.einsum('kgab,fg->kfagb', g8, eye)
    return out.reshape(S5_KBLK, 8 * w.shape[1], 8 * w.shape[2])


def _block_diag_extract(wbd, a, b):
    w5 = wbd.reshape(S5_KBLK, 8, a, 8, b)
    idx = jnp.arange(8)
    return w5[:, idx, :, idx, :].transpose(1, 0, 2, 3).reshape(S5_GROUPS, a, b)


def _ret_constants():
    log_g = np.log1p(-np.exp2(-5.0 - np.arange(RET_HEADS, dtype=np.float32))).astype(np.float32)
    idx = np.arange(RET_CHUNK, dtype=np.float32)
    diff = idx[:, None] - idx[None, :]
    decay = np.where(diff >= 0, np.exp(log_g[:, None, None] * np.maximum(diff, 0.0)), 0.0).astype(np.float32)
    xi = np.exp(log_g[None, :] * (idx[:, None] + 1.0)).astype(np.float32)
    zeta = np.exp(log_g[None, :] * (RET_CHUNK - 1.0 - idx[:, None])).astype(np.float32)
    chunk_decay = np.exp(log_g * RET_CHUNK).astype(np.float32)
    return decay, xi, zeta, chunk_decay


def _rope_tables(L):
    half = RET_DK // 2
    inv = ROPE_BASE ** (-jnp.arange(half, dtype=F32) / half)
    ang = jnp.arange(L, dtype=F32)[:, None] * inv[None, :]
    return jnp.cos(ang), jnp.sin(ang)


def _rot(xh, cos, sin):
    x1, x2 = xh[:, :128], xh[:, 128:]
    return jnp.concatenate([x1 * cos - x2 * sin, x1 * sin + x2 * cos], axis=1)


def _rot_t(dh, cos, sin):
    d1, d2 = dh[:, :128], dh[:, 128:]
    return jnp.concatenate([d1 * cos + d2 * sin, d2 * cos - d1 * sin], axis=1)


def retention_forward(p, cos, sin, gain):
    L = p.shape[0]
    nc = L // RET_CHUNK
    decay_np, xi_np, zeta_np, cd_np = _ret_constants()
    decay, xi, zeta = jnp.asarray(decay_np), jnp.asarray(xi_np), jnp.asarray(zeta_np)
    scale = RET_DK ** -0.5

    def body(q_ref, k_ref, v_ref, z_ref, cos_ref, sin_ref, dec_ref, xi_ref, zeta_ref, gain_ref,
             yb_ref, prev_ref, state):
        @pl.when(pl.program_id(0) == 0)
        def _():
            state[...] = jnp.zeros_like(state)

        H = range(RET_HEADS)
        hs = [slice(RET_DK * h, RET_DK * (h + 1)) for h in H]
        cs, sn = cos_ref[...], sin_ref[...]
        sz, _ = _silu_and_grad(z_ref[...])
        qh = [_rot(q_ref[:, hs[h]], cs, sn) for h in H]
        kh = [_rot(k_ref[:, hs[h]], cs, sn) * scale for h in H]
        vh = [v_ref[:, hs[h]] for h in H]
        prev = [state[h] for h in H]
        sc = [_mm_nt(qh[h], kh[h]) * dec_ref[h] for h in H]
        cross = [_mm(qh[h] * xi_ref[:, h:h + 1], prev[h]) for h in H]
        loc = [_mm_tn(kh[h] * zeta_ref[:, h:h + 1], vh[h]) for h in H]
        o = [_mm(sc[h], vh[h]) + cross[h] for h in H]
        oc = [o[h] - jnp.mean(o[h], axis=-1, keepdims=True) for h in H]
        on = [oc[h] * lax.rsqrt(jnp.mean(oc[h] * oc[h], axis=-1, keepdims=True) + NORM_EPS) for h in H]
        for h in H:
            prev_ref[0, h] = prev[h].astype(prev_ref.dtype)
            state[h] = prev[h] * float(cd_np[h]) + loc[h]
            yb_ref[:, hs[h]] = (on[h] * gain_ref[:, hs[h]] * sz[:, hs[h]]).astype(yb_ref.dtype)

    blk = lambda c: pl.BlockSpec((RET_CHUNK, 1024), lambda i, c=c: (i, c))
    return pl.pallas_call(
        body, name="retention_forward", grid=(nc,),
        in_specs=[blk(0), blk(1), blk(2), blk(3),
                  pl.BlockSpec((RET_CHUNK, 128), lambda i: (i, 0)), pl.BlockSpec((RET_CHUNK, 128), lambda i: (i, 0)),
                  _full(decay.shape), _full(xi.shape), _full(zeta.shape), _full((1, 1024))],
        out_specs=[pl.BlockSpec((RET_CHUNK, 1024), lambda i: (i, 0)),
                   pl.BlockSpec((1, RET_HEADS, RET_DK, RET_DK), lambda i: (i, 0, 0, 0))],
        out_shape=[jax.ShapeDtypeStruct((L, 1024), MXU_DTYPE),
                   jax.ShapeDtypeStruct((nc, RET_HEADS, RET_DK, RET_DK), MXU_DTYPE)],
        scratch_shapes=[pltpu.VMEM((RET_HEADS, RET_DK, RET_DK), F32)],
        compiler_params=_cparams(("arbitrary",)),
    )(p, p, p, p, cos, sin, decay, xi, zeta, gain)


def retention_backward(p, dy, prevs, cos, sin, gain, plan=None):
    L = p.shape[0]
    nc = L // RET_CHUNK
    decay_np, xi_np, zeta_np, cd_np = _ret_constants()
    decay, xi, zeta = jnp.asarray(decay_np), jnp.asarray(xi_np), jnp.asarray(zeta_np)
    scale = RET_DK ** -0.5

    def body(q_ref, k_ref, v_ref, z_ref, dyb_ref, prev_ref, cos_ref, sin_ref, dec_ref, xi_ref, zeta_ref, gain_ref,
             dp_ref, dgain_ref, dstate):
        @pl.when(pl.program_id(0) == 0)
        def _():
            dstate[...] = jnp.zeros_like(dstate)
            dgain_ref[...] = jnp.zeros_like(dgain_ref)

        H = range(RET_HEADS)
        hs = [slice(RET_DK * h, RET_DK * (h + 1)) for h in H]
        cs, sn = cos_ref[...], sin_ref[...]
        sz, dsz = _silu_and_grad(z_ref[...])
        dyb = dyb_ref[...]
        xih = [xi_ref[:, h:h + 1] for h in H]
        zth = [zeta_ref[:, h:h + 1] for h in H]
        qh = [_rot(q_ref[:, hs[h]], cs, sn) for h in H]
        kh = [_rot(k_ref[:, hs[h]], cs, sn) * scale for h in H]
        vh = [v_ref[:, hs[h]] for h in H]
        prev = [prev_ref[0, h] for h in H]
        dst = [dstate[h] for h in H]
        qx = [qh[h] * xih[h] for h in H]
        kz = [kh[h] * zth[h] for h in H]
        sc = [_mm_nt(qh[h], kh[h]) * dec_ref[h] for h in H]
        cross = [_mm(qx[h], prev[h]) for h in H]
        dk_st = [_mm_nt(vh[h], dst[h]) for h in H]
        dv_st = [_mm(kz[h], dst[h]) for h in H]
        o = [_mm(sc[h], vh[h]) + cross[h] for h in H]
        oc = [o[h] - jnp.mean(o[h], axis=-1, keepdims=True) for h in H]
        rstd = [lax.rsqrt(jnp.mean(oc[h] * oc[h], axis=-1, keepdims=True) + NORM_EPS) for h in H]
        on = [oc[h] * rstd[h] for h in H]
        dong = [dyb[:, hs[h]] * sz[:, hs[h]] for h in H]
        don = [dong[h] * gain_ref[:, hs[h]] for h in H]
        do = [rstd[h] * (don[h] - jnp.mean(don[h], axis=-1, keepdims=True)
                         - on[h] * jnp.mean(don[h] * on[h], axis=-1, keepdims=True)) for h in H]
        dsc = [_mm_nt(do[h], vh[h]) * dec_ref[h] for h in H]
        dq_st = [_mm_nt(do[h], prev[h]) for h in H]
        dnew = [_mm_tn(qx[h], do[h]) for h in H]
        dqh = [_mm(dsc[h], kh[h]) + dq_st[h] * xih[h] for h in H]
        dkh = [_mm_tn(dsc[h], qh[h]) + dk_st[h] * zth[h] for h in H]
        dvh = [_mm_tn(sc[h], do[h]) + dv_st[h] for h in H]
        for h in H:
            dstate[h] = dst[h] * float(cd_np[h]) + dnew[h]
            dgain_ref[:, hs[h]] += jnp.sum(dong[h] * on[h], axis=0, keepdims=True)
            dp_ref[:, hs[h]] = _rot_t(dqh[h], cs, sn).astype(dp_ref.dtype)
            dp_ref[:, 1024 + RET_DK * h:1024 + RET_DK * (h + 1)] = (_rot_t(dkh[h], cs, sn) * scale).astype(dp_ref.dtype)
            dp_ref[:, 2048 + RET_DK * h:2048 + RET_DK * (h + 1)] = dvh[h].astype(dp_ref.dtype)
            dp_ref[:, 3072 + RET_DK * h:3072 + RET_DK * (h + 1)] = (
                dyb[:, hs[h]] * on[h] * gain_ref[:, hs[h]] * dsz[:, hs[h]]).astype(dp_ref.dtype)

    blk = lambda c: pl.BlockSpec((RET_CHUNK, 1024), lambda i, c=c: (nc - 1 - i, c))
    tab = pl.BlockSpec((RET_CHUNK, 128), lambda i: (nc - 1 - i, 0))
    return _call(
        body, plan, name="retention_backward", grid=(nc,),
        in_specs=[blk(0), blk(1), blk(2), blk(3), blk(0),
                  pl.BlockSpec((1, RET_HEADS, RET_DK, RET_DK), lambda i: (nc - 1 - i, 0, 0, 0)),
                  tab, tab, _full(decay.shape), _full(xi.shape), _full(zeta.shape), _full((1, 1024))],
        out_specs=[pl.BlockSpec((RET_CHUNK, 4096), lambda i: (nc - 1 - i, 0)), _full((1, 1024))],
        out_shape=[jax.ShapeDtypeStruct((L, 4096), MXU_DTYPE), jax.ShapeDtypeStruct((1, 1024), F32)],
        scratch_shapes=[pltpu.VMEM((RET_HEADS, RET_DK, RET_DK), F32)],
        sem=("arbitrary",),
    )(p, p, p, p, dy, prevs, cos, sin, decay, xi, zeta, gain)


def _sgu_mix(p_ref, gain_ref, wm_ref, bt_ref, tl):
    pu, pv, z = p_ref[:, :2048], p_ref[:, 2048:4096], p_ref[:, 4096:]
    (u, du), (v, dv) = _gelu_and_grad(pu), _gelu_and_grad(pv)
    mu = jnp.mean(v, axis=-1, keepdims=True)
    vc = v - mu
    rstd = lax.rsqrt(jnp.mean(vc * vc, axis=-1, keepdims=True) + NORM_EPS)
    vn = vc * rstd
    vg = vn * gain_ref[...]
    mask = (lax.broadcasted_iota(jnp.int32, (SGU_CHUNK, SGU_CHUNK), 0)
            >= lax.broadcasted_iota(jnp.int32, (SGU_CHUNK, SGU_CHUNK), 1))
    wms = [jnp.where(mask, wm_ref[g], 0.0) for g in range(SGU_GROUPS)]
    rows = []
    for c in range(tl // SGU_CHUNK):
        rs = slice(SGU_CHUNK * c, SGU_CHUNK * (c + 1))
        cols = []
        for g in range(SGU_GROUPS):
            gs = slice(SGU_GDIM * g, SGU_GDIM * (g + 1))
            cols.append(_mm(wms[g], vg[rs, gs]) + bt_ref[:, g:g + 1])
        rows.append(jnp.concatenate(cols, axis=1))
    s = rows[0] if len(rows) == 1 else jnp.concatenate(rows, axis=0)
    return du, dv, z, u, vn, rstd, vg, wms, mask, s


def sgu_forward(p, gain, wm, bt):
    L = p.shape[0]
    tl = min(TL_SGU, L)

    def body(p_ref, gain_ref, wm_ref, bt_ref, y_ref):
        _, _, z, u, _, _, _, _, _, s = _sgu_mix(p_ref, gain_ref, wm_ref, bt_ref, tl)
        sz, _ = _silu_and_grad(z)
        y_ref[...] = (u * s * sz).astype(y_ref.dtype)

    return pl.pallas_call(
        body, name="sgu_forward", grid=(L // tl,),
        in_specs=[pl.BlockSpec((tl, ODD_IN), lambda i: (i, 0)), _full((1, 2048)), _full(wm.shape), _full(bt.shape)],
        out_specs=pl.BlockSpec((tl, 2048), lambda i: (i, 0)),
        out_shape=jax.ShapeDtypeStruct((L, 2048), MXU_DTYPE),
        compiler_params=_cparams(("arbitrary",)),
    )(p, gain, wm, bt)


def sgu_backward(p, dy, gain, wm, bt, plan=None):
    L = p.shape[0]
    tl = min(TL_SGU, L)

    def body(p_ref, dy_ref, gain_ref, wm_ref, bt_ref, dp_ref, dgain_ref, dwm_ref, dbt_ref):
        @pl.when(pl.program_id(0) == 0)
        def _():
            dgain_ref[...] = jnp.zeros_like(dgain_ref)
            dwm_ref[...] = jnp.zeros_like(dwm_ref)
            dbt_ref[...] = jnp.zeros_like(dbt_ref)

        gu, gv, z, u, vn, rstd, vg, wms, mask, s = _sgu_mix(p_ref, gain_ref, wm_ref, bt_ref, tl)
        sz, dsz = _silu_and_grad(z)
        dyv = dy_ref[...]
        dp_ref[:, 4096:] = (dyv * u * s * dsz).astype(dp_ref.dtype)
        dsg = dyv * sz
        dp_ref[:, :2048] = (dsg * s * gu).astype(dp_ref.dtype)
        ds = dsg * u
        rows = []
        dbs = [jnp.zeros((SGU_CHUNK, 1), F32) for _ in range(SGU_GROUPS)]
        for c in range(tl // SGU_CHUNK):
            rs = slice(SGU_CHUNK * c, SGU_CHUNK * (c + 1))
            cols = []
            for g in range(SGU_GROUPS):
                gs = slice(SGU_GDIM * g, SGU_GDIM * (g + 1))
                dsg_c = ds[rs, gs]
                dbs[g] = dbs[g] + jnp.sum(dsg_c, axis=1, keepdims=True)
                dwm_ref[g] += jnp.where(mask, _mm_nt(dsg_c, vg[rs, gs]), 0.0)
                cols.append(_mm_tn(wms[g], dsg_c))
            rows.append(jnp.concatenate(cols, axis=1))
        dbt_ref[...] += jnp.concatenate(dbs, axis=1)
        dvg = rows[0] if len(rows) == 1 else jnp.concatenate(rows, axis=0)
        dgain_ref[...] += jnp.sum(dvg * vn, axis=0, keepdims=True)
        dvn = dvg * gain_ref[...]
        dv = rstd * (dvn - jnp.mean(dvn, axis=-1, keepdims=True) - vn * jnp.mean(dvn * vn, axis=-1, keepdims=True))
        dp_ref[:, 2048:4096] = (dv * gv).astype(dp_ref.dtype)

    return _call(
        body, plan, name="sgu_backward", grid=(L // tl,),
        in_specs=[pl.BlockSpec((tl, ODD_IN), lambda i: (i, 0)), pl.BlockSpec((tl, 2048), lambda i: (i, 0)),
                  _full((1, 2048)), _full(wm.shape), _full(bt.shape)],
        out_specs=[pl.BlockSpec((tl, ODD_IN), lambda i: (i, 0)), _full((1, 2048)), _full(wm.shape), _full(bt.shape)],
        out_shape=[jax.ShapeDtypeStruct((L, ODD_IN), MXU_DTYPE), jax.ShapeDtypeStruct((1, 2048), F32),
                   jax.ShapeDtypeStruct(wm.shape, F32), jax.ShapeDtypeStruct(bt.shape, F32)],
        sem=("arbitrary",),
    )(p, dy, gain, wm, bt)


def cast_shards(mats):
    n = len(mats)

    def body(*refs):
        for p in range(n):
            refs[n + p][...] = refs[p][...].astype(MXU_DTYPE)

    return pl.pallas_call(
        body, name="cast_shards", out_shape=[jax.ShapeDtypeStruct(m.shape, MXU_DTYPE) for m in mats],
        compiler_params=pltpu.CompilerParams(vmem_limit_bytes=VMEM_LIMIT),
    )(*mats)


def local_grads(x, tgt, w):
    L = x.shape[0]
    ne, gf = w["norm_even"], w["final_norm"].reshape(1, D_MODEL)
    sh = dict(zip(MATRICES, cast_shards([w[n][0] for n in MATRICES])))
    (w_in_e,) = run_plan(gather_plan([sh["w_in_even"]]), "gather_w_in_even")
    lam_re, lam_im = w["s5_lam_re"][0], w["s5_lam_im"][0]
    log_dt = w["s5_log_dt"].reshape(S5_GROUPS, 1)
    bt_re = jnp.transpose(w["s5_b_re"][0], (2, 0, 1))
    bt_im = jnp.transpose(w["s5_b_im"][0], (2, 0, 1))
    c_re, c_im = w["s5_c_re"][0], w["s5_c_im"][0]
    wm = w["sgu_w_spatial"][0]
    bt = jnp.transpose(w["sgu_b_spatial"][0])

    tl5 = min(TL_S5, L)
    ab_re, ab_im, bb_re, bb_im, at_re, at_im = s5_params_fwd(lam_re, lam_im, log_dt, bt_re, bt_im, tl5 // 8)
    atab = jnp.stack([ab_re.reshape(S5_LANES), ab_im.reshape(S5_LANES),
                      at_re.reshape(S5_LANES), at_im.reshape(S5_LANES)])
    wbd = jnp.concatenate([_block_diag(jnp.transpose(bb_re, (1, 0, 2)), True),
                           _block_diag(jnp.transpose(bb_im, (1, 0, 2)), True)], axis=2).astype(MXU_DTYPE)
    cre = _block_diag(jnp.transpose(c_re, (0, 2, 1)), True).astype(MXU_DTYPE)
    cim = _block_diag(jnp.transpose(c_im, (0, 2, 1)), True).astype(MXU_DTYPE)
    cos, sin = _rope_tables(L)

    s5_cols = 2 * S5_WIDTH
    w_s5 = jnp.concatenate([w_in_e[0], w_in_e[1][:, :s5_cols - EVEN_IN // N_CHIPS]], axis=1)
    w_ret = jnp.concatenate([w_in_e[1][:, s5_cols - EVEN_IN // N_CHIPS:], w_in_e[2], w_in_e[3]], axis=1)
    (p1a, h0s), (w_glu,) = norm_matmul(stream_order(x, tl5), ne, w_s5, "even_in_s5",
                                       gather_plan([sh["s5_w_glu"]]), tn=1024)
    (p1b, h0), (w_out_e,) = norm_matmul(x, ne, w_ret, "even_in_ret", gather_plan([sh["w_out_even"]]), tn=1024)
    w_glu = w_glu.reshape(S5_WIDTH, S5_WIDTH)
    w_out_e = w_out_e.reshape(2 * S5_WIDTH, D_MODEL)
    (ya, st_re, st_im), (w_in_o, w_out_o, no, sg_gain) = s5_forward(
        p1a, wbd, cre, cim, atab, w["s5_d"], w_glu, w["s5_b_glu"],
        gather_plan([sh["w_in_odd"], sh["w_out_odd"], w["norm_odd"], w["sgu_norm_gain"]]))
    w_out_o = w_out_o.reshape(SGU_WIDTH, D_MODEL)
    no, sg_gain = no.reshape(1, D_MODEL), sg_gain.reshape(1, SGU_WIDTH)
    yb, prevs = retention_forward(p1b, cos, sin, w["ret_gn_gain"])
    ya = token_order(ya, tl5)
    x1 = matmul_residual([ya, yb], w_out_e, x, "even_out")
    (p2, h1), _ = norm_matmul(x1, no, w_in_o, "odd_in")
    y2 = sgu_forward(p2, sg_gain, wm, bt)
    dx2, loss, dgf = out_proj_loss(y2, w_out_o, x1, gf, tgt, "odd_out_loss")

    g, landed = {}, {}
    shard_major = lambda a, n: a.reshape((N_CHIPS,) + w[n].shape[1:])
    dy2, g_w_out_o = out_proj_bwd(dx2, w_out_o, [y2], "odd_out_bwd")
    (dp2, g["sgu_norm_gain"], dwm, dbt), (landed["w_out_odd"],) = sgu_backward(
        p2, dy2, sg_gain, wm, bt, reduce_plan([shard_major(g_w_out_o, "w_out_odd")]))
    g_w_in_o = in_proj_bwd_dw(h1, dp2, "odd_in_dw", ODD_IN // N_CHIPS)
    (dx1, g["norm_odd"]), _ = in_proj_bwd_dx(x1, no, [dp2], [w_in_o], dx2, "odd_in_dx")
    dya, dyb, g_w_out_e = out_proj_bwd(dx1, w_out_e, [ya, yb], "even_out_bwd")
    ((dpa, dwbd, dcre, dcim, dab_re, dab_im, g["s5_d"], g_w_glu, g["s5_b_glu"]),
     (landed["w_in_odd"], landed["w_out_even"])) = s5_backward(
        p1a, stream_order(dya, tl5), st_re, st_im, wbd, cre, cim, atab, w["s5_d"], w_glu,
        w["s5_b_glu"], reduce_plan([g_w_in_o, shard_major(g_w_out_e, "w_out_even")]))

    dbb_re = jnp.transpose(_block_diag_extract(dwbd[:, :, :512], S5_GROUP, S5_STATE), (1, 0, 2))
    dbb_im = jnp.transpose(_block_diag_extract(dwbd[:, :, 512:], S5_GROUP, S5_STATE), (1, 0, 2))
    dlr, dli, ddt, dbt_re, dbt_im = s5_params_bwd(
        lam_re, lam_im, log_dt, bt_re, bt_im, dab_re.reshape(8, S5_GROUPS, S5_STATE),
        dab_im.reshape(8, S5_GROUPS, S5_STATE), dbb_re, dbb_im)
    g["s5_lam_re"], g["s5_lam_im"] = dlr[None], dli[None]
    g["s5_log_dt"] = ddt.reshape(1, S5_GROUPS)
    g["s5_b_re"], g["s5_b_im"] = dbt_re, dbt_im
    g["s5_c_re"] = jnp.transpose(_block_diag_extract(dcre, S5_STATE, S5_GROUP), (0, 2, 1))[None]
    g["s5_c_im"] = jnp.transpose(_block_diag_extract(dcim, S5_STATE, S5_GROUP), (0, 2, 1))[None]
    g["sgu_w_spatial"] = dwm[None]
    g["sgu_b_spatial"] = jnp.transpose(dbt)[None]
    g["final_norm"] = dgf.reshape(D_MODEL)
    g["loss"] = loss

    early = BEHIND_RETENTION_BWD + ("loss",)
    (dpb, g["ret_gn_gain"]), recv = retention_backward(
        p1b, dyb, prevs, cos, sin, w["ret_gn_gain"],
        reduce_plan([shard_major(g_w_glu, "s5_w_glu")], [g[n] for n in early]))
    landed.update(zip(("s5_w_glu",) + early, recv))
    g_w_in_e = in_proj_bwd_dw(h0s, dpa, "even_in_dw_s5", 512, dtype=MXU_DTYPE)
    g_w_in_e = in_proj_bwd_dw(h0, dpb, "even_in_dw_ret", 512, first=s5_cols // 512, into=g_w_in_e, dtype=MXU_DTYPE)
    (dx0, g["norm_even"]), recv = in_proj_bwd_dx(
        x, ne, [token_order(dpa, tl5), dpb], [w_s5, w_ret], dx1, "even_in_dx",
        reduce_plan([g_w_in_e], [g[n] for n in BEHIND_EVEN_IN_DX]))
    landed.update(zip(("w_in_even",) + BEHIND_EVEN_IN_DX, recv))
    (landed["norm_even"],) = run_plan(reduce_plan([], [g["norm_even"]]), "exchange_norm_even")
    return dx0, landed


def sibling_exchange(arrs):
    n = len(arrs)

    def body(*refs):
        in_refs, out_refs = refs[:n], refs[n:2 * n]
        send_sems, recv_sems = refs[2 * n:]
        x, y, c = _place()
        copies = [pltpu.make_async_remote_copy(
            src_ref=in_refs[p], dst_ref=out_refs[p], send_sem=send_sems.at[p], recv_sem=recv_sems.at[p],
            device_id=(x, y, 1 - c), device_id_type=MESH) for p in range(n)]
        for cp in copies:
            cp.start()
        for cp in copies:
            cp.wait_recv()
        for cp in copies:
            cp.wait_send()

    return pl.pallas_call(
        body, name="sibling_exchange", in_specs=[ANY] * n, out_specs=[ANY] * n,
        out_shape=[jax.ShapeDtypeStruct(a.shape, a.dtype) for a in arrs],
        scratch_shapes=[pltpu.SemaphoreType.DMA((n,)), pltpu.SemaphoreType.DMA((n,))],
    )(*arrs)


def _row_block(rows):
    return 128 if rows % 128 == 0 else rows


def sum_slabs(r, name):
    _, R, C = r.shape
    tr = _row_block(R)

    def body(r_ref, o_ref):
        a, b, c, d = (r_ref[k].astype(F32) for k in range(N_CHIPS))
        o_ref[...] = (a + b) + (c + d)

    return pl.pallas_call(
        body, name=name, grid=(R // tr,),
        in_specs=[pl.BlockSpec((N_CHIPS, tr, C), lambda i: (0, i, 0))],
        out_specs=pl.BlockSpec((tr, C), lambda i: (i, 0)),
        out_shape=jax.ShapeDtypeStruct((R, C), F32),
        compiler_params=_cparams(("arbitrary",)),
    )(r)


def _adam(w, m, v, g):
    mn = ADAM_B1 * m + (1.0 - ADAM_B1) * g
    vn = ADAM_B2 * v + (1.0 - ADAM_B2) * (g * g)
    m_hat = mn / (1.0 - ADAM_B1 ** ADAM_STEP)
    v_hat = vn / (1.0 - ADAM_B2 ** ADAM_STEP)
    return -ADAM_LR * (m_hat / (jnp.sqrt(v_hat) + ADAM_EPS) + ADAM_WD * w), mn, vn


def adam_update(w, m, v, ga, gb, name):
    R, C = w.shape
    tr = _row_block(R)

    def body(w_ref, m_ref, v_ref, ga_ref, gb_ref, g_out, d_out, m_out, v_out):
        g = ga_ref[...] + gb_ref[...]
        g_out[...] = g
        d_out[...], m_out[...], v_out[...] = _adam(w_ref[...], m_ref[...], v_ref[...], g)

    blk = pl.BlockSpec((tr, C), lambda i: (i, 0))
    return pl.pallas_call(
        body, name=name, grid=(R // tr,),
        in_specs=[blk] * 5, out_specs=[blk] * 4,
        out_shape=[jax.ShapeDtypeStruct((R, C), F32)] * 4,
        compiler_params=_cparams(("arbitrary",)),
    )(w, m, v, ga, gb)


WIDE_ROWS = ("s5_b_re", "s5_b_im")
GROUP_BLK = 8


def _by_groups(arrs, lead):
    def spec(a):
        blk = a.shape[:lead] + (GROUP_BLK,) + a.shape[lead + 1:]
        nd = len(a.shape)
        return pl.BlockSpec(blk, lambda i: (0,) * lead + (i,) + (0,) * (nd - lead - 1))
    return [spec(a) for a in arrs]


def sum_small(landed):
    def body(*refs):
        k = len(refs) // 2
        for i in range(k):
            r = refs[i]
            refs[k + i][...] = (r[0] + r[1]) + (r[2] + r[3])

    names = list(landed)
    res = pl.pallas_call(
        body, name="sum_small", out_shape=[jax.ShapeDtypeStruct(landed[n].shape[1:], F32) for n in names],
        compiler_params=pltpu.CompilerParams(vmem_limit_bytes=VMEM_LIMIT),
    )(*[landed[n] for n in names])
    return dict(zip(names, res))


def adam_small(names, w, m, v, ga, gb):
    def body(*refs):
        k = len(refs) // 9
        me = 2 * lax.axis_index("x") + lax.axis_index("y")
        for i in range(k):
            w_ref, m_ref, v_ref, ga_ref, gb_ref = refs[i], refs[k + i], refs[2 * k + i], refs[3 * k + i], refs[4 * k + i]
            size = w_ref.shape[-1]
            if ga_ref.shape != w_ref.shape:
                part = pl.ds(pl.multiple_of(me * size, LANES), size)
                g = ga_ref[:, part] + gb_ref[:, part]
            else:
                g = ga_ref[...] + gb_ref[...]
            refs[5 * k + i][...] = g
            refs[6 * k + i][...], refs[7 * k + i][...], refs[8 * k + i][...] = _adam(w_ref[...], m_ref[...], v_ref[...], g)

    def run(group, **kw):
        ins = [d[n] for d in (w, m, v, ga, gb) for n in group]
        outs = [jax.ShapeDtypeStruct(w[n].shape, F32) for _ in range(4) for n in group]
        res = pl.pallas_call(functools.partial(body), out_shape=outs, **kw)(*ins)
        k = len(group)
        return [dict(zip(group, res[j * k:(j + 1) * k])) for j in range(4)]

    plain = [n for n in names if n not in WIDE_ROWS]
    wide = [n for n in names if n in WIDE_ROWS]
    res = run(plain, name="adam_small", compiler_params=pltpu.CompilerParams(vmem_limit_bytes=VMEM_LIMIT))
    if wide:
        specs = _by_groups([w[n] for n in wide], 1)
        res_w = run(wide, name="adam_small_wide", grid=(S5_GROUPS // GROUP_BLK,), in_specs=specs * 5,
                    out_specs=specs * 4, compiler_params=_cparams(("arbitrary",)))
        for d, dw in zip(res, res_w):
            d.update(dw)
    return res


WEIGHTS = ("norm_even", "w_in_even", "s5_lam_re", "s5_lam_im", "s5_log_dt", "s5_b_re", "s5_b_im", "s5_c_re",
           "s5_c_im", "s5_d", "s5_w_glu", "s5_b_glu", "ret_gn_gain", "w_out_even", "norm_odd", "w_in_odd",
           "sgu_norm_gain", "sgu_w_spatial", "sgu_b_spatial", "w_out_odd", "final_norm")
MATRICES = ("w_in_even", "s5_w_glu", "w_out_even", "w_in_odd", "w_out_odd")
SHARDED_VECS = ("norm_odd", "sgu_norm_gain")
REPLICATED = tuple(n for n in WEIGHTS if n not in MATRICES and n not in SHARDED_VECS)
SMALL = tuple(n for n in WEIGHTS if n not in MATRICES)
BEHIND_RETENTION_BWD = tuple(n for n in SMALL if n not in ("ret_gn_gain", "norm_even"))
BEHIND_EVEN_IN_DX = ("ret_gn_gain",)
LANES = 128


def kernel(x, norm_even, w_in_even, s5_lam_re, s5_lam_im, s5_log_dt, s5_b_re, s5_b_im, s5_c_re, s5_c_im, s5_d, s5_w_glu, s5_b_glu, ret_gn_gain, w_out_even, norm_odd, w_in_odd, sgu_norm_gain, sgu_w_spatial, sgu_b_spatial, w_out_odd, final_norm, loss_target, m_norm_even, m_w_in_even, m_s5_lam_re, m_s5_lam_im, m_s5_log_dt, m_s5_b_re, m_s5_b_im, m_s5_c_re, m_s5_c_im, m_s5_d, m_s5_w_glu, m_s5_b_glu, m_ret_gn_gain, m_w_out_even, m_norm_odd, m_w_in_odd, m_sgu_norm_gain, m_sgu_w_spatial, m_sgu_b_spatial, m_w_out_odd, m_final_norm, v_norm_even, v_w_in_even, v_s5_lam_re, v_s5_lam_im, v_s5_log_dt, v_s5_b_re, v_s5_b_im, v_s5_c_re, v_s5_c_im, v_s5_d, v_s5_w_glu, v_s5_b_glu, v_ret_gn_gain, v_w_out_even, v_norm_odd, v_w_in_odd, v_sgu_norm_gain, v_sgu_w_spatial, v_sgu_b_spatial, v_w_out_odd, v_final_norm):
    w = dict(norm_even=norm_even, w_in_even=w_in_even, s5_lam_re=s5_lam_re, s5_lam_im=s5_lam_im, s5_log_dt=s5_log_dt, s5_b_re=s5_b_re, s5_b_im=s5_b_im, s5_c_re=s5_c_re, s5_c_im=s5_c_im, s5_d=s5_d, s5_w_glu=s5_w_glu, s5_b_glu=s5_b_glu, ret_gn_gain=ret_gn_gain, w_out_even=w_out_even, norm_odd=norm_odd, w_in_odd=w_in_odd, sgu_norm_gain=sgu_norm_gain, sgu_w_spatial=sgu_w_spatial, sgu_b_spatial=sgu_b_spatial, w_out_odd=w_out_odd, final_norm=final_norm)
    m = dict(norm_even=m_norm_even, w_in_even=m_w_in_even, s5_lam_re=m_s5_lam_re, s5_lam_im=m_s5_lam_im, s5_log_dt=m_s5_log_dt, s5_b_re=m_s5_b_re, s5_b_im=m_s5_b_im, s5_c_re=m_s5_c_re, s5_c_im=m_s5_c_im, s5_d=m_s5_d, s5_w_glu=m_s5_w_glu, s5_b_glu=m_s5_b_glu, ret_gn_gain=m_ret_gn_gain, w_out_even=m_w_out_even, norm_odd=m_norm_odd, w_in_odd=m_w_in_odd, sgu_norm_gain=m_sgu_norm_gain, sgu_w_spatial=m_sgu_w_spatial, sgu_b_spatial=m_sgu_b_spatial, w_out_odd=m_w_out_odd, final_norm=m_final_norm)
    v = dict(norm_even=v_norm_even, w_in_even=v_w_in_even, s5_lam_re=v_s5_lam_re, s5_lam_im=v_s5_lam_im, s5_log_dt=v_s5_log_dt, s5_b_re=v_s5_b_re, s5_b_im=v_s5_b_im, s5_c_re=v_s5_c_re, s5_c_im=v_s5_c_im, s5_d=v_s5_d, s5_w_glu=v_s5_w_glu, s5_b_glu=v_s5_b_glu, ret_gn_gain=v_ret_gn_gain, w_out_even=v_w_out_even, norm_odd=v_norm_odd, w_in_odd=v_w_in_odd, sgu_norm_gain=v_sgu_norm_gain, sgu_w_spatial=v_sgu_w_spatial, sgu_b_spatial=v_sgu_b_spatial, w_out_odd=v_w_out_odd, final_norm=v_final_norm)

    grad_x, landed = local_grads(x[0], loss_target[0], w)

    small = SMALL + ("loss",)
    part = {n: sum_slabs(landed[n], "sum_" + n) for n in MATRICES}
    part.update(sum_small({n: landed[n] for n in small}))
    names = MATRICES + small
    other = dict(zip(names, sibling_exchange([part[n] for n in names])))
    for n in WIDE_ROWS:
        part[n] = jnp.transpose(part[n], (1, 2, 0))[None]
        other[n] = jnp.transpose(other[n], (1, 2, 0))[None]

    out_g, out_d, out_m, out_v = adam_small(SMALL, w, m, v, part, other)
    for n in MATRICES:
        res = adam_update(w[n][0], m[n][0], v[n][0], part[n], other[n], "adam_" + n)
        out_g[n], out_d[n], out_m[n], out_v[n] = (r[None] for r in res)
    total_loss = (part["loss"] + other["loss"])[0, 0]

    return (total_loss, grad_x[None], *[out_g[n] for n in WEIGHTS], *[out_d[n] for n in WEIGHTS],
            *[out_m[n] for n in WEIGHTS], *[out_v[n] for n in WEIGHTS])
```

```python
import functools
import math

import numpy as np
import jax
import jax.numpy as jnp
from jax import lax
from jax.experimental import pallas as pl
from jax.experimental.pallas import tpu as pltpu

F32 = jnp.float32
MXU_DTYPE = jnp.bfloat16
NORM_EPS = 1e-6
D_MODEL = 1024
S5_WIDTH = 1024
S5_GROUP = 16
S5_GROUPS = 64
S5_STATE = 64
S5_LANES = S5_GROUPS * S5_STATE
S5_KBLK = 8
RET_HEADS = 4
RET_DK = 256
RET_CHUNK = 128
ROPE_BASE = 10000.0
SGU_WIDTH = 2048
SGU_GROUPS = 4
SGU_GDIM = 512
SGU_CHUNK = 128
EVEN_IN = 6144
ODD_IN = 6144
ADAM_LR = 0.001
ADAM_B1 = 0.9
ADAM_B2 = 0.999
ADAM_EPS = 1e-08
ADAM_WD = 0.01
ADAM_STEP = 10
N_CHIPS = 4
VMEM_LIMIT = 56 * 1024 * 1024

TL_PROJ = 512
TL_DW = 1024
TL_S5 = 256
TL_SGU = 128


def _cparams(sem, **kw):
    return pltpu.CompilerParams(dimension_semantics=sem, vmem_limit_bytes=VMEM_LIMIT, **kw)


def _mm(a, b):
    return jnp.dot(a.astype(MXU_DTYPE), b.astype(MXU_DTYPE), preferred_element_type=F32)


def _mm_nt(a, b):
    return lax.dot_general(a.astype(MXU_DTYPE), b.astype(MXU_DTYPE),
                           (((1,), (1,)), ((), ())), preferred_element_type=F32)


def _mm_tn(a, b):
    return lax.dot_general(a.astype(MXU_DTYPE), b.astype(MXU_DTYPE),
                           (((0,), (0,)), ((), ())), preferred_element_type=F32)


_GELU_C = math.sqrt(2.0 / math.pi)


def _gelu_parts(x):
    x2 = x * x
    th = jnp.tanh(x * (_GELU_C + (_GELU_C * 0.044715) * x2))
    hx = 0.5 * x
    return hx + hx * th, th, x2, hx


def _gelu(x):
    return _gelu_parts(x)[0]


def _gelu_and_grad(x):
    g, th, x2, hx = _gelu_parts(x)
    return g, (0.5 + 0.5 * th) + hx * (1.0 - th * th) * (_GELU_C + (3.0 * _GELU_C * 0.044715) * x2)


def _gelu_grad(x):
    return _gelu_and_grad(x)[1]


def _sigmoid(x):
    return 1.0 / (1.0 + jnp.exp(-x))


def _silu_and_grad(x):
    s = _sigmoid(x)
    return x * s, s * (1.0 + x * (1.0 - s))


def _rms(x):
    return lax.rsqrt(jnp.mean(x * x, axis=-1, keepdims=True) + NORM_EPS)


def _full(shape):
    nd = len(shape)
    return pl.BlockSpec(shape, lambda *_: (0,) * nd)


MESH = pl.DeviceIdType.MESH
ANY = pl.BlockSpec(memory_space=pl.ANY)


def _place():
    return lax.axis_index("x"), lax.axis_index("y"), lax.axis_index("c")


def _chip_peer(x, y, c, d):
    return (1 - x if d >= 2 else x, 1 - y if d % 2 else y, c)


class _Plan:
    def __init__(self, inputs, out_shape, build):
        self.inputs, self.out_shape, self._build = list(inputs), list(out_shape), build
        n = len(self.inputs)
        self.sems = [pltpu.SemaphoreType.DMA((n, 3)), pltpu.SemaphoreType.DMA((n, 3)), pltpu.SemaphoreType.DMA((n,))]

    def start(self, in_refs, out_refs, sems):
        send, recv, local = self._build(in_refs, out_refs, sems)
        for p in range(len(self.inputs)):
            local[p].start()
            for cp in send[p]:
                cp.start()

    def wait(self, in_refs, out_refs, sems):
        send, recv, local = self._build(in_refs, out_refs, sems)
        for p in range(len(self.inputs)):
            for cp in recv[p]:
                cp.wait_recv()
        for p in range(len(self.inputs)):
            for cp in send[p]:
                cp.wait_send()
            local[p].wait()


class _GatherPlan:
    def __init__(self, shards):
        self.inputs = list(shards)
        self.out_shape = [jax.ShapeDtypeStruct((N_CHIPS,) + s.shape, s.dtype) for s in shards]
        n = len(shards)
        self.halved = [s.shape[0] % 32 == 0 for s in shards]
        self.sems = [pltpu.SemaphoreType.DMA((n, 3)) for _ in range(4)] + [pltpu.SemaphoreType.DMA((n,))]

    def _copies(self, in_refs, out_refs, sems):
        ici_s, ici_r, d2d_s, d2d_r, loc = sems
        x, y, c = _place()
        me = 2 * x + y

        def rows(p, core):
            if not self.halved[p]:
                return slice(None)
            half = self.inputs[p].shape[0] // 2
            return pl.ds(pl.multiple_of(core * half, 16), half)

        def ici(p, d, slab, core, src=None):
            dst = out_refs[p].at[slab, rows(p, core)]
            return pltpu.make_async_remote_copy(
                src_ref=in_refs[p].at[rows(p, core)] if src is None else src, dst_ref=dst,
                send_sem=ici_s.at[p, d - 1], recv_sem=ici_r.at[p, d - 1],
                device_id=_chip_peer(x, y, c, d), device_id_type=MESH)

        def d2d(p, d, core):
            part = out_refs[p].at[me ^ d, rows(p, core)]
            return pltpu.make_async_remote_copy(
                src_ref=part, dst_ref=part, send_sem=d2d_s.at[p, d - 1], recv_sem=d2d_r.at[p, d - 1],
                device_id=(x, y, 1 - c), device_id_type=MESH)

        local = [pltpu.make_async_copy(in_refs[p], out_refs[p].at[me], loc.at[p]) for p in range(len(self.inputs))]
        return me, c, ici, d2d, local

    def start(self, in_refs, out_refs, sems):
        me, c, ici, d2d, local = self._copies(in_refs, out_refs, sems)
        for p in range(len(self.inputs)):
            local[p].start()
            for d in (1, 2, 3):
                ici(p, d, me, c).start()

    def wait(self, in_refs, out_refs, sems):
        me, c, ici, d2d, local = self._copies(in_refs, out_refs, sems)
        n = len(self.inputs)
        for p in range(n):
            for d in (1, 2, 3):
                ici(p, d, me ^ d, c).wait_recv()
                if self.halved[p]:
                    d2d(p, d, c).start()
        for p in range(n):
            for d in (1, 2, 3):
                if self.halved[p]:
                    d2d(p, d, 1 - c).wait_recv()
                    d2d(p, d, c).wait_send()
                ici(p, d, me, c).wait_send()
            local[p].wait()


def gather_plan(shards):
    return _GatherPlan(shards)


def reduce_plan(shards, whole=()):
    n_s = len(shards)

    def build(in_refs, out_refs, sems):
        send_sems, recv_sems, loc_sems = sems
        x, y, c = _place()
        me = 2 * x + y

        def src(p, slab):
            return in_refs[p].at[slab] if p < n_s else in_refs[p]

        def remote(p, d):
            return pltpu.make_async_remote_copy(
                src_ref=src(p, me ^ d), dst_ref=out_refs[p].at[d], send_sem=send_sems.at[p, d - 1],
                recv_sem=recv_sems.at[p, d - 1], device_id=_chip_peer(x, y, c, d), device_id_type=MESH)

        n = len(in_refs)
        send = [[remote(p, d) for d in (1, 2, 3)] for p in range(n)]
        local = [pltpu.make_async_copy(src(p, me), out_refs[p].at[0], loc_sems.at[p]) for p in range(n)]
        return send, send, local

    outs = [jax.ShapeDtypeStruct(s.shape, s.dtype) for s in shards]
    outs += [jax.ShapeDtypeStruct((N_CHIPS,) + a.shape, a.dtype) for a in whole]
    return _Plan(list(shards) + list(whole), outs, build)


def run_plan(plan, name):
    n = len(plan.inputs)

    def body(*refs):
        plan.start(refs[:n], refs[n:2 * n], refs[2 * n:])
        plan.wait(refs[:n], refs[n:2 * n], refs[2 * n:])

    return pl.pallas_call(body, name=name, in_specs=[ANY] * n, out_specs=[ANY] * n, out_shape=plan.out_shape,
                          scratch_shapes=plan.sems)(*plan.inputs)


def _call(body, plan, *, name, grid, in_specs, out_specs, out_shape, sem, scratch_shapes=()):
    single = not isinstance(out_shape, (list, tuple))
    out_specs = [out_specs] if single else list(out_specs)
    out_shape = [out_shape] if single else list(out_shape)
    n_in, n_out, n_scr = len(in_specs), len(out_specs), len(scratch_shapes)
    ci = 0 if plan is None else len(plan.inputs)

    def hosted(*refs):
        ins, cins = refs[:n_in], refs[n_in:n_in + ci]
        k = n_in + ci
        outs, couts = refs[k:k + n_out], refs[k + n_out:k + n_out + ci]
        k += n_out + ci
        scr, sems = refs[k:k + n_scr], refs[k + n_scr:]
        ids = [pl.program_id(a) for a in range(len(grid))]
        first = functools.reduce(jnp.logical_and, [i == 0 for i in ids])
        last = functools.reduce(jnp.logical_and, [i == g - 1 for i, g in zip(ids, grid)])

        @pl.when(first)
        def _():
            plan.start(cins, couts, sems)

        body(*ins, *outs, *scr)

        @pl.when(last)
        def _():
            plan.wait(cins, couts, sems)

    def run(*args):
        if plan is None:
            res = pl.pallas_call(body, name=name, grid=grid, in_specs=list(in_specs), out_specs=out_specs,
                                 out_shape=out_shape, scratch_shapes=list(scratch_shapes),
                                 compiler_params=_cparams(sem))(*args)
            return (res[0] if single else res), []
        res = pl.pallas_call(hosted, name=name, grid=grid, in_specs=list(in_specs) + [ANY] * ci,
                             out_specs=out_specs + [ANY] * ci, out_shape=out_shape + plan.out_shape,
                             scratch_shapes=list(scratch_shapes) + plan.sems,
                             compiler_params=_cparams(sem))(*args, *plan.inputs)
        return (res[0] if single else res[:n_out]), list(res[n_out:])

    return run


def norm_matmul(x, g, w, name, plan=None, tn=None):
    L, D = x.shape
    tl = min(TL_DW, L)
    if w.ndim == 3:
        nt, _, tn = w.shape
        w_spec = pl.BlockSpec((1, D, tn), lambda i, n: (n, 0, 0))
    else:
        nt = w.shape[1] // tn
        w_spec = pl.BlockSpec((D, tn), lambda i, n: (0, n))

    def body(x_ref, g_ref, w_ref, o_ref, h_ref):
        xv = x_ref[...]
        h = (xv * _rms(xv) * g_ref[...]).astype(h_ref.dtype)
        h_ref[...] = h
        o_ref[...] = _mm(h, w_ref[0] if w.ndim == 3 else w_ref[...])

    return _call(
        body, plan, name=name, grid=(L // tl, nt),
        in_specs=[pl.BlockSpec((tl, D), lambda i, n: (i, 0)), _full((1, D)), w_spec],
        out_specs=[pl.BlockSpec((tl, tn), lambda i, n: (i, n)), pl.BlockSpec((tl, D), lambda i, n: (i, 0))],
        out_shape=[jax.ShapeDtypeStruct((L, nt * tn), F32), jax.ShapeDtypeStruct((L, D), MXU_DTYPE)],
        sem=("arbitrary", "arbitrary"),
    )(x, g, w)


def matmul_residual(ys, w, x, name):
    L, D = x.shape
    tl = min(TL_PROJ, L)
    n = len(ys)
    offs = np.cumsum([0] + [y.shape[1] for y in ys])

    def body(*refs):
        y_refs, w_ref, x_ref, o_ref = refs[:n], refs[n], refs[n + 1], refs[n + 2]
        acc = x_ref[...]
        for k in range(n):
            acc = acc + _mm(y_refs[k][...], w_ref[offs[k]:offs[k + 1], :])
        o_ref[...] = acc

    return pl.pallas_call(
        body, name=name, grid=(L // tl,),
        in_specs=[pl.BlockSpec((tl, y.shape[1]), lambda i: (i, 0)) for y in ys]
        + [_full(w.shape), pl.BlockSpec((tl, D), lambda i: (i, 0))],
        out_specs=pl.BlockSpec((tl, D), lambda i: (i, 0)),
        out_shape=jax.ShapeDtypeStruct((L, D), F32),
        compiler_params=_cparams(("arbitrary",)),
    )(*ys, w, x)


def out_proj_loss(y, w, x, gf, tgt, name):
    L, K = y.shape
    D = w.shape[1]
    tl = min(TL_PROJ, L)

    def body(y_ref, w_ref, x_ref, gf_ref, t_ref, dx_ref, loss_ref, dg_ref):
        @pl.when(pl.program_id(0) == 0)
        def _():
            loss_ref[...] = jnp.zeros_like(loss_ref)
            dg_ref[...] = jnp.zeros_like(dg_ref)

        x2 = x_ref[...] + _mm(y_ref[...], w_ref[...])
        r = _rms(x2)
        xn = x2 * r
        e = xn * gf_ref[...] - t_ref[...]
        loss_ref[...] += (0.5 / D) * jnp.sum(e * e)
        dout = e * (1.0 / D)
        dg_ref[...] += jnp.sum(dout * xn, axis=0, keepdims=True)
        dxn = dout * gf_ref[...]
        dx_ref[...] = r * (dxn - xn * jnp.mean(dxn * xn, axis=-1, keepdims=True))

    return pl.pallas_call(
        body, name=name, grid=(L // tl,),
        in_specs=[pl.BlockSpec((tl, K), lambda i: (i, 0)), _full((K, D)),
                  pl.BlockSpec((tl, D), lambda i: (i, 0)), _full((1, D)),
                  pl.BlockSpec((tl, D), lambda i: (i, 0))],
        out_specs=[pl.BlockSpec((tl, D), lambda i: (i, 0)), _full((8, 128)), _full((1, D))],
        out_shape=[jax.ShapeDtypeStruct((L, D), F32), jax.ShapeDtypeStruct((8, 128), F32),
                   jax.ShapeDtypeStruct((1, D), F32)],
        compiler_params=_cparams(("arbitrary",)),
    )(y, w, x, gf, tgt)


def out_proj_bwd(dx, w, ys, name):
    L, D = dx.shape
    K = w.shape[0]
    tl = min(TL_PROJ, L)
    n = len(ys)
    offs = np.cumsum([0] + [y.shape[1] for y in ys])

    def body(*refs):
        dx_ref, w_ref, y_refs = refs[0], refs[1], refs[2:2 + n]
        dy_refs, dw_ref = refs[2 + n:2 + 2 * n], refs[2 + 2 * n]

        @pl.when(pl.program_id(0) == 0)
        def _():
            dw_ref[...] = jnp.zeros_like(dw_ref)

        dxv = dx_ref[...]
        for k in range(n):
            dy_refs[k][...] = _mm_nt(dxv, w_ref[offs[k]:offs[k + 1], :])
            dw_ref[offs[k]:offs[k + 1], :] += _mm_tn(y_refs[k][...], dxv)

    y_specs = [pl.BlockSpec((tl, y.shape[1]), lambda i: (i, 0)) for y in ys]
    return pl.pallas_call(
        body, name=name, grid=(L // tl,),
        in_specs=[pl.BlockSpec((tl, D), lambda i: (i, 0)), _full((K, D))] + y_specs,
        out_specs=y_specs + [_full((K, D))],
        out_shape=[jax.ShapeDtypeStruct(y.shape, F32) for y in ys] + [jax.ShapeDtypeStruct((K, D), F32)],
        compiler_params=_cparams(("arbitrary",)),
    )(dx, w, *ys)


def in_proj_bwd_dx(x, g, dps, ws, dres, name, plan=None):
    L, D = x.shape
    tl = min(TL_PROJ, L)
    n = len(dps)

    def body(*refs):
        x_ref, g_ref, dres_ref = refs[:3]
        dp_refs, w_refs = refs[3:3 + n], refs[3 + n:3 + 2 * n]
        dx_ref, dg_ref = refs[3 + 2 * n:]

        @pl.when(pl.program_id(0) == 0)
        def _():
            dg_ref[...] = jnp.zeros_like(dg_ref)

        dh = None
        for dp_ref, w_ref, w in zip(dp_refs, w_refs, ws):
            if w.ndim == 3:
                tn = w.shape[2]
                parts = [_mm_nt(dp_ref[:, tn * k:tn * (k + 1)], w_ref[k]) for k in range(w.shape[0])]
            else:
                parts = [_mm_nt(dp_ref[...], w_ref[...])]
            for part in parts:
                dh = part if dh is None else dh + part
        xv = x_ref[...]
        r = _rms(xv)
        xn = xv * r
        dg_ref[...] += jnp.sum(dh * xn, axis=0, keepdims=True)
        dxn = dh * g_ref[...]
        dx_ref[...] = dres_ref[...] + r * (dxn - xn * jnp.mean(dxn * xn, axis=-1, keepdims=True))

    return _call(
        body, plan, name=name, grid=(L // tl,),
        in_specs=[pl.BlockSpec((tl, D), lambda i: (i, 0)), _full((1, D)), pl.BlockSpec((tl, D), lambda i: (i, 0))]
        + [pl.BlockSpec((tl, dp.shape[1]), lambda i: (i, 0)) for dp in dps] + [_full(w.shape) for w in ws],
        out_specs=[pl.BlockSpec((tl, D), lambda i: (i, 0)), _full((1, D))],
        out_shape=[jax.ShapeDtypeStruct((L, D), F32), jax.ShapeDtypeStruct((1, D), F32)],
        sem=("arbitrary",),
    )(x, g, dres, *dps, *ws)


def in_proj_bwd_dw(h, dp, name, tn, first=0, into=None, dtype=F32):
    L, D = h.shape
    tl = min(TL_DW, L)
    wb = EVEN_IN // N_CHIPS
    per = wb // tn
    count = dp.shape[1] // tn
    last = L // tl - 1

    def body(*refs):
        h_ref, dp_ref, dw_ref, acc = refs[0], refs[1], refs[-2], refs[-1]

        @pl.when(pl.program_id(1) == 0)
        def _():
            acc[...] = jnp.zeros_like(acc)

        acc[...] += _mm_tn(h_ref[...], dp_ref[...])

        @pl.when(pl.program_id(1) == last)
        def _():
            dw_ref[0] = acc[...].astype(dw_ref.dtype)

    ins = [h, dp] + ([] if into is None else [into])
    return pl.pallas_call(
        body, name=name, grid=(count, L // tl),
        in_specs=[pl.BlockSpec((tl, D), lambda n, i: (i, 0)), pl.BlockSpec((tl, tn), lambda n, i: (i, n))]
        + ([] if into is None else [ANY]),
        out_specs=pl.BlockSpec((1, D, tn), lambda n, i: ((n + first) // per, 0, (n + first) % per)),
        out_shape=jax.ShapeDtypeStruct((N_CHIPS, D, wb), dtype),
        scratch_shapes=[pltpu.VMEM((D, tn), F32)],
        input_output_aliases={} if into is None else {2: 0},
        compiler_params=_cparams(("arbitrary", "arbitrary")),
    )(*ins)


def _s5_param_fn(lam_re, lam_im, log_dt, b_re, b_im):
    lr = jnp.minimum(lam_re, -1e-4)
    li = lam_im
    dt = jnp.exp(log_dt)
    mag = jnp.exp(lr * dt)
    ab_re = mag * jnp.cos(li * dt)
    ab_im = mag * jnp.sin(li * dt)
    den = lr * lr + li * li
    n_re = ab_re - 1.0
    n_im = ab_im
    z_re = (n_re * lr + n_im * li) / den
    z_im = (n_im * lr - n_re * li) / den
    bb_re = z_re[None] * b_re - z_im[None] * b_im
    bb_im = z_re[None] * b_im + z_im[None] * b_re
    return ab_re, ab_im, bb_re, bb_im


def s5_params_fwd(lam_re, lam_im, log_dt, b_re, b_im, span):
    G, P = lam_re.shape
    H = b_re.shape[0]
    assert span & (span - 1) == 0

    def body(lr_ref, li_ref, dt_ref, br_ref, bi_ref, abr_ref, abi_ref, bbr_ref, bbi_ref, pr_ref, pi_ref):
        ab_re, ab_im, bb_re, bb_im = _s5_param_fn(lr_ref[...], li_ref[...], dt_ref[...], br_ref[...], bi_ref[...])
        abr_ref[...] = ab_re
        abi_ref[...] = ab_im
        bbr_ref[...] = bb_re
        bbi_ref[...] = bb_im
        cr, ci = ab_re, ab_im
        for _ in range(span.bit_length() - 1):
            cr, ci = cr * cr - ci * ci, 2.0 * cr * ci
        pr_ref[...] = cr
        pi_ref[...] = ci

    shp = lambda *s: jax.ShapeDtypeStruct(s, F32)
    return pl.pallas_call(
        body, name="s5_params_fwd",
        out_shape=[shp(G, P), shp(G, P), shp(H, G, P), shp(H, G, P), shp(G, P), shp(G, P)],
    )(lam_re, lam_im, log_dt, b_re, b_im)


def s5_params_bwd(lam_re, lam_im, log_dt, b_re, b_im, d_ab_re, d_ab_im, d_bb_re, d_bb_im):
    G, P = lam_re.shape
    H = b_re.shape[0]

    def body(lr_ref, li_ref, dt_ref, br_ref, bi_ref, g0, g1, g2, g3, o0, o1, o2, o3, o4):
        prim = (lr_ref[...], li_ref[...], dt_ref[...], br_ref[...], bi_ref[...])
        _, vjp = jax.vjp(_s5_param_fn, *prim)
        d = vjp((jnp.sum(g0[...], axis=0), jnp.sum(g1[...], axis=0), g2[...], g3[...]))
        o0[...], o1[...], o2[...], o3[...], o4[...] = d

    shp = lambda *s: jax.ShapeDtypeStruct(s, F32)
    return pl.pallas_call(
        body, name="s5_params_bwd",
        out_shape=[shp(G, P), shp(G, P), shp(G, 1), shp(H, G, P), shp(H, G, P)],
    )(lam_re, lam_im, log_dt, b_re, b_im, d_ab_re, d_ab_im, d_bb_re, d_bb_im)


def stream_order(a, tl):
    L, C = a.shape
    return a.reshape(L // tl, 8, tl // 8, C).transpose(0, 2, 1, 3).reshape(L, C)


def token_order(a, tl):
    L, C = a.shape
    return a.reshape(L // tl, tl // 8, 8, C).transpose(0, 2, 1, 3).reshape(L, C)


_LANE_BLK = 1024


def _cmul_add(ar, ai, xr, xi, br, bi):
    return br + (ar * xr - ai * xi), bi + (ar * xi + ai * xr)


def _cmulc_add(ar, ai, xr, xi, br, bi):
    return br + (ar * xr + ai * xi), bi + (ar * xi - ai * xr)


def _s5_states(u, wbd_ref, a_re, a_im, at_re, at_im, s_re, s_im, e_re, e_im, c0_re, c0_im, tl):
    t8 = tl // 8
    for k in range(S5_KBLK):
        bu = _mm(u[:, 128 * k:128 * (k + 1)], wbd_ref[k])
        s_re[:, 512 * k:512 * (k + 1)] = bu[:, :512]
        s_im[:, 512 * k:512 * (k + 1)] = bu[:, 512:]
    outs_re, outs_im = [], []
    for b in range(S5_LANES // _LANE_BLK):
        lanes = slice(_LANE_BLK * b, _LANE_BLK * (b + 1))
        ar = jnp.broadcast_to(a_re[:, lanes], (8, _LANE_BLK))
        ai = jnp.broadcast_to(a_im[:, lanes], (8, _LANE_BLK))

        def local(i, carry, lanes=lanes, ar=ar, ai=ai):
            r = pl.multiple_of(i * 8, 8)
            sr, si = _cmul_add(ar, ai, carry[0], carry[1], s_re[pl.ds(r, 8), lanes], s_im[pl.ds(r, 8), lanes])
            s_re[pl.ds(r, 8), lanes] = sr
            s_im[pl.ds(r, 8), lanes] = si
            return sr, si

        zero = jnp.zeros((8, _LANE_BLK), F32)
        fr, fi = lax.fori_loop(0, t8, local, (zero, zero), unroll=True)
        tr, ti = at_re[:, lanes], at_im[:, lanes]
        er, ei = c0_re[:, lanes], c0_im[:, lanes]
        ers, eis = [er], [ei]
        for j in range(8):
            er, ei = _cmul_add(tr, ti, er, ei, fr[j:j + 1], fi[j:j + 1])
            ers.append(er)
            eis.append(ei)
        outs_re.append(ers[8])
        outs_im.append(eis[8])
        ent_r, ent_i = jnp.concatenate(ers[:8], axis=0), jnp.concatenate(eis[:8], axis=0)
        e_re[:, lanes] = ent_r
        e_im[:, lanes] = ent_i

        def fix(i, carry, lanes=lanes, ar=ar, ai=ai):
            r = pl.multiple_of(i * 8, 8)
            zr, zi = ar * carry[0] - ai * carry[1], ar * carry[1] + ai * carry[0]
            s_re[pl.ds(r, 8), lanes] = s_re[pl.ds(r, 8), lanes] + zr
            s_im[pl.ds(r, 8), lanes] = s_im[pl.ds(r, 8), lanes] + zi
            return zr, zi

        lax.fori_loop(0, t8, fix, (ent_r, ent_i), unroll=True)
    return jnp.concatenate(outs_re, axis=1), jnp.concatenate(outs_im, axis=1)


def _s5_readout(s_re, s_im, cre_ref, cim_ref):
    ys = []
    for k in range(S5_KBLK):
        lanes = slice(512 * k, 512 * (k + 1))
        ys.append(_mm(s_re[:, lanes], cre_ref[k]) - _mm(s_im[:, lanes], cim_ref[k]))
    return jnp.concatenate(ys, axis=1)


def s5_forward(p, wbd, cre, cim, atab, d_skip, w_glu, b_glu, plan=None):
    L = p.shape[0]
    tl = min(TL_S5, L)
    nch = L // tl

    def body(u_ref, z_ref, wbd_ref, cre_ref, cim_ref, at_ref, d_ref, wg_ref, bg_ref,
             ya_ref, st_re_ref, st_im_ref, sv_re_ref, sv_im_ref, s_re, s_im, e_re, e_im, car_re, car_im):
        @pl.when(pl.program_id(0) == 0)
        def _():
            car_re[...] = jnp.zeros_like(car_re)
            car_im[...] = jnp.zeros_like(car_im)

        c0_re, c0_im = car_re[...], car_im[...]
        st_re_ref[0] = c0_re
        st_im_ref[0] = c0_im
        u = u_ref[...]
        x_re, x_im = _s5_states(u, wbd_ref, at_ref[0:1], at_ref[1:2], at_ref[2:3], at_ref[3:4],
                                s_re, s_im, e_re, e_im, c0_re, c0_im, tl)
        car_re[...] = x_re
        car_im[...] = x_im
        sv_re_ref[...] = s_re[...].astype(sv_re_ref.dtype)
        sv_im_ref[...] = s_im[...].astype(sv_im_ref.dtype)
        y = _s5_readout(sv_re_ref, sv_im_ref, cre_ref, cim_ref) + d_ref[...] * u
        yg = _gelu(y)
        gate = _sigmoid(_mm(yg, wg_ref[...]) + bg_ref[...])
        sz, _ = _silu_and_grad(z_ref[...])
        ya_ref[...] = (yg * gate * sz).astype(ya_ref.dtype)

    return _call(
        body, plan, name="s5_forward", grid=(nch,),
        in_specs=[pl.BlockSpec((tl, 1024), lambda i: (i, 0)), pl.BlockSpec((tl, 1024), lambda i: (i, 1)),
                  _full(wbd.shape), _full(cre.shape), _full(cim.shape), _full(atab.shape),
                  _full((1, 1024)), _full((1024, 1024)), _full((1, 1024))],
        out_specs=[pl.BlockSpec((tl, 1024), lambda i: (i, 0)),
                   pl.BlockSpec((1, 1, S5_LANES), lambda i: (i, 0, 0)),
                   pl.BlockSpec((1, 1, S5_LANES), lambda i: (i, 0, 0)),
                   pl.BlockSpec((tl, S5_LANES), lambda i: (i, 0)), pl.BlockSpec((tl, S5_LANES), lambda i: (i, 0))],
        out_shape=[jax.ShapeDtypeStruct((L, 1024), MXU_DTYPE),
                   jax.ShapeDtypeStruct((nch, 1, S5_LANES), F32), jax.ShapeDtypeStruct((nch, 1, S5_LANES), F32),
                   jax.ShapeDtypeStruct((L, S5_LANES), MXU_DTYPE), jax.ShapeDtypeStruct((L, S5_LANES), MXU_DTYPE)],
        scratch_shapes=[pltpu.VMEM((tl, S5_LANES), F32), pltpu.VMEM((tl, S5_LANES), F32),
                        pltpu.VMEM((8, S5_LANES), F32), pltpu.VMEM((8, S5_LANES), F32),
                        pltpu.VMEM((1, S5_LANES), F32), pltpu.VMEM((1, S5_LANES), F32)],
        sem=("arbitrary",),
    )(p, p, wbd, cre, cim, atab, d_skip, w_glu, b_glu)


def s5_backward(p, dya, st_re, st_im, sv_re, sv_im, wbd, cre, cim, atab, d_skip, w_glu, b_glu, plan=None):
    L = p.shape[0]
    tl = min(TL_S5, L)
    t8 = tl // 8
    nch = L // tl
    rev = lambda i: (nch - 1 - i, 0)
    rev1 = lambda i: (nch - 1 - i, 1)
    rev3 = lambda i: (nch - 1 - i, 0, 0)
    ct_shape = (S5_KBLK, cre.shape[2], cre.shape[1])

    def body(u_ref, z_ref, dya_ref, str_ref, sti_ref, s_re, s_im, wbd_ref, cre_ref, cim_ref, at_ref,
             d_ref, wg_ref, bg_ref,
             dp_ref, dwbd_ref, dcre_ref, dcim_ref, dabr_ref, dabi_ref, dd_ref, dwg_ref, dbg_ref,
             g_re, g_im, car_re, car_im):
        @pl.when(pl.program_id(0) == 0)
        def _():
            car_re[...] = jnp.zeros_like(car_re)
            car_im[...] = jnp.zeros_like(car_im)
            for r in (dwbd_ref, dcre_ref, dcim_ref, dabr_ref, dabi_ref, dd_ref, dwg_ref, dbg_ref):
                r[...] = jnp.zeros_like(r)

        u = u_ref[...]
        a_re, a_im, at_re, at_im = at_ref[0:1], at_ref[1:2], at_ref[2:3], at_ref[3:4]
        y = _s5_readout(s_re, s_im, cre_ref, cim_ref) + d_ref[...] * u
        yg, dyg = _gelu_and_grad(y)
        gate = _sigmoid(_mm(yg, wg_ref[...]) + bg_ref[...])
        sz, dsz = _silu_and_grad(z_ref[...])
        dya = dya_ref[...]
        s5out = yg * gate
        dp_ref[:, 1024:] = (dya * s5out * dsz).astype(dp_ref.dtype)
        ds5 = dya * sz
        dt = ds5 * yg * gate * (1.0 - gate)
        dwg_ref[...] += _mm_tn(yg, dt)
        dbg_ref[...] += jnp.sum(dt, axis=0, keepdims=True)
        dyv = (ds5 * gate + _mm_nt(dt, wg_ref[...])) * dyg
        dd_ref[...] += jnp.sum(dyv * u, axis=0, keepdims=True)

        for k in range(S5_KBLK):
            lanes = slice(512 * k, 512 * (k + 1))
            dyk = dyv[:, 128 * k:128 * (k + 1)]
            g_re[:, lanes] = _mm_nt(dyk, cre_ref[k])
            g_im[:, lanes] = -_mm_nt(dyk, cim_ref[k])
            dcre_ref[k] += _mm_tn(dyk, s_re[:, lanes])
            dcim_ref[k] -= _mm_tn(dyk, s_im[:, lanes])

        for b in range(S5_LANES // _LANE_BLK):
            lanes = slice(_LANE_BLK * b, _LANE_BLK * (b + 1))
            ar = jnp.broadcast_to(a_re[:, lanes], (8, _LANE_BLK))
            ai = jnp.broadcast_to(a_im[:, lanes], (8, _LANE_BLK))

            def local(j, carry, lanes=lanes, ar=ar, ai=ai):
                r = pl.multiple_of((t8 - 1 - j) * 8, 8)
                gr, gi = _cmulc_add(ar, ai, carry[0], carry[1], g_re[pl.ds(r, 8), lanes], g_im[pl.ds(r, 8), lanes])
                g_re[pl.ds(r, 8), lanes] = gr
                g_im[pl.ds(r, 8), lanes] = gi
                return gr, gi

            zero = jnp.zeros((8, _LANE_BLK), F32)
            fr, fi = lax.fori_loop(0, t8, local, (zero, zero), unroll=True)
            tr, ti = at_re[:, lanes], at_im[:, lanes]
            hr, hi = car_re[:, lanes], car_im[:, lanes]
            hrs, his = [hr], [hi]
            for j in range(7, -1, -1):
                hr, hi = _cmulc_add(tr, ti, hr, hi, fr[j:j + 1], fi[j:j + 1])
                hrs.append(hr)
                his.append(hi)
            car_re[:, lanes] = hrs[8]
            car_im[:, lanes] = his[8]
            in_r = jnp.concatenate(hrs[7::-1], axis=0)
            in_i = jnp.concatenate(his[7::-1], axis=0)

            wr, wi, nr, ni, accr, acci = in_r, in_i, zero, zero, zero, zero
            for pair in range(t8 // 2 - 1, -1, -1):
                rows = slice(16 * pair, 16 * pair + 16)
                s16r, s16i = s_re[rows, lanes].astype(F32), s_im[rows, lanes].astype(F32)
                for half in (1, 0):
                    r = 16 * pair + 8 * half
                    sr, si = s16r[8 * half:8 * half + 8], s16i[8 * half:8 * half + 8]
                    accr, acci = accr + (sr * nr + si * ni), acci + (sr * ni - si * nr)
                    wr, wi = ar * wr + ai * wi, ar * wi - ai * wr
                    nr, ni = g_re[r:r + 8, lanes] + wr, g_im[r:r + 8, lanes] + wi
                    g_re[r:r + 8, lanes] = nr
                    g_im[r:r + 8, lanes] = ni
            lr, li = s_re[tl - 16:tl, lanes].astype(F32)[8:], s_im[tl - 16:tl, lanes].astype(F32)[8:]
            row0 = lax.broadcasted_iota(jnp.int32, (8, _LANE_BLK), 0) == 0
            sr = jnp.where(row0, jnp.broadcast_to(str_ref[0][:, lanes], (8, _LANE_BLK)), pltpu.roll(lr, 1, 0))
            si = jnp.where(row0, jnp.broadcast_to(sti_ref[0][:, lanes], (8, _LANE_BLK)), pltpu.roll(li, 1, 0))
            dabr_ref[:, lanes] += accr + (sr * nr + si * ni)
            dabi_ref[:, lanes] += acci + (sr * ni - si * nr)

        dus = []
        for k in range(S5_KBLK):
            lanes = slice(512 * k, 512 * (k + 1))
            g = jnp.concatenate([g_re[:, lanes], g_im[:, lanes]], axis=1)
            dwbd_ref[k] += _mm_tn(u[:, 128 * k:128 * (k + 1)], g)
            dus.append(_mm_nt(g, wbd_ref[k]))
        du = jnp.concatenate(dus, axis=1) + dyv * d_ref[...]
        dp_ref[:, :1024] = du.astype(dp_ref.dtype)

    shp = lambda *s: jax.ShapeDtypeStruct(s, F32)
    return _call(
        body, plan, name="s5_backward", grid=(nch,),
        in_specs=[pl.BlockSpec((tl, 1024), rev), pl.BlockSpec((tl, 1024), rev1), pl.BlockSpec((tl, 1024), rev),
                  pl.BlockSpec((1, 1, S5_LANES), rev3), pl.BlockSpec((1, 1, S5_LANES), rev3),
                  pl.BlockSpec((tl, S5_LANES), rev), pl.BlockSpec((tl, S5_LANES), rev),
                  _full(wbd.shape), _full(cre.shape), _full(cim.shape), _full(atab.shape),
                  _full((1, 1024)), _full((1024, 1024)), _full((1, 1024))],
        out_specs=[pl.BlockSpec((tl, 2048), rev), _full(wbd.shape), _full(ct_shape), _full(ct_shape),
                   _full((8, S5_LANES)), _full((8, S5_LANES)), _full((1, 1024)), _full((1024, 1024)), _full((1, 1024))],
        out_shape=[jax.ShapeDtypeStruct((L, 2048), MXU_DTYPE), shp(*wbd.shape), shp(*ct_shape), shp(*ct_shape),
                   shp(8, S5_LANES), shp(8, S5_LANES), shp(1, 1024), shp(1024, 1024), shp(1, 1024)],
        scratch_shapes=[pltpu.VMEM((tl, S5_LANES), F32), pltpu.VMEM((tl, S5_LANES), F32),
                        pltpu.VMEM((1, S5_LANES), F32), pltpu.VMEM((1, S5_LANES), F32)],
        sem=("arbitrary",),
    )(p, p, dya, st_re, st_im, sv_re, sv_im, wbd, cre, cim, atab, d_skip, w_glu, b_glu)


def _block_diag(w, rows_first):
    g8 = w.reshape(S5_KBLK, 8, w.shape[1], w.shape[2])
    eye = jnp.eye(8, dtype=w.dtype)
    out = jnp.einsum('kgab,fg->kfagb', g8, eye)
    return out.reshape(S5_KBLK, 8 * w.shape[1], 8 * w.shape[2])


def _block_diag_extract(wbd, a, b):
    w5 = wbd.reshape(S5_KBLK, 8, a, 8, b)
    idx = jnp.arange(8)
    return w5[:, idx, :, idx, :].transpose(1, 0, 2, 3).reshape(S5_GROUPS, a, b)


def _ret_constants():
    log_g = np.log1p(-np.exp2(-5.0 - np.arange(RET_HEADS, dtype=np.float32))).astype(np.float32)
    idx = np.arange(RET_CHUNK, dtype=np.float32)
    diff = idx[:, None] - idx[None, :]
    decay = np.where(diff >= 0, np.exp(log_g[:, None, None] * np.maximum(diff, 0.0)), 0.0).astype(np.float32)
    xi = np.exp(log_g[None, :] * (idx[:, None] + 1.0)).astype(np.float32)
    zeta = np.exp(log_g[None, :] * (RET_CHUNK - 1.0 - idx[:, None])).astype(np.float32)
    chunk_decay = np.exp(log_g * RET_CHUNK).astype(np.float32)
    return decay, xi, zeta, chunk_decay


def _rope_tables(L):
    half = RET_DK // 2
    inv = ROPE_BASE ** (-jnp.arange(half, dtype=F32) / half)
    ang = jnp.arange(L, dtype=F32)[:, None] * inv[None, :]
    return jnp.cos(ang), jnp.sin(ang)


def _rot(xh, cos, sin):
    x1, x2 = xh[:, :128], xh[:, 128:]
    return jnp.concatenate([x1 * cos - x2 * sin, x1 * sin + x2 * cos], axis=1)


def _rot_t(dh, cos, sin):
    d1, d2 = dh[:, :128], dh[:, 128:]
    return jnp.concatenate([d1 * cos + d2 * sin, d2 * cos - d1 * sin], axis=1)


def retention_forward(p, cos, sin, gain):
    L = p.shape[0]
    nc = L // RET_CHUNK
    decay_np, xi_np, zeta_np, cd_np = _ret_constants()
    decay, xi, zeta = jnp.asarray(decay_np), jnp.asarray(xi_np), jnp.asarray(zeta_np)
    scale = RET_DK ** -0.5

    def body(q_ref, k_ref, v_ref, z_ref, cos_ref, sin_ref, dec_ref, xi_ref, zeta_ref, gain_ref,
             yb_ref, prev_ref, state):
        @pl.when(pl.program_id(0) == 0)
        def _():
            state[...] = jnp.zeros_like(state)

        H = range(RET_HEADS)
        hs = [slice(RET_DK * h, RET_DK * (h + 1)) for h in H]
        cs, sn = cos_ref[...], sin_ref[...]
        sz, _ = _silu_and_grad(z_ref[...])
        qh = [_rot(q_ref[:, hs[h]], cs, sn) for h in H]
        kh = [_rot(k_ref[:, hs[h]], cs, sn) * scale for h in H]
        vh = [v_ref[:, hs[h]] for h in H]
        prev = [state[h] for h in H]
        sc = [_mm_nt(qh[h], kh[h]) * dec_ref[h] for h in H]
        cross = [_mm(qh[h] * xi_ref[:, h:h + 1], prev[h]) for h in H]
        loc = [_mm_tn(kh[h] * zeta_ref[:, h:h + 1], vh[h]) for h in H]
        o = [_mm(sc[h], vh[h]) + cross[h] for h in H]
        oc = [o[h] - jnp.mean(o[h], axis=-1, keepdims=True) for h in H]
        on = [oc[h] * lax.rsqrt(jnp.mean(oc[h] * oc[h], axis=-1, keepdims=True) + NORM_EPS) for h in H]
        for h in H:
            prev_ref[0, h] = prev[h].astype(prev_ref.dtype)
            state[h] = prev[h] * float(cd_np[h]) + loc[h]
            yb_ref[:, hs[h]] = (on[h] * gain_ref[:, hs[h]] * sz[:, hs[h]]).astype(yb_ref.dtype)

    blk = lambda c: pl.BlockSpec((RET_CHUNK, 1024), lambda i, c=c: (i, c))
    return pl.pallas_call(
        body, name="retention_forward", grid=(nc,),
        in_specs=[blk(0), blk(1), blk(2), blk(3),
                  pl.BlockSpec((RET_CHUNK, 128), lambda i: (i, 0)), pl.BlockSpec((RET_CHUNK, 128), lambda i: (i, 0)),
                  _full(decay.shape), _full(xi.shape), _full(zeta.shape), _full((1, 1024))],
        out_specs=[pl.BlockSpec((RET_CHUNK, 1024), lambda i: (i, 0)),
                   pl.BlockSpec((1, RET_HEADS, RET_DK, RET_DK), lambda i: (i, 0, 0, 0))],
        out_shape=[jax.ShapeDtypeStruct((L, 1024), MXU_DTYPE),
                   jax.ShapeDtypeStruct((nc, RET_HEADS, RET_DK, RET_DK), MXU_DTYPE)],
        scratch_shapes=[pltpu.VMEM((RET_HEADS, RET_DK, RET_DK), F32)],
        compiler_params=_cparams(("arbitrary",)),
    )(p, p, p, p, cos, sin, decay, xi, zeta, gain)


def retention_backward(p, dy, prevs, cos, sin, gain, plan=None):
    L = p.shape[0]
    nc = L // RET_CHUNK
    decay_np, xi_np, zeta_np, cd_np = _ret_constants()
    decay, xi, zeta = jnp.asarray(decay_np), jnp.asarray(xi_np), jnp.asarray(zeta_np)
    scale = RET_DK ** -0.5

    def body(q_ref, k_ref, v_ref, z_ref, dyb_ref, prev_ref, cos_ref, sin_ref, dec_ref, xi_ref, zeta_ref, gain_ref,
             dp_ref, dgain_ref, dstate):
        @pl.when(pl.program_id(0) == 0)
        def _():
            dstate[...] = jnp.zeros_like(dstate)
            dgain_ref[...] = jnp.zeros_like(dgain_ref)

        H = range(RET_HEADS)
        hs = [slice(RET_DK * h, RET_DK * (h + 1)) for h in H]
        cs, sn = cos_ref[...], sin_ref[...]
        sz, dsz = _silu_and_grad(z_ref[...])
        dyb = dyb_ref[...]
        xih = [xi_ref[:, h:h + 1] for h in H]
        zth = [zeta_ref[:, h:h + 1] for h in H]
        qh = [_rot(q_ref[:, hs[h]], cs, sn) for h in H]
        kh = [_rot(k_ref[:, hs[h]], cs, sn) * scale for h in H]
        vh = [v_ref[:, hs[h]] for h in H]
        prev = [prev_ref[0, h] for h in H]
        dst = [dstate[h] for h in H]
        qx = [qh[h] * xih[h] for h in H]
        kz = [kh[h] * zth[h] for h in H]
        sc = [_mm_nt(qh[h], kh[h]) * dec_ref[h] for h in H]
        cross = [_mm(qx[h], prev[h]) for h in H]
        dk_st = [_mm_nt(vh[h], dst[h]) for h in H]
        dv_st = [_mm(kz[h], dst[h]) for h in H]
        o = [_mm(sc[h], vh[h]) + cross[h] for h in H]
        oc = [o[h] - jnp.mean(o[h], axis=-1, keepdims=True) for h in H]
        rstd = [lax.rsqrt(jnp.mean(oc[h] * oc[h], axis=-1, keepdims=True) + NORM_EPS) for h in H]
        on = [oc[h] * rstd[h] for h in H]
        dong = [dyb[:, hs[h]] * sz[:, hs[h]] for h in H]
        don = [dong[h] * gain_ref[:, hs[h]] for h in H]
        do = [rstd[h] * (don[h] - jnp.mean(don[h], axis=-1, keepdims=True)
                         - on[h] * jnp.mean(don[h] * on[h], axis=-1, keepdims=True)) for h in H]
        dsc = [_mm_nt(do[h], vh[h]) * dec_ref[h] for h in H]
        dq_st = [_mm_nt(do[h], prev[h]) for h in H]
        dnew = [_mm_tn(qx[h], do[h]) for h in H]
        dqh = [_mm(dsc[h], kh[h]) + dq_st[h] * xih[h] for h in H]
        dkh = [_mm_tn(dsc[h], qh[h]) + dk_st[h] * zth[h] for h in H]
        dvh = [_mm_tn(sc[h], do[h]) + dv_st[h] for h in H]
        for h in H:
            dstate[h] = dst[h] * float(cd_np[h]) + dnew[h]
            dgain_ref[:, hs[h]] += jnp.sum(dong[h] * on[h], axis=0, keepdims=True)
            dp_ref[:, hs[h]] = _rot_t(dqh[h], cs, sn).astype(dp_ref.dtype)
            dp_ref[:, 1024 + RET_DK * h:1024 + RET_DK * (h + 1)] = (_rot_t(dkh[h], cs, sn) * scale).astype(dp_ref.dtype)
            dp_ref[:, 2048 + RET_DK * h:2048 + RET_DK * (h + 1)] = dvh[h].astype(dp_ref.dtype)
            dp_ref[:, 3072 + RET_DK * h:3072 + RET_DK * (h + 1)] = (
                dyb[:, hs[h]] * on[h] * gain_ref[:, hs[h]] * dsz[:, hs[h]]).astype(dp_ref.dtype)

    blk = lambda c: pl.BlockSpec((RET_CHUNK, 1024), lambda i, c=c: (nc - 1 - i, c))
    tab = pl.BlockSpec((RET_CHUNK, 128), lambda i: (nc - 1 - i, 0))
    return _call(
        body, plan, name="retention_backward", grid=(nc,),
        in_specs=[blk(0), blk(1), blk(2), blk(3), blk(0),
                  pl.BlockSpec((1, RET_HEADS, RET_DK, RET_DK), lambda i: (nc - 1 - i, 0, 0, 0)),
                  tab, tab, _full(decay.shape), _full(xi.shape), _full(zeta.shape), _full((1, 1024))],
        out_specs=[pl.BlockSpec((RET_CHUNK, 4096), lambda i: (nc - 1 - i, 0)), _full((1, 1024))],
        out_shape=[jax.ShapeDtypeStruct((L, 4096), MXU_DTYPE), jax.ShapeDtypeStruct((1, 1024), F32)],
        scratch_shapes=[pltpu.VMEM((RET_HEADS, RET_DK, RET_DK), F32)],
        sem=("arbitrary",),
    )(p, p, p, p, dy, prevs, cos, sin, decay, xi, zeta, gain)


def _sgu_mix(p_ref, gain_ref, wm_ref, bt_ref, tl):
    pu, pv, z = p_ref[:, :2048], p_ref[:, 2048:4096], p_ref[:, 4096:]
    (u, du), (v, dv) = _gelu_and_grad(pu), _gelu_and_grad(pv)
    mu = jnp.mean(v, axis=-1, keepdims=True)
    vc = v - mu
    rstd = lax.rsqrt(jnp.mean(vc * vc, axis=-1, keepdims=True) + NORM_EPS)
    vn = vc * rstd
    vg = vn * gain_ref[...]
    mask = (lax.broadcasted_iota(jnp.int32, (SGU_CHUNK, SGU_CHUNK), 0)
            >= lax.broadcasted_iota(jnp.int32, (SGU_CHUNK, SGU_CHUNK), 1))
    wms = [jnp.where(mask, wm_ref[g], 0.0) for g in range(SGU_GROUPS)]
    rows = []
    for c in range(tl // SGU_CHUNK):
        rs = slice(SGU_CHUNK * c, SGU_CHUNK * (c + 1))
        cols = []
        for g in range(SGU_GROUPS):
            gs = slice(SGU_GDIM * g, SGU_GDIM * (g + 1))
            cols.append(_mm(wms[g], vg[rs, gs]) + bt_ref[:, g:g + 1])
        rows.append(jnp.concatenate(cols, axis=1))
    s = rows[0] if len(rows) == 1 else jnp.concatenate(rows, axis=0)
    return du, dv, z, u, vn, rstd, vg, wms, mask, s


def sgu_forward(p, gain, wm, bt):
    L = p.shape[0]
    tl = min(TL_SGU, L)

    def body(p_ref, gain_ref, wm_ref, bt_ref, y_ref):
        _, _, z, u, _, _, _, _, _, s = _sgu_mix(p_ref, gain_ref, wm_ref, bt_ref, tl)
        sz, _ = _silu_and_grad(z)
        y_ref[...] = (u * s * sz).astype(y_ref.dtype)

    return pl.pallas_call(
        body, name="sgu_forward", grid=(L // tl,),
        in_specs=[pl.BlockSpec((tl, ODD_IN), lambda i: (i, 0)), _full((1, 2048)), _full(wm.shape), _full(bt.shape)],
        out_specs=pl.BlockSpec((tl, 2048), lambda i: (i, 0)),
        out_shape=jax.ShapeDtypeStruct((L, 2048), MXU_DTYPE),
        compiler_params=_cparams(("arbitrary",)),
    )(p, gain, wm, bt)


def sgu_backward(p, dy, gain, wm, bt, plan=None):
    L = p.shape[0]
    tl = min(TL_SGU, L)

    def body(p_ref, dy_ref, gain_ref, wm_ref, bt_ref, dp_ref, dgain_ref, dwm_ref, dbt_ref):
        @pl.when(pl.program_id(0) == 0)
        def _():
            dgain_ref[...] = jnp.zeros_like(dgain_ref)
            dwm_ref[...] = jnp.zeros_like(dwm_ref)
            dbt_ref[...] = jnp.zeros_like(dbt_ref)

        gu, gv, z, u, vn, rstd, vg, wms, mask, s = _sgu_mix(p_ref, gain_ref, wm_ref, bt_ref, tl)
        sz, dsz = _silu_and_grad(z)
        dyv = dy_ref[...]
        dp_ref[:, 4096:] = (dyv * u * s * dsz).astype(dp_ref.dtype)
        dsg = dyv * sz
        dp_ref[:, :2048] = (dsg * s * gu).astype(dp_ref.dtype)
        ds = dsg * u
        rows = []
        dbs = [jnp.zeros((SGU_CHUNK, 1), F32) for _ in range(SGU_GROUPS)]
        for c in range(tl // SGU_CHUNK):
            rs = slice(SGU_CHUNK * c, SGU_CHUNK * (c + 1))
            cols = []
            for g in range(SGU_GROUPS):
                gs = slice(SGU_GDIM * g, SGU_GDIM * (g + 1))
                dsg_c = ds[rs, gs]
                dbs[g] = dbs[g] + jnp.sum(dsg_c, axis=1, keepdims=True)
                dwm_ref[g] += jnp.where(mask, _mm_nt(dsg_c, vg[rs, gs]), 0.0)
                cols.append(_mm_tn(wms[g], dsg_c))
            rows.append(jnp.concatenate(cols, axis=1))
        dbt_ref[...] += jnp.concatenate(dbs, axis=1)
        dvg = rows[0] if len(rows) == 1 else jnp.concatenate(rows, axis=0)
        dgain_ref[...] += jnp.sum(dvg * vn, axis=0, keepdims=True)
        dvn = dvg * gain_ref[...]
        dv = rstd * (dvn - jnp.mean(dvn, axis=-1, keepdims=True) - vn * jnp.mean(dvn * vn, axis=-1, keepdims=True))
        dp_ref[:, 2048:4096] = (dv * gv).astype(dp_ref.dtype)

    return _call(
        body, plan, name="sgu_backward", grid=(L // tl,),
        in_specs=[pl.BlockSpec((tl, ODD_IN), lambda i: (i, 0)), pl.BlockSpec((tl, 2048), lambda i: (i, 0)),
                  _full((1, 2048)), _full(wm.shape), _full(bt.shape)],
        out_specs=[pl.BlockSpec((tl, ODD_IN), lambda i: (i, 0)), _full((1, 2048)), _full(wm.shape), _full(bt.shape)],
        out_shape=[jax.ShapeDtypeStruct((L, ODD_IN), MXU_DTYPE), jax.ShapeDtypeStruct((1, 2048), F32),
                   jax.ShapeDtypeStruct(wm.shape, F32), jax.ShapeDtypeStruct(bt.shape, F32)],
        sem=("arbitrary",),
    )(p, dy, gain, wm, bt)


def cast_shards(mats):
    n = len(mats)

    def body(*refs):
        for p in range(n):
            refs[n + p][...] = refs[p][...].astype(MXU_DTYPE)

    return pl.pallas_call(
        body, name="cast_shards", out_shape=[jax.ShapeDtypeStruct(m.shape, MXU_DTYPE) for m in mats],
        compiler_params=pltpu.CompilerParams(vmem_limit_bytes=VMEM_LIMIT),
    )(*mats)


def local_grads(x, tgt, w):
    L = x.shape[0]
    ne, gf = w["norm_even"], w["final_norm"].reshape(1, D_MODEL)
    sh = dict(zip(MATRICES, cast_shards([w[n][0] for n in MATRICES])))
    (w_in_e,) = run_plan(gather_plan([sh["w_in_even"]]), "gather_w_in_even")
    lam_re, lam_im = w["s5_lam_re"][0], w["s5_lam_im"][0]
    log_dt = w["s5_log_dt"].reshape(S5_GROUPS, 1)
    bt_re = jnp.transpose(w["s5_b_re"][0], (2, 0, 1))
    bt_im = jnp.transpose(w["s5_b_im"][0], (2, 0, 1))
    c_re, c_im = w["s5_c_re"][0], w["s5_c_im"][0]
    wm = w["sgu_w_spatial"][0]
    bt = jnp.transpose(w["sgu_b_spatial"][0])

    tl5 = min(TL_S5, L)
    ab_re, ab_im, bb_re, bb_im, at_re, at_im = s5_params_fwd(lam_re, lam_im, log_dt, bt_re, bt_im, tl5 // 8)
    atab = jnp.stack([ab_re.reshape(S5_LANES), ab_im.reshape(S5_LANES),
                      at_re.reshape(S5_LANES), at_im.reshape(S5_LANES)])
    wbd = jnp.concatenate([_block_diag(jnp.transpose(bb_re, (1, 0, 2)), True),
                           _block_diag(jnp.transpose(bb_im, (1, 0, 2)), True)], axis=2).astype(MXU_DTYPE)
    cre = _block_diag(jnp.transpose(c_re, (0, 2, 1)), True).astype(MXU_DTYPE)
    cim = _block_diag(jnp.transpose(c_im, (0, 2, 1)), True).astype(MXU_DTYPE)
    cos, sin = _rope_tables(L)

    s5_cols = 2 * S5_WIDTH
    w_s5 = jnp.concatenate([w_in_e[0], w_in_e[1][:, :s5_cols - EVEN_IN // N_CHIPS]], axis=1)
    w_ret = jnp.concatenate([w_in_e[1][:, s5_cols - EVEN_IN // N_CHIPS:], w_in_e[2], w_in_e[3]], axis=1)
    (p1a, h0s), (w_glu,) = norm_matmul(stream_order(x, tl5), ne, w_s5, "even_in_s5",
                                       gather_plan([sh["s5_w_glu"]]), tn=1024)
    (p1b, h0), (w_out_e,) = norm_matmul(x, ne, w_ret, "even_in_ret", gather_plan([sh["w_out_even"]]), tn=1024)
    w_glu = w_glu.reshape(S5_WIDTH, S5_WIDTH)
    w_out_e = w_out_e.reshape(2 * S5_WIDTH, D_MODEL)
    (ya, st_re, st_im, sv_re, sv_im), (w_in_o, w_out_o, no, sg_gain) = s5_forward(
        p1a, wbd, cre, cim, atab, w["s5_d"], w_glu, w["s5_b_glu"],
        gather_plan([sh["w_in_odd"], sh["w_out_odd"], w["norm_odd"], w["sgu_norm_gain"]]))
    w_out_o = w_out_o.reshape(SGU_WIDTH, D_MODEL)
    no, sg_gain = no.reshape(1, D_MODEL), sg_gain.reshape(1, SGU_WIDTH)
    yb, prevs = retention_forward(p1b, cos, sin, w["ret_gn_gain"])
    ya = token_order(ya, tl5)
    x1 = matmul_residual([ya, yb], w_out_e, x, "even_out")
    (p2, h1), _ = norm_matmul(x1, no, w_in_o, "odd_in")
    y2 = sgu_forward(p2, sg_gain, wm, bt)
    dx2, loss, dgf = out_proj_loss(y2, w_out_o, x1, gf, tgt, "odd_out_loss")

    g, landed = {}, {}
    shard_major = lambda a, n: a.reshape((N_CHIPS,) + w[n].shape[1:])
    dy2, g_w_out_o = out_proj_bwd(dx2, w_out_o, [y2], "odd_out_bwd")
    (dp2, g["sgu_norm_gain"], dwm, dbt), (landed["w_out_odd"],) = sgu_backward(
        p2, dy2, sg_gain, wm, bt, reduce_plan([shard_major(g_w_out_o, "w_out_odd")]))
    g_w_in_o = in_proj_bwd_dw(h1, dp2, "odd_in_dw", ODD_IN // N_CHIPS)
    (dx1, g["norm_odd"]), _ = in_proj_bwd_dx(x1, no, [dp2], [w_in_o], dx2, "odd_in_dx")
    dya, dyb, g_w_out_e = out_proj_bwd(dx1, w_out_e, [ya, yb], "even_out_bwd")
    ((dpa, dwbd, dcre, dcim, dab_re, dab_im, g["s5_d"], g_w_glu, g["s5_b_glu"]),
     (landed["w_in_odd"], landed["w_out_even"])) = s5_backward(
        p1a, stream_order(dya, tl5), st_re, st_im, sv_re, sv_im, wbd, cre, cim, atab, w["s5_d"], w_glu,
        w["s5_b_glu"], reduce_plan([g_w_in_o, shard_major(g_w_out_e, "w_out_even")]))

    dbb_re = jnp.transpose(_block_diag_extract(dwbd[:, :, :512], S5_GROUP, S5_STATE), (1, 0, 2))
    dbb_im = jnp.transpose(_block_diag_extract(dwbd[:, :, 512:], S5_GROUP, S5_STATE), (1, 0, 2))
    dlr, dli, ddt, dbt_re, dbt_im = s5_params_bwd(
        lam_re, lam_im, log_dt, bt_re, bt_im, dab_re.reshape(8, S5_GROUPS, S5_STATE),
        dab_im.reshape(8, S5_GROUPS, S5_STATE), dbb_re, dbb_im)
    g["s5_lam_re"], g["s5_lam_im"] = dlr[None], dli[None]
    g["s5_log_dt"] = ddt.reshape(1, S5_GROUPS)
    g["s5_b_re"], g["s5_b_im"] = dbt_re, dbt_im
    g["s5_c_re"] = _block_diag_extract(dcre, S5_GROUP, S5_STATE)[None]
    g["s5_c_im"] = _block_diag_extract(dcim, S5_GROUP, S5_STATE)[None]
    g["sgu_w_spatial"] = dwm[None]
    g["sgu_b_spatial"] = jnp.transpose(dbt)[None]
    g["final_norm"] = dgf.reshape(D_MODEL)
    g["loss"] = loss

    early = BEHIND_RETENTION_BWD + ("loss",)
    (dpb, g["ret_gn_gain"]), recv = retention_backward(
        p1b, dyb, prevs, cos, sin, w["ret_gn_gain"],
        reduce_plan([shard_major(g_w_glu, "s5_w_glu")], [g[n] for n in early]))
    landed.update(zip(("s5_w_glu",) + early, recv))
    g_w_in_e = in_proj_bwd_dw(h0s, dpa, "even_in_dw_s5", 512, dtype=MXU_DTYPE)
    g_w_in_e = in_proj_bwd_dw(h0, dpb, "even_in_dw_ret", 512, first=s5_cols // 512, into=g_w_in_e, dtype=MXU_DTYPE)
    (dx0, g["norm_even"]), recv = in_proj_bwd_dx(
        x, ne, [token_order(dpa, tl5), dpb], [w_s5, w_ret], dx1, "even_in_dx",
        reduce_plan([g_w_in_e], [g[n] for n in BEHIND_EVEN_IN_DX]))
    landed.update(zip(("w_in_even",) + BEHIND_EVEN_IN_DX, recv))
    (landed["norm_even"],) = run_plan(reduce_plan([], [g["norm_even"]]), "exchange_norm_even")
    return dx0, landed


def sibling_exchange(arrs):
    n = len(arrs)

    def body(*refs):
        in_refs, out_refs = refs[:n], refs[n:2 * n]
        send_sems, recv_sems = refs[2 * n:]
        x, y, c = _place()
        copies = [pltpu.make_async_remote_copy(
            src_ref=in_refs[p], dst_ref=out_refs[p], send_sem=send_sems.at[p], recv_sem=recv_sems.at[p],
            device_id=(x, y, 1 - c), device_id_type=MESH) for p in range(n)]
        for cp in copies:
            cp.start()
        for cp in copies:
            cp.wait_recv()
        for cp in copies:
            cp.wait_send()

    return pl.pallas_call(
        body, name="sibling_exchange", in_specs=[ANY] * n, out_specs=[ANY] * n,
        out_shape=[jax.ShapeDtypeStruct(a.shape, a.dtype) for a in arrs],
        scratch_shapes=[pltpu.SemaphoreType.DMA((n,)), pltpu.SemaphoreType.DMA((n,))],
    )(*arrs)


def _row_block(rows):
    return 128 if rows % 128 == 0 else rows


def sum_slabs(r, name):
    _, R, C = r.shape
    tr = _row_block(R)

    def body(r_ref, o_ref):
        a, b, c, d = (r_ref[k].astype(F32) for k in range(N_CHIPS))
        o_ref[...] = (a + b) + (c + d)

    return pl.pallas_call(
        body, name=name, grid=(R // tr,),
        in_specs=[pl.BlockSpec((N_CHIPS, tr, C), lambda i: (0, i, 0))],
        out_specs=pl.BlockSpec((tr, C), lambda i: (i, 0)),
        out_shape=jax.ShapeDtypeStruct((R, C), F32),
        compiler_params=_cparams(("arbitrary",)),
    )(r)


def _adam(w, m, v, g):
    mn = ADAM_B1 * m + (1.0 - ADAM_B1) * g
    vn = ADAM_B2 * v + (1.0 - ADAM_B2) * (g * g)
    m_hat = mn / (1.0 - ADAM_B1 ** ADAM_STEP)
    v_hat = vn / (1.0 - ADAM_B2 ** ADAM_STEP)
    return -ADAM_LR * (m_hat / (jnp.sqrt(v_hat) + ADAM_EPS) + ADAM_WD * w), mn, vn


def adam_update(w, m, v, ga, gb, name):
    R, C = w.shape
    tr = _row_block(R)

    def body(w_ref, m_ref, v_ref, ga_ref, gb_ref, g_out, d_out, m_out, v_out):
        g = ga_ref[...] + gb_ref[...]
        g_out[...] = g
        d_out[...], m_out[...], v_out[...] = _adam(w_ref[...], m_ref[...], v_ref[...], g)

    blk = pl.BlockSpec((tr, C), lambda i: (i, 0))
    return pl.pallas_call(
        body, name=name, grid=(R // tr,),
        in_specs=[blk] * 5, out_specs=[blk] * 4,
        out_shape=[jax.ShapeDtypeStruct((R, C), F32)] * 4,
        compiler_params=_cparams(("arbitrary",)),
    )(w, m, v, ga, gb)


WIDE_ROWS = ("s5_b_re", "s5_b_im")


def sum_small(landed):
    def body(*refs):
        k = len(refs) // 2
        for i in range(k):
            r = refs[i]
            refs[k + i][...] = (r[0] + r[1]) + (r[2] + r[3])

    names = list(landed)
    res = pl.pallas_call(
        body, name="sum_small", out_shape=[jax.ShapeDtypeStruct(landed[n].shape[1:], F32) for n in names],
        compiler_params=pltpu.CompilerParams(vmem_limit_bytes=VMEM_LIMIT),
    )(*[landed[n] for n in names])
    return dict(zip(names, res))


def adam_small(names, w, m, v, ga, gb):
    def body(*refs):
        k = len(refs) // 9
        me = 2 * lax.axis_index("x") + lax.axis_index("y")
        for i in range(k):
            w_ref, m_ref, v_ref, ga_ref, gb_ref = refs[i], refs[k + i], refs[2 * k + i], refs[3 * k + i], refs[4 * k + i]
            size = w_ref.shape[-1]
            if ga_ref.shape != w_ref.shape:
                part = pl.ds(pl.multiple_of(me * size, LANES), size)
                g = ga_ref[:, part] + gb_ref[:, part]
            else:
                g = ga_ref[...] + gb_ref[...]
            refs[5 * k + i][...] = g
            refs[6 * k + i][...], refs[7 * k + i][...], refs[8 * k + i][...] = _adam(w_ref[...], m_ref[...], v_ref[...], g)

    ins = [d[n] for d in (w, m, v, ga, gb) for n in names]
    outs = [jax.ShapeDtypeStruct(w[n].shape, F32) for _ in range(4) for n in names]
    res = pl.pallas_call(body, name="adam_small", out_shape=outs,
                         compiler_params=pltpu.CompilerParams(vmem_limit_bytes=VMEM_LIMIT))(*ins)
    k = len(names)
    return [dict(zip(names, res[j * k:(j + 1) * k])) for j in range(4)]


WEIGHTS = ("norm_even", "w_in_even", "s5_lam_re", "s5_lam_im", "s5_log_dt", "s5_b_re", "s5_b_im", "s5_c_re",
           "s5_c_im", "s5_d", "s5_w_glu", "s5_b_glu", "ret_gn_gain", "w_out_even", "norm_odd", "w_in_odd",
           "sgu_norm_gain", "sgu_w_spatial", "sgu_b_spatial", "w_out_odd", "final_norm")
MATRICES = ("w_in_even", "s5_w_glu", "w_out_even", "w_in_odd", "w_out_odd")
SHARDED_VECS = ("norm_odd", "sgu_norm_gain")
REPLICATED = tuple(n for n in WEIGHTS if n not in MATRICES and n not in SHARDED_VECS)
SMALL = tuple(n for n in WEIGHTS if n not in MATRICES)
BEHIND_RETENTION_BWD = tuple(n for n in SMALL if n not in ("ret_gn_gain", "norm_even"))
BEHIND_EVEN_IN_DX = ("ret_gn_gain",)
LANES = 128


def kernel(x, norm_even, w_in_even, s5_lam_re, s5_lam_im, s5_log_dt, s5_b_re, s5_b_im, s5_c_re, s5_c_im, s5_d, s5_w_glu, s5_b_glu, ret_gn_gain, w_out_even, norm_odd, w_in_odd, sgu_norm_gain, sgu_w_spatial, sgu_b_spatial, w_out_odd, final_norm, loss_target, m_norm_even, m_w_in_even, m_s5_lam_re, m_s5_lam_im, m_s5_log_dt, m_s5_b_re, m_s5_b_im, m_s5_c_re, m_s5_c_im, m_s5_d, m_s5_w_glu, m_s5_b_glu, m_ret_gn_gain, m_w_out_even, m_norm_odd, m_w_in_odd, m_sgu_norm_gain, m_sgu_w_spatial, m_sgu_b_spatial, m_w_out_odd, m_final_norm, v_norm_even, v_w_in_even, v_s5_lam_re, v_s5_lam_im, v_s5_log_dt, v_s5_b_re, v_s5_b_im, v_s5_c_re, v_s5_c_im, v_s5_d, v_s5_w_glu, v_s5_b_glu, v_ret_gn_gain, v_w_out_even, v_norm_odd, v_w_in_odd, v_sgu_norm_gain, v_sgu_w_spatial, v_sgu_b_spatial, v_w_out_odd, v_final_norm):
    w = dict(norm_even=norm_even, w_in_even=w_in_even, s5_lam_re=s5_lam_re, s5_lam_im=s5_lam_im, s5_log_dt=s5_log_dt, s5_b_re=s5_b_re, s5_b_im=s5_b_im, s5_c_re=s5_c_re, s5_c_im=s5_c_im, s5_d=s5_d, s5_w_glu=s5_w_glu, s5_b_glu=s5_b_glu, ret_gn_gain=ret_gn_gain, w_out_even=w_out_even, norm_odd=norm_odd, w_in_odd=w_in_odd, sgu_norm_gain=sgu_norm_gain, sgu_w_spatial=sgu_w_spatial, sgu_b_spatial=sgu_b_spatial, w_out_odd=w_out_odd, final_norm=final_norm)
    m = dict(norm_even=m_norm_even, w_in_even=m_w_in_even, s5_lam_re=m_s5_lam_re, s5_lam_im=m_s5_lam_im, s5_log_dt=m_s5_log_dt, s5_b_re=m_s5_b_re, s5_b_im=m_s5_b_im, s5_c_re=m_s5_c_re, s5_c_im=m_s5_c_im, s5_d=m_s5_d, s5_w_glu=m_s5_w_glu, s5_b_glu=m_s5_b_glu, ret_gn_gain=m_ret_gn_gain, w_out_even=m_w_out_even, norm_odd=m_norm_odd, w_in_odd=m_w_in_odd, sgu_norm_gain=m_sgu_norm_gain, sgu_w_spatial=m_sgu_w_spatial, sgu_b_spatial=m_sgu_b_spatial, w_out_odd=m_w_out_odd, final_norm=m_final_norm)
    v = dict(norm_even=v_norm_even, w_in_even=v_w_in_even, s5_lam_re=v_s5_lam_re, s5_lam_im=v_s5_lam_im, s5_log_dt=v_s5_log_dt, s5_b_re=v_s5_b_re, s5_b_im=v_s5_b_im, s5_c_re=v_s5_c_re, s5_c_im=v_s5_c_im, s5_d=v_s5_d, s5_w_glu=v_s5_w_glu, s5_b_glu=v_s5_b_glu, ret_gn_gain=v_ret_gn_gain, w_out_even=v_w_out_even, norm_odd=v_norm_odd, w_in_odd=v_w_in_odd, sgu_norm_gain=v_sgu_norm_gain, sgu_w_spatial=v_sgu_w_spatial, sgu_b_spatial=v_sgu_b_spatial, w_out_odd=v_w_out_odd, final_norm=v_final_norm)

    grad_x, landed = local_grads(x[0], loss_target[0], w)

    small = SMALL + ("loss",)
    part = {n: sum_slabs(landed[n], "sum_" + n) for n in MATRICES}
    part.update(sum_small({n: landed[n] for n in small}))
    names = MATRICES + small
    other = dict(zip(names, sibling_exchange([part[n] for n in names])))

    wt, mt, vt = dict(w), dict(m), dict(v)
    for n in WIDE_ROWS:
        wt[n], mt[n], vt[n] = (jnp.transpose(a[n][0], (2, 0, 1)) for a in (w, m, v))
    out_g, out_d, out_m, out_v = adam_small(SMALL, wt, mt, vt, part, other)
    for n in WIDE_ROWS:
        for out in (out_g, out_d, out_m, out_v):
            out[n] = jnp.transpose(out[n], (1, 2, 0))[None]
    for n in MATRICES:
        res = adam_update(w[n][0], m[n][0], v[n][0], part[n], other[n], "adam_" + n)
        out_g[n], out_d[n], out_m[n], out_v[n] = (r[None] for r in res)
    total_loss = (part["loss"] + other["loss"])[0, 0]

    return (total_loss, grad_x[None], *[out_g[n] for n in WEIGHTS], *[out_d[n] for n in WEIGHTS],
            *[out_m[n] for n in WEIGHTS], *[out_v[n] for n in WEIGHTS])
```

```python
import functools
import math

import numpy as np
import jax
import jax.numpy as jnp
from jax import lax
from jax.experimental import pallas as pl
from jax.experimental.pallas import tpu as pltpu

F32 = jnp.float32
MXU_DTYPE = jnp.bfloat16
NORM_EPS = 1e-6
D_MODEL = 1024
S5_WIDTH = 1024
S5_GROUP = 16
S5_GROUPS = 64
S5_STATE = 64
S5_LANES = S5_GROUPS * S5_STATE
S5_KBLK = 8
RET_HEADS = 4
RET_DK = 256
RET_CHUNK = 128
ROPE_BASE = 10000.0
SGU_WIDTH = 2048
SGU_GROUPS = 4
SGU_GDIM = 512
SGU_CHUNK = 128
EVEN_IN = 6144
ODD_IN = 6144
ADAM_LR = 0.001
ADAM_B1 = 0.9
ADAM_B2 = 0.999
ADAM_EPS = 1e-08
ADAM_WD = 0.01
ADAM_STEP = 10
N_CHIPS = 4
VMEM_LIMIT = 56 * 1024 * 1024

TL_PROJ = 512
TL_DW = 1024
TL_S5 = 256
TL_SGU = 256


def _cparams(sem, **kw):
    return pltpu.CompilerParams(dimension_semantics=sem, vmem_limit_bytes=VMEM_LIMIT, **kw)


def _mm(a, b):
    return jnp.dot(a.astype(MXU_DTYPE), b.astype(MXU_DTYPE), preferred_element_type=F32)


def _mm_nt(a, b):
    return lax.dot_general(a.astype(MXU_DTYPE), b.astype(MXU_DTYPE),
                           (((1,), (1,)), ((), ())), preferred_element_type=F32)


def _mm_tn(a, b):
    return lax.dot_general(a.astype(MXU_DTYPE), b.astype(MXU_DTYPE),
                           (((0,), (0,)), ((), ())), preferred_element_type=F32)


_GELU_C = math.sqrt(2.0 / math.pi)


def _gelu_parts(x):
    x2 = x * x
    th = jnp.tanh(x * (_GELU_C + (_GELU_C * 0.044715) * x2))
    hx = 0.5 * x
    return hx + hx * th, th, x2, hx


def _gelu(x):
    return _gelu_parts(x)[0]


def _gelu_and_grad(x):
    g, th, x2, hx = _gelu_parts(x)
    return g, (0.5 + 0.5 * th) + hx * (1.0 - th * th) * (_GELU_C + (3.0 * _GELU_C * 0.044715) * x2)


def _gelu_grad(x):
    return _gelu_and_grad(x)[1]


def _sigmoid(x):
    return 1.0 / (1.0 + jnp.exp(-x))


def _silu_and_grad(x):
    s = _sigmoid(x)
    return x * s, s * (1.0 + x * (1.0 - s))


def _rms(x):
    return lax.rsqrt(jnp.mean(x * x, axis=-1, keepdims=True) + NORM_EPS)


def _full(shape):
    nd = len(shape)
    return pl.BlockSpec(shape, lambda *_: (0,) * nd)


MESH = pl.DeviceIdType.MESH
ANY = pl.BlockSpec(memory_space=pl.ANY)


def _place():
    return lax.axis_index("x"), lax.axis_index("y"), lax.axis_index("c")


def _chip_peer(x, y, c, d):
    return (1 - x if d >= 2 else x, 1 - y if d % 2 else y, c)


class _Plan:
    def __init__(self, inputs, out_shape, build):
        self.inputs, self.out_shape, self._build = list(inputs), list(out_shape), build
        n = len(self.inputs)
        self.sems = [pltpu.SemaphoreType.DMA((n, 3)), pltpu.SemaphoreType.DMA((n, 3)), pltpu.SemaphoreType.DMA((n,))]

    def start(self, in_refs, out_refs, sems):
        send, recv, local = self._build(in_refs, out_refs, sems)
        for p in range(len(self.inputs)):
            local[p].start()
            for cp in send[p]:
                cp.start()

    def wait(self, in_refs, out_refs, sems):
        send, recv, local = self._build(in_refs, out_refs, sems)
        for p in range(len(self.inputs)):
            for cp in recv[p]:
                cp.wait_recv()
        for p in range(len(self.inputs)):
            for cp in send[p]:
                cp.wait_send()
            local[p].wait()


class _GatherPlan:
    def __init__(self, shards):
        self.inputs = list(shards)
        self.out_shape = [jax.ShapeDtypeStruct((N_CHIPS,) + s.shape, s.dtype) for s in shards]
        n = len(shards)
        self.halved = [s.shape[0] % 32 == 0 for s in shards]
        self.sems = [pltpu.SemaphoreType.DMA((n, 3)) for _ in range(4)] + [pltpu.SemaphoreType.DMA((n,))]

    def _copies(self, in_refs, out_refs, sems):
        ici_s, ici_r, d2d_s, d2d_r, loc = sems
        x, y, c = _place()
        me = 2 * x + y

        def rows(p, core):
            if not self.halved[p]:
                return slice(None)
            half = self.inputs[p].shape[0] // 2
            return pl.ds(pl.multiple_of(core * half, 16), half)

        def ici(p, d, slab, core, src=None):
            dst = out_refs[p].at[slab, rows(p, core)]
            return pltpu.make_async_remote_copy(
                src_ref=in_refs[p].at[rows(p, core)] if src is None else src, dst_ref=dst,
                send_sem=ici_s.at[p, d - 1], recv_sem=ici_r.at[p, d - 1],
                device_id=_chip_peer(x, y, c, d), device_id_type=MESH)

        def d2d(p, d, core):
            part = out_refs[p].at[me ^ d, rows(p, core)]
            return pltpu.make_async_remote_copy(
                src_ref=part, dst_ref=part, send_sem=d2d_s.at[p, d - 1], recv_sem=d2d_r.at[p, d - 1],
                device_id=(x, y, 1 - c), device_id_type=MESH)

        local = [pltpu.make_async_copy(in_refs[p], out_refs[p].at[me], loc.at[p]) for p in range(len(self.inputs))]
        return me, c, ici, d2d, local

    def start(self, in_refs, out_refs, sems):
        me, c, ici, d2d, local = self._copies(in_refs, out_refs, sems)
        for p in range(len(self.inputs)):
            local[p].start()
            for d in (1, 2, 3):
                ici(p, d, me, c).start()

    def wait(self, in_refs, out_refs, sems):
        me, c, ici, d2d, local = self._copies(in_refs, out_refs, sems)
        n = len(self.inputs)
        for p in range(n):
            for d in (1, 2, 3):
                ici(p, d, me ^ d, c).wait_recv()
                if self.halved[p]:
                    d2d(p, d, c).start()
        for p in range(n):
            for d in (1, 2, 3):
                if self.halved[p]:
                    d2d(p, d, 1 - c).wait_recv()
                    d2d(p, d, c).wait_send()
                ici(p, d, me, c).wait_send()
            local[p].wait()


def gather_plan(shards):
    return _GatherPlan(shards)


def reduce_plan(shards, whole=()):
    n_s = len(shards)

    def build(in_refs, out_refs, sems):
        send_sems, recv_sems, loc_sems = sems
        x, y, c = _place()
        me = 2 * x + y

        def src(p, slab):
            return in_refs[p].at[slab] if p < n_s else in_refs[p]

        def remote(p, d):
            return pltpu.make_async_remote_copy(
                src_ref=src(p, me ^ d), dst_ref=out_refs[p].at[d], send_sem=send_sems.at[p, d - 1],
                recv_sem=recv_sems.at[p, d - 1], device_id=_chip_peer(x, y, c, d), device_id_type=MESH)

        n = len(in_refs)
        send = [[remote(p, d) for d in (1, 2, 3)] for p in range(n)]
        local = [pltpu.make_async_copy(src(p, me), out_refs[p].at[0], loc_sems.at[p]) for p in range(n)]
        return send, send, local

    outs = [jax.ShapeDtypeStruct(s.shape, s.dtype) for s in shards]
    outs += [jax.ShapeDtypeStruct((N_CHIPS,) + a.shape, a.dtype) for a in whole]
    return _Plan(list(shards) + list(whole), outs, build)


def run_plan(plan, name):
    n = len(plan.inputs)

    def body(*refs):
        plan.start(refs[:n], refs[n:2 * n], refs[2 * n:])
        plan.wait(refs[:n], refs[n:2 * n], refs[2 * n:])

    return pl.pallas_call(body, name=name, in_specs=[ANY] * n, out_specs=[ANY] * n, out_shape=plan.out_shape,
                          scratch_shapes=plan.sems)(*plan.inputs)


def _call(body, plan, *, name, grid, in_specs, out_specs, out_shape, sem, scratch_shapes=()):
    single = not isinstance(out_shape, (list, tuple))
    out_specs = [out_specs] if single else list(out_specs)
    out_shape = [out_shape] if single else list(out_shape)
    n_in, n_out, n_scr = len(in_specs), len(out_specs), len(scratch_shapes)
    ci = 0 if plan is None else len(plan.inputs)

    def hosted(*refs):
        ins, cins = refs[:n_in], refs[n_in:n_in + ci]
        k = n_in + ci
        outs, couts = refs[k:k + n_out], refs[k + n_out:k + n_out + ci]
        k += n_out + ci
        scr, sems = refs[k:k + n_scr], refs[k + n_scr:]
        ids = [pl.program_id(a) for a in range(len(grid))]
        first = functools.reduce(jnp.logical_and, [i == 0 for i in ids])
        last = functools.reduce(jnp.logical_and, [i == g - 1 for i, g in zip(ids, grid)])

        @pl.when(first)
        def _():
            plan.start(cins, couts, sems)

        body(*ins, *outs, *scr)

        @pl.when(last)
        def _():
            plan.wait(cins, couts, sems)

    def run(*args):
        if plan is None:
            res = pl.pallas_call(body, name=name, grid=grid, in_specs=list(in_specs), out_specs=out_specs,
                                 out_shape=out_shape, scratch_shapes=list(scratch_shapes),
                                 compiler_params=_cparams(sem))(*args)
            return (res[0] if single else res), []
        res = pl.pallas_call(hosted, name=name, grid=grid, in_specs=list(in_specs) + [ANY] * ci,
                             out_specs=out_specs + [ANY] * ci, out_shape=out_shape + plan.out_shape,
                             scratch_shapes=list(scratch_shapes) + plan.sems,
                             compiler_params=_cparams(sem))(*args, *plan.inputs)
        return (res[0] if single else res[:n_out]), list(res[n_out:])

    return run


def norm_matmul(x, g, w, name, plan=None, tn=None):
    L, D = x.shape
    tl = min(TL_DW, L)
    if w.ndim == 3:
        nt, _, tn = w.shape
        w_spec = pl.BlockSpec((1, D, tn), lambda i, n: (n, 0, 0))
    else:
        nt = w.shape[1] // tn
        w_spec = pl.BlockSpec((D, tn), lambda i, n: (0, n))

    def body(x_ref, g_ref, w_ref, o_ref, h_ref):
        xv = x_ref[...]
        h = (xv * _rms(xv) * g_ref[...]).astype(h_ref.dtype)
        h_ref[...] = h
        o_ref[...] = _mm(h, w_ref[0] if w.ndim == 3 else w_ref[...])

    return _call(
        body, plan, name=name, grid=(L // tl, nt),
        in_specs=[pl.BlockSpec((tl, D), lambda i, n: (i, 0)), _full((1, D)), w_spec],
        out_specs=[pl.BlockSpec((tl, tn), lambda i, n: (i, n)), pl.BlockSpec((tl, D), lambda i, n: (i, 0))],
        out_shape=[jax.ShapeDtypeStruct((L, nt * tn), F32), jax.ShapeDtypeStruct((L, D), MXU_DTYPE)],
        sem=("arbitrary", "arbitrary"),
    )(x, g, w)


def matmul_residual(ys, w, x, name):
    L, D = x.shape
    tl = min(TL_PROJ, L)
    n = len(ys)
    offs = np.cumsum([0] + [y.shape[1] for y in ys])

    def body(*refs):
        y_refs, w_ref, x_ref, o_ref = refs[:n], refs[n], refs[n + 1], refs[n + 2]
        acc = x_ref[...]
        for k in range(n):
            acc = acc + _mm(y_refs[k][...], w_ref[offs[k]:offs[k + 1], :])
        o_ref[...] = acc

    return pl.pallas_call(
        body, name=name, grid=(L // tl,),
        in_specs=[pl.BlockSpec((tl, y.shape[1]), lambda i: (i, 0)) for y in ys]
        + [_full(w.shape), pl.BlockSpec((tl, D), lambda i: (i, 0))],
        out_specs=pl.BlockSpec((tl, D), lambda i: (i, 0)),
        out_shape=jax.ShapeDtypeStruct((L, D), F32),
        compiler_params=_cparams(("arbitrary",)),
    )(*ys, w, x)


def out_proj_loss(y, w, x, gf, tgt, name):
    L, K = y.shape
    D = w.shape[1]
    tl = min(TL_PROJ, L)

    def body(y_ref, w_ref, x_ref, gf_ref, t_ref, dx_ref, loss_ref, dg_ref):
        @pl.when(pl.program_id(0) == 0)
        def _():
            loss_ref[...] = jnp.zeros_like(loss_ref)
            dg_ref[...] = jnp.zeros_like(dg_ref)

        x2 = x_ref[...] + _mm(y_ref[...], w_ref[...])
        r = _rms(x2)
        xn = x2 * r
        e = xn * gf_ref[...] - t_ref[...]
        loss_ref[...] += (0.5 / D) * jnp.sum(e * e)
        dout = e * (1.0 / D)
        dg_ref[...] += jnp.sum(dout * xn, axis=0, keepdims=True)
        dxn = dout * gf_ref[...]
        dx_ref[...] = r * (dxn - xn * jnp.mean(dxn * xn, axis=-1, keepdims=True))

    return pl.pallas_call(
        body, name=name, grid=(L // tl,),
        in_specs=[pl.BlockSpec((tl, K), lambda i: (i, 0)), _full((K, D)),
                  pl.BlockSpec((tl, D), lambda i: (i, 0)), _full((1, D)),
                  pl.BlockSpec((tl, D), lambda i: (i, 0))],
        out_specs=[pl.BlockSpec((tl, D), lambda i: (i, 0)), _full((8, 128)), _full((1, D))],
        out_shape=[jax.ShapeDtypeStruct((L, D), F32), jax.ShapeDtypeStruct((8, 128), F32),
                   jax.ShapeDtypeStruct((1, D), F32)],
        compiler_params=_cparams(("arbitrary",)),
    )(y, w, x, gf, tgt)


def out_proj_bwd(dx, w, ys, name):
    L, D = dx.shape
    K = w.shape[0]
    tl = min(TL_PROJ, L)
    n = len(ys)
    offs = np.cumsum([0] + [y.shape[1] for y in ys])

    def body(*refs):
        dx_ref, w_ref, y_refs = refs[0], refs[1], refs[2:2 + n]
        dy_refs, dw_ref = refs[2 + n:2 + 2 * n], refs[2 + 2 * n]

        @pl.when(pl.program_id(0) == 0)
        def _():
            dw_ref[...] = jnp.zeros_like(dw_ref)

        dxv = dx_ref[...]
        for k in range(n):
            dy_refs[k][...] = _mm_nt(dxv, w_ref[offs[k]:offs[k + 1], :])
            dw_ref[offs[k]:offs[k + 1], :] += _mm_tn(y_refs[k][...], dxv)

    y_specs = [pl.BlockSpec((tl, y.shape[1]), lambda i: (i, 0)) for y in ys]
    return pl.pallas_call(
        body, name=name, grid=(L // tl,),
        in_specs=[pl.BlockSpec((tl, D), lambda i: (i, 0)), _full((K, D))] + y_specs,
        out_specs=y_specs + [_full((K, D))],
        out_shape=[jax.ShapeDtypeStruct(y.shape, F32) for y in ys] + [jax.ShapeDtypeStruct((K, D), F32)],
        compiler_params=_cparams(("arbitrary",)),
    )(dx, w, *ys)


def in_proj_bwd_dx(x, g, dps, ws, dres, name, plan=None):
    L, D = x.shape
    tl = min(TL_PROJ, L)
    n = len(dps)

    def body(*refs):
        x_ref, g_ref, dres_ref = refs[:3]
        dp_refs, w_refs = refs[3:3 + n], refs[3 + n:3 + 2 * n]
        dx_ref, dg_ref = refs[3 + 2 * n:]

        @pl.when(pl.program_id(0) == 0)
        def _():
            dg_ref[...] = jnp.zeros_like(dg_ref)

        dh = None
        for dp_ref, w_ref, w in zip(dp_refs, w_refs, ws):
            if w.ndim == 3:
                tn = w.shape[2]
                parts = [_mm_nt(dp_ref[:, tn * k:tn * (k + 1)], w_ref[k]) for k in range(w.shape[0])]
            else:
                parts = [_mm_nt(dp_ref[...], w_ref[...])]
            for part in parts:
                dh = part if dh is None else dh + part
        xv = x_ref[...]
        r = _rms(xv)
        xn = xv * r
        dg_ref[...] += jnp.sum(dh * xn, axis=0, keepdims=True)
        dxn = dh * g_ref[...]
        dx_ref[...] = dres_ref[...] + r * (dxn - xn * jnp.mean(dxn * xn, axis=-1, keepdims=True))

    return _call(
        body, plan, name=name, grid=(L // tl,),
        in_specs=[pl.BlockSpec((tl, D), lambda i: (i, 0)), _full((1, D)), pl.BlockSpec((tl, D), lambda i: (i, 0))]
        + [pl.BlockSpec((tl, dp.shape[1]), lambda i: (i, 0)) for dp in dps] + [_full(w.shape) for w in ws],
        out_specs=[pl.BlockSpec((tl, D), lambda i: (i, 0)), _full((1, D))],
        out_shape=[jax.ShapeDtypeStruct((L, D), F32), jax.ShapeDtypeStruct((1, D), F32)],
        sem=("arbitrary",),
    )(x, g, dres, *dps, *ws)


def in_proj_bwd_dw(h, dp, name, tn, first=0, into=None, dtype=F32):
    L, D = h.shape
    tl = min(TL_DW, L)
    wb = EVEN_IN // N_CHIPS
    per = wb // tn
    count = dp.shape[1] // tn
    last = L // tl - 1

    def body(*refs):
        h_ref, dp_ref, dw_ref, acc = refs[0], refs[1], refs[-2], refs[-1]

        @pl.when(pl.program_id(1) == 0)
        def _():
            acc[...] = jnp.zeros_like(acc)

        acc[...] += _mm_tn(h_ref[...], dp_ref[...])

        @pl.when(pl.program_id(1) == last)
        def _():
            dw_ref[0] = acc[...].astype(dw_ref.dtype)

    ins = [h, dp] + ([] if into is None else [into])
    return pl.pallas_call(
        body, name=name, grid=(count, L // tl),
        in_specs=[pl.BlockSpec((tl, D), lambda n, i: (i, 0)), pl.BlockSpec((tl, tn), lambda n, i: (i, n))]
        + ([] if into is None else [ANY]),
        out_specs=pl.BlockSpec((1, D, tn), lambda n, i: ((n + first) // per, 0, (n + first) % per)),
        out_shape=jax.ShapeDtypeStruct((N_CHIPS, D, wb), dtype),
        scratch_shapes=[pltpu.VMEM((D, tn), F32)],
        input_output_aliases={} if into is None else {2: 0},
        compiler_params=_cparams(("arbitrary", "arbitrary")),
    )(*ins)


def _s5_param_fn(lam_re, lam_im, log_dt, b_re, b_im):
    lr = jnp.minimum(lam_re, -1e-4)
    li = lam_im
    dt = jnp.exp(log_dt)
    mag = jnp.exp(lr * dt)
    ab_re = mag * jnp.cos(li * dt)
    ab_im = mag * jnp.sin(li * dt)
    den = lr * lr + li * li
    n_re = ab_re - 1.0
    n_im = ab_im
    z_re = (n_re * lr + n_im * li) / den
    z_im = (n_im * lr - n_re * li) / den
    bb_re = z_re[None] * b_re - z_im[None] * b_im
    bb_im = z_re[None] * b_im + z_im[None] * b_re
    return ab_re, ab_im, bb_re, bb_im


def s5_params_fwd(lam_re, lam_im, log_dt, b_re, b_im, span):
    G, P = lam_re.shape
    H = b_re.shape[0]
    assert span & (span - 1) == 0

    def body(lr_ref, li_ref, dt_ref, br_ref, bi_ref, abr_ref, abi_ref, bbr_ref, bbi_ref, pr_ref, pi_ref):
        ab_re, ab_im, bb_re, bb_im = _s5_param_fn(lr_ref[...], li_ref[...], dt_ref[...], br_ref[...], bi_ref[...])
        abr_ref[...] = ab_re
        abi_ref[...] = ab_im
        bbr_ref[...] = bb_re
        bbi_ref[...] = bb_im
        cr, ci = ab_re, ab_im
        for _ in range(span.bit_length() - 1):
            cr, ci = cr * cr - ci * ci, 2.0 * cr * ci
        pr_ref[...] = cr
        pi_ref[...] = ci

    shp = lambda *s: jax.ShapeDtypeStruct(s, F32)
    return pl.pallas_call(
        body, name="s5_params_fwd",
        out_shape=[shp(G, P), shp(G, P), shp(H, G, P), shp(H, G, P), shp(G, P), shp(G, P)],
    )(lam_re, lam_im, log_dt, b_re, b_im)


def s5_params_bwd(lam_re, lam_im, log_dt, b_re, b_im, d_ab_re, d_ab_im, d_bb_re, d_bb_im):
    G, P = lam_re.shape
    H = b_re.shape[0]

    def body(lr_ref, li_ref, dt_ref, br_ref, bi_ref, g0, g1, g2, g3, o0, o1, o2, o3, o4):
        prim = (lr_ref[...], li_ref[...], dt_ref[...], br_ref[...], bi_ref[...])
        _, vjp = jax.vjp(_s5_param_fn, *prim)
        d = vjp((jnp.sum(g0[...], axis=0), jnp.sum(g1[...], axis=0), g2[...], g3[...]))
        o0[...], o1[...], o2[...], o3[...], o4[...] = d

    shp = lambda *s: jax.ShapeDtypeStruct(s, F32)
    return pl.pallas_call(
        body, name="s5_params_bwd",
        out_shape=[shp(G, P), shp(G, P), shp(G, 1), shp(H, G, P), shp(H, G, P)],
    )(lam_re, lam_im, log_dt, b_re, b_im, d_ab_re, d_ab_im, d_bb_re, d_bb_im)


def stream_order(a, tl):
    L, C = a.shape
    return a.reshape(L // tl, 8, tl // 8, C).transpose(0, 2, 1, 3).reshape(L, C)


def token_order(a, tl):
    L, C = a.shape
    return a.reshape(L // tl, tl // 8, 8, C).transpose(0, 2, 1, 3).reshape(L, C)


_LANE_BLK = 1024


def _cmul_add(ar, ai, xr, xi, br, bi):
    return br + (ar * xr - ai * xi), bi + (ar * xi + ai * xr)


def _cmulc_add(ar, ai, xr, xi, br, bi):
    return br + (ar * xr + ai * xi), bi + (ar * xi - ai * xr)


def _s5_states(u, wbd_ref, a_re, a_im, at_re, at_im, s_re, s_im, e_re, e_im, c0_re, c0_im, tl):
    t8 = tl // 8
    for k in range(S5_KBLK):
        bu = _mm(u[:, 128 * k:128 * (k + 1)], wbd_ref[k])
        s_re[:, 512 * k:512 * (k + 1)] = bu[:, :512]
        s_im[:, 512 * k:512 * (k + 1)] = bu[:, 512:]
    outs_re, outs_im = [], []
    for b in range(S5_LANES // _LANE_BLK):
        lanes = slice(_LANE_BLK * b, _LANE_BLK * (b + 1))
        ar = jnp.broadcast_to(a_re[:, lanes], (8, _LANE_BLK))
        ai = jnp.broadcast_to(a_im[:, lanes], (8, _LANE_BLK))

        def local(i, carry, lanes=lanes, ar=ar, ai=ai):
            r = pl.multiple_of(i * 8, 8)
            sr, si = _cmul_add(ar, ai, carry[0], carry[1], s_re[pl.ds(r, 8), lanes], s_im[pl.ds(r, 8), lanes])
            s_re[pl.ds(r, 8), lanes] = sr
            s_im[pl.ds(r, 8), lanes] = si
            return sr, si

        zero = jnp.zeros((8, _LANE_BLK), F32)
        fr, fi = lax.fori_loop(0, t8, local, (zero, zero), unroll=True)
        tr, ti = at_re[:, lanes], at_im[:, lanes]
        er, ei = c0_re[:, lanes], c0_im[:, lanes]
        ers, eis = [er], [ei]
        for j in range(8):
            er, ei = _cmul_add(tr, ti, er, ei, fr[j:j + 1], fi[j:j + 1])
            ers.append(er)
            eis.append(ei)
        outs_re.append(ers[8])
        outs_im.append(eis[8])
        ent_r, ent_i = jnp.concatenate(ers[:8], axis=0), jnp.concatenate(eis[:8], axis=0)
        e_re[:, lanes] = ent_r
        e_im[:, lanes] = ent_i

        def fix(i, carry, lanes=lanes, ar=ar, ai=ai):
            r = pl.multiple_of(i * 8, 8)
            zr, zi = ar * carry[0] - ai * carry[1], ar * carry[1] + ai * carry[0]
            s_re[pl.ds(r, 8), lanes] = s_re[pl.ds(r, 8), lanes] + zr
            s_im[pl.ds(r, 8), lanes] = s_im[pl.ds(r, 8), lanes] + zi
            return zr, zi

        lax.fori_loop(0, t8, fix, (ent_r, ent_i), unroll=True)
    return jnp.concatenate(outs_re, axis=1), jnp.concatenate(outs_im, axis=1)


def _s5_readout(s_re, s_im, cre_ref, cim_ref):
    ys = []
    for k in range(S5_KBLK):
        lanes = slice(512 * k, 512 * (k + 1))
        ys.append(_mm(s_re[:, lanes], cre_ref[k]) - _mm(s_im[:, lanes], cim_ref[k]))
    return jnp.concatenate(ys, axis=1)


def s5_forward(p, wbd, cre, cim, atab, d_skip, w_glu, b_glu, plan=None):
    L = p.shape[0]
    tl = min(TL_S5, L)
    nch = L // tl

    def body(u_ref, z_ref, wbd_ref, cre_ref, cim_ref, at_ref, d_ref, wg_ref, bg_ref,
             ya_ref, st_re_ref, st_im_ref, sv_re_ref, sv_im_ref, s_re, s_im, e_re, e_im, car_re, car_im):
        @pl.when(pl.program_id(0) == 0)
        def _():
            car_re[...] = jnp.zeros_like(car_re)
            car_im[...] = jnp.zeros_like(car_im)

        c0_re, c0_im = car_re[...], car_im[...]
        st_re_ref[0] = c0_re
        st_im_ref[0] = c0_im
        u = u_ref[...]
        x_re, x_im = _s5_states(u, wbd_ref, at_ref[0:1], at_ref[1:2], at_ref[2:3], at_ref[3:4],
                                s_re, s_im, e_re, e_im, c0_re, c0_im, tl)
        car_re[...] = x_re
        car_im[...] = x_im
        sv_re_ref[...] = s_re[...].astype(sv_re_ref.dtype)
        sv_im_ref[...] = s_im[...].astype(sv_im_ref.dtype)
        y = _s5_readout(sv_re_ref, sv_im_ref, cre_ref, cim_ref) + d_ref[...] * u
        yg = _gelu(y)
        gate = _sigmoid(_mm(yg, wg_ref[...]) + bg_ref[...])
        sz, _ = _silu_and_grad(z_ref[...])
        ya_ref[...] = (yg * gate * sz).astype(ya_ref.dtype)

    return _call(
        body, plan, name="s5_forward", grid=(nch,),
        in_specs=[pl.BlockSpec((tl, 1024), lambda i: (i, 0)), pl.BlockSpec((tl, 1024), lambda i: (i, 1)),
                  _full(wbd.shape), _full(cre.shape), _full(cim.shape), _full(atab.shape),
                  _full((1, 1024)), _full((1024, 1024)), _full((1, 1024))],
        out_specs=[pl.BlockSpec((tl, 1024), lambda i: (i, 0)),
                   pl.BlockSpec((1, 1, S5_LANES), lambda i: (i, 0, 0)),
                   pl.BlockSpec((1, 1, S5_LANES), lambda i: (i, 0, 0)),
                   pl.BlockSpec((tl, S5_LANES), lambda i: (i, 0)), pl.BlockSpec((tl, S5_LANES), lambda i: (i, 0))],
        out_shape=[jax.ShapeDtypeStruct((L, 1024), MXU_DTYPE),
                   jax.ShapeDtypeStruct((nch, 1, S5_LANES), F32), jax.ShapeDtypeStruct((nch, 1, S5_LANES), F32),
                   jax.ShapeDtypeStruct((L, S5_LANES), MXU_DTYPE), jax.ShapeDtypeStruct((L, S5_LANES), MXU_DTYPE)],
        scratch_shapes=[pltpu.VMEM((tl, S5_LANES), F32), pltpu.VMEM((tl, S5_LANES), F32),
                        pltpu.VMEM((8, S5_LANES), F32), pltpu.VMEM((8, S5_LANES), F32),
                        pltpu.VMEM((1, S5_LANES), F32), pltpu.VMEM((1, S5_LANES), F32)],
        sem=("arbitrary",),
    )(p, p, wbd, cre, cim, atab, d_skip, w_glu, b_glu)


def s5_backward(p, dya, st_re, st_im, sv_re, sv_im, wbd, cre, cim, atab, d_skip, w_glu, b_glu, plan=None):
    L = p.shape[0]
    tl = min(TL_S5, L)
    t8 = tl // 8
    nch = L // tl
    rev = lambda i: (nch - 1 - i, 0)
    rev1 = lambda i: (nch - 1 - i, 1)
    rev3 = lambda i: (nch - 1 - i, 0, 0)
    ct_shape = (S5_KBLK, cre.shape[2], cre.shape[1])

    def body(u_ref, z_ref, dya_ref, str_ref, sti_ref, s_re, s_im, wbd_ref, cre_ref, cim_ref, at_ref,
             d_ref, wg_ref, bg_ref,
             dp_ref, dwbd_ref, dcre_ref, dcim_ref, dabr_ref, dabi_ref, dd_ref, dwg_ref, dbg_ref,
             g_re, g_im, car_re, car_im):
        @pl.when(pl.program_id(0) == 0)
        def _():
            car_re[...] = jnp.zeros_like(car_re)
            car_im[...] = jnp.zeros_like(car_im)
            for r in (dwbd_ref, dcre_ref, dcim_ref, dabr_ref, dabi_ref, dd_ref, dwg_ref, dbg_ref):
                r[...] = jnp.zeros_like(r)

        u = u_ref[...]
        a_re, a_im, at_re, at_im = at_ref[0:1], at_ref[1:2], at_ref[2:3], at_ref[3:4]
        y = _s5_readout(s_re, s_im, cre_ref, cim_ref) + d_ref[...] * u
        yg, dyg = _gelu_and_grad(y)
        gate = _sigmoid(_mm(yg, wg_ref[...]) + bg_ref[...])
        sz, dsz = _silu_and_grad(z_ref[...])
        dya = dya_ref[...]
        s5out = yg * gate
        dp_ref[:, 1024:] = (dya * s5out * dsz).astype(dp_ref.dtype)
        ds5 = dya * sz
        dt = ds5 * yg * gate * (1.0 - gate)
        dwg_ref[...] += _mm_tn(yg, dt)
        dbg_ref[...] += jnp.sum(dt, axis=0, keepdims=True)
        dyv = (ds5 * gate + _mm_nt(dt, wg_ref[...])) * dyg
        dd_ref[...] += jnp.sum(dyv * u, axis=0, keepdims=True)

        for k in range(S5_KBLK):
            lanes = slice(512 * k, 512 * (k + 1))
            dyk = dyv[:, 128 * k:128 * (k + 1)]
            g_re[:, lanes] = _mm_nt(dyk, cre_ref[k])
            g_im[:, lanes] = -_mm_nt(dyk, cim_ref[k])
            dcre_ref[k] += _mm_tn(dyk, s_re[:, lanes])
            dcim_ref[k] -= _mm_tn(dyk, s_im[:, lanes])

        for b in range(S5_LANES // _LANE_BLK):
            lanes = slice(_LANE_BLK * b, _LANE_BLK * (b + 1))
            ar = jnp.broadcast_to(a_re[:, lanes], (8, _LANE_BLK))
            ai = jnp.broadcast_to(a_im[:, lanes], (8, _LANE_BLK))

            def local(j, carry, lanes=lanes, ar=ar, ai=ai):
                r = pl.multiple_of((t8 - 1 - j) * 8, 8)
                gr, gi = _cmulc_add(ar, ai, carry[0], carry[1], g_re[pl.ds(r, 8), lanes], g_im[pl.ds(r, 8), lanes])
                g_re[pl.ds(r, 8), lanes] = gr
                g_im[pl.ds(r, 8), lanes] = gi
                return gr, gi

            zero = jnp.zeros((8, _LANE_BLK), F32)
            fr, fi = lax.fori_loop(0, t8, local, (zero, zero), unroll=True)
            tr, ti = at_re[:, lanes], at_im[:, lanes]
            hr, hi = car_re[:, lanes], car_im[:, lanes]
            hrs, his = [hr], [hi]
            for j in range(7, -1, -1):
                hr, hi = _cmulc_add(tr, ti, hr, hi, fr[j:j + 1], fi[j:j + 1])
                hrs.append(hr)
                his.append(hi)
            car_re[:, lanes] = hrs[8]
            car_im[:, lanes] = his[8]
            in_r = jnp.concatenate(hrs[7::-1], axis=0)
            in_i = jnp.concatenate(his[7::-1], axis=0)

            wr, wi, nr, ni, accr, acci = in_r, in_i, zero, zero, zero, zero
            for pair in range(t8 // 2 - 1, -1, -1):
                rows = slice(16 * pair, 16 * pair + 16)
                s16r, s16i = s_re[rows, lanes].astype(F32), s_im[rows, lanes].astype(F32)
                for half in (1, 0):
                    r = 16 * pair + 8 * half
                    sr, si = s16r[8 * half:8 * half + 8], s16i[8 * half:8 * half + 8]
                    accr, acci = accr + (sr * nr + si * ni), acci + (sr * ni - si * nr)
                    wr, wi = ar * wr + ai * wi, ar * wi - ai * wr
                    nr, ni = g_re[r:r + 8, lanes] + wr, g_im[r:r + 8, lanes] + wi
                    g_re[r:r + 8, lanes] = nr
                    g_im[r:r + 8, lanes] = ni
            lr, li = s_re[tl - 16:tl, lanes].astype(F32)[8:], s_im[tl - 16:tl, lanes].astype(F32)[8:]
            row0 = lax.broadcasted_iota(jnp.int32, (8, _LANE_BLK), 0) == 0
            sr = jnp.where(row0, jnp.broadcast_to(str_ref[0][:, lanes], (8, _LANE_BLK)), pltpu.roll(lr, 1, 0))
            si = jnp.where(row0, jnp.broadcast_to(sti_ref[0][:, lanes], (8, _LANE_BLK)), pltpu.roll(li, 1, 0))
            dabr_ref[:, lanes] += accr + (sr * nr + si * ni)
            dabi_ref[:, lanes] += acci + (sr * ni - si * nr)

        dus = []
        for k in range(S5_KBLK):
            lanes = slice(512 * k, 512 * (k + 1))
            g = jnp.concatenate([g_re[:, lanes], g_im[:, lanes]], axis=1)
            dwbd_ref[k] += _mm_tn(u[:, 128 * k:128 * (k + 1)], g)
            dus.append(_mm_nt(g, wbd_ref[k]))
        du = jnp.concatenate(dus, axis=1) + dyv * d_ref[...]
        dp_ref[:, :1024] = du.astype(dp_ref.dtype)

    shp = lambda *s: jax.ShapeDtypeStruct(s, F32)
    return _call(
        body, plan, name="s5_backward", grid=(nch,),
        in_specs=[pl.BlockSpec((tl, 1024), rev), pl.BlockSpec((tl, 1024), rev1), pl.BlockSpec((tl, 1024), rev),
                  pl.BlockSpec((1, 1, S5_LANES), rev3), pl.BlockSpec((1, 1, S5_LANES), rev3),
                  pl.BlockSpec((tl, S5_LANES), rev), pl.BlockSpec((tl, S5_LANES), rev),
                  _full(wbd.shape), _full(cre.shape), _full(cim.shape), _full(atab.shape),
                  _full((1, 1024)), _full((1024, 1024)), _full((1, 1024))],
        out_specs=[pl.BlockSpec((tl, 2048), rev), _full(wbd.shape), _full(ct_shape), _full(ct_shape),
                   _full((8, S5_LANES)), _full((8, S5_LANES)), _full((1, 1024)), _full((1024, 1024)), _full((1, 1024))],
        out_shape=[jax.ShapeDtypeStruct((L, 2048), MXU_DTYPE), shp(*wbd.shape), shp(*ct_shape), shp(*ct_shape),
                   shp(8, S5_LANES), shp(8, S5_LANES), shp(1, 1024), shp(1024, 1024), shp(1, 1024)],
        scratch_shapes=[pltpu.VMEM((tl, S5_LANES), F32), pltpu.VMEM((tl, S5_LANES), F32),
                        pltpu.VMEM((1, S5_LANES), F32), pltpu.VMEM((1, S5_LANES), F32)],
        sem=("arbitrary",),
    )(p, p, dya, st_re, st_im, sv_re, sv_im, wbd, cre, cim, atab, d_skip, w_glu, b_glu)


def _block_diag(w, rows_first):
    g8 = w.reshape(S5_KBLK, 8, w.shape[1], w.shape[2])
    eye = jnp.eye(8, dtype=w.dtype)
    out = jnp.einsum('kgab,fg->kfagb', g8, eye)
    return out.reshape(S5_KBLK, 8 * w.shape[1], 8 * w.shape[2])


def _block_diag_extract(wbd, a, b):
    w5 = wbd.reshape(S5_KBLK, 8, a, 8, b)
    idx = jnp.arange(8)
    return w5[:, idx, :, idx, :].transpose(1, 0, 2, 3).reshape(S5_GROUPS, a, b)


def _ret_constants():
    log_g = np.log1p(-np.exp2(-5.0 - np.arange(RET_HEADS, dtype=np.float32))).astype(np.float32)
    idx = np.arange(RET_CHUNK, dtype=np.float32)
    diff = idx[:, None] - idx[None, :]
    decay = np.where(diff >= 0, np.exp(log_g[:, None, None] * np.maximum(diff, 0.0)), 0.0).astype(np.float32)
    xi = np.exp(log_g[None, :] * (idx[:, None] + 1.0)).astype(np.float32)
    zeta = np.exp(log_g[None, :] * (RET_CHUNK - 1.0 - idx[:, None])).astype(np.float32)
    chunk_decay = np.exp(log_g * RET_CHUNK).astype(np.float32)
    return decay, xi, zeta, chunk_decay


def _rope_tables(L):
    half = RET_DK // 2
    inv = ROPE_BASE ** (-jnp.arange(half, dtype=F32) / half)
    ang = jnp.arange(L, dtype=F32)[:, None] * inv[None, :]
    return jnp.cos(ang), jnp.sin(ang)


def _rot(xh, cos, sin):
    x1, x2 = xh[:, :128], xh[:, 128:]
    return jnp.concatenate([x1 * cos - x2 * sin, x1 * sin + x2 * cos], axis=1)


def _rot_t(dh, cos, sin):
    d1, d2 = dh[:, :128], dh[:, 128:]
    return jnp.concatenate([d1 * cos + d2 * sin, d2 * cos - d1 * sin], axis=1)


RET_PER_STEP = 2


def _ret_setup(L):
    nc = L // RET_CHUNK
    per = RET_PER_STEP if nc % RET_PER_STEP == 0 else 1
    decay_np, xi_np, zeta_np, cd_np = _ret_constants()
    tables = (jnp.asarray(decay_np), jnp.asarray(np.tile(xi_np, (per, 1))), jnp.asarray(np.tile(zeta_np, (per, 1))))
    return nc // per, per, tables, [float(c) for c in cd_np]


def _ret_rows(q_ref, k_ref, v_ref, cos_ref, sin_ref, xi_ref, zeta_ref):
    H = range(RET_HEADS)
    hs = [slice(RET_DK * h, RET_DK * (h + 1)) for h in H]
    cs, sn = cos_ref[...], sin_ref[...]
    qh = [_rot(q_ref[:, hs[h]], cs, sn) for h in H]
    kh = [_rot(k_ref[:, hs[h]], cs, sn) * (RET_DK ** -0.5) for h in H]
    vh = [v_ref[:, hs[h]] for h in H]
    qx = [qh[h] * xi_ref[:, h:h + 1] for h in H]
    kz = [kh[h] * zeta_ref[:, h:h + 1] for h in H]
    return hs, cs, sn, qh, kh, vh, qx, kz


def _ret_normed(qh, kh, vh, qx, dec_ref, prevs, per):
    H, C = range(RET_HEADS), range(per)
    rs = [slice(RET_CHUNK * c, RET_CHUNK * (c + 1)) for c in C]
    sc = [[_mm_nt(qh[h][rs[c]], kh[h][rs[c]]) * dec_ref[h] for h in H] for c in C]
    inner = [[_mm(sc[c][h], vh[h][rs[c]]) for h in H] for c in C]
    cross = [[_mm(qx[h][rs[c]], prevs[c][h]) for h in H] for c in C]
    o = [jnp.concatenate([inner[c][h] + cross[c][h] for c in C], axis=0) for h in H]
    oc = [o[h] - jnp.mean(o[h], axis=-1, keepdims=True) for h in H]
    rstd = [lax.rsqrt(jnp.mean(oc[h] * oc[h], axis=-1, keepdims=True) + NORM_EPS) for h in H]
    on = [oc[h] * rstd[h] for h in H]
    return rs, sc, rstd, on


def retention_forward(p, cos, sin, gain):
    L = p.shape[0]
    steps, per, (decay, xi, zeta), cd = _ret_setup(L)
    rows = RET_CHUNK * per

    def body(q_ref, k_ref, v_ref, z_ref, cos_ref, sin_ref, dec_ref, xi_ref, zeta_ref, gain_ref,
             yb_ref, prev_ref, state):
        @pl.when(pl.program_id(0) == 0)
        def _():
            state[...] = jnp.zeros_like(state)

        H, C = range(RET_HEADS), range(per)
        hs, cs, sn, qh, kh, vh, qx, kz = _ret_rows(q_ref, k_ref, v_ref, cos_ref, sin_ref, xi_ref, zeta_ref)
        prevs = [[state[h] for h in H]]
        for c in C:
            rs_c = slice(RET_CHUNK * c, RET_CHUNK * (c + 1))
            prevs.append([prevs[c][h] * cd[h] + _mm_tn(kz[h][rs_c], vh[h][rs_c]) for h in H])
        _, _, _, on = _ret_normed(qh, kh, vh, qx, dec_ref, prevs, per)
        sz, _ = _silu_and_grad(z_ref[...])
        for h in H:
            for c in C:
                prev_ref[c, h] = prevs[c][h].astype(prev_ref.dtype)
            state[h] = prevs[per][h]
            yb_ref[:, hs[h]] = (on[h] * gain_ref[:, hs[h]] * sz[:, hs[h]]).astype(yb_ref.dtype)

    blk = lambda c: pl.BlockSpec((rows, 1024), lambda i, c=c: (i, c))
    return pl.pallas_call(
        body, name="retention_forward", grid=(steps,),
        in_specs=[blk(0), blk(1), blk(2), blk(3),
                  pl.BlockSpec((rows, 128), lambda i: (i, 0)), pl.BlockSpec((rows, 128), lambda i: (i, 0)),
                  _full(decay.shape), _full(xi.shape), _full(zeta.shape), _full((1, 1024))],
        out_specs=[pl.BlockSpec((rows, 1024), lambda i: (i, 0)),
                   pl.BlockSpec((per, RET_HEADS, RET_DK, RET_DK), lambda i: (i, 0, 0, 0))],
        out_shape=[jax.ShapeDtypeStruct((L, 1024), MXU_DTYPE),
                   jax.ShapeDtypeStruct((steps * per, RET_HEADS, RET_DK, RET_DK), MXU_DTYPE)],
        scratch_shapes=[pltpu.VMEM((RET_HEADS, RET_DK, RET_DK), F32)],
        compiler_params=_cparams(("arbitrary",)),
    )(p, p, p, p, cos, sin, decay, xi, zeta, gain)


def retention_backward(p, dy, prevs, cos, sin, gain, plan=None):
    L = p.shape[0]
    steps, per, (decay, xi, zeta), cd = _ret_setup(L)
    rows = RET_CHUNK * per
    scale = RET_DK ** -0.5

    def body(q_ref, k_ref, v_ref, z_ref, dyb_ref, prev_ref, cos_ref, sin_ref, dec_ref, xi_ref, zeta_ref, gain_ref,
             dp_ref, dgain_ref, dstate):
        @pl.when(pl.program_id(0) == 0)
        def _():
            dstate[...] = jnp.zeros_like(dstate)
            dgain_ref[...] = jnp.zeros_like(dgain_ref)

        H, C = range(RET_HEADS), range(per)
        hs, cs, sn, qh, kh, vh, qx, kz = _ret_rows(q_ref, k_ref, v_ref, cos_ref, sin_ref, xi_ref, zeta_ref)
        prevs = [[prev_ref[c, h] for h in H] for c in C]
        rs, sc, rstd, on = _ret_normed(qh, kh, vh, qx, dec_ref, prevs, per)
        sz, dsz = _silu_and_grad(z_ref[...])
        dyb = dyb_ref[...]
        dong = [dyb[:, hs[h]] * sz[:, hs[h]] for h in H]
        don = [dong[h] * gain_ref[:, hs[h]] for h in H]
        do = [rstd[h] * (don[h] - jnp.mean(don[h], axis=-1, keepdims=True)
                         - on[h] * jnp.mean(don[h] * on[h], axis=-1, keepdims=True)) for h in H]
        dsc = [[_mm_nt(do[h][rs[c]], vh[h][rs[c]]) * dec_ref[h] for h in H] for c in C]
        dq_st = [[_mm_nt(do[h][rs[c]], prevs[c][h]) for h in H] for c in C]
        dnew = [[_mm_tn(qx[h][rs[c]], do[h][rs[c]]) for h in H] for c in C]
        dsts = [None] * per + [[dstate[h] for h in H]]
        for c in reversed(C):
            dsts[c] = [dsts[c + 1][h] * cd[h] + dnew[c][h] for h in H]
        dk_st = [[_mm_nt(vh[h][rs[c]], dsts[c + 1][h]) for h in H] for c in C]
        dv_st = [[_mm(kz[h][rs[c]], dsts[c + 1][h]) for h in H] for c in C]
        rows_of = lambda parts: jnp.concatenate(parts, axis=0)
        dqh = [rows_of([_mm(dsc[c][h], kh[h][rs[c]]) for c in C])
               + rows_of([dq_st[c][h] for c in C]) * xi_ref[:, h:h + 1] for h in H]
        dkh = [rows_of([_mm_tn(dsc[c][h], qh[h][rs[c]]) for c in C])
               + rows_of([dk_st[c][h] for c in C]) * zeta_ref[:, h:h + 1] for h in H]
        dvh = [rows_of([_mm_tn(sc[c][h], do[h][rs[c]]) + dv_st[c][h] for c in C]) for h in H]
        for h in H:
            dstate[h] = dsts[0][h]
            dgain_ref[:, hs[h]] += jnp.sum(dong[h] * on[h], axis=0, keepdims=True)
            dp_ref[:, hs[h]] = _rot_t(dqh[h], cs, sn).astype(dp_ref.dtype)
            dp_ref[:, 1024 + RET_DK * h:1024 + RET_DK * (h + 1)] = (_rot_t(dkh[h], cs, sn) * scale).astype(dp_ref.dtype)
            dp_ref[:, 2048 + RET_DK * h:2048 + RET_DK * (h + 1)] = dvh[h].astype(dp_ref.dtype)
            dp_ref[:, 3072 + RET_DK * h:3072 + RET_DK * (h + 1)] = (
                dyb[:, hs[h]] * on[h] * gain_ref[:, hs[h]] * dsz[:, hs[h]]).astype(dp_ref.dtype)

    blk = lambda c: pl.BlockSpec((rows, 1024), lambda i, c=c: (steps - 1 - i, c))
    tab = pl.BlockSpec((rows, 128), lambda i: (steps - 1 - i, 0))
    return _call(
        body, plan, name="retention_backward", grid=(steps,),
        in_specs=[blk(0), blk(1), blk(2), blk(3), blk(0),
                  pl.BlockSpec((per, RET_HEADS, RET_DK, RET_DK), lambda i: (steps - 1 - i, 0, 0, 0)),
                  tab, tab, _full(decay.shape), _full(xi.shape), _full(zeta.shape), _full((1, 1024))],
        out_specs=[pl.BlockSpec((rows, 4096), lambda i: (steps - 1 - i, 0)), _full((1, 1024))],
        out_shape=[jax.ShapeDtypeStruct((L, 4096), MXU_DTYPE), jax.ShapeDtypeStruct((1, 1024), F32)],
        scratch_shapes=[pltpu.VMEM((RET_HEADS, RET_DK, RET_DK), F32)],
        sem=("arbitrary",),
    )(p, p, p, p, dy, prevs, cos, sin, decay, xi, zeta, gain)


def _sgu_mix(p_ref, gain_ref, wm_ref, bt_ref, tl):
    pu, pv, z = p_ref[:, :2048], p_ref[:, 2048:4096], p_ref[:, 4096:]
    (u, du), (v, dv) = _gelu_and_grad(pu), _gelu_and_grad(pv)
    mu = jnp.mean(v, axis=-1, keepdims=True)
    vc = v - mu
    rstd = lax.rsqrt(jnp.mean(vc * vc, axis=-1, keepdims=True) + NORM_EPS)
    vn = vc * rstd
    vg = vn * gain_ref[...]
    mask = (lax.broadcasted_iota(jnp.int32, (SGU_CHUNK, SGU_CHUNK), 0)
            >= lax.broadcasted_iota(jnp.int32, (SGU_CHUNK, SGU_CHUNK), 1))
    wms = [jnp.where(mask, wm_ref[g], 0.0) for g in range(SGU_GROUPS)]
    rows = []
    for c in range(tl // SGU_CHUNK):
        rs = slice(SGU_CHUNK * c, SGU_CHUNK * (c + 1))
        cols = []
        for g in range(SGU_GROUPS):
            gs = slice(SGU_GDIM * g, SGU_GDIM * (g + 1))
            cols.append(_mm(wms[g], vg[rs, gs]) + bt_ref[:, g:g + 1])
        rows.append(jnp.concatenate(cols, axis=1))
    s = rows[0] if len(rows) == 1 else jnp.concatenate(rows, axis=0)
    return du, dv, z, u, vn, rstd, vg, wms, mask, s


def sgu_forward(p, gain, wm, bt):
    L = p.shape[0]
    tl = min(TL_SGU, L)

    def body(p_ref, gain_ref, wm_ref, bt_ref, y_ref):
        _, _, z, u, _, _, _, _, _, s = _sgu_mix(p_ref, gain_ref, wm_ref, bt_ref, tl)
        sz, _ = _silu_and_grad(z)
        y_ref[...] = (u * s * sz).astype(y_ref.dtype)

    return pl.pallas_call(
        body, name="sgu_forward", grid=(L // tl,),
        in_specs=[pl.BlockSpec((tl, ODD_IN), lambda i: (i, 0)), _full((1, 2048)), _full(wm.shape), _full(bt.shape)],
        out_specs=pl.BlockSpec((tl, 2048), lambda i: (i, 0)),
        out_shape=jax.ShapeDtypeStruct((L, 2048), MXU_DTYPE),
        compiler_params=_cparams(("arbitrary",)),
    )(p, gain, wm, bt)


def sgu_backward(p, dy, gain, wm, bt, plan=None):
    L = p.shape[0]
    tl = min(TL_SGU, L)

    def body(p_ref, dy_ref, gain_ref, wm_ref, bt_ref, dp_ref, dgain_ref, dwm_ref, dbt_ref):
        @pl.when(pl.program_id(0) == 0)
        def _():
            dgain_ref[...] = jnp.zeros_like(dgain_ref)
            dwm_ref[...] = jnp.zeros_like(dwm_ref)
            dbt_ref[...] = jnp.zeros_like(dbt_ref)

        gu, gv, z, u, vn, rstd, vg, wms, mask, s = _sgu_mix(p_ref, gain_ref, wm_ref, bt_ref, tl)
        sz, dsz = _silu_and_grad(z)
        dyv = dy_ref[...]
        dp_ref[:, 4096:] = (dyv * u * s * dsz).astype(dp_ref.dtype)
        dsg = dyv * sz
        dp_ref[:, :2048] = (dsg * s * gu).astype(dp_ref.dtype)
        ds = dsg * u
        rows = []
        dbs = [jnp.zeros((SGU_CHUNK, 1), F32) for _ in range(SGU_GROUPS)]
        for c in range(tl // SGU_CHUNK):
            rs = slice(SGU_CHUNK * c, SGU_CHUNK * (c + 1))
            cols = []
            for g in range(SGU_GROUPS):
                gs = slice(SGU_GDIM * g, SGU_GDIM * (g + 1))
                dsg_c = ds[rs, gs]
                dbs[g] = dbs[g] + jnp.sum(dsg_c, axis=1, keepdims=True)
                dwm_ref[g] += jnp.where(mask, _mm_nt(dsg_c, vg[rs, gs]), 0.0)
                cols.append(_mm_tn(wms[g], dsg_c))
            rows.append(jnp.concatenate(cols, axis=1))
        dbt_ref[...] += jnp.concatenate(dbs, axis=1)
        dvg = rows[0] if len(rows) == 1 else jnp.concatenate(rows, axis=0)
        dgain_ref[...] += jnp.sum(dvg * vn, axis=0, keepdims=True)
        dvn = dvg * gain_ref[...]
        dv = rstd * (dvn - jnp.mean(dvn, axis=-1, keepdims=True) - vn * jnp.mean(dvn * vn, axis=-1, keepdims=True))
        dp_ref[:, 2048:4096] = (dv * gv).astype(dp_ref.dtype)

    return _call(
        body, plan, name="sgu_backward", grid=(L // tl,),
        in_specs=[pl.BlockSpec((tl, ODD_IN), lambda i: (i, 0)), pl.BlockSpec((tl, 2048), lambda i: (i, 0)),
                  _full((1, 2048)), _full(wm.shape), _full(bt.shape)],
        out_specs=[pl.BlockSpec((tl, ODD_IN), lambda i: (i, 0)), _full((1, 2048)), _full(wm.shape), _full(bt.shape)],
        out_shape=[jax.ShapeDtypeStruct((L, ODD_IN), MXU_DTYPE), jax.ShapeDtypeStruct((1, 2048), F32),
                   jax.ShapeDtypeStruct(wm.shape, F32), jax.ShapeDtypeStruct(bt.shape, F32)],
        sem=("arbitrary",),
    )(p, dy, gain, wm, bt)


def cast_shards(mats):
    n = len(mats)
    steps = 8

    def body(*refs):
        for p in range(n):
            refs[n + p][...] = refs[p][...].astype(MXU_DTYPE)

    specs = [pl.BlockSpec((m.shape[0] // steps, m.shape[1]), lambda i: (i, 0)) for m in mats]
    return pl.pallas_call(
        body, name="cast_shards", grid=(steps,), in_specs=specs, out_specs=specs,
        out_shape=[jax.ShapeDtypeStruct(m.shape, MXU_DTYPE) for m in mats],
        compiler_params=_cparams(("arbitrary",)),
    )(*mats)


def local_grads(x, tgt, w):
    L = x.shape[0]
    ne, gf = w["norm_even"], w["final_norm"].reshape(1, D_MODEL)
    sh = dict(zip(MATRICES, cast_shards([w[n][0] for n in MATRICES])))
    (w_in_e,) = run_plan(gather_plan([sh["w_in_even"]]), "gather_w_in_even")
    lam_re, lam_im = w["s5_lam_re"][0], w["s5_lam_im"][0]
    log_dt = w["s5_log_dt"].reshape(S5_GROUPS, 1)
    bt_re = jnp.transpose(w["s5_b_re"][0], (2, 0, 1))
    bt_im = jnp.transpose(w["s5_b_im"][0], (2, 0, 1))
    c_re, c_im = w["s5_c_re"][0], w["s5_c_im"][0]
    wm = w["sgu_w_spatial"][0]
    bt = jnp.transpose(w["sgu_b_spatial"][0])

    tl5 = min(TL_S5, L)
    ab_re, ab_im, bb_re, bb_im, at_re, at_im = s5_params_fwd(lam_re, lam_im, log_dt, bt_re, bt_im, tl5 // 8)
    atab = jnp.stack([ab_re.reshape(S5_LANES), ab_im.reshape(S5_LANES),
                      at_re.reshape(S5_LANES), at_im.reshape(S5_LANES)])
    wbd = jnp.concatenate([_block_diag(jnp.transpose(bb_re, (1, 0, 2)), True),
                           _block_diag(jnp.transpose(bb_im, (1, 0, 2)), True)], axis=2).astype(MXU_DTYPE)
    cre = _block_diag(jnp.transpose(c_re, (0, 2, 1)), True).astype(MXU_DTYPE)
    cim = _block_diag(jnp.transpose(c_im, (0, 2, 1)), True).astype(MXU_DTYPE)
    cos, sin = _rope_tables(L)

    s5_cols = 2 * S5_WIDTH
    w_s5 = jnp.concatenate([w_in_e[0], w_in_e[1][:, :s5_cols - EVEN_IN // N_CHIPS]], axis=1)
    w_ret = jnp.concatenate([w_in_e[1][:, s5_cols - EVEN_IN // N_CHIPS:], w_in_e[2], w_in_e[3]], axis=1)
    (p1a, h0s), (w_glu,) = norm_matmul(stream_order(x, tl5), ne, w_s5, "even_in_s5",
                                       gather_plan([sh["s5_w_glu"]]), tn=1024)
    (p1b, h0), (w_out_e,) = norm_matmul(x, ne, w_ret, "even_in_ret", gather_plan([sh["w_out_even"]]), tn=1024)
    w_glu = w_glu.reshape(S5_WIDTH, S5_WIDTH)
    w_out_e = w_out_e.reshape(2 * S5_WIDTH, D_MODEL)
    (ya, st_re, st_im, sv_re, sv_im), (w_in_o, w_out_o, no, sg_gain) = s5_forward(
        p1a, wbd, cre, cim, atab, w["s5_d"], w_glu, w["s5_b_glu"],
        gather_plan([sh["w_in_odd"], sh["w_out_odd"], w["norm_odd"], w["sgu_norm_gain"]]))
    w_out_o = w_out_o.reshape(SGU_WIDTH, D_MODEL)
    no, sg_gain = no.reshape(1, D_MODEL), sg_gain.reshape(1, SGU_WIDTH)
    yb, prevs = retention_forward(p1b, cos, sin, w["ret_gn_gain"])
    ya = token_order(ya, tl5)
    x1 = matmul_residual([ya, yb], w_out_e, x, "even_out")
    (p2, h1), _ = norm_matmul(x1, no, w_in_o, "odd_in")
    y2 = sgu_forward(p2, sg_gain, wm, bt)
    dx2, loss, dgf = out_proj_loss(y2, w_out_o, x1, gf, tgt, "odd_out_loss")

    g, landed = {}, {}
    shard_major = lambda a, n: a.reshape((N_CHIPS,) + w[n].shape[1:])
    dy2, g_w_out_o = out_proj_bwd(dx2, w_out_o, [y2], "odd_out_bwd")
    (dp2, g["sgu_norm_gain"], dwm, dbt), (landed["w_out_odd"],) = sgu_backward(
        p2, dy2, sg_gain, wm, bt, reduce_plan([shard_major(g_w_out_o, "w_out_odd")]))
    g_w_in_o = in_proj_bwd_dw(h1, dp2, "odd_in_dw", ODD_IN // N_CHIPS)
    (dx1, g["norm_odd"]), _ = in_proj_bwd_dx(x1, no, [dp2], [w_in_o], dx2, "odd_in_dx")
    dya, dyb, g_w_out_e = out_proj_bwd(dx1, w_out_e, [ya, yb], "even_out_bwd")
    ((dpa, dwbd, dcre, dcim, dab_re, dab_im, g["s5_d"], g_w_glu, g["s5_b_glu"]),
     (landed["w_in_odd"], landed["w_out_even"])) = s5_backward(
        p1a, stream_order(dya, tl5), st_re, st_im, sv_re, sv_im, wbd, cre, cim, atab, w["s5_d"], w_glu,
        w["s5_b_glu"], reduce_plan([g_w_in_o, shard_major(g_w_out_e, "w_out_even")]))

    dbb_re = jnp.transpose(_block_diag_extract(dwbd[:, :, :512], S5_GROUP, S5_STATE), (1, 0, 2))
    dbb_im = jnp.transpose(_block_diag_extract(dwbd[:, :, 512:], S5_GROUP, S5_STATE), (1, 0, 2))
    dlr, dli, ddt, dbt_re, dbt_im = s5_params_bwd(
        lam_re, lam_im, log_dt, bt_re, bt_im, dab_re.reshape(8, S5_GROUPS, S5_STATE),
        dab_im.reshape(8, S5_GROUPS, S5_STATE), dbb_re, dbb_im)
    g["s5_lam_re"], g["s5_lam_im"] = dlr[None], dli[None]
    g["s5_log_dt"] = ddt.reshape(1, S5_GROUPS)
    g["s5_b_re"], g["s5_b_im"] = dbt_re, dbt_im
    g["s5_c_re"] = _block_diag_extract(dcre, S5_GROUP, S5_STATE)[None]
    g["s5_c_im"] = _block_diag_extract(dcim, S5_GROUP, S5_STATE)[None]
    g["sgu_w_spatial"] = dwm[None]
    g["sgu_b_spatial"] = jnp.transpose(dbt)[None]
    g["final_norm"] = dgf.reshape(D_MODEL)
    g["loss"] = loss

    early = BEHIND_RETENTION_BWD + ("loss",)
    (dpb, g["ret_gn_gain"]), recv = retention_backward(
        p1b, dyb, prevs, cos, sin, w["ret_gn_gain"],
        reduce_plan([shard_major(g_w_glu, "s5_w_glu")], [g[n] for n in early]))
    landed.update(zip(("s5_w_glu",) + early, recv))
    g_w_in_e = in_proj_bwd_dw(h0s, dpa, "even_in_dw_s5", 512, dtype=MXU_DTYPE)
    g_w_in_e = in_proj_bwd_dw(h0, dpb, "even_in_dw_ret", 512, first=s5_cols // 512, into=g_w_in_e, dtype=MXU_DTYPE)
    (dx0, g["norm_even"]), recv = in_proj_bwd_dx(
        x, ne, [token_order(dpa, tl5), dpb], [w_s5, w_ret], dx1, "even_in_dx",
        reduce_plan([g_w_in_e], [g[n] for n in BEHIND_EVEN_IN_DX]))
    landed.update(zip(("w_in_even",) + BEHIND_EVEN_IN_DX, recv))
    (landed["norm_even"],) = run_plan(reduce_plan([], [g["norm_even"]]), "exchange_norm_even")
    return dx0, landed


def sibling_exchange(arrs):
    n = len(arrs)

    def body(*refs):
        in_refs, out_refs = refs[:n], refs[n:2 * n]
        send_sems, recv_sems = refs[2 * n:]
        x, y, c = _place()
        copies = [pltpu.make_async_remote_copy(
            src_ref=in_refs[p], dst_ref=out_refs[p], send_sem=send_sems.at[p], recv_sem=recv_sems.at[p],
            device_id=(x, y, 1 - c), device_id_type=MESH) for p in range(n)]
        for cp in copies:
            cp.start()
        for cp in copies:
            cp.wait_recv()
        for cp in copies:
            cp.wait_send()

    return pl.pallas_call(
        body, name="sibling_exchange", in_specs=[ANY] * n, out_specs=[ANY] * n,
        out_shape=[jax.ShapeDtypeStruct(a.shape, a.dtype) for a in arrs],
        scratch_shapes=[pltpu.SemaphoreType.DMA((n,)), pltpu.SemaphoreType.DMA((n,))],
    )(*arrs)


def _row_block(rows):
    return 128 if rows % 128 == 0 else rows


def sum_slabs(r, name):
    _, R, C = r.shape
    tr = _row_block(R)

    def body(r_ref, o_ref):
        a, b, c, d = (r_ref[k].astype(F32) for k in range(N_CHIPS))
        o_ref[...] = (a + b) + (c + d)

    return pl.pallas_call(
        body, name=name, grid=(R // tr,),
        in_specs=[pl.BlockSpec((N_CHIPS, tr, C), lambda i: (0, i, 0))],
        out_specs=pl.BlockSpec((tr, C), lambda i: (i, 0)),
        out_shape=jax.ShapeDtypeStruct((R, C), F32),
        compiler_params=_cparams(("arbitrary",)),
    )(r)


def _adam(w, m, v, g):
    mn = ADAM_B1 * m + (1.0 - ADAM_B1) * g
    vn = ADAM_B2 * v + (1.0 - ADAM_B2) * (g * g)
    m_hat = mn / (1.0 - ADAM_B1 ** ADAM_STEP)
    v_hat = vn / (1.0 - ADAM_B2 ** ADAM_STEP)
    return -ADAM_LR * (m_hat / (jnp.sqrt(v_hat) + ADAM_EPS) + ADAM_WD * w), mn, vn


def adam_update(w, m, v, ga, gb, name):
    R, C = w.shape
    tr = _row_block(R)

    def body(w_ref, m_ref, v_ref, ga_ref, gb_ref, g_out, d_out, m_out, v_out):
        g = ga_ref[...] + gb_ref[...]
        g_out[...] = g
        d_out[...], m_out[...], v_out[...] = _adam(w_ref[...], m_ref[...], v_ref[...], g)

    blk = pl.BlockSpec((tr, C), lambda i: (i, 0))
    return pl.pallas_call(
        body, name=name, grid=(R // tr,),
        in_specs=[blk] * 5, out_specs=[blk] * 4,
        out_shape=[jax.ShapeDtypeStruct((R, C), F32)] * 4,
        compiler_params=_cparams(("arbitrary",)),
    )(w, m, v, ga, gb)


WIDE_ROWS = ("s5_b_re", "s5_b_im")


def sum_small(landed):
    def body(*refs):
        k = len(refs) // 2
        for i in range(k):
            r = refs[i]
            refs[k + i][...] = (r[0] + r[1]) + (r[2] + r[3])

    names = list(landed)
    res = pl.pallas_call(
        body, name="sum_small", out_shape=[jax.ShapeDtypeStruct(landed[n].shape[1:], F32) for n in names],
        compiler_params=pltpu.CompilerParams(vmem_limit_bytes=VMEM_LIMIT),
    )(*[landed[n] for n in names])
    return dict(zip(names, res))


def adam_small(names, w, m, v, ga, gb):
    def body(*refs):
        k = len(refs) // 9
        me = 2 * lax.axis_index("x") + lax.axis_index("y")
        for i in range(k):
            w_ref, m_ref, v_ref, ga_ref, gb_ref = refs[i], refs[k + i], refs[2 * k + i], refs[3 * k + i], refs[4 * k + i]
            size = w_ref.shape[-1]
            if ga_ref.shape != w_ref.shape:
                part = pl.ds(pl.multiple_of(me * size, LANES), size)
                g = ga_ref[:, part] + gb_ref[:, part]
            else:
                g = ga_ref[...] + gb_ref[...]
            refs[5 * k + i][...] = g
            refs[6 * k + i][...], refs[7 * k + i][...], refs[8 * k + i][...] = _adam(w_ref[...], m_ref[...], v_ref[...], g)

    ins = [d[n] for d in (w, m, v, ga, gb) for n in names]
    outs = [jax.ShapeDtypeStruct(w[n].shape, F32) for _ in range(4) for n in names]
    res = pl.pallas_call(body, name="adam_small", out_shape=outs,
                         compiler_params=pltpu.CompilerParams(vmem_limit_bytes=VMEM_LIMIT))(*ins)
    k = len(names)
    return [dict(zip(names, res[j * k:(j + 1) * k])) for j in range(4)]


WEIGHTS = ("norm_even", "w_in_even", "s5_lam_re", "s5_lam_im", "s5_log_dt", "s5_b_re", "s5_b_im", "s5_c_re",
           "s5_c_im", "s5_d", "s5_w_glu", "s5_b_glu", "ret_gn_gain", "w_out_even", "norm_odd", "w_in_odd",
           "sgu_norm_gain", "sgu_w_spatial", "sgu_b_spatial", "w_out_odd", "final_norm")
MATRICES = ("w_in_even", "s5_w_glu", "w_out_even", "w_in_odd", "w_out_odd")
SHARDED_VECS = ("norm_odd", "sgu_norm_gain")
REPLICATED = tuple(n for n in WEIGHTS if n not in MATRICES and n not in SHARDED_VECS)
SMALL = tuple(n for n in WEIGHTS if n not in MATRICES)
BEHIND_RETENTION_BWD = tuple(n for n in SMALL if n not in ("ret_gn_gain", "norm_even"))
BEHIND_EVEN_IN_DX = ("ret_gn_gain",)
LANES = 128


def kernel(x, norm_even, w_in_even, s5_lam_re, s5_lam_im, s5_log_dt, s5_b_re, s5_b_im, s5_c_re, s5_c_im, s5_d, s5_w_glu, s5_b_glu, ret_gn_gain, w_out_even, norm_odd, w_in_odd, sgu_norm_gain, sgu_w_spatial, sgu_b_spatial, w_out_odd, final_norm, loss_target, m_norm_even, m_w_in_even, m_s5_lam_re, m_s5_lam_im, m_s5_log_dt, m_s5_b_re, m_s5_b_im, m_s5_c_re, m_s5_c_im, m_s5_d, m_s5_w_glu, m_s5_b_glu, m_ret_gn_gain, m_w_out_even, m_norm_odd, m_w_in_odd, m_sgu_norm_gain, m_sgu_w_spatial, m_sgu_b_spatial, m_w_out_odd, m_final_norm, v_norm_even, v_w_in_even, v_s5_lam_re, v_s5_lam_im, v_s5_log_dt, v_s5_b_re, v_s5_b_im, v_s5_c_re, v_s5_c_im, v_s5_d, v_s5_w_glu, v_s5_b_glu, v_ret_gn_gain, v_w_out_even, v_norm_odd, v_w_in_odd, v_sgu_norm_gain, v_sgu_w_spatial, v_sgu_b_spatial, v_w_out_odd, v_final_norm):
    w = dict(norm_even=norm_even, w_in_even=w_in_even, s5_lam_re=s5_lam_re, s5_lam_im=s5_lam_im, s5_log_dt=s5_log_dt, s5_b_re=s5_b_re, s5_b_im=s5_b_im, s5_c_re=s5_c_re, s5_c_im=s5_c_im, s5_d=s5_d, s5_w_glu=s5_w_glu, s5_b_glu=s5_b_glu, ret_gn_gain=ret_gn_gain, w_out_even=w_out_even, norm_odd=norm_odd, w_in_odd=w_in_odd, sgu_norm_gain=sgu_norm_gain, sgu_w_spatial=sgu_w_spatial, sgu_b_spatial=sgu_b_spatial, w_out_odd=w_out_odd, final_norm=final_norm)
    m = dict(norm_even=m_norm_even, w_in_even=m_w_in_even, s5_lam_re=m_s5_lam_re, s5_lam_im=m_s5_lam_im, s5_log_dt=m_s5_log_dt, s5_b_re=m_s5_b_re, s5_b_im=m_s5_b_im, s5_c_re=m_s5_c_re, s5_c_im=m_s5_c_im, s5_d=m_s5_d, s5_w_glu=m_s5_w_glu, s5_b_glu=m_s5_b_glu, ret_gn_gain=m_ret_gn_gain, w_out_even=m_w_out_even, norm_odd=m_norm_odd, w_in_odd=m_w_in_odd, sgu_norm_gain=m_sgu_norm_gain, sgu_w_spatial=m_sgu_w_spatial, sgu_b_spatial=m_sgu_b_spatial, w_out_odd=m_w_out_odd, final_norm=m_final_norm)
    v = dict(norm_even=v_norm_even, w_in_even=v_w_in_even, s5_lam_re=v_s5_lam_re, s5_lam_im=v_s5_lam_im, s5_log_dt=v_s5_log_dt, s5_b_re=v_s5_b_re, s5_b_im=v_s5_b_im, s5_c_re=v_s5_c_re, s5_c_im=v_s5_c_im, s5_d=v_s5_d, s5_w_glu=v_s5_w_glu, s5_b_glu=v_s5_b_glu, ret_gn_gain=v_ret_gn_gain, w_out_even=v_w_out_even, norm_odd=v_norm_odd, w_in_odd=v_w_in_odd, sgu_norm_gain=v_sgu_norm_gain, sgu_w_spatial=v_sgu_w_spatial, sgu_b_spatial=v_sgu_b_spatial, w_out_odd=v_w_out_odd, final_norm=v_final_norm)

    grad_x, landed = local_grads(x[0], loss_target[0], w)

    small = SMALL + ("loss",)
    part = {n: sum_slabs(landed[n], "sum_" + n) for n in MATRICES}
    part.update(sum_small({n: landed[n] for n in small}))
    names = MATRICES + small
    other = dict(zip(names, sibling_exchange([part[n] for n in names])))

    wt, mt, vt = dict(w), dict(m), dict(v)
    for n in WIDE_ROWS:
        wt[n], mt[n], vt[n] = (jnp.transpose(a[n][0], (2, 0, 1)) for a in (w, m, v))
    out_g, out_d, out_m, out_v = adam_small(SMALL, wt, mt, vt, part, other)
    for n in WIDE_ROWS:
        for out in (out_g, out_d, out_m, out_v):
            out[n] = jnp.transpose(out[n], (1, 2, 0))[None]
    for n in MATRICES:
        res = adam_update(w[n][0], m[n][0], v[n][0], part[n], other[n], "adam_" + n)
        out_g[n], out_d[n], out_m[n], out_v[n] = (r[None] for r in res)
    total_loss = (part["loss"] + other["loss"])[0, 0]

    return (total_loss, grad_x[None], *[out_g[n] for n in WEIGHTS], *[out_d[n] for n in WEIGHTS],
            *[out_m[n] for n in WEIGHTS], *[out_v[n] for n in WEIGHTS])
```

```python
import functools
import math

import numpy as np
import jax
import jax.numpy as jnp
from jax import lax
from jax.experimental import pallas as pl
from jax.experimental.pallas import tpu as pltpu

F32 = jnp.float32
MXU_DTYPE = jnp.bfloat16
NORM_EPS = 1e-6
D_MODEL = 1024
S5_WIDTH = 1024
S5_GROUP = 16
S5_GROUPS = 64
S5_STATE = 64
S5_LANES = S5_GROUPS * S5_STATE
S5_KBLK = 8
RET_HEADS = 4
RET_DK = 256
RET_CHUNK = 128
ROPE_BASE = 10000.0
SGU_WIDTH = 2048
SGU_GROUPS = 4
SGU_GDIM = 512
SGU_CHUNK = 128
EVEN_IN = 6144
ODD_IN = 6144
ADAM_LR = 0.001
ADAM_B1 = 0.9
ADAM_B2 = 0.999
ADAM_EPS = 1e-08
ADAM_WD = 0.01
ADAM_STEP = 10
N_CHIPS = 4
VMEM_LIMIT = 56 * 1024 * 1024

TL_PROJ = 512
TL_DW = 1024
TL_S5 = 256
TL_SGU = 256


def _cparams(sem, **kw):
    return pltpu.CompilerParams(dimension_semantics=sem, vmem_limit_bytes=VMEM_LIMIT, **kw)


def _mm(a, b):
    return jnp.dot(a.astype(MXU_DTYPE), b.astype(MXU_DTYPE), preferred_element_type=F32)


def _mm_nt(a, b):
    return lax.dot_general(a.astype(MXU_DTYPE), b.astype(MXU_DTYPE),
                           (((1,), (1,)), ((), ())), preferred_element_type=F32)


def _mm_tn(a, b):
    return lax.dot_general(a.astype(MXU_DTYPE), b.astype(MXU_DTYPE),
                           (((0,), (0,)), ((), ())), preferred_element_type=F32)


_GELU_C = math.sqrt(2.0 / math.pi)


def _gelu_parts(x):
    x2 = x * x
    th = jnp.tanh(x * (_GELU_C + (_GELU_C * 0.044715) * x2))
    hx = 0.5 * x
    return hx + hx * th, th, x2, hx


def _gelu(x):
    return _gelu_parts(x)[0]


def _gelu_and_grad(x):
    g, th, x2, hx = _gelu_parts(x)
    return g, (0.5 + 0.5 * th) + hx * (1.0 - th * th) * (_GELU_C + (3.0 * _GELU_C * 0.044715) * x2)


def _gelu_grad(x):
    return _gelu_and_grad(x)[1]


def _sigmoid(x):
    return 1.0 / (1.0 + jnp.exp(-x))


def _silu_and_grad(x):
    s = _sigmoid(x)
    return x * s, s * (1.0 + x * (1.0 - s))


def _rms(x):
    return lax.rsqrt(jnp.mean(x * x, axis=-1, keepdims=True) + NORM_EPS)


def _full(shape):
    nd = len(shape)
    return pl.BlockSpec(shape, lambda *_: (0,) * nd)


MESH = pl.DeviceIdType.MESH
ANY = pl.BlockSpec(memory_space=pl.ANY)


def _place():
    return lax.axis_index("x"), lax.axis_index("y"), lax.axis_index("c")


def _chip_peer(x, y, c, d):
    return (1 - x if d >= 2 else x, 1 - y if d % 2 else y, c)


class _Plan:
    def __init__(self, inputs, out_shape, build):
        self.inputs, self.out_shape, self._build = list(inputs), list(out_shape), build
        n = len(self.inputs)
        self.sems = [pltpu.SemaphoreType.DMA((n, 3)), pltpu.SemaphoreType.DMA((n, 3)), pltpu.SemaphoreType.DMA((n,))]

    def start(self, in_refs, out_refs, sems):
        send, recv, local = self._build(in_refs, out_refs, sems)
        for p in range(len(self.inputs)):
            local[p].start()
            for cp in send[p]:
                cp.start()

    def wait(self, in_refs, out_refs, sems):
        send, recv, local = self._build(in_refs, out_refs, sems)
        for p in range(len(self.inputs)):
            for cp in recv[p]:
                cp.wait_recv()
        for p in range(len(self.inputs)):
            for cp in send[p]:
                cp.wait_send()
            local[p].wait()


class _GatherPlan:
    def __init__(self, shards):
        self.inputs = list(shards)
        self.out_shape = [jax.ShapeDtypeStruct((N_CHIPS,) + s.shape, s.dtype) for s in shards]
        n = len(shards)
        self.halved = [s.shape[0] % 32 == 0 for s in shards]
        self.sems = [pltpu.SemaphoreType.DMA((n, 3)) for _ in range(4)] + [pltpu.SemaphoreType.DMA((n,))]

    def _copies(self, in_refs, out_refs, sems):
        ici_s, ici_r, d2d_s, d2d_r, loc = sems
        x, y, c = _place()
        me = 2 * x + y

        def rows(p, core):
            if not self.halved[p]:
                return slice(None)
            half = self.inputs[p].shape[0] // 2
            return pl.ds(pl.multiple_of(core * half, 16), half)

        def ici(p, d, slab, core, src=None):
            dst = out_refs[p].at[slab, rows(p, core)]
            return pltpu.make_async_remote_copy(
                src_ref=in_refs[p].at[rows(p, core)] if src is None else src, dst_ref=dst,
                send_sem=ici_s.at[p, d - 1], recv_sem=ici_r.at[p, d - 1],
                device_id=_chip_peer(x, y, c, d), device_id_type=MESH)

        def d2d(p, d, core):
            part = out_refs[p].at[me ^ d, rows(p, core)]
            return pltpu.make_async_remote_copy(
                src_ref=part, dst_ref=part, send_sem=d2d_s.at[p, d - 1], recv_sem=d2d_r.at[p, d - 1],
                device_id=(x, y, 1 - c), device_id_type=MESH)

        local = [pltpu.make_async_copy(in_refs[p], out_refs[p].at[me], loc.at[p]) for p in range(len(self.inputs))]
        return me, c, ici, d2d, local

    def start(self, in_refs, out_refs, sems):
        me, c, ici, d2d, local = self._copies(in_refs, out_refs, sems)
        for p in range(len(self.inputs)):
            local[p].start()
            for d in (1, 2, 3):
                ici(p, d, me, c).start()

    def wait(self, in_refs, out_refs, sems):
        me, c, ici, d2d, local = self._copies(in_refs, out_refs, sems)
        n = len(self.inputs)
        for p in range(n):
            for d in (1, 2, 3):
                ici(p, d, me ^ d, c).wait_recv()
                if self.halved[p]:
                    d2d(p, d, c).start()
        for p in range(n):
            for d in (1, 2, 3):
                if self.halved[p]:
                    d2d(p, d, 1 - c).wait_recv()
                    d2d(p, d, c).wait_send()
                ici(p, d, me, c).wait_send()
            local[p].wait()


def gather_plan(shards):
    return _GatherPlan(shards)


def reduce_plan(shards, whole=()):
    n_s = len(shards)

    def build(in_refs, out_refs, sems):
        send_sems, recv_sems, loc_sems = sems
        x, y, c = _place()
        me = 2 * x + y

        def src(p, slab):
            return in_refs[p].at[slab] if p < n_s else in_refs[p]

        def remote(p, d):
            return pltpu.make_async_remote_copy(
                src_ref=src(p, me ^ d), dst_ref=out_refs[p].at[d], send_sem=send_sems.at[p, d - 1],
                recv_sem=recv_sems.at[p, d - 1], device_id=_chip_peer(x, y, c, d), device_id_type=MESH)

        n = len(in_refs)
        send = [[remote(p, d) for d in (1, 2, 3)] for p in range(n)]
        local = [pltpu.make_async_copy(src(p, me), out_refs[p].at[0], loc_sems.at[p]) for p in range(n)]
        return send, send, local

    outs = [jax.ShapeDtypeStruct(s.shape, s.dtype) for s in shards]
    outs += [jax.ShapeDtypeStruct((N_CHIPS,) + a.shape, a.dtype) for a in whole]
    return _Plan(list(shards) + list(whole), outs, build)


class _SiblingPlan:
    def __init__(self, arrs):
        self.inputs = list(arrs)
        self.out_shape = [jax.ShapeDtypeStruct(a.shape, a.dtype) for a in arrs]
        n = len(arrs)
        self.sems = [pltpu.SemaphoreType.DMA((n,)), pltpu.SemaphoreType.DMA((n,))]

    def _copies(self, in_refs, out_refs, sems):
        x, y, c = _place()
        return [pltpu.make_async_remote_copy(
            src_ref=in_refs[p], dst_ref=out_refs[p], send_sem=sems[0].at[p], recv_sem=sems[1].at[p],
            device_id=(x, y, 1 - c), device_id_type=MESH) for p in range(len(self.inputs))]

    def start(self, in_refs, out_refs, sems):
        for cp in self._copies(in_refs, out_refs, sems):
            cp.start()

    def wait(self, in_refs, out_refs, sems):
        copies = self._copies(in_refs, out_refs, sems)
        for cp in copies:
            cp.wait_recv()
        for cp in copies:
            cp.wait_send()


def run_plan(plan, name):
    n = len(plan.inputs)

    def body(*refs):
        plan.start(refs[:n], refs[n:2 * n], refs[2 * n:])
        plan.wait(refs[:n], refs[n:2 * n], refs[2 * n:])

    return pl.pallas_call(body, name=name, in_specs=[ANY] * n, out_specs=[ANY] * n, out_shape=plan.out_shape,
                          scratch_shapes=plan.sems)(*plan.inputs)


def _call(body, plan, *, name, grid, in_specs, out_specs, out_shape, sem, scratch_shapes=(), aliases=None):
    aliases = {} if aliases is None else aliases
    single = not isinstance(out_shape, (list, tuple))
    out_specs = [out_specs] if single else list(out_specs)
    out_shape = [out_shape] if single else list(out_shape)
    n_in, n_out, n_scr = len(in_specs), len(out_specs), len(scratch_shapes)
    ci = 0 if plan is None else len(plan.inputs)

    def hosted(*refs):
        ins, cins = refs[:n_in], refs[n_in:n_in + ci]
        k = n_in + ci
        outs, couts = refs[k:k + n_out], refs[k + n_out:k + n_out + ci]
        k += n_out + ci
        scr, sems = refs[k:k + n_scr], refs[k + n_scr:]
        ids = [pl.program_id(a) for a in range(len(grid))]
        first = functools.reduce(jnp.logical_and, [i == 0 for i in ids])
        last = functools.reduce(jnp.logical_and, [i == g - 1 for i, g in zip(ids, grid)])

        @pl.when(first)
        def _():
            plan.start(cins, couts, sems)

        body(*ins, *outs, *scr)

        @pl.when(last)
        def _():
            plan.wait(cins, couts, sems)

    def run(*args):
        if plan is None:
            res = pl.pallas_call(body, name=name, grid=grid, in_specs=list(in_specs), out_specs=out_specs,
                                 out_shape=out_shape, scratch_shapes=list(scratch_shapes),
                                 input_output_aliases=aliases, compiler_params=_cparams(sem))(*args)
            return (res[0] if single else res), []
        res = pl.pallas_call(hosted, name=name, grid=grid, in_specs=list(in_specs) + [ANY] * ci,
                             out_specs=out_specs + [ANY] * ci, out_shape=out_shape + plan.out_shape,
                             scratch_shapes=list(scratch_shapes) + plan.sems, input_output_aliases=aliases,
                             compiler_params=_cparams(sem))(*args, *plan.inputs)
        return (res[0] if single else res[:n_out]), list(res[n_out:])

    return run


def norm_matmul(x, g, w, name, plan=None, tn=None):
    L, D = x.shape
    tl = min(TL_DW, L)
    if w.ndim == 3:
        nt, _, tn = w.shape
        w_spec = pl.BlockSpec((1, D, tn), lambda i, n: (n, 0, 0))
    else:
        nt = w.shape[1] // tn
        w_spec = pl.BlockSpec((D, tn), lambda i, n: (0, n))

    def body(x_ref, g_ref, w_ref, o_ref, h_ref):
        xv = x_ref[...]
        h = (xv * _rms(xv) * g_ref[...]).astype(h_ref.dtype)
        h_ref[...] = h
        o_ref[...] = _mm(h, w_ref[0] if w.ndim == 3 else w_ref[...])

    return _call(
        body, plan, name=name, grid=(L // tl, nt),
        in_specs=[pl.BlockSpec((tl, D), lambda i, n: (i, 0)), _full((1, D)), w_spec],
        out_specs=[pl.BlockSpec((tl, tn), lambda i, n: (i, n)), pl.BlockSpec((tl, D), lambda i, n: (i, 0))],
        out_shape=[jax.ShapeDtypeStruct((L, nt * tn), F32), jax.ShapeDtypeStruct((L, D), MXU_DTYPE)],
        sem=("arbitrary", "arbitrary"),
    )(x, g, w)


def matmul_residual(ys, w, x, name):
    L, D = x.shape
    tl = min(TL_PROJ, L)
    n = len(ys)
    offs = np.cumsum([0] + [y.shape[1] for y in ys])

    def body(*refs):
        y_refs, w_ref, x_ref, o_ref = refs[:n], refs[n], refs[n + 1], refs[n + 2]
        acc = x_ref[...]
        for k in range(n):
            acc = acc + _mm(y_refs[k][...], w_ref[offs[k]:offs[k + 1], :])
        o_ref[...] = acc

    return pl.pallas_call(
        body, name=name, grid=(L // tl,),
        in_specs=[pl.BlockSpec((tl, y.shape[1]), lambda i: (i, 0)) for y in ys]
        + [_full(w.shape), pl.BlockSpec((tl, D), lambda i: (i, 0))],
        out_specs=pl.BlockSpec((tl, D), lambda i: (i, 0)),
        out_shape=jax.ShapeDtypeStruct((L, D), F32),
        compiler_params=_cparams(("arbitrary",)),
    )(*ys, w, x)


def out_proj_loss(y, w, x, gf, tgt, name):
    L, K = y.shape
    D = w.shape[1]
    tl = min(TL_PROJ, L)

    def body(y_ref, w_ref, x_ref, gf_ref, t_ref, dx_ref, loss_ref, dg_ref):
        @pl.when(pl.program_id(0) == 0)
        def _():
            loss_ref[...] = jnp.zeros_like(loss_ref)
            dg_ref[...] = jnp.zeros_like(dg_ref)

        x2 = x_ref[...] + _mm(y_ref[...], w_ref[...])
        r = _rms(x2)
        xn = x2 * r
        e = xn * gf_ref[...] - t_ref[...]
        loss_ref[...] += (0.5 / D) * jnp.sum(e * e)
        dout = e * (1.0 / D)
        dg_ref[...] += jnp.sum(dout * xn, axis=0, keepdims=True)
        dxn = dout * gf_ref[...]
        dx_ref[...] = r * (dxn - xn * jnp.mean(dxn * xn, axis=-1, keepdims=True))

    return pl.pallas_call(
        body, name=name, grid=(L // tl,),
        in_specs=[pl.BlockSpec((tl, K), lambda i: (i, 0)), _full((K, D)),
                  pl.BlockSpec((tl, D), lambda i: (i, 0)), _full((1, D)),
                  pl.BlockSpec((tl, D), lambda i: (i, 0))],
        out_specs=[pl.BlockSpec((tl, D), lambda i: (i, 0)), _full((8, 128)), _full((1, D))],
        out_shape=[jax.ShapeDtypeStruct((L, D), F32), jax.ShapeDtypeStruct((8, 128), F32),
                   jax.ShapeDtypeStruct((1, D), F32)],
        compiler_params=_cparams(("arbitrary",)),
    )(y, w, x, gf, tgt)


def out_proj_bwd(dx, w, ys, name):
    L, D = dx.shape
    K = w.shape[0]
    tl = min(TL_PROJ, L)
    n = len(ys)
    offs = np.cumsum([0] + [y.shape[1] for y in ys])

    def body(*refs):
        dx_ref, w_ref, y_refs = refs[0], refs[1], refs[2:2 + n]
        dy_refs, dw_ref = refs[2 + n:2 + 2 * n], refs[2 + 2 * n]

        @pl.when(pl.program_id(0) == 0)
        def _():
            dw_ref[...] = jnp.zeros_like(dw_ref)

        dxv = dx_ref[...]
        for k in range(n):
            dy_refs[k][...] = _mm_nt(dxv, w_ref[offs[k]:offs[k + 1], :])
            dw_ref[offs[k]:offs[k + 1], :] += _mm_tn(y_refs[k][...], dxv)

    y_specs = [pl.BlockSpec((tl, y.shape[1]), lambda i: (i, 0)) for y in ys]
    return pl.pallas_call(
        body, name=name, grid=(L // tl,),
        in_specs=[pl.BlockSpec((tl, D), lambda i: (i, 0)), _full((K, D))] + y_specs,
        out_specs=y_specs + [_full((K, D))],
        out_shape=[jax.ShapeDtypeStruct(y.shape, F32) for y in ys] + [jax.ShapeDtypeStruct((K, D), F32)],
        compiler_params=_cparams(("arbitrary",)),
    )(dx, w, *ys)


def in_proj_bwd_dx(x, g, dps, ws, dres, name, plan=None):
    L, D = x.shape
    tl = min(TL_PROJ, L)
    n = len(dps)

    def body(*refs):
        x_ref, g_ref, dres_ref = refs[:3]
        dp_refs, w_refs = refs[3:3 + n], refs[3 + n:3 + 2 * n]
        dx_ref, dg_ref = refs[3 + 2 * n:]

        @pl.when(pl.program_id(0) == 0)
        def _():
            dg_ref[...] = jnp.zeros_like(dg_ref)

        dh = None
        for dp_ref, w_ref, w in zip(dp_refs, w_refs, ws):
            if w.ndim == 3:
                tn = w.shape[2]
                parts = [_mm_nt(dp_ref[:, tn * k:tn * (k + 1)], w_ref[k]) for k in range(w.shape[0])]
            else:
                parts = [_mm_nt(dp_ref[...], w_ref[...])]
            for part in parts:
                dh = part if dh is None else dh + part
        xv = x_ref[...]
        r = _rms(xv)
        xn = xv * r
        dg_ref[...] += jnp.sum(dh * xn, axis=0, keepdims=True)
        dxn = dh * g_ref[...]
        dx_ref[...] = dres_ref[...] + r * (dxn - xn * jnp.mean(dxn * xn, axis=-1, keepdims=True))

    return _call(
        body, plan, name=name, grid=(L // tl,),
        in_specs=[pl.BlockSpec((tl, D), lambda i: (i, 0)), _full((1, D)), pl.BlockSpec((tl, D), lambda i: (i, 0))]
        + [pl.BlockSpec((tl, dp.shape[1]), lambda i: (i, 0)) for dp in dps] + [_full(w.shape) for w in ws],
        out_specs=[pl.BlockSpec((tl, D), lambda i: (i, 0)), _full((1, D))],
        out_shape=[jax.ShapeDtypeStruct((L, D), F32), jax.ShapeDtypeStruct((1, D), F32)],
        sem=("arbitrary",),
    )(x, g, dres, *dps, *ws)


def in_proj_bwd_dw(h, dp, name, tn, first=0, into=None, dtype=F32, plan=None):
    L, D = h.shape
    tl = min(TL_DW, L)
    wb = EVEN_IN // N_CHIPS
    per = wb // tn
    count = dp.shape[1] // tn
    last = L // tl - 1

    def body(*refs):
        h_ref, dp_ref, dw_ref, acc = refs[0], refs[1], refs[-2], refs[-1]

        @pl.when(pl.program_id(1) == 0)
        def _():
            acc[...] = jnp.zeros_like(acc)

        acc[...] += _mm_tn(h_ref[...], dp_ref[...])

        @pl.when(pl.program_id(1) == last)
        def _():
            dw_ref[0] = acc[...].astype(dw_ref.dtype)

    ins = [h, dp] + ([] if into is None else [into])
    return _call(
        body, plan, name=name, grid=(count, L // tl),
        in_specs=[pl.BlockSpec((tl, D), lambda n, i: (i, 0)), pl.BlockSpec((tl, tn), lambda n, i: (i, n))]
        + ([] if into is None else [ANY]),
        out_specs=pl.BlockSpec((1, D, tn), lambda n, i: ((n + first) // per, 0, (n + first) % per)),
        out_shape=jax.ShapeDtypeStruct((N_CHIPS, D, wb), dtype),
        scratch_shapes=[pltpu.VMEM((D, tn), F32)],
        aliases={} if into is None else {2: 0},
        sem=("arbitrary", "arbitrary"),
    )(*ins)


def _s5_param_fn(lam_re, lam_im, log_dt, b_re, b_im):
    lr = jnp.minimum(lam_re, -1e-4)
    li = lam_im
    dt = jnp.exp(log_dt)
    mag = jnp.exp(lr * dt)
    ab_re = mag * jnp.cos(li * dt)
    ab_im = mag * jnp.sin(li * dt)
    den = lr * lr + li * li
    n_re = ab_re - 1.0
    n_im = ab_im
    z_re = (n_re * lr + n_im * li) / den
    z_im = (n_im * lr - n_re * li) / den
    bb_re = z_re[None] * b_re - z_im[None] * b_im
    bb_im = z_re[None] * b_im + z_im[None] * b_re
    return ab_re, ab_im, bb_re, bb_im


def s5_params_fwd(lam_re, lam_im, log_dt, b_re, b_im, span):
    G, P = lam_re.shape
    H = b_re.shape[0]
    assert span & (span - 1) == 0

    def body(lr_ref, li_ref, dt_ref, br_ref, bi_ref, abr_ref, abi_ref, bbr_ref, bbi_ref, pr_ref, pi_ref):
        ab_re, ab_im, bb_re, bb_im = _s5_param_fn(lr_ref[...], li_ref[...], dt_ref[...], br_ref[...], bi_ref[...])
        abr_ref[...] = ab_re
        abi_ref[...] = ab_im
        bbr_ref[...] = bb_re
        bbi_ref[...] = bb_im
        cr, ci = ab_re, ab_im
        for _ in range(span.bit_length() - 1):
            cr, ci = cr * cr - ci * ci, 2.0 * cr * ci
        pr_ref[...] = cr
        pi_ref[...] = ci

    shp = lambda *s: jax.ShapeDtypeStruct(s, F32)
    return pl.pallas_call(
        body, name="s5_params_fwd",
        out_shape=[shp(G, P), shp(G, P), shp(H, G, P), shp(H, G, P), shp(G, P), shp(G, P)],
    )(lam_re, lam_im, log_dt, b_re, b_im)


def s5_params_bwd(lam_re, lam_im, log_dt, b_re, b_im, d_ab_re, d_ab_im, d_bb_re, d_bb_im):
    G, P = lam_re.shape
    H = b_re.shape[0]

    def body(lr_ref, li_ref, dt_ref, br_ref, bi_ref, g0, g1, g2, g3, o0, o1, o2, o3, o4):
        prim = (lr_ref[...], li_ref[...], dt_ref[...], br_ref[...], bi_ref[...])
        _, vjp = jax.vjp(_s5_param_fn, *prim)
        d = vjp((jnp.sum(g0[...], axis=0), jnp.sum(g1[...], axis=0), g2[...], g3[...]))
        o0[...], o1[...], o2[...], o3[...], o4[...] = d

    shp = lambda *s: jax.ShapeDtypeStruct(s, F32)
    return pl.pallas_call(
        body, name="s5_params_bwd",
        out_shape=[shp(G, P), shp(G, P), shp(G, 1), shp(H, G, P), shp(H, G, P)],
    )(lam_re, lam_im, log_dt, b_re, b_im, d_ab_re, d_ab_im, d_bb_re, d_bb_im)


def stream_order(a, tl):
    L, C = a.shape
    return a.reshape(L // tl, 8, tl // 8, C).transpose(0, 2, 1, 3).reshape(L, C)


def token_order(a, tl):
    L, C = a.shape
    return a.reshape(L // tl, tl // 8, 8, C).transpose(0, 2, 1, 3).reshape(L, C)


_LANE_BLK = 1024


def _cmul_add(ar, ai, xr, xi, br, bi):
    return br + (ar * xr - ai * xi), bi + (ar * xi + ai * xr)


def _cmulc_add(ar, ai, xr, xi, br, bi):
    return br + (ar * xr + ai * xi), bi + (ar * xi - ai * xr)


def _s5_states(u, wbd_ref, a_re, a_im, at_re, at_im, s_re, s_im, e_re, e_im, c0_re, c0_im, tl):
    t8 = tl // 8
    for k in range(S5_KBLK):
        bu = _mm(u[:, 128 * k:128 * (k + 1)], wbd_ref[k])
        s_re[:, 512 * k:512 * (k + 1)] = bu[:, :512]
        s_im[:, 512 * k:512 * (k + 1)] = bu[:, 512:]
    outs_re, outs_im = [], []
    for b in range(S5_LANES // _LANE_BLK):
        lanes = slice(_LANE_BLK * b, _LANE_BLK * (b + 1))
        ar = jnp.broadcast_to(a_re[:, lanes], (8, _LANE_BLK))
        ai = jnp.broadcast_to(a_im[:, lanes], (8, _LANE_BLK))

        def local(i, carry, lanes=lanes, ar=ar, ai=ai):
            r = pl.multiple_of(i * 8, 8)
            sr, si = _cmul_add(ar, ai, carry[0], carry[1], s_re[pl.ds(r, 8), lanes], s_im[pl.ds(r, 8), lanes])
            s_re[pl.ds(r, 8), lanes] = sr
            s_im[pl.ds(r, 8), lanes] = si
            return sr, si

        zero = jnp.zeros((8, _LANE_BLK), F32)
        fr, fi = lax.fori_loop(0, t8, local, (zero, zero), unroll=True)
        tr, ti = at_re[:, lanes], at_im[:, lanes]
        er, ei = c0_re[:, lanes], c0_im[:, lanes]
        ers, eis = [er], [ei]
        for j in range(8):
            er, ei = _cmul_add(tr, ti, er, ei, fr[j:j + 1], fi[j:j + 1])
            ers.append(er)
            eis.append(ei)
        outs_re.append(ers[8])
        outs_im.append(eis[8])
        ent_r, ent_i = jnp.concatenate(ers[:8], axis=0), jnp.concatenate(eis[:8], axis=0)
        e_re[:, lanes] = ent_r
        e_im[:, lanes] = ent_i

        def fix(i, carry, lanes=lanes, ar=ar, ai=ai):
            r = pl.multiple_of(i * 8, 8)
            zr, zi = ar * carry[0] - ai * carry[1], ar * carry[1] + ai * carry[0]
            s_re[pl.ds(r, 8), lanes] = s_re[pl.ds(r, 8), lanes] + zr
            s_im[pl.ds(r, 8), lanes] = s_im[pl.ds(r, 8), lanes] + zi
            return zr, zi

        lax.fori_loop(0, t8, fix, (ent_r, ent_i), unroll=True)
    return jnp.concatenate(outs_re, axis=1), jnp.concatenate(outs_im, axis=1)


def _s5_readout(s_re, s_im, cre_ref, cim_ref):
    ys = []
    for k in range(S5_KBLK):
        lanes = slice(512 * k, 512 * (k + 1))
        ys.append(_mm(s_re[:, lanes], cre_ref[k]) - _mm(s_im[:, lanes], cim_ref[k]))
    return jnp.concatenate(ys, axis=1)


def s5_forward(p, wbd, cre, cim, atab, d_skip, w_glu, b_glu, plan=None):
    L = p.shape[0]
    tl = min(TL_S5, L)
    nch = L // tl

    def body(u_ref, z_ref, wbd_ref, cre_ref, cim_ref, at_ref, d_ref, wg_ref, bg_ref,
             ya_ref, st_re_ref, st_im_ref, sv_re_ref, sv_im_ref, s_re, s_im, e_re, e_im, car_re, car_im):
        @pl.when(pl.program_id(0) == 0)
        def _():
            car_re[...] = jnp.zeros_like(car_re)
            car_im[...] = jnp.zeros_like(car_im)

        c0_re, c0_im = car_re[...], car_im[...]
        st_re_ref[0] = c0_re
        st_im_ref[0] = c0_im
        u = u_ref[...]
        x_re, x_im = _s5_states(u, wbd_ref, at_ref[0:1], at_ref[1:2], at_ref[2:3], at_ref[3:4],
                                s_re, s_im, e_re, e_im, c0_re, c0_im, tl)
        car_re[...] = x_re
        car_im[...] = x_im
        sv_re_ref[...] = s_re[...].astype(sv_re_ref.dtype)
        sv_im_ref[...] = s_im[...].astype(sv_im_ref.dtype)
        y = _s5_readout(sv_re_ref, sv_im_ref, cre_ref, cim_ref) + d_ref[...] * u
        yg = _gelu(y)
        gate = _sigmoid(_mm(yg, wg_ref[...]) + bg_ref[...])
        sz, _ = _silu_and_grad(z_ref[...])
        ya_ref[...] = (yg * gate * sz).astype(ya_ref.dtype)

    return _call(
        body, plan, name="s5_forward", grid=(nch,),
        in_specs=[pl.BlockSpec((tl, 1024), lambda i: (i, 0)), pl.BlockSpec((tl, 1024), lambda i: (i, 1)),
                  _full(wbd.shape), _full(cre.shape), _full(cim.shape), _full(atab.shape),
                  _full((1, 1024)), _full((1024, 1024)), _full((1, 1024))],
        out_specs=[pl.BlockSpec((tl, 1024), lambda i: (i, 0)),
                   pl.BlockSpec((1, 1, S5_LANES), lambda i: (i, 0, 0)),
                   pl.BlockSpec((1, 1, S5_LANES), lambda i: (i, 0, 0)),
                   pl.BlockSpec((tl, S5_LANES), lambda i: (i, 0)), pl.BlockSpec((tl, S5_LANES), lambda i: (i, 0))],
        out_shape=[jax.ShapeDtypeStruct((L, 1024), MXU_DTYPE),
                   jax.ShapeDtypeStruct((nch, 1, S5_LANES), F32), jax.ShapeDtypeStruct((nch, 1, S5_LANES), F32),
                   jax.ShapeDtypeStruct((L, S5_LANES), MXU_DTYPE), jax.ShapeDtypeStruct((L, S5_LANES), MXU_DTYPE)],
        scratch_shapes=[pltpu.VMEM((tl, S5_LANES), F32), pltpu.VMEM((tl, S5_LANES), F32),
                        pltpu.VMEM((8, S5_LANES), F32), pltpu.VMEM((8, S5_LANES), F32),
                        pltpu.VMEM((1, S5_LANES), F32), pltpu.VMEM((1, S5_LANES), F32)],
        sem=("arbitrary",),
    )(p, p, wbd, cre, cim, atab, d_skip, w_glu, b_glu)


def s5_backward(p, dya, st_re, st_im, sv_re, sv_im, wbd, cre, cim, atab, d_skip, w_glu, b_glu, plan=None):
    L = p.shape[0]
    tl = min(TL_S5, L)
    t8 = tl // 8
    nch = L // tl
    rev = lambda i: (nch - 1 - i, 0)
    rev1 = lambda i: (nch - 1 - i, 1)
    rev3 = lambda i: (nch - 1 - i, 0, 0)
    ct_shape = (S5_KBLK, cre.shape[2], cre.shape[1])

    def body(u_ref, z_ref, dya_ref, str_ref, sti_ref, s_re, s_im, wbd_ref, cre_ref, cim_ref, at_ref,
             d_ref, wg_ref, bg_ref,
             dp_ref, dwbd_ref, dcre_ref, dcim_ref, dabr_ref, dabi_ref, dd_ref, dwg_ref, dbg_ref,
             g_re, g_im, car_re, car_im):
        @pl.when(pl.program_id(0) == 0)
        def _():
            car_re[...] = jnp.zeros_like(car_re)
            car_im[...] = jnp.zeros_like(car_im)
            for r in (dwbd_ref, dcre_ref, dcim_ref, dabr_ref, dabi_ref, dd_ref, dwg_ref, dbg_ref):
                r[...] = jnp.zeros_like(r)

        u = u_ref[...]
        a_re, a_im, at_re, at_im = at_ref[0:1], at_ref[1:2], at_ref[2:3], at_ref[3:4]
        y = _s5_readout(s_re, s_im, cre_ref, cim_ref) + d_ref[...] * u
        yg, dyg = _gelu_and_grad(y)
        gate = _sigmoid(_mm(yg, wg_ref[...]) + bg_ref[...])
        sz, dsz = _silu_and_grad(z_ref[...])
        dya = dya_ref[...]
        s5out = yg * gate
        dp_ref[:, 1024:] = (dya * s5out * dsz).astype(dp_ref.dtype)
        ds5 = dya * sz
        dt = ds5 * yg * gate * (1.0 - gate)
        dwg_ref[...] += _mm_tn(yg, dt)
        dbg_ref[...] += jnp.sum(dt, axis=0, keepdims=True)
        dyv = (ds5 * gate + _mm_nt(dt, wg_ref[...])) * dyg
        dd_ref[...] += jnp.sum(dyv * u, axis=0, keepdims=True)

        for k in range(S5_KBLK):
            lanes = slice(512 * k, 512 * (k + 1))
            dyk = dyv[:, 128 * k:128 * (k + 1)]
            g_re[:, lanes] = _mm_nt(dyk, cre_ref[k])
            g_im[:, lanes] = -_mm_nt(dyk, cim_ref[k])
            dcre_ref[k] += _mm_tn(dyk, s_re[:, lanes])
            dcim_ref[k] -= _mm_tn(dyk, s_im[:, lanes])

        for b in range(S5_LANES // _LANE_BLK):
            lanes = slice(_LANE_BLK * b, _LANE_BLK * (b + 1))
            ar = jnp.broadcast_to(a_re[:, lanes], (8, _LANE_BLK))
            ai = jnp.broadcast_to(a_im[:, lanes], (8, _LANE_BLK))

            def local(j, carry, lanes=lanes, ar=ar, ai=ai):
                r = pl.multiple_of((t8 - 1 - j) * 8, 8)
                gr, gi = _cmulc_add(ar, ai, carry[0], carry[1], g_re[pl.ds(r, 8), lanes], g_im[pl.ds(r, 8), lanes])
                g_re[pl.ds(r, 8), lanes] = gr
                g_im[pl.ds(r, 8), lanes] = gi
                return gr, gi

            zero = jnp.zeros((8, _LANE_BLK), F32)
            fr, fi = lax.fori_loop(0, t8, local, (zero, zero), unroll=True)
            tr, ti = at_re[:, lanes], at_im[:, lanes]
            hr, hi = car_re[:, lanes], car_im[:, lanes]
            hrs, his = [hr], [hi]
            for j in range(7, -1, -1):
                hr, hi = _cmulc_add(tr, ti, hr, hi, fr[j:j + 1], fi[j:j + 1])
                hrs.append(hr)
                his.append(hi)
            car_re[:, lanes] = hrs[8]
            car_im[:, lanes] = his[8]
            in_r = jnp.concatenate(hrs[7::-1], axis=0)
            in_i = jnp.concatenate(his[7::-1], axis=0)

            wr, wi, nr, ni, accr, acci = in_r, in_i, zero, zero, zero, zero
            for pair in range(t8 // 2 - 1, -1, -1):
                rows = slice(16 * pair, 16 * pair + 16)
                s16r, s16i = s_re[rows, lanes].astype(F32), s_im[rows, lanes].astype(F32)
                for half in (1, 0):
                    r = 16 * pair + 8 * half
                    sr, si = s16r[8 * half:8 * half + 8], s16i[8 * half:8 * half + 8]
                    accr, acci = accr + (sr * nr + si * ni), acci + (sr * ni - si * nr)
                    wr, wi = ar * wr + ai * wi, ar * wi - ai * wr
                    nr, ni = g_re[r:r + 8, lanes] + wr, g_im[r:r + 8, lanes] + wi
                    g_re[r:r + 8, lanes] = nr
                    g_im[r:r + 8, lanes] = ni
            lr, li = s_re[tl - 16:tl, lanes].astype(F32)[8:], s_im[tl - 16:tl, lanes].astype(F32)[8:]
            row0 = lax.broadcasted_iota(jnp.int32, (8, _LANE_BLK), 0) == 0
            sr = jnp.where(row0, jnp.broadcast_to(str_ref[0][:, lanes], (8, _LANE_BLK)), pltpu.roll(lr, 1, 0))
            si = jnp.where(row0, jnp.broadcast_to(sti_ref[0][:, lanes], (8, _LANE_BLK)), pltpu.roll(li, 1, 0))
            dabr_ref[:, lanes] += accr + (sr * nr + si * ni)
            dabi_ref[:, lanes] += acci + (sr * ni - si * nr)

        dus = []
        for k in range(S5_KBLK):
            lanes = slice(512 * k, 512 * (k + 1))
            g = jnp.concatenate([g_re[:, lanes], g_im[:, lanes]], axis=1)
            dwbd_ref[k] += _mm_tn(u[:, 128 * k:128 * (k + 1)], g)
            dus.append(_mm_nt(g, wbd_ref[k]))
        du = jnp.concatenate(dus, axis=1) + dyv * d_ref[...]
        dp_ref[:, :1024] = du.astype(dp_ref.dtype)

    shp = lambda *s: jax.ShapeDtypeStruct(s, F32)
    return _call(
        body, plan, name="s5_backward", grid=(nch,),
        in_specs=[pl.BlockSpec((tl, 1024), rev), pl.BlockSpec((tl, 1024), rev1), pl.BlockSpec((tl, 1024), rev),
                  pl.BlockSpec((1, 1, S5_LANES), rev3), pl.BlockSpec((1, 1, S5_LANES), rev3),
                  pl.BlockSpec((tl, S5_LANES), rev), pl.BlockSpec((tl, S5_LANES), rev),
                  _full(wbd.shape), _full(cre.shape), _full(cim.shape), _full(atab.shape),
                  _full((1, 1024)), _full((1024, 1024)), _full((1, 1024))],
        out_specs=[pl.BlockSpec((tl, 2048), rev), _full(wbd.shape), _full(ct_shape), _full(ct_shape),
                   _full((8, S5_LANES)), _full((8, S5_LANES)), _full((1, 1024)), _full((1024, 1024)), _full((1, 1024))],
        out_shape=[jax.ShapeDtypeStruct((L, 2048), MXU_DTYPE), shp(*wbd.shape), shp(*ct_shape), shp(*ct_shape),
                   shp(8, S5_LANES), shp(8, S5_LANES), shp(1, 1024), shp(1024, 1024), shp(1, 1024)],
        scratch_shapes=[pltpu.VMEM((tl, S5_LANES), F32), pltpu.VMEM((tl, S5_LANES), F32),
                        pltpu.VMEM((1, S5_LANES), F32), pltpu.VMEM((1, S5_LANES), F32)],
        sem=("arbitrary",),
    )(p, p, dya, st_re, st_im, sv_re, sv_im, wbd, cre, cim, atab, d_skip, w_glu, b_glu)


def _block_diag(w, rows_first):
    g8 = w.reshape(S5_KBLK, 8, w.shape[1], w.shape[2])
    eye = jnp.eye(8, dtype=w.dtype)
    out = jnp.einsum('kgab,fg->kfagb', g8, eye)
    return out.reshape(S5_KBLK, 8 * w.shape[1], 8 * w.shape[2])


def _block_diag_extract(wbd, a, b):
    w5 = wbd.reshape(S5_KBLK, 8, a, 8, b)
    idx = jnp.arange(8)
    return w5[:, idx, :, idx, :].transpose(1, 0, 2, 3).reshape(S5_GROUPS, a, b)


def _ret_constants():
    log_g = np.log1p(-np.exp2(-5.0 - np.arange(RET_HEADS, dtype=np.float32))).astype(np.float32)
    idx = np.arange(RET_CHUNK, dtype=np.float32)
    diff = idx[:, None] - idx[None, :]
    decay = np.where(diff >= 0, np.exp(log_g[:, None, None] * np.maximum(diff, 0.0)), 0.0).astype(np.float32)
    xi = np.exp(log_g[None, :] * (idx[:, None] + 1.0)).astype(np.float32)
    zeta = np.exp(log_g[None, :] * (RET_CHUNK - 1.0 - idx[:, None])).astype(np.float32)
    chunk_decay = np.exp(log_g * RET_CHUNK).astype(np.float32)
    return decay, xi, zeta, chunk_decay


def _rope_tables(L):
    half = RET_DK // 2
    inv = ROPE_BASE ** (-jnp.arange(half, dtype=F32) / half)
    ang = jnp.arange(L, dtype=F32)[:, None] * inv[None, :]
    return jnp.cos(ang), jnp.sin(ang)


def _rot(xh, cos, sin):
    x1, x2 = xh[:, :128], xh[:, 128:]
    return jnp.concatenate([x1 * cos - x2 * sin, x1 * sin + x2 * cos], axis=1)


def _rot_t(dh, cos, sin):
    d1, d2 = dh[:, :128], dh[:, 128:]
    return jnp.concatenate([d1 * cos + d2 * sin, d2 * cos - d1 * sin], axis=1)


RET_PER_STEP = 2


def _ret_setup(L):
    nc = L // RET_CHUNK
    per = RET_PER_STEP if nc % RET_PER_STEP == 0 else 1
    decay_np, xi_np, zeta_np, cd_np = _ret_constants()
    tables = (jnp.asarray(decay_np), jnp.asarray(np.tile(xi_np, (per, 1))), jnp.asarray(np.tile(zeta_np, (per, 1))))
    return nc // per, per, tables, [float(c) for c in cd_np]


def _ret_rows(q_ref, k_ref, v_ref, cos_ref, sin_ref, xi_ref, zeta_ref):
    H = range(RET_HEADS)
    hs = [slice(RET_DK * h, RET_DK * (h + 1)) for h in H]
    cs, sn = cos_ref[...], sin_ref[...]
    qh = [_rot(q_ref[:, hs[h]], cs, sn) for h in H]
    kh = [_rot(k_ref[:, hs[h]], cs, sn) * (RET_DK ** -0.5) for h in H]
    vh = [v_ref[:, hs[h]] for h in H]
    qx = [qh[h] * xi_ref[:, h:h + 1] for h in H]
    kz = [kh[h] * zeta_ref[:, h:h + 1] for h in H]
    return hs, cs, sn, qh, kh, vh, qx, kz


def _ret_normed(qh, kh, vh, qx, dec_ref, prevs, per):
    H, C = range(RET_HEADS), range(per)
    rs = [slice(RET_CHUNK * c, RET_CHUNK * (c + 1)) for c in C]
    sc = [[_mm_nt(qh[h][rs[c]], kh[h][rs[c]]) * dec_ref[h] for h in H] for c in C]
    inner = [[_mm(sc[c][h], vh[h][rs[c]]) for h in H] for c in C]
    cross = [[_mm(qx[h][rs[c]], prevs[c][h]) for h in H] for c in C]
    o = [jnp.concatenate([inner[c][h] + cross[c][h] for c in C], axis=0) for h in H]
    oc = [o[h] - jnp.mean(o[h], axis=-1, keepdims=True) for h in H]
    rstd = [lax.rsqrt(jnp.mean(oc[h] * oc[h], axis=-1, keepdims=True) + NORM_EPS) for h in H]
    on = [oc[h] * rstd[h] for h in H]
    return rs, sc, rstd, on


def retention_forward(p, cos, sin, gain):
    L = p.shape[0]
    steps, per, (decay, xi, zeta), cd = _ret_setup(L)
    rows = RET_CHUNK * per

    def body(q_ref, k_ref, v_ref, z_ref, cos_ref, sin_ref, dec_ref, xi_ref, zeta_ref, gain_ref,
             yb_ref, prev_ref, state):
        @pl.when(pl.program_id(0) == 0)
        def _():
            state[...] = jnp.zeros_like(state)

        H, C = range(RET_HEADS), range(per)
        hs, cs, sn, qh, kh, vh, qx, kz = _ret_rows(q_ref, k_ref, v_ref, cos_ref, sin_ref, xi_ref, zeta_ref)
        prevs = [[state[h] for h in H]]
        for c in C:
            rs_c = slice(RET_CHUNK * c, RET_CHUNK * (c + 1))
            prevs.append([prevs[c][h] * cd[h] + _mm_tn(kz[h][rs_c], vh[h][rs_c]) for h in H])
        _, _, _, on = _ret_normed(qh, kh, vh, qx, dec_ref, prevs, per)
        sz, _ = _silu_and_grad(z_ref[...])
        for h in H:
            for c in C:
                prev_ref[c, h] = prevs[c][h].astype(prev_ref.dtype)
            state[h] = prevs[per][h]
            yb_ref[:, hs[h]] = (on[h] * gain_ref[:, hs[h]] * sz[:, hs[h]]).astype(yb_ref.dtype)

    blk = lambda c: pl.BlockSpec((rows, 1024), lambda i, c=c: (i, c))
    return pl.pallas_call(
        body, name="retention_forward", grid=(steps,),
        in_specs=[blk(0), blk(1), blk(2), blk(3),
                  pl.BlockSpec((rows, 128), lambda i: (i, 0)), pl.BlockSpec((rows, 128), lambda i: (i, 0)),
                  _full(decay.shape), _full(xi.shape), _full(zeta.shape), _full((1, 1024))],
        out_specs=[pl.BlockSpec((rows, 1024), lambda i: (i, 0)),
                   pl.BlockSpec((per, RET_HEADS, RET_DK, RET_DK), lambda i: (i, 0, 0, 0))],
        out_shape=[jax.ShapeDtypeStruct((L, 1024), MXU_DTYPE),
                   jax.ShapeDtypeStruct((steps * per, RET_HEADS, RET_DK, RET_DK), MXU_DTYPE)],
        scratch_shapes=[pltpu.VMEM((RET_HEADS, RET_DK, RET_DK), F32)],
        compiler_params=_cparams(("arbitrary",)),
    )(p, p, p, p, cos, sin, decay, xi, zeta, gain)


def retention_backward(p, dy, prevs, cos, sin, gain, plan=None):
    L = p.shape[0]
    steps, per, (decay, xi, zeta), cd = _ret_setup(L)
    rows = RET_CHUNK * per
    scale = RET_DK ** -0.5

    def body(q_ref, k_ref, v_ref, z_ref, dyb_ref, prev_ref, cos_ref, sin_ref, dec_ref, xi_ref, zeta_ref, gain_ref,
             dp_ref, dgain_ref, dstate):
        @pl.when(pl.program_id(0) == 0)
        def _():
            dstate[...] = jnp.zeros_like(dstate)
            dgain_ref[...] = jnp.zeros_like(dgain_ref)

        H, C = range(RET_HEADS), range(per)
        hs, cs, sn, qh, kh, vh, qx, kz = _ret_rows(q_ref, k_ref, v_ref, cos_ref, sin_ref, xi_ref, zeta_ref)
        prevs = [[prev_ref[c, h] for h in H] for c in C]
        rs, sc, rstd, on = _ret_normed(qh, kh, vh, qx, dec_ref, prevs, per)
        sz, dsz = _silu_and_grad(z_ref[...])
        dyb = dyb_ref[...]
        dong = [dyb[:, hs[h]] * sz[:, hs[h]] for h in H]
        don = [dong[h] * gain_ref[:, hs[h]] for h in H]
        do = [rstd[h] * (don[h] - jnp.mean(don[h], axis=-1, keepdims=True)
                         - on[h] * jnp.mean(don[h] * on[h], axis=-1, keepdims=True)) for h in H]
        dsc = [[_mm_nt(do[h][rs[c]], vh[h][rs[c]]) * dec_ref[h] for h in H] for c in C]
        dq_st = [[_mm_nt(do[h][rs[c]], prevs[c][h]) for h in H] for c in C]
        dnew = [[_mm_tn(qx[h][rs[c]], do[h][rs[c]]) for h in H] for c in C]
        dsts = [None] * per + [[dstate[h] for h in H]]
        for c in reversed(C):
            dsts[c] = [dsts[c + 1][h] * cd[h] + dnew[c][h] for h in H]
        dk_st = [[_mm_nt(vh[h][rs[c]], dsts[c + 1][h]) for h in H] for c in C]
        dv_st = [[_mm(kz[h][rs[c]], dsts[c + 1][h]) for h in H] for c in C]
        rows_of = lambda parts: jnp.concatenate(parts, axis=0)
        dqh = [rows_of([_mm(dsc[c][h], kh[h][rs[c]]) for c in C])
               + rows_of([dq_st[c][h] for c in C]) * xi_ref[:, h:h + 1] for h in H]
        dkh = [rows_of([_mm_tn(dsc[c][h], qh[h][rs[c]]) for c in C])
               + rows_of([dk_st[c][h] for c in C]) * zeta_ref[:, h:h + 1] for h in H]
        dvh = [rows_of([_mm_tn(sc[c][h], do[h][rs[c]]) + dv_st[c][h] for c in C]) for h in H]
        for h in H:
            dstate[h] = dsts[0][h]
            dgain_ref[:, hs[h]] += jnp.sum(dong[h] * on[h], axis=0, keepdims=True)
            dp_ref[:, hs[h]] = _rot_t(dqh[h], cs, sn).astype(dp_ref.dtype)
            dp_ref[:, 1024 + RET_DK * h:1024 + RET_DK * (h + 1)] = (_rot_t(dkh[h], cs, sn) * scale).astype(dp_ref.dtype)
            dp_ref[:, 2048 + RET_DK * h:2048 + RET_DK * (h + 1)] = dvh[h].astype(dp_ref.dtype)
            dp_ref[:, 3072 + RET_DK * h:3072 + RET_DK * (h + 1)] = (
                dyb[:, hs[h]] * on[h] * gain_ref[:, hs[h]] * dsz[:, hs[h]]).astype(dp_ref.dtype)

    blk = lambda c: pl.BlockSpec((rows, 1024), lambda i, c=c: (steps - 1 - i, c))
    tab = pl.BlockSpec((rows, 128), lambda i: (steps - 1 - i, 0))
    return _call(
        body, plan, name="retention_backward", grid=(steps,),
        in_specs=[blk(0), blk(1), blk(2), blk(3), blk(0),
                  pl.BlockSpec((per, RET_HEADS, RET_DK, RET_DK), lambda i: (steps - 1 - i, 0, 0, 0)),
                  tab, tab, _full(decay.shape), _full(xi.shape), _full(zeta.shape), _full((1, 1024))],
        out_specs=[pl.BlockSpec((rows, 4096), lambda i: (steps - 1 - i, 0)), _full((1, 1024))],
        out_shape=[jax.ShapeDtypeStruct((L, 4096), MXU_DTYPE), jax.ShapeDtypeStruct((1, 1024), F32)],
        scratch_shapes=[pltpu.VMEM((RET_HEADS, RET_DK, RET_DK), F32)],
        sem=("arbitrary",),
    )(p, p, p, p, dy, prevs, cos, sin, decay, xi, zeta, gain)


def _sgu_mix(p_ref, gain_ref, wm_ref, bt_ref, tl):
    pu, pv, z = p_ref[:, :2048], p_ref[:, 2048:4096], p_ref[:, 4096:]
    (u, du), (v, dv) = _gelu_and_grad(pu), _gelu_and_grad(pv)
    mu = jnp.mean(v, axis=-1, keepdims=True)
    vc = v - mu
    rstd = lax.rsqrt(jnp.mean(vc * vc, axis=-1, keepdims=True) + NORM_EPS)
    vn = vc * rstd
    vg = vn * gain_ref[...]
    mask = (lax.broadcasted_iota(jnp.int32, (SGU_CHUNK, SGU_CHUNK), 0)
            >= lax.broadcasted_iota(jnp.int32, (SGU_CHUNK, SGU_CHUNK), 1))
    wms = [jnp.where(mask, wm_ref[g], 0.0) for g in range(SGU_GROUPS)]
    rows = []
    for c in range(tl // SGU_CHUNK):
        rs = slice(SGU_CHUNK * c, SGU_CHUNK * (c + 1))
        cols = []
        for g in range(SGU_GROUPS):
            gs = slice(SGU_GDIM * g, SGU_GDIM * (g + 1))
            cols.append(_mm(wms[g], vg[rs, gs]) + bt_ref[:, g:g + 1])
        rows.append(jnp.concatenate(cols, axis=1))
    s = rows[0] if len(rows) == 1 else jnp.concatenate(rows, axis=0)
    return du, dv, z, u, vn, rstd, vg, wms, mask, s


def sgu_forward(p, gain, wm, bt):
    L = p.shape[0]
    tl = min(TL_SGU, L)

    def body(p_ref, gain_ref, wm_ref, bt_ref, y_ref):
        _, _, z, u, _, _, _, _, _, s = _sgu_mix(p_ref, gain_ref, wm_ref, bt_ref, tl)
        sz, _ = _silu_and_grad(z)
        y_ref[...] = (u * s * sz).astype(y_ref.dtype)

    return pl.pallas_call(
        body, name="sgu_forward", grid=(L // tl,),
        in_specs=[pl.BlockSpec((tl, ODD_IN), lambda i: (i, 0)), _full((1, 2048)), _full(wm.shape), _full(bt.shape)],
        out_specs=pl.BlockSpec((tl, 2048), lambda i: (i, 0)),
        out_shape=jax.ShapeDtypeStruct((L, 2048), MXU_DTYPE),
        compiler_params=_cparams(("arbitrary",)),
    )(p, gain, wm, bt)


def sgu_backward(p, dy, gain, wm, bt, plan=None):
    L = p.shape[0]
    tl = min(TL_SGU, L)

    def body(p_ref, dy_ref, gain_ref, wm_ref, bt_ref, dp_ref, dgain_ref, dwm_ref, dbt_ref):
        @pl.when(pl.program_id(0) == 0)
        def _():
            dgain_ref[...] = jnp.zeros_like(dgain_ref)
            dwm_ref[...] = jnp.zeros_like(dwm_ref)
            dbt_ref[...] = jnp.zeros_like(dbt_ref)

        gu, gv, z, u, vn, rstd, vg, wms, mask, s = _sgu_mix(p_ref, gain_ref, wm_ref, bt_ref, tl)
        sz, dsz = _silu_and_grad(z)
        dyv = dy_ref[...]
        dp_ref[:, 4096:] = (dyv * u * s * dsz).astype(dp_ref.dtype)
        dsg = dyv * sz
        dp_ref[:, :2048] = (dsg * s * gu).astype(dp_ref.dtype)
        ds = dsg * u
        rows = []
        dbs = [jnp.zeros((SGU_CHUNK, 1), F32) for _ in range(SGU_GROUPS)]
        for c in range(tl // SGU_CHUNK):
            rs = slice(SGU_CHUNK * c, SGU_CHUNK * (c + 1))
            cols = []
            for g in range(SGU_GROUPS):
                gs = slice(SGU_GDIM * g, SGU_GDIM * (g + 1))
                dsg_c = ds[rs, gs]
                dbs[g] = dbs[g] + jnp.sum(dsg_c, axis=1, keepdims=True)
                dwm_ref[g] += jnp.where(mask, _mm_nt(dsg_c, vg[rs, gs]), 0.0)
                cols.append(_mm_tn(wms[g], dsg_c))
            rows.append(jnp.concatenate(cols, axis=1))
        dbt_ref[...] += jnp.concatenate(dbs, axis=1)
        dvg = rows[0] if len(rows) == 1 else jnp.concatenate(rows, axis=0)
        dgain_ref[...] += jnp.sum(dvg * vn, axis=0, keepdims=True)
        dvn = dvg * gain_ref[...]
        dv = rstd * (dvn - jnp.mean(dvn, axis=-1, keepdims=True) - vn * jnp.mean(dvn * vn, axis=-1, keepdims=True))
        dp_ref[:, 2048:4096] = (dv * gv).astype(dp_ref.dtype)

    return _call(
        body, plan, name="sgu_backward", grid=(L // tl,),
        in_specs=[pl.BlockSpec((tl, ODD_IN), lambda i: (i, 0)), pl.BlockSpec((tl, 2048), lambda i: (i, 0)),
                  _full((1, 2048)), _full(wm.shape), _full(bt.shape)],
        out_specs=[pl.BlockSpec((tl, ODD_IN), lambda i: (i, 0)), _full((1, 2048)), _full(wm.shape), _full(bt.shape)],
        out_shape=[jax.ShapeDtypeStruct((L, ODD_IN), MXU_DTYPE), jax.ShapeDtypeStruct((1, 2048), F32),
                   jax.ShapeDtypeStruct(wm.shape, F32), jax.ShapeDtypeStruct(bt.shape, F32)],
        sem=("arbitrary",),
    )(p, dy, gain, wm, bt)


def cast_shards(mats):
    n = len(mats)
    steps = 8

    def body(*refs):
        for p in range(n):
            refs[n + p][...] = refs[p][...].astype(MXU_DTYPE)

    specs = [pl.BlockSpec((m.shape[0] // steps, m.shape[1]), lambda i: (i, 0)) for m in mats]
    return pl.pallas_call(
        body, name="cast_shards", grid=(steps,), in_specs=specs, out_specs=specs,
        out_shape=[jax.ShapeDtypeStruct(m.shape, MXU_DTYPE) for m in mats],
        compiler_params=_cparams(("arbitrary",)),
    )(*mats)


def local_grads(x, tgt, w):
    L = x.shape[0]
    ne, gf = w["norm_even"], w["final_norm"].reshape(1, D_MODEL)
    sh = dict(zip(MATRICES, cast_shards([w[n][0] for n in MATRICES])))
    (w_in_e,) = run_plan(gather_plan([sh["w_in_even"]]), "gather_w_in_even")
    lam_re, lam_im = w["s5_lam_re"][0], w["s5_lam_im"][0]
    log_dt = w["s5_log_dt"].reshape(S5_GROUPS, 1)
    bt_re = jnp.transpose(w["s5_b_re"][0], (2, 0, 1))
    bt_im = jnp.transpose(w["s5_b_im"][0], (2, 0, 1))
    c_re, c_im = w["s5_c_re"][0], w["s5_c_im"][0]
    wm = w["sgu_w_spatial"][0]
    bt = jnp.transpose(w["sgu_b_spatial"][0])

    tl5 = min(TL_S5, L)
    ab_re, ab_im, bb_re, bb_im, at_re, at_im = s5_params_fwd(lam_re, lam_im, log_dt, bt_re, bt_im, tl5 // 8)
    atab = jnp.stack([ab_re.reshape(S5_LANES), ab_im.reshape(S5_LANES),
                      at_re.reshape(S5_LANES), at_im.reshape(S5_LANES)])
    wbd = jnp.concatenate([_block_diag(jnp.transpose(bb_re, (1, 0, 2)), True),
                           _block_diag(jnp.transpose(bb_im, (1, 0, 2)), True)], axis=2).astype(MXU_DTYPE)
    cre = _block_diag(jnp.transpose(c_re, (0, 2, 1)), True).astype(MXU_DTYPE)
    cim = _block_diag(jnp.transpose(c_im, (0, 2, 1)), True).astype(MXU_DTYPE)
    cos, sin = _rope_tables(L)

    s5_cols = 2 * S5_WIDTH
    w_s5 = jnp.concatenate([w_in_e[0], w_in_e[1][:, :s5_cols - EVEN_IN // N_CHIPS]], axis=1)
    w_ret = jnp.concatenate([w_in_e[1][:, s5_cols - EVEN_IN // N_CHIPS:], w_in_e[2], w_in_e[3]], axis=1)
    (p1a, h0s), (w_glu,) = norm_matmul(stream_order(x, tl5), ne, w_s5, "even_in_s5",
                                       gather_plan([sh["s5_w_glu"]]), tn=1024)
    (p1b, h0), (w_out_e,) = norm_matmul(x, ne, w_ret, "even_in_ret", gather_plan([sh["w_out_even"]]), tn=1024)
    w_glu = w_glu.reshape(S5_WIDTH, S5_WIDTH)
    w_out_e = w_out_e.reshape(2 * S5_WIDTH, D_MODEL)
    (ya, st_re, st_im, sv_re, sv_im), (w_in_o, w_out_o, no, sg_gain) = s5_forward(
        p1a, wbd, cre, cim, atab, w["s5_d"], w_glu, w["s5_b_glu"],
        gather_plan([sh["w_in_odd"], sh["w_out_odd"], w["norm_odd"], w["sgu_norm_gain"]]))
    w_out_o = w_out_o.reshape(SGU_WIDTH, D_MODEL)
    no, sg_gain = no.reshape(1, D_MODEL), sg_gain.reshape(1, SGU_WIDTH)
    yb, prevs = retention_forward(p1b, cos, sin, w["ret_gn_gain"])
    ya = token_order(ya, tl5)
    x1 = matmul_residual([ya, yb], w_out_e, x, "even_out")
    (p2, h1), _ = norm_matmul(x1, no, w_in_o, "odd_in")
    y2 = sgu_forward(p2, sg_gain, wm, bt)
    dx2, loss, dgf = out_proj_loss(y2, w_out_o, x1, gf, tgt, "odd_out_loss")

    g, landed = {}, {}
    shard_major = lambda a, n: a.reshape((N_CHIPS,) + w[n].shape[1:])
    dy2, g_w_out_o = out_proj_bwd(dx2, w_out_o, [y2], "odd_out_bwd")
    (dp2, g["sgu_norm_gain"], dwm, dbt), (landed["w_out_odd"],) = sgu_backward(
        p2, dy2, sg_gain, wm, bt, reduce_plan([shard_major(g_w_out_o, "w_out_odd")]))
    g_w_in_o, _ = in_proj_bwd_dw(h1, dp2, "odd_in_dw", ODD_IN // N_CHIPS)
    (dx1, g["norm_odd"]), _ = in_proj_bwd_dx(x1, no, [dp2], [w_in_o], dx2, "odd_in_dx")
    dya, dyb, g_w_out_e = out_proj_bwd(dx1, w_out_e, [ya, yb], "even_out_bwd")
    ((dpa, dwbd, dcre, dcim, dab_re, dab_im, g["s5_d"], g_w_glu, g["s5_b_glu"]),
     (landed["w_in_odd"], landed["w_out_even"])) = s5_backward(
        p1a, stream_order(dya, tl5), st_re, st_im, sv_re, sv_im, wbd, cre, cim, atab, w["s5_d"], w_glu,
        w["s5_b_glu"], reduce_plan([g_w_in_o, shard_major(g_w_out_e, "w_out_even")]))

    dbb_re = jnp.transpose(_block_diag_extract(dwbd[:, :, :512], S5_GROUP, S5_STATE), (1, 0, 2))
    dbb_im = jnp.transpose(_block_diag_extract(dwbd[:, :, 512:], S5_GROUP, S5_STATE), (1, 0, 2))
    dlr, dli, ddt, dbt_re, dbt_im = s5_params_bwd(
        lam_re, lam_im, log_dt, bt_re, bt_im, dab_re.reshape(8, S5_GROUPS, S5_STATE),
        dab_im.reshape(8, S5_GROUPS, S5_STATE), dbb_re, dbb_im)
    g["s5_lam_re"], g["s5_lam_im"] = dlr[None], dli[None]
    g["s5_log_dt"] = ddt.reshape(1, S5_GROUPS)
    g["s5_b_re"], g["s5_b_im"] = dbt_re, dbt_im
    g["s5_c_re"] = _block_diag_extract(dcre, S5_GROUP, S5_STATE)[None]
    g["s5_c_im"] = _block_diag_extract(dcim, S5_GROUP, S5_STATE)[None]
    g["sgu_w_spatial"] = dwm[None]
    g["sgu_b_spatial"] = jnp.transpose(dbt)[None]
    g["final_norm"] = dgf.reshape(D_MODEL)
    g["loss"] = loss

    (dpb, g["ret_gn_gain"]), (landed["s5_w_glu"],) = retention_backward(
        p1b, dyb, prevs, cos, sin, w["ret_gn_gain"], reduce_plan([shard_major(g_w_glu, "s5_w_glu")]))
    done = tuple(n for n in MATRICES if n != "w_in_even")
    part = {n: sum_slabs(landed[n], "sum_" + n) for n in done}
    g_w_in_e, recv = in_proj_bwd_dw(h0s, dpa, "even_in_dw_s5", 512, dtype=MXU_DTYPE,
                                    plan=_SiblingPlan([part[n] for n in done]))
    other = dict(zip(done, recv))
    small = tuple(n for n in SMALL if n != "norm_even") + ("loss",)
    g_w_in_e, recv = in_proj_bwd_dw(h0, dpb, "even_in_dw_ret", 512, first=s5_cols // 512, into=g_w_in_e,
                                    dtype=MXU_DTYPE, plan=reduce_plan([], [g[n] for n in small]))
    landed.update(zip(small, recv))
    (dx0, g["norm_even"]), (landed["w_in_even"],) = in_proj_bwd_dx(
        x, ne, [token_order(dpa, tl5), dpb], [w_s5, w_ret], dx1, "even_in_dx", reduce_plan([g_w_in_e]))
    (landed["norm_even"],) = run_plan(reduce_plan([], [g["norm_even"]]), "exchange_norm_even")
    return dx0, landed, part, other


def _row_block(rows):
    return 128 if rows % 128 == 0 else rows


def sum_slabs(r, name):
    _, R, C = r.shape
    tr = _row_block(R)

    def body(r_ref, o_ref):
        a, b, c, d = (r_ref[k].astype(F32) for k in range(N_CHIPS))
        o_ref[...] = (a + b) + (c + d)

    return pl.pallas_call(
        body, name=name, grid=(R // tr,),
        in_specs=[pl.BlockSpec((N_CHIPS, tr, C), lambda i: (0, i, 0))],
        out_specs=pl.BlockSpec((tr, C), lambda i: (i, 0)),
        out_shape=jax.ShapeDtypeStruct((R, C), F32),
        compiler_params=_cparams(("arbitrary",)),
    )(r)


def _adam(w, m, v, g):
    mn = ADAM_B1 * m + (1.0 - ADAM_B1) * g
    vn = ADAM_B2 * v + (1.0 - ADAM_B2) * (g * g)
    m_hat = mn / (1.0 - ADAM_B1 ** ADAM_STEP)
    v_hat = vn / (1.0 - ADAM_B2 ** ADAM_STEP)
    return -ADAM_LR * (m_hat / (jnp.sqrt(v_hat) + ADAM_EPS) + ADAM_WD * w), mn, vn


def adam_update(w, m, v, ga, gb, name):
    R, C = w.shape
    tr = _row_block(R)

    def body(w_ref, m_ref, v_ref, ga_ref, gb_ref, g_out, d_out, m_out, v_out):
        g = ga_ref[...] + gb_ref[...]
        g_out[...] = g
        d_out[...], m_out[...], v_out[...] = _adam(w_ref[...], m_ref[...], v_ref[...], g)

    blk = pl.BlockSpec((tr, C), lambda i: (i, 0))
    return pl.pallas_call(
        body, name=name, grid=(R // tr,),
        in_specs=[blk] * 5, out_specs=[blk] * 4,
        out_shape=[jax.ShapeDtypeStruct((R, C), F32)] * 4,
        compiler_params=_cparams(("arbitrary",)),
    )(w, m, v, ga, gb)


WIDE_ROWS = ("s5_b_re", "s5_b_im")


def sum_small(landed):
    def body(*refs):
        k = len(refs) // 2
        for i in range(k):
            r = refs[i]
            refs[k + i][...] = (r[0] + r[1]) + (r[2] + r[3])

    names = list(landed)
    res = pl.pallas_call(
        body, name="sum_small", out_shape=[jax.ShapeDtypeStruct(landed[n].shape[1:], F32) for n in names],
        compiler_params=pltpu.CompilerParams(vmem_limit_bytes=VMEM_LIMIT),
    )(*[landed[n] for n in names])
    return dict(zip(names, res))


def adam_small(names, w, m, v, ga, gb):
    def body(*refs):
        k = len(refs) // 9
        me = 2 * lax.axis_index("x") + lax.axis_index("y")
        for i in range(k):
            w_ref, m_ref, v_ref, ga_ref, gb_ref = refs[i], refs[k + i], refs[2 * k + i], refs[3 * k + i], refs[4 * k + i]
            size = w_ref.shape[-1]
            if ga_ref.shape != w_ref.shape:
                part = pl.ds(pl.multiple_of(me * size, LANES), size)
                g = ga_ref[:, part] + gb_ref[:, part]
            else:
                g = ga_ref[...] + gb_ref[...]
            refs[5 * k + i][...] = g
            refs[6 * k + i][...], refs[7 * k + i][...], refs[8 * k + i][...] = _adam(w_ref[...], m_ref[...], v_ref[...], g)

    ins = [d[n] for d in (w, m, v, ga, gb) for n in names]
    outs = [jax.ShapeDtypeStruct(w[n].shape, F32) for _ in range(4) for n in names]
    res = pl.pallas_call(body, name="adam_small", out_shape=outs,
                         compiler_params=pltpu.CompilerParams(vmem_limit_bytes=VMEM_LIMIT))(*ins)
    k = len(names)
    return [dict(zip(names, res[j * k:(j + 1) * k])) for j in range(4)]


WEIGHTS = ("norm_even", "w_in_even", "s5_lam_re", "s5_lam_im", "s5_log_dt", "s5_b_re", "s5_b_im", "s5_c_re",
           "s5_c_im", "s5_d", "s5_w_glu", "s5_b_glu", "ret_gn_gain", "w_out_even", "norm_odd", "w_in_odd",
           "sgu_norm_gain", "sgu_w_spatial", "sgu_b_spatial", "w_out_odd", "final_norm")
MATRICES = ("w_in_even", "s5_w_glu", "w_out_even", "w_in_odd", "w_out_odd")
SHARDED_VECS = ("norm_odd", "sgu_norm_gain")
REPLICATED = tuple(n for n in WEIGHTS if n not in MATRICES and n not in SHARDED_VECS)
SMALL = tuple(n for n in WEIGHTS if n not in MATRICES)
LANES = 128


def kernel(x, norm_even, w_in_even, s5_lam_re, s5_lam_im, s5_log_dt, s5_b_re, s5_b_im, s5_c_re, s5_c_im, s5_d, s5_w_glu, s5_b_glu, ret_gn_gain, w_out_even, norm_odd, w_in_odd, sgu_norm_gain, sgu_w_spatial, sgu_b_spatial, w_out_odd, final_norm, loss_target, m_norm_even, m_w_in_even, m_s5_lam_re, m_s5_lam_im, m_s5_log_dt, m_s5_b_re, m_s5_b_im, m_s5_c_re, m_s5_c_im, m_s5_d, m_s5_w_glu, m_s5_b_glu, m_ret_gn_gain, m_w_out_even, m_norm_odd, m_w_in_odd, m_sgu_norm_gain, m_sgu_w_spatial, m_sgu_b_spatial, m_w_out_odd, m_final_norm, v_norm_even, v_w_in_even, v_s5_lam_re, v_s5_lam_im, v_s5_log_dt, v_s5_b_re, v_s5_b_im, v_s5_c_re, v_s5_c_im, v_s5_d, v_s5_w_glu, v_s5_b_glu, v_ret_gn_gain, v_w_out_even, v_norm_odd, v_w_in_odd, v_sgu_norm_gain, v_sgu_w_spatial, v_sgu_b_spatial, v_w_out_odd, v_final_norm):
    w = dict(norm_even=norm_even, w_in_even=w_in_even, s5_lam_re=s5_lam_re, s5_lam_im=s5_lam_im, s5_log_dt=s5_log_dt, s5_b_re=s5_b_re, s5_b_im=s5_b_im, s5_c_re=s5_c_re, s5_c_im=s5_c_im, s5_d=s5_d, s5_w_glu=s5_w_glu, s5_b_glu=s5_b_glu, ret_gn_gain=ret_gn_gain, w_out_even=w_out_even, norm_odd=norm_odd, w_in_odd=w_in_odd, sgu_norm_gain=sgu_norm_gain, sgu_w_spatial=sgu_w_spatial, sgu_b_spatial=sgu_b_spatial, w_out_odd=w_out_odd, final_norm=final_norm)
    m = dict(norm_even=m_norm_even, w_in_even=m_w_in_even, s5_lam_re=m_s5_lam_re, s5_lam_im=m_s5_lam_im, s5_log_dt=m_s5_log_dt, s5_b_re=m_s5_b_re, s5_b_im=m_s5_b_im, s5_c_re=m_s5_c_re, s5_c_im=m_s5_c_im, s5_d=m_s5_d, s5_w_glu=m_s5_w_glu, s5_b_glu=m_s5_b_glu, ret_gn_gain=m_ret_gn_gain, w_out_even=m_w_out_even, norm_odd=m_norm_odd, w_in_odd=m_w_in_odd, sgu_norm_gain=m_sgu_norm_gain, sgu_w_spatial=m_sgu_w_spatial, sgu_b_spatial=m_sgu_b_spatial, w_out_odd=m_w_out_odd, final_norm=m_final_norm)
    v = dict(norm_even=v_norm_even, w_in_even=v_w_in_even, s5_lam_re=v_s5_lam_re, s5_lam_im=v_s5_lam_im, s5_log_dt=v_s5_log_dt, s5_b_re=v_s5_b_re, s5_b_im=v_s5_b_im, s5_c_re=v_s5_c_re, s5_c_im=v_s5_c_im, s5_d=v_s5_d, s5_w_glu=v_s5_w_glu, s5_b_glu=v_s5_b_glu, ret_gn_gain=v_ret_gn_gain, w_out_even=v_w_out_even, norm_odd=v_norm_odd, w_in_odd=v_w_in_odd, sgu_norm_gain=v_sgu_norm_gain, sgu_w_spatial=v_sgu_w_spatial, sgu_b_spatial=v_sgu_b_spatial, w_out_odd=v_w_out_odd, final_norm=v_final_norm)

    grad_x, landed, part, other = local_grads(x[0], loss_target[0], w)

    small = SMALL + ("loss",)
    part["w_in_even"] = sum_slabs(landed["w_in_even"], "sum_w_in_even")
    part.update(sum_small({n: landed[n] for n in small}))
    names = ("w_in_even",) + small
    other.update(zip(names, run_plan(_SiblingPlan([part[n] for n in names]), "sibling_exchange")))

    wt, mt, vt = dict(w), dict(m), dict(v)
    for n in WIDE_ROWS:
        wt[n], mt[n], vt[n] = (jnp.transpose(a[n][0], (2, 0, 1)) for a in (w, m, v))
    out_g, out_d, out_m, out_v = adam_small(SMALL, wt, mt, vt, part, other)
    for n in WIDE_ROWS:
        for out in (out_g, out_d, out_m, out_v):
            out[n] = jnp.transpose(out[n], (1, 2, 0))[None]
    for n in MATRICES:
        res = adam_update(w[n][0], m[n][0], v[n][0], part[n], other[n], "adam_" + n)
        out_g[n], out_d[n], out_m[n], out_v[n] = (r[None] for r in res)
    total_loss = (part["loss"] + other["loss"])[0, 0]

    return (total_loss, grad_x[None], *[out_g[n] for n in WEIGHTS], *[out_d[n] for n in WEIGHTS],
            *[out_m[n] for n in WEIGHTS], *[out_v[n] for n in WEIGHTS])
```

```python
import functools
import math

import numpy as np
import jax
import jax.numpy as jnp
from jax import lax
from jax.experimental import pallas as pl
from jax.experimental.pallas import tpu as pltpu

F32 = jnp.float32
MXU_DTYPE = jnp.bfloat16
NORM_EPS = 1e-6
D_MODEL = 1024
S5_WIDTH = 1024
S5_GROUP = 16
S5_GROUPS = 64
S5_STATE = 64
S5_LANES = S5_GROUPS * S5_STATE
S5_KBLK = 8
RET_HEADS = 4
RET_DK = 256
RET_CHUNK = 128
ROPE_BASE = 10000.0
SGU_WIDTH = 2048
SGU_GROUPS = 4
SGU_GDIM = 512
SGU_CHUNK = 128
EVEN_IN = 6144
ODD_IN = 6144
ADAM_LR = 0.001
ADAM_B1 = 0.9
ADAM_B2 = 0.999
ADAM_EPS = 1e-08
ADAM_WD = 0.01
ADAM_STEP = 10
N_CHIPS = 4
VMEM_LIMIT = 56 * 1024 * 1024

TL_PROJ = 512
TL_DW = 1024
TL_S5 = 256
TL_SGU = 256


def _cparams(sem, **kw):
    return pltpu.CompilerParams(dimension_semantics=sem, vmem_limit_bytes=VMEM_LIMIT, **kw)


def _mm(a, b):
    return jnp.dot(a.astype(MXU_DTYPE), b.astype(MXU_DTYPE), preferred_element_type=F32)


def _mm_nt(a, b):
    return lax.dot_general(a.astype(MXU_DTYPE), b.astype(MXU_DTYPE),
                           (((1,), (1,)), ((), ())), preferred_element_type=F32)


def _mm_tn(a, b):
    return lax.dot_general(a.astype(MXU_DTYPE), b.astype(MXU_DTYPE),
                           (((0,), (0,)), ((), ())), preferred_element_type=F32)


_GELU_C = math.sqrt(2.0 / math.pi)


def _gelu_parts(x):
    x2 = x * x
    th = jnp.tanh(x * (_GELU_C + (_GELU_C * 0.044715) * x2))
    hx = 0.5 * x
    return hx + hx * th, th, x2, hx


def _gelu(x):
    return _gelu_parts(x)[0]


def _gelu_and_grad(x):
    g, th, x2, hx = _gelu_parts(x)
    return g, (0.5 + 0.5 * th) + hx * (1.0 - th * th) * (_GELU_C + (3.0 * _GELU_C * 0.044715) * x2)


def _gelu_grad(x):
    return _gelu_and_grad(x)[1]


def _sigmoid(x):
    return 1.0 / (1.0 + jnp.exp(-x))


def _silu_and_grad(x):
    s = _sigmoid(x)
    return x * s, s * (1.0 + x * (1.0 - s))


def _rms(x):
    return lax.rsqrt(jnp.mean(x * x, axis=-1, keepdims=True) + NORM_EPS)


def _full(shape):
    nd = len(shape)
    return pl.BlockSpec(shape, lambda *_: (0,) * nd)


MESH = pl.DeviceIdType.MESH
ANY = pl.BlockSpec(memory_space=pl.ANY)


def _place():
    return lax.axis_index("x"), lax.axis_index("y"), lax.axis_index("c")


def _chip_peer(x, y, c, d):
    return (1 - x if d >= 2 else x, 1 - y if d % 2 else y, c)


class _Plan:
    def __init__(self, inputs, out_shape, build):
        self.inputs, self.out_shape, self._build = list(inputs), list(out_shape), build
        n = len(self.inputs)
        self.sems = [pltpu.SemaphoreType.DMA((n, 3)), pltpu.SemaphoreType.DMA((n, 3)), pltpu.SemaphoreType.DMA((n,))]

    def start(self, in_refs, out_refs, sems):
        send, recv, local = self._build(in_refs, out_refs, sems)
        for p in range(len(self.inputs)):
            local[p].start()
            for cp in send[p]:
                cp.start()

    def wait(self, in_refs, out_refs, sems):
        send, recv, local = self._build(in_refs, out_refs, sems)
        for p in range(len(self.inputs)):
            for cp in recv[p]:
                cp.wait_recv()
        for p in range(len(self.inputs)):
            for cp in send[p]:
                cp.wait_send()
            local[p].wait()


class _GatherPlan:
    def __init__(self, shards):
        self.inputs = list(shards)
        self.out_shape = [jax.ShapeDtypeStruct((N_CHIPS,) + s.shape, s.dtype) for s in shards]
        n = len(shards)
        self.halved = [s.shape[0] % 32 == 0 for s in shards]
        self.sems = [pltpu.SemaphoreType.DMA((n, 3)) for _ in range(4)] + [pltpu.SemaphoreType.DMA((n,))]

    def _copies(self, in_refs, out_refs, sems):
        ici_s, ici_r, d2d_s, d2d_r, loc = sems
        x, y, c = _place()
        me = 2 * x + y

        def rows(p, core):
            if not self.halved[p]:
                return slice(None)
            half = self.inputs[p].shape[0] // 2
            return pl.ds(pl.multiple_of(core * half, 16), half)

        def ici(p, d, slab, core, src=None):
            dst = out_refs[p].at[slab, rows(p, core)]
            return pltpu.make_async_remote_copy(
                src_ref=in_refs[p].at[rows(p, core)] if src is None else src, dst_ref=dst,
                send_sem=ici_s.at[p, d - 1], recv_sem=ici_r.at[p, d - 1],
                device_id=_chip_peer(x, y, c, d), device_id_type=MESH)

        def d2d(p, d, core):
            part = out_refs[p].at[me ^ d, rows(p, core)]
            return pltpu.make_async_remote_copy(
                src_ref=part, dst_ref=part, send_sem=d2d_s.at[p, d - 1], recv_sem=d2d_r.at[p, d - 1],
                device_id=(x, y, 1 - c), device_id_type=MESH)

        local = [pltpu.make_async_copy(in_refs[p], out_refs[p].at[me], loc.at[p]) for p in range(len(self.inputs))]
        return me, c, ici, d2d, local

    def start(self, in_refs, out_refs, sems):
        me, c, ici, d2d, local = self._copies(in_refs, out_refs, sems)
        for p in range(len(self.inputs)):
            local[p].start()
            for d in (1, 2, 3):
                ici(p, d, me, c).start()

    def wait(self, in_refs, out_refs, sems):
        me, c, ici, d2d, local = self._copies(in_refs, out_refs, sems)
        n = len(self.inputs)
        for p in range(n):
            for d in (1, 2, 3):
                ici(p, d, me ^ d, c).wait_recv()
                if self.halved[p]:
                    d2d(p, d, c).start()
        for p in range(n):
            for d in (1, 2, 3):
                if self.halved[p]:
                    d2d(p, d, 1 - c).wait_recv()
                    d2d(p, d, c).wait_send()
                ici(p, d, me, c).wait_send()
            local[p].wait()


def gather_plan(shards):
    return _GatherPlan(shards)


def reduce_plan(shards, whole=()):
    n_s = len(shards)

    def build(in_refs, out_refs, sems):
        send_sems, recv_sems, loc_sems = sems
        x, y, c = _place()
        me = 2 * x + y

        def src(p, slab):
            return in_refs[p].at[slab] if p < n_s else in_refs[p]

        def remote(p, d):
            return pltpu.make_async_remote_copy(
                src_ref=src(p, me ^ d), dst_ref=out_refs[p].at[d], send_sem=send_sems.at[p, d - 1],
                recv_sem=recv_sems.at[p, d - 1], device_id=_chip_peer(x, y, c, d), device_id_type=MESH)

        n = len(in_refs)
        send = [[remote(p, d) for d in (1, 2, 3)] for p in range(n)]
        local = [pltpu.make_async_copy(src(p, me), out_refs[p].at[0], loc_sems.at[p]) for p in range(n)]
        return send, send, local

    outs = [jax.ShapeDtypeStruct(s.shape, s.dtype) for s in shards]
    outs += [jax.ShapeDtypeStruct((N_CHIPS,) + a.shape, a.dtype) for a in whole]
    return _Plan(list(shards) + list(whole), outs, build)


class _SiblingPlan:
    def __init__(self, arrs):
        self.inputs = list(arrs)
        self.out_shape = [jax.ShapeDtypeStruct(a.shape, a.dtype) for a in arrs]
        n = len(arrs)
        self.sems = [pltpu.SemaphoreType.DMA((n,)), pltpu.SemaphoreType.DMA((n,))]

    def _copies(self, in_refs, out_refs, sems):
        x, y, c = _place()
        return [pltpu.make_async_remote_copy(
            src_ref=in_refs[p], dst_ref=out_refs[p], send_sem=sems[0].at[p], recv_sem=sems[1].at[p],
            device_id=(x, y, 1 - c), device_id_type=MESH) for p in range(len(self.inputs))]

    def start(self, in_refs, out_refs, sems):
        for cp in self._copies(in_refs, out_refs, sems):
            cp.start()

    def wait(self, in_refs, out_refs, sems):
        copies = self._copies(in_refs, out_refs, sems)
        for cp in copies:
            cp.wait_recv()
        for cp in copies:
            cp.wait_send()


def run_plan(plan, name):
    n = len(plan.inputs)

    def body(*refs):
        plan.start(refs[:n], refs[n:2 * n], refs[2 * n:])
        plan.wait(refs[:n], refs[n:2 * n], refs[2 * n:])

    return pl.pallas_call(body, name=name, in_specs=[ANY] * n, out_specs=[ANY] * n, out_shape=plan.out_shape,
                          scratch_shapes=plan.sems)(*plan.inputs)


def _call(body, plan, *, name, grid, in_specs, out_specs, out_shape, sem, scratch_shapes=(), aliases=None,
          n_prefetch=0):
    aliases = {} if aliases is None else aliases
    single = not isinstance(out_shape, (list, tuple))
    out_specs = [out_specs] if single else list(out_specs)
    out_shape = [out_shape] if single else list(out_shape)
    n_in, n_out, n_scr = len(in_specs), len(out_specs), len(scratch_shapes)
    ci = 0 if plan is None else len(plan.inputs)

    def hosted(*refs):
        pre, refs = refs[:n_prefetch], refs[n_prefetch:]
        ins, cins = refs[:n_in], refs[n_in:n_in + ci]
        k = n_in + ci
        outs, couts = refs[k:k + n_out], refs[k + n_out:k + n_out + ci]
        k += n_out + ci
        scr, sems = refs[k:k + n_scr], refs[k + n_scr:]
        ids = [pl.program_id(a) for a in range(len(grid))]
        first = functools.reduce(jnp.logical_and, [i == 0 for i in ids])
        last = functools.reduce(jnp.logical_and, [i == g - 1 for i, g in zip(ids, grid)])

        @pl.when(first)
        def _():
            plan.start(cins, couts, sems)

        body(*pre, *ins, *outs, *scr)

        @pl.when(last)
        def _():
            plan.wait(cins, couts, sems)

    def run(*args):
        hosting = plan is not None
        spec = pltpu.PrefetchScalarGridSpec(
            num_scalar_prefetch=n_prefetch, grid=grid,
            in_specs=list(in_specs) + ([ANY] * ci if hosting else []),
            out_specs=out_specs + ([ANY] * ci if hosting else []),
            scratch_shapes=list(scratch_shapes) + (plan.sems if hosting else []))
        res = pl.pallas_call(hosted if hosting else body, name=name, grid_spec=spec,
                             out_shape=out_shape + (plan.out_shape if hosting else []),
                             input_output_aliases=aliases, compiler_params=_cparams(sem),
                             )(*args, *(plan.inputs if hosting else []))
        return (res[0] if single else res[:n_out]), list(res[n_out:])

    return run


def norm_matmul(x, g, w, name, plan=None, tn=None):
    L, D = x.shape
    tl = min(TL_DW, L)
    if w.ndim == 3:
        nt, _, tn = w.shape
        w_spec = pl.BlockSpec((1, D, tn), lambda i, n: (n, 0, 0))
    else:
        nt = w.shape[1] // tn
        w_spec = pl.BlockSpec((D, tn), lambda i, n: (0, n))

    def body(x_ref, g_ref, w_ref, o_ref, h_ref):
        xv = x_ref[...]
        h = (xv * _rms(xv) * g_ref[...]).astype(h_ref.dtype)
        h_ref[...] = h
        o_ref[...] = _mm(h, w_ref[0] if w.ndim == 3 else w_ref[...])

    return _call(
        body, plan, name=name, grid=(L // tl, nt),
        in_specs=[pl.BlockSpec((tl, D), lambda i, n: (i, 0)), _full((1, D)), w_spec],
        out_specs=[pl.BlockSpec((tl, tn), lambda i, n: (i, n)), pl.BlockSpec((tl, D), lambda i, n: (i, 0))],
        out_shape=[jax.ShapeDtypeStruct((L, nt * tn), F32), jax.ShapeDtypeStruct((L, D), MXU_DTYPE)],
        sem=("arbitrary", "arbitrary"),
    )(x, g, w)


def even_in_slabs(x, xs, g, w, slabs, wsel, name, p_in=None, plan=None):
    L, D = x.shape
    tl = min(TL_PROJ, L)
    wb = EVEN_IN // N_CHIPS
    n = slabs.shape[0]
    s5_cols = 2 * S5_WIDTH - wb
    first = p_in is None

    def body(slabs_ref, wsel_ref, xs_ref, x_ref, g_ref, w_ref, *rest):
        o_ref = rest[-3] if first else rest[-1]
        j = slabs_ref[pl.program_id(0)]
        hs = (xs_ref[...] * _rms(xs_ref[...]) * g_ref[...]).astype(MXU_DTYPE)
        h = (x_ref[...] * _rms(x_ref[...]) * g_ref[...]).astype(MXU_DTYPE)
        if first:
            rest[-2][...] = hs
            rest[-1][...] = h
        o_ref[:, :s5_cols] = _mm(jnp.where(j <= 1, hs, h), w_ref[0, :, :s5_cols])
        o_ref[:, s5_cols:] = _mm(jnp.where(j == 0, hs, h), w_ref[0, :, s5_cols:])

    row = pl.BlockSpec((tl, D), lambda s, i, slabs_ref, wsel_ref: (i, 0))
    in_specs = [row if first else
                pl.BlockSpec((tl, D), lambda s, i, slabs_ref, wsel_ref: (jnp.where(slabs_ref[s] <= 1, i, 0), 0)),
                row if first else
                pl.BlockSpec((tl, D), lambda s, i, slabs_ref, wsel_ref: (jnp.where(slabs_ref[s] >= 1, i, 0), 0)),
                pl.BlockSpec((1, D), lambda s, i, slabs_ref, wsel_ref: (0, 0)),
                pl.BlockSpec((1, D, wb), lambda s, i, slabs_ref, wsel_ref: (wsel_ref[s], 0, 0))]
    out_specs = [pl.BlockSpec((tl, wb), lambda s, i, slabs_ref, wsel_ref: (i, slabs_ref[s]))]
    out_shape = [jax.ShapeDtypeStruct((L, EVEN_IN), F32)]
    args = [slabs, wsel, xs, x, g, w]
    if first:
        out_specs += [row, row]
        out_shape += [jax.ShapeDtypeStruct((L, D), MXU_DTYPE)] * 2
    else:
        in_specs.append(ANY)
        args.append(p_in)
    return _call(body, plan, name=name, grid=(n, L // tl), in_specs=in_specs, out_specs=out_specs,
                 out_shape=out_shape, sem=("arbitrary", "arbitrary"), n_prefetch=2,
                 aliases={} if first else {6: 0})(*args)


def matmul_residual(ys, w, x, name):
    L, D = x.shape
    tl = min(TL_PROJ, L)
    n = len(ys)
    offs = np.cumsum([0] + [y.shape[1] for y in ys])

    def body(*refs):
        y_refs, w_ref, x_ref, o_ref = refs[:n], refs[n], refs[n + 1], refs[n + 2]
        acc = x_ref[...]
        for k in range(n):
            acc = acc + _mm(y_refs[k][...], w_ref[offs[k]:offs[k + 1], :])
        o_ref[...] = acc

    return pl.pallas_call(
        body, name=name, grid=(L // tl,),
        in_specs=[pl.BlockSpec((tl, y.shape[1]), lambda i: (i, 0)) for y in ys]
        + [_full(w.shape), pl.BlockSpec((tl, D), lambda i: (i, 0))],
        out_specs=pl.BlockSpec((tl, D), lambda i: (i, 0)),
        out_shape=jax.ShapeDtypeStruct((L, D), F32),
        compiler_params=_cparams(("arbitrary",)),
    )(*ys, w, x)


def out_proj_loss(y, w, x, gf, tgt, name):
    L, K = y.shape
    D = w.shape[1]
    tl = min(TL_PROJ, L)

    def body(y_ref, w_ref, x_ref, gf_ref, t_ref, dx_ref, loss_ref, dg_ref):
        @pl.when(pl.program_id(0) == 0)
        def _():
            loss_ref[...] = jnp.zeros_like(loss_ref)
            dg_ref[...] = jnp.zeros_like(dg_ref)

        x2 = x_ref[...] + _mm(y_ref[...], w_ref[...])
        r = _rms(x2)
        xn = x2 * r
        e = xn * gf_ref[...] - t_ref[...]
        loss_ref[...] += (0.5 / D) * jnp.sum(e * e)
        dout = e * (1.0 / D)
        dg_ref[...] += jnp.sum(dout * xn, axis=0, keepdims=True)
        dxn = dout * gf_ref[...]
        dx_ref[...] = r * (dxn - xn * jnp.mean(dxn * xn, axis=-1, keepdims=True))

    return pl.pallas_call(
        body, name=name, grid=(L // tl,),
        in_specs=[pl.BlockSpec((tl, K), lambda i: (i, 0)), _full((K, D)),
                  pl.BlockSpec((tl, D), lambda i: (i, 0)), _full((1, D)),
                  pl.BlockSpec((tl, D), lambda i: (i, 0))],
        out_specs=[pl.BlockSpec((tl, D), lambda i: (i, 0)), _full((8, 128)), _full((1, D))],
        out_shape=[jax.ShapeDtypeStruct((L, D), F32), jax.ShapeDtypeStruct((8, 128), F32),
                   jax.ShapeDtypeStruct((1, D), F32)],
        compiler_params=_cparams(("arbitrary",)),
    )(y, w, x, gf, tgt)


def out_proj_bwd(dx, w, ys, name):
    L, D = dx.shape
    K = w.shape[0]
    tl = min(TL_PROJ, L)
    n = len(ys)
    offs = np.cumsum([0] + [y.shape[1] for y in ys])

    def body(*refs):
        dx_ref, w_ref, y_refs = refs[0], refs[1], refs[2:2 + n]
        dy_refs, dw_ref = refs[2 + n:2 + 2 * n], refs[2 + 2 * n]

        @pl.when(pl.program_id(0) == 0)
        def _():
            dw_ref[...] = jnp.zeros_like(dw_ref)

        dxv = dx_ref[...]
        for k in range(n):
            dy_refs[k][...] = _mm_nt(dxv, w_ref[offs[k]:offs[k + 1], :])
            dw_ref[offs[k]:offs[k + 1], :] += _mm_tn(y_refs[k][...], dxv)

    y_specs = [pl.BlockSpec((tl, y.shape[1]), lambda i: (i, 0)) for y in ys]
    return pl.pallas_call(
        body, name=name, grid=(L // tl,),
        in_specs=[pl.BlockSpec((tl, D), lambda i: (i, 0)), _full((K, D))] + y_specs,
        out_specs=y_specs + [_full((K, D))],
        out_shape=[jax.ShapeDtypeStruct(y.shape, F32) for y in ys] + [jax.ShapeDtypeStruct((K, D), F32)],
        compiler_params=_cparams(("arbitrary",)),
    )(dx, w, *ys)


def in_proj_bwd_dx(x, g, dps, ws, dres, name, plan=None):
    L, D = x.shape
    tl = min(TL_PROJ, L)
    n = len(dps)

    def body(*refs):
        x_ref, g_ref, dres_ref = refs[:3]
        dp_refs, w_refs = refs[3:3 + n], refs[3 + n:3 + 2 * n]
        dx_ref, dg_ref = refs[3 + 2 * n:]

        @pl.when(pl.program_id(0) == 0)
        def _():
            dg_ref[...] = jnp.zeros_like(dg_ref)

        dh = None
        for dp_ref, w_ref, w in zip(dp_refs, w_refs, ws):
            if w.ndim == 3:
                tn = w.shape[2]
                parts = [_mm_nt(dp_ref[:, tn * k:tn * (k + 1)], w_ref[k]) for k in range(w.shape[0])]
            else:
                parts = [_mm_nt(dp_ref[...], w_ref[...])]
            for part in parts:
                dh = part if dh is None else dh + part
        xv = x_ref[...]
        r = _rms(xv)
        xn = xv * r
        dg_ref[...] += jnp.sum(dh * xn, axis=0, keepdims=True)
        dxn = dh * g_ref[...]
        dx_ref[...] = dres_ref[...] + r * (dxn - xn * jnp.mean(dxn * xn, axis=-1, keepdims=True))

    return _call(
        body, plan, name=name, grid=(L // tl,),
        in_specs=[pl.BlockSpec((tl, D), lambda i: (i, 0)), _full((1, D)), pl.BlockSpec((tl, D), lambda i: (i, 0))]
        + [pl.BlockSpec((tl, dp.shape[1]), lambda i: (i, 0)) for dp in dps] + [_full(w.shape) for w in ws],
        out_specs=[pl.BlockSpec((tl, D), lambda i: (i, 0)), _full((1, D))],
        out_shape=[jax.ShapeDtypeStruct((L, D), F32), jax.ShapeDtypeStruct((1, D), F32)],
        sem=("arbitrary",),
    )(x, g, dres, *dps, *ws)


def in_proj_bwd_dw(h, dp, name, tn, first=0, into=None, dtype=F32, plan=None):
    L, D = h.shape
    tl = min(TL_DW, L)
    wb = EVEN_IN // N_CHIPS
    per = wb // tn
    count = dp.shape[1] // tn
    last = L // tl - 1

    def body(*refs):
        h_ref, dp_ref, dw_ref, acc = refs[0], refs[1], refs[-2], refs[-1]

        @pl.when(pl.program_id(1) == 0)
        def _():
            acc[...] = jnp.zeros_like(acc)

        acc[...] += _mm_tn(h_ref[...], dp_ref[...])

        @pl.when(pl.program_id(1) == last)
        def _():
            dw_ref[0] = acc[...].astype(dw_ref.dtype)

    ins = [h, dp] + ([] if into is None else [into])
    return _call(
        body, plan, name=name, grid=(count, L // tl),
        in_specs=[pl.BlockSpec((tl, D), lambda n, i: (i, 0)), pl.BlockSpec((tl, tn), lambda n, i: (i, n))]
        + ([] if into is None else [ANY]),
        out_specs=pl.BlockSpec((1, D, tn), lambda n, i: ((n + first) // per, 0, (n + first) % per)),
        out_shape=jax.ShapeDtypeStruct((N_CHIPS, D, wb), dtype),
        scratch_shapes=[pltpu.VMEM((D, tn), F32)],
        aliases={} if into is None else {2: 0},
        sem=("arbitrary", "arbitrary"),
    )(*ins)


def _s5_param_fn(lam_re, lam_im, log_dt, b_re, b_im):
    lr = jnp.minimum(lam_re, -1e-4)
    li = lam_im
    dt = jnp.exp(log_dt)
    mag = jnp.exp(lr * dt)
    ab_re = mag * jnp.cos(li * dt)
    ab_im = mag * jnp.sin(li * dt)
    den = lr * lr + li * li
    n_re = ab_re - 1.0
    n_im = ab_im
    z_re = (n_re * lr + n_im * li) / den
    z_im = (n_im * lr - n_re * li) / den
    bb_re = z_re[None] * b_re - z_im[None] * b_im
    bb_im = z_re[None] * b_im + z_im[None] * b_re
    return ab_re, ab_im, bb_re, bb_im


def s5_params_fwd(lam_re, lam_im, log_dt, b_re, b_im, span):
    G, P = lam_re.shape
    H = b_re.shape[0]
    assert span & (span - 1) == 0

    def body(lr_ref, li_ref, dt_ref, br_ref, bi_ref, abr_ref, abi_ref, bbr_ref, bbi_ref, pr_ref, pi_ref):
        ab_re, ab_im, bb_re, bb_im = _s5_param_fn(lr_ref[...], li_ref[...], dt_ref[...], br_ref[...], bi_ref[...])
        abr_ref[...] = ab_re
        abi_ref[...] = ab_im
        bbr_ref[...] = bb_re
        bbi_ref[...] = bb_im
        cr, ci = ab_re, ab_im
        for _ in range(span.bit_length() - 1):
            cr, ci = cr * cr - ci * ci, 2.0 * cr * ci
        pr_ref[...] = cr
        pi_ref[...] = ci

    shp = lambda *s: jax.ShapeDtypeStruct(s, F32)
    return pl.pallas_call(
        body, name="s5_params_fwd",
        out_shape=[shp(G, P), shp(G, P), shp(H, G, P), shp(H, G, P), shp(G, P), shp(G, P)],
    )(lam_re, lam_im, log_dt, b_re, b_im)


def s5_params_bwd(lam_re, lam_im, log_dt, b_re, b_im, d_ab_re, d_ab_im, d_bb_re, d_bb_im):
    G, P = lam_re.shape
    H = b_re.shape[0]

    def body(lr_ref, li_ref, dt_ref, br_ref, bi_ref, g0, g1, g2, g3, o0, o1, o2, o3, o4):
        prim = (lr_ref[...], li_ref[...], dt_ref[...], br_ref[...], bi_ref[...])
        _, vjp = jax.vjp(_s5_param_fn, *prim)
        d = vjp((jnp.sum(g0[...], axis=0), jnp.sum(g1[...], axis=0), g2[...], g3[...]))
        o0[...], o1[...], o2[...], o3[...], o4[...] = d

    shp = lambda *s: jax.ShapeDtypeStruct(s, F32)
    return pl.pallas_call(
        body, name="s5_params_bwd",
        out_shape=[shp(G, P), shp(G, P), shp(G, 1), shp(H, G, P), shp(H, G, P)],
    )(lam_re, lam_im, log_dt, b_re, b_im, d_ab_re, d_ab_im, d_bb_re, d_bb_im)


def stream_order(a, tl):
    L, C = a.shape
    return a.reshape(L // tl, 8, tl // 8, C).transpose(0, 2, 1, 3).reshape(L, C)


def token_order(a, tl):
    L, C = a.shape
    return a.reshape(L // tl, tl // 8, 8, C).transpose(0, 2, 1, 3).reshape(L, C)


_LANE_BLK = 1024
_LANE_BLK_BWD = 1024


def _cmul_add(ar, ai, xr, xi, br, bi):
    return br + (ar * xr - ai * xi), bi + (ar * xi + ai * xr)


def _cmulc_add(ar, ai, xr, xi, br, bi):
    return br + (ar * xr + ai * xi), bi + (ar * xi - ai * xr)


def _s5_states(u, wbd_ref, a_re, a_im, at_re, at_im, s_re, s_im, e_re, e_im, c0_re, c0_im, tl):
    t8 = tl // 8
    for k in range(S5_KBLK):
        bu = _mm(u[:, 128 * k:128 * (k + 1)], wbd_ref[k])
        s_re[:, 512 * k:512 * (k + 1)] = bu[:, :512]
        s_im[:, 512 * k:512 * (k + 1)] = bu[:, 512:]
    outs_re, outs_im = [], []
    for b in range(S5_LANES // _LANE_BLK):
        lanes = slice(_LANE_BLK * b, _LANE_BLK * (b + 1))
        ar = jnp.broadcast_to(a_re[:, lanes], (8, _LANE_BLK))
        ai = jnp.broadcast_to(a_im[:, lanes], (8, _LANE_BLK))

        def local(i, carry, lanes=lanes, ar=ar, ai=ai):
            r = pl.multiple_of(i * 8, 8)
            sr, si = _cmul_add(ar, ai, carry[0], carry[1], s_re[pl.ds(r, 8), lanes], s_im[pl.ds(r, 8), lanes])
            s_re[pl.ds(r, 8), lanes] = sr
            s_im[pl.ds(r, 8), lanes] = si
            return sr, si

        zero = jnp.zeros((8, _LANE_BLK), F32)
        fr, fi = lax.fori_loop(0, t8, local, (zero, zero), unroll=True)
        tr, ti = at_re[:, lanes], at_im[:, lanes]
        er, ei = c0_re[:, lanes], c0_im[:, lanes]
        ers, eis = [er], [ei]
        for j in range(8):
            er, ei = _cmul_add(tr, ti, er, ei, fr[j:j + 1], fi[j:j + 1])
            ers.append(er)
            eis.append(ei)
        outs_re.append(ers[8])
        outs_im.append(eis[8])
        ent_r, ent_i = jnp.concatenate(ers[:8], axis=0), jnp.concatenate(eis[:8], axis=0)
        e_re[:, lanes] = ent_r
        e_im[:, lanes] = ent_i

        def fix(i, carry, lanes=lanes, ar=ar, ai=ai):
            r = pl.multiple_of(i * 8, 8)
            zr, zi = ar * carry[0] - ai * carry[1], ar * carry[1] + ai * carry[0]
            s_re[pl.ds(r, 8), lanes] = s_re[pl.ds(r, 8), lanes] + zr
            s_im[pl.ds(r, 8), lanes] = s_im[pl.ds(r, 8), lanes] + zi
            return zr, zi

        lax.fori_loop(0, t8, fix, (ent_r, ent_i), unroll=True)
    return jnp.concatenate(outs_re, axis=1), jnp.concatenate(outs_im, axis=1)


def _s5_readout(s_re, s_im, cre_ref, cim_ref):
    ys = []
    for k in range(S5_KBLK):
        lanes = slice(512 * k, 512 * (k + 1))
        ys.append(_mm(s_re[:, lanes], cre_ref[k]) - _mm(s_im[:, lanes], cim_ref[k]))
    return jnp.concatenate(ys, axis=1)


def s5_forward(p, wbd, cre, cim, atab, d_skip, w_glu, b_glu, plan=None):
    L = p.shape[0]
    tl = min(TL_S5, L)
    nch = L // tl

    def body(u_ref, z_ref, wbd_ref, cre_ref, cim_ref, at_ref, d_ref, wg_ref, bg_ref,
             ya_ref, st_re_ref, st_im_ref, sv_re_ref, sv_im_ref, s_re, s_im, e_re, e_im, car_re, car_im):
        @pl.when(pl.program_id(0) == 0)
        def _():
            car_re[...] = jnp.zeros_like(car_re)
            car_im[...] = jnp.zeros_like(car_im)

        c0_re, c0_im = car_re[...], car_im[...]
        st_re_ref[0] = c0_re
        st_im_ref[0] = c0_im
        u = u_ref[...]
        x_re, x_im = _s5_states(u, wbd_ref, at_ref[0:1], at_ref[1:2], at_ref[2:3], at_ref[3:4],
                                s_re, s_im, e_re, e_im, c0_re, c0_im, tl)
        car_re[...] = x_re
        car_im[...] = x_im
        sv_re_ref[...] = s_re[...].astype(sv_re_ref.dtype)
        sv_im_ref[...] = s_im[...].astype(sv_im_ref.dtype)
        y = _s5_readout(sv_re_ref, sv_im_ref, cre_ref, cim_ref) + d_ref[...] * u
        yg = _gelu(y)
        gate = _sigmoid(_mm(yg, wg_ref[...]) + bg_ref[...])
        sz, _ = _silu_and_grad(z_ref[...])
        ya_ref[...] = (yg * gate * sz).astype(ya_ref.dtype)

    return _call(
        body, plan, name="s5_forward", grid=(nch,),
        in_specs=[pl.BlockSpec((tl, 1024), lambda i: (i, 0)), pl.BlockSpec((tl, 1024), lambda i: (i, 1)),
                  _full(wbd.shape), _full(cre.shape), _full(cim.shape), _full(atab.shape),
                  _full((1, 1024)), _full((1024, 1024)), _full((1, 1024))],
        out_specs=[pl.BlockSpec((tl, 1024), lambda i: (i, 0)),
                   pl.BlockSpec((1, 1, S5_LANES), lambda i: (i, 0, 0)),
                   pl.BlockSpec((1, 1, S5_LANES), lambda i: (i, 0, 0)),
                   pl.BlockSpec((tl, S5_LANES), lambda i: (i, 0)), pl.BlockSpec((tl, S5_LANES), lambda i: (i, 0))],
        out_shape=[jax.ShapeDtypeStruct((L, 1024), MXU_DTYPE),
                   jax.ShapeDtypeStruct((nch, 1, S5_LANES), F32), jax.ShapeDtypeStruct((nch, 1, S5_LANES), F32),
                   jax.ShapeDtypeStruct((L, S5_LANES), MXU_DTYPE), jax.ShapeDtypeStruct((L, S5_LANES), MXU_DTYPE)],
        scratch_shapes=[pltpu.VMEM((tl, S5_LANES), F32), pltpu.VMEM((tl, S5_LANES), F32),
                        pltpu.VMEM((8, S5_LANES), F32), pltpu.VMEM((8, S5_LANES), F32),
                        pltpu.VMEM((1, S5_LANES), F32), pltpu.VMEM((1, S5_LANES), F32)],
        sem=("arbitrary",),
    )(p, p, wbd, cre, cim, atab, d_skip, w_glu, b_glu)


def s5_backward(p, dya, st_re, st_im, sv_re, sv_im, wbd, cre, cim, atab, d_skip, w_glu, b_glu, plan=None):
    L = p.shape[0]
    tl = min(TL_S5, L)
    t8 = tl // 8
    nch = L // tl
    rev = lambda i: (nch - 1 - i, 0)
    rev1 = lambda i: (nch - 1 - i, 1)
    rev3 = lambda i: (nch - 1 - i, 0, 0)
    ct_shape = (S5_KBLK, cre.shape[2], cre.shape[1])

    def body(u_ref, z_ref, dya_ref, str_ref, sti_ref, s_re, s_im, wbd_ref, cre_ref, cim_ref, at_ref,
             d_ref, wg_ref, bg_ref,
             dp_ref, dwbd_ref, dcre_ref, dcim_ref, dabr_ref, dabi_ref, dd_ref, dwg_ref, dbg_ref,
             g_re, g_im, car_re, car_im):
        @pl.when(pl.program_id(0) == 0)
        def _():
            car_re[...] = jnp.zeros_like(car_re)
            car_im[...] = jnp.zeros_like(car_im)
            for r in (dwbd_ref, dcre_ref, dcim_ref, dabr_ref, dabi_ref, dd_ref, dwg_ref, dbg_ref):
                r[...] = jnp.zeros_like(r)

        u = u_ref[...]
        a_re, a_im, at_re, at_im = at_ref[0:1], at_ref[1:2], at_ref[2:3], at_ref[3:4]
        y = _s5_readout(s_re, s_im, cre_ref, cim_ref) + d_ref[...] * u
        yg, dyg = _gelu_and_grad(y)
        gate = _sigmoid(_mm(yg, wg_ref[...]) + bg_ref[...])
        sz, dsz = _silu_and_grad(z_ref[...])
        dya = dya_ref[...]
        s5out = yg * gate
        dp_ref[:, 1024:] = (dya * s5out * dsz).astype(dp_ref.dtype)
        ds5 = dya * sz
        dt = ds5 * yg * gate * (1.0 - gate)
        dwg_ref[...] += _mm_tn(yg, dt)
        dbg_ref[...] += jnp.sum(dt, axis=0, keepdims=True)
        dyv = (ds5 * gate + _mm_nt(dt, wg_ref[...])) * dyg
        dd_ref[...] += jnp.sum(dyv * u, axis=0, keepdims=True)

        for k in range(S5_KBLK):
            lanes = slice(512 * k, 512 * (k + 1))
            dyk = dyv[:, 128 * k:128 * (k + 1)]
            g_re[:, lanes] = _mm_nt(dyk, cre_ref[k])
            g_im[:, lanes] = -_mm_nt(dyk, cim_ref[k])
            dcre_ref[k] += _mm_tn(dyk, s_re[:, lanes])
            dcim_ref[k] -= _mm_tn(dyk, s_im[:, lanes])

        blk = _LANE_BLK_BWD
        for b in range(S5_LANES // blk):
            lanes = slice(blk * b, blk * (b + 1))
            ar = jnp.broadcast_to(a_re[:, lanes], (8, blk))
            ai = jnp.broadcast_to(a_im[:, lanes], (8, blk))

            def local(j, carry, lanes=lanes, ar=ar, ai=ai):
                r = pl.multiple_of((t8 - 1 - j) * 8, 8)
                gr, gi = _cmulc_add(ar, ai, carry[0], carry[1], g_re[pl.ds(r, 8), lanes], g_im[pl.ds(r, 8), lanes])
                g_re[pl.ds(r, 8), lanes] = gr
                g_im[pl.ds(r, 8), lanes] = gi
                return gr, gi

            zero = jnp.zeros((8, blk), F32)
            fr, fi = lax.fori_loop(0, t8, local, (zero, zero), unroll=True)
            tr, ti = at_re[:, lanes], at_im[:, lanes]
            hr, hi = car_re[:, lanes], car_im[:, lanes]
            hrs, his = [hr], [hi]
            for j in range(7, -1, -1):
                hr, hi = _cmulc_add(tr, ti, hr, hi, fr[j:j + 1], fi[j:j + 1])
                hrs.append(hr)
                his.append(hi)
            car_re[:, lanes] = hrs[8]
            car_im[:, lanes] = his[8]
            in_r = jnp.concatenate(hrs[7::-1], axis=0)
            in_i = jnp.concatenate(his[7::-1], axis=0)

            wr, wi, nr, ni, accr, acci = in_r, in_i, zero, zero, zero, zero
            for pair in range(t8 // 2 - 1, -1, -1):
                rows = slice(16 * pair, 16 * pair + 16)
                s16r, s16i = s_re[rows, lanes].astype(F32), s_im[rows, lanes].astype(F32)
                for half in (1, 0):
                    r = 16 * pair + 8 * half
                    sr, si = s16r[8 * half:8 * half + 8], s16i[8 * half:8 * half + 8]
                    accr, acci = accr + (sr * nr + si * ni), acci + (sr * ni - si * nr)
                    wr, wi = ar * wr + ai * wi, ar * wi - ai * wr
                    nr, ni = g_re[r:r + 8, lanes] + wr, g_im[r:r + 8, lanes] + wi
                    g_re[r:r + 8, lanes] = nr
                    g_im[r:r + 8, lanes] = ni
            lr, li = s_re[tl - 16:tl, lanes].astype(F32)[8:], s_im[tl - 16:tl, lanes].astype(F32)[8:]
            row0 = lax.broadcasted_iota(jnp.int32, (8, blk), 0) == 0
            sr = jnp.where(row0, jnp.broadcast_to(str_ref[0][:, lanes], (8, blk)), pltpu.roll(lr, 1, 0))
            si = jnp.where(row0, jnp.broadcast_to(sti_ref[0][:, lanes], (8, blk)), pltpu.roll(li, 1, 0))
            dabr_ref[:, lanes] += accr + (sr * nr + si * ni)
            dabi_ref[:, lanes] += acci + (sr * ni - si * nr)

        dus = []
        for k in range(S5_KBLK):
            lanes = slice(512 * k, 512 * (k + 1))
            g = jnp.concatenate([g_re[:, lanes], g_im[:, lanes]], axis=1)
            dwbd_ref[k] += _mm_tn(u[:, 128 * k:128 * (k + 1)], g)
            dus.append(_mm_nt(g, wbd_ref[k]))
        du = jnp.concatenate(dus, axis=1) + dyv * d_ref[...]
        dp_ref[:, :1024] = du.astype(dp_ref.dtype)

    shp = lambda *s: jax.ShapeDtypeStruct(s, F32)
    return _call(
        body, plan, name="s5_backward", grid=(nch,),
        in_specs=[pl.BlockSpec((tl, 1024), rev), pl.BlockSpec((tl, 1024), rev1), pl.BlockSpec((tl, 1024), rev),
                  pl.BlockSpec((1, 1, S5_LANES), rev3), pl.BlockSpec((1, 1, S5_LANES), rev3),
                  pl.BlockSpec((tl, S5_LANES), rev), pl.BlockSpec((tl, S5_LANES), rev),
                  _full(wbd.shape), _full(cre.shape), _full(cim.shape), _full(atab.shape),
                  _full((1, 1024)), _full((1024, 1024)), _full((1, 1024))],
        out_specs=[pl.BlockSpec((tl, 2048), rev), _full(wbd.shape), _full(ct_shape), _full(ct_shape),
                   _full((8, S5_LANES)), _full((8, S5_LANES)), _full((1, 1024)), _full((1024, 1024)), _full((1, 1024))],
        out_shape=[jax.ShapeDtypeStruct((L, 2048), MXU_DTYPE), shp(*wbd.shape), shp(*ct_shape), shp(*ct_shape),
                   shp(8, S5_LANES), shp(8, S5_LANES), shp(1, 1024), shp(1024, 1024), shp(1, 1024)],
        scratch_shapes=[pltpu.VMEM((tl, S5_LANES), F32), pltpu.VMEM((tl, S5_LANES), F32),
                        pltpu.VMEM((1, S5_LANES), F32), pltpu.VMEM((1, S5_LANES), F32)],
        sem=("arbitrary",),
    )(p, p, dya, st_re, st_im, sv_re, sv_im, wbd, cre, cim, atab, d_skip, w_glu, b_glu)


def _block_diag(w, rows_first):
    g8 = w.reshape(S5_KBLK, 8, w.shape[1], w.shape[2])
    eye = jnp.eye(8, dtype=w.dtype)
    out = jnp.einsum('kgab,fg->kfagb', g8, eye)
    return out.reshape(S5_KBLK, 8 * w.shape[1], 8 * w.shape[2])


def _block_diag_extract(wbd, a, b):
    w5 = wbd.reshape(S5_KBLK, 8, a, 8, b)
    idx = jnp.arange(8)
    return w5[:, idx, :, idx, :].transpose(1, 0, 2, 3).reshape(S5_GROUPS, a, b)


def _ret_constants():
    log_g = np.log1p(-np.exp2(-5.0 - np.arange(RET_HEADS, dtype=np.float32))).astype(np.float32)
    idx = np.arange(RET_CHUNK, dtype=np.float32)
    diff = idx[:, None] - idx[None, :]
    decay = np.where(diff >= 0, np.exp(log_g[:, None, None] * np.maximum(diff, 0.0)), 0.0).astype(np.float32)
    xi = np.exp(log_g[None, :] * (idx[:, None] + 1.0)).astype(np.float32)
    zeta = np.exp(log_g[None, :] * (RET_CHUNK - 1.0 - idx[:, None])).astype(np.float32)
    chunk_decay = np.exp(log_g * RET_CHUNK).astype(np.float32)
    return decay, xi, zeta, chunk_decay


def _rope_tables(L):
    half = RET_DK // 2
    inv = ROPE_BASE ** (-jnp.arange(half, dtype=F32) / half)
    ang = jnp.arange(L, dtype=F32)[:, None] * inv[None, :]
    return jnp.cos(ang), jnp.sin(ang)


def _rot(xh, cos, sin):
    x1, x2 = xh[:, :128], xh[:, 128:]
    return jnp.concatenate([x1 * cos - x2 * sin, x1 * sin + x2 * cos], axis=1)


def _rot_t(dh, cos, sin):
    d1, d2 = dh[:, :128], dh[:, 128:]
    return jnp.concatenate([d1 * cos + d2 * sin, d2 * cos - d1 * sin], axis=1)


RET_PER_STEP = 2


def _ret_setup(L):
    nc = L // RET_CHUNK
    per = RET_PER_STEP if nc % RET_PER_STEP == 0 else 1
    decay_np, xi_np, zeta_np, cd_np = _ret_constants()
    tables = (jnp.asarray(decay_np), jnp.asarray(np.tile(xi_np, (per, 1))), jnp.asarray(np.tile(zeta_np, (per, 1))))
    return nc // per, per, tables, [float(c) for c in cd_np]


def _ret_rows(q_ref, k_ref, v_ref, cos_ref, sin_ref, xi_ref, zeta_ref):
    H = range(RET_HEADS)
    hs = [slice(RET_DK * h, RET_DK * (h + 1)) for h in H]
    cs, sn = cos_ref[...], sin_ref[...]
    qh = [_rot(q_ref[:, hs[h]], cs, sn) for h in H]
    kh = [_rot(k_ref[:, hs[h]], cs, sn) * (RET_DK ** -0.5) for h in H]
    vh = [v_ref[:, hs[h]] for h in H]
    qx = [qh[h] * xi_ref[:, h:h + 1] for h in H]
    kz = [kh[h] * zeta_ref[:, h:h + 1] for h in H]
    return hs, cs, sn, qh, kh, vh, qx, kz


def _ret_normed(qh, kh, vh, qx, dec_ref, prevs, per):
    H, C = range(RET_HEADS), range(per)
    rs = [slice(RET_CHUNK * c, RET_CHUNK * (c + 1)) for c in C]
    sc = [[_mm_nt(qh[h][rs[c]], kh[h][rs[c]]) * dec_ref[h] for h in H] for c in C]
    inner = [[_mm(sc[c][h], vh[h][rs[c]]) for h in H] for c in C]
    cross = [[_mm(qx[h][rs[c]], prevs[c][h]) for h in H] for c in C]
    o = [jnp.concatenate([inner[c][h] + cross[c][h] for c in C], axis=0) for h in H]
    oc = [o[h] - jnp.mean(o[h], axis=-1, keepdims=True) for h in H]
    rstd = [lax.rsqrt(jnp.mean(oc[h] * oc[h], axis=-1, keepdims=True) + NORM_EPS) for h in H]
    on = [oc[h] * rstd[h] for h in H]
    return rs, sc, rstd, on


def retention_forward(p, cos, sin, gain):
    L = p.shape[0]
    steps, per, (decay, xi, zeta), cd = _ret_setup(L)
    rows = RET_CHUNK * per

    def body(q_ref, k_ref, v_ref, z_ref, cos_ref, sin_ref, dec_ref, xi_ref, zeta_ref, gain_ref,
             yb_ref, prev_ref, state):
        @pl.when(pl.program_id(0) == 0)
        def _():
            state[...] = jnp.zeros_like(state)

        H, C = range(RET_HEADS), range(per)
        hs, cs, sn, qh, kh, vh, qx, kz = _ret_rows(q_ref, k_ref, v_ref, cos_ref, sin_ref, xi_ref, zeta_ref)
        prevs = [[state[h] for h in H]]
        for c in C:
            rs_c = slice(RET_CHUNK * c, RET_CHUNK * (c + 1))
            prevs.append([prevs[c][h] * cd[h] + _mm_tn(kz[h][rs_c], vh[h][rs_c]) for h in H])
        _, _, _, on = _ret_normed(qh, kh, vh, qx, dec_ref, prevs, per)
        sz, _ = _silu_and_grad(z_ref[...])
        for h in H:
            for c in C:
                prev_ref[c, h] = prevs[c][h].astype(prev_ref.dtype)
            state[h] = prevs[per][h]
            yb_ref[:, hs[h]] = (on[h] * gain_ref[:, hs[h]] * sz[:, hs[h]]).astype(yb_ref.dtype)

    col0 = p.shape[1] // 1024 - 4
    blk = lambda c: pl.BlockSpec((rows, 1024), lambda i, c=c: (i, c + col0))
    return pl.pallas_call(
        body, name="retention_forward", grid=(steps,),
        in_specs=[blk(0), blk(1), blk(2), blk(3),
                  pl.BlockSpec((rows, 128), lambda i: (i, 0)), pl.BlockSpec((rows, 128), lambda i: (i, 0)),
                  _full(decay.shape), _full(xi.shape), _full(zeta.shape), _full((1, 1024))],
        out_specs=[pl.BlockSpec((rows, 1024), lambda i: (i, 0)),
                   pl.BlockSpec((per, RET_HEADS, RET_DK, RET_DK), lambda i: (i, 0, 0, 0))],
        out_shape=[jax.ShapeDtypeStruct((L, 1024), MXU_DTYPE),
                   jax.ShapeDtypeStruct((steps * per, RET_HEADS, RET_DK, RET_DK), MXU_DTYPE)],
        scratch_shapes=[pltpu.VMEM((RET_HEADS, RET_DK, RET_DK), F32)],
        compiler_params=_cparams(("arbitrary",)),
    )(p, p, p, p, cos, sin, decay, xi, zeta, gain)


def retention_backward(p, dy, prevs, cos, sin, gain, plan=None):
    L = p.shape[0]
    steps, per, (decay, xi, zeta), cd = _ret_setup(L)
    rows = RET_CHUNK * per
    scale = RET_DK ** -0.5

    def body(q_ref, k_ref, v_ref, z_ref, dyb_ref, prev_ref, cos_ref, sin_ref, dec_ref, xi_ref, zeta_ref, gain_ref,
             dp_ref, dgain_ref, dstate):
        @pl.when(pl.program_id(0) == 0)
        def _():
            dstate[...] = jnp.zeros_like(dstate)
            dgain_ref[...] = jnp.zeros_like(dgain_ref)

        H, C = range(RET_HEADS), range(per)
        hs, cs, sn, qh, kh, vh, qx, kz = _ret_rows(q_ref, k_ref, v_ref, cos_ref, sin_ref, xi_ref, zeta_ref)
        prevs = [[prev_ref[c, h] for h in H] for c in C]
        rs, sc, rstd, on = _ret_normed(qh, kh, vh, qx, dec_ref, prevs, per)
        sz, dsz = _silu_and_grad(z_ref[...])
        dyb = dyb_ref[...]
        dong = [dyb[:, hs[h]] * sz[:, hs[h]] for h in H]
        don = [dong[h] * gain_ref[:, hs[h]] for h in H]
        do = [rstd[h] * (don[h] - jnp.mean(don[h], axis=-1, keepdims=True)
                         - on[h] * jnp.mean(don[h] * on[h], axis=-1, keepdims=True)) for h in H]
        dsc = [[_mm_nt(do[h][rs[c]], vh[h][rs[c]]) * dec_ref[h] for h in H] for c in C]
        dq_st = [[_mm_nt(do[h][rs[c]], prevs[c][h]) for h in H] for c in C]
        dnew = [[_mm_tn(qx[h][rs[c]], do[h][rs[c]]) for h in H] for c in C]
        dsts = [None] * per + [[dstate[h] for h in H]]
        for c in reversed(C):
            dsts[c] = [dsts[c + 1][h] * cd[h] + dnew[c][h] for h in H]
        dk_st = [[_mm_nt(vh[h][rs[c]], dsts[c + 1][h]) for h in H] for c in C]
        dv_st = [[_mm(kz[h][rs[c]], dsts[c + 1][h]) for h in H] for c in C]
        rows_of = lambda parts: jnp.concatenate(parts, axis=0)
        dqh = [rows_of([_mm(dsc[c][h], kh[h][rs[c]]) for c in C])
               + rows_of([dq_st[c][h] for c in C]) * xi_ref[:, h:h + 1] for h in H]
        dkh = [rows_of([_mm_tn(dsc[c][h], qh[h][rs[c]]) for c in C])
               + rows_of([dk_st[c][h] for c in C]) * zeta_ref[:, h:h + 1] for h in H]
        dvh = [rows_of([_mm_tn(sc[c][h], do[h][rs[c]]) + dv_st[c][h] for c in C]) for h in H]
        for h in H:
            dstate[h] = dsts[0][h]
            dgain_ref[:, hs[h]] += jnp.sum(dong[h] * on[h], axis=0, keepdims=True)
            dp_ref[:, hs[h]] = _rot_t(dqh[h], cs, sn).astype(dp_ref.dtype)
            dp_ref[:, 1024 + RET_DK * h:1024 + RET_DK * (h + 1)] = (_rot_t(dkh[h], cs, sn) * scale).astype(dp_ref.dtype)
            dp_ref[:, 2048 + RET_DK * h:2048 + RET_DK * (h + 1)] = dvh[h].astype(dp_ref.dtype)
            dp_ref[:, 3072 + RET_DK * h:3072 + RET_DK * (h + 1)] = (
                dyb[:, hs[h]] * on[h] * gain_ref[:, hs[h]] * dsz[:, hs[h]]).astype(dp_ref.dtype)

    col0 = p.shape[1] // 1024 - 4
    blk = lambda c: pl.BlockSpec((rows, 1024), lambda i, c=c: (steps - 1 - i, c + col0))
    tab = pl.BlockSpec((rows, 128), lambda i: (steps - 1 - i, 0))
    return _call(
        body, plan, name="retention_backward", grid=(steps,),
        in_specs=[blk(0), blk(1), blk(2), blk(3), pl.BlockSpec((rows, 1024), lambda i: (steps - 1 - i, 0)),
                  pl.BlockSpec((per, RET_HEADS, RET_DK, RET_DK), lambda i: (steps - 1 - i, 0, 0, 0)),
                  tab, tab, _full(decay.shape), _full(xi.shape), _full(zeta.shape), _full((1, 1024))],
        out_specs=[pl.BlockSpec((rows, 4096), lambda i: (steps - 1 - i, 0)), _full((1, 1024))],
        out_shape=[jax.ShapeDtypeStruct((L, 4096), MXU_DTYPE), jax.ShapeDtypeStruct((1, 1024), F32)],
        scratch_shapes=[pltpu.VMEM((RET_HEADS, RET_DK, RET_DK), F32)],
        sem=("arbitrary",),
    )(p, p, p, p, dy, prevs, cos, sin, decay, xi, zeta, gain)


def _sgu_mix(p_ref, gain_ref, wm_ref, bt_ref, tl):
    pu, pv, z = p_ref[:, :2048], p_ref[:, 2048:4096], p_ref[:, 4096:]
    (u, du), (v, dv) = _gelu_and_grad(pu), _gelu_and_grad(pv)
    mu = jnp.mean(v, axis=-1, keepdims=True)
    vc = v - mu
    rstd = lax.rsqrt(jnp.mean(vc * vc, axis=-1, keepdims=True) + NORM_EPS)
    vn = vc * rstd
    vg = vn * gain_ref[...]
    mask = (lax.broadcasted_iota(jnp.int32, (SGU_CHUNK, SGU_CHUNK), 0)
            >= lax.broadcasted_iota(jnp.int32, (SGU_CHUNK, SGU_CHUNK), 1))
    wms = [jnp.where(mask, wm_ref[g], 0.0) for g in range(SGU_GROUPS)]
    rows = []
    for c in range(tl // SGU_CHUNK):
        rs = slice(SGU_CHUNK * c, SGU_CHUNK * (c + 1))
        cols = []
        for g in range(SGU_GROUPS):
            gs = slice(SGU_GDIM * g, SGU_GDIM * (g + 1))
            cols.append(_mm(wms[g], vg[rs, gs]) + bt_ref[:, g:g + 1])
        rows.append(jnp.concatenate(cols, axis=1))
    s = rows[0] if len(rows) == 1 else jnp.concatenate(rows, axis=0)
    return du, dv, z, u, vn, rstd, vg, wms, mask, s


def sgu_forward(p, gain, wm, bt):
    L = p.shape[0]
    tl = min(TL_SGU, L)

    def body(p_ref, gain_ref, wm_ref, bt_ref, y_ref):
        _, _, z, u, _, _, _, _, _, s = _sgu_mix(p_ref, gain_ref, wm_ref, bt_ref, tl)
        sz, _ = _silu_and_grad(z)
        y_ref[...] = (u * s * sz).astype(y_ref.dtype)

    return pl.pallas_call(
        body, name="sgu_forward", grid=(L // tl,),
        in_specs=[pl.BlockSpec((tl, ODD_IN), lambda i: (i, 0)), _full((1, 2048)), _full(wm.shape), _full(bt.shape)],
        out_specs=pl.BlockSpec((tl, 2048), lambda i: (i, 0)),
        out_shape=jax.ShapeDtypeStruct((L, 2048), MXU_DTYPE),
        compiler_params=_cparams(("arbitrary",)),
    )(p, gain, wm, bt)


def sgu_backward(p, dy, gain, wm, bt, plan=None):
    L = p.shape[0]
    tl = min(TL_SGU, L)

    def body(p_ref, dy_ref, gain_ref, wm_ref, bt_ref, dp_ref, dgain_ref, dwm_ref, dbt_ref):
        @pl.when(pl.program_id(0) == 0)
        def _():
            dgain_ref[...] = jnp.zeros_like(dgain_ref)
            dwm_ref[...] = jnp.zeros_like(dwm_ref)
            dbt_ref[...] = jnp.zeros_like(dbt_ref)

        gu, gv, z, u, vn, rstd, vg, wms, mask, s = _sgu_mix(p_ref, gain_ref, wm_ref, bt_ref, tl)
        sz, dsz = _silu_and_grad(z)
        dyv = dy_ref[...]
        dp_ref[:, 4096:] = (dyv * u * s * dsz).astype(dp_ref.dtype)
        dsg = dyv * sz
        dp_ref[:, :2048] = (dsg * s * gu).astype(dp_ref.dtype)
        ds = dsg * u
        rows = []
        dbs = [jnp.zeros((SGU_CHUNK, 1), F32) for _ in range(SGU_GROUPS)]
        for c in range(tl // SGU_CHUNK):
            rs = slice(SGU_CHUNK * c, SGU_CHUNK * (c + 1))
            cols = []
            for g in range(SGU_GROUPS):
                gs = slice(SGU_GDIM * g, SGU_GDIM * (g + 1))
                dsg_c = ds[rs, gs]
                dbs[g] = dbs[g] + jnp.sum(dsg_c, axis=1, keepdims=True)
                dwm_ref[g] += jnp.where(mask, _mm_nt(dsg_c, vg[rs, gs]), 0.0)
                cols.append(_mm_tn(wms[g], dsg_c))
            rows.append(jnp.concatenate(cols, axis=1))
        dbt_ref[...] += jnp.concatenate(dbs, axis=1)
        dvg = rows[0] if len(rows) == 1 else jnp.concatenate(rows, axis=0)
        dgain_ref[...] += jnp.sum(dvg * vn, axis=0, keepdims=True)
        dvn = dvg * gain_ref[...]
        dv = rstd * (dvn - jnp.mean(dvn, axis=-1, keepdims=True) - vn * jnp.mean(dvn * vn, axis=-1, keepdims=True))
        dp_ref[:, 2048:4096] = (dv * gv).astype(dp_ref.dtype)

    return _call(
        body, plan, name="sgu_backward", grid=(L // tl,),
        in_specs=[pl.BlockSpec((tl, ODD_IN), lambda i: (i, 0)), pl.BlockSpec((tl, 2048), lambda i: (i, 0)),
                  _full((1, 2048)), _full(wm.shape), _full(bt.shape)],
        out_specs=[pl.BlockSpec((tl, ODD_IN), lambda i: (i, 0)), _full((1, 2048)), _full(wm.shape), _full(bt.shape)],
        out_shape=[jax.ShapeDtypeStruct((L, ODD_IN), MXU_DTYPE), jax.ShapeDtypeStruct((1, 2048), F32),
                   jax.ShapeDtypeStruct(wm.shape, F32), jax.ShapeDtypeStruct(bt.shape, F32)],
        sem=("arbitrary",),
    )(p, dy, gain, wm, bt)


def cast_shards(mats):
    n = len(mats)
    steps = 8

    def body(*refs):
        for p in range(n):
            refs[n + p][...] = refs[p][...].astype(MXU_DTYPE)

    specs = [pl.BlockSpec((m.shape[0] // steps, m.shape[1]), lambda i: (i, 0)) for m in mats]
    return pl.pallas_call(
        body, name="cast_shards", grid=(steps,), in_specs=specs, out_specs=specs,
        out_shape=[jax.ShapeDtypeStruct(m.shape, MXU_DTYPE) for m in mats],
        compiler_params=_cparams(("arbitrary",)),
    )(*mats)


def local_grads(x, tgt, w):
    L = x.shape[0]
    ne, gf = w["norm_even"], w["final_norm"].reshape(1, D_MODEL)
    sh = dict(zip(MATRICES, cast_shards([w[n][0] for n in MATRICES])))
    lam_re, lam_im = w["s5_lam_re"][0], w["s5_lam_im"][0]
    log_dt = w["s5_log_dt"].reshape(S5_GROUPS, 1)
    bt_re = jnp.transpose(w["s5_b_re"][0], (2, 0, 1))
    bt_im = jnp.transpose(w["s5_b_im"][0], (2, 0, 1))
    c_re, c_im = w["s5_c_re"][0], w["s5_c_im"][0]
    wm = w["sgu_w_spatial"][0]
    bt = jnp.transpose(w["sgu_b_spatial"][0])

    tl5 = min(TL_S5, L)
    ab_re, ab_im, bb_re, bb_im, at_re, at_im = s5_params_fwd(lam_re, lam_im, log_dt, bt_re, bt_im, tl5 // 8)
    atab = jnp.stack([ab_re.reshape(S5_LANES), ab_im.reshape(S5_LANES),
                      at_re.reshape(S5_LANES), at_im.reshape(S5_LANES)])
    wbd = jnp.concatenate([_block_diag(jnp.transpose(bb_re, (1, 0, 2)), True),
                           _block_diag(jnp.transpose(bb_im, (1, 0, 2)), True)], axis=2).astype(MXU_DTYPE)
    cre = _block_diag(jnp.transpose(c_re, (0, 2, 1)), True).astype(MXU_DTYPE)
    cim = _block_diag(jnp.transpose(c_im, (0, 2, 1)), True).astype(MXU_DTYPE)
    cos, sin = _rope_tables(L)

    s5_cols = 2 * S5_WIDTH
    me = (2 * lax.axis_index("x") + lax.axis_index("y")).astype(jnp.int32)
    xs = stream_order(x, tl5)
    (p1, h0s, h0), (w_in_e, w_glu) = even_in_slabs(
        x, xs, ne, sh["w_in_even"][None], jnp.stack([me]), jnp.zeros((1,), jnp.int32), "even_in_own",
        plan=gather_plan([sh["w_in_even"], sh["s5_w_glu"]]))
    others = jnp.stack([me ^ 1, me ^ 2, me ^ 3])
    (p1,), (w_out_e,) = even_in_slabs(x, xs, ne, w_in_e, others, others, "even_in_rest", p_in=p1,
                                      plan=gather_plan([sh["w_out_even"]]))
    w_s5 = jnp.concatenate([w_in_e[0], w_in_e[1][:, :s5_cols - EVEN_IN // N_CHIPS]], axis=1)
    w_ret = jnp.concatenate([w_in_e[1][:, s5_cols - EVEN_IN // N_CHIPS:], w_in_e[2], w_in_e[3]], axis=1)
    w_glu = w_glu.reshape(S5_WIDTH, S5_WIDTH)
    w_out_e = w_out_e.reshape(2 * S5_WIDTH, D_MODEL)
    (ya, st_re, st_im, sv_re, sv_im), (w_in_o, w_out_o, no, sg_gain) = s5_forward(
        p1, wbd, cre, cim, atab, w["s5_d"], w_glu, w["s5_b_glu"],
        gather_plan([sh["w_in_odd"], sh["w_out_odd"], w["norm_odd"], w["sgu_norm_gain"]]))
    w_out_o = w_out_o.reshape(SGU_WIDTH, D_MODEL)
    no, sg_gain = no.reshape(1, D_MODEL), sg_gain.reshape(1, SGU_WIDTH)
    yb, prevs = retention_forward(p1, cos, sin, w["ret_gn_gain"])
    ya = token_order(ya, tl5)
    x1 = matmul_residual([ya, yb], w_out_e, x, "even_out")
    (p2, h1), _ = norm_matmul(x1, no, w_in_o, "odd_in")
    y2 = sgu_forward(p2, sg_gain, wm, bt)
    dx2, loss, dgf = out_proj_loss(y2, w_out_o, x1, gf, tgt, "odd_out_loss")

    g, landed = {}, {}
    shard_major = lambda a, n: a.reshape((N_CHIPS,) + w[n].shape[1:])
    dy2, g_w_out_o = out_proj_bwd(dx2, w_out_o, [y2], "odd_out_bwd")
    (dp2, g["sgu_norm_gain"], dwm, dbt), (landed["w_out_odd"],) = sgu_backward(
        p2, dy2, sg_gain, wm, bt, reduce_plan([shard_major(g_w_out_o, "w_out_odd")]))
    g_w_in_o, _ = in_proj_bwd_dw(h1, dp2, "odd_in_dw", ODD_IN // N_CHIPS)
    (dx1, g["norm_odd"]), _ = in_proj_bwd_dx(x1, no, [dp2], [w_in_o], dx2, "odd_in_dx")
    dya, dyb, g_w_out_e = out_proj_bwd(dx1, w_out_e, [ya, yb], "even_out_bwd")
    ((dpa, dwbd, dcre, dcim, dab_re, dab_im, g["s5_d"], g_w_glu, g["s5_b_glu"]),
     (landed["w_in_odd"], landed["w_out_even"])) = s5_backward(
        p1, stream_order(dya, tl5), st_re, st_im, sv_re, sv_im, wbd, cre, cim, atab, w["s5_d"], w_glu,
        w["s5_b_glu"], reduce_plan([g_w_in_o, shard_major(g_w_out_e, "w_out_even")]))

    dbb_re = jnp.transpose(_block_diag_extract(dwbd[:, :, :512], S5_GROUP, S5_STATE), (1, 0, 2))
    dbb_im = jnp.transpose(_block_diag_extract(dwbd[:, :, 512:], S5_GROUP, S5_STATE), (1, 0, 2))
    dlr, dli, ddt, dbt_re, dbt_im = s5_params_bwd(
        lam_re, lam_im, log_dt, bt_re, bt_im, dab_re.reshape(8, S5_GROUPS, S5_STATE),
        dab_im.reshape(8, S5_GROUPS, S5_STATE), dbb_re, dbb_im)
    g["s5_lam_re"], g["s5_lam_im"] = dlr[None], dli[None]
    g["s5_log_dt"] = ddt.reshape(1, S5_GROUPS)
    g["s5_b_re"], g["s5_b_im"] = dbt_re, dbt_im
    g["s5_c_re"] = _block_diag_extract(dcre, S5_GROUP, S5_STATE)[None]
    g["s5_c_im"] = _block_diag_extract(dcim, S5_GROUP, S5_STATE)[None]
    g["sgu_w_spatial"] = dwm[None]
    g["sgu_b_spatial"] = jnp.transpose(dbt)[None]
    g["final_norm"] = dgf.reshape(D_MODEL)
    g["loss"] = loss

    (dpb, g["ret_gn_gain"]), (landed["s5_w_glu"],) = retention_backward(
        p1, dyb, prevs, cos, sin, w["ret_gn_gain"], reduce_plan([shard_major(g_w_glu, "s5_w_glu")]))
    done = tuple(n for n in MATRICES if n != "w_in_even")
    part = {n: sum_slabs(landed[n], "sum_" + n) for n in done}
    g_w_in_e, recv = in_proj_bwd_dw(h0s, dpa, "even_in_dw_s5", 512, dtype=MXU_DTYPE,
                                    plan=_SiblingPlan([part[n] for n in done]))
    other = dict(zip(done, recv))
    small = tuple(n for n in SMALL if n != "norm_even") + ("loss",)
    g_w_in_e, recv = in_proj_bwd_dw(h0, dpb, "even_in_dw_ret", 512, first=s5_cols // 512, into=g_w_in_e,
                                    dtype=MXU_DTYPE, plan=reduce_plan([], [g[n] for n in small]))
    landed.update(zip(small, recv))
    (dx0, g["norm_even"]), (landed["w_in_even"],) = in_proj_bwd_dx(
        x, ne, [token_order(dpa, tl5), dpb], [w_s5, w_ret], dx1, "even_in_dx", reduce_plan([g_w_in_e]))
    (landed["norm_even"],) = run_plan(reduce_plan([], [g["norm_even"]]), "exchange_norm_even")
    return dx0, landed, part, other


def _row_block(rows):
    return 128 if rows % 128 == 0 else rows


def sum_slabs(r, name):
    _, R, C = r.shape
    tr = _row_block(R)

    def body(r_ref, o_ref):
        a, b, c, d = (r_ref[k].astype(F32) for k in range(N_CHIPS))
        o_ref[...] = (a + b) + (c + d)

    return pl.pallas_call(
        body, name=name, grid=(R // tr,),
        in_specs=[pl.BlockSpec((N_CHIPS, tr, C), lambda i: (0, i, 0))],
        out_specs=pl.BlockSpec((tr, C), lambda i: (i, 0)),
        out_shape=jax.ShapeDtypeStruct((R, C), F32),
        compiler_params=_cparams(("arbitrary",)),
    )(r)


def _adam(w, m, v, g):
    mn = ADAM_B1 * m + (1.0 - ADAM_B1) * g
    vn = ADAM_B2 * v + (1.0 - ADAM_B2) * (g * g)
    m_hat = mn / (1.0 - ADAM_B1 ** ADAM_STEP)
    v_hat = vn / (1.0 - ADAM_B2 ** ADAM_STEP)
    return -ADAM_LR * (m_hat / (jnp.sqrt(v_hat) + ADAM_EPS) + ADAM_WD * w), mn, vn


def adam_update(w, m, v, ga, gb, name):
    R, C = w.shape
    tr = _row_block(R)

    def body(w_ref, m_ref, v_ref, ga_ref, gb_ref, g_out, d_out, m_out, v_out):
        g = ga_ref[...] + gb_ref[...]
        g_out[...] = g
        d_out[...], m_out[...], v_out[...] = _adam(w_ref[...], m_ref[...], v_ref[...], g)

    blk = pl.BlockSpec((tr, C), lambda i: (i, 0))
    return pl.pallas_call(
        body, name=name, grid=(R // tr,),
        in_specs=[blk] * 5, out_specs=[blk] * 4,
        out_shape=[jax.ShapeDtypeStruct((R, C), F32)] * 4,
        compiler_params=_cparams(("arbitrary",)),
    )(w, m, v, ga, gb)


WIDE_ROWS = ("s5_b_re", "s5_b_im")


def sum_small(landed):
    def body(*refs):
        k = len(refs) // 2
        for i in range(k):
            r = refs[i]
            refs[k + i][...] = (r[0] + r[1]) + (r[2] + r[3])

    names = list(landed)
    res = pl.pallas_call(
        body, name="sum_small", out_shape=[jax.ShapeDtypeStruct(landed[n].shape[1:], F32) for n in names],
        compiler_params=pltpu.CompilerParams(vmem_limit_bytes=VMEM_LIMIT),
    )(*[landed[n] for n in names])
    return dict(zip(names, res))


def adam_small(names, w, m, v, ga, gb):
    def body(*refs):
        k = len(refs) // 9
        me = 2 * lax.axis_index("x") + lax.axis_index("y")
        for i in range(k):
            w_ref, m_ref, v_ref, ga_ref, gb_ref = refs[i], refs[k + i], refs[2 * k + i], refs[3 * k + i], refs[4 * k + i]
            size = w_ref.shape[-1]
            if ga_ref.shape != w_ref.shape:
                part = pl.ds(pl.multiple_of(me * size, LANES), size)
                g = ga_ref[:, part] + gb_ref[:, part]
            else:
                g = ga_ref[...] + gb_ref[...]
            refs[5 * k + i][...] = g
            refs[6 * k + i][...], refs[7 * k + i][...], refs[8 * k + i][...] = _adam(w_ref[...], m_ref[...], v_ref[...], g)

    ins = [d[n] for d in (w, m, v, ga, gb) for n in names]
    outs = [jax.ShapeDtypeStruct(w[n].shape, F32) for _ in range(4) for n in names]
    res = pl.pallas_call(body, name="adam_small", out_shape=outs,
                         compiler_params=pltpu.CompilerParams(vmem_limit_bytes=VMEM_LIMIT))(*ins)
    k = len(names)
    return [dict(zip(names, res[j * k:(j + 1) * k])) for j in range(4)]


WEIGHTS = ("norm_even", "w_in_even", "s5_lam_re", "s5_lam_im", "s5_log_dt", "s5_b_re", "s5_b_im", "s5_c_re",
           "s5_c_im", "s5_d", "s5_w_glu", "s5_b_glu", "ret_gn_gain", "w_out_even", "norm_odd", "w_in_odd",
           "sgu_norm_gain", "sgu_w_spatial", "sgu_b_spatial", "w_out_odd", "final_norm")
MATRICES = ("w_in_even", "s5_w_glu", "w_out_even", "w_in_odd", "w_out_odd")
SHARDED_VECS = ("norm_odd", "sgu_norm_gain")
REPLICATED = tuple(n for n in WEIGHTS if n not in MATRICES and n not in SHARDED_VECS)
SMALL = tuple(n for n in WEIGHTS if n not in MATRICES)
LANES = 128


def kernel(x, norm_even, w_in_even, s5_lam_re, s5_lam_im, s5_log_dt, s5_b_re, s5_b_im, s5_c_re, s5_c_im, s5_d, s5_w_glu, s5_b_glu, ret_gn_gain, w_out_even, norm_odd, w_in_odd, sgu_norm_gain, sgu_w_spatial, sgu_b_spatial, w_out_odd, final_norm, loss_target, m_norm_even, m_w_in_even, m_s5_lam_re, m_s5_lam_im, m_s5_log_dt, m_s5_b_re, m_s5_b_im, m_s5_c_re, m_s5_c_im, m_s5_d, m_s5_w_glu, m_s5_b_glu, m_ret_gn_gain, m_w_out_even, m_norm_odd, m_w_in_odd, m_sgu_norm_gain, m_sgu_w_spatial, m_sgu_b_spatial, m_w_out_odd, m_final_norm, v_norm_even, v_w_in_even, v_s5_lam_re, v_s5_lam_im, v_s5_log_dt, v_s5_b_re, v_s5_b_im, v_s5_c_re, v_s5_c_im, v_s5_d, v_s5_w_glu, v_s5_b_glu, v_ret_gn_gain, v_w_out_even, v_norm_odd, v_w_in_odd, v_sgu_norm_gain, v_sgu_w_spatial, v_sgu_b_spatial, v_w_out_odd, v_final_norm):
    w = dict(norm_even=norm_even, w_in_even=w_in_even, s5_lam_re=s5_lam_re, s5_lam_im=s5_lam_im, s5_log_dt=s5_log_dt, s5_b_re=s5_b_re, s5_b_im=s5_b_im, s5_c_re=s5_c_re, s5_c_im=s5_c_im, s5_d=s5_d, s5_w_glu=s5_w_glu, s5_b_glu=s5_b_glu, ret_gn_gain=ret_gn_gain, w_out_even=w_out_even, norm_odd=norm_odd, w_in_odd=w_in_odd, sgu_norm_gain=sgu_norm_gain, sgu_w_spatial=sgu_w_spatial, sgu_b_spatial=sgu_b_spatial, w_out_odd=w_out_odd, final_norm=final_norm)
    m = dict(norm_even=m_norm_even, w_in_even=m_w_in_even, s5_lam_re=m_s5_lam_re, s5_lam_im=m_s5_lam_im, s5_log_dt=m_s5_log_dt, s5_b_re=m_s5_b_re, s5_b_im=m_s5_b_im, s5_c_re=m_s5_c_re, s5_c_im=m_s5_c_im, s5_d=m_s5_d, s5_w_glu=m_s5_w_glu, s5_b_glu=m_s5_b_glu, ret_gn_gain=m_ret_gn_gain, w_out_even=m_w_out_even, norm_odd=m_norm_odd, w_in_odd=m_w_in_odd, sgu_norm_gain=m_sgu_norm_gain, sgu_w_spatial=m_sgu_w_spatial, sgu_b_spatial=m_sgu_b_spatial, w_out_odd=m_w_out_odd, final_norm=m_final_norm)
    v = dict(norm_even=v_norm_even, w_in_even=v_w_in_even, s5_lam_re=v_s5_lam_re, s5_lam_im=v_s5_lam_im, s5_log_dt=v_s5_log_dt, s5_b_re=v_s5_b_re, s5_b_im=v_s5_b_im, s5_c_re=v_s5_c_re, s5_c_im=v_s5_c_im, s5_d=v_s5_d, s5_w_glu=v_s5_w_glu, s5_b_glu=v_s5_b_glu, ret_gn_gain=v_ret_gn_gain, w_out_even=v_w_out_even, norm_odd=v_norm_odd, w_in_odd=v_w_in_odd, sgu_norm_gain=v_sgu_norm_gain, sgu_w_spatial=v_sgu_w_spatial, sgu_b_spatial=v_sgu_b_spatial, w_out_odd=v_w_out_odd, final_norm=v_final_norm)

    grad_x, landed, part, other = local_grads(x[0], loss_target[0], w)

    small = SMALL + ("loss",)
    part["w_in_even"] = sum_slabs(landed["w_in_even"], "sum_w_in_even")
    part.update(sum_small({n: landed[n] for n in small}))
    names = ("w_in_even",) + small
    other.update(zip(names, run_plan(_SiblingPlan([part[n] for n in names]), "sibling_exchange")))

    wt, mt, vt = dict(w), dict(m), dict(v)
    for n in WIDE_ROWS:
        wt[n], mt[n], vt[n] = (jnp.transpose(a[n][0], (2, 0, 1)) for a in (w, m, v))
    out_g, out_d, out_m, out_v = adam_small(SMALL, wt, mt, vt, part, other)
    for n in WIDE_ROWS:
        for out in (out_g, out_d, out_m, out_v):
            out[n] = jnp.transpose(out[n], (1, 2, 0))[None]
    for n in MATRICES:
        res = adam_update(w[n][0], m[n][0], v[n][0], part[n], other[n], "adam_" + n)
        out_g[n], out_d[n], out_m[n], out_v[n] = (r[None] for r in res)
    total_loss = (part["loss"] + other["loss"])[0, 0]

    return (total_loss, grad_x[None], *[out_g[n] for n in WEIGHTS], *[out_d[n] for n in WEIGHTS],
            *[out_m[n] for n in WEIGHTS], *[out_v[n] for n in WEIGHTS])
```

```python
import functools
import math

import numpy as np
import jax
import jax.numpy as jnp
from jax import lax
from jax.experimental import pallas as pl
from jax.experimental.pallas import tpu as pltpu

F32 = jnp.float32
MXU_DTYPE = jnp.bfloat16
NORM_EPS = 1e-6
D_MODEL = 1024
S5_WIDTH = 1024
S5_GROUP = 16
S5_GROUPS = 64
S5_STATE = 64
S5_LANES = S5_GROUPS * S5_STATE
S5_KBLK = 8
RET_HEADS = 4
RET_DK = 256
RET_CHUNK = 128
ROPE_BASE = 10000.0
SGU_WIDTH = 2048
SGU_GROUPS = 4
SGU_GDIM = 512
SGU_CHUNK = 128
EVEN_IN = 6144
ODD_IN = 6144
ADAM_LR = 0.001
ADAM_B1 = 0.9
ADAM_B2 = 0.999
ADAM_EPS = 1e-08
ADAM_WD = 0.01
ADAM_STEP = 10
N_CHIPS = 4
VMEM_LIMIT = 56 * 1024 * 1024

TL_PROJ = 512
TL_DW = 1024
TL_S5 = 256
TL_SGU = 256


def _cparams(sem, **kw):
    return pltpu.CompilerParams(dimension_semantics=sem, vmem_limit_bytes=VMEM_LIMIT, **kw)


def _mm(a, b):
    return jnp.dot(a.astype(MXU_DTYPE), b.astype(MXU_DTYPE), preferred_element_type=F32)


def _mm_nt(a, b):
    return lax.dot_general(a.astype(MXU_DTYPE), b.astype(MXU_DTYPE),
                           (((1,), (1,)), ((), ())), preferred_element_type=F32)


def _mm_tn(a, b):
    return lax.dot_general(a.astype(MXU_DTYPE), b.astype(MXU_DTYPE),
                           (((0,), (0,)), ((), ())), preferred_element_type=F32)


_GELU_C = math.sqrt(2.0 / math.pi)


def _gelu_parts(x):
    x2 = x * x
    th = jnp.tanh(x * (_GELU_C + (_GELU_C * 0.044715) * x2))
    hx = 0.5 * x
    return hx + hx * th, th, x2, hx


def _gelu(x):
    return _gelu_parts(x)[0]


def _gelu_and_grad(x):
    g, th, x2, hx = _gelu_parts(x)
    return g, (0.5 + 0.5 * th) + hx * (1.0 - th * th) * (_GELU_C + (3.0 * _GELU_C * 0.044715) * x2)


def _gelu_grad(x):
    return _gelu_and_grad(x)[1]


def _sigmoid(x):
    return 1.0 / (1.0 + jnp.exp(-x))


def _silu_and_grad(x):
    s = _sigmoid(x)
    return x * s, s * (1.0 + x * (1.0 - s))


def _rms(x):
    return lax.rsqrt(jnp.mean(x * x, axis=-1, keepdims=True) + NORM_EPS)


def _full(shape):
    nd = len(shape)
    return pl.BlockSpec(shape, lambda *_: (0,) * nd)


MESH = pl.DeviceIdType.MESH
ANY = pl.BlockSpec(memory_space=pl.ANY)


def _place():
    return lax.axis_index("x"), lax.axis_index("y"), lax.axis_index("c")


def _chip_peer(x, y, c, d):
    return (1 - x if d >= 2 else x, 1 - y if d % 2 else y, c)


class _Plan:
    def __init__(self, inputs, out_shape, build):
        self.inputs, self.out_shape, self._build = list(inputs), list(out_shape), build
        n = len(self.inputs)
        self.sems = [pltpu.SemaphoreType.DMA((n, 3)), pltpu.SemaphoreType.DMA((n, 3)), pltpu.SemaphoreType.DMA((n,))]

    def start(self, in_refs, out_refs, sems):
        send, recv, local = self._build(in_refs, out_refs, sems)
        for p in range(len(self.inputs)):
            local[p].start()
            for cp in send[p]:
                cp.start()

    def wait(self, in_refs, out_refs, sems):
        send, recv, local = self._build(in_refs, out_refs, sems)
        for p in range(len(self.inputs)):
            for cp in recv[p]:
                cp.wait_recv()
        for p in range(len(self.inputs)):
            for cp in send[p]:
                cp.wait_send()
            local[p].wait()


class _GatherPlan:
    def __init__(self, shards, only=None):
        n = len(shards)
        self.n, self.only = n, only
        self.peers = (1, 2, 3) if only is None else (only,)
        self.inputs = list(shards)
        slabs = N_CHIPS if only is None else 1
        self.out_shape = [jax.ShapeDtypeStruct((slabs,) + s.shape, s.dtype) for s in shards]
        self.halved = [s.shape[0] % 32 == 0 for s in shards]
        self.sems = [pltpu.SemaphoreType.DMA((n, 3)) for _ in range(4)] + [pltpu.SemaphoreType.DMA((n,))]

    def _copies(self, in_refs, out_refs, sems):
        ici_s, ici_r, d2d_s, d2d_r, loc = sems
        x, y, c = _place()
        me = 2 * x + y

        def rows(p, core):
            if not self.halved[p]:
                return slice(None)
            half = self.inputs[p].shape[0] // 2
            return pl.ds(pl.multiple_of(core * half, 16), half)

        def slab(chip):
            return chip if self.only is None else 0

        def ici(p, d, chip, core):
            return pltpu.make_async_remote_copy(
                src_ref=in_refs[p].at[rows(p, core)], dst_ref=out_refs[p].at[slab(chip), rows(p, core)],
                send_sem=ici_s.at[p, d - 1], recv_sem=ici_r.at[p, d - 1],
                device_id=_chip_peer(x, y, c, d), device_id_type=MESH)

        def d2d(p, d, core):
            part = out_refs[p].at[slab(me ^ d), rows(p, core)]
            return pltpu.make_async_remote_copy(
                src_ref=part, dst_ref=part, send_sem=d2d_s.at[p, d - 1], recv_sem=d2d_r.at[p, d - 1],
                device_id=(x, y, 1 - c), device_id_type=MESH)

        local = [pltpu.make_async_copy(in_refs[p], out_refs[p].at[slab(me)], loc.at[p]) for p in range(self.n)]
        return me, c, ici, d2d, local

    def start(self, in_refs, out_refs, sems):
        me, c, ici, d2d, local = self._copies(in_refs, out_refs, sems)
        for p in range(self.n):
            if self.only is None:
                local[p].start()
            for d in self.peers:
                ici(p, d, me, c).start()

    def wait(self, in_refs, out_refs, sems):
        me, c, ici, d2d, local = self._copies(in_refs, out_refs, sems)
        for p in range(self.n):
            for d in self.peers:
                ici(p, d, me ^ d, c).wait_recv()
                if self.halved[p]:
                    d2d(p, d, c).start()
        for p in range(self.n):
            for d in self.peers:
                if self.halved[p]:
                    d2d(p, d, 1 - c).wait_recv()
                    d2d(p, d, c).wait_send()
                ici(p, d, me, c).wait_send()
            if self.only is None:
                local[p].wait()


def gather_plan(shards, only=None):
    return _GatherPlan(shards, only)


def reduce_plan(shards, whole=()):
    n_s = len(shards)

    def build(in_refs, out_refs, sems):
        send_sems, recv_sems, loc_sems = sems
        x, y, c = _place()
        me = 2 * x + y

        def src(p, slab):
            return in_refs[p].at[slab] if p < n_s else in_refs[p]

        def remote(p, d):
            return pltpu.make_async_remote_copy(
                src_ref=src(p, me ^ d), dst_ref=out_refs[p].at[d], send_sem=send_sems.at[p, d - 1],
                recv_sem=recv_sems.at[p, d - 1], device_id=_chip_peer(x, y, c, d), device_id_type=MESH)

        n = len(in_refs)
        send = [[remote(p, d) for d in (1, 2, 3)] for p in range(n)]
        local = [pltpu.make_async_copy(src(p, me), out_refs[p].at[0], loc_sems.at[p]) for p in range(n)]
        return send, send, local

    outs = [jax.ShapeDtypeStruct(s.shape, s.dtype) for s in shards]
    outs += [jax.ShapeDtypeStruct((N_CHIPS,) + a.shape, a.dtype) for a in whole]
    return _Plan(list(shards) + list(whole), outs, build)


class _SiblingPlan:
    def __init__(self, arrs):
        self.inputs = list(arrs)
        self.out_shape = [jax.ShapeDtypeStruct(a.shape, a.dtype) for a in arrs]
        n = len(arrs)
        self.sems = [pltpu.SemaphoreType.DMA((n,)), pltpu.SemaphoreType.DMA((n,))]

    def _copies(self, in_refs, out_refs, sems):
        x, y, c = _place()
        return [pltpu.make_async_remote_copy(
            src_ref=in_refs[p], dst_ref=out_refs[p], send_sem=sems[0].at[p], recv_sem=sems[1].at[p],
            device_id=(x, y, 1 - c), device_id_type=MESH) for p in range(len(self.inputs))]

    def start(self, in_refs, out_refs, sems):
        for cp in self._copies(in_refs, out_refs, sems):
            cp.start()

    def wait(self, in_refs, out_refs, sems):
        copies = self._copies(in_refs, out_refs, sems)
        for cp in copies:
            cp.wait_recv()
        for cp in copies:
            cp.wait_send()


def run_plan(plan, name):
    n = len(plan.inputs)

    def body(*refs):
        plan.start(refs[:n], refs[n:2 * n], refs[2 * n:])
        plan.wait(refs[:n], refs[n:2 * n], refs[2 * n:])

    return pl.pallas_call(body, name=name, in_specs=[ANY] * n, out_specs=[ANY] * n, out_shape=plan.out_shape,
                          scratch_shapes=plan.sems)(*plan.inputs)


def _call(body, plan, *, name, grid, in_specs, out_specs, out_shape, sem, scratch_shapes=(), aliases=None,
          n_prefetch=0):
    aliases = {} if aliases is None else aliases
    single = not isinstance(out_shape, (list, tuple))
    out_specs = [out_specs] if single else list(out_specs)
    out_shape = [out_shape] if single else list(out_shape)
    n_in, n_out, n_scr = len(in_specs), len(out_specs), len(scratch_shapes)
    ci = 0 if plan is None else len(plan.inputs)
    co = 0 if plan is None else len(plan.out_shape)

    def hosted(*refs):
        pre, refs = refs[:n_prefetch], refs[n_prefetch:]
        ins, cins = refs[:n_in], refs[n_in:n_in + ci]
        k = n_in + ci
        outs, couts = refs[k:k + n_out], refs[k + n_out:k + n_out + co]
        k += n_out + co
        scr, sems = refs[k:k + n_scr], refs[k + n_scr:]
        ids = [pl.program_id(a) for a in range(len(grid))]
        first = functools.reduce(jnp.logical_and, [i == 0 for i in ids])
        last = functools.reduce(jnp.logical_and, [i == g - 1 for i, g in zip(ids, grid)])

        @pl.when(first)
        def _():
            plan.start(cins, couts, sems)

        body(*pre, *ins, *outs, *scr)

        @pl.when(last)
        def _():
            plan.wait(cins, couts, sems)

    def run(*args):
        hosting = plan is not None
        spec = pltpu.PrefetchScalarGridSpec(
            num_scalar_prefetch=n_prefetch, grid=grid,
            in_specs=list(in_specs) + ([ANY] * ci if hosting else []),
            out_specs=out_specs + ([ANY] * co if hosting else []),
            scratch_shapes=list(scratch_shapes) + (plan.sems if hosting else []))
        res = pl.pallas_call(hosted if hosting else body, name=name, grid_spec=spec,
                             out_shape=out_shape + (plan.out_shape if hosting else []),
                             input_output_aliases=aliases, compiler_params=_cparams(sem),
                             )(*args, *(plan.inputs if hosting else []))
        return (res[0] if single else res[:n_out]), list(res[n_out:])

    return run


def norm_matmul(x, g, w, name, plan=None, tn=None):
    L, D = x.shape
    tl = min(TL_DW, L)
    if w.ndim == 3:
        nt, _, tn = w.shape
        w_spec = pl.BlockSpec((1, D, tn), lambda i, n: (n, 0, 0))
    else:
        nt = w.shape[1] // tn
        w_spec = pl.BlockSpec((D, tn), lambda i, n: (0, n))

    def body(x_ref, g_ref, w_ref, o_ref, h_ref):
        xv = x_ref[...]
        h = (xv * _rms(xv) * g_ref[...]).astype(h_ref.dtype)
        h_ref[...] = h
        o_ref[...] = _mm(h, w_ref[0] if w.ndim == 3 else w_ref[...])

    return _call(
        body, plan, name=name, grid=(L // tl, nt),
        in_specs=[pl.BlockSpec((tl, D), lambda i, n: (i, 0)), _full((1, D)), w_spec],
        out_specs=[pl.BlockSpec((tl, tn), lambda i, n: (i, n)), pl.BlockSpec((tl, D), lambda i, n: (i, 0))],
        out_shape=[jax.ShapeDtypeStruct((L, nt * tn), F32), jax.ShapeDtypeStruct((L, D), MXU_DTYPE)],
        sem=("arbitrary", "arbitrary"),
    )(x, g, w)


def even_in_slabs(x, xs, g, w, slabs, wsel, name, p_in=None, plan=None):
    L, D = x.shape
    tl = min(TL_PROJ, L)
    wb = EVEN_IN // N_CHIPS
    n = slabs.shape[0]
    s5_cols = 2 * S5_WIDTH - wb
    first = p_in is None

    def body(slabs_ref, wsel_ref, xs_ref, x_ref, g_ref, w_ref, *rest):
        o_ref = rest[-3] if first else rest[-1]
        j = slabs_ref[pl.program_id(0)]
        hs = (xs_ref[...] * _rms(xs_ref[...]) * g_ref[...]).astype(MXU_DTYPE)
        h = (x_ref[...] * _rms(x_ref[...]) * g_ref[...]).astype(MXU_DTYPE)
        if first:
            rest[-2][...] = hs
            rest[-1][...] = h
        o_ref[:, :s5_cols] = _mm(jnp.where(j <= 1, hs, h), w_ref[0, :, :s5_cols])
        o_ref[:, s5_cols:] = _mm(jnp.where(j == 0, hs, h), w_ref[0, :, s5_cols:])

    row = pl.BlockSpec((tl, D), lambda s, i, slabs_ref, wsel_ref: (i, 0))
    in_specs = [row if first else
                pl.BlockSpec((tl, D), lambda s, i, slabs_ref, wsel_ref: (jnp.where(slabs_ref[s] <= 1, i, 0), 0)),
                row if first else
                pl.BlockSpec((tl, D), lambda s, i, slabs_ref, wsel_ref: (jnp.where(slabs_ref[s] >= 1, i, 0), 0)),
                pl.BlockSpec((1, D), lambda s, i, slabs_ref, wsel_ref: (0, 0)),
                pl.BlockSpec((1, D, wb), lambda s, i, slabs_ref, wsel_ref: (wsel_ref[s], 0, 0))]
    out_specs = [pl.BlockSpec((tl, wb), lambda s, i, slabs_ref, wsel_ref: (i, slabs_ref[s]))]
    out_shape = [jax.ShapeDtypeStruct((L, EVEN_IN), F32)]
    args = [slabs, wsel, xs, x, g, w]
    if first:
        out_specs += [row, row]
        out_shape += [jax.ShapeDtypeStruct((L, D), MXU_DTYPE)] * 2
    else:
        in_specs.append(ANY)
        args.append(p_in)
    return _call(body, plan, name=name, grid=(n, L // tl), in_specs=in_specs, out_specs=out_specs,
                 out_shape=out_shape, sem=("arbitrary", "arbitrary"), n_prefetch=2,
                 aliases={} if first else {6: 0})(*args)


def matmul_residual(ys, w, x, name):
    L, D = x.shape
    tl = min(TL_PROJ, L)
    n = len(ys)
    offs = np.cumsum([0] + [y.shape[1] for y in ys])

    def body(*refs):
        y_refs, w_ref, x_ref, o_ref = refs[:n], refs[n], refs[n + 1], refs[n + 2]
        acc = x_ref[...]
        for k in range(n):
            acc = acc + _mm(y_refs[k][...], w_ref[offs[k]:offs[k + 1], :])
        o_ref[...] = acc

    return pl.pallas_call(
        body, name=name, grid=(L // tl,),
        in_specs=[pl.BlockSpec((tl, y.shape[1]), lambda i: (i, 0)) for y in ys]
        + [_full(w.shape), pl.BlockSpec((tl, D), lambda i: (i, 0))],
        out_specs=pl.BlockSpec((tl, D), lambda i: (i, 0)),
        out_shape=jax.ShapeDtypeStruct((L, D), F32),
        compiler_params=_cparams(("arbitrary",)),
    )(*ys, w, x)


def out_proj_loss(y, w, x, gf, tgt, name):
    L, K = y.shape
    D = w.shape[1]
    tl = min(TL_PROJ, L)

    def body(y_ref, w_ref, x_ref, gf_ref, t_ref, dx_ref, loss_ref, dg_ref):
        @pl.when(pl.program_id(0) == 0)
        def _():
            loss_ref[...] = jnp.zeros_like(loss_ref)
            dg_ref[...] = jnp.zeros_like(dg_ref)

        x2 = x_ref[...] + _mm(y_ref[...], w_ref[...])
        r = _rms(x2)
        xn = x2 * r
        e = xn * gf_ref[...] - t_ref[...]
        loss_ref[...] += (0.5 / D) * jnp.sum(e * e)
        dout = e * (1.0 / D)
        dg_ref[...] += jnp.sum(dout * xn, axis=0, keepdims=True)
        dxn = dout * gf_ref[...]
        dx_ref[...] = r * (dxn - xn * jnp.mean(dxn * xn, axis=-1, keepdims=True))

    return pl.pallas_call(
        body, name=name, grid=(L // tl,),
        in_specs=[pl.BlockSpec((tl, K), lambda i: (i, 0)), _full((K, D)),
                  pl.BlockSpec((tl, D), lambda i: (i, 0)), _full((1, D)),
                  pl.BlockSpec((tl, D), lambda i: (i, 0))],
        out_specs=[pl.BlockSpec((tl, D), lambda i: (i, 0)), _full((8, 128)), _full((1, D))],
        out_shape=[jax.ShapeDtypeStruct((L, D), F32), jax.ShapeDtypeStruct((8, 128), F32),
                   jax.ShapeDtypeStruct((1, D), F32)],
        compiler_params=_cparams(("arbitrary",)),
    )(y, w, x, gf, tgt)


def out_proj_bwd(dx, w, ys, name):
    L, D = dx.shape
    K = w.shape[0]
    tl = min(TL_PROJ, L)
    n = len(ys)
    offs = np.cumsum([0] + [y.shape[1] for y in ys])

    def body(*refs):
        dx_ref, w_ref, y_refs = refs[0], refs[1], refs[2:2 + n]
        dy_refs, dw_ref = refs[2 + n:2 + 2 * n], refs[2 + 2 * n]

        @pl.when(pl.program_id(0) == 0)
        def _():
            dw_ref[...] = jnp.zeros_like(dw_ref)

        dxv = dx_ref[...]
        for k in range(n):
            dy_refs[k][...] = _mm_nt(dxv, w_ref[offs[k]:offs[k + 1], :])
            dw_ref[offs[k]:offs[k + 1], :] += _mm_tn(y_refs[k][...], dxv)

    y_specs = [pl.BlockSpec((tl, y.shape[1]), lambda i: (i, 0)) for y in ys]
    return pl.pallas_call(
        body, name=name, grid=(L // tl,),
        in_specs=[pl.BlockSpec((tl, D), lambda i: (i, 0)), _full((K, D))] + y_specs,
        out_specs=y_specs + [_full((K, D))],
        out_shape=[jax.ShapeDtypeStruct(y.shape, F32) for y in ys] + [jax.ShapeDtypeStruct((K, D), F32)],
        compiler_params=_cparams(("arbitrary",)),
    )(dx, w, *ys)


def in_proj_bwd_dx(x, g, dps, ws, dres, name, plan=None):
    L, D = x.shape
    tl = min(TL_PROJ, L)
    n = len(dps)

    def body(*refs):
        x_ref, g_ref, dres_ref = refs[:3]
        dp_refs, w_refs = refs[3:3 + n], refs[3 + n:3 + 2 * n]
        dx_ref, dg_ref = refs[3 + 2 * n:]

        @pl.when(pl.program_id(0) == 0)
        def _():
            dg_ref[...] = jnp.zeros_like(dg_ref)

        dh = None
        for dp_ref, w_ref, w in zip(dp_refs, w_refs, ws):
            if w.ndim == 3:
                tn = w.shape[2]
                parts = [_mm_nt(dp_ref[:, tn * k:tn * (k + 1)], w_ref[k]) for k in range(w.shape[0])]
            else:
                parts = [_mm_nt(dp_ref[...], w_ref[...])]
            for part in parts:
                dh = part if dh is None else dh + part
        xv = x_ref[...]
        r = _rms(xv)
        xn = xv * r
        dg_ref[...] += jnp.sum(dh * xn, axis=0, keepdims=True)
        dxn = dh * g_ref[...]
        dx_ref[...] = dres_ref[...] + r * (dxn - xn * jnp.mean(dxn * xn, axis=-1, keepdims=True))

    return _call(
        body, plan, name=name, grid=(L // tl,),
        in_specs=[pl.BlockSpec((tl, D), lambda i: (i, 0)), _full((1, D)), pl.BlockSpec((tl, D), lambda i: (i, 0))]
        + [pl.BlockSpec((tl, dp.shape[1]), lambda i: (i, 0)) for dp in dps] + [_full(w.shape) for w in ws],
        out_specs=[pl.BlockSpec((tl, D), lambda i: (i, 0)), _full((1, D))],
        out_shape=[jax.ShapeDtypeStruct((L, D), F32), jax.ShapeDtypeStruct((1, D), F32)],
        sem=("arbitrary",),
    )(x, g, dres, *dps, *ws)


def in_proj_bwd_dw(h, dp, name, tn, first=0, into=None, dtype=F32, plan=None):
    L, D = h.shape
    tl = min(TL_DW, L)
    wb = EVEN_IN // N_CHIPS
    per = wb // tn
    count = dp.shape[1] // tn
    last = L // tl - 1

    def body(*refs):
        h_ref, dp_ref, dw_ref, acc = refs[0], refs[1], refs[-2], refs[-1]

        @pl.when(pl.program_id(1) == 0)
        def _():
            acc[...] = jnp.zeros_like(acc)

        acc[...] += _mm_tn(h_ref[...], dp_ref[...])

        @pl.when(pl.program_id(1) == last)
        def _():
            dw_ref[0] = acc[...].astype(dw_ref.dtype)

    ins = [h, dp] + ([] if into is None else [into])
    return _call(
        body, plan, name=name, grid=(count, L // tl),
        in_specs=[pl.BlockSpec((tl, D), lambda n, i: (i, 0)), pl.BlockSpec((tl, tn), lambda n, i: (i, n))]
        + ([] if into is None else [ANY]),
        out_specs=pl.BlockSpec((1, D, tn), lambda n, i: ((n + first) // per, 0, (n + first) % per)),
        out_shape=jax.ShapeDtypeStruct((N_CHIPS, D, wb), dtype),
        scratch_shapes=[pltpu.VMEM((D, tn), F32)],
        aliases={} if into is None else {2: 0},
        sem=("arbitrary", "arbitrary"),
    )(*ins)


def _s5_param_fn(lam_re, lam_im, log_dt, b_re, b_im):
    lr = jnp.minimum(lam_re, -1e-4)
    li = lam_im
    dt = jnp.exp(log_dt)
    mag = jnp.exp(lr * dt)
    ab_re = mag * jnp.cos(li * dt)
    ab_im = mag * jnp.sin(li * dt)
    den = lr * lr + li * li
    n_re = ab_re - 1.0
    n_im = ab_im
    z_re = (n_re * lr + n_im * li) / den
    z_im = (n_im * lr - n_re * li) / den
    bb_re = z_re[None] * b_re - z_im[None] * b_im
    bb_im = z_re[None] * b_im + z_im[None] * b_re
    return ab_re, ab_im, bb_re, bb_im


def s5_params_fwd(lam_re, lam_im, log_dt, b_re, b_im, span):
    G, P = lam_re.shape
    H = b_re.shape[0]
    assert span & (span - 1) == 0

    def body(lr_ref, li_ref, dt_ref, br_ref, bi_ref, abr_ref, abi_ref, bbr_ref, bbi_ref, pr_ref, pi_ref):
        ab_re, ab_im, bb_re, bb_im = _s5_param_fn(lr_ref[...], li_ref[...], dt_ref[...], br_ref[...], bi_ref[...])
        abr_ref[...] = ab_re
        abi_ref[...] = ab_im
        bbr_ref[...] = bb_re
        bbi_ref[...] = bb_im
        cr, ci = ab_re, ab_im
        for _ in range(span.bit_length() - 1):
            cr, ci = cr * cr - ci * ci, 2.0 * cr * ci
        pr_ref[...] = cr
        pi_ref[...] = ci

    shp = lambda *s: jax.ShapeDtypeStruct(s, F32)
    return pl.pallas_call(
        body, name="s5_params_fwd",
        out_shape=[shp(G, P), shp(G, P), shp(H, G, P), shp(H, G, P), shp(G, P), shp(G, P)],
    )(lam_re, lam_im, log_dt, b_re, b_im)


def s5_params_bwd(lam_re, lam_im, log_dt, b_re, b_im, d_ab_re, d_ab_im, d_bb_re, d_bb_im):
    G, P = lam_re.shape
    H = b_re.shape[0]

    def body(lr_ref, li_ref, dt_ref, br_ref, bi_ref, g0, g1, g2, g3, o0, o1, o2, o3, o4):
        prim = (lr_ref[...], li_ref[...], dt_ref[...], br_ref[...], bi_ref[...])
        _, vjp = jax.vjp(_s5_param_fn, *prim)
        d = vjp((jnp.sum(g0[...], axis=0), jnp.sum(g1[...], axis=0), g2[...], g3[...]))
        o0[...], o1[...], o2[...], o3[...], o4[...] = d

    shp = lambda *s: jax.ShapeDtypeStruct(s, F32)
    return pl.pallas_call(
        body, name="s5_params_bwd",
        out_shape=[shp(G, P), shp(G, P), shp(G, 1), shp(H, G, P), shp(H, G, P)],
    )(lam_re, lam_im, log_dt, b_re, b_im, d_ab_re, d_ab_im, d_bb_re, d_bb_im)


def stream_order(a, tl):
    L, C = a.shape
    return a.reshape(L // tl, 8, tl // 8, C).transpose(0, 2, 1, 3).reshape(L, C)


def token_order(a, tl):
    L, C = a.shape
    return a.reshape(L // tl, tl // 8, 8, C).transpose(0, 2, 1, 3).reshape(L, C)


_LANE_BLK = 1024
_LANE_BLK_BWD = 1024


def _cmul_add(ar, ai, xr, xi, br, bi):
    return br + (ar * xr - ai * xi), bi + (ar * xi + ai * xr)


def _cmulc_add(ar, ai, xr, xi, br, bi):
    return br + (ar * xr + ai * xi), bi + (ar * xi - ai * xr)


def _s5_states(u, wbd_ref, a_re, a_im, at_re, at_im, s_re, s_im, e_re, e_im, c0_re, c0_im, tl):
    t8 = tl // 8
    for k in range(S5_KBLK):
        bu = _mm(u[:, 128 * k:128 * (k + 1)], wbd_ref[k])
        s_re[:, 512 * k:512 * (k + 1)] = bu[:, :512]
        s_im[:, 512 * k:512 * (k + 1)] = bu[:, 512:]
    outs_re, outs_im = [], []
    for b in range(S5_LANES // _LANE_BLK):
        lanes = slice(_LANE_BLK * b, _LANE_BLK * (b + 1))
        ar = jnp.broadcast_to(a_re[:, lanes], (8, _LANE_BLK))
        ai = jnp.broadcast_to(a_im[:, lanes], (8, _LANE_BLK))

        def local(i, carry, lanes=lanes, ar=ar, ai=ai):
            r = pl.multiple_of(i * 8, 8)
            sr, si = _cmul_add(ar, ai, carry[0], carry[1], s_re[pl.ds(r, 8), lanes], s_im[pl.ds(r, 8), lanes])
            s_re[pl.ds(r, 8), lanes] = sr
            s_im[pl.ds(r, 8), lanes] = si
            return sr, si

        zero = jnp.zeros((8, _LANE_BLK), F32)
        fr, fi = lax.fori_loop(0, t8, local, (zero, zero), unroll=True)
        tr, ti = at_re[:, lanes], at_im[:, lanes]
        er, ei = c0_re[:, lanes], c0_im[:, lanes]
        ers, eis = [er], [ei]
        for j in range(8):
            er, ei = _cmul_add(tr, ti, er, ei, fr[j:j + 1], fi[j:j + 1])
            ers.append(er)
            eis.append(ei)
        outs_re.append(ers[8])
        outs_im.append(eis[8])
        ent_r, ent_i = jnp.concatenate(ers[:8], axis=0), jnp.concatenate(eis[:8], axis=0)
        e_re[:, lanes] = ent_r
        e_im[:, lanes] = ent_i

        def fix(i, carry, lanes=lanes, ar=ar, ai=ai):
            r = pl.multiple_of(i * 8, 8)
            zr, zi = ar * carry[0] - ai * carry[1], ar * carry[1] + ai * carry[0]
            s_re[pl.ds(r, 8), lanes] = s_re[pl.ds(r, 8), lanes] + zr
            s_im[pl.ds(r, 8), lanes] = s_im[pl.ds(r, 8), lanes] + zi
            return zr, zi

        lax.fori_loop(0, t8, fix, (ent_r, ent_i), unroll=True)
    return jnp.concatenate(outs_re, axis=1), jnp.concatenate(outs_im, axis=1)


def _s5_readout(s_re, s_im, cre_ref, cim_ref):
    ys = []
    for k in range(S5_KBLK):
        lanes = slice(512 * k, 512 * (k + 1))
        ys.append(_mm(s_re[:, lanes], cre_ref[k]) - _mm(s_im[:, lanes], cim_ref[k]))
    return jnp.concatenate(ys, axis=1)


def s5_forward(p, wbd, cre, cim, atab, d_skip, w_glu, b_glu, plan=None):
    L = p.shape[0]
    tl = min(TL_S5, L)
    nch = L // tl

    def body(u_ref, z_ref, wbd_ref, cre_ref, cim_ref, at_ref, d_ref, wg_ref, bg_ref,
             ya_ref, st_re_ref, st_im_ref, sv_re_ref, sv_im_ref, s_re, s_im, e_re, e_im, car_re, car_im):
        @pl.when(pl.program_id(0) == 0)
        def _():
            car_re[...] = jnp.zeros_like(car_re)
            car_im[...] = jnp.zeros_like(car_im)

        c0_re, c0_im = car_re[...], car_im[...]
        st_re_ref[0] = c0_re
        st_im_ref[0] = c0_im
        u = u_ref[...]
        x_re, x_im = _s5_states(u, wbd_ref, at_ref[0:1], at_ref[1:2], at_ref[2:3], at_ref[3:4],
                                s_re, s_im, e_re, e_im, c0_re, c0_im, tl)
        car_re[...] = x_re
        car_im[...] = x_im
        sv_re_ref[...] = s_re[...].astype(sv_re_ref.dtype)
        sv_im_ref[...] = s_im[...].astype(sv_im_ref.dtype)
        y = _s5_readout(sv_re_ref, sv_im_ref, cre_ref, cim_ref) + d_ref[...] * u
        yg = _gelu(y)
        gate = _sigmoid(_mm(yg, wg_ref[...]) + bg_ref[...])
        sz, _ = _silu_and_grad(z_ref[...])
        ya_ref[...] = (yg * gate * sz).astype(ya_ref.dtype)

    return _call(
        body, plan, name="s5_forward", grid=(nch,),
        in_specs=[pl.BlockSpec((tl, 1024), lambda i: (i, 0)), pl.BlockSpec((tl, 1024), lambda i: (i, 1)),
                  _full(wbd.shape), _full(cre.shape), _full(cim.shape), _full(atab.shape),
                  _full((1, 1024)), _full((1024, 1024)), _full((1, 1024))],
        out_specs=[pl.BlockSpec((tl, 1024), lambda i: (i, 0)),
                   pl.BlockSpec((1, 1, S5_LANES), lambda i: (i, 0, 0)),
                   pl.BlockSpec((1, 1, S5_LANES), lambda i: (i, 0, 0)),
                   pl.BlockSpec((tl, S5_LANES), lambda i: (i, 0)), pl.BlockSpec((tl, S5_LANES), lambda i: (i, 0))],
        out_shape=[jax.ShapeDtypeStruct((L, 1024), MXU_DTYPE),
                   jax.ShapeDtypeStruct((nch, 1, S5_LANES), F32), jax.ShapeDtypeStruct((nch, 1, S5_LANES), F32),
                   jax.ShapeDtypeStruct((L, S5_LANES), MXU_DTYPE), jax.ShapeDtypeStruct((L, S5_LANES), MXU_DTYPE)],
        scratch_shapes=[pltpu.VMEM((tl, S5_LANES), F32), pltpu.VMEM((tl, S5_LANES), F32),
                        pltpu.VMEM((8, S5_LANES), F32), pltpu.VMEM((8, S5_LANES), F32),
                        pltpu.VMEM((1, S5_LANES), F32), pltpu.VMEM((1, S5_LANES), F32)],
        sem=("arbitrary",),
    )(p, p, wbd, cre, cim, atab, d_skip, w_glu, b_glu)


def s5_backward(p, dya, st_re, st_im, sv_re, sv_im, wbd, cre, cim, atab, d_skip, w_glu, b_glu, plan=None):
    L = p.shape[0]
    tl = min(TL_S5, L)
    t8 = tl // 8
    nch = L // tl
    rev = lambda i: (nch - 1 - i, 0)
    rev1 = lambda i: (nch - 1 - i, 1)
    rev3 = lambda i: (nch - 1 - i, 0, 0)
    ct_shape = (S5_KBLK, cre.shape[2], cre.shape[1])

    def body(u_ref, z_ref, dya_ref, str_ref, sti_ref, s_re, s_im, wbd_ref, cre_ref, cim_ref, at_ref,
             d_ref, wg_ref, bg_ref,
             dp_ref, dwbd_ref, dcre_ref, dcim_ref, dabr_ref, dabi_ref, dd_ref, dwg_ref, dbg_ref,
             g_re, g_im, car_re, car_im):
        @pl.when(pl.program_id(0) == 0)
        def _():
            car_re[...] = jnp.zeros_like(car_re)
            car_im[...] = jnp.zeros_like(car_im)
            for r in (dwbd_ref, dcre_ref, dcim_ref, dabr_ref, dabi_ref, dd_ref, dwg_ref, dbg_ref):
                r[...] = jnp.zeros_like(r)

        u = u_ref[...]
        a_re, a_im, at_re, at_im = at_ref[0:1], at_ref[1:2], at_ref[2:3], at_ref[3:4]
        y = _s5_readout(s_re, s_im, cre_ref, cim_ref) + d_ref[...] * u
        yg, dyg = _gelu_and_grad(y)
        gate = _sigmoid(_mm(yg, wg_ref[...]) + bg_ref[...])
        sz, dsz = _silu_and_grad(z_ref[...])
        dya = dya_ref[...]
        s5out = yg * gate
        dp_ref[:, 1024:] = (dya * s5out * dsz).astype(dp_ref.dtype)
        ds5 = dya * sz
        dt = ds5 * yg * gate * (1.0 - gate)
        dwg_ref[...] += _mm_tn(yg, dt)
        dbg_ref[...] += jnp.sum(dt, axis=0, keepdims=True)
        dyv = (ds5 * gate + _mm_nt(dt, wg_ref[...])) * dyg
        dd_ref[...] += jnp.sum(dyv * u, axis=0, keepdims=True)

        for k in range(S5_KBLK):
            lanes = slice(512 * k, 512 * (k + 1))
            dyk = dyv[:, 128 * k:128 * (k + 1)]
            g_re[:, lanes] = _mm_nt(dyk, cre_ref[k])
            g_im[:, lanes] = -_mm_nt(dyk, cim_ref[k])
            dcre_ref[k] += _mm_tn(dyk, s_re[:, lanes])
            dcim_ref[k] -= _mm_tn(dyk, s_im[:, lanes])

        blk = _LANE_BLK_BWD
        for b in range(S5_LANES // blk):
            lanes = slice(blk * b, blk * (b + 1))
            ar = jnp.broadcast_to(a_re[:, lanes], (8, blk))
            ai = jnp.broadcast_to(a_im[:, lanes], (8, blk))

            def local(j, carry, lanes=lanes, ar=ar, ai=ai):
                r = pl.multiple_of((t8 - 1 - j) * 8, 8)
                gr, gi = _cmulc_add(ar, ai, carry[0], carry[1], g_re[pl.ds(r, 8), lanes], g_im[pl.ds(r, 8), lanes])
                g_re[pl.ds(r, 8), lanes] = gr
                g_im[pl.ds(r, 8), lanes] = gi
                return gr, gi

            zero = jnp.zeros((8, blk), F32)
            fr, fi = lax.fori_loop(0, t8, local, (zero, zero), unroll=True)
            tr, ti = at_re[:, lanes], at_im[:, lanes]
            hr, hi = car_re[:, lanes], car_im[:, lanes]
            hrs, his = [hr], [hi]
            for j in range(7, -1, -1):
                hr, hi = _cmulc_add(tr, ti, hr, hi, fr[j:j + 1], fi[j:j + 1])
                hrs.append(hr)
                his.append(hi)
            car_re[:, lanes] = hrs[8]
            car_im[:, lanes] = his[8]
            in_r = jnp.concatenate(hrs[7::-1], axis=0)
            in_i = jnp.concatenate(his[7::-1], axis=0)

            wr, wi, nr, ni, accr, acci = in_r, in_i, zero, zero, zero, zero
            for pair in range(t8 // 2 - 1, -1, -1):
                rows = slice(16 * pair, 16 * pair + 16)
                s16r, s16i = s_re[rows, lanes].astype(F32), s_im[rows, lanes].astype(F32)
                for half in (1, 0):
                    r = 16 * pair + 8 * half
                    sr, si = s16r[8 * half:8 * half + 8], s16i[8 * half:8 * half + 8]
                    accr, acci = accr + (sr * nr + si * ni), acci + (sr * ni - si * nr)
                    wr, wi = ar * wr + ai * wi, ar * wi - ai * wr
                    nr, ni = g_re[r:r + 8, lanes] + wr, g_im[r:r + 8, lanes] + wi
                    g_re[r:r + 8, lanes] = nr
                    g_im[r:r + 8, lanes] = ni
            lr, li = s_re[tl - 16:tl, lanes].astype(F32)[8:], s_im[tl - 16:tl, lanes].astype(F32)[8:]
            row0 = lax.broadcasted_iota(jnp.int32, (8, blk), 0) == 0
            sr = jnp.where(row0, jnp.broadcast_to(str_ref[0][:, lanes], (8, blk)), pltpu.roll(lr, 1, 0))
            si = jnp.where(row0, jnp.broadcast_to(sti_ref[0][:, lanes], (8, blk)), pltpu.roll(li, 1, 0))
            dabr_ref[:, lanes] += accr + (sr * nr + si * ni)
            dabi_ref[:, lanes] += acci + (sr * ni - si * nr)

        dus = []
        for k in range(S5_KBLK):
            lanes = slice(512 * k, 512 * (k + 1))
            g = jnp.concatenate([g_re[:, lanes], g_im[:, lanes]], axis=1)
            dwbd_ref[k] += _mm_tn(u[:, 128 * k:128 * (k + 1)], g)
            dus.append(_mm_nt(g, wbd_ref[k]))
        du = jnp.concatenate(dus, axis=1) + dyv * d_ref[...]
        dp_ref[:, :1024] = du.astype(dp_ref.dtype)

    shp = lambda *s: jax.ShapeDtypeStruct(s, F32)
    return _call(
        body, plan, name="s5_backward", grid=(nch,),
        in_specs=[pl.BlockSpec((tl, 1024), rev), pl.BlockSpec((tl, 1024), rev1), pl.BlockSpec((tl, 1024), rev),
                  pl.BlockSpec((1, 1, S5_LANES), rev3), pl.BlockSpec((1, 1, S5_LANES), rev3),
                  pl.BlockSpec((tl, S5_LANES), rev), pl.BlockSpec((tl, S5_LANES), rev),
                  _full(wbd.shape), _full(cre.shape), _full(cim.shape), _full(atab.shape),
                  _full((1, 1024)), _full((1024, 1024)), _full((1, 1024))],
        out_specs=[pl.BlockSpec((tl, 2048), rev), _full(wbd.shape), _full(ct_shape), _full(ct_shape),
                   _full((8, S5_LANES)), _full((8, S5_LANES)), _full((1, 1024)), _full((1024, 1024)), _full((1, 1024))],
        out_shape=[jax.ShapeDtypeStruct((L, 2048), MXU_DTYPE), shp(*wbd.shape), shp(*ct_shape), shp(*ct_shape),
                   shp(8, S5_LANES), shp(8, S5_LANES), shp(1, 1024), shp(1024, 1024), shp(1, 1024)],
        scratch_shapes=[pltpu.VMEM((tl, S5_LANES), F32), pltpu.VMEM((tl, S5_LANES), F32),
                        pltpu.VMEM((1, S5_LANES), F32), pltpu.VMEM((1, S5_LANES), F32)],
        sem=("arbitrary",),
    )(p, p, dya, st_re, st_im, sv_re, sv_im, wbd, cre, cim, atab, d_skip, w_glu, b_glu)


def _block_diag(w, rows_first):
    g8 = w.reshape(S5_KBLK, 8, w.shape[1], w.shape[2])
    eye = jnp.eye(8, dtype=w.dtype)
    out = jnp.einsum('kgab,fg->kfagb', g8, eye)
    return out.reshape(S5_KBLK, 8 * w.shape[1], 8 * w.shape[2])


def _block_diag_extract(wbd, a, b):
    w5 = wbd.reshape(S5_KBLK, 8, a, 8, b)
    idx = jnp.arange(8)
    return w5[:, idx, :, idx, :].transpose(1, 0, 2, 3).reshape(S5_GROUPS, a, b)


def _ret_constants():
    log_g = np.log1p(-np.exp2(-5.0 - np.arange(RET_HEADS, dtype=np.float32))).astype(np.float32)
    idx = np.arange(RET_CHUNK, dtype=np.float32)
    diff = idx[:, None] - idx[None, :]
    decay = np.where(diff >= 0, np.exp(log_g[:, None, None] * np.maximum(diff, 0.0)), 0.0).astype(np.float32)
    xi = np.exp(log_g[None, :] * (idx[:, None] + 1.0)).astype(np.float32)
    zeta = np.exp(log_g[None, :] * (RET_CHUNK - 1.0 - idx[:, None])).astype(np.float32)
    chunk_decay = np.exp(log_g * RET_CHUNK).astype(np.float32)
    return decay, xi, zeta, chunk_decay


def _rope_tables(L):
    half = RET_DK // 2
    inv = ROPE_BASE ** (-jnp.arange(half, dtype=F32) / half)
    ang = jnp.arange(L, dtype=F32)[:, None] * inv[None, :]
    return jnp.cos(ang), jnp.sin(ang)


def _rot(xh, cos, sin):
    x1, x2 = xh[:, :128], xh[:, 128:]
    return jnp.concatenate([x1 * cos - x2 * sin, x1 * sin + x2 * cos], axis=1)


def _rot_t(dh, cos, sin):
    d1, d2 = dh[:, :128], dh[:, 128:]
    return jnp.concatenate([d1 * cos + d2 * sin, d2 * cos - d1 * sin], axis=1)


RET_PER_STEP = 2


def _ret_setup(L):
    nc = L // RET_CHUNK
    per = RET_PER_STEP if nc % RET_PER_STEP == 0 else 1
    decay_np, xi_np, zeta_np, cd_np = _ret_constants()
    tables = (jnp.asarray(decay_np), jnp.asarray(np.tile(xi_np, (per, 1))), jnp.asarray(np.tile(zeta_np, (per, 1))))
    return nc // per, per, tables, [float(c) for c in cd_np]


def _ret_rows(q_ref, k_ref, v_ref, cos_ref, sin_ref, xi_ref, zeta_ref):
    H = range(RET_HEADS)
    hs = [slice(RET_DK * h, RET_DK * (h + 1)) for h in H]
    cs, sn = cos_ref[...], sin_ref[...]
    qh = [_rot(q_ref[:, hs[h]], cs, sn) for h in H]
    kh = [_rot(k_ref[:, hs[h]], cs, sn) * (RET_DK ** -0.5) for h in H]
    vh = [v_ref[:, hs[h]] for h in H]
    qx = [qh[h] * xi_ref[:, h:h + 1] for h in H]
    kz = [kh[h] * zeta_ref[:, h:h + 1] for h in H]
    return hs, cs, sn, qh, kh, vh, qx, kz


def _ret_normed(qh, kh, vh, qx, dec_ref, prevs, per):
    H, C = range(RET_HEADS), range(per)
    rs = [slice(RET_CHUNK * c, RET_CHUNK * (c + 1)) for c in C]
    sc = [[_mm_nt(qh[h][rs[c]], kh[h][rs[c]]) * dec_ref[h] for h in H] for c in C]
    inner = [[_mm(sc[c][h], vh[h][rs[c]]) for h in H] for c in C]
    cross = [[_mm(qx[h][rs[c]], prevs[c][h]) for h in H] for c in C]
    o = [jnp.concatenate([inner[c][h] + cross[c][h] for c in C], axis=0) for h in H]
    oc = [o[h] - jnp.mean(o[h], axis=-1, keepdims=True) for h in H]
    rstd = [lax.rsqrt(jnp.mean(oc[h] * oc[h], axis=-1, keepdims=True) + NORM_EPS) for h in H]
    on = [oc[h] * rstd[h] for h in H]
    return rs, sc, rstd, on


def retention_forward(p, cos, sin, gain):
    L = p.shape[0]
    steps, per, (decay, xi, zeta), cd = _ret_setup(L)
    rows = RET_CHUNK * per

    def body(q_ref, k_ref, v_ref, z_ref, cos_ref, sin_ref, dec_ref, xi_ref, zeta_ref, gain_ref,
             yb_ref, prev_ref, state):
        @pl.when(pl.program_id(0) == 0)
        def _():
            state[...] = jnp.zeros_like(state)

        H, C = range(RET_HEADS), range(per)
        hs, cs, sn, qh, kh, vh, qx, kz = _ret_rows(q_ref, k_ref, v_ref, cos_ref, sin_ref, xi_ref, zeta_ref)
        prevs = [[state[h] for h in H]]
        for c in C:
            rs_c = slice(RET_CHUNK * c, RET_CHUNK * (c + 1))
            prevs.append([prevs[c][h] * cd[h] + _mm_tn(kz[h][rs_c], vh[h][rs_c]) for h in H])
        _, _, _, on = _ret_normed(qh, kh, vh, qx, dec_ref, prevs, per)
        sz, _ = _silu_and_grad(z_ref[...])
        for h in H:
            for c in C:
                prev_ref[c, h] = prevs[c][h].astype(prev_ref.dtype)
            state[h] = prevs[per][h]
            yb_ref[:, hs[h]] = (on[h] * gain_ref[:, hs[h]] * sz[:, hs[h]]).astype(yb_ref.dtype)

    col0 = p.shape[1] // 1024 - 4
    blk = lambda c: pl.BlockSpec((rows, 1024), lambda i, c=c: (i, c + col0))
    return pl.pallas_call(
        body, name="retention_forward", grid=(steps,),
        in_specs=[blk(0), blk(1), blk(2), blk(3),
                  pl.BlockSpec((rows, 128), lambda i: (i, 0)), pl.BlockSpec((rows, 128), lambda i: (i, 0)),
                  _full(decay.shape), _full(xi.shape), _full(zeta.shape), _full((1, 1024))],
        out_specs=[pl.BlockSpec((rows, 1024), lambda i: (i, 0)),
                   pl.BlockSpec((per, RET_HEADS, RET_DK, RET_DK), lambda i: (i, 0, 0, 0))],
        out_shape=[jax.ShapeDtypeStruct((L, 1024), MXU_DTYPE),
                   jax.ShapeDtypeStruct((steps * per, RET_HEADS, RET_DK, RET_DK), MXU_DTYPE)],
        scratch_shapes=[pltpu.VMEM((RET_HEADS, RET_DK, RET_DK), F32)],
        compiler_params=_cparams(("arbitrary",)),
    )(p, p, p, p, cos, sin, decay, xi, zeta, gain)


def retention_backward(p, dy, prevs, cos, sin, gain, plan=None):
    L = p.shape[0]
    steps, per, (decay, xi, zeta), cd = _ret_setup(L)
    rows = RET_CHUNK * per
    scale = RET_DK ** -0.5

    def body(q_ref, k_ref, v_ref, z_ref, dyb_ref, prev_ref, cos_ref, sin_ref, dec_ref, xi_ref, zeta_ref, gain_ref,
             dp_ref, dgain_ref, dstate):
        @pl.when(pl.program_id(0) == 0)
        def _():
            dstate[...] = jnp.zeros_like(dstate)
            dgain_ref[...] = jnp.zeros_like(dgain_ref)

        H, C = range(RET_HEADS), range(per)
        hs, cs, sn, qh, kh, vh, qx, kz = _ret_rows(q_ref, k_ref, v_ref, cos_ref, sin_ref, xi_ref, zeta_ref)
        prevs = [[prev_ref[c, h] for h in H] for c in C]
        rs, sc, rstd, on = _ret_normed(qh, kh, vh, qx, dec_ref, prevs, per)
        sz, dsz = _silu_and_grad(z_ref[...])
        dyb = dyb_ref[...]
        dong = [dyb[:, hs[h]] * sz[:, hs[h]] for h in H]
        don = [dong[h] * gain_ref[:, hs[h]] for h in H]
        do = [rstd[h] * (don[h] - jnp.mean(don[h], axis=-1, keepdims=True)
                         - on[h] * jnp.mean(don[h] * on[h], axis=-1, keepdims=True)) for h in H]
        dsc = [[_mm_nt(do[h][rs[c]], vh[h][rs[c]]) * dec_ref[h] for h in H] for c in C]
        dq_st = [[_mm_nt(do[h][rs[c]], prevs[c][h]) for h in H] for c in C]
        dnew = [[_mm_tn(qx[h][rs[c]], do[h][rs[c]]) for h in H] for c in C]
        dsts = [None] * per + [[dstate[h] for h in H]]
        for c in reversed(C):
            dsts[c] = [dsts[c + 1][h] * cd[h] + dnew[c][h] for h in H]
        dk_st = [[_mm_nt(vh[h][rs[c]], dsts[c + 1][h]) for h in H] for c in C]
        dv_st = [[_mm(kz[h][rs[c]], dsts[c + 1][h]) for h in H] for c in C]
        rows_of = lambda parts: jnp.concatenate(parts, axis=0)
        dqh = [rows_of([_mm(dsc[c][h], kh[h][rs[c]]) for c in C])
               + rows_of([dq_st[c][h] for c in C]) * xi_ref[:, h:h + 1] for h in H]
        dkh = [rows_of([_mm_tn(dsc[c][h], qh[h][rs[c]]) for c in C])
               + rows_of([dk_st[c][h] for c in C]) * zeta_ref[:, h:h + 1] for h in H]
        dvh = [rows_of([_mm_tn(sc[c][h], do[h][rs[c]]) + dv_st[c][h] for c in C]) for h in H]
        for h in H:
            dstate[h] = dsts[0][h]
            dgain_ref[:, hs[h]] += jnp.sum(dong[h] * on[h], axis=0, keepdims=True)
            dp_ref[:, hs[h]] = _rot_t(dqh[h], cs, sn).astype(dp_ref.dtype)
            dp_ref[:, 1024 + RET_DK * h:1024 + RET_DK * (h + 1)] = (_rot_t(dkh[h], cs, sn) * scale).astype(dp_ref.dtype)
            dp_ref[:, 2048 + RET_DK * h:2048 + RET_DK * (h + 1)] = dvh[h].astype(dp_ref.dtype)
            dp_ref[:, 3072 + RET_DK * h:3072 + RET_DK * (h + 1)] = (
                dyb[:, hs[h]] * on[h] * gain_ref[:, hs[h]] * dsz[:, hs[h]]).astype(dp_ref.dtype)

    col0 = p.shape[1] // 1024 - 4
    blk = lambda c: pl.BlockSpec((rows, 1024), lambda i, c=c: (steps - 1 - i, c + col0))
    tab = pl.BlockSpec((rows, 128), lambda i: (steps - 1 - i, 0))
    return _call(
        body, plan, name="retention_backward", grid=(steps,),
        in_specs=[blk(0), blk(1), blk(2), blk(3), pl.BlockSpec((rows, 1024), lambda i: (steps - 1 - i, 0)),
                  pl.BlockSpec((per, RET_HEADS, RET_DK, RET_DK), lambda i: (steps - 1 - i, 0, 0, 0)),
                  tab, tab, _full(decay.shape), _full(xi.shape), _full(zeta.shape), _full((1, 1024))],
        out_specs=[pl.BlockSpec((rows, 4096), lambda i: (steps - 1 - i, 0)), _full((1, 1024))],
        out_shape=[jax.ShapeDtypeStruct((L, 4096), MXU_DTYPE), jax.ShapeDtypeStruct((1, 1024), F32)],
        scratch_shapes=[pltpu.VMEM((RET_HEADS, RET_DK, RET_DK), F32)],
        sem=("arbitrary",),
    )(p, p, p, p, dy, prevs, cos, sin, decay, xi, zeta, gain)


def _sgu_mix(p_ref, gain_ref, wm_ref, bt_ref, tl):
    pu, pv, z = p_ref[:, :2048], p_ref[:, 2048:4096], p_ref[:, 4096:]
    (u, du), (v, dv) = _gelu_and_grad(pu), _gelu_and_grad(pv)
    mu = jnp.mean(v, axis=-1, keepdims=True)
    vc = v - mu
    rstd = lax.rsqrt(jnp.mean(vc * vc, axis=-1, keepdims=True) + NORM_EPS)
    vn = vc * rstd
    vg = vn * gain_ref[...]
    mask = (lax.broadcasted_iota(jnp.int32, (SGU_CHUNK, SGU_CHUNK), 0)
            >= lax.broadcasted_iota(jnp.int32, (SGU_CHUNK, SGU_CHUNK), 1))
    wms = [jnp.where(mask, wm_ref[g], 0.0) for g in range(SGU_GROUPS)]
    rows = []
    for c in range(tl // SGU_CHUNK):
        rs = slice(SGU_CHUNK * c, SGU_CHUNK * (c + 1))
        cols = []
        for g in range(SGU_GROUPS):
            gs = slice(SGU_GDIM * g, SGU_GDIM * (g + 1))
            cols.append(_mm(wms[g], vg[rs, gs]) + bt_ref[:, g:g + 1])
        rows.append(jnp.concatenate(cols, axis=1))
    s = rows[0] if len(rows) == 1 else jnp.concatenate(rows, axis=0)
    return du, dv, z, u, vn, rstd, vg, wms, mask, s


def sgu_forward(p, gain, wm, bt):
    L = p.shape[0]
    tl = min(TL_SGU, L)

    def body(p_ref, gain_ref, wm_ref, bt_ref, y_ref):
        _, _, z, u, _, _, _, _, _, s = _sgu_mix(p_ref, gain_ref, wm_ref, bt_ref, tl)
        sz, _ = _silu_and_grad(z)
        y_ref[...] = (u * s * sz).astype(y_ref.dtype)

    return pl.pallas_call(
        body, name="sgu_forward", grid=(L // tl,),
        in_specs=[pl.BlockSpec((tl, ODD_IN), lambda i: (i, 0)), _full((1, 2048)), _full(wm.shape), _full(bt.shape)],
        out_specs=pl.BlockSpec((tl, 2048), lambda i: (i, 0)),
        out_shape=jax.ShapeDtypeStruct((L, 2048), MXU_DTYPE),
        compiler_params=_cparams(("arbitrary",)),
    )(p, gain, wm, bt)


def sgu_backward(p, dy, gain, wm, bt, plan=None):
    L = p.shape[0]
    tl = min(TL_SGU, L)

    def body(p_ref, dy_ref, gain_ref, wm_ref, bt_ref, dp_ref, dgain_ref, dwm_ref, dbt_ref):
        @pl.when(pl.program_id(0) == 0)
        def _():
            dgain_ref[...] = jnp.zeros_like(dgain_ref)
            dwm_ref[...] = jnp.zeros_like(dwm_ref)
            dbt_ref[...] = jnp.zeros_like(dbt_ref)

        gu, gv, z, u, vn, rstd, vg, wms, mask, s = _sgu_mix(p_ref, gain_ref, wm_ref, bt_ref, tl)
        sz, dsz = _silu_and_grad(z)
        dyv = dy_ref[...]
        dp_ref[:, 4096:] = (dyv * u * s * dsz).astype(dp_ref.dtype)
        dsg = dyv * sz
        dp_ref[:, :2048] = (dsg * s * gu).astype(dp_ref.dtype)
        ds = dsg * u
        rows = []
        dbs = [jnp.zeros((SGU_CHUNK, 1), F32) for _ in range(SGU_GROUPS)]
        for c in range(tl // SGU_CHUNK):
            rs = slice(SGU_CHUNK * c, SGU_CHUNK * (c + 1))
            cols = []
            for g in range(SGU_GROUPS):
                gs = slice(SGU_GDIM * g, SGU_GDIM * (g + 1))
                dsg_c = ds[rs, gs]
                dbs[g] = dbs[g] + jnp.sum(dsg_c, axis=1, keepdims=True)
                dwm_ref[g] += jnp.where(mask, _mm_nt(dsg_c, vg[rs, gs]), 0.0)
                cols.append(_mm_tn(wms[g], dsg_c))
            rows.append(jnp.concatenate(cols, axis=1))
        dbt_ref[...] += jnp.concatenate(dbs, axis=1)
        dvg = rows[0] if len(rows) == 1 else jnp.concatenate(rows, axis=0)
        dgain_ref[...] += jnp.sum(dvg * vn, axis=0, keepdims=True)
        dvn = dvg * gain_ref[...]
        dv = rstd * (dvn - jnp.mean(dvn, axis=-1, keepdims=True) - vn * jnp.mean(dvn * vn, axis=-1, keepdims=True))
        dp_ref[:, 2048:4096] = (dv * gv).astype(dp_ref.dtype)

    return _call(
        body, plan, name="sgu_backward", grid=(L // tl,),
        in_specs=[pl.BlockSpec((tl, ODD_IN), lambda i: (i, 0)), pl.BlockSpec((tl, 2048), lambda i: (i, 0)),
                  _full((1, 2048)), _full(wm.shape), _full(bt.shape)],
        out_specs=[pl.BlockSpec((tl, ODD_IN), lambda i: (i, 0)), _full((1, 2048)), _full(wm.shape), _full(bt.shape)],
        out_shape=[jax.ShapeDtypeStruct((L, ODD_IN), MXU_DTYPE), jax.ShapeDtypeStruct((1, 2048), F32),
                   jax.ShapeDtypeStruct(wm.shape, F32), jax.ShapeDtypeStruct(bt.shape, F32)],
        sem=("arbitrary",),
    )(p, dy, gain, wm, bt)


def cast_shards(mats):
    n = len(mats)
    steps = 8

    def body(*refs):
        for p in range(n):
            refs[n + p][...] = refs[p][...].astype(MXU_DTYPE)

    specs = [pl.BlockSpec((m.shape[0] // steps, m.shape[1]), lambda i: (i, 0)) for m in mats]
    return pl.pallas_call(
        body, name="cast_shards", grid=(steps,), in_specs=specs, out_specs=specs,
        out_shape=[jax.ShapeDtypeStruct(m.shape, MXU_DTYPE) for m in mats],
        compiler_params=_cparams(("arbitrary",)),
    )(*mats)


def local_grads(x, tgt, w):
    L = x.shape[0]
    ne, gf = w["norm_even"], w["final_norm"].reshape(1, D_MODEL)
    sh = dict(zip(MATRICES, cast_shards([w[n][0] for n in MATRICES])))
    lam_re, lam_im = w["s5_lam_re"][0], w["s5_lam_im"][0]
    log_dt = w["s5_log_dt"].reshape(S5_GROUPS, 1)
    bt_re = jnp.transpose(w["s5_b_re"][0], (2, 0, 1))
    bt_im = jnp.transpose(w["s5_b_im"][0], (2, 0, 1))
    c_re, c_im = w["s5_c_re"][0], w["s5_c_im"][0]
    wm = w["sgu_w_spatial"][0]
    bt = jnp.transpose(w["sgu_b_spatial"][0])

    tl5 = min(TL_S5, L)
    ab_re, ab_im, bb_re, bb_im, at_re, at_im = s5_params_fwd(lam_re, lam_im, log_dt, bt_re, bt_im, tl5 // 8)
    atab = jnp.stack([ab_re.reshape(S5_LANES), ab_im.reshape(S5_LANES),
                      at_re.reshape(S5_LANES), at_im.reshape(S5_LANES)])
    wbd = jnp.concatenate([_block_diag(jnp.transpose(bb_re, (1, 0, 2)), True),
                           _block_diag(jnp.transpose(bb_im, (1, 0, 2)), True)], axis=2).astype(MXU_DTYPE)
    cre = _block_diag(jnp.transpose(c_re, (0, 2, 1)), True).astype(MXU_DTYPE)
    cim = _block_diag(jnp.transpose(c_im, (0, 2, 1)), True).astype(MXU_DTYPE)
    cos, sin = _rope_tables(L)

    s5_cols = 2 * S5_WIDTH
    me = (2 * lax.axis_index("x") + lax.axis_index("y")).astype(jnp.int32)
    xs = stream_order(x, tl5)
    slab = lambda d: jnp.stack([me ^ d])
    zero = jnp.zeros((1,), jnp.int32)
    shards = [sh["w_in_even"][None]]
    (p1, h0s, h0), (got,) = even_in_slabs(x, xs, ne, shards[0], slab(0), zero, "even_in_0",
                                          plan=gather_plan([sh["w_in_even"]], only=1))
    for d in (1, 2, 3):
        shards.append(got)
        plan = gather_plan([sh["w_in_even"]], only=d + 1) if d < 3 else gather_plan([sh["s5_w_glu"]])
        (p1,), (got,) = even_in_slabs(x, xs, ne, shards[d], slab(d), zero, "even_in_%d" % d, p_in=p1, plan=plan)
    w_glu = got
    by_xor = jnp.concatenate(shards)
    w_in_e = [lax.dynamic_index_in_dim(by_xor, me ^ j, 0, keepdims=False) for j in range(N_CHIPS)]
    w_s5 = jnp.concatenate([w_in_e[0], w_in_e[1][:, :s5_cols - EVEN_IN // N_CHIPS]], axis=1)
    w_ret = jnp.concatenate([w_in_e[1][:, s5_cols - EVEN_IN // N_CHIPS:], w_in_e[2], w_in_e[3]], axis=1)
    w_glu = w_glu.reshape(S5_WIDTH, S5_WIDTH)
    (ya, st_re, st_im, sv_re, sv_im), (w_out_e, w_in_o, w_out_o, no, sg_gain) = s5_forward(
        p1, wbd, cre, cim, atab, w["s5_d"], w_glu, w["s5_b_glu"],
        gather_plan([sh["w_out_even"], sh["w_in_odd"], sh["w_out_odd"], w["norm_odd"], w["sgu_norm_gain"]]))
    w_out_e = w_out_e.reshape(2 * S5_WIDTH, D_MODEL)
    w_out_o = w_out_o.reshape(SGU_WIDTH, D_MODEL)
    no, sg_gain = no.reshape(1, D_MODEL), sg_gain.reshape(1, SGU_WIDTH)
    yb, prevs = retention_forward(p1, cos, sin, w["ret_gn_gain"])
    ya = token_order(ya, tl5)
    x1 = matmul_residual([ya, yb], w_out_e, x, "even_out")
    (p2, h1), _ = norm_matmul(x1, no, w_in_o, "odd_in")
    y2 = sgu_forward(p2, sg_gain, wm, bt)
    dx2, loss, dgf = out_proj_loss(y2, w_out_o, x1, gf, tgt, "odd_out_loss")

    g, landed = {}, {}
    shard_major = lambda a, n: a.reshape((N_CHIPS,) + w[n].shape[1:])
    dy2, g_w_out_o = out_proj_bwd(dx2, w_out_o, [y2], "odd_out_bwd")
    (dp2, g["sgu_norm_gain"], dwm, dbt), (landed["w_out_odd"],) = sgu_backward(
        p2, dy2, sg_gain, wm, bt, reduce_plan([shard_major(g_w_out_o, "w_out_odd")]))
    g_w_in_o, _ = in_proj_bwd_dw(h1, dp2, "odd_in_dw", ODD_IN // N_CHIPS)
    (dx1, g["norm_odd"]), _ = in_proj_bwd_dx(x1, no, [dp2], [w_in_o], dx2, "odd_in_dx")
    dya, dyb, g_w_out_e = out_proj_bwd(dx1, w_out_e, [ya, yb], "even_out_bwd")
    ((dpa, dwbd, dcre, dcim, dab_re, dab_im, g["s5_d"], g_w_glu, g["s5_b_glu"]),
     (landed["w_in_odd"], landed["w_out_even"])) = s5_backward(
        p1, stream_order(dya, tl5), st_re, st_im, sv_re, sv_im, wbd, cre, cim, atab, w["s5_d"], w_glu,
        w["s5_b_glu"], reduce_plan([g_w_in_o, shard_major(g_w_out_e, "w_out_even")]))

    dbb_re = jnp.transpose(_block_diag_extract(dwbd[:, :, :512], S5_GROUP, S5_STATE), (1, 0, 2))
    dbb_im = jnp.transpose(_block_diag_extract(dwbd[:, :, 512:], S5_GROUP, S5_STATE), (1, 0, 2))
    dlr, dli, ddt, dbt_re, dbt_im = s5_params_bwd(
        lam_re, lam_im, log_dt, bt_re, bt_im, dab_re.reshape(8, S5_GROUPS, S5_STATE),
        dab_im.reshape(8, S5_GROUPS, S5_STATE), dbb_re, dbb_im)
    g["s5_lam_re"], g["s5_lam_im"] = dlr[None], dli[None]
    g["s5_log_dt"] = ddt.reshape(1, S5_GROUPS)
    g["s5_b_re"], g["s5_b_im"] = dbt_re, dbt_im
    g["s5_c_re"] = _block_diag_extract(dcre, S5_GROUP, S5_STATE)[None]
    g["s5_c_im"] = _block_diag_extract(dcim, S5_GROUP, S5_STATE)[None]
    g["sgu_w_spatial"] = dwm[None]
    g["sgu_b_spatial"] = jnp.transpose(dbt)[None]
    g["final_norm"] = dgf.reshape(D_MODEL)
    g["loss"] = loss

    (dpb, g["ret_gn_gain"]), (landed["s5_w_glu"],) = retention_backward(
        p1, dyb, prevs, cos, sin, w["ret_gn_gain"], reduce_plan([shard_major(g_w_glu, "s5_w_glu")]))
    done = tuple(n for n in MATRICES if n != "w_in_even")
    part = {n: sum_slabs(landed[n], "sum_" + n) for n in done}
    g_w_in_e, recv = in_proj_bwd_dw(h0s, dpa, "even_in_dw_s5", 512, dtype=MXU_DTYPE,
                                    plan=_SiblingPlan([part[n] for n in done]))
    other = dict(zip(done, recv))
    small = tuple(n for n in SMALL if n != "norm_even") + ("loss",)
    g_w_in_e, recv = in_proj_bwd_dw(h0, dpb, "even_in_dw_ret", 512, first=s5_cols // 512, into=g_w_in_e,
                                    dtype=MXU_DTYPE, plan=reduce_plan([], [g[n] for n in small]))
    landed.update(zip(small, recv))
    (dx0, g["norm_even"]), (landed["w_in_even"],) = in_proj_bwd_dx(
        x, ne, [token_order(dpa, tl5), dpb], [w_s5, w_ret], dx1, "even_in_dx", reduce_plan([g_w_in_e]))
    (landed["norm_even"],) = run_plan(reduce_plan([], [g["norm_even"]]), "exchange_norm_even")
    return dx0, landed, part, other


def _row_block(rows):
    return 128 if rows % 128 == 0 else rows


def sum_slabs(r, name):
    _, R, C = r.shape
    tr = _row_block(R)

    def body(r_ref, o_ref):
        a, b, c, d = (r_ref[k].astype(F32) for k in range(N_CHIPS))
        o_ref[...] = (a + b) + (c + d)

    return pl.pallas_call(
        body, name=name, grid=(R // tr,),
        in_specs=[pl.BlockSpec((N_CHIPS, tr, C), lambda i: (0, i, 0))],
        out_specs=pl.BlockSpec((tr, C), lambda i: (i, 0)),
        out_shape=jax.ShapeDtypeStruct((R, C), F32),
        compiler_params=_cparams(("arbitrary",)),
    )(r)


def _adam(w, m, v, g):
    mn = ADAM_B1 * m + (1.0 - ADAM_B1) * g
    vn = ADAM_B2 * v + (1.0 - ADAM_B2) * (g * g)
    m_hat = mn / (1.0 - ADAM_B1 ** ADAM_STEP)
    v_hat = vn / (1.0 - ADAM_B2 ** ADAM_STEP)
    return -ADAM_LR * (m_hat / (jnp.sqrt(v_hat) + ADAM_EPS) + ADAM_WD * w), mn, vn


def adam_update(w, m, v, ga, gb, name):
    R, C = w.shape
    tr = _row_block(R)

    def body(w_ref, m_ref, v_ref, ga_ref, gb_ref, g_out, d_out, m_out, v_out):
        g = ga_ref[...] + gb_ref[...]
        g_out[...] = g
        d_out[...], m_out[...], v_out[...] = _adam(w_ref[...], m_ref[...], v_ref[...], g)

    blk = pl.BlockSpec((tr, C), lambda i: (i, 0))
    return pl.pallas_call(
        body, name=name, grid=(R // tr,),
        in_specs=[blk] * 5, out_specs=[blk] * 4,
        out_shape=[jax.ShapeDtypeStruct((R, C), F32)] * 4,
        compiler_params=_cparams(("arbitrary",)),
    )(w, m, v, ga, gb)


WIDE_ROWS = ("s5_b_re", "s5_b_im")


def sum_small(landed):
    def body(*refs):
        k = len(refs) // 2
        for i in range(k):
            r = refs[i]
            refs[k + i][...] = (r[0] + r[1]) + (r[2] + r[3])

    names = list(landed)
    res = pl.pallas_call(
        body, name="sum_small", out_shape=[jax.ShapeDtypeStruct(landed[n].shape[1:], F32) for n in names],
        compiler_params=pltpu.CompilerParams(vmem_limit_bytes=VMEM_LIMIT),
    )(*[landed[n] for n in names])
    return dict(zip(names, res))


def adam_small(names, w, m, v, ga, gb):
    def body(*refs):
        k = len(refs) // 9
        me = 2 * lax.axis_index("x") + lax.axis_index("y")
        for i in range(k):
            w_ref, m_ref, v_ref, ga_ref, gb_ref = refs[i], refs[k + i], refs[2 * k + i], refs[3 * k + i], refs[4 * k + i]
            size = w_ref.shape[-1]
            if ga_ref.shape != w_ref.shape:
                part = pl.ds(pl.multiple_of(me * size, LANES), size)
                g = ga_ref[:, part] + gb_ref[:, part]
            else:
                g = ga_ref[...] + gb_ref[...]
            refs[5 * k + i][...] = g
            refs[6 * k + i][...], refs[7 * k + i][...], refs[8 * k + i][...] = _adam(w_ref[...], m_ref[...], v_ref[...], g)

    ins = [d[n] for d in (w, m, v, ga, gb) for n in names]
    outs = [jax.ShapeDtypeStruct(w[n].shape, F32) for _ in range(4) for n in names]
    res = pl.pallas_call(body, name="adam_small", out_shape=outs,
                         compiler_params=pltpu.CompilerParams(vmem_limit_bytes=VMEM_LIMIT))(*ins)
    k = len(names)
    return [dict(zip(names, res[j * k:(j + 1) * k])) for j in range(4)]


WEIGHTS = ("norm_even", "w_in_even", "s5_lam_re", "s5_lam_im", "s5_log_dt", "s5_b_re", "s5_b_im", "s5_c_re",
           "s5_c_im", "s5_d", "s5_w_glu", "s5_b_glu", "ret_gn_gain", "w_out_even", "norm_odd", "w_in_odd",
           "sgu_norm_gain", "sgu_w_spatial", "sgu_b_spatial", "w_out_odd", "final_norm")
MATRICES = ("w_in_even", "s5_w_glu", "w_out_even", "w_in_odd", "w_out_odd")
SHARDED_VECS = ("norm_odd", "sgu_norm_gain")
REPLICATED = tuple(n for n in WEIGHTS if n not in MATRICES and n not in SHARDED_VECS)
SMALL = tuple(n for n in WEIGHTS if n not in MATRICES)
LANES = 128


def kernel(x, norm_even, w_in_even, s5_lam_re, s5_lam_im, s5_log_dt, s5_b_re, s5_b_im, s5_c_re, s5_c_im, s5_d, s5_w_glu, s5_b_glu, ret_gn_gain, w_out_even, norm_odd, w_in_odd, sgu_norm_gain, sgu_w_spatial, sgu_b_spatial, w_out_odd, final_norm, loss_target, m_norm_even, m_w_in_even, m_s5_lam_re, m_s5_lam_im, m_s5_log_dt, m_s5_b_re, m_s5_b_im, m_s5_c_re, m_s5_c_im, m_s5_d, m_s5_w_glu, m_s5_b_glu, m_ret_gn_gain, m_w_out_even, m_norm_odd, m_w_in_odd, m_sgu_norm_gain, m_sgu_w_spatial, m_sgu_b_spatial, m_w_out_odd, m_final_norm, v_norm_even, v_w_in_even, v_s5_lam_re, v_s5_lam_im, v_s5_log_dt, v_s5_b_re, v_s5_b_im, v_s5_c_re, v_s5_c_im, v_s5_d, v_s5_w_glu, v_s5_b_glu, v_ret_gn_gain, v_w_out_even, v_norm_odd, v_w_in_odd, v_sgu_norm_gain, v_sgu_w_spatial, v_sgu_b_spatial, v_w_out_odd, v_final_norm):
    w = dict(norm_even=norm_even, w_in_even=w_in_even, s5_lam_re=s5_lam_re, s5_lam_im=s5_lam_im, s5_log_dt=s5_log_dt, s5_b_re=s5_b_re, s5_b_im=s5_b_im, s5_c_re=s5_c_re, s5_c_im=s5_c_im, s5_d=s5_d, s5_w_glu=s5_w_glu, s5_b_glu=s5_b_glu, ret_gn_gain=ret_gn_gain, w_out_even=w_out_even, norm_odd=norm_odd, w_in_odd=w_in_odd, sgu_norm_gain=sgu_norm_gain, sgu_w_spatial=sgu_w_spatial, sgu_b_spatial=sgu_b_spatial, w_out_odd=w_out_odd, final_norm=final_norm)
    m = dict(norm_even=m_norm_even, w_in_even=m_w_in_even, s5_lam_re=m_s5_lam_re, s5_lam_im=m_s5_lam_im, s5_log_dt=m_s5_log_dt, s5_b_re=m_s5_b_re, s5_b_im=m_s5_b_im, s5_c_re=m_s5_c_re, s5_c_im=m_s5_c_im, s5_d=m_s5_d, s5_w_glu=m_s5_w_glu, s5_b_glu=m_s5_b_glu, ret_gn_gain=m_ret_gn_gain, w_out_even=m_w_out_even, norm_odd=m_norm_odd, w_in_odd=m_w_in_odd, sgu_norm_gain=m_sgu_norm_gain, sgu_w_spatial=m_sgu_w_spatial, sgu_b_spatial=m_sgu_b_spatial, w_out_odd=m_w_out_odd, final_norm=m_final_norm)
    v = dict(norm_even=v_norm_even, w_in_even=v_w_in_even, s5_lam_re=v_s5_lam_re, s5_lam_im=v_s5_lam_im, s5_log_dt=v_s5_log_dt, s5_b_re=v_s5_b_re, s5_b_im=v_s5_b_im, s5_c_re=v_s5_c_re, s5_c_im=v_s5_c_im, s5_d=v_s5_d, s5_w_glu=v_s5_w_glu, s5_b_glu=v_s5_b_glu, ret_gn_gain=v_ret_gn_gain, w_out_even=v_w_out_even, norm_odd=v_norm_odd, w_in_odd=v_w_in_odd, sgu_norm_gain=v_sgu_norm_gain, sgu_w_spatial=v_sgu_w_spatial, sgu_b_spatial=v_sgu_b_spatial, w_out_odd=v_w_out_odd, final_norm=v_final_norm)

    grad_x, landed, part, other = local_grads(x[0], loss_target[0], w)

    small = SMALL + ("loss",)
    part["w_in_even"] = sum_slabs(landed["w_in_even"], "sum_w_in_even")
    part.update(sum_small({n: landed[n] for n in small}))
    names = ("w_in_even",) + small
    other.update(zip(names, run_plan(_SiblingPlan([part[n] for n in names]), "sibling_exchange")))

    wt, mt, vt = dict(w), dict(m), dict(v)
    for n in WIDE_ROWS:
        wt[n], mt[n], vt[n] = (jnp.transpose(a[n][0], (2, 0, 1)) for a in (w, m, v))
    out_g, out_d, out_m, out_v = adam_small(SMALL, wt, mt, vt, part, other)
    for n in WIDE_ROWS:
        for out in (out_g, out_d, out_m, out_v):
            out[n] = jnp.transpose(out[n], (1, 2, 0))[None]
    for n in MATRICES:
        res = adam_update(w[n][0], m[n][0], v[n][0], part[n], other[n], "adam_" + n)
        out_g[n], out_d[n], out_m[n], out_v[n] = (r[None] for r in res)
    total_loss = (part["loss"] + other["loss"])[0, 0]

    return (total_loss, grad_x[None], *[out_g[n] for n in WEIGHTS], *[out_d[n] for n in WEIGHTS],
            *[out_m[n] for n in WEIGHTS], *[out_v[n] for n in WEIGHTS])
```

```python
import functools
import math

import numpy as np
import jax
import jax.numpy as jnp
from jax import lax
from jax.experimental import pallas as pl
from jax.experimental.pallas import tpu as pltpu

F32 = jnp.float32
MXU_DTYPE = jnp.bfloat16
NORM_EPS = 1e-6
D_MODEL = 1024
S5_WIDTH = 1024
S5_GROUP = 16
S5_GROUPS = 64
S5_STATE = 64
S5_LANES = S5_GROUPS * S5_STATE
S5_KBLK = 8
RET_HEADS = 4
RET_DK = 256
RET_CHUNK = 128
ROPE_BASE = 10000.0
SGU_WIDTH = 2048
SGU_GROUPS = 4
SGU_GDIM = 512
SGU_CHUNK = 128
EVEN_IN = 6144
ODD_IN = 6144
ADAM_LR = 0.001
ADAM_B1 = 0.9
ADAM_B2 = 0.999
ADAM_EPS = 1e-08
ADAM_WD = 0.01
ADAM_STEP = 10
N_CHIPS = 4
VMEM_LIMIT = 56 * 1024 * 1024

TL_PROJ = 512
TL_DW = 1024
TL_S5 = 256
TL_SGU = 256


def _cparams(sem, **kw):
    return pltpu.CompilerParams(dimension_semantics=sem, vmem_limit_bytes=VMEM_LIMIT, **kw)


def _mm(a, b):
    return jnp.dot(a.astype(MXU_DTYPE), b.astype(MXU_DTYPE), preferred_element_type=F32)


def _mm_nt(a, b):
    return lax.dot_general(a.astype(MXU_DTYPE), b.astype(MXU_DTYPE),
                           (((1,), (1,)), ((), ())), preferred_element_type=F32)


def _mm_tn(a, b):
    return lax.dot_general(a.astype(MXU_DTYPE), b.astype(MXU_DTYPE),
                           (((0,), (0,)), ((), ())), preferred_element_type=F32)


_GELU_C = math.sqrt(2.0 / math.pi)


def _gelu_parts(x):
    x2 = x * x
    th = jnp.tanh(x * (_GELU_C + (_GELU_C * 0.044715) * x2))
    hx = 0.5 * x
    return hx + hx * th, th, x2, hx


def _gelu(x):
    return _gelu_parts(x)[0]


def _gelu_and_grad(x):
    g, th, x2, hx = _gelu_parts(x)
    return g, (0.5 + 0.5 * th) + hx * (1.0 - th * th) * (_GELU_C + (3.0 * _GELU_C * 0.044715) * x2)


def _gelu_grad(x):
    return _gelu_and_grad(x)[1]


def _sigmoid(x):
    return 1.0 / (1.0 + jnp.exp(-x))


def _silu_and_grad(x):
    s = _sigmoid(x)
    return x * s, s * (1.0 + x * (1.0 - s))


def _rms(x):
    return lax.rsqrt(jnp.mean(x * x, axis=-1, keepdims=True) + NORM_EPS)


def _full(shape):
    nd = len(shape)
    return pl.BlockSpec(shape, lambda *_: (0,) * nd)


MESH = pl.DeviceIdType.MESH
ANY = pl.BlockSpec(memory_space=pl.ANY)


def _place():
    return lax.axis_index("x"), lax.axis_index("y"), lax.axis_index("c")


def _chip_peer(x, y, c, d):
    return (1 - x if d >= 2 else x, 1 - y if d % 2 else y, c)


class _Plan:
    def __init__(self, inputs, out_shape, build):
        self.inputs, self.out_shape, self._build = list(inputs), list(out_shape), build
        n = len(self.inputs)
        self.sems = [pltpu.SemaphoreType.DMA((n, 3)), pltpu.SemaphoreType.DMA((n, 3)), pltpu.SemaphoreType.DMA((n,))]

    def start(self, in_refs, out_refs, sems):
        send, recv, local = self._build(in_refs, out_refs, sems)
        for p in range(len(self.inputs)):
            local[p].start()
            for cp in send[p]:
                cp.start()

    def wait(self, in_refs, out_refs, sems):
        send, recv, local = self._build(in_refs, out_refs, sems)
        for p in range(len(self.inputs)):
            for cp in recv[p]:
                cp.wait_recv()
        for p in range(len(self.inputs)):
            for cp in send[p]:
                cp.wait_send()
            local[p].wait()


class _GatherPlan:
    def __init__(self, shards, only=None):
        n = len(shards)
        self.n, self.only = n, only
        self.peers = (1, 2, 3) if only is None else (only,)
        self.inputs = list(shards)
        slabs = N_CHIPS if only is None else 1
        self.out_shape = [jax.ShapeDtypeStruct((slabs,) + s.shape, s.dtype) for s in shards]
        self.halved = [s.shape[0] % 32 == 0 for s in shards]
        self.sems = [pltpu.SemaphoreType.DMA((n, 3)) for _ in range(4)] + [pltpu.SemaphoreType.DMA((n,))]

    def _copies(self, in_refs, out_refs, sems):
        ici_s, ici_r, d2d_s, d2d_r, loc = sems
        x, y, c = _place()
        me = 2 * x + y

        def rows(p, core):
            if not self.halved[p]:
                return slice(None)
            half = self.inputs[p].shape[0] // 2
            return pl.ds(pl.multiple_of(core * half, 16), half)

        def slab(chip):
            return chip if self.only is None else 0

        def ici(p, d, chip, core):
            return pltpu.make_async_remote_copy(
                src_ref=in_refs[p].at[rows(p, core)], dst_ref=out_refs[p].at[slab(chip), rows(p, core)],
                send_sem=ici_s.at[p, d - 1], recv_sem=ici_r.at[p, d - 1],
                device_id=_chip_peer(x, y, c, d), device_id_type=MESH)

        def d2d(p, d, core):
            part = out_refs[p].at[slab(me ^ d), rows(p, core)]
            return pltpu.make_async_remote_copy(
                src_ref=part, dst_ref=part, send_sem=d2d_s.at[p, d - 1], recv_sem=d2d_r.at[p, d - 1],
                device_id=(x, y, 1 - c), device_id_type=MESH)

        local = [pltpu.make_async_copy(in_refs[p], out_refs[p].at[slab(me)], loc.at[p]) for p in range(self.n)]
        return me, c, ici, d2d, local

    def start(self, in_refs, out_refs, sems):
        me, c, ici, d2d, local = self._copies(in_refs, out_refs, sems)
        for p in range(self.n):
            if self.only is None:
                local[p].start()
            for d in self.peers:
                ici(p, d, me, c).start()

    def wait(self, in_refs, out_refs, sems):
        me, c, ici, d2d, local = self._copies(in_refs, out_refs, sems)
        for p in range(self.n):
            for d in self.peers:
                ici(p, d, me ^ d, c).wait_recv()
                if self.halved[p]:
                    d2d(p, d, c).start()
        for p in range(self.n):
            for d in self.peers:
                if self.halved[p]:
                    d2d(p, d, 1 - c).wait_recv()
                    d2d(p, d, c).wait_send()
                ici(p, d, me, c).wait_send()
            if self.only is None:
                local[p].wait()


def gather_plan(shards, only=None):
    return _GatherPlan(shards, only)


def reduce_plan(shards, whole=()):
    n_s = len(shards)

    def build(in_refs, out_refs, sems):
        send_sems, recv_sems, loc_sems = sems
        x, y, c = _place()
        me = 2 * x + y

        def src(p, slab):
            return in_refs[p].at[slab] if p < n_s else in_refs[p]

        def remote(p, d):
            return pltpu.make_async_remote_copy(
                src_ref=src(p, me ^ d), dst_ref=out_refs[p].at[d], send_sem=send_sems.at[p, d - 1],
                recv_sem=recv_sems.at[p, d - 1], device_id=_chip_peer(x, y, c, d), device_id_type=MESH)

        n = len(in_refs)
        send = [[remote(p, d) for d in (1, 2, 3)] for p in range(n)]
        local = [pltpu.make_async_copy(src(p, me), out_refs[p].at[0], loc_sems.at[p]) for p in range(n)]
        return send, send, local

    outs = [jax.ShapeDtypeStruct(s.shape, s.dtype) for s in shards]
    outs += [jax.ShapeDtypeStruct((N_CHIPS,) + a.shape, a.dtype) for a in whole]
    return _Plan(list(shards) + list(whole), outs, build)


class _SiblingPlan:
    def __init__(self, arrs):
        self.inputs = list(arrs)
        self.out_shape = [jax.ShapeDtypeStruct(a.shape, a.dtype) for a in arrs]
        n = len(arrs)
        self.sems = [pltpu.SemaphoreType.DMA((n,)), pltpu.SemaphoreType.DMA((n,))]

    def _copies(self, in_refs, out_refs, sems):
        x, y, c = _place()
        return [pltpu.make_async_remote_copy(
            src_ref=in_refs[p], dst_ref=out_refs[p], send_sem=sems[0].at[p], recv_sem=sems[1].at[p],
            device_id=(x, y, 1 - c), device_id_type=MESH) for p in range(len(self.inputs))]

    def start(self, in_refs, out_refs, sems):
        for cp in self._copies(in_refs, out_refs, sems):
            cp.start()

    def wait(self, in_refs, out_refs, sems):
        copies = self._copies(in_refs, out_refs, sems)
        for cp in copies:
            cp.wait_recv()
        for cp in copies:
            cp.wait_send()


def run_plan(plan, name):
    n = len(plan.inputs)

    def body(*refs):
        plan.start(refs[:n], refs[n:2 * n], refs[2 * n:])
        plan.wait(refs[:n], refs[n:2 * n], refs[2 * n:])

    return pl.pallas_call(body, name=name, in_specs=[ANY] * n, out_specs=[ANY] * n, out_shape=plan.out_shape,
                          scratch_shapes=plan.sems)(*plan.inputs)


def _call(body, plan, *, name, grid, in_specs, out_specs, out_shape, sem, scratch_shapes=(), aliases=None,
          n_prefetch=0):
    aliases = {} if aliases is None else aliases
    single = not isinstance(out_shape, (list, tuple))
    out_specs = [out_specs] if single else list(out_specs)
    out_shape = [out_shape] if single else list(out_shape)
    n_in, n_out, n_scr = len(in_specs), len(out_specs), len(scratch_shapes)
    ci = 0 if plan is None else len(plan.inputs)
    co = 0 if plan is None else len(plan.out_shape)

    def hosted(*refs):
        pre, refs = refs[:n_prefetch], refs[n_prefetch:]
        ins, cins = refs[:n_in], refs[n_in:n_in + ci]
        k = n_in + ci
        outs, couts = refs[k:k + n_out], refs[k + n_out:k + n_out + co]
        k += n_out + co
        scr, sems = refs[k:k + n_scr], refs[k + n_scr:]
        ids = [pl.program_id(a) for a in range(len(grid))]
        first = functools.reduce(jnp.logical_and, [i == 0 for i in ids])
        last = functools.reduce(jnp.logical_and, [i == g - 1 for i, g in zip(ids, grid)])

        @pl.when(first)
        def _():
            plan.start(cins, couts, sems)

        body(*pre, *ins, *outs, *scr)

        @pl.when(last)
        def _():
            plan.wait(cins, couts, sems)

    def run(*args):
        hosting = plan is not None
        spec = pltpu.PrefetchScalarGridSpec(
            num_scalar_prefetch=n_prefetch, grid=grid,
            in_specs=list(in_specs) + ([ANY] * ci if hosting else []),
            out_specs=out_specs + ([ANY] * co if hosting else []),
            scratch_shapes=list(scratch_shapes) + (plan.sems if hosting else []))
        res = pl.pallas_call(hosted if hosting else body, name=name, grid_spec=spec,
                             out_shape=out_shape + (plan.out_shape if hosting else []),
                             input_output_aliases=aliases, compiler_params=_cparams(sem),
                             )(*args, *(plan.inputs if hosting else []))
        return (res[0] if single else res[:n_out]), list(res[n_out:])

    return run


def norm_matmul(x, g, w, name, plan=None, tn=None):
    L, D = x.shape
    tl = min(TL_DW, L)
    if w.ndim == 3:
        nt, _, tn = w.shape
        w_spec = pl.BlockSpec((1, D, tn), lambda i, n: (n, 0, 0))
    else:
        nt = w.shape[1] // tn
        w_spec = pl.BlockSpec((D, tn), lambda i, n: (0, n))

    def body(x_ref, g_ref, w_ref, o_ref, h_ref):
        xv = x_ref[...]
        h = (xv * _rms(xv) * g_ref[...]).astype(h_ref.dtype)
        h_ref[...] = h
        o_ref[...] = _mm(h, w_ref[0] if w.ndim == 3 else w_ref[...])

    return _call(
        body, plan, name=name, grid=(L // tl, nt),
        in_specs=[pl.BlockSpec((tl, D), lambda i, n: (i, 0)), _full((1, D)), w_spec],
        out_specs=[pl.BlockSpec((tl, tn), lambda i, n: (i, n)), pl.BlockSpec((tl, D), lambda i, n: (i, 0))],
        out_shape=[jax.ShapeDtypeStruct((L, nt * tn), F32), jax.ShapeDtypeStruct((L, D), MXU_DTYPE)],
        sem=("arbitrary", "arbitrary"),
    )(x, g, w)


def even_in_slabs(x, xs, g, w, slabs, wsel, name, p_in=None, plan=None):
    L, D = x.shape
    tl = min(TL_PROJ, L)
    wb = EVEN_IN // N_CHIPS
    n = slabs.shape[0]
    s5_cols = 2 * S5_WIDTH - wb
    first = p_in is None

    def body(slabs_ref, wsel_ref, xs_ref, x_ref, g_ref, w_ref, *rest):
        o_ref = rest[-3] if first else rest[-1]
        j = slabs_ref[pl.program_id(0)]
        hs = (xs_ref[...] * _rms(xs_ref[...]) * g_ref[...]).astype(MXU_DTYPE)
        h = (x_ref[...] * _rms(x_ref[...]) * g_ref[...]).astype(MXU_DTYPE)
        if first:
            rest[-2][...] = hs
            rest[-1][...] = h
        o_ref[:, :s5_cols] = _mm(jnp.where(j <= 1, hs, h), w_ref[0, :, :s5_cols])
        o_ref[:, s5_cols:] = _mm(jnp.where(j == 0, hs, h), w_ref[0, :, s5_cols:])

    row = pl.BlockSpec((tl, D), lambda s, i, slabs_ref, wsel_ref: (i, 0))
    in_specs = [row if first else
                pl.BlockSpec((tl, D), lambda s, i, slabs_ref, wsel_ref: (jnp.where(slabs_ref[s] <= 1, i, 0), 0)),
                row if first else
                pl.BlockSpec((tl, D), lambda s, i, slabs_ref, wsel_ref: (jnp.where(slabs_ref[s] >= 1, i, 0), 0)),
                pl.BlockSpec((1, D), lambda s, i, slabs_ref, wsel_ref: (0, 0)),
                pl.BlockSpec((1, D, wb), lambda s, i, slabs_ref, wsel_ref: (wsel_ref[s], 0, 0))]
    out_specs = [pl.BlockSpec((tl, wb), lambda s, i, slabs_ref, wsel_ref: (i, slabs_ref[s]))]
    out_shape = [jax.ShapeDtypeStruct((L, EVEN_IN), F32)]
    args = [slabs, wsel, xs, x, g, w]
    if first:
        out_specs += [row, row]
        out_shape += [jax.ShapeDtypeStruct((L, D), MXU_DTYPE)] * 2
    else:
        in_specs.append(ANY)
        args.append(p_in)
    return _call(body, plan, name=name, grid=(n, L // tl), in_specs=in_specs, out_specs=out_specs,
                 out_shape=out_shape, sem=("arbitrary", "arbitrary"), n_prefetch=2,
                 aliases={} if first else {6: 0})(*args)


def split_w_in_even(shards, me):
    D, wb = shards[0].shape[1:]
    s5 = 2 * S5_WIDTH

    def body(me_ref, s0, s1, s2, s3, w_s5, w_ret):
        for d, src in enumerate((s0, s1, s2, s3)):
            for chip in range(N_CHIPS):
                @pl.when((me_ref[0] ^ d) == chip)
                def _(src=src, chip=chip):
                    lo = chip * wb
                    if lo < s5:
                        n = min(wb, s5 - lo)
                        pltpu.sync_copy(src.at[0, :, pl.ds(0, n)], w_s5.at[:, pl.ds(lo, n)])
                    if lo + wb > s5:
                        n = min(wb, lo + wb - s5)
                        pltpu.sync_copy(src.at[0, :, pl.ds(wb - n, n)], w_ret.at[:, pl.ds(lo + wb - n - s5, n)])

    spec = pltpu.PrefetchScalarGridSpec(num_scalar_prefetch=1, grid=(1,), in_specs=[ANY] * N_CHIPS,
                                        out_specs=[ANY, ANY])
    return pl.pallas_call(
        body, name="split_w_in_even", grid_spec=spec,
        out_shape=[jax.ShapeDtypeStruct((D, s5), shards[0].dtype),
                   jax.ShapeDtypeStruct((D, EVEN_IN - s5), shards[0].dtype)],
    )(me, *shards)


def matmul_residual(ys, w, x, name):
    L, D = x.shape
    tl = min(TL_PROJ, L)
    n = len(ys)
    offs = np.cumsum([0] + [y.shape[1] for y in ys])

    def body(*refs):
        y_refs, w_ref, x_ref, o_ref = refs[:n], refs[n], refs[n + 1], refs[n + 2]
        acc = x_ref[...]
        for k in range(n):
            acc = acc + _mm(y_refs[k][...], w_ref[offs[k]:offs[k + 1], :])
        o_ref[...] = acc

    return pl.pallas_call(
        body, name=name, grid=(L // tl,),
        in_specs=[pl.BlockSpec((tl, y.shape[1]), lambda i: (i, 0)) for y in ys]
        + [_full(w.shape), pl.BlockSpec((tl, D), lambda i: (i, 0))],
        out_specs=pl.BlockSpec((tl, D), lambda i: (i, 0)),
        out_shape=jax.ShapeDtypeStruct((L, D), F32),
        compiler_params=_cparams(("arbitrary",)),
    )(*ys, w, x)


def out_proj_loss(y, w, x, gf, tgt, name):
    L, K = y.shape
    D = w.shape[1]
    tl = min(TL_PROJ, L)

    def body(y_ref, w_ref, x_ref, gf_ref, t_ref, dx_ref, loss_ref, dg_ref):
        @pl.when(pl.program_id(0) == 0)
        def _():
            loss_ref[...] = jnp.zeros_like(loss_ref)
            dg_ref[...] = jnp.zeros_like(dg_ref)

        x2 = x_ref[...] + _mm(y_ref[...], w_ref[...])
        r = _rms(x2)
        xn = x2 * r
        e = xn * gf_ref[...] - t_ref[...]
        loss_ref[...] += (0.5 / D) * jnp.sum(e * e)
        dout = e * (1.0 / D)
        dg_ref[...] += jnp.sum(dout * xn, axis=0, keepdims=True)
        dxn = dout * gf_ref[...]
        dx_ref[...] = r * (dxn - xn * jnp.mean(dxn * xn, axis=-1, keepdims=True))

    return pl.pallas_call(
        body, name=name, grid=(L // tl,),
        in_specs=[pl.BlockSpec((tl, K), lambda i: (i, 0)), _full((K, D)),
                  pl.BlockSpec((tl, D), lambda i: (i, 0)), _full((1, D)),
                  pl.BlockSpec((tl, D), lambda i: (i, 0))],
        out_specs=[pl.BlockSpec((tl, D), lambda i: (i, 0)), _full((8, 128)), _full((1, D))],
        out_shape=[jax.ShapeDtypeStruct((L, D), F32), jax.ShapeDtypeStruct((8, 128), F32),
                   jax.ShapeDtypeStruct((1, D), F32)],
        compiler_params=_cparams(("arbitrary",)),
    )(y, w, x, gf, tgt)


def out_proj_bwd(dx, w, ys, name):
    L, D = dx.shape
    K = w.shape[0]
    tl = min(TL_PROJ, L)
    n = len(ys)
    offs = np.cumsum([0] + [y.shape[1] for y in ys])

    def body(*refs):
        dx_ref, w_ref, y_refs = refs[0], refs[1], refs[2:2 + n]
        dy_refs, dw_ref = refs[2 + n:2 + 2 * n], refs[2 + 2 * n]

        @pl.when(pl.program_id(0) == 0)
        def _():
            dw_ref[...] = jnp.zeros_like(dw_ref)

        dxv = dx_ref[...]
        for k in range(n):
            dy_refs[k][...] = _mm_nt(dxv, w_ref[offs[k]:offs[k + 1], :])
            dw_ref[offs[k]:offs[k + 1], :] += _mm_tn(y_refs[k][...], dxv)

    y_specs = [pl.BlockSpec((tl, y.shape[1]), lambda i: (i, 0)) for y in ys]
    return pl.pallas_call(
        body, name=name, grid=(L // tl,),
        in_specs=[pl.BlockSpec((tl, D), lambda i: (i, 0)), _full((K, D))] + y_specs,
        out_specs=y_specs + [_full((K, D))],
        out_shape=[jax.ShapeDtypeStruct(y.shape, F32) for y in ys] + [jax.ShapeDtypeStruct((K, D), F32)],
        compiler_params=_cparams(("arbitrary",)),
    )(dx, w, *ys)


def in_proj_bwd_dx(x, g, dps, ws, dres, name, plan=None):
    L, D = x.shape
    tl = min(TL_PROJ, L)
    n = len(dps)

    def body(*refs):
        x_ref, g_ref, dres_ref = refs[:3]
        dp_refs, w_refs = refs[3:3 + n], refs[3 + n:3 + 2 * n]
        dx_ref, dg_ref = refs[3 + 2 * n:]

        @pl.when(pl.program_id(0) == 0)
        def _():
            dg_ref[...] = jnp.zeros_like(dg_ref)

        dh = None
        for dp_ref, w_ref, w in zip(dp_refs, w_refs, ws):
            if w.ndim == 3:
                tn = w.shape[2]
                parts = [_mm_nt(dp_ref[:, tn * k:tn * (k + 1)], w_ref[k]) for k in range(w.shape[0])]
            else:
                parts = [_mm_nt(dp_ref[...], w_ref[...])]
            for part in parts:
                dh = part if dh is None else dh + part
        xv = x_ref[...]
        r = _rms(xv)
        xn = xv * r
        dg_ref[...] += jnp.sum(dh * xn, axis=0, keepdims=True)
        dxn = dh * g_ref[...]
        dx_ref[...] = dres_ref[...] + r * (dxn - xn * jnp.mean(dxn * xn, axis=-1, keepdims=True))

    return _call(
        body, plan, name=name, grid=(L // tl,),
        in_specs=[pl.BlockSpec((tl, D), lambda i: (i, 0)), _full((1, D)), pl.BlockSpec((tl, D), lambda i: (i, 0))]
        + [pl.BlockSpec((tl, dp.shape[1]), lambda i: (i, 0)) for dp in dps] + [_full(w.shape) for w in ws],
        out_specs=[pl.BlockSpec((tl, D), lambda i: (i, 0)), _full((1, D))],
        out_shape=[jax.ShapeDtypeStruct((L, D), F32), jax.ShapeDtypeStruct((1, D), F32)],
        sem=("arbitrary",),
    )(x, g, dres, *dps, *ws)


def in_proj_bwd_dw(h, dp, name, tn, first=0, into=None, dtype=F32, plan=None):
    L, D = h.shape
    tl = min(TL_DW, L)
    wb = EVEN_IN // N_CHIPS
    per = wb // tn
    count = dp.shape[1] // tn
    last = L // tl - 1

    def body(*refs):
        h_ref, dp_ref, dw_ref, acc = refs[0], refs[1], refs[-2], refs[-1]

        @pl.when(pl.program_id(1) == 0)
        def _():
            acc[...] = jnp.zeros_like(acc)

        acc[...] += _mm_tn(h_ref[...], dp_ref[...])

        @pl.when(pl.program_id(1) == last)
        def _():
            dw_ref[0] = acc[...].astype(dw_ref.dtype)

    ins = [h, dp] + ([] if into is None else [into])
    return _call(
        body, plan, name=name, grid=(count, L // tl),
        in_specs=[pl.BlockSpec((tl, D), lambda n, i: (i, 0)), pl.BlockSpec((tl, tn), lambda n, i: (i, n))]
        + ([] if into is None else [ANY]),
        out_specs=pl.BlockSpec((1, D, tn), lambda n, i: ((n + first) // per, 0, (n + first) % per)),
        out_shape=jax.ShapeDtypeStruct((N_CHIPS, D, wb), dtype),
        scratch_shapes=[pltpu.VMEM((D, tn), F32)],
        aliases={} if into is None else {2: 0},
        sem=("arbitrary", "arbitrary"),
    )(*ins)


def _s5_param_fn(lam_re, lam_im, log_dt, b_re, b_im):
    lr = jnp.minimum(lam_re, -1e-4)
    li = lam_im
    dt = jnp.exp(log_dt)
    mag = jnp.exp(lr * dt)
    ab_re = mag * jnp.cos(li * dt)
    ab_im = mag * jnp.sin(li * dt)
    den = lr * lr + li * li
    n_re = ab_re - 1.0
    n_im = ab_im
    z_re = (n_re * lr + n_im * li) / den
    z_im = (n_im * lr - n_re * li) / den
    bb_re = z_re[None] * b_re - z_im[None] * b_im
    bb_im = z_re[None] * b_im + z_im[None] * b_re
    return ab_re, ab_im, bb_re, bb_im


def s5_params_fwd(lam_re, lam_im, log_dt, b_re, b_im, span):
    G, P = lam_re.shape
    H = b_re.shape[0]
    assert span & (span - 1) == 0

    def body(lr_ref, li_ref, dt_ref, br_ref, bi_ref, abr_ref, abi_ref, bbr_ref, bbi_ref, pr_ref, pi_ref):
        ab_re, ab_im, bb_re, bb_im = _s5_param_fn(lr_ref[...], li_ref[...], dt_ref[...], br_ref[...], bi_ref[...])
        abr_ref[...] = ab_re
        abi_ref[...] = ab_im
        bbr_ref[...] = bb_re
        bbi_ref[...] = bb_im
        cr, ci = ab_re, ab_im
        for _ in range(span.bit_length() - 1):
            cr, ci = cr * cr - ci * ci, 2.0 * cr * ci
        pr_ref[...] = cr
        pi_ref[...] = ci

    shp = lambda *s: jax.ShapeDtypeStruct(s, F32)
    return pl.pallas_call(
        body, name="s5_params_fwd",
        out_shape=[shp(G, P), shp(G, P), shp(H, G, P), shp(H, G, P), shp(G, P), shp(G, P)],
    )(lam_re, lam_im, log_dt, b_re, b_im)


def s5_params_bwd(lam_re, lam_im, log_dt, b_re, b_im, d_ab_re, d_ab_im, d_bb_re, d_bb_im):
    G, P = lam_re.shape
    H = b_re.shape[0]

    def body(lr_ref, li_ref, dt_ref, br_ref, bi_ref, g0, g1, g2, g3, o0, o1, o2, o3, o4):
        prim = (lr_ref[...], li_ref[...], dt_ref[...], br_ref[...], bi_ref[...])
        _, vjp = jax.vjp(_s5_param_fn, *prim)
        d = vjp((jnp.sum(g0[...], axis=0), jnp.sum(g1[...], axis=0), g2[...], g3[...]))
        o0[...], o1[...], o2[...], o3[...], o4[...] = d

    shp = lambda *s: jax.ShapeDtypeStruct(s, F32)
    return pl.pallas_call(
        body, name="s5_params_bwd",
        out_shape=[shp(G, P), shp(G, P), shp(G, 1), shp(H, G, P), shp(H, G, P)],
    )(lam_re, lam_im, log_dt, b_re, b_im, d_ab_re, d_ab_im, d_bb_re, d_bb_im)


def stream_order(a, tl):
    L, C = a.shape
    return a.reshape(L // tl, 8, tl // 8, C).transpose(0, 2, 1, 3).reshape(L, C)


def token_order(a, tl):
    L, C = a.shape
    return a.reshape(L // tl, tl // 8, 8, C).transpose(0, 2, 1, 3).reshape(L, C)


_LANE_BLK = 1024
_LANE_BLK_BWD = 1024


def _cmul_add(ar, ai, xr, xi, br, bi):
    return br + (ar * xr - ai * xi), bi + (ar * xi + ai * xr)


def _cmulc_add(ar, ai, xr, xi, br, bi):
    return br + (ar * xr + ai * xi), bi + (ar * xi - ai * xr)


def _s5_states(u, wbd_ref, a_re, a_im, at_re, at_im, s_re, s_im, e_re, e_im, c0_re, c0_im, tl):
    t8 = tl // 8
    for k in range(S5_KBLK):
        bu = _mm(u[:, 128 * k:128 * (k + 1)], wbd_ref[k])
        s_re[:, 512 * k:512 * (k + 1)] = bu[:, :512]
        s_im[:, 512 * k:512 * (k + 1)] = bu[:, 512:]
    outs_re, outs_im = [], []
    for b in range(S5_LANES // _LANE_BLK):
        lanes = slice(_LANE_BLK * b, _LANE_BLK * (b + 1))
        ar = jnp.broadcast_to(a_re[:, lanes], (8, _LANE_BLK))
        ai = jnp.broadcast_to(a_im[:, lanes], (8, _LANE_BLK))

        def local(i, carry, lanes=lanes, ar=ar, ai=ai):
            r = pl.multiple_of(i * 8, 8)
            sr, si = _cmul_add(ar, ai, carry[0], carry[1], s_re[pl.ds(r, 8), lanes], s_im[pl.ds(r, 8), lanes])
            s_re[pl.ds(r, 8), lanes] = sr
            s_im[pl.ds(r, 8), lanes] = si
            return sr, si

        zero = jnp.zeros((8, _LANE_BLK), F32)
        fr, fi = lax.fori_loop(0, t8, local, (zero, zero), unroll=True)
        tr, ti = at_re[:, lanes], at_im[:, lanes]
        er, ei = c0_re[:, lanes], c0_im[:, lanes]
        ers, eis = [er], [ei]
        for j in range(8):
            er, ei = _cmul_add(tr, ti, er, ei, fr[j:j + 1], fi[j:j + 1])
            ers.append(er)
            eis.append(ei)
        outs_re.append(ers[8])
        outs_im.append(eis[8])
        ent_r, ent_i = jnp.concatenate(ers[:8], axis=0), jnp.concatenate(eis[:8], axis=0)
        e_re[:, lanes] = ent_r
        e_im[:, lanes] = ent_i

        def fix(i, carry, lanes=lanes, ar=ar, ai=ai):
            r = pl.multiple_of(i * 8, 8)
            zr, zi = ar * carry[0] - ai * carry[1], ar * carry[1] + ai * carry[0]
            s_re[pl.ds(r, 8), lanes] = s_re[pl.ds(r, 8), lanes] + zr
            s_im[pl.ds(r, 8), lanes] = s_im[pl.ds(r, 8), lanes] + zi
            return zr, zi

        lax.fori_loop(0, t8, fix, (ent_r, ent_i), unroll=True)
    return jnp.concatenate(outs_re, axis=1), jnp.concatenate(outs_im, axis=1)


def _s5_readout(s_re, s_im, cre_ref, cim_ref):
    ys = []
    for k in range(S5_KBLK):
        lanes = slice(512 * k, 512 * (k + 1))
        ys.append(_mm(s_re[:, lanes], cre_ref[k]) - _mm(s_im[:, lanes], cim_ref[k]))
    return jnp.concatenate(ys, axis=1)


def s5_forward(p, wbd, cre, cim, atab, d_skip, w_glu, b_glu, plan=None):
    L = p.shape[0]
    tl = min(TL_S5, L)
    nch = L // tl

    def body(u_ref, z_ref, wbd_ref, cre_ref, cim_ref, at_ref, d_ref, wg_ref, bg_ref,
             ya_ref, st_re_ref, st_im_ref, sv_re_ref, sv_im_ref, s_re, s_im, e_re, e_im, car_re, car_im):
        @pl.when(pl.program_id(0) == 0)
        def _():
            car_re[...] = jnp.zeros_like(car_re)
            car_im[...] = jnp.zeros_like(car_im)

        c0_re, c0_im = car_re[...], car_im[...]
        st_re_ref[0] = c0_re
        st_im_ref[0] = c0_im
        u = u_ref[...]
        x_re, x_im = _s5_states(u, wbd_ref, at_ref[0:1], at_ref[1:2], at_ref[2:3], at_ref[3:4],
                                s_re, s_im, e_re, e_im, c0_re, c0_im, tl)
        car_re[...] = x_re
        car_im[...] = x_im
        sv_re_ref[...] = s_re[...].astype(sv_re_ref.dtype)
        sv_im_ref[...] = s_im[...].astype(sv_im_ref.dtype)
        y = _s5_readout(sv_re_ref, sv_im_ref, cre_ref, cim_ref) + d_ref[...] * u
        yg = _gelu(y)
        gate = _sigmoid(_mm(yg, wg_ref[...]) + bg_ref[...])
        sz, _ = _silu_and_grad(z_ref[...])
        ya_ref[...] = (yg * gate * sz).astype(ya_ref.dtype)

    return _call(
        body, plan, name="s5_forward", grid=(nch,),
        in_specs=[pl.BlockSpec((tl, 1024), lambda i: (i, 0)), pl.BlockSpec((tl, 1024), lambda i: (i, 1)),
                  _full(wbd.shape), _full(cre.shape), _full(cim.shape), _full(atab.shape),
                  _full((1, 1024)), _full((1024, 1024)), _full((1, 1024))],
        out_specs=[pl.BlockSpec((tl, 1024), lambda i: (i, 0)),
                   pl.BlockSpec((1, 1, S5_LANES), lambda i: (i, 0, 0)),
                   pl.BlockSpec((1, 1, S5_LANES), lambda i: (i, 0, 0)),
                   pl.BlockSpec((tl, S5_LANES), lambda i: (i, 0)), pl.BlockSpec((tl, S5_LANES), lambda i: (i, 0))],
        out_shape=[jax.ShapeDtypeStruct((L, 1024), MXU_DTYPE),
                   jax.ShapeDtypeStruct((nch, 1, S5_LANES), F32), jax.ShapeDtypeStruct((nch, 1, S5_LANES), F32),
                   jax.ShapeDtypeStruct((L, S5_LANES), MXU_DTYPE), jax.ShapeDtypeStruct((L, S5_LANES), MXU_DTYPE)],
        scratch_shapes=[pltpu.VMEM((tl, S5_LANES), F32), pltpu.VMEM((tl, S5_LANES), F32),
                        pltpu.VMEM((8, S5_LANES), F32), pltpu.VMEM((8, S5_LANES), F32),
                        pltpu.VMEM((1, S5_LANES), F32), pltpu.VMEM((1, S5_LANES), F32)],
        sem=("arbitrary",),
    )(p, p, wbd, cre, cim, atab, d_skip, w_glu, b_glu)


def s5_backward(p, dya, st_re, st_im, sv_re, sv_im, wbd, cre, cim, atab, d_skip, w_glu, b_glu, plan=None):
    L = p.shape[0]
    tl = min(TL_S5, L)
    t8 = tl // 8
    nch = L // tl
    rev = lambda i: (nch - 1 - i, 0)
    rev1 = lambda i: (nch - 1 - i, 1)
    rev3 = lambda i: (nch - 1 - i, 0, 0)
    ct_shape = (S5_KBLK, cre.shape[2], cre.shape[1])

    def body(u_ref, z_ref, dya_ref, str_ref, sti_ref, s_re, s_im, wbd_ref, cre_ref, cim_ref, at_ref,
             d_ref, wg_ref, bg_ref,
             dp_ref, dwbd_ref, dcre_ref, dcim_ref, dabr_ref, dabi_ref, dd_ref, dwg_ref, dbg_ref,
             g_re, g_im, car_re, car_im):
        @pl.when(pl.program_id(0) == 0)
        def _():
            car_re[...] = jnp.zeros_like(car_re)
            car_im[...] = jnp.zeros_like(car_im)
            for r in (dwbd_ref, dcre_ref, dcim_ref, dabr_ref, dabi_ref, dd_ref, dwg_ref, dbg_ref):
                r[...] = jnp.zeros_like(r)

        u = u_ref[...]
        a_re, a_im, at_re, at_im = at_ref[0:1], at_ref[1:2], at_ref[2:3], at_ref[3:4]
        y = _s5_readout(s_re, s_im, cre_ref, cim_ref) + d_ref[...] * u
        yg, dyg = _gelu_and_grad(y)
        gate = _sigmoid(_mm(yg, wg_ref[...]) + bg_ref[...])
        sz, dsz = _silu_and_grad(z_ref[...])
        dya = dya_ref[...]
        s5out = yg * gate
        dp_ref[:, 1024:] = (dya * s5out * dsz).astype(dp_ref.dtype)
        ds5 = dya * sz
        dt = ds5 * yg * gate * (1.0 - gate)
        dwg_ref[...] += _mm_tn(yg, dt)
        dbg_ref[...] += jnp.sum(dt, axis=0, keepdims=True)
        dyv = (ds5 * gate + _mm_nt(dt, wg_ref[...])) * dyg
        dd_ref[...] += jnp.sum(dyv * u, axis=0, keepdims=True)

        for k in range(S5_KBLK):
            lanes = slice(512 * k, 512 * (k + 1))
            dyk = dyv[:, 128 * k:128 * (k + 1)]
            g_re[:, lanes] = _mm_nt(dyk, cre_ref[k])
            g_im[:, lanes] = -_mm_nt(dyk, cim_ref[k])
            dcre_ref[k] += _mm_tn(dyk, s_re[:, lanes])
            dcim_ref[k] -= _mm_tn(dyk, s_im[:, lanes])

        blk = _LANE_BLK_BWD
        for b in range(S5_LANES // blk):
            lanes = slice(blk * b, blk * (b + 1))
            ar = jnp.broadcast_to(a_re[:, lanes], (8, blk))
            ai = jnp.broadcast_to(a_im[:, lanes], (8, blk))

            def local(j, carry, lanes=lanes, ar=ar, ai=ai):
                r = pl.multiple_of((t8 - 1 - j) * 8, 8)
                gr, gi = _cmulc_add(ar, ai, carry[0], carry[1], g_re[pl.ds(r, 8), lanes], g_im[pl.ds(r, 8), lanes])
                g_re[pl.ds(r, 8), lanes] = gr
                g_im[pl.ds(r, 8), lanes] = gi
                return gr, gi

            zero = jnp.zeros((8, blk), F32)
            fr, fi = lax.fori_loop(0, t8, local, (zero, zero), unroll=True)
            tr, ti = at_re[:, lanes], at_im[:, lanes]
            hr, hi = car_re[:, lanes], car_im[:, lanes]
            hrs, his = [hr], [hi]
            for j in range(7, -1, -1):
                hr, hi = _cmulc_add(tr, ti, hr, hi, fr[j:j + 1], fi[j:j + 1])
                hrs.append(hr)
                his.append(hi)
            car_re[:, lanes] = hrs[8]
            car_im[:, lanes] = his[8]
            in_r = jnp.concatenate(hrs[7::-1], axis=0)
            in_i = jnp.concatenate(his[7::-1], axis=0)

            wr, wi, nr, ni, accr, acci = in_r, in_i, zero, zero, zero, zero
            for pair in range(t8 // 2 - 1, -1, -1):
                rows = slice(16 * pair, 16 * pair + 16)
                s16r, s16i = s_re[rows, lanes].astype(F32), s_im[rows, lanes].astype(F32)
                for half in (1, 0):
                    r = 16 * pair + 8 * half
                    sr, si = s16r[8 * half:8 * half + 8], s16i[8 * half:8 * half + 8]
                    accr, acci = accr + (sr * nr + si * ni), acci + (sr * ni - si * nr)
                    wr, wi = ar * wr + ai * wi, ar * wi - ai * wr
                    nr, ni = g_re[r:r + 8, lanes] + wr, g_im[r:r + 8, lanes] + wi
                    g_re[r:r + 8, lanes] = nr
                    g_im[r:r + 8, lanes] = ni
            lr, li = s_re[tl - 16:tl, lanes].astype(F32)[8:], s_im[tl - 16:tl, lanes].astype(F32)[8:]
            row0 = lax.broadcasted_iota(jnp.int32, (8, blk), 0) == 0
            sr = jnp.where(row0, jnp.broadcast_to(str_ref[0][:, lanes], (8, blk)), pltpu.roll(lr, 1, 0))
            si = jnp.where(row0, jnp.broadcast_to(sti_ref[0][:, lanes], (8, blk)), pltpu.roll(li, 1, 0))
            dabr_ref[:, lanes] += accr + (sr * nr + si * ni)
            dabi_ref[:, lanes] += acci + (sr * ni - si * nr)

        dus = []
        for k in range(S5_KBLK):
            lanes = slice(512 * k, 512 * (k + 1))
            g = jnp.concatenate([g_re[:, lanes], g_im[:, lanes]], axis=1)
            dwbd_ref[k] += _mm_tn(u[:, 128 * k:128 * (k + 1)], g)
            dus.append(_mm_nt(g, wbd_ref[k]))
        du = jnp.concatenate(dus, axis=1) + dyv * d_ref[...]
        dp_ref[:, :1024] = du.astype(dp_ref.dtype)

    shp = lambda *s: jax.ShapeDtypeStruct(s, F32)
    return _call(
        body, plan, name="s5_backward", grid=(nch,),
        in_specs=[pl.BlockSpec((tl, 1024), rev), pl.BlockSpec((tl, 1024), rev1), pl.BlockSpec((tl, 1024), rev),
                  pl.BlockSpec((1, 1, S5_LANES), rev3), pl.BlockSpec((1, 1, S5_LANES), rev3),
                  pl.BlockSpec((tl, S5_LANES), rev), pl.BlockSpec((tl, S5_LANES), rev),
                  _full(wbd.shape), _full(cre.shape), _full(cim.shape), _full(atab.shape),
                  _full((1, 1024)), _full((1024, 1024)), _full((1, 1024))],
        out_specs=[pl.BlockSpec((tl, 2048), rev), _full(wbd.shape), _full(ct_shape), _full(ct_shape),
                   _full((8, S5_LANES)), _full((8, S5_LANES)), _full((1, 1024)), _full((1024, 1024)), _full((1, 1024))],
        out_shape=[jax.ShapeDtypeStruct((L, 2048), MXU_DTYPE), shp(*wbd.shape), shp(*ct_shape), shp(*ct_shape),
                   shp(8, S5_LANES), shp(8, S5_LANES), shp(1, 1024), shp(1024, 1024), shp(1, 1024)],
        scratch_shapes=[pltpu.VMEM((tl, S5_LANES), F32), pltpu.VMEM((tl, S5_LANES), F32),
                        pltpu.VMEM((1, S5_LANES), F32), pltpu.VMEM((1, S5_LANES), F32)],
        sem=("arbitrary",),
    )(p, p, dya, st_re, st_im, sv_re, sv_im, wbd, cre, cim, atab, d_skip, w_glu, b_glu)


def _block_diag(w, rows_first):
    g8 = w.reshape(S5_KBLK, 8, w.shape[1], w.shape[2])
    eye = jnp.eye(8, dtype=w.dtype)
    out = jnp.einsum('kgab,fg->kfagb', g8, eye)
    return out.reshape(S5_KBLK, 8 * w.shape[1], 8 * w.shape[2])


def _block_diag_extract(wbd, a, b):
    w5 = wbd.reshape(S5_KBLK, 8, a, 8, b)
    idx = jnp.arange(8)
    return w5[:, idx, :, idx, :].transpose(1, 0, 2, 3).reshape(S5_GROUPS, a, b)


def _ret_constants():
    log_g = np.log1p(-np.exp2(-5.0 - np.arange(RET_HEADS, dtype=np.float32))).astype(np.float32)
    idx = np.arange(RET_CHUNK, dtype=np.float32)
    diff = idx[:, None] - idx[None, :]
    decay = np.where(diff >= 0, np.exp(log_g[:, None, None] * np.maximum(diff, 0.0)), 0.0).astype(np.float32)
    xi = np.exp(log_g[None, :] * (idx[:, None] + 1.0)).astype(np.float32)
    zeta = np.exp(log_g[None, :] * (RET_CHUNK - 1.0 - idx[:, None])).astype(np.float32)
    chunk_decay = np.exp(log_g * RET_CHUNK).astype(np.float32)
    return decay, xi, zeta, chunk_decay


def _rope_tables(L):
    half = RET_DK // 2
    inv = ROPE_BASE ** (-jnp.arange(half, dtype=F32) / half)
    ang = jnp.arange(L, dtype=F32)[:, None] * inv[None, :]
    return jnp.cos(ang), jnp.sin(ang)


def _rot(xh, cos, sin):
    x1, x2 = xh[:, :128], xh[:, 128:]
    return jnp.concatenate([x1 * cos - x2 * sin, x1 * sin + x2 * cos], axis=1)


def _rot_t(dh, cos, sin):
    d1, d2 = dh[:, :128], dh[:, 128:]
    return jnp.concatenate([d1 * cos + d2 * sin, d2 * cos - d1 * sin], axis=1)


RET_PER_STEP = 2


def _ret_setup(L):
    nc = L // RET_CHUNK
    per = RET_PER_STEP if nc % RET_PER_STEP == 0 else 1
    decay_np, xi_np, zeta_np, cd_np = _ret_constants()
    tables = (jnp.asarray(decay_np), jnp.asarray(np.tile(xi_np, (per, 1))), jnp.asarray(np.tile(zeta_np, (per, 1))))
    return nc // per, per, tables, [float(c) for c in cd_np]


def _ret_rows(q_ref, k_ref, v_ref, cos_ref, sin_ref, xi_ref, zeta_ref):
    H = range(RET_HEADS)
    hs = [slice(RET_DK * h, RET_DK * (h + 1)) for h in H]
    cs, sn = cos_ref[...], sin_ref[...]
    qh = [_rot(q_ref[:, hs[h]], cs, sn) for h in H]
    kh = [_rot(k_ref[:, hs[h]], cs, sn) * (RET_DK ** -0.5) for h in H]
    vh = [v_ref[:, hs[h]] for h in H]
    qx = [qh[h] * xi_ref[:, h:h + 1] for h in H]
    kz = [kh[h] * zeta_ref[:, h:h + 1] for h in H]
    return hs, cs, sn, qh, kh, vh, qx, kz


def _ret_normed(qh, kh, vh, qx, dec_ref, prevs, per):
    H, C = range(RET_HEADS), range(per)
    rs = [slice(RET_CHUNK * c, RET_CHUNK * (c + 1)) for c in C]
    sc = [[_mm_nt(qh[h][rs[c]], kh[h][rs[c]]) * dec_ref[h] for h in H] for c in C]
    inner = [[_mm(sc[c][h], vh[h][rs[c]]) for h in H] for c in C]
    cross = [[_mm(qx[h][rs[c]], prevs[c][h]) for h in H] for c in C]
    o = [jnp.concatenate([inner[c][h] + cross[c][h] for c in C], axis=0) for h in H]
    oc = [o[h] - jnp.mean(o[h], axis=-1, keepdims=True) for h in H]
    rstd = [lax.rsqrt(jnp.mean(oc[h] * oc[h], axis=-1, keepdims=True) + NORM_EPS) for h in H]
    on = [oc[h] * rstd[h] for h in H]
    return rs, sc, rstd, on


def retention_forward(p, cos, sin, gain):
    L = p.shape[0]
    steps, per, (decay, xi, zeta), cd = _ret_setup(L)
    rows = RET_CHUNK * per

    def body(q_ref, k_ref, v_ref, z_ref, cos_ref, sin_ref, dec_ref, xi_ref, zeta_ref, gain_ref,
             yb_ref, prev_ref, state):
        @pl.when(pl.program_id(0) == 0)
        def _():
            state[...] = jnp.zeros_like(state)

        H, C = range(RET_HEADS), range(per)
        hs, cs, sn, qh, kh, vh, qx, kz = _ret_rows(q_ref, k_ref, v_ref, cos_ref, sin_ref, xi_ref, zeta_ref)
        prevs = [[state[h] for h in H]]
        for c in C:
            rs_c = slice(RET_CHUNK * c, RET_CHUNK * (c + 1))
            prevs.append([prevs[c][h] * cd[h] + _mm_tn(kz[h][rs_c], vh[h][rs_c]) for h in H])
        _, _, _, on = _ret_normed(qh, kh, vh, qx, dec_ref, prevs, per)
        sz, _ = _silu_and_grad(z_ref[...])
        for h in H:
            for c in C:
                prev_ref[c, h] = prevs[c][h].astype(prev_ref.dtype)
            state[h] = prevs[per][h]
            yb_ref[:, hs[h]] = (on[h] * gain_ref[:, hs[h]] * sz[:, hs[h]]).astype(yb_ref.dtype)

    col0 = p.shape[1] // 1024 - 4
    blk = lambda c: pl.BlockSpec((rows, 1024), lambda i, c=c: (i, c + col0))
    return pl.pallas_call(
        body, name="retention_forward", grid=(steps,),
        in_specs=[blk(0), blk(1), blk(2), blk(3),
                  pl.BlockSpec((rows, 128), lambda i: (i, 0)), pl.BlockSpec((rows, 128), lambda i: (i, 0)),
                  _full(decay.shape), _full(xi.shape), _full(zeta.shape), _full((1, 1024))],
        out_specs=[pl.BlockSpec((rows, 1024), lambda i: (i, 0)),
                   pl.BlockSpec((per, RET_HEADS, RET_DK, RET_DK), lambda i: (i, 0, 0, 0))],
        out_shape=[jax.ShapeDtypeStruct((L, 1024), MXU_DTYPE),
                   jax.ShapeDtypeStruct((steps * per, RET_HEADS, RET_DK, RET_DK), MXU_DTYPE)],
        scratch_shapes=[pltpu.VMEM((RET_HEADS, RET_DK, RET_DK), F32)],
        compiler_params=_cparams(("arbitrary",)),
    )(p, p, p, p, cos, sin, decay, xi, zeta, gain)


def retention_backward(p, dy, prevs, cos, sin, gain, plan=None):
    L = p.shape[0]
    steps, per, (decay, xi, zeta), cd = _ret_setup(L)
    rows = RET_CHUNK * per
    scale = RET_DK ** -0.5

    def body(q_ref, k_ref, v_ref, z_ref, dyb_ref, prev_ref, cos_ref, sin_ref, dec_ref, xi_ref, zeta_ref, gain_ref,
             dp_ref, dgain_ref, dstate):
        @pl.when(pl.program_id(0) == 0)
        def _():
            dstate[...] = jnp.zeros_like(dstate)
            dgain_ref[...] = jnp.zeros_like(dgain_ref)

        H, C = range(RET_HEADS), range(per)
        hs, cs, sn, qh, kh, vh, qx, kz = _ret_rows(q_ref, k_ref, v_ref, cos_ref, sin_ref, xi_ref, zeta_ref)
        prevs = [[prev_ref[c, h] for h in H] for c in C]
        rs, sc, rstd, on = _ret_normed(qh, kh, vh, qx, dec_ref, prevs, per)
        sz, dsz = _silu_and_grad(z_ref[...])
        dyb = dyb_ref[...]
        dong = [dyb[:, hs[h]] * sz[:, hs[h]] for h in H]
        don = [dong[h] * gain_ref[:, hs[h]] for h in H]
        do = [rstd[h] * (don[h] - jnp.mean(don[h], axis=-1, keepdims=True)
                         - on[h] * jnp.mean(don[h] * on[h], axis=-1, keepdims=True)) for h in H]
        dsc = [[_mm_nt(do[h][rs[c]], vh[h][rs[c]]) * dec_ref[h] for h in H] for c in C]
        dq_st = [[_mm_nt(do[h][rs[c]], prevs[c][h]) for h in H] for c in C]
        dnew = [[_mm_tn(qx[h][rs[c]], do[h][rs[c]]) for h in H] for c in C]
        dsts = [None] * per + [[dstate[h] for h in H]]
        for c in reversed(C):
            dsts[c] = [dsts[c + 1][h] * cd[h] + dnew[c][h] for h in H]
        dk_st = [[_mm_nt(vh[h][rs[c]], dsts[c + 1][h]) for h in H] for c in C]
        dv_st = [[_mm(kz[h][rs[c]], dsts[c + 1][h]) for h in H] for c in C]
        rows_of = lambda parts: jnp.concatenate(parts, axis=0)
        dqh = [rows_of([_mm(dsc[c][h], kh[h][rs[c]]) for c in C])
               + rows_of([dq_st[c][h] for c in C]) * xi_ref[:, h:h + 1] for h in H]
        dkh = [rows_of([_mm_tn(dsc[c][h], qh[h][rs[c]]) for c in C])
               + rows_of([dk_st[c][h] for c in C]) * zeta_ref[:, h:h + 1] for h in H]
        dvh = [rows_of([_mm_tn(sc[c][h], do[h][rs[c]]) + dv_st[c][h] for c in C]) for h in H]
        for h in H:
            dstate[h] = dsts[0][h]
            dgain_ref[:, hs[h]] += jnp.sum(dong[h] * on[h], axis=0, keepdims=True)
            dp_ref[:, hs[h]] = _rot_t(dqh[h], cs, sn).astype(dp_ref.dtype)
            dp_ref[:, 1024 + RET_DK * h:1024 + RET_DK * (h + 1)] = (_rot_t(dkh[h], cs, sn) * scale).astype(dp_ref.dtype)
            dp_ref[:, 2048 + RET_DK * h:2048 + RET_DK * (h + 1)] = dvh[h].astype(dp_ref.dtype)
            dp_ref[:, 3072 + RET_DK * h:3072 + RET_DK * (h + 1)] = (
                dyb[:, hs[h]] * on[h] * gain_ref[:, hs[h]] * dsz[:, hs[h]]).astype(dp_ref.dtype)

    col0 = p.shape[1] // 1024 - 4
    blk = lambda c: pl.BlockSpec((rows, 1024), lambda i, c=c: (steps - 1 - i, c + col0))
    tab = pl.BlockSpec((rows, 128), lambda i: (steps - 1 - i, 0))
    return _call(
        body, plan, name="retention_backward", grid=(steps,),
        in_specs=[blk(0), blk(1), blk(2), blk(3), pl.BlockSpec((rows, 1024), lambda i: (steps - 1 - i, 0)),
                  pl.BlockSpec((per, RET_HEADS, RET_DK, RET_DK), lambda i: (steps - 1 - i, 0, 0, 0)),
                  tab, tab, _full(decay.shape), _full(xi.shape), _full(zeta.shape), _full((1, 1024))],
        out_specs=[pl.BlockSpec((rows, 4096), lambda i: (steps - 1 - i, 0)), _full((1, 1024))],
        out_shape=[jax.ShapeDtypeStruct((L, 4096), MXU_DTYPE), jax.ShapeDtypeStruct((1, 1024), F32)],
        scratch_shapes=[pltpu.VMEM((RET_HEADS, RET_DK, RET_DK), F32)],
        sem=("arbitrary",),
    )(p, p, p, p, dy, prevs, cos, sin, decay, xi, zeta, gain)


def _sgu_mix(p_ref, gain_ref, wm_ref, bt_ref, tl):
    pu, pv, z = p_ref[:, :2048], p_ref[:, 2048:4096], p_ref[:, 4096:]
    (u, du), (v, dv) = _gelu_and_grad(pu), _gelu_and_grad(pv)
    mu = jnp.mean(v, axis=-1, keepdims=True)
    vc = v - mu
    rstd = lax.rsqrt(jnp.mean(vc * vc, axis=-1, keepdims=True) + NORM_EPS)
    vn = vc * rstd
    vg = vn * gain_ref[...]
    mask = (lax.broadcasted_iota(jnp.int32, (SGU_CHUNK, SGU_CHUNK), 0)
            >= lax.broadcasted_iota(jnp.int32, (SGU_CHUNK, SGU_CHUNK), 1))
    wms = [jnp.where(mask, wm_ref[g], 0.0) for g in range(SGU_GROUPS)]
    rows = []
    for c in range(tl // SGU_CHUNK):
        rs = slice(SGU_CHUNK * c, SGU_CHUNK * (c + 1))
        cols = []
        for g in range(SGU_GROUPS):
            gs = slice(SGU_GDIM * g, SGU_GDIM * (g + 1))
            cols.append(_mm(wms[g], vg[rs, gs]) + bt_ref[:, g:g + 1])
        rows.append(jnp.concatenate(cols, axis=1))
    s = rows[0] if len(rows) == 1 else jnp.concatenate(rows, axis=0)
    return du, dv, z, u, vn, rstd, vg, wms, mask, s


def sgu_forward(p, gain, wm, bt):
    L = p.shape[0]
    tl = min(TL_SGU, L)

    def body(p_ref, gain_ref, wm_ref, bt_ref, y_ref):
        _, _, z, u, _, _, _, _, _, s = _sgu_mix(p_ref, gain_ref, wm_ref, bt_ref, tl)
        sz, _ = _silu_and_grad(z)
        y_ref[...] = (u * s * sz).astype(y_ref.dtype)

    return pl.pallas_call(
        body, name="sgu_forward", grid=(L // tl,),
        in_specs=[pl.BlockSpec((tl, ODD_IN), lambda i: (i, 0)), _full((1, 2048)), _full(wm.shape), _full(bt.shape)],
        out_specs=pl.BlockSpec((tl, 2048), lambda i: (i, 0)),
        out_shape=jax.ShapeDtypeStruct((L, 2048), MXU_DTYPE),
        compiler_params=_cparams(("arbitrary",)),
    )(p, gain, wm, bt)


def sgu_backward(p, dy, gain, wm, bt, plan=None):
    L = p.shape[0]
    tl = min(TL_SGU, L)

    def body(p_ref, dy_ref, gain_ref, wm_ref, bt_ref, dp_ref, dgain_ref, dwm_ref, dbt_ref):
        @pl.when(pl.program_id(0) == 0)
        def _():
            dgain_ref[...] = jnp.zeros_like(dgain_ref)
            dwm_ref[...] = jnp.zeros_like(dwm_ref)
            dbt_ref[...] = jnp.zeros_like(dbt_ref)

        gu, gv, z, u, vn, rstd, vg, wms, mask, s = _sgu_mix(p_ref, gain_ref, wm_ref, bt_ref, tl)
        sz, dsz = _silu_and_grad(z)
        dyv = dy_ref[...]
        dp_ref[:, 4096:] = (dyv * u * s * dsz).astype(dp_ref.dtype)
        dsg = dyv * sz
        dp_ref[:, :2048] = (dsg * s * gu).astype(dp_ref.dtype)
        ds = dsg * u
        rows = []
        dbs = [jnp.zeros((SGU_CHUNK, 1), F32) for _ in range(SGU_GROUPS)]
        for c in range(tl // SGU_CHUNK):
            rs = slice(SGU_CHUNK * c, SGU_CHUNK * (c + 1))
            cols = []
            for g in range(SGU_GROUPS):
                gs = slice(SGU_GDIM * g, SGU_GDIM * (g + 1))
                dsg_c = ds[rs, gs]
                dbs[g] = dbs[g] + jnp.sum(dsg_c, axis=1, keepdims=True)
                dwm_ref[g] += jnp.where(mask, _mm_nt(dsg_c, vg[rs, gs]), 0.0)
                cols.append(_mm_tn(wms[g], dsg_c))
            rows.append(jnp.concatenate(cols, axis=1))
        dbt_ref[...] += jnp.concatenate(dbs, axis=1)
        dvg = rows[0] if len(rows) == 1 else jnp.concatenate(rows, axis=0)
        dgain_ref[...] += jnp.sum(dvg * vn, axis=0, keepdims=True)
        dvn = dvg * gain_ref[...]
        dv = rstd * (dvn - jnp.mean(dvn, axis=-1, keepdims=True) - vn * jnp.mean(dvn * vn, axis=-1, keepdims=True))
        dp_ref[:, 2048:4096] = (dv * gv).astype(dp_ref.dtype)

    return _call(
        body, plan, name="sgu_backward", grid=(L // tl,),
        in_specs=[pl.BlockSpec((tl, ODD_IN), lambda i: (i, 0)), pl.BlockSpec((tl, 2048), lambda i: (i, 0)),
                  _full((1, 2048)), _full(wm.shape), _full(bt.shape)],
        out_specs=[pl.BlockSpec((tl, ODD_IN), lambda i: (i, 0)), _full((1, 2048)), _full(wm.shape), _full(bt.shape)],
        out_shape=[jax.ShapeDtypeStruct((L, ODD_IN), MXU_DTYPE), jax.ShapeDtypeStruct((1, 2048), F32),
                   jax.ShapeDtypeStruct(wm.shape, F32), jax.ShapeDtypeStruct(bt.shape, F32)],
        sem=("arbitrary",),
    )(p, dy, gain, wm, bt)


def cast_shards(mats):
    n = len(mats)
    steps = 8

    def body(*refs):
        for p in range(n):
            refs[n + p][...] = refs[p][...].astype(MXU_DTYPE)

    specs = [pl.BlockSpec((m.shape[0] // steps, m.shape[1]), lambda i: (i, 0)) for m in mats]
    return pl.pallas_call(
        body, name="cast_shards", grid=(steps,), in_specs=specs, out_specs=specs,
        out_shape=[jax.ShapeDtypeStruct(m.shape, MXU_DTYPE) for m in mats],
        compiler_params=_cparams(("arbitrary",)),
    )(*mats)


def local_grads(x, tgt, w):
    L = x.shape[0]
    ne, gf = w["norm_even"], w["final_norm"].reshape(1, D_MODEL)
    sh = dict(zip(MATRICES, cast_shards([w[n][0] for n in MATRICES])))
    lam_re, lam_im = w["s5_lam_re"][0], w["s5_lam_im"][0]
    log_dt = w["s5_log_dt"].reshape(S5_GROUPS, 1)
    bt_re = jnp.transpose(w["s5_b_re"][0], (2, 0, 1))
    bt_im = jnp.transpose(w["s5_b_im"][0], (2, 0, 1))
    c_re, c_im = w["s5_c_re"][0], w["s5_c_im"][0]
    wm = w["sgu_w_spatial"][0]
    bt = jnp.transpose(w["sgu_b_spatial"][0])

    tl5 = min(TL_S5, L)
    ab_re, ab_im, bb_re, bb_im, at_re, at_im = s5_params_fwd(lam_re, lam_im, log_dt, bt_re, bt_im, tl5 // 8)
    atab = jnp.stack([ab_re.reshape(S5_LANES), ab_im.reshape(S5_LANES),
                      at_re.reshape(S5_LANES), at_im.reshape(S5_LANES)])
    wbd = jnp.concatenate([_block_diag(jnp.transpose(bb_re, (1, 0, 2)), True),
                           _block_diag(jnp.transpose(bb_im, (1, 0, 2)), True)], axis=2).astype(MXU_DTYPE)
    cre = _block_diag(jnp.transpose(c_re, (0, 2, 1)), True).astype(MXU_DTYPE)
    cim = _block_diag(jnp.transpose(c_im, (0, 2, 1)), True).astype(MXU_DTYPE)
    cos, sin = _rope_tables(L)

    s5_cols = 2 * S5_WIDTH
    me = (2 * lax.axis_index("x") + lax.axis_index("y")).astype(jnp.int32)
    xs = stream_order(x, tl5)
    slab = lambda d: jnp.stack([me ^ d])
    zero = jnp.zeros((1,), jnp.int32)
    shards = [sh["w_in_even"][None]]
    (p1, h0s, h0), (got,) = even_in_slabs(x, xs, ne, shards[0], slab(0), zero, "even_in_0",
                                          plan=gather_plan([sh["w_in_even"]], only=1))
    for d in (1, 2, 3):
        shards.append(got)
        plan = gather_plan([sh["w_in_even"]], only=d + 1) if d < 3 else gather_plan([sh["s5_w_glu"]])
        (p1,), (got,) = even_in_slabs(x, xs, ne, shards[d], slab(d), zero, "even_in_%d" % d, p_in=p1, plan=plan)
    w_glu = got
    w_s5, w_ret = split_w_in_even(shards, slab(0))
    w_glu = w_glu.reshape(S5_WIDTH, S5_WIDTH)
    (ya, st_re, st_im, sv_re, sv_im), (w_out_e, w_in_o, w_out_o, no, sg_gain) = s5_forward(
        p1, wbd, cre, cim, atab, w["s5_d"], w_glu, w["s5_b_glu"],
        gather_plan([sh["w_out_even"], sh["w_in_odd"], sh["w_out_odd"], w["norm_odd"], w["sgu_norm_gain"]]))
    w_out_e = w_out_e.reshape(2 * S5_WIDTH, D_MODEL)
    w_out_o = w_out_o.reshape(SGU_WIDTH, D_MODEL)
    no, sg_gain = no.reshape(1, D_MODEL), sg_gain.reshape(1, SGU_WIDTH)
    yb, prevs = retention_forward(p1, cos, sin, w["ret_gn_gain"])
    ya = token_order(ya, tl5)
    x1 = matmul_residual([ya, yb], w_out_e, x, "even_out")
    (p2, h1), _ = norm_matmul(x1, no, w_in_o, "odd_in")
    y2 = sgu_forward(p2, sg_gain, wm, bt)
    dx2, loss, dgf = out_proj_loss(y2, w_out_o, x1, gf, tgt, "odd_out_loss")

    g, landed = {}, {}
    shard_major = lambda a, n: a.reshape((N_CHIPS,) + w[n].shape[1:])
    dy2, g_w_out_o = out_proj_bwd(dx2, w_out_o, [y2], "odd_out_bwd")
    (dp2, g["sgu_norm_gain"], dwm, dbt), (landed["w_out_odd"],) = sgu_backward(
        p2, dy2, sg_gain, wm, bt, reduce_plan([shard_major(g_w_out_o, "w_out_odd")]))
    g_w_in_o, _ = in_proj_bwd_dw(h1, dp2, "odd_in_dw", ODD_IN // N_CHIPS)
    (dx1, g["norm_odd"]), _ = in_proj_bwd_dx(x1, no, [dp2], [w_in_o], dx2, "odd_in_dx")
    dya, dyb, g_w_out_e = out_proj_bwd(dx1, w_out_e, [ya, yb], "even_out_bwd")
    ((dpa, dwbd, dcre, dcim, dab_re, dab_im, g["s5_d"], g_w_glu, g["s5_b_glu"]),
     (landed["w_in_odd"], landed["w_out_even"])) = s5_backward(
        p1, stream_order(dya, tl5), st_re, st_im, sv_re, sv_im, wbd, cre, cim, atab, w["s5_d"], w_glu,
        w["s5_b_glu"], reduce_plan([g_w_in_o, shard_major(g_w_out_e, "w_out_even")]))

    dbb_re = jnp.transpose(_block_diag_extract(dwbd[:, :, :512], S5_GROUP, S5_STATE), (1, 0, 2))
    dbb_im = jnp.transpose(_block_diag_extract(dwbd[:, :, 512:], S5_GROUP, S5_STATE), (1, 0, 2))
    dlr, dli, ddt, dbt_re, dbt_im = s5_params_bwd(
        lam_re, lam_im, log_dt, bt_re, bt_im, dab_re.reshape(8, S5_GROUPS, S5_STATE),
        dab_im.reshape(8, S5_GROUPS, S5_STATE), dbb_re, dbb_im)
    g["s5_lam_re"], g["s5_lam_im"] = dlr[None], dli[None]
    g["s5_log_dt"] = ddt.reshape(1, S5_GROUPS)
    g["s5_b_re"], g["s5_b_im"] = dbt_re, dbt_im
    g["s5_c_re"] = _block_diag_extract(dcre, S5_GROUP, S5_STATE)[None]
    g["s5_c_im"] = _block_diag_extract(dcim, S5_GROUP, S5_STATE)[None]
    g["sgu_w_spatial"] = dwm[None]
    g["sgu_b_spatial"] = jnp.transpose(dbt)[None]
    g["final_norm"] = dgf.reshape(D_MODEL)
    g["loss"] = loss

    (dpb, g["ret_gn_gain"]), (landed["s5_w_glu"],) = retention_backward(
        p1, dyb, prevs, cos, sin, w["ret_gn_gain"], reduce_plan([shard_major(g_w_glu, "s5_w_glu")]))
    done = tuple(n for n in MATRICES if n != "w_in_even")
    part = {n: sum_slabs(landed[n], "sum_" + n) for n in done}
    g_w_in_e, recv = in_proj_bwd_dw(h0s, dpa, "even_in_dw_s5", 512, dtype=MXU_DTYPE,
                                    plan=_SiblingPlan([part[n] for n in done]))
    other = dict(zip(done, recv))
    small = tuple(n for n in SMALL if n != "norm_even") + ("loss",)
    g_w_in_e, recv = in_proj_bwd_dw(h0, dpb, "even_in_dw_ret", 512, first=s5_cols // 512, into=g_w_in_e,
                                    dtype=MXU_DTYPE, plan=reduce_plan([], [g[n] for n in small]))
    landed.update(zip(small, recv))
    (dx0, g["norm_even"]), (landed["w_in_even"],) = in_proj_bwd_dx(
        x, ne, [token_order(dpa, tl5), dpb], [w_s5, w_ret], dx1, "even_in_dx", reduce_plan([g_w_in_e]))
    (landed["norm_even"],) = run_plan(reduce_plan([], [g["norm_even"]]), "exchange_norm_even")
    return dx0, landed, part, other


def _row_block(rows):
    return 128 if rows % 128 == 0 else rows


def sum_slabs(r, name):
    _, R, C = r.shape
    tr = _row_block(R)

    def body(r_ref, o_ref):
        a, b, c, d = (r_ref[k].astype(F32) for k in range(N_CHIPS))
        o_ref[...] = (a + b) + (c + d)

    return pl.pallas_call(
        body, name=name, grid=(R // tr,),
        in_specs=[pl.BlockSpec((N_CHIPS, tr, C), lambda i: (0, i, 0))],
        out_specs=pl.BlockSpec((tr, C), lambda i: (i, 0)),
        out_shape=jax.ShapeDtypeStruct((R, C), F32),
        compiler_params=_cparams(("arbitrary",)),
    )(r)


def _adam(w, m, v, g):
    mn = ADAM_B1 * m + (1.0 - ADAM_B1) * g
    vn = ADAM_B2 * v + (1.0 - ADAM_B2) * (g * g)
    m_hat = mn / (1.0 - ADAM_B1 ** ADAM_STEP)
    v_hat = vn / (1.0 - ADAM_B2 ** ADAM_STEP)
    return -ADAM_LR * (m_hat / (jnp.sqrt(v_hat) + ADAM_EPS) + ADAM_WD * w), mn, vn


def adam_update(w, m, v, ga, gb, name):
    R, C = w.shape
    tr = _row_block(R)

    def body(w_ref, m_ref, v_ref, ga_ref, gb_ref, g_out, d_out, m_out, v_out):
        g = ga_ref[...] + gb_ref[...]
        g_out[...] = g
        d_out[...], m_out[...], v_out[...] = _adam(w_ref[...], m_ref[...], v_ref[...], g)

    blk = pl.BlockSpec((tr, C), lambda i: (i, 0))
    return pl.pallas_call(
        body, name=name, grid=(R // tr,),
        in_specs=[blk] * 5, out_specs=[blk] * 4,
        out_shape=[jax.ShapeDtypeStruct((R, C), F32)] * 4,
        compiler_params=_cparams(("arbitrary",)),
    )(w, m, v, ga, gb)


WIDE_ROWS = ("s5_b_re", "s5_b_im")


def sum_small(landed):
    def body(*refs):
        k = len(refs) // 2
        for i in range(k):
            r = refs[i]
            refs[k + i][...] = (r[0] + r[1]) + (r[2] + r[3])

    names = list(landed)
    res = pl.pallas_call(
        body, name="sum_small", out_shape=[jax.ShapeDtypeStruct(landed[n].shape[1:], F32) for n in names],
        compiler_params=pltpu.CompilerParams(vmem_limit_bytes=VMEM_LIMIT),
    )(*[landed[n] for n in names])
    return dict(zip(names, res))


def adam_small(names, w, m, v, ga, gb):
    def body(*refs):
        k = len(refs) // 9
        me = 2 * lax.axis_index("x") + lax.axis_index("y")
        for i in range(k):
            w_ref, m_ref, v_ref, ga_ref, gb_ref = refs[i], refs[k + i], refs[2 * k + i], refs[3 * k + i], refs[4 * k + i]
            size = w_ref.shape[-1]
            if ga_ref.shape != w_ref.shape:
                part = pl.ds(pl.multiple_of(me * size, LANES), size)
                g = ga_ref[:, part] + gb_ref[:, part]
            else:
                g = ga_ref[...] + gb_ref[...]
            refs[5 * k + i][...] = g
            refs[6 * k + i][...], refs[7 * k + i][...], refs[8 * k + i][...] = _adam(w_ref[...], m_ref[...], v_ref[...], g)

    ins = [d[n] for d in (w, m, v, ga, gb) for n in names]
    outs = [jax.ShapeDtypeStruct(w[n].shape, F32) for _ in range(4) for n in names]
    res = pl.pallas_call(body, name="adam_small", out_shape=outs,
                         compiler_params=pltpu.CompilerParams(vmem_limit_bytes=VMEM_LIMIT))(*ins)
    k = len(names)
    return [dict(zip(names, res[j * k:(j + 1) * k])) for j in range(4)]


WEIGHTS = ("norm_even", "w_in_even", "s5_lam_re", "s5_lam_im", "s5_log_dt", "s5_b_re", "s5_b_im", "s5_c_re",
           "s5_c_im", "s5_d", "s5_w_glu", "s5_b_glu", "ret_gn_gain", "w_out_even", "norm_odd", "w_in_odd",
           "sgu_norm_gain", "sgu_w_spatial", "sgu_b_spatial", "w_out_odd", "final_norm")
MATRICES = ("w_in_even", "s5_w_glu", "w_out_even", "w_in_odd", "w_out_odd")
SHARDED_VECS = ("norm_odd", "sgu_norm_gain")
REPLICATED = tuple(n for n in WEIGHTS if n not in MATRICES and n not in SHARDED_VECS)
SMALL = tuple(n for n in WEIGHTS if n not in MATRICES)
LANES = 128


def kernel(x, norm_even, w_in_even, s5_lam_re, s5_lam_im, s5_log_dt, s5_b_re, s5_b_im, s5_c_re, s5_c_im, s5_d, s5_w_glu, s5_b_glu, ret_gn_gain, w_out_even, norm_odd, w_in_odd, sgu_norm_gain, sgu_w_spatial, sgu_b_spatial, w_out_odd, final_norm, loss_target, m_norm_even, m_w_in_even, m_s5_lam_re, m_s5_lam_im, m_s5_log_dt, m_s5_b_re, m_s5_b_im, m_s5_c_re, m_s5_c_im, m_s5_d, m_s5_w_glu, m_s5_b_glu, m_ret_gn_gain, m_w_out_even, m_norm_odd, m_w_in_odd, m_sgu_norm_gain, m_sgu_w_spatial, m_sgu_b_spatial, m_w_out_odd, m_final_norm, v_norm_even, v_w_in_even, v_s5_lam_re, v_s5_lam_im, v_s5_log_dt, v_s5_b_re, v_s5_b_im, v_s5_c_re, v_s5_c_im, v_s5_d, v_s5_w_glu, v_s5_b_glu, v_ret_gn_gain, v_w_out_even, v_norm_odd, v_w_in_odd, v_sgu_norm_gain, v_sgu_w_spatial, v_sgu_b_spatial, v_w_out_odd, v_final_norm):
    w = dict(norm_even=norm_even, w_in_even=w_in_even, s5_lam_re=s5_lam_re, s5_lam_im=s5_lam_im, s5_log_dt=s5_log_dt, s5_b_re=s5_b_re, s5_b_im=s5_b_im, s5_c_re=s5_c_re, s5_c_im=s5_c_im, s5_d=s5_d, s5_w_glu=s5_w_glu, s5_b_glu=s5_b_glu, ret_gn_gain=ret_gn_gain, w_out_even=w_out_even, norm_odd=norm_odd, w_in_odd=w_in_odd, sgu_norm_gain=sgu_norm_gain, sgu_w_spatial=sgu_w_spatial, sgu_b_spatial=sgu_b_spatial, w_out_odd=w_out_odd, final_norm=final_norm)
    m = dict(norm_even=m_norm_even, w_in_even=m_w_in_even, s5_lam_re=m_s5_lam_re, s5_lam_im=m_s5_lam_im, s5_log_dt=m_s5_log_dt, s5_b_re=m_s5_b_re, s5_b_im=m_s5_b_im, s5_c_re=m_s5_c_re, s5_c_im=m_s5_c_im, s5_d=m_s5_d, s5_w_glu=m_s5_w_glu, s5_b_glu=m_s5_b_glu, ret_gn_gain=m_ret_gn_gain, w_out_even=m_w_out_even, norm_odd=m_norm_odd, w_in_odd=m_w_in_odd, sgu_norm_gain=m_sgu_norm_gain, sgu_w_spatial=m_sgu_w_spatial, sgu_b_spatial=m_sgu_b_spatial, w_out_odd=m_w_out_odd, final_norm=m_final_norm)
    v = dict(norm_even=v_norm_even, w_in_even=v_w_in_even, s5_lam_re=v_s5_lam_re, s5_lam_im=v_s5_lam_im, s5_log_dt=v_s5_log_dt, s5_b_re=v_s5_b_re, s5_b_im=v_s5_b_im, s5_c_re=v_s5_c_re, s5_c_im=v_s5_c_im, s5_d=v_s5_d, s5_w_glu=v_s5_w_glu, s5_b_glu=v_s5_b_glu, ret_gn_gain=v_ret_gn_gain, w_out_even=v_w_out_even, norm_odd=v_norm_odd, w_in_odd=v_w_in_odd, sgu_norm_gain=v_sgu_norm_gain, sgu_w_spatial=v_sgu_w_spatial, sgu_b_spatial=v_sgu_b_spatial, w_out_odd=v_w_out_odd, final_norm=v_final_norm)

    grad_x, landed, part, other = local_grads(x[0], loss_target[0], w)

    small = SMALL + ("loss",)
    part["w_in_even"] = sum_slabs(landed["w_in_even"], "sum_w_in_even")
    part.update(sum_small({n: landed[n] for n in small}))
    names = ("w_in_even",) + small
    other.update(zip(names, run_plan(_SiblingPlan([part[n] for n in names]), "sibling_exchange")))

    wt, mt, vt = dict(w), dict(m), dict(v)
    for n in WIDE_ROWS:
        wt[n], mt[n], vt[n] = (jnp.transpose(a[n][0], (2, 0, 1)) for a in (w, m, v))
    out_g, out_d, out_m, out_v = adam_small(SMALL, wt, mt, vt, part, other)
    for n in WIDE_ROWS:
        for out in (out_g, out_d, out_m, out_v):
            out[n] = jnp.transpose(out[n], (1, 2, 0))[None]
    for n in MATRICES:
        res = adam_update(w[n][0], m[n][0], v[n][0], part[n], other[n], "adam_" + n)
        out_g[n], out_d[n], out_m[n], out_v[n] = (r[None] for r in res)
    total_loss = (part["loss"] + other["loss"])[0, 0]

    return (total_loss, grad_x[None], *[out_g[n] for n in WEIGHTS], *[out_d[n] for n in WEIGHTS],
            *[out_m[n] for n in WEIGHTS], *[out_v[n] for n in WEIGHTS])
```

```python
import functools
import math

import numpy as np
import jax
import jax.numpy as jnp
from jax import lax
from jax.experimental import pallas as pl
from jax.experimental.pallas import tpu as pltpu

F32 = jnp.float32
MXU_DTYPE = jnp.bfloat16
NORM_EPS = 1e-6
D_MODEL = 1024
S5_WIDTH = 1024
S5_GROUP = 16
S5_GROUPS = 64
S5_STATE = 64
S5_LANES = S5_GROUPS * S5_STATE
S5_KBLK = 8
RET_HEADS = 4
RET_DK = 256
RET_CHUNK = 128
ROPE_BASE = 10000.0
SGU_WIDTH = 2048
SGU_GROUPS = 4
SGU_GDIM = 512
SGU_CHUNK = 128
EVEN_IN = 6144
ODD_IN = 6144
ADAM_LR = 0.001
ADAM_B1 = 0.9
ADAM_B2 = 0.999
ADAM_EPS = 1e-08
ADAM_WD = 0.01
ADAM_STEP = 10
N_CHIPS = 4
VMEM_LIMIT = 56 * 1024 * 1024

TL_PROJ = 512
TL_DW = 1024
TL_S5 = 256
TL_SGU = 256


def _cparams(sem, **kw):
    return pltpu.CompilerParams(dimension_semantics=sem, vmem_limit_bytes=VMEM_LIMIT, **kw)


def _mm(a, b):
    return jnp.dot(a.astype(MXU_DTYPE), b.astype(MXU_DTYPE), preferred_element_type=F32)


def _mm_nt(a, b):
    return lax.dot_general(a.astype(MXU_DTYPE), b.astype(MXU_DTYPE),
                           (((1,), (1,)), ((), ())), preferred_element_type=F32)


def _mm_tn(a, b):
    return lax.dot_general(a.astype(MXU_DTYPE), b.astype(MXU_DTYPE),
                           (((0,), (0,)), ((), ())), preferred_element_type=F32)


_GELU_C = math.sqrt(2.0 / math.pi)


def _gelu_parts(x):
    x2 = x * x
    th = jnp.tanh(x * (_GELU_C + (_GELU_C * 0.044715) * x2))
    hx = 0.5 * x
    return hx + hx * th, th, x2, hx


def _gelu(x):
    return _gelu_parts(x)[0]


def _gelu_and_grad(x):
    g, th, x2, hx = _gelu_parts(x)
    return g, (0.5 + 0.5 * th) + hx * (1.0 - th * th) * (_GELU_C + (3.0 * _GELU_C * 0.044715) * x2)


def _gelu_grad(x):
    return _gelu_and_grad(x)[1]


def _sigmoid(x):
    return 1.0 / (1.0 + jnp.exp(-x))


def _silu_and_grad(x):
    s = _sigmoid(x)
    return x * s, s * (1.0 + x * (1.0 - s))


def _rms(x):
    return lax.rsqrt(jnp.mean(x * x, axis=-1, keepdims=True) + NORM_EPS)


def _full(shape):
    nd = len(shape)
    return pl.BlockSpec(shape, lambda *_: (0,) * nd)


MESH = pl.DeviceIdType.MESH
ANY = pl.BlockSpec(memory_space=pl.ANY)


def _place():
    return lax.axis_index("x"), lax.axis_index("y"), lax.axis_index("c")


def _chip_peer(x, y, c, d):
    return (1 - x if d >= 2 else x, 1 - y if d % 2 else y, c)


class _Plan:
    def __init__(self, inputs, out_shape, build):
        self.inputs, self.out_shape, self._build = list(inputs), list(out_shape), build
        n = len(self.inputs)
        self.sems = [pltpu.SemaphoreType.DMA((n, 3)), pltpu.SemaphoreType.DMA((n, 3)), pltpu.SemaphoreType.DMA((n,))]

    def start(self, in_refs, out_refs, sems):
        send, recv, local = self._build(in_refs, out_refs, sems)
        for p in range(len(self.inputs)):
            local[p].start()
            for cp in send[p]:
                cp.start()

    def wait(self, in_refs, out_refs, sems):
        send, recv, local = self._build(in_refs, out_refs, sems)
        for p in range(len(self.inputs)):
            for cp in recv[p]:
                cp.wait_recv()
        for p in range(len(self.inputs)):
            for cp in send[p]:
                cp.wait_send()
            local[p].wait()


class _GatherPlan:
    def __init__(self, shards, only=None):
        n = len(shards)
        self.n, self.only = n, only
        self.peers = (1, 2, 3) if only is None else (only,)
        self.inputs = list(shards)
        slabs = N_CHIPS if only is None else 1
        self.out_shape = [jax.ShapeDtypeStruct((slabs,) + s.shape, s.dtype) for s in shards]
        self.halved = [s.shape[0] % 32 == 0 for s in shards]
        self.sems = [pltpu.SemaphoreType.DMA((n, 3)) for _ in range(4)] + [pltpu.SemaphoreType.DMA((n,))]

    def _copies(self, in_refs, out_refs, sems):
        ici_s, ici_r, d2d_s, d2d_r, loc = sems
        x, y, c = _place()
        me = 2 * x + y

        def rows(p, core):
            if not self.halved[p]:
                return slice(None)
            half = self.inputs[p].shape[0] // 2
            return pl.ds(pl.multiple_of(core * half, 16), half)

        def slab(chip):
            return chip if self.only is None else 0

        def ici(p, d, chip, core):
            return pltpu.make_async_remote_copy(
                src_ref=in_refs[p].at[rows(p, core)], dst_ref=out_refs[p].at[slab(chip), rows(p, core)],
                send_sem=ici_s.at[p, d - 1], recv_sem=ici_r.at[p, d - 1],
                device_id=_chip_peer(x, y, c, d), device_id_type=MESH)

        def d2d(p, d, core):
            part = out_refs[p].at[slab(me ^ d), rows(p, core)]
            return pltpu.make_async_remote_copy(
                src_ref=part, dst_ref=part, send_sem=d2d_s.at[p, d - 1], recv_sem=d2d_r.at[p, d - 1],
                device_id=(x, y, 1 - c), device_id_type=MESH)

        local = [pltpu.make_async_copy(in_refs[p], out_refs[p].at[slab(me)], loc.at[p]) for p in range(self.n)]
        return me, c, ici, d2d, local

    def start(self, in_refs, out_refs, sems):
        me, c, ici, d2d, local = self._copies(in_refs, out_refs, sems)
        for p in range(self.n):
            if self.only is None:
                local[p].start()
            for d in self.peers:
                ici(p, d, me, c).start()

    def wait(self, in_refs, out_refs, sems):
        me, c, ici, d2d, local = self._copies(in_refs, out_refs, sems)
        for p in range(self.n):
            for d in self.peers:
                ici(p, d, me ^ d, c).wait_recv()
                if self.halved[p]:
                    d2d(p, d, c).start()
        for p in range(self.n):
            for d in self.peers:
                if self.halved[p]:
                    d2d(p, d, 1 - c).wait_recv()
                    d2d(p, d, c).wait_send()
                ici(p, d, me, c).wait_send()
            if self.only is None:
                local[p].wait()


def gather_plan(shards, only=None):
    return _GatherPlan(shards, only)


def reduce_plan(shards, whole=()):
    n_s = len(shards)

    def build(in_refs, out_refs, sems):
        send_sems, recv_sems, loc_sems = sems
        x, y, c = _place()
        me = 2 * x + y

        def src(p, slab):
            return in_refs[p].at[slab] if p < n_s else in_refs[p]

        def remote(p, d):
            return pltpu.make_async_remote_copy(
                src_ref=src(p, me ^ d), dst_ref=out_refs[p].at[d], send_sem=send_sems.at[p, d - 1],
                recv_sem=recv_sems.at[p, d - 1], device_id=_chip_peer(x, y, c, d), device_id_type=MESH)

        n = len(in_refs)
        send = [[remote(p, d) for d in (1, 2, 3)] for p in range(n)]
        local = [pltpu.make_async_copy(src(p, me), out_refs[p].at[0], loc_sems.at[p]) for p in range(n)]
        return send, send, local

    outs = [jax.ShapeDtypeStruct(s.shape, s.dtype) for s in shards]
    outs += [jax.ShapeDtypeStruct((N_CHIPS,) + a.shape, a.dtype) for a in whole]
    return _Plan(list(shards) + list(whole), outs, build)


class _SiblingPlan:
    def __init__(self, arrs):
        self.inputs = list(arrs)
        self.out_shape = [jax.ShapeDtypeStruct(a.shape, a.dtype) for a in arrs]
        n = len(arrs)
        self.sems = [pltpu.SemaphoreType.DMA((n,)), pltpu.SemaphoreType.DMA((n,))]

    def _copies(self, in_refs, out_refs, sems):
        x, y, c = _place()
        return [pltpu.make_async_remote_copy(
            src_ref=in_refs[p], dst_ref=out_refs[p], send_sem=sems[0].at[p], recv_sem=sems[1].at[p],
            device_id=(x, y, 1 - c), device_id_type=MESH) for p in range(len(self.inputs))]

    def start(self, in_refs, out_refs, sems):
        for cp in self._copies(in_refs, out_refs, sems):
            cp.start()

    def wait(self, in_refs, out_refs, sems):
        copies = self._copies(in_refs, out_refs, sems)
        for cp in copies:
            cp.wait_recv()
        for cp in copies:
            cp.wait_send()


def run_plan(plan, name):
    n = len(plan.inputs)

    def body(*refs):
        plan.start(refs[:n], refs[n:2 * n], refs[2 * n:])
        plan.wait(refs[:n], refs[n:2 * n], refs[2 * n:])

    return pl.pallas_call(body, name=name, in_specs=[ANY] * n, out_specs=[ANY] * n, out_shape=plan.out_shape,
                          scratch_shapes=plan.sems)(*plan.inputs)


def _call(body, plan, *, name, grid, in_specs, out_specs, out_shape, sem, scratch_shapes=(), aliases=None,
          n_prefetch=0):
    aliases = {} if aliases is None else aliases
    single = not isinstance(out_shape, (list, tuple))
    out_specs = [out_specs] if single else list(out_specs)
    out_shape = [out_shape] if single else list(out_shape)
    n_in, n_out, n_scr = len(in_specs), len(out_specs), len(scratch_shapes)
    ci = 0 if plan is None else len(plan.inputs)
    co = 0 if plan is None else len(plan.out_shape)

    def hosted(*refs):
        pre, refs = refs[:n_prefetch], refs[n_prefetch:]
        ins, cins = refs[:n_in], refs[n_in:n_in + ci]
        k = n_in + ci
        outs, couts = refs[k:k + n_out], refs[k + n_out:k + n_out + co]
        k += n_out + co
        scr, sems = refs[k:k + n_scr], refs[k + n_scr:]
        ids = [pl.program_id(a) for a in range(len(grid))]
        first = functools.reduce(jnp.logical_and, [i == 0 for i in ids])
        last = functools.reduce(jnp.logical_and, [i == g - 1 for i, g in zip(ids, grid)])

        @pl.when(first)
        def _():
            plan.start(cins, couts, sems)

        body(*pre, *ins, *outs, *scr)

        @pl.when(last)
        def _():
            plan.wait(cins, couts, sems)

    def run(*args):
        hosting = plan is not None
        spec = pltpu.PrefetchScalarGridSpec(
            num_scalar_prefetch=n_prefetch, grid=grid,
            in_specs=list(in_specs) + ([ANY] * ci if hosting else []),
            out_specs=out_specs + ([ANY] * co if hosting else []),
            scratch_shapes=list(scratch_shapes) + (plan.sems if hosting else []))
        res = pl.pallas_call(hosted if hosting else body, name=name, grid_spec=spec,
                             out_shape=out_shape + (plan.out_shape if hosting else []),
                             input_output_aliases=aliases, compiler_params=_cparams(sem),
                             )(*args, *(plan.inputs if hosting else []))
        return (res[0] if single else res[:n_out]), list(res[n_out:])

    return run


def norm_matmul(x, g, w, name, plan=None, tn=None):
    L, D = x.shape
    tl = min(TL_DW, L)
    if w.ndim == 3:
        nt, _, tn = w.shape
        w_spec = pl.BlockSpec((1, D, tn), lambda i, n: (n, 0, 0))
    else:
        nt = w.shape[1] // tn
        w_spec = pl.BlockSpec((D, tn), lambda i, n: (0, n))

    def body(x_ref, g_ref, w_ref, o_ref, h_ref):
        xv = x_ref[...]
        h = (xv * _rms(xv) * g_ref[...]).astype(h_ref.dtype)
        h_ref[...] = h
        o_ref[...] = _mm(h, w_ref[0] if w.ndim == 3 else w_ref[...])

    return _call(
        body, plan, name=name, grid=(L // tl, nt),
        in_specs=[pl.BlockSpec((tl, D), lambda i, n: (i, 0)), _full((1, D)), w_spec],
        out_specs=[pl.BlockSpec((tl, tn), lambda i, n: (i, n)), pl.BlockSpec((tl, D), lambda i, n: (i, 0))],
        out_shape=[jax.ShapeDtypeStruct((L, nt * tn), F32), jax.ShapeDtypeStruct((L, D), MXU_DTYPE)],
        sem=("arbitrary", "arbitrary"),
    )(x, g, w)


def even_in_slabs(x, xs, g, w, slabs, wsel, name, p_in=None, plan=None):
    L, D = x.shape
    tl = min(TL_PROJ, L)
    wb = EVEN_IN // N_CHIPS
    n = slabs.shape[0]
    s5_cols = 2 * S5_WIDTH - wb
    first = p_in is None

    def body(slabs_ref, wsel_ref, xs_ref, x_ref, g_ref, w_ref, *rest):
        o_ref = rest[-3] if first else rest[-1]
        j = slabs_ref[pl.program_id(0)]
        hs = (xs_ref[...] * _rms(xs_ref[...]) * g_ref[...]).astype(MXU_DTYPE)
        h = (x_ref[...] * _rms(x_ref[...]) * g_ref[...]).astype(MXU_DTYPE)
        if first:
            rest[-2][...] = hs
            rest[-1][...] = h
        o_ref[:, :s5_cols] = _mm(jnp.where(j <= 1, hs, h), w_ref[0, :, :s5_cols])
        o_ref[:, s5_cols:] = _mm(jnp.where(j == 0, hs, h), w_ref[0, :, s5_cols:])

    row = pl.BlockSpec((tl, D), lambda s, i, slabs_ref, wsel_ref: (i, 0))
    in_specs = [row if first else
                pl.BlockSpec((tl, D), lambda s, i, slabs_ref, wsel_ref: (jnp.where(slabs_ref[s] <= 1, i, 0), 0)),
                row if first else
                pl.BlockSpec((tl, D), lambda s, i, slabs_ref, wsel_ref: (jnp.where(slabs_ref[s] >= 1, i, 0), 0)),
                pl.BlockSpec((1, D), lambda s, i, slabs_ref, wsel_ref: (0, 0)),
                pl.BlockSpec((1, D, wb), lambda s, i, slabs_ref, wsel_ref: (wsel_ref[s], 0, 0))]
    out_specs = [pl.BlockSpec((tl, wb), lambda s, i, slabs_ref, wsel_ref: (i, slabs_ref[s]))]
    out_shape = [jax.ShapeDtypeStruct((L, EVEN_IN), F32)]
    args = [slabs, wsel, xs, x, g, w]
    if first:
        out_specs += [row, row]
        out_shape += [jax.ShapeDtypeStruct((L, D), MXU_DTYPE)] * 2
    else:
        in_specs.append(ANY)
        args.append(p_in)
    return _call(body, plan, name=name, grid=(n, L // tl), in_specs=in_specs, out_specs=out_specs,
                 out_shape=out_shape, sem=("arbitrary", "arbitrary"), n_prefetch=2,
                 aliases={} if first else {6: 0})(*args)


def matmul_residual(ys, w, x, name):
    L, D = x.shape
    tl = min(TL_PROJ, L)
    n = len(ys)
    offs = np.cumsum([0] + [y.shape[1] for y in ys])

    def body(*refs):
        y_refs, w_ref, x_ref, o_ref = refs[:n], refs[n], refs[n + 1], refs[n + 2]
        acc = x_ref[...]
        for k in range(n):
            acc = acc + _mm(y_refs[k][...], w_ref[offs[k]:offs[k + 1], :])
        o_ref[...] = acc

    return pl.pallas_call(
        body, name=name, grid=(L // tl,),
        in_specs=[pl.BlockSpec((tl, y.shape[1]), lambda i: (i, 0)) for y in ys]
        + [_full(w.shape), pl.BlockSpec((tl, D), lambda i: (i, 0))],
        out_specs=pl.BlockSpec((tl, D), lambda i: (i, 0)),
        out_shape=jax.ShapeDtypeStruct((L, D), F32),
        compiler_params=_cparams(("arbitrary",)),
    )(*ys, w, x)


def out_proj_loss(y, w, x, gf, tgt, name):
    L, K = y.shape
    D = w.shape[1]
    tl = min(TL_PROJ, L)

    def body(y_ref, w_ref, x_ref, gf_ref, t_ref, dx_ref, loss_ref, dg_ref):
        @pl.when(pl.program_id(0) == 0)
        def _():
            loss_ref[...] = jnp.zeros_like(loss_ref)
            dg_ref[...] = jnp.zeros_like(dg_ref)

        x2 = x_ref[...] + _mm(y_ref[...], w_ref[...])
        r = _rms(x2)
        xn = x2 * r
        e = xn * gf_ref[...] - t_ref[...]
        loss_ref[...] += (0.5 / D) * jnp.sum(e * e)
        dout = e * (1.0 / D)
        dg_ref[...] += jnp.sum(dout * xn, axis=0, keepdims=True)
        dxn = dout * gf_ref[...]
        dx_ref[...] = r * (dxn - xn * jnp.mean(dxn * xn, axis=-1, keepdims=True))

    return pl.pallas_call(
        body, name=name, grid=(L // tl,),
        in_specs=[pl.BlockSpec((tl, K), lambda i: (i, 0)), _full((K, D)),
                  pl.BlockSpec((tl, D), lambda i: (i, 0)), _full((1, D)),
                  pl.BlockSpec((tl, D), lambda i: (i, 0))],
        out_specs=[pl.BlockSpec((tl, D), lambda i: (i, 0)), _full((8, 128)), _full((1, D))],
        out_shape=[jax.ShapeDtypeStruct((L, D), F32), jax.ShapeDtypeStruct((8, 128), F32),
                   jax.ShapeDtypeStruct((1, D), F32)],
        compiler_params=_cparams(("arbitrary",)),
    )(y, w, x, gf, tgt)


def out_proj_bwd(dx, w, ys, name):
    L, D = dx.shape
    K = w.shape[0]
    tl = min(TL_PROJ, L)
    n = len(ys)
    offs = np.cumsum([0] + [y.shape[1] for y in ys])

    def body(*refs):
        dx_ref, w_ref, y_refs = refs[0], refs[1], refs[2:2 + n]
        dy_refs, dw_ref = refs[2 + n:2 + 2 * n], refs[2 + 2 * n]

        @pl.when(pl.program_id(0) == 0)
        def _():
            dw_ref[...] = jnp.zeros_like(dw_ref)

        dxv = dx_ref[...]
        for k in range(n):
            dy_refs[k][...] = _mm_nt(dxv, w_ref[offs[k]:offs[k + 1], :])
            dw_ref[offs[k]:offs[k + 1], :] += _mm_tn(y_refs[k][...], dxv)

    y_specs = [pl.BlockSpec((tl, y.shape[1]), lambda i: (i, 0)) for y in ys]
    return pl.pallas_call(
        body, name=name, grid=(L // tl,),
        in_specs=[pl.BlockSpec((tl, D), lambda i: (i, 0)), _full((K, D))] + y_specs,
        out_specs=y_specs + [_full((K, D))],
        out_shape=[jax.ShapeDtypeStruct(y.shape, F32) for y in ys] + [jax.ShapeDtypeStruct((K, D), F32)],
        compiler_params=_cparams(("arbitrary",)),
    )(dx, w, *ys)


def in_proj_bwd_dx(x, g, dps, ws, dres, name, plan=None):
    L, D = x.shape
    tl = min(TL_PROJ, L)
    n = len(dps)

    def body(*refs):
        x_ref, g_ref, dres_ref = refs[:3]
        dp_refs, w_refs = refs[3:3 + n], refs[3 + n:3 + 2 * n]
        dx_ref, dg_ref = refs[3 + 2 * n:]

        @pl.when(pl.program_id(0) == 0)
        def _():
            dg_ref[...] = jnp.zeros_like(dg_ref)

        dh = None
        for dp_ref, w_ref, w in zip(dp_refs, w_refs, ws):
            if w.ndim == 3:
                tn = w.shape[2]
                parts = [_mm_nt(dp_ref[:, tn * k:tn * (k + 1)], w_ref[k]) for k in range(w.shape[0])]
            else:
                parts = [_mm_nt(dp_ref[...], w_ref[...])]
            for part in parts:
                dh = part if dh is None else dh + part
        xv = x_ref[...]
        r = _rms(xv)
        xn = xv * r
        dg_ref[...] += jnp.sum(dh * xn, axis=0, keepdims=True)
        dxn = dh * g_ref[...]
        dx_ref[...] = dres_ref[...] + r * (dxn - xn * jnp.mean(dxn * xn, axis=-1, keepdims=True))

    return _call(
        body, plan, name=name, grid=(L // tl,),
        in_specs=[pl.BlockSpec((tl, D), lambda i: (i, 0)), _full((1, D)), pl.BlockSpec((tl, D), lambda i: (i, 0))]
        + [pl.BlockSpec((tl, dp.shape[1]), lambda i: (i, 0)) for dp in dps] + [_full(w.shape) for w in ws],
        out_specs=[pl.BlockSpec((tl, D), lambda i: (i, 0)), _full((1, D))],
        out_shape=[jax.ShapeDtypeStruct((L, D), F32), jax.ShapeDtypeStruct((1, D), F32)],
        sem=("arbitrary",),
    )(x, g, dres, *dps, *ws)


def in_proj_bwd_dw(h, dp, name, tn, first=0, into=None, dtype=F32, plan=None, dp_first=0, count=None):
    L, D = h.shape
    tl = min(TL_DW, L)
    wb = EVEN_IN // N_CHIPS
    per = wb // tn
    count = dp.shape[1] // tn if count is None else count
    last = L // tl - 1

    def body(*refs):
        h_ref, dp_ref, dw_ref, acc = refs[0], refs[1], refs[-2], refs[-1]

        @pl.when(pl.program_id(1) == 0)
        def _():
            acc[...] = jnp.zeros_like(acc)

        acc[...] += _mm_tn(h_ref[...], dp_ref[...])

        @pl.when(pl.program_id(1) == last)
        def _():
            dw_ref[0] = acc[...].astype(dw_ref.dtype)

    ins = [h, dp] + ([] if into is None else [into])
    return _call(
        body, plan, name=name, grid=(count, L // tl),
        in_specs=[pl.BlockSpec((tl, D), lambda n, i: (i, 0)), pl.BlockSpec((tl, tn), lambda n, i: (i, n + dp_first))]
        + ([] if into is None else [ANY]),
        out_specs=pl.BlockSpec((1, D, tn), lambda n, i: ((n + first) // per, 0, (n + first) % per)),
        out_shape=jax.ShapeDtypeStruct((N_CHIPS, D, wb), dtype),
        scratch_shapes=[pltpu.VMEM((D, tn), F32)],
        aliases={} if into is None else {2: 0},
        sem=("arbitrary", "arbitrary"),
    )(*ins)


def _s5_param_fn(lam_re, lam_im, log_dt, b_re, b_im):
    lr = jnp.minimum(lam_re, -1e-4)
    li = lam_im
    dt = jnp.exp(log_dt)
    mag = jnp.exp(lr * dt)
    ab_re = mag * jnp.cos(li * dt)
    ab_im = mag * jnp.sin(li * dt)
    den = lr * lr + li * li
    n_re = ab_re - 1.0
    n_im = ab_im
    z_re = (n_re * lr + n_im * li) / den
    z_im = (n_im * lr - n_re * li) / den
    bb_re = z_re[None] * b_re - z_im[None] * b_im
    bb_im = z_re[None] * b_im + z_im[None] * b_re
    return ab_re, ab_im, bb_re, bb_im


def s5_params_fwd(lam_re, lam_im, log_dt, b_re, b_im, span):
    G, P = lam_re.shape
    H = b_re.shape[0]
    assert span & (span - 1) == 0

    def body(lr_ref, li_ref, dt_ref, br_ref, bi_ref, abr_ref, abi_ref, bbr_ref, bbi_ref, pr_ref, pi_ref):
        ab_re, ab_im, bb_re, bb_im = _s5_param_fn(lr_ref[...], li_ref[...], dt_ref[...], br_ref[...], bi_ref[...])
        abr_ref[...] = ab_re
        abi_ref[...] = ab_im
        bbr_ref[...] = bb_re
        bbi_ref[...] = bb_im
        cr, ci = ab_re, ab_im
        for _ in range(span.bit_length() - 1):
            cr, ci = cr * cr - ci * ci, 2.0 * cr * ci
        pr_ref[...] = cr
        pi_ref[...] = ci

    shp = lambda *s: jax.ShapeDtypeStruct(s, F32)
    return pl.pallas_call(
        body, name="s5_params_fwd",
        out_shape=[shp(G, P), shp(G, P), shp(H, G, P), shp(H, G, P), shp(G, P), shp(G, P)],
    )(lam_re, lam_im, log_dt, b_re, b_im)


def s5_params_bwd(lam_re, lam_im, log_dt, b_re, b_im, d_ab_re, d_ab_im, d_bb_re, d_bb_im):
    G, P = lam_re.shape
    H = b_re.shape[0]

    def body(lr_ref, li_ref, dt_ref, br_ref, bi_ref, g0, g1, g2, g3, o0, o1, o2, o3, o4):
        prim = (lr_ref[...], li_ref[...], dt_ref[...], br_ref[...], bi_ref[...])
        _, vjp = jax.vjp(_s5_param_fn, *prim)
        d = vjp((jnp.sum(g0[...], axis=0), jnp.sum(g1[...], axis=0), g2[...], g3[...]))
        o0[...], o1[...], o2[...], o3[...], o4[...] = d

    shp = lambda *s: jax.ShapeDtypeStruct(s, F32)
    return pl.pallas_call(
        body, name="s5_params_bwd",
        out_shape=[shp(G, P), shp(G, P), shp(G, 1), shp(H, G, P), shp(H, G, P)],
    )(lam_re, lam_im, log_dt, b_re, b_im, d_ab_re, d_ab_im, d_bb_re, d_bb_im)


def stream_order(a, tl):
    L, C = a.shape
    return a.reshape(L // tl, 8, tl // 8, C).transpose(0, 2, 1, 3).reshape(L, C)


def token_order(a, tl):
    L, C = a.shape
    return a.reshape(L // tl, tl // 8, 8, C).transpose(0, 2, 1, 3).reshape(L, C)


_LANE_BLK = 1024
_LANE_BLK_BWD = 1024


def _cmul_add(ar, ai, xr, xi, br, bi):
    return br + (ar * xr - ai * xi), bi + (ar * xi + ai * xr)


def _cmulc_add(ar, ai, xr, xi, br, bi):
    return br + (ar * xr + ai * xi), bi + (ar * xi - ai * xr)


def _s5_states(u, wbd_ref, a_re, a_im, at_re, at_im, s_re, s_im, e_re, e_im, c0_re, c0_im, tl):
    t8 = tl // 8
    for k in range(S5_KBLK):
        bu = _mm(u[:, 128 * k:128 * (k + 1)], wbd_ref[k])
        s_re[:, 512 * k:512 * (k + 1)] = bu[:, :512]
        s_im[:, 512 * k:512 * (k + 1)] = bu[:, 512:]
    outs_re, outs_im = [], []
    for b in range(S5_LANES // _LANE_BLK):
        lanes = slice(_LANE_BLK * b, _LANE_BLK * (b + 1))
        ar = jnp.broadcast_to(a_re[:, lanes], (8, _LANE_BLK))
        ai = jnp.broadcast_to(a_im[:, lanes], (8, _LANE_BLK))

        def local(i, carry, lanes=lanes, ar=ar, ai=ai):
            r = pl.multiple_of(i * 8, 8)
            sr, si = _cmul_add(ar, ai, carry[0], carry[1], s_re[pl.ds(r, 8), lanes], s_im[pl.ds(r, 8), lanes])
            s_re[pl.ds(r, 8), lanes] = sr
            s_im[pl.ds(r, 8), lanes] = si
            return sr, si

        zero = jnp.zeros((8, _LANE_BLK), F32)
        fr, fi = lax.fori_loop(0, t8, local, (zero, zero), unroll=True)
        tr, ti = at_re[:, lanes], at_im[:, lanes]
        er, ei = c0_re[:, lanes], c0_im[:, lanes]
        ers, eis = [er], [ei]
        for j in range(8):
            er, ei = _cmul_add(tr, ti, er, ei, fr[j:j + 1], fi[j:j + 1])
            ers.append(er)
            eis.append(ei)
        outs_re.append(ers[8])
        outs_im.append(eis[8])
        ent_r, ent_i = jnp.concatenate(ers[:8], axis=0), jnp.concatenate(eis[:8], axis=0)
        e_re[:, lanes] = ent_r
        e_im[:, lanes] = ent_i

        def fix(i, carry, lanes=lanes, ar=ar, ai=ai):
            r = pl.multiple_of(i * 8, 8)
            zr, zi = ar * carry[0] - ai * carry[1], ar * carry[1] + ai * carry[0]
            s_re[pl.ds(r, 8), lanes] = s_re[pl.ds(r, 8), lanes] + zr
            s_im[pl.ds(r, 8), lanes] = s_im[pl.ds(r, 8), lanes] + zi
            return zr, zi

        lax.fori_loop(0, t8, fix, (ent_r, ent_i), unroll=True)
    return jnp.concatenate(outs_re, axis=1), jnp.concatenate(outs_im, axis=1)


def _s5_readout(s_re, s_im, cre_ref, cim_ref):
    ys = []
    for k in range(S5_KBLK):
        lanes = slice(512 * k, 512 * (k + 1))
        ys.append(_mm(s_re[:, lanes], cre_ref[k]) - _mm(s_im[:, lanes], cim_ref[k]))
    return jnp.concatenate(ys, axis=1)


def s5_forward(p, wbd, cre, cim, atab, d_skip, w_glu, b_glu, plan=None):
    L = p.shape[0]
    tl = min(TL_S5, L)
    nch = L // tl

    def body(u_ref, z_ref, wbd_ref, cre_ref, cim_ref, at_ref, d_ref, wg_ref, bg_ref,
             ya_ref, st_re_ref, st_im_ref, sv_re_ref, sv_im_ref, s_re, s_im, e_re, e_im, car_re, car_im):
        @pl.when(pl.program_id(0) == 0)
        def _():
            car_re[...] = jnp.zeros_like(car_re)
            car_im[...] = jnp.zeros_like(car_im)

        c0_re, c0_im = car_re[...], car_im[...]
        st_re_ref[0] = c0_re
        st_im_ref[0] = c0_im
        u = u_ref[...]
        x_re, x_im = _s5_states(u, wbd_ref, at_ref[0:1], at_ref[1:2], at_ref[2:3], at_ref[3:4],
                                s_re, s_im, e_re, e_im, c0_re, c0_im, tl)
        car_re[...] = x_re
        car_im[...] = x_im
        sv_re_ref[...] = s_re[...].astype(sv_re_ref.dtype)
        sv_im_ref[...] = s_im[...].astype(sv_im_ref.dtype)
        y = _s5_readout(sv_re_ref, sv_im_ref, cre_ref, cim_ref) + d_ref[...] * u
        yg = _gelu(y)
        gate = _sigmoid(_mm(yg, wg_ref[...]) + bg_ref[...])
        sz, _ = _silu_and_grad(z_ref[...])
        ya_ref[...] = (yg * gate * sz).astype(ya_ref.dtype)

    return _call(
        body, plan, name="s5_forward", grid=(nch,),
        in_specs=[pl.BlockSpec((tl, 1024), lambda i: (i, 0)), pl.BlockSpec((tl, 1024), lambda i: (i, 1)),
                  _full(wbd.shape), _full(cre.shape), _full(cim.shape), _full(atab.shape),
                  _full((1, 1024)), _full((1024, 1024)), _full((1, 1024))],
        out_specs=[pl.BlockSpec((tl, 1024), lambda i: (i, 0)),
                   pl.BlockSpec((1, 1, S5_LANES), lambda i: (i, 0, 0)),
                   pl.BlockSpec((1, 1, S5_LANES), lambda i: (i, 0, 0)),
                   pl.BlockSpec((tl, S5_LANES), lambda i: (i, 0)), pl.BlockSpec((tl, S5_LANES), lambda i: (i, 0))],
        out_shape=[jax.ShapeDtypeStruct((L, 1024), MXU_DTYPE),
                   jax.ShapeDtypeStruct((nch, 1, S5_LANES), F32), jax.ShapeDtypeStruct((nch, 1, S5_LANES), F32),
                   jax.ShapeDtypeStruct((L, S5_LANES), MXU_DTYPE), jax.ShapeDtypeStruct((L, S5_LANES), MXU_DTYPE)],
        scratch_shapes=[pltpu.VMEM((tl, S5_LANES), F32), pltpu.VMEM((tl, S5_LANES), F32),
                        pltpu.VMEM((8, S5_LANES), F32), pltpu.VMEM((8, S5_LANES), F32),
                        pltpu.VMEM((1, S5_LANES), F32), pltpu.VMEM((1, S5_LANES), F32)],
        sem=("arbitrary",),
    )(p, p, wbd, cre, cim, atab, d_skip, w_glu, b_glu)


def s5_backward(p, dya, st_re, st_im, sv_re, sv_im, wbd, cre, cim, atab, d_skip, w_glu, b_glu, plan=None):
    L = p.shape[0]
    tl = min(TL_S5, L)
    t8 = tl // 8
    nch = L // tl
    rev = lambda i: (nch - 1 - i, 0)
    rev1 = lambda i: (nch - 1 - i, 1)
    rev3 = lambda i: (nch - 1 - i, 0, 0)
    ct_shape = (S5_KBLK, cre.shape[2], cre.shape[1])

    def body(u_ref, z_ref, dya_ref, str_ref, sti_ref, s_re, s_im, wbd_ref, cre_ref, cim_ref, at_ref,
             d_ref, wg_ref, bg_ref,
             dp_ref, dwbd_ref, dcre_ref, dcim_ref, dabr_ref, dabi_ref, dd_ref, dwg_ref, dbg_ref,
             g_re, g_im, car_re, car_im):
        @pl.when(pl.program_id(0) == 0)
        def _():
            car_re[...] = jnp.zeros_like(car_re)
            car_im[...] = jnp.zeros_like(car_im)
            for r in (dwbd_ref, dcre_ref, dcim_ref, dabr_ref, dabi_ref, dd_ref, dwg_ref, dbg_ref):
                r[...] = jnp.zeros_like(r)

        u = u_ref[...]
        a_re, a_im, at_re, at_im = at_ref[0:1], at_ref[1:2], at_ref[2:3], at_ref[3:4]
        y = _s5_readout(s_re, s_im, cre_ref, cim_ref) + d_ref[...] * u
        yg, dyg = _gelu_and_grad(y)
        gate = _sigmoid(_mm(yg, wg_ref[...]) + bg_ref[...])
        sz, dsz = _silu_and_grad(z_ref[...])
        dya = dya_ref[...]
        s5out = yg * gate
        dp_ref[:, 1024:] = (dya * s5out * dsz).astype(dp_ref.dtype)
        ds5 = dya * sz
        dt = ds5 * yg * gate * (1.0 - gate)
        dwg_ref[...] += _mm_tn(yg, dt)
        dbg_ref[...] += jnp.sum(dt, axis=0, keepdims=True)
        dyv = (ds5 * gate + _mm_nt(dt, wg_ref[...])) * dyg
        dd_ref[...] += jnp.sum(dyv * u, axis=0, keepdims=True)

        for k in range(S5_KBLK):
            lanes = slice(512 * k, 512 * (k + 1))
            dyk = dyv[:, 128 * k:128 * (k + 1)]
            g_re[:, lanes] = _mm_nt(dyk, cre_ref[k])
            g_im[:, lanes] = -_mm_nt(dyk, cim_ref[k])
            dcre_ref[k] += _mm_tn(dyk, s_re[:, lanes])
            dcim_ref[k] -= _mm_tn(dyk, s_im[:, lanes])

        blk = _LANE_BLK_BWD
        for b in range(S5_LANES // blk):
            lanes = slice(blk * b, blk * (b + 1))
            ar = jnp.broadcast_to(a_re[:, lanes], (8, blk))
            ai = jnp.broadcast_to(a_im[:, lanes], (8, blk))

            def local(j, carry, lanes=lanes, ar=ar, ai=ai):
                r = pl.multiple_of((t8 - 1 - j) * 8, 8)
                gr, gi = _cmulc_add(ar, ai, carry[0], carry[1], g_re[pl.ds(r, 8), lanes], g_im[pl.ds(r, 8), lanes])
                g_re[pl.ds(r, 8), lanes] = gr
                g_im[pl.ds(r, 8), lanes] = gi
                return gr, gi

            zero = jnp.zeros((8, blk), F32)
            fr, fi = lax.fori_loop(0, t8, local, (zero, zero), unroll=True)
            tr, ti = at_re[:, lanes], at_im[:, lanes]
            hr, hi = car_re[:, lanes], car_im[:, lanes]
            hrs, his = [hr], [hi]
            for j in range(7, -1, -1):
                hr, hi = _cmulc_add(tr, ti, hr, hi, fr[j:j + 1], fi[j:j + 1])
                hrs.append(hr)
                his.append(hi)
            car_re[:, lanes] = hrs[8]
            car_im[:, lanes] = his[8]
            in_r = jnp.concatenate(hrs[7::-1], axis=0)
            in_i = jnp.concatenate(his[7::-1], axis=0)

            wr, wi, nr, ni, accr, acci = in_r, in_i, zero, zero, zero, zero
            for pair in range(t8 // 2 - 1, -1, -1):
                rows = slice(16 * pair, 16 * pair + 16)
                s16r, s16i = s_re[rows, lanes].astype(F32), s_im[rows, lanes].astype(F32)
                for half in (1, 0):
                    r = 16 * pair + 8 * half
                    sr, si = s16r[8 * half:8 * half + 8], s16i[8 * half:8 * half + 8]
                    accr, acci = accr + (sr * nr + si * ni), acci + (sr * ni - si * nr)
                    wr, wi = ar * wr + ai * wi, ar * wi - ai * wr
                    nr, ni = g_re[r:r + 8, lanes] + wr, g_im[r:r + 8, lanes] + wi
                    g_re[r:r + 8, lanes] = nr
                    g_im[r:r + 8, lanes] = ni
            lr, li = s_re[tl - 16:tl, lanes].astype(F32)[8:], s_im[tl - 16:tl, lanes].astype(F32)[8:]
            row0 = lax.broadcasted_iota(jnp.int32, (8, blk), 0) == 0
            sr = jnp.where(row0, jnp.broadcast_to(str_ref[0][:, lanes], (8, blk)), pltpu.roll(lr, 1, 0))
            si = jnp.where(row0, jnp.broadcast_to(sti_ref[0][:, lanes], (8, blk)), pltpu.roll(li, 1, 0))
            dabr_ref[:, lanes] += accr + (sr * nr + si * ni)
            dabi_ref[:, lanes] += acci + (sr * ni - si * nr)

        dus = []
        for k in range(S5_KBLK):
            lanes = slice(512 * k, 512 * (k + 1))
            g = jnp.concatenate([g_re[:, lanes], g_im[:, lanes]], axis=1)
            dwbd_ref[k] += _mm_tn(u[:, 128 * k:128 * (k + 1)], g)
            dus.append(_mm_nt(g, wbd_ref[k]))
        du = jnp.concatenate(dus, axis=1) + dyv * d_ref[...]
        dp_ref[:, :1024] = du.astype(dp_ref.dtype)

    shp = lambda *s: jax.ShapeDtypeStruct(s, F32)
    return _call(
        body, plan, name="s5_backward", grid=(nch,),
        in_specs=[pl.BlockSpec((tl, 1024), rev), pl.BlockSpec((tl, 1024), rev1), pl.BlockSpec((tl, 1024), rev),
                  pl.BlockSpec((1, 1, S5_LANES), rev3), pl.BlockSpec((1, 1, S5_LANES), rev3),
                  pl.BlockSpec((tl, S5_LANES), rev), pl.BlockSpec((tl, S5_LANES), rev),
                  _full(wbd.shape), _full(cre.shape), _full(cim.shape), _full(atab.shape),
                  _full((1, 1024)), _full((1024, 1024)), _full((1, 1024))],
        out_specs=[pl.BlockSpec((tl, 2048), rev), _full(wbd.shape), _full(ct_shape), _full(ct_shape),
                   _full((8, S5_LANES)), _full((8, S5_LANES)), _full((1, 1024)), _full((1024, 1024)), _full((1, 1024))],
        out_shape=[jax.ShapeDtypeStruct((L, 2048), MXU_DTYPE), shp(*wbd.shape), shp(*ct_shape), shp(*ct_shape),
                   shp(8, S5_LANES), shp(8, S5_LANES), shp(1, 1024), shp(1024, 1024), shp(1, 1024)],
        scratch_shapes=[pltpu.VMEM((tl, S5_LANES), F32), pltpu.VMEM((tl, S5_LANES), F32),
                        pltpu.VMEM((1, S5_LANES), F32), pltpu.VMEM((1, S5_LANES), F32)],
        sem=("arbitrary",),
    )(p, p, dya, st_re, st_im, sv_re, sv_im, wbd, cre, cim, atab, d_skip, w_glu, b_glu)


def _block_diag(w, rows_first):
    g8 = w.reshape(S5_KBLK, 8, w.shape[1], w.shape[2])
    eye = jnp.eye(8, dtype=w.dtype)
    out = jnp.einsum('kgab,fg->kfagb', g8, eye)
    return out.reshape(S5_KBLK, 8 * w.shape[1], 8 * w.shape[2])


def _block_diag_extract(wbd, a, b):
    w5 = wbd.reshape(S5_KBLK, 8, a, 8, b)
    idx = jnp.arange(8)
    return w5[:, idx, :, idx, :].transpose(1, 0, 2, 3).reshape(S5_GROUPS, a, b)


def _ret_constants():
    log_g = np.log1p(-np.exp2(-5.0 - np.arange(RET_HEADS, dtype=np.float32))).astype(np.float32)
    idx = np.arange(RET_CHUNK, dtype=np.float32)
    diff = idx[:, None] - idx[None, :]
    decay = np.where(diff >= 0, np.exp(log_g[:, None, None] * np.maximum(diff, 0.0)), 0.0).astype(np.float32)
    xi = np.exp(log_g[None, :] * (idx[:, None] + 1.0)).astype(np.float32)
    zeta = np.exp(log_g[None, :] * (RET_CHUNK - 1.0 - idx[:, None])).astype(np.float32)
    chunk_decay = np.exp(log_g * RET_CHUNK).astype(np.float32)
    return decay, xi, zeta, chunk_decay


def _rope_tables(L):
    half = RET_DK // 2
    inv = ROPE_BASE ** (-jnp.arange(half, dtype=F32) / half)
    ang = jnp.arange(L, dtype=F32)[:, None] * inv[None, :]
    return jnp.cos(ang), jnp.sin(ang)


def _rot(xh, cos, sin):
    x1, x2 = xh[:, :128], xh[:, 128:]
    return jnp.concatenate([x1 * cos - x2 * sin, x1 * sin + x2 * cos], axis=1)


def _rot_t(dh, cos, sin):
    d1, d2 = dh[:, :128], dh[:, 128:]
    return jnp.concatenate([d1 * cos + d2 * sin, d2 * cos - d1 * sin], axis=1)


RET_PER_STEP = 2


def _ret_setup(L):
    nc = L // RET_CHUNK
    per = RET_PER_STEP if nc % RET_PER_STEP == 0 else 1
    decay_np, xi_np, zeta_np, cd_np = _ret_constants()
    tables = (jnp.asarray(decay_np), jnp.asarray(np.tile(xi_np, (per, 1))), jnp.asarray(np.tile(zeta_np, (per, 1))))
    return nc // per, per, tables, [float(c) for c in cd_np]


def _ret_rows(q_ref, k_ref, v_ref, cos_ref, sin_ref, xi_ref, zeta_ref):
    H = range(RET_HEADS)
    hs = [slice(RET_DK * h, RET_DK * (h + 1)) for h in H]
    cs, sn = cos_ref[...], sin_ref[...]
    qh = [_rot(q_ref[:, hs[h]], cs, sn) for h in H]
    kh = [_rot(k_ref[:, hs[h]], cs, sn) * (RET_DK ** -0.5) for h in H]
    vh = [v_ref[:, hs[h]] for h in H]
    qx = [qh[h] * xi_ref[:, h:h + 1] for h in H]
    kz = [kh[h] * zeta_ref[:, h:h + 1] for h in H]
    return hs, cs, sn, qh, kh, vh, qx, kz


def _ret_normed(qh, kh, vh, qx, dec_ref, prevs, per):
    H, C = range(RET_HEADS), range(per)
    rs = [slice(RET_CHUNK * c, RET_CHUNK * (c + 1)) for c in C]
    sc = [[_mm_nt(qh[h][rs[c]], kh[h][rs[c]]) * dec_ref[h] for h in H] for c in C]
    inner = [[_mm(sc[c][h], vh[h][rs[c]]) for h in H] for c in C]
    cross = [[_mm(qx[h][rs[c]], prevs[c][h]) for h in H] for c in C]
    o = [jnp.concatenate([inner[c][h] + cross[c][h] for c in C], axis=0) for h in H]
    oc = [o[h] - jnp.mean(o[h], axis=-1, keepdims=True) for h in H]
    rstd = [lax.rsqrt(jnp.mean(oc[h] * oc[h], axis=-1, keepdims=True) + NORM_EPS) for h in H]
    on = [oc[h] * rstd[h] for h in H]
    return rs, sc, rstd, on


def retention_forward(p, cos, sin, gain):
    L = p.shape[0]
    steps, per, (decay, xi, zeta), cd = _ret_setup(L)
    rows = RET_CHUNK * per

    def body(q_ref, k_ref, v_ref, z_ref, cos_ref, sin_ref, dec_ref, xi_ref, zeta_ref, gain_ref,
             yb_ref, prev_ref, state):
        @pl.when(pl.program_id(0) == 0)
        def _():
            state[...] = jnp.zeros_like(state)

        H, C = range(RET_HEADS), range(per)
        hs, cs, sn, qh, kh, vh, qx, kz = _ret_rows(q_ref, k_ref, v_ref, cos_ref, sin_ref, xi_ref, zeta_ref)
        prevs = [[state[h] for h in H]]
        for c in C:
            rs_c = slice(RET_CHUNK * c, RET_CHUNK * (c + 1))
            prevs.append([prevs[c][h] * cd[h] + _mm_tn(kz[h][rs_c], vh[h][rs_c]) for h in H])
        _, _, _, on = _ret_normed(qh, kh, vh, qx, dec_ref, prevs, per)
        sz, _ = _silu_and_grad(z_ref[...])
        for h in H:
            for c in C:
                prev_ref[c, h] = prevs[c][h].astype(prev_ref.dtype)
            state[h] = prevs[per][h]
            yb_ref[:, hs[h]] = (on[h] * gain_ref[:, hs[h]] * sz[:, hs[h]]).astype(yb_ref.dtype)

    col0 = p.shape[1] // 1024 - 4
    blk = lambda c: pl.BlockSpec((rows, 1024), lambda i, c=c: (i, c + col0))
    return pl.pallas_call(
        body, name="retention_forward", grid=(steps,),
        in_specs=[blk(0), blk(1), blk(2), blk(3),
                  pl.BlockSpec((rows, 128), lambda i: (i, 0)), pl.BlockSpec((rows, 128), lambda i: (i, 0)),
                  _full(decay.shape), _full(xi.shape), _full(zeta.shape), _full((1, 1024))],
        out_specs=[pl.BlockSpec((rows, 1024), lambda i: (i, 0)),
                   pl.BlockSpec((per, RET_HEADS, RET_DK, RET_DK), lambda i: (i, 0, 0, 0))],
        out_shape=[jax.ShapeDtypeStruct((L, 1024), MXU_DTYPE),
                   jax.ShapeDtypeStruct((steps * per, RET_HEADS, RET_DK, RET_DK), MXU_DTYPE)],
        scratch_shapes=[pltpu.VMEM((RET_HEADS, RET_DK, RET_DK), F32)],
        compiler_params=_cparams(("arbitrary",)),
    )(p, p, p, p, cos, sin, decay, xi, zeta, gain)


def retention_backward(p, dy, prevs, cos, sin, gain, plan=None):
    L = p.shape[0]
    steps, per, (decay, xi, zeta), cd = _ret_setup(L)
    rows = RET_CHUNK * per
    scale = RET_DK ** -0.5

    def body(q_ref, k_ref, v_ref, z_ref, dyb_ref, prev_ref, cos_ref, sin_ref, dec_ref, xi_ref, zeta_ref, gain_ref,
             dp_ref, dgain_ref, dstate):
        @pl.when(pl.program_id(0) == 0)
        def _():
            dstate[...] = jnp.zeros_like(dstate)
            dgain_ref[...] = jnp.zeros_like(dgain_ref)

        H, C = range(RET_HEADS), range(per)
        hs, cs, sn, qh, kh, vh, qx, kz = _ret_rows(q_ref, k_ref, v_ref, cos_ref, sin_ref, xi_ref, zeta_ref)
        prevs = [[prev_ref[c, h] for h in H] for c in C]
        rs, sc, rstd, on = _ret_normed(qh, kh, vh, qx, dec_ref, prevs, per)
        sz, dsz = _silu_and_grad(z_ref[...])
        dyb = dyb_ref[...]
        dong = [dyb[:, hs[h]] * sz[:, hs[h]] for h in H]
        don = [dong[h] * gain_ref[:, hs[h]] for h in H]
        do = [rstd[h] * (don[h] - jnp.mean(don[h], axis=-1, keepdims=True)
                         - on[h] * jnp.mean(don[h] * on[h], axis=-1, keepdims=True)) for h in H]
        dsc = [[_mm_nt(do[h][rs[c]], vh[h][rs[c]]) * dec_ref[h] for h in H] for c in C]
        dq_st = [[_mm_nt(do[h][rs[c]], prevs[c][h]) for h in H] for c in C]
        dnew = [[_mm_tn(qx[h][rs[c]], do[h][rs[c]]) for h in H] for c in C]
        dsts = [None] * per + [[dstate[h] for h in H]]
        for c in reversed(C):
            dsts[c] = [dsts[c + 1][h] * cd[h] + dnew[c][h] for h in H]
        dk_st = [[_mm_nt(vh[h][rs[c]], dsts[c + 1][h]) for h in H] for c in C]
        dv_st = [[_mm(kz[h][rs[c]], dsts[c + 1][h]) for h in H] for c in C]
        rows_of = lambda parts: jnp.concatenate(parts, axis=0)
        dqh = [rows_of([_mm(dsc[c][h], kh[h][rs[c]]) for c in C])
               + rows_of([dq_st[c][h] for c in C]) * xi_ref[:, h:h + 1] for h in H]
        dkh = [rows_of([_mm_tn(dsc[c][h], qh[h][rs[c]]) for c in C])
               + rows_of([dk_st[c][h] for c in C]) * zeta_ref[:, h:h + 1] for h in H]
        dvh = [rows_of([_mm_tn(sc[c][h], do[h][rs[c]]) + dv_st[c][h] for c in C]) for h in H]
        for h in H:
            dstate[h] = dsts[0][h]
            dgain_ref[:, hs[h]] += jnp.sum(dong[h] * on[h], axis=0, keepdims=True)
            dp_ref[:, hs[h]] = (_rot_t(dkh[h], cs, sn) * scale).astype(dp_ref.dtype)
            dp_ref[:, 1024 + RET_DK * h:1024 + RET_DK * (h + 1)] = dvh[h].astype(dp_ref.dtype)
            dp_ref[:, 2048 + RET_DK * h:2048 + RET_DK * (h + 1)] = (
                dyb[:, hs[h]] * on[h] * gain_ref[:, hs[h]] * dsz[:, hs[h]]).astype(dp_ref.dtype)
            dp_ref[:, 3072 + RET_DK * h:3072 + RET_DK * (h + 1)] = _rot_t(dqh[h], cs, sn).astype(dp_ref.dtype)

    col0 = p.shape[1] // 1024 - 4
    blk = lambda c: pl.BlockSpec((rows, 1024), lambda i, c=c: (steps - 1 - i, c + col0))
    tab = pl.BlockSpec((rows, 128), lambda i: (steps - 1 - i, 0))
    return _call(
        body, plan, name="retention_backward", grid=(steps,),
        in_specs=[blk(0), blk(1), blk(2), blk(3), pl.BlockSpec((rows, 1024), lambda i: (steps - 1 - i, 0)),
                  pl.BlockSpec((per, RET_HEADS, RET_DK, RET_DK), lambda i: (steps - 1 - i, 0, 0, 0)),
                  tab, tab, _full(decay.shape), _full(xi.shape), _full(zeta.shape), _full((1, 1024))],
        out_specs=[pl.BlockSpec((rows, 4096), lambda i: (steps - 1 - i, 0)), _full((1, 1024))],
        out_shape=[jax.ShapeDtypeStruct((L, 4096), MXU_DTYPE), jax.ShapeDtypeStruct((1, 1024), F32)],
        scratch_shapes=[pltpu.VMEM((RET_HEADS, RET_DK, RET_DK), F32)],
        sem=("arbitrary",),
    )(p, p, p, p, dy, prevs, cos, sin, decay, xi, zeta, gain)


def _sgu_mix(p_ref, gain_ref, wm_ref, bt_ref, tl):
    pu, pv, z = p_ref[:, :2048], p_ref[:, 2048:4096], p_ref[:, 4096:]
    (u, du), (v, dv) = _gelu_and_grad(pu), _gelu_and_grad(pv)
    mu = jnp.mean(v, axis=-1, keepdims=True)
    vc = v - mu
    rstd = lax.rsqrt(jnp.mean(vc * vc, axis=-1, keepdims=True) + NORM_EPS)
    vn = vc * rstd
    vg = vn * gain_ref[...]
    mask = (lax.broadcasted_iota(jnp.int32, (SGU_CHUNK, SGU_CHUNK), 0)
            >= lax.broadcasted_iota(jnp.int32, (SGU_CHUNK, SGU_CHUNK), 1))
    wms = [jnp.where(mask, wm_ref[g], 0.0) for g in range(SGU_GROUPS)]
    rows = []
    for c in range(tl // SGU_CHUNK):
        rs = slice(SGU_CHUNK * c, SGU_CHUNK * (c + 1))
        cols = []
        for g in range(SGU_GROUPS):
            gs = slice(SGU_GDIM * g, SGU_GDIM * (g + 1))
            cols.append(_mm(wms[g], vg[rs, gs]) + bt_ref[:, g:g + 1])
        rows.append(jnp.concatenate(cols, axis=1))
    s = rows[0] if len(rows) == 1 else jnp.concatenate(rows, axis=0)
    return du, dv, z, u, vn, rstd, vg, wms, mask, s


def sgu_forward(p, gain, wm, bt):
    L = p.shape[0]
    tl = min(TL_SGU, L)

    def body(p_ref, gain_ref, wm_ref, bt_ref, y_ref):
        _, _, z, u, _, _, _, _, _, s = _sgu_mix(p_ref, gain_ref, wm_ref, bt_ref, tl)
        sz, _ = _silu_and_grad(z)
        y_ref[...] = (u * s * sz).astype(y_ref.dtype)

    return pl.pallas_call(
        body, name="sgu_forward", grid=(L // tl,),
        in_specs=[pl.BlockSpec((tl, ODD_IN), lambda i: (i, 0)), _full((1, 2048)), _full(wm.shape), _full(bt.shape)],
        out_specs=pl.BlockSpec((tl, 2048), lambda i: (i, 0)),
        out_shape=jax.ShapeDtypeStruct((L, 2048), MXU_DTYPE),
        compiler_params=_cparams(("arbitrary",)),
    )(p, gain, wm, bt)


def sgu_backward(p, dy, gain, wm, bt, plan=None):
    L = p.shape[0]
    tl = min(TL_SGU, L)

    def body(p_ref, dy_ref, gain_ref, wm_ref, bt_ref, dp_ref, dgain_ref, dwm_ref, dbt_ref):
        @pl.when(pl.program_id(0) == 0)
        def _():
            dgain_ref[...] = jnp.zeros_like(dgain_ref)
            dwm_ref[...] = jnp.zeros_like(dwm_ref)
            dbt_ref[...] = jnp.zeros_like(dbt_ref)

        gu, gv, z, u, vn, rstd, vg, wms, mask, s = _sgu_mix(p_ref, gain_ref, wm_ref, bt_ref, tl)
        sz, dsz = _silu_and_grad(z)
        dyv = dy_ref[...]
        dp_ref[:, 4096:] = (dyv * u * s * dsz).astype(dp_ref.dtype)
        dsg = dyv * sz
        dp_ref[:, :2048] = (dsg * s * gu).astype(dp_ref.dtype)
        ds = dsg * u
        rows = []
        dbs = [jnp.zeros((SGU_CHUNK, 1), F32) for _ in range(SGU_GROUPS)]
        for c in range(tl // SGU_CHUNK):
            rs = slice(SGU_CHUNK * c, SGU_CHUNK * (c + 1))
            cols = []
            for g in range(SGU_GROUPS):
                gs = slice(SGU_GDIM * g, SGU_GDIM * (g + 1))
                dsg_c = ds[rs, gs]
                dbs[g] = dbs[g] + jnp.sum(dsg_c, axis=1, keepdims=True)
                dwm_ref[g] += jnp.where(mask, _mm_nt(dsg_c, vg[rs, gs]), 0.0)
                cols.append(_mm_tn(wms[g], dsg_c))
            rows.append(jnp.concatenate(cols, axis=1))
        dbt_ref[...] += jnp.concatenate(dbs, axis=1)
        dvg = rows[0] if len(rows) == 1 else jnp.concatenate(rows, axis=0)
        dgain_ref[...] += jnp.sum(dvg * vn, axis=0, keepdims=True)
        dvn = dvg * gain_ref[...]
        dv = rstd * (dvn - jnp.mean(dvn, axis=-1, keepdims=True) - vn * jnp.mean(dvn * vn, axis=-1, keepdims=True))
        dp_ref[:, 2048:4096] = (dv * gv).astype(dp_ref.dtype)

    return _call(
        body, plan, name="sgu_backward", grid=(L // tl,),
        in_specs=[pl.BlockSpec((tl, ODD_IN), lambda i: (i, 0)), pl.BlockSpec((tl, 2048), lambda i: (i, 0)),
                  _full((1, 2048)), _full(wm.shape), _full(bt.shape)],
        out_specs=[pl.BlockSpec((tl, ODD_IN), lambda i: (i, 0)), _full((1, 2048)), _full(wm.shape), _full(bt.shape)],
        out_shape=[jax.ShapeDtypeStruct((L, ODD_IN), MXU_DTYPE), jax.ShapeDtypeStruct((1, 2048), F32),
                   jax.ShapeDtypeStruct(wm.shape, F32), jax.ShapeDtypeStruct(bt.shape, F32)],
        sem=("arbitrary",),
    )(p, dy, gain, wm, bt)


def cast_shards(mats):
    n = len(mats)
    steps = 8

    def body(*refs):
        for p in range(n):
            refs[n + p][...] = refs[p][...].astype(MXU_DTYPE)

    specs = [pl.BlockSpec((m.shape[0] // steps, m.shape[1]), lambda i: (i, 0)) for m in mats]
    return pl.pallas_call(
        body, name="cast_shards", grid=(steps,), in_specs=specs, out_specs=specs,
        out_shape=[jax.ShapeDtypeStruct(m.shape, MXU_DTYPE) for m in mats],
        compiler_params=_cparams(("arbitrary",)),
    )(*mats)


def local_grads(x, tgt, w):
    L = x.shape[0]
    ne, gf = w["norm_even"], w["final_norm"].reshape(1, D_MODEL)
    sh = dict(zip(MATRICES, cast_shards([w[n][0] for n in MATRICES])))
    lam_re, lam_im = w["s5_lam_re"][0], w["s5_lam_im"][0]
    log_dt = w["s5_log_dt"].reshape(S5_GROUPS, 1)
    bt_re = jnp.transpose(w["s5_b_re"][0], (2, 0, 1))
    bt_im = jnp.transpose(w["s5_b_im"][0], (2, 0, 1))
    c_re, c_im = w["s5_c_re"][0], w["s5_c_im"][0]
    wm = w["sgu_w_spatial"][0]
    bt = jnp.transpose(w["sgu_b_spatial"][0])

    tl5 = min(TL_S5, L)
    ab_re, ab_im, bb_re, bb_im, at_re, at_im = s5_params_fwd(lam_re, lam_im, log_dt, bt_re, bt_im, tl5 // 8)
    atab = jnp.stack([ab_re.reshape(S5_LANES), ab_im.reshape(S5_LANES),
                      at_re.reshape(S5_LANES), at_im.reshape(S5_LANES)])
    wbd = jnp.concatenate([_block_diag(jnp.transpose(bb_re, (1, 0, 2)), True),
                           _block_diag(jnp.transpose(bb_im, (1, 0, 2)), True)], axis=2).astype(MXU_DTYPE)
    cre = _block_diag(jnp.transpose(c_re, (0, 2, 1)), True).astype(MXU_DTYPE)
    cim = _block_diag(jnp.transpose(c_im, (0, 2, 1)), True).astype(MXU_DTYPE)
    cos, sin = _rope_tables(L)

    s5_cols = 2 * S5_WIDTH
    me = (2 * lax.axis_index("x") + lax.axis_index("y")).astype(jnp.int32)
    xs = stream_order(x, tl5)
    slab = lambda d: jnp.stack([me ^ d])
    zero = jnp.zeros((1,), jnp.int32)
    shards = [sh["w_in_even"][None]]
    (p1, h0s, h0), (got,) = even_in_slabs(x, xs, ne, shards[0], slab(0), zero, "even_in_0",
                                          plan=gather_plan([sh["w_in_even"]], only=1))
    for d in (1, 2, 3):
        shards.append(got)
        plan = gather_plan([sh["w_in_even"]], only=d + 1) if d < 3 else gather_plan([sh["s5_w_glu"]])
        (p1,), (got,) = even_in_slabs(x, xs, ne, shards[d], slab(d), zero, "even_in_%d" % d, p_in=p1, plan=plan)
    w_glu = got
    by_xor = jnp.concatenate(shards)
    w_in_e = [lax.dynamic_index_in_dim(by_xor, me ^ j, 0, keepdims=False) for j in range(N_CHIPS)]
    w_s5 = jnp.concatenate([w_in_e[0], w_in_e[1][:, :s5_cols - EVEN_IN // N_CHIPS]], axis=1)
    w_kvzq = jnp.concatenate([w_in_e[2], w_in_e[3], w_in_e[1][:, s5_cols - EVEN_IN // N_CHIPS:]], axis=1)
    w_glu = w_glu.reshape(S5_WIDTH, S5_WIDTH)
    (ya, st_re, st_im, sv_re, sv_im), (w_out_e, w_in_o, w_out_o, no, sg_gain) = s5_forward(
        p1, wbd, cre, cim, atab, w["s5_d"], w_glu, w["s5_b_glu"],
        gather_plan([sh["w_out_even"], sh["w_in_odd"], sh["w_out_odd"], w["norm_odd"], w["sgu_norm_gain"]]))
    w_out_e = w_out_e.reshape(2 * S5_WIDTH, D_MODEL)
    w_out_o = w_out_o.reshape(SGU_WIDTH, D_MODEL)
    no, sg_gain = no.reshape(1, D_MODEL), sg_gain.reshape(1, SGU_WIDTH)
    yb, prevs = retention_forward(p1, cos, sin, w["ret_gn_gain"])
    ya = token_order(ya, tl5)
    x1 = matmul_residual([ya, yb], w_out_e, x, "even_out")
    (p2, h1), _ = norm_matmul(x1, no, w_in_o, "odd_in")
    y2 = sgu_forward(p2, sg_gain, wm, bt)
    dx2, loss, dgf = out_proj_loss(y2, w_out_o, x1, gf, tgt, "odd_out_loss")

    g, landed = {}, {}
    shard_major = lambda a, n: a.reshape((N_CHIPS,) + w[n].shape[1:])
    dy2, g_w_out_o = out_proj_bwd(dx2, w_out_o, [y2], "odd_out_bwd")
    (dp2, g["sgu_norm_gain"], dwm, dbt), (landed["w_out_odd"],) = sgu_backward(
        p2, dy2, sg_gain, wm, bt, reduce_plan([shard_major(g_w_out_o, "w_out_odd")]))
    g_w_in_o, _ = in_proj_bwd_dw(h1, dp2, "odd_in_dw", ODD_IN // N_CHIPS)
    (dx1, g["norm_odd"]), _ = in_proj_bwd_dx(x1, no, [dp2], [w_in_o], dx2, "odd_in_dx")
    dya, dyb, g_w_out_e = out_proj_bwd(dx1, w_out_e, [ya, yb], "even_out_bwd")
    ((dpa, dwbd, dcre, dcim, dab_re, dab_im, g["s5_d"], g_w_glu, g["s5_b_glu"]),
     (landed["w_in_odd"], landed["w_out_even"])) = s5_backward(
        p1, stream_order(dya, tl5), st_re, st_im, sv_re, sv_im, wbd, cre, cim, atab, w["s5_d"], w_glu,
        w["s5_b_glu"], reduce_plan([g_w_in_o, shard_major(g_w_out_e, "w_out_even")]))

    dbb_re = jnp.transpose(_block_diag_extract(dwbd[:, :, :512], S5_GROUP, S5_STATE), (1, 0, 2))
    dbb_im = jnp.transpose(_block_diag_extract(dwbd[:, :, 512:], S5_GROUP, S5_STATE), (1, 0, 2))
    dlr, dli, ddt, dbt_re, dbt_im = s5_params_bwd(
        lam_re, lam_im, log_dt, bt_re, bt_im, dab_re.reshape(8, S5_GROUPS, S5_STATE),
        dab_im.reshape(8, S5_GROUPS, S5_STATE), dbb_re, dbb_im)
    g["s5_lam_re"], g["s5_lam_im"] = dlr[None], dli[None]
    g["s5_log_dt"] = ddt.reshape(1, S5_GROUPS)
    g["s5_b_re"], g["s5_b_im"] = dbt_re, dbt_im
    g["s5_c_re"] = _block_diag_extract(dcre, S5_GROUP, S5_STATE)[None]
    g["s5_c_im"] = _block_diag_extract(dcim, S5_GROUP, S5_STATE)[None]
    g["sgu_w_spatial"] = dwm[None]
    g["sgu_b_spatial"] = jnp.transpose(dbt)[None]
    g["final_norm"] = dgf.reshape(D_MODEL)
    g["loss"] = loss

    (dpb, g["ret_gn_gain"]), (landed["s5_w_glu"],) = retention_backward(
        p1, dyb, prevs, cos, sin, w["ret_gn_gain"], reduce_plan([shard_major(g_w_glu, "s5_w_glu")]))
    done = tuple(n for n in MATRICES if n != "w_in_even")
    part = {n: sum_slabs(landed[n], "sum_" + n) for n in done}
    g_w_in_e, recv = in_proj_bwd_dw(h0s, dpa, "even_in_dw_s5", 512, dtype=MXU_DTYPE,
                                    plan=_SiblingPlan([part[n] for n in done]))
    other = dict(zip(done, recv))
    small = tuple(n for n in SMALL if n != "norm_even") + ("loss",)
    wb = EVEN_IN // N_CHIPS
    g_w_in_e, _ = in_proj_bwd_dw(h0, dpb, "even_in_dw_q", 512, first=s5_cols // 512, into=g_w_in_e,
                                 dtype=MXU_DTYPE, dp_first=2 * wb // 512, count=RET_HEADS * RET_DK // 512)
    g_w_in_e, recv = in_proj_bwd_dw(h0, dpb, "even_in_dw_kvz", wb, first=2, into=g_w_in_e, dtype=MXU_DTYPE,
                                    count=2, plan=reduce_plan([], [g[n] for n in small]))
    landed.update(zip(small, recv))
    (dx0, g["norm_even"]), (landed["w_in_even"],) = in_proj_bwd_dx(
        x, ne, [token_order(dpa, tl5), dpb], [w_s5, w_kvzq], dx1, "even_in_dx", reduce_plan([g_w_in_e]))
    (landed["norm_even"],) = run_plan(reduce_plan([], [g["norm_even"]]), "exchange_norm_even")
    return dx0, landed, part, other


def _row_block(rows):
    return 128 if rows % 128 == 0 else rows


def sum_slabs(r, name):
    _, R, C = r.shape
    tr = _row_block(R)

    def body(r_ref, o_ref):
        a, b, c, d = (r_ref[k].astype(F32) for k in range(N_CHIPS))
        o_ref[...] = (a + b) + (c + d)

    return pl.pallas_call(
        body, name=name, grid=(R // tr,),
        in_specs=[pl.BlockSpec((N_CHIPS, tr, C), lambda i: (0, i, 0))],
        out_specs=pl.BlockSpec((tr, C), lambda i: (i, 0)),
        out_shape=jax.ShapeDtypeStruct((R, C), F32),
        compiler_params=_cparams(("arbitrary",)),
    )(r)


def _adam(w, m, v, g):
    mn = ADAM_B1 * m + (1.0 - ADAM_B1) * g
    vn = ADAM_B2 * v + (1.0 - ADAM_B2) * (g * g)
    m_hat = mn / (1.0 - ADAM_B1 ** ADAM_STEP)
    v_hat = vn / (1.0 - ADAM_B2 ** ADAM_STEP)
    return -ADAM_LR * (m_hat / (jnp.sqrt(v_hat) + ADAM_EPS) + ADAM_WD * w), mn, vn


def adam_update(w, m, v, ga, gb, name):
    R, C = w.shape
    tr = _row_block(R)

    def body(w_ref, m_ref, v_ref, ga_ref, gb_ref, g_out, d_out, m_out, v_out):
        g = ga_ref[...] + gb_ref[...]
        g_out[...] = g
        d_out[...], m_out[...], v_out[...] = _adam(w_ref[...], m_ref[...], v_ref[...], g)

    blk = pl.BlockSpec((tr, C), lambda i: (i, 0))
    return pl.pallas_call(
        body, name=name, grid=(R // tr,),
        in_specs=[blk] * 5, out_specs=[blk] * 4,
        out_shape=[jax.ShapeDtypeStruct((R, C), F32)] * 4,
        compiler_params=_cparams(("arbitrary",)),
    )(w, m, v, ga, gb)


WIDE_ROWS = ("s5_b_re", "s5_b_im")


def sum_small(landed):
    def body(*refs):
        k = len(refs) // 2
        for i in range(k):
            r = refs[i]
            refs[k + i][...] = (r[0] + r[1]) + (r[2] + r[3])

    names = list(landed)
    res = pl.pallas_call(
        body, name="sum_small", out_shape=[jax.ShapeDtypeStruct(landed[n].shape[1:], F32) for n in names],
        compiler_params=pltpu.CompilerParams(vmem_limit_bytes=VMEM_LIMIT),
    )(*[landed[n] for n in names])
    return dict(zip(names, res))


def adam_small(names, w, m, v, ga, gb):
    def body(*refs):
        k = len(refs) // 9
        me = 2 * lax.axis_index("x") + lax.axis_index("y")
        for i in range(k):
            w_ref, m_ref, v_ref, ga_ref, gb_ref = refs[i], refs[k + i], refs[2 * k + i], refs[3 * k + i], refs[4 * k + i]
            size = w_ref.shape[-1]
            if ga_ref.shape != w_ref.shape:
                part = pl.ds(pl.multiple_of(me * size, LANES), size)
                g = ga_ref[:, part] + gb_ref[:, part]
            else:
                g = ga_ref[...] + gb_ref[...]
            refs[5 * k + i][...] = g
            refs[6 * k + i][...], refs[7 * k + i][...], refs[8 * k + i][...] = _adam(w_ref[...], m_ref[...], v_ref[...], g)

    ins = [d[n] for d in (w, m, v, ga, gb) for n in names]
    outs = [jax.ShapeDtypeStruct(w[n].shape, F32) for _ in range(4) for n in names]
    res = pl.pallas_call(body, name="adam_small", out_shape=outs,
                         compiler_params=pltpu.CompilerParams(vmem_limit_bytes=VMEM_LIMIT))(*ins)
    k = len(names)
    return [dict(zip(names, res[j * k:(j + 1) * k])) for j in range(4)]


WEIGHTS = ("norm_even", "w_in_even", "s5_lam_re", "s5_lam_im", "s5_log_dt", "s5_b_re", "s5_b_im", "s5_c_re",
           "s5_c_im", "s5_d", "s5_w_glu", "s5_b_glu", "ret_gn_gain", "w_out_even", "norm_odd", "w_in_odd",
           "sgu_norm_gain", "sgu_w_spatial", "sgu_b_spatial", "w_out_odd", "final_norm")
MATRICES = ("w_in_even", "s5_w_glu", "w_out_even", "w_in_odd", "w_out_odd")
SHARDED_VECS = ("norm_odd", "sgu_norm_gain")
REPLICATED = tuple(n for n in WEIGHTS if n not in MATRICES and n not in SHARDED_VECS)
SMALL = tuple(n for n in WEIGHTS if n not in MATRICES)
LANES = 128


def kernel(x, norm_even, w_in_even, s5_lam_re, s5_lam_im, s5_log_dt, s5_b_re, s5_b_im, s5_c_re, s5_c_im, s5_d, s5_w_glu, s5_b_glu, ret_gn_gain, w_out_even, norm_odd, w_in_odd, sgu_norm_gain, sgu_w_spatial, sgu_b_spatial, w_out_odd, final_norm, loss_target, m_norm_even, m_w_in_even, m_s5_lam_re, m_s5_lam_im, m_s5_log_dt, m_s5_b_re, m_s5_b_im, m_s5_c_re, m_s5_c_im, m_s5_d, m_s5_w_glu, m_s5_b_glu, m_ret_gn_gain, m_w_out_even, m_norm_odd, m_w_in_odd, m_sgu_norm_gain, m_sgu_w_spatial, m_sgu_b_spatial, m_w_out_odd, m_final_norm, v_norm_even, v_w_in_even, v_s5_lam_re, v_s5_lam_im, v_s5_log_dt, v_s5_b_re, v_s5_b_im, v_s5_c_re, v_s5_c_im, v_s5_d, v_s5_w_glu, v_s5_b_glu, v_ret_gn_gain, v_w_out_even, v_norm_odd, v_w_in_odd, v_sgu_norm_gain, v_sgu_w_spatial, v_sgu_b_spatial, v_w_out_odd, v_final_norm):
    w = dict(norm_even=norm_even, w_in_even=w_in_even, s5_lam_re=s5_lam_re, s5_lam_im=s5_lam_im, s5_log_dt=s5_log_dt, s5_b_re=s5_b_re, s5_b_im=s5_b_im, s5_c_re=s5_c_re, s5_c_im=s5_c_im, s5_d=s5_d, s5_w_glu=s5_w_glu, s5_b_glu=s5_b_glu, ret_gn_gain=ret_gn_gain, w_out_even=w_out_even, norm_odd=norm_odd, w_in_odd=w_in_odd, sgu_norm_gain=sgu_norm_gain, sgu_w_spatial=sgu_w_spatial, sgu_b_spatial=sgu_b_spatial, w_out_odd=w_out_odd, final_norm=final_norm)
    m = dict(norm_even=m_norm_even, w_in_even=m_w_in_even, s5_lam_re=m_s5_lam_re, s5_lam_im=m_s5_lam_im, s5_log_dt=m_s5_log_dt, s5_b_re=m_s5_b_re, s5_b_im=m_s5_b_im, s5_c_re=m_s5_c_re, s5_c_im=m_s5_c_im, s5_d=m_s5_d, s5_w_glu=m_s5_w_glu, s5_b_glu=m_s5_b_glu, ret_gn_gain=m_ret_gn_gain, w_out_even=m_w_out_even, norm_odd=m_norm_odd, w_in_odd=m_w_in_odd, sgu_norm_gain=m_sgu_norm_gain, sgu_w_spatial=m_sgu_w_spatial, sgu_b_spatial=m_sgu_b_spatial, w_out_odd=m_w_out_odd, final_norm=m_final_norm)
    v = dict(norm_even=v_norm_even, w_in_even=v_w_in_even, s5_lam_re=v_s5_lam_re, s5_lam_im=v_s5_lam_im, s5_log_dt=v_s5_log_dt, s5_b_re=v_s5_b_re, s5_b_im=v_s5_b_im, s5_c_re=v_s5_c_re, s5_c_im=v_s5_c_im, s5_d=v_s5_d, s5_w_glu=v_s5_w_glu, s5_b_glu=v_s5_b_glu, ret_gn_gain=v_ret_gn_gain, w_out_even=v_w_out_even, norm_odd=v_norm_odd, w_in_odd=v_w_in_odd, sgu_norm_gain=v_sgu_norm_gain, sgu_w_spatial=v_sgu_w_spatial, sgu_b_spatial=v_sgu_b_spatial, w_out_odd=v_w_out_odd, final_norm=v_final_norm)

    grad_x, landed, part, other = local_grads(x[0], loss_target[0], w)

    small = SMALL + ("loss",)
    part["w_in_even"] = sum_slabs(landed["w_in_even"], "sum_w_in_even")
    part.update(sum_small({n: landed[n] for n in small}))
    names = ("w_in_even",) + small
    other.update(zip(names, run_plan(_SiblingPlan([part[n] for n in names]), "sibling_exchange")))

    wt, mt, vt = dict(w), dict(m), dict(v)
    for n in WIDE_ROWS:
        wt[n], mt[n], vt[n] = (jnp.transpose(a[n][0], (2, 0, 1)) for a in (w, m, v))
    out_g, out_d, out_m, out_v = adam_small(SMALL, wt, mt, vt, part, other)
    for n in WIDE_ROWS:
        for out in (out_g, out_d, out_m, out_v):
            out[n] = jnp.transpose(out[n], (1, 2, 0))[None]
    for n in MATRICES:
        res = adam_update(w[n][0], m[n][0], v[n][0], part[n], other[n], "adam_" + n)
        out_g[n], out_d[n], out_m[n], out_v[n] = (r[None] for r in res)
    total_loss = (part["loss"] + other["loss"])[0, 0]

    return (total_loss, grad_x[None], *[out_g[n] for n in WEIGHTS], *[out_d[n] for n in WEIGHTS],
            *[out_m[n] for n in WEIGHTS], *[out_v[n] for n in WEIGHTS])
```

```python
import functools
import math

import numpy as np
import jax
import jax.numpy as jnp
from jax import lax
from jax.experimental import pallas as pl
from jax.experimental.pallas import tpu as pltpu

F32 = jnp.float32
MXU_DTYPE = jnp.bfloat16
NORM_EPS = 1e-6
D_MODEL = 1024
S5_WIDTH = 1024
S5_GROUP = 16
S5_GROUPS = 64
S5_STATE = 64
S5_LANES = S5_GROUPS * S5_STATE
S5_KBLK = 8
RET_HEADS = 4
RET_DK = 256
RET_CHUNK = 128
ROPE_BASE = 10000.0
SGU_WIDTH = 2048
SGU_GROUPS = 4
SGU_GDIM = 512
SGU_CHUNK = 128
EVEN_IN = 6144
ODD_IN = 6144
ADAM_LR = 0.001
ADAM_B1 = 0.9
ADAM_B2 = 0.999
ADAM_EPS = 1e-08
ADAM_WD = 0.01
ADAM_STEP = 10
N_CHIPS = 4
VMEM_LIMIT = 56 * 1024 * 1024

TL_PROJ = 512
TL_DW = 1024
TL_S5 = 256
TL_SGU = 256


def _cparams(sem, **kw):
    return pltpu.CompilerParams(dimension_semantics=sem, vmem_limit_bytes=VMEM_LIMIT, **kw)


def _mm(a, b):
    return jnp.dot(a.astype(MXU_DTYPE), b.astype(MXU_DTYPE), preferred_element_type=F32)


def _mm_nt(a, b):
    return lax.dot_general(a.astype(MXU_DTYPE), b.astype(MXU_DTYPE),
                           (((1,), (1,)), ((), ())), preferred_element_type=F32)


def _mm_tn(a, b):
    return lax.dot_general(a.astype(MXU_DTYPE), b.astype(MXU_DTYPE),
                           (((0,), (0,)), ((), ())), preferred_element_type=F32)


_GELU_C = math.sqrt(2.0 / math.pi)


def _gelu_parts(x):
    x2 = x * x
    th = jnp.tanh(x * (_GELU_C + (_GELU_C * 0.044715) * x2))
    hx = 0.5 * x
    return hx + hx * th, th, x2, hx


def _gelu(x):
    return _gelu_parts(x)[0]


def _gelu_and_grad(x):
    g, th, x2, hx = _gelu_parts(x)
    return g, (0.5 + 0.5 * th) + hx * (1.0 - th * th) * (_GELU_C + (3.0 * _GELU_C * 0.044715) * x2)


def _gelu_grad(x):
    return _gelu_and_grad(x)[1]


def _sigmoid(x):
    return 1.0 / (1.0 + jnp.exp(-x))


def _silu_and_grad(x):
    s = _sigmoid(x)
    return x * s, s * (1.0 + x * (1.0 - s))


def _rms(x):
    return lax.rsqrt(jnp.mean(x * x, axis=-1, keepdims=True) + NORM_EPS)


def _full(shape):
    nd = len(shape)
    return pl.BlockSpec(shape, lambda *_: (0,) * nd)


MESH = pl.DeviceIdType.MESH
ANY = pl.BlockSpec(memory_space=pl.ANY)


def _place():
    return lax.axis_index("x"), lax.axis_index("y"), lax.axis_index("c")


def _chip_peer(x, y, c, d):
    return (1 - x if d >= 2 else x, 1 - y if d % 2 else y, c)


class _Plan:
    def __init__(self, inputs, out_shape, build):
        self.inputs, self.out_shape, self._build = list(inputs), list(out_shape), build
        n = len(self.inputs)
        self.sems = [pltpu.SemaphoreType.DMA((n, 3)), pltpu.SemaphoreType.DMA((n, 3)), pltpu.SemaphoreType.DMA((n,))]

    def start(self, in_refs, out_refs, sems):
        send, recv, local = self._build(in_refs, out_refs, sems)
        for p in range(len(self.inputs)):
            local[p].start()
            for cp in send[p]:
                cp.start()

    def wait(self, in_refs, out_refs, sems):
        send, recv, local = self._build(in_refs, out_refs, sems)
        for p in range(len(self.inputs)):
            for cp in recv[p]:
                cp.wait_recv()
        for p in range(len(self.inputs)):
            for cp in send[p]:
                cp.wait_send()
            local[p].wait()


class _GatherPlan:
    def __init__(self, shards, only=None):
        n = len(shards)
        self.n, self.only = n, only
        self.peers = (1, 2, 3) if only is None else (only,)
        self.inputs = list(shards)
        slabs = N_CHIPS if only is None else 1
        self.out_shape = [jax.ShapeDtypeStruct((slabs,) + s.shape, s.dtype) for s in shards]
        self.halved = [s.shape[0] % 32 == 0 for s in shards]
        self.sems = [pltpu.SemaphoreType.DMA((n, 3)) for _ in range(4)] + [pltpu.SemaphoreType.DMA((n,))]

    def _copies(self, in_refs, out_refs, sems):
        ici_s, ici_r, d2d_s, d2d_r, loc = sems
        x, y, c = _place()
        me = 2 * x + y

        def rows(p, core):
            if not self.halved[p]:
                return slice(None)
            half = self.inputs[p].shape[0] // 2
            return pl.ds(pl.multiple_of(core * half, 16), half)

        def slab(chip):
            return chip if self.only is None else 0

        def ici(p, d, chip, core):
            return pltpu.make_async_remote_copy(
                src_ref=in_refs[p].at[rows(p, core)], dst_ref=out_refs[p].at[slab(chip), rows(p, core)],
                send_sem=ici_s.at[p, d - 1], recv_sem=ici_r.at[p, d - 1],
                device_id=_chip_peer(x, y, c, d), device_id_type=MESH)

        def d2d(p, d, core):
            part = out_refs[p].at[slab(me ^ d), rows(p, core)]
            return pltpu.make_async_remote_copy(
                src_ref=part, dst_ref=part, send_sem=d2d_s.at[p, d - 1], recv_sem=d2d_r.at[p, d - 1],
                device_id=(x, y, 1 - c), device_id_type=MESH)

        local = [pltpu.make_async_copy(in_refs[p], out_refs[p].at[slab(me)], loc.at[p]) for p in range(self.n)]
        return me, c, ici, d2d, local

    def start(self, in_refs, out_refs, sems):
        me, c, ici, d2d, local = self._copies(in_refs, out_refs, sems)
        for p in range(self.n):
            if self.only is None:
                local[p].start()
            for d in self.peers:
                ici(p, d, me, c).start()

    def wait(self, in_refs, out_refs, sems):
        me, c, ici, d2d, local = self._copies(in_refs, out_refs, sems)
        for p in range(self.n):
            for d in self.peers:
                ici(p, d, me ^ d, c).wait_recv()
                if self.halved[p]:
                    d2d(p, d, c).start()
        for p in range(self.n):
            for d in self.peers:
                if self.halved[p]:
                    d2d(p, d, 1 - c).wait_recv()
                    d2d(p, d, c).wait_send()
                ici(p, d, me, c).wait_send()
            if self.only is None:
                local[p].wait()


def gather_plan(shards, only=None):
    return _GatherPlan(shards, only)


def reduce_plan(shards, whole=()):
    n_s = len(shards)

    def build(in_refs, out_refs, sems):
        send_sems, recv_sems, loc_sems = sems
        x, y, c = _place()
        me = 2 * x + y

        def src(p, slab):
            return in_refs[p].at[slab] if p < n_s else in_refs[p]

        def remote(p, d):
            return pltpu.make_async_remote_copy(
                src_ref=src(p, me ^ d), dst_ref=out_refs[p].at[d], send_sem=send_sems.at[p, d - 1],
                recv_sem=recv_sems.at[p, d - 1], device_id=_chip_peer(x, y, c, d), device_id_type=MESH)

        n = len(in_refs)
        send = [[remote(p, d) for d in (1, 2, 3)] for p in range(n)]
        local = [pltpu.make_async_copy(src(p, me), out_refs[p].at[0], loc_sems.at[p]) for p in range(n)]
        return send, send, local

    outs = [jax.ShapeDtypeStruct(s.shape, s.dtype) for s in shards]
    outs += [jax.ShapeDtypeStruct((N_CHIPS,) + a.shape, a.dtype) for a in whole]
    return _Plan(list(shards) + list(whole), outs, build)


class _SiblingPlan:
    def __init__(self, arrs):
        self.inputs = list(arrs)
        self.out_shape = [jax.ShapeDtypeStruct(a.shape, a.dtype) for a in arrs]
        n = len(arrs)
        self.sems = [pltpu.SemaphoreType.DMA((n,)), pltpu.SemaphoreType.DMA((n,))]

    def _copies(self, in_refs, out_refs, sems):
        x, y, c = _place()
        return [pltpu.make_async_remote_copy(
            src_ref=in_refs[p], dst_ref=out_refs[p], send_sem=sems[0].at[p], recv_sem=sems[1].at[p],
            device_id=(x, y, 1 - c), device_id_type=MESH) for p in range(len(self.inputs))]

    def start(self, in_refs, out_refs, sems):
        for cp in self._copies(in_refs, out_refs, sems):
            cp.start()

    def wait(self, in_refs, out_refs, sems):
        copies = self._copies(in_refs, out_refs, sems)
        for cp in copies:
            cp.wait_recv()
        for cp in copies:
            cp.wait_send()


def run_plan(plan, name):
    n = len(plan.inputs)

    def body(*refs):
        plan.start(refs[:n], refs[n:2 * n], refs[2 * n:])
        plan.wait(refs[:n], refs[n:2 * n], refs[2 * n:])

    return pl.pallas_call(body, name=name, in_specs=[ANY] * n, out_specs=[ANY] * n, out_shape=plan.out_shape,
                          scratch_shapes=plan.sems)(*plan.inputs)


def _call(body, plan, *, name, grid, in_specs, out_specs, out_shape, sem, scratch_shapes=(), aliases=None,
          n_prefetch=0):
    aliases = {} if aliases is None else aliases
    single = not isinstance(out_shape, (list, tuple))
    out_specs = [out_specs] if single else list(out_specs)
    out_shape = [out_shape] if single else list(out_shape)
    n_in, n_out, n_scr = len(in_specs), len(out_specs), len(scratch_shapes)
    ci = 0 if plan is None else len(plan.inputs)
    co = 0 if plan is None else len(plan.out_shape)

    def hosted(*refs):
        pre, refs = refs[:n_prefetch], refs[n_prefetch:]
        ins, cins = refs[:n_in], refs[n_in:n_in + ci]
        k = n_in + ci
        outs, couts = refs[k:k + n_out], refs[k + n_out:k + n_out + co]
        k += n_out + co
        scr, sems = refs[k:k + n_scr], refs[k + n_scr:]
        ids = [pl.program_id(a) for a in range(len(grid))]
        first = functools.reduce(jnp.logical_and, [i == 0 for i in ids])
        last = functools.reduce(jnp.logical_and, [i == g - 1 for i, g in zip(ids, grid)])

        @pl.when(first)
        def _():
            plan.start(cins, couts, sems)

        body(*pre, *ins, *outs, *scr)

        @pl.when(last)
        def _():
            plan.wait(cins, couts, sems)

    def run(*args):
        hosting = plan is not None
        spec = pltpu.PrefetchScalarGridSpec(
            num_scalar_prefetch=n_prefetch, grid=grid,
            in_specs=list(in_specs) + ([ANY] * ci if hosting else []),
            out_specs=out_specs + ([ANY] * co if hosting else []),
            scratch_shapes=list(scratch_shapes) + (plan.sems if hosting else []))
        res = pl.pallas_call(hosted if hosting else body, name=name, grid_spec=spec,
                             out_shape=out_shape + (plan.out_shape if hosting else []),
                             input_output_aliases=aliases, compiler_params=_cparams(sem),
                             )(*args, *(plan.inputs if hosting else []))
        return (res[0] if single else res[:n_out]), list(res[n_out:])

    return run


def norm_matmul(x, g, w, name, plan=None, tn=None):
    L, D = x.shape
    tl = min(TL_DW, L)
    if w.ndim == 3:
        nt, _, tn = w.shape
        w_spec = pl.BlockSpec((1, D, tn), lambda i, n: (n, 0, 0))
    else:
        nt = w.shape[1] // tn
        w_spec = pl.BlockSpec((D, tn), lambda i, n: (0, n))

    def body(x_ref, g_ref, w_ref, o_ref, h_ref):
        xv = x_ref[...]
        h = (xv * _rms(xv) * g_ref[...]).astype(h_ref.dtype)
        h_ref[...] = h
        o_ref[...] = _mm(h, w_ref[0] if w.ndim == 3 else w_ref[...])

    return _call(
        body, plan, name=name, grid=(L // tl, nt),
        in_specs=[pl.BlockSpec((tl, D), lambda i, n: (i, 0)), _full((1, D)), w_spec],
        out_specs=[pl.BlockSpec((tl, tn), lambda i, n: (i, n)), pl.BlockSpec((tl, D), lambda i, n: (i, 0))],
        out_shape=[jax.ShapeDtypeStruct((L, nt * tn), F32), jax.ShapeDtypeStruct((L, D), MXU_DTYPE)],
        sem=("arbitrary", "arbitrary"),
    )(x, g, w)


def even_in_slabs(x, xs, g, w, slabs, wsel, name, p_in=None, plan=None):
    L, D = x.shape
    tl = min(TL_PROJ, L)
    wb = EVEN_IN // N_CHIPS
    n = slabs.shape[0]
    s5_cols = 2 * S5_WIDTH - wb
    first = p_in is None

    def body(slabs_ref, wsel_ref, xs_ref, x_ref, g_ref, w_ref, *rest):
        o_ref = rest[-3] if first else rest[-1]
        j = slabs_ref[pl.program_id(0)]
        hs = (xs_ref[...] * _rms(xs_ref[...]) * g_ref[...]).astype(MXU_DTYPE)
        h = (x_ref[...] * _rms(x_ref[...]) * g_ref[...]).astype(MXU_DTYPE)
        if first:
            rest[-2][...] = hs
            rest[-1][...] = h
        o_ref[:, :s5_cols] = _mm(jnp.where(j <= 1, hs, h), w_ref[0, :, :s5_cols])
        o_ref[:, s5_cols:] = _mm(jnp.where(j == 0, hs, h), w_ref[0, :, s5_cols:])

    row = pl.BlockSpec((tl, D), lambda s, i, slabs_ref, wsel_ref: (i, 0))
    in_specs = [row if first else
                pl.BlockSpec((tl, D), lambda s, i, slabs_ref, wsel_ref: (jnp.where(slabs_ref[s] <= 1, i, 0), 0)),
                row if first else
                pl.BlockSpec((tl, D), lambda s, i, slabs_ref, wsel_ref: (jnp.where(slabs_ref[s] >= 1, i, 0), 0)),
                pl.BlockSpec((1, D), lambda s, i, slabs_ref, wsel_ref: (0, 0)),
                pl.BlockSpec((1, D, wb), lambda s, i, slabs_ref, wsel_ref: (wsel_ref[s], 0, 0))]
    out_specs = [pl.BlockSpec((tl, wb), lambda s, i, slabs_ref, wsel_ref: (i, slabs_ref[s]))]
    out_shape = [jax.ShapeDtypeStruct((L, EVEN_IN), F32)]
    args = [slabs, wsel, xs, x, g, w]
    if first:
        out_specs += [row, row]
        out_shape += [jax.ShapeDtypeStruct((L, D), MXU_DTYPE)] * 2
    else:
        in_specs.append(ANY)
        args.append(p_in)
    return _call(body, plan, name=name, grid=(n, L // tl), in_specs=in_specs, out_specs=out_specs,
                 out_shape=out_shape, sem=("arbitrary", "arbitrary"), n_prefetch=2,
                 aliases={} if first else {6: 0})(*args)


def matmul_residual(ys, w, x, name):
    L, D = x.shape
    tl = min(TL_PROJ, L)
    n = len(ys)
    offs = np.cumsum([0] + [y.shape[1] for y in ys])

    def body(*refs):
        y_refs, w_ref, x_ref, o_ref = refs[:n], refs[n], refs[n + 1], refs[n + 2]
        acc = x_ref[...]
        for k in range(n):
            acc = acc + _mm(y_refs[k][...], w_ref[offs[k]:offs[k + 1], :])
        o_ref[...] = acc

    return pl.pallas_call(
        body, name=name, grid=(L // tl,),
        in_specs=[pl.BlockSpec((tl, y.shape[1]), lambda i: (i, 0)) for y in ys]
        + [_full(w.shape), pl.BlockSpec((tl, D), lambda i: (i, 0))],
        out_specs=pl.BlockSpec((tl, D), lambda i: (i, 0)),
        out_shape=jax.ShapeDtypeStruct((L, D), F32),
        compiler_params=_cparams(("arbitrary",)),
    )(*ys, w, x)


def out_proj_loss(y, w, x, gf, tgt, name):
    L, K = y.shape
    D = w.shape[1]
    tl = min(TL_PROJ, L)

    def body(y_ref, w_ref, x_ref, gf_ref, t_ref, dx_ref, loss_ref, dg_ref):
        @pl.when(pl.program_id(0) == 0)
        def _():
            loss_ref[...] = jnp.zeros_like(loss_ref)
            dg_ref[...] = jnp.zeros_like(dg_ref)

        x2 = x_ref[...] + _mm(y_ref[...], w_ref[...])
        r = _rms(x2)
        xn = x2 * r
        e = xn * gf_ref[...] - t_ref[...]
        loss_ref[...] += (0.5 / D) * jnp.sum(e * e)
        dout = e * (1.0 / D)
        dg_ref[...] += jnp.sum(dout * xn, axis=0, keepdims=True)
        dxn = dout * gf_ref[...]
        dx_ref[...] = r * (dxn - xn * jnp.mean(dxn * xn, axis=-1, keepdims=True))

    return pl.pallas_call(
        body, name=name, grid=(L // tl,),
        in_specs=[pl.BlockSpec((tl, K), lambda i: (i, 0)), _full((K, D)),
                  pl.BlockSpec((tl, D), lambda i: (i, 0)), _full((1, D)),
                  pl.BlockSpec((tl, D), lambda i: (i, 0))],
        out_specs=[pl.BlockSpec((tl, D), lambda i: (i, 0)), _full((8, 128)), _full((1, D))],
        out_shape=[jax.ShapeDtypeStruct((L, D), F32), jax.ShapeDtypeStruct((8, 128), F32),
                   jax.ShapeDtypeStruct((1, D), F32)],
        compiler_params=_cparams(("arbitrary",)),
    )(y, w, x, gf, tgt)


def out_proj_bwd(dx, w, ys, name):
    L, D = dx.shape
    K = w.shape[0]
    tl = min(TL_PROJ, L)
    n = len(ys)
    offs = np.cumsum([0] + [y.shape[1] for y in ys])

    def body(*refs):
        dx_ref, w_ref, y_refs = refs[0], refs[1], refs[2:2 + n]
        dy_refs, dw_ref = refs[2 + n:2 + 2 * n], refs[2 + 2 * n]

        @pl.when(pl.program_id(0) == 0)
        def _():
            dw_ref[...] = jnp.zeros_like(dw_ref)

        dxv = dx_ref[...]
        for k in range(n):
            dy_refs[k][...] = _mm_nt(dxv, w_ref[offs[k]:offs[k + 1], :])
            dw_ref[offs[k]:offs[k + 1], :] += _mm_tn(y_refs[k][...], dxv)

    y_specs = [pl.BlockSpec((tl, y.shape[1]), lambda i: (i, 0)) for y in ys]
    return pl.pallas_call(
        body, name=name, grid=(L // tl,),
        in_specs=[pl.BlockSpec((tl, D), lambda i: (i, 0)), _full((K, D))] + y_specs,
        out_specs=y_specs + [_full((K, D))],
        out_shape=[jax.ShapeDtypeStruct(y.shape, F32) for y in ys] + [jax.ShapeDtypeStruct((K, D), F32)],
        compiler_params=_cparams(("arbitrary",)),
    )(dx, w, *ys)


def in_proj_bwd_dx(x, g, dps, ws, dres, name, plan=None):
    L, D = x.shape
    tl = min(TL_PROJ, L)
    n = len(dps)

    def body(*refs):
        x_ref, g_ref, dres_ref = refs[:3]
        dp_refs, w_refs = refs[3:3 + n], refs[3 + n:3 + 2 * n]
        dx_ref, dg_ref = refs[3 + 2 * n:]

        @pl.when(pl.program_id(0) == 0)
        def _():
            dg_ref[...] = jnp.zeros_like(dg_ref)

        dh = None
        for dp_ref, w_ref, w in zip(dp_refs, w_refs, ws):
            if w.ndim == 3:
                tn = w.shape[2]
                parts = [_mm_nt(dp_ref[:, tn * k:tn * (k + 1)], w_ref[k]) for k in range(w.shape[0])]
            else:
                parts = [_mm_nt(dp_ref[...], w_ref[...])]
            for part in parts:
                dh = part if dh is None else dh + part
        xv = x_ref[...]
        r = _rms(xv)
        xn = xv * r
        dg_ref[...] += jnp.sum(dh * xn, axis=0, keepdims=True)
        dxn = dh * g_ref[...]
        dx_ref[...] = dres_ref[...] + r * (dxn - xn * jnp.mean(dxn * xn, axis=-1, keepdims=True))

    return _call(
        body, plan, name=name, grid=(L // tl,),
        in_specs=[pl.BlockSpec((tl, D), lambda i: (i, 0)), _full((1, D)), pl.BlockSpec((tl, D), lambda i: (i, 0))]
        + [pl.BlockSpec((tl, dp.shape[1]), lambda i: (i, 0)) for dp in dps] + [_full(w.shape) for w in ws],
        out_specs=[pl.BlockSpec((tl, D), lambda i: (i, 0)), _full((1, D))],
        out_shape=[jax.ShapeDtypeStruct((L, D), F32), jax.ShapeDtypeStruct((1, D), F32)],
        sem=("arbitrary",),
    )(x, g, dres, *dps, *ws)


def in_proj_bwd_dw(h, dp, name, tn, first=0, into=None, dtype=F32, plan=None, dp_first=0, count=None):
    L, D = h.shape
    tl = min(TL_DW, L)
    wb = EVEN_IN // N_CHIPS
    per = wb // tn
    count = dp.shape[1] // tn if count is None else count
    last = L // tl - 1

    def body(*refs):
        h_ref, dp_ref, dw_ref, acc = refs[0], refs[1], refs[-2], refs[-1]

        @pl.when(pl.program_id(1) == 0)
        def _():
            acc[...] = jnp.zeros_like(acc)

        acc[...] += _mm_tn(h_ref[...], dp_ref[...])

        @pl.when(pl.program_id(1) == last)
        def _():
            dw_ref[0] = acc[...].astype(dw_ref.dtype)

    ins = [h, dp] + ([] if into is None else [into])
    return _call(
        body, plan, name=name, grid=(count, L // tl),
        in_specs=[pl.BlockSpec((tl, D), lambda n, i: (i, 0)), pl.BlockSpec((tl, tn), lambda n, i: (i, n + dp_first))]
        + ([] if into is None else [ANY]),
        out_specs=pl.BlockSpec((1, D, tn), lambda n, i: ((n + first) // per, 0, (n + first) % per)),
        out_shape=jax.ShapeDtypeStruct((N_CHIPS, D, wb), dtype),
        scratch_shapes=[pltpu.VMEM((D, tn), F32)],
        aliases={} if into is None else {2: 0},
        sem=("arbitrary", "arbitrary"),
    )(*ins)


def _s5_param_fn(lam_re, lam_im, log_dt, b_re, b_im):
    lr = jnp.minimum(lam_re, -1e-4)
    li = lam_im
    dt = jnp.exp(log_dt)
    mag = jnp.exp(lr * dt)
    ab_re = mag * jnp.cos(li * dt)
    ab_im = mag * jnp.sin(li * dt)
    den = lr * lr + li * li
    n_re = ab_re - 1.0
    n_im = ab_im
    z_re = (n_re * lr + n_im * li) / den
    z_im = (n_im * lr - n_re * li) / den
    bb_re = z_re[None] * b_re - z_im[None] * b_im
    bb_im = z_re[None] * b_im + z_im[None] * b_re
    return ab_re, ab_im, bb_re, bb_im


def s5_params_fwd(lam_re, lam_im, log_dt, b_re, b_im, span):
    G, P = lam_re.shape
    H = b_re.shape[0]
    assert span & (span - 1) == 0

    def body(lr_ref, li_ref, dt_ref, br_ref, bi_ref, abr_ref, abi_ref, bbr_ref, bbi_ref, pr_ref, pi_ref):
        ab_re, ab_im, bb_re, bb_im = _s5_param_fn(lr_ref[...], li_ref[...], dt_ref[...], br_ref[...], bi_ref[...])
        abr_ref[...] = ab_re
        abi_ref[...] = ab_im
        bbr_ref[...] = bb_re
        bbi_ref[...] = bb_im
        cr, ci = ab_re, ab_im
        for _ in range(span.bit_length() - 1):
            cr, ci = cr * cr - ci * ci, 2.0 * cr * ci
        pr_ref[...] = cr
        pi_ref[...] = ci

    shp = lambda *s: jax.ShapeDtypeStruct(s, F32)
    return pl.pallas_call(
        body, name="s5_params_fwd",
        out_shape=[shp(G, P), shp(G, P), shp(H, G, P), shp(H, G, P), shp(G, P), shp(G, P)],
    )(lam_re, lam_im, log_dt, b_re, b_im)


def s5_params_bwd(lam_re, lam_im, log_dt, b_re, b_im, d_ab_re, d_ab_im, d_bb_re, d_bb_im):
    G, P = lam_re.shape
    H = b_re.shape[0]

    def body(lr_ref, li_ref, dt_ref, br_ref, bi_ref, g0, g1, g2, g3, o0, o1, o2, o3, o4):
        prim = (lr_ref[...], li_ref[...], dt_ref[...], br_ref[...], bi_ref[...])
        _, vjp = jax.vjp(_s5_param_fn, *prim)
        d = vjp((jnp.sum(g0[...], axis=0), jnp.sum(g1[...], axis=0), g2[...], g3[...]))
        o0[...], o1[...], o2[...], o3[...], o4[...] = d

    shp = lambda *s: jax.ShapeDtypeStruct(s, F32)
    return pl.pallas_call(
        body, name="s5_params_bwd",
        out_shape=[shp(G, P), shp(G, P), shp(G, 1), shp(H, G, P), shp(H, G, P)],
    )(lam_re, lam_im, log_dt, b_re, b_im, d_ab_re, d_ab_im, d_bb_re, d_bb_im)


def stream_order(a, tl):
    L, C = a.shape
    return a.reshape(L // tl, 8, tl // 8, C).transpose(0, 2, 1, 3).reshape(L, C)


def token_order(a, tl):
    L, C = a.shape
    return a.reshape(L // tl, tl // 8, 8, C).transpose(0, 2, 1, 3).reshape(L, C)


_LANE_BLK = 1024
_LANE_BLK_BWD = 1024


def _cmul_add(ar, ai, xr, xi, br, bi):
    return br + (ar * xr - ai * xi), bi + (ar * xi + ai * xr)


def _cmulc_add(ar, ai, xr, xi, br, bi):
    return br + (ar * xr + ai * xi), bi + (ar * xi - ai * xr)


def _s5_states(u, wbd_ref, a_re, a_im, at_re, at_im, s_re, s_im, e_re, e_im, c0_re, c0_im, tl):
    t8 = tl // 8
    for k in range(S5_KBLK):
        bu = _mm(u[:, 128 * k:128 * (k + 1)], wbd_ref[k])
        s_re[:, 512 * k:512 * (k + 1)] = bu[:, :512]
        s_im[:, 512 * k:512 * (k + 1)] = bu[:, 512:]
    outs_re, outs_im = [], []
    for b in range(S5_LANES // _LANE_BLK):
        lanes = slice(_LANE_BLK * b, _LANE_BLK * (b + 1))
        ar = jnp.broadcast_to(a_re[:, lanes], (8, _LANE_BLK))
        ai = jnp.broadcast_to(a_im[:, lanes], (8, _LANE_BLK))

        def local(i, carry, lanes=lanes, ar=ar, ai=ai):
            r = pl.multiple_of(i * 8, 8)
            sr, si = _cmul_add(ar, ai, carry[0], carry[1], s_re[pl.ds(r, 8), lanes], s_im[pl.ds(r, 8), lanes])
            s_re[pl.ds(r, 8), lanes] = sr
            s_im[pl.ds(r, 8), lanes] = si
            return sr, si

        zero = jnp.zeros((8, _LANE_BLK), F32)
        fr, fi = lax.fori_loop(0, t8, local, (zero, zero), unroll=True)
        tr, ti = at_re[:, lanes], at_im[:, lanes]
        er, ei = c0_re[:, lanes], c0_im[:, lanes]
        ers, eis = [er], [ei]
        for j in range(8):
            er, ei = _cmul_add(tr, ti, er, ei, fr[j:j + 1], fi[j:j + 1])
            ers.append(er)
            eis.append(ei)
        outs_re.append(ers[8])
        outs_im.append(eis[8])
        ent_r, ent_i = jnp.concatenate(ers[:8], axis=0), jnp.concatenate(eis[:8], axis=0)
        e_re[:, lanes] = ent_r
        e_im[:, lanes] = ent_i

        def fix(i, carry, lanes=lanes, ar=ar, ai=ai):
            r = pl.multiple_of(i * 8, 8)
            zr, zi = ar * carry[0] - ai * carry[1], ar * carry[1] + ai * carry[0]
            s_re[pl.ds(r, 8), lanes] = s_re[pl.ds(r, 8), lanes] + zr
            s_im[pl.ds(r, 8), lanes] = s_im[pl.ds(r, 8), lanes] + zi
            return zr, zi

        lax.fori_loop(0, t8, fix, (ent_r, ent_i), unroll=True)
    return jnp.concatenate(outs_re, axis=1), jnp.concatenate(outs_im, axis=1)


def _s5_readout(s_re, s_im, cre_ref, cim_ref):
    ys = []
    for k in range(S5_KBLK):
        lanes = slice(512 * k, 512 * (k + 1))
        ys.append(_mm(s_re[:, lanes], cre_ref[k]) - _mm(s_im[:, lanes], cim_ref[k]))
    return jnp.concatenate(ys, axis=1)


def s5_forward(p, wbd, cre, cim, atab, d_skip, w_glu, b_glu, plan=None):
    L = p.shape[0]
    tl = min(TL_S5, L)
    nch = L // tl

    def body(u_ref, z_ref, wbd_ref, cre_ref, cim_ref, at_ref, d_ref, wg_ref, bg_ref,
             ya_ref, st_re_ref, st_im_ref, sv_re_ref, sv_im_ref, s_re, s_im, e_re, e_im, car_re, car_im):
        @pl.when(pl.program_id(0) == 0)
        def _():
            car_re[...] = jnp.zeros_like(car_re)
            car_im[...] = jnp.zeros_like(car_im)

        c0_re, c0_im = car_re[...], car_im[...]
        st_re_ref[0] = c0_re
        st_im_ref[0] = c0_im
        u = u_ref[...]
        x_re, x_im = _s5_states(u, wbd_ref, at_ref[0:1], at_ref[1:2], at_ref[2:3], at_ref[3:4],
                                s_re, s_im, e_re, e_im, c0_re, c0_im, tl)
        car_re[...] = x_re
        car_im[...] = x_im
        sv_re_ref[...] = s_re[...].astype(sv_re_ref.dtype)
        sv_im_ref[...] = s_im[...].astype(sv_im_ref.dtype)
        y = _s5_readout(sv_re_ref, sv_im_ref, cre_ref, cim_ref) + d_ref[...] * u
        yg = _gelu(y)
        gate = _sigmoid(_mm(yg, wg_ref[...]) + bg_ref[...])
        sz, _ = _silu_and_grad(z_ref[...])
        ya_ref[...] = (yg * gate * sz).astype(ya_ref.dtype)

    return _call(
        body, plan, name="s5_forward", grid=(nch,),
        in_specs=[pl.BlockSpec((tl, 1024), lambda i: (i, 0)), pl.BlockSpec((tl, 1024), lambda i: (i, 1)),
                  _full(wbd.shape), _full(cre.shape), _full(cim.shape), _full(atab.shape),
                  _full((1, 1024)), _full((1024, 1024)), _full((1, 1024))],
        out_specs=[pl.BlockSpec((tl, 1024), lambda i: (i, 0)),
                   pl.BlockSpec((1, 1, S5_LANES), lambda i: (i, 0, 0)),
                   pl.BlockSpec((1, 1, S5_LANES), lambda i: (i, 0, 0)),
                   pl.BlockSpec((tl, S5_LANES), lambda i: (i, 0)), pl.BlockSpec((tl, S5_LANES), lambda i: (i, 0))],
        out_shape=[jax.ShapeDtypeStruct((L, 1024), MXU_DTYPE),
                   jax.ShapeDtypeStruct((nch, 1, S5_LANES), F32), jax.ShapeDtypeStruct((nch, 1, S5_LANES), F32),
                   jax.ShapeDtypeStruct((L, S5_LANES), MXU_DTYPE), jax.ShapeDtypeStruct((L, S5_LANES), MXU_DTYPE)],
        scratch_shapes=[pltpu.VMEM((tl, S5_LANES), F32), pltpu.VMEM((tl, S5_LANES), F32),
                        pltpu.VMEM((8, S5_LANES), F32), pltpu.VMEM((8, S5_LANES), F32),
                        pltpu.VMEM((1, S5_LANES), F32), pltpu.VMEM((1, S5_LANES), F32)],
        sem=("arbitrary",),
    )(p, p, wbd, cre, cim, atab, d_skip, w_glu, b_glu)


def s5_backward(p, dya, st_re, st_im, sv_re, sv_im, wbd, cre, cim, atab, d_skip, w_glu, b_glu, plan=None):
    L = p.shape[0]
    tl = min(TL_S5, L)
    t8 = tl // 8
    nch = L // tl
    rev = lambda i: (nch - 1 - i, 0)
    rev1 = lambda i: (nch - 1 - i, 1)
    rev3 = lambda i: (nch - 1 - i, 0, 0)
    ct_shape = (S5_KBLK, cre.shape[2], cre.shape[1])

    def body(u_ref, z_ref, dya_ref, str_ref, sti_ref, s_re, s_im, wbd_ref, cre_ref, cim_ref, at_ref,
             d_ref, wg_ref, bg_ref,
             dp_ref, dwbd_ref, dcre_ref, dcim_ref, dabr_ref, dabi_ref, dd_ref, dwg_ref, dbg_ref,
             g_re, g_im, car_re, car_im):
        @pl.when(pl.program_id(0) == 0)
        def _():
            car_re[...] = jnp.zeros_like(car_re)
            car_im[...] = jnp.zeros_like(car_im)
            for r in (dwbd_ref, dcre_ref, dcim_ref, dabr_ref, dabi_ref, dd_ref, dwg_ref, dbg_ref):
                r[...] = jnp.zeros_like(r)

        u = u_ref[...]
        a_re, a_im, at_re, at_im = at_ref[0:1], at_ref[1:2], at_ref[2:3], at_ref[3:4]
        y = _s5_readout(s_re, s_im, cre_ref, cim_ref) + d_ref[...] * u
        yg, dyg = _gelu_and_grad(y)
        gate = _sigmoid(_mm(yg, wg_ref[...]) + bg_ref[...])
        sz, dsz = _silu_and_grad(z_ref[...])
        dya = dya_ref[...]
        s5out = yg * gate
        dp_ref[:, 1024:] = (dya * s5out * dsz).astype(dp_ref.dtype)
        ds5 = dya * sz
        dt = ds5 * yg * gate * (1.0 - gate)
        dwg_ref[...] += _mm_tn(yg, dt)
        dbg_ref[...] += jnp.sum(dt, axis=0, keepdims=True)
        dyv = (ds5 * gate + _mm_nt(dt, wg_ref[...])) * dyg
        dd_ref[...] += jnp.sum(dyv * u, axis=0, keepdims=True)

        for k in range(S5_KBLK):
            lanes = slice(512 * k, 512 * (k + 1))
            dyk = dyv[:, 128 * k:128 * (k + 1)]
            g_re[:, lanes] = _mm_nt(dyk, cre_ref[k])
            g_im[:, lanes] = -_mm_nt(dyk, cim_ref[k])
            dcre_ref[k] += _mm_tn(dyk, s_re[:, lanes])
            dcim_ref[k] -= _mm_tn(dyk, s_im[:, lanes])

        blk = _LANE_BLK_BWD
        for b in range(S5_LANES // blk):
            lanes = slice(blk * b, blk * (b + 1))
            ar = jnp.broadcast_to(a_re[:, lanes], (8, blk))
            ai = jnp.broadcast_to(a_im[:, lanes], (8, blk))

            def local(j, carry, lanes=lanes, ar=ar, ai=ai):
                r = pl.multiple_of((t8 - 1 - j) * 8, 8)
                gr, gi = _cmulc_add(ar, ai, carry[0], carry[1], g_re[pl.ds(r, 8), lanes], g_im[pl.ds(r, 8), lanes])
                g_re[pl.ds(r, 8), lanes] = gr
                g_im[pl.ds(r, 8), lanes] = gi
                return gr, gi

            zero = jnp.zeros((8, blk), F32)
            fr, fi = lax.fori_loop(0, t8, local, (zero, zero), unroll=True)
            tr, ti = at_re[:, lanes], at_im[:, lanes]
            hr, hi = car_re[:, lanes], car_im[:, lanes]
            hrs, his = [hr], [hi]
            for j in range(7, -1, -1):
                hr, hi = _cmulc_add(tr, ti, hr, hi, fr[j:j + 1], fi[j:j + 1])
                hrs.append(hr)
                his.append(hi)
            car_re[:, lanes] = hrs[8]
            car_im[:, lanes] = his[8]
            in_r = jnp.concatenate(hrs[7::-1], axis=0)
            in_i = jnp.concatenate(his[7::-1], axis=0)

            wr, wi, nr, ni, accr, acci = in_r, in_i, zero, zero, zero, zero
            for pair in range(t8 // 2 - 1, -1, -1):
                rows = slice(16 * pair, 16 * pair + 16)
                s16r, s16i = s_re[rows, lanes].astype(F32), s_im[rows, lanes].astype(F32)
                for half in (1, 0):
                    r = 16 * pair + 8 * half
                    sr, si = s16r[8 * half:8 * half + 8], s16i[8 * half:8 * half + 8]
                    accr, acci = accr + (sr * nr + si * ni), acci + (sr * ni - si * nr)
                    wr, wi = ar * wr + ai * wi, ar * wi - ai * wr
                    nr, ni = g_re[r:r + 8, lanes] + wr, g_im[r:r + 8, lanes] + wi
                    g_re[r:r + 8, lanes] = nr
                    g_im[r:r + 8, lanes] = ni
            lr, li = s_re[tl - 16:tl, lanes].astype(F32)[8:], s_im[tl - 16:tl, lanes].astype(F32)[8:]
            row0 = lax.broadcasted_iota(jnp.int32, (8, blk), 0) == 0
            sr = jnp.where(row0, jnp.broadcast_to(str_ref[0][:, lanes], (8, blk)), pltpu.roll(lr, 1, 0))
            si = jnp.where(row0, jnp.broadcast_to(sti_ref[0][:, lanes], (8, blk)), pltpu.roll(li, 1, 0))
            dabr_ref[:, lanes] += accr + (sr * nr + si * ni)
            dabi_ref[:, lanes] += acci + (sr * ni - si * nr)

        dus = []
        for k in range(S5_KBLK):
            lanes = slice(512 * k, 512 * (k + 1))
            g = jnp.concatenate([g_re[:, lanes], g_im[:, lanes]], axis=1)
            dwbd_ref[k] += _mm_tn(u[:, 128 * k:128 * (k + 1)], g)
            dus.append(_mm_nt(g, wbd_ref[k]))
        du = jnp.concatenate(dus, axis=1) + dyv * d_ref[...]
        dp_ref[:, :1024] = du.astype(dp_ref.dtype)

    shp = lambda *s: jax.ShapeDtypeStruct(s, F32)
    return _call(
        body, plan, name="s5_backward", grid=(nch,),
        in_specs=[pl.BlockSpec((tl, 1024), rev), pl.BlockSpec((tl, 1024), rev1), pl.BlockSpec((tl, 1024), rev),
                  pl.BlockSpec((1, 1, S5_LANES), rev3), pl.BlockSpec((1, 1, S5_LANES), rev3),
                  pl.BlockSpec((tl, S5_LANES), rev), pl.BlockSpec((tl, S5_LANES), rev),
                  _full(wbd.shape), _full(cre.shape), _full(cim.shape), _full(atab.shape),
                  _full((1, 1024)), _full((1024, 1024)), _full((1, 1024))],
        out_specs=[pl.BlockSpec((tl, 2048), rev), _full(wbd.shape), _full(ct_shape), _full(ct_shape),
                   _full((8, S5_LANES)), _full((8, S5_LANES)), _full((1, 1024)), _full((1024, 1024)), _full((1, 1024))],
        out_shape=[jax.ShapeDtypeStruct((L, 2048), MXU_DTYPE), shp(*wbd.shape), shp(*ct_shape), shp(*ct_shape),
                   shp(8, S5_LANES), shp(8, S5_LANES), shp(1, 1024), shp(1024, 1024), shp(1, 1024)],
        scratch_shapes=[pltpu.VMEM((tl, S5_LANES), F32), pltpu.VMEM((tl, S5_LANES), F32),
                        pltpu.VMEM((1, S5_LANES), F32), pltpu.VMEM((1, S5_LANES), F32)],
        sem=("arbitrary",),
    )(p, p, dya, st_re, st_im, sv_re, sv_im, wbd, cre, cim, atab, d_skip, w_glu, b_glu)


def _block_diag(w, rows_first):
    g8 = w.reshape(S5_KBLK, 8, w.shape[1], w.shape[2])
    eye = jnp.eye(8, dtype=w.dtype)
    out = jnp.einsum('kgab,fg->kfagb', g8, eye)
    return out.reshape(S5_KBLK, 8 * w.shape[1], 8 * w.shape[2])


def _block_diag_extract(wbd, a, b):
    w5 = wbd.reshape(S5_KBLK, 8, a, 8, b)
    idx = jnp.arange(8)
    return w5[:, idx, :, idx, :].transpose(1, 0, 2, 3).reshape(S5_GROUPS, a, b)


def _ret_constants():
    log_g = np.log1p(-np.exp2(-5.0 - np.arange(RET_HEADS, dtype=np.float32))).astype(np.float32)
    idx = np.arange(RET_CHUNK, dtype=np.float32)
    diff = idx[:, None] - idx[None, :]
    decay = np.where(diff >= 0, np.exp(log_g[:, None, None] * np.maximum(diff, 0.0)), 0.0).astype(np.float32)
    xi = np.exp(log_g[None, :] * (idx[:, None] + 1.0)).astype(np.float32)
    zeta = np.exp(log_g[None, :] * (RET_CHUNK - 1.0 - idx[:, None])).astype(np.float32)
    chunk_decay = np.exp(log_g * RET_CHUNK).astype(np.float32)
    return decay, xi, zeta, chunk_decay


def _rope_tables(L):
    half = RET_DK // 2
    inv = ROPE_BASE ** (-jnp.arange(half, dtype=F32) / half)
    ang = jnp.arange(L, dtype=F32)[:, None] * inv[None, :]
    return jnp.cos(ang), jnp.sin(ang)


def _rot(xh, cos, sin):
    x1, x2 = xh[:, :128], xh[:, 128:]
    return jnp.concatenate([x1 * cos - x2 * sin, x1 * sin + x2 * cos], axis=1)


def _rot_t(dh, cos, sin):
    d1, d2 = dh[:, :128], dh[:, 128:]
    return jnp.concatenate([d1 * cos + d2 * sin, d2 * cos - d1 * sin], axis=1)


RET_PER_STEP = 4


def _ret_setup(L):
    nc = L // RET_CHUNK
    per = RET_PER_STEP if nc % RET_PER_STEP == 0 else 1
    decay_np, xi_np, zeta_np, cd_np = _ret_constants()
    tables = (jnp.asarray(decay_np), jnp.asarray(np.tile(xi_np, (per, 1))), jnp.asarray(np.tile(zeta_np, (per, 1))))
    return nc // per, per, tables, [float(c) for c in cd_np]


def _ret_rows(q_ref, k_ref, v_ref, cos_ref, sin_ref, xi_ref, zeta_ref):
    H = range(RET_HEADS)
    hs = [slice(RET_DK * h, RET_DK * (h + 1)) for h in H]
    cs, sn = cos_ref[...], sin_ref[...]
    qh = [_rot(q_ref[:, hs[h]], cs, sn) for h in H]
    kh = [_rot(k_ref[:, hs[h]], cs, sn) * (RET_DK ** -0.5) for h in H]
    vh = [v_ref[:, hs[h]] for h in H]
    qx = [qh[h] * xi_ref[:, h:h + 1] for h in H]
    kz = [kh[h] * zeta_ref[:, h:h + 1] for h in H]
    return hs, cs, sn, qh, kh, vh, qx, kz


def _ret_normed(qh, kh, vh, qx, dec_ref, prevs, per):
    H, C = range(RET_HEADS), range(per)
    rs = [slice(RET_CHUNK * c, RET_CHUNK * (c + 1)) for c in C]
    sc = [[_mm_nt(qh[h][rs[c]], kh[h][rs[c]]) * dec_ref[h] for h in H] for c in C]
    inner = [[_mm(sc[c][h], vh[h][rs[c]]) for h in H] for c in C]
    cross = [[_mm(qx[h][rs[c]], prevs[c][h]) for h in H] for c in C]
    o = [jnp.concatenate([inner[c][h] + cross[c][h] for c in C], axis=0) for h in H]
    oc = [o[h] - jnp.mean(o[h], axis=-1, keepdims=True) for h in H]
    rstd = [lax.rsqrt(jnp.mean(oc[h] * oc[h], axis=-1, keepdims=True) + NORM_EPS) for h in H]
    on = [oc[h] * rstd[h] for h in H]
    return rs, sc, rstd, on


def retention_forward(p, cos, sin, gain):
    L = p.shape[0]
    steps, per, (decay, xi, zeta), cd = _ret_setup(L)
    rows = RET_CHUNK * per

    def body(q_ref, k_ref, v_ref, z_ref, cos_ref, sin_ref, dec_ref, xi_ref, zeta_ref, gain_ref,
             yb_ref, prev_ref, state):
        @pl.when(pl.program_id(0) == 0)
        def _():
            state[...] = jnp.zeros_like(state)

        H, C = range(RET_HEADS), range(per)
        hs, cs, sn, qh, kh, vh, qx, kz = _ret_rows(q_ref, k_ref, v_ref, cos_ref, sin_ref, xi_ref, zeta_ref)
        prevs = [[state[h] for h in H]]
        for c in C:
            rs_c = slice(RET_CHUNK * c, RET_CHUNK * (c + 1))
            prevs.append([prevs[c][h] * cd[h] + _mm_tn(kz[h][rs_c], vh[h][rs_c]) for h in H])
        _, _, _, on = _ret_normed(qh, kh, vh, qx, dec_ref, prevs, per)
        sz, _ = _silu_and_grad(z_ref[...])
        for h in H:
            for c in C:
                prev_ref[c, h] = prevs[c][h].astype(prev_ref.dtype)
            state[h] = prevs[per][h]
            yb_ref[:, hs[h]] = (on[h] * gain_ref[:, hs[h]] * sz[:, hs[h]]).astype(yb_ref.dtype)

    col0 = p.shape[1] // 1024 - 4
    blk = lambda c: pl.BlockSpec((rows, 1024), lambda i, c=c: (i, c + col0))
    return pl.pallas_call(
        body, name="retention_forward", grid=(steps,),
        in_specs=[blk(0), blk(1), blk(2), blk(3),
                  pl.BlockSpec((rows, 128), lambda i: (i, 0)), pl.BlockSpec((rows, 128), lambda i: (i, 0)),
                  _full(decay.shape), _full(xi.shape), _full(zeta.shape), _full((1, 1024))],
        out_specs=[pl.BlockSpec((rows, 1024), lambda i: (i, 0)),
                   pl.BlockSpec((per, RET_HEADS, RET_DK, RET_DK), lambda i: (i, 0, 0, 0))],
        out_shape=[jax.ShapeDtypeStruct((L, 1024), MXU_DTYPE),
                   jax.ShapeDtypeStruct((steps * per, RET_HEADS, RET_DK, RET_DK), MXU_DTYPE)],
        scratch_shapes=[pltpu.VMEM((RET_HEADS, RET_DK, RET_DK), F32)],
        compiler_params=_cparams(("arbitrary",)),
    )(p, p, p, p, cos, sin, decay, xi, zeta, gain)


def retention_backward(p, dy, prevs, cos, sin, gain, plan=None):
    L = p.shape[0]
    steps, per, (decay, xi, zeta), cd = _ret_setup(L)
    rows = RET_CHUNK * per
    scale = RET_DK ** -0.5

    def body(q_ref, k_ref, v_ref, z_ref, dyb_ref, prev_ref, cos_ref, sin_ref, dec_ref, xi_ref, zeta_ref, gain_ref,
             dp_ref, dgain_ref, dstate):
        @pl.when(pl.program_id(0) == 0)
        def _():
            dstate[...] = jnp.zeros_like(dstate)
            dgain_ref[...] = jnp.zeros_like(dgain_ref)

        H, C = range(RET_HEADS), range(per)
        hs, cs, sn, qh, kh, vh, qx, kz = _ret_rows(q_ref, k_ref, v_ref, cos_ref, sin_ref, xi_ref, zeta_ref)
        prevs = [[prev_ref[c, h] for h in H] for c in C]
        rs, sc, rstd, on = _ret_normed(qh, kh, vh, qx, dec_ref, prevs, per)
        sz, dsz = _silu_and_grad(z_ref[...])
        dyb = dyb_ref[...]
        dong = [dyb[:, hs[h]] * sz[:, hs[h]] for h in H]
        don = [dong[h] * gain_ref[:, hs[h]] for h in H]
        do = [rstd[h] * (don[h] - jnp.mean(don[h], axis=-1, keepdims=True)
                         - on[h] * jnp.mean(don[h] * on[h], axis=-1, keepdims=True)) for h in H]
        dsc = [[_mm_nt(do[h][rs[c]], vh[h][rs[c]]) * dec_ref[h] for h in H] for c in C]
        dq_st = [[_mm_nt(do[h][rs[c]], prevs[c][h]) for h in H] for c in C]
        dnew = [[_mm_tn(qx[h][rs[c]], do[h][rs[c]]) for h in H] for c in C]
        dsts = [None] * per + [[dstate[h] for h in H]]
        for c in reversed(C):
            dsts[c] = [dsts[c + 1][h] * cd[h] + dnew[c][h] for h in H]
        dk_st = [[_mm_nt(vh[h][rs[c]], dsts[c + 1][h]) for h in H] for c in C]
        dv_st = [[_mm(kz[h][rs[c]], dsts[c + 1][h]) for h in H] for c in C]
        rows_of = lambda parts: jnp.concatenate(parts, axis=0)
        dqh = [rows_of([_mm(dsc[c][h], kh[h][rs[c]]) for c in C])
               + rows_of([dq_st[c][h] for c in C]) * xi_ref[:, h:h + 1] for h in H]
        dkh = [rows_of([_mm_tn(dsc[c][h], qh[h][rs[c]]) for c in C])
               + rows_of([dk_st[c][h] for c in C]) * zeta_ref[:, h:h + 1] for h in H]
        dvh = [rows_of([_mm_tn(sc[c][h], do[h][rs[c]]) + dv_st[c][h] for c in C]) for h in H]
        for h in H:
            dstate[h] = dsts[0][h]
            dgain_ref[:, hs[h]] += jnp.sum(dong[h] * on[h], axis=0, keepdims=True)
            dp_ref[:, hs[h]] = (_rot_t(dkh[h], cs, sn) * scale).astype(dp_ref.dtype)
            dp_ref[:, 1024 + RET_DK * h:1024 + RET_DK * (h + 1)] = dvh[h].astype(dp_ref.dtype)
            dp_ref[:, 2048 + RET_DK * h:2048 + RET_DK * (h + 1)] = (
                dyb[:, hs[h]] * on[h] * gain_ref[:, hs[h]] * dsz[:, hs[h]]).astype(dp_ref.dtype)
            dp_ref[:, 3072 + RET_DK * h:3072 + RET_DK * (h + 1)] = _rot_t(dqh[h], cs, sn).astype(dp_ref.dtype)

    col0 = p.shape[1] // 1024 - 4
    blk = lambda c: pl.BlockSpec((rows, 1024), lambda i, c=c: (steps - 1 - i, c + col0))
    tab = pl.BlockSpec((rows, 128), lambda i: (steps - 1 - i, 0))
    return _call(
        body, plan, name="retention_backward", grid=(steps,),
        in_specs=[blk(0), blk(1), blk(2), blk(3), pl.BlockSpec((rows, 1024), lambda i: (steps - 1 - i, 0)),
                  pl.BlockSpec((per, RET_HEADS, RET_DK, RET_DK), lambda i: (steps - 1 - i, 0, 0, 0)),
                  tab, tab, _full(decay.shape), _full(xi.shape), _full(zeta.shape), _full((1, 1024))],
        out_specs=[pl.BlockSpec((rows, 4096), lambda i: (steps - 1 - i, 0)), _full((1, 1024))],
        out_shape=[jax.ShapeDtypeStruct((L, 4096), MXU_DTYPE), jax.ShapeDtypeStruct((1, 1024), F32)],
        scratch_shapes=[pltpu.VMEM((RET_HEADS, RET_DK, RET_DK), F32)],
        sem=("arbitrary",),
    )(p, p, p, p, dy, prevs, cos, sin, decay, xi, zeta, gain)


def _sgu_mix(p_ref, gain_ref, wm_ref, bt_ref, tl):
    pu, pv, z = p_ref[:, :2048], p_ref[:, 2048:4096], p_ref[:, 4096:]
    (u, du), (v, dv) = _gelu_and_grad(pu), _gelu_and_grad(pv)
    mu = jnp.mean(v, axis=-1, keepdims=True)
    vc = v - mu
    rstd = lax.rsqrt(jnp.mean(vc * vc, axis=-1, keepdims=True) + NORM_EPS)
    vn = vc * rstd
    vg = vn * gain_ref[...]
    mask = (lax.broadcasted_iota(jnp.int32, (SGU_CHUNK, SGU_CHUNK), 0)
            >= lax.broadcasted_iota(jnp.int32, (SGU_CHUNK, SGU_CHUNK), 1))
    wms = [jnp.where(mask, wm_ref[g], 0.0) for g in range(SGU_GROUPS)]
    rows = []
    for c in range(tl // SGU_CHUNK):
        rs = slice(SGU_CHUNK * c, SGU_CHUNK * (c + 1))
        cols = []
        for g in range(SGU_GROUPS):
            gs = slice(SGU_GDIM * g, SGU_GDIM * (g + 1))
            cols.append(_mm(wms[g], vg[rs, gs]) + bt_ref[:, g:g + 1])
        rows.append(jnp.concatenate(cols, axis=1))
    s = rows[0] if len(rows) == 1 else jnp.concatenate(rows, axis=0)
    return du, dv, z, u, vn, rstd, vg, wms, mask, s


def sgu_forward(p, gain, wm, bt):
    L = p.shape[0]
    tl = min(TL_SGU, L)

    def body(p_ref, gain_ref, wm_ref, bt_ref, y_ref):
        _, _, z, u, _, _, _, _, _, s = _sgu_mix(p_ref, gain_ref, wm_ref, bt_ref, tl)
        sz, _ = _silu_and_grad(z)
        y_ref[...] = (u * s * sz).astype(y_ref.dtype)

    return pl.pallas_call(
        body, name="sgu_forward", grid=(L // tl,),
        in_specs=[pl.BlockSpec((tl, ODD_IN), lambda i: (i, 0)), _full((1, 2048)), _full(wm.shape), _full(bt.shape)],
        out_specs=pl.BlockSpec((tl, 2048), lambda i: (i, 0)),
        out_shape=jax.ShapeDtypeStruct((L, 2048), MXU_DTYPE),
        compiler_params=_cparams(("arbitrary",)),
    )(p, gain, wm, bt)


def sgu_backward(p, dy, gain, wm, bt, plan=None):
    L = p.shape[0]
    tl = min(TL_SGU, L)

    def body(p_ref, dy_ref, gain_ref, wm_ref, bt_ref, dp_ref, dgain_ref, dwm_ref, dbt_ref):
        @pl.when(pl.program_id(0) == 0)
        def _():
            dgain_ref[...] = jnp.zeros_like(dgain_ref)
            dwm_ref[...] = jnp.zeros_like(dwm_ref)
            dbt_ref[...] = jnp.zeros_like(dbt_ref)

        gu, gv, z, u, vn, rstd, vg, wms, mask, s = _sgu_mix(p_ref, gain_ref, wm_ref, bt_ref, tl)
        sz, dsz = _silu_and_grad(z)
        dyv = dy_ref[...]
        dp_ref[:, 4096:] = (dyv * u * s * dsz).astype(dp_ref.dtype)
        dsg = dyv * sz
        dp_ref[:, :2048] = (dsg * s * gu).astype(dp_ref.dtype)
        ds = dsg * u
        rows = []
        dbs = [jnp.zeros((SGU_CHUNK, 1), F32) for _ in range(SGU_GROUPS)]
        for c in range(tl // SGU_CHUNK):
            rs = slice(SGU_CHUNK * c, SGU_CHUNK * (c + 1))
            cols = []
            for g in range(SGU_GROUPS):
                gs = slice(SGU_GDIM * g, SGU_GDIM * (g + 1))
                dsg_c = ds[rs, gs]
                dbs[g] = dbs[g] + jnp.sum(dsg_c, axis=1, keepdims=True)
                dwm_ref[g] += jnp.where(mask, _mm_nt(dsg_c, vg[rs, gs]), 0.0)
                cols.append(_mm_tn(wms[g], dsg_c))
            rows.append(jnp.concatenate(cols, axis=1))
        dbt_ref[...] += jnp.concatenate(dbs, axis=1)
        dvg = rows[0] if len(rows) == 1 else jnp.concatenate(rows, axis=0)
        dgain_ref[...] += jnp.sum(dvg * vn, axis=0, keepdims=True)
        dvn = dvg * gain_ref[...]
        dv = rstd * (dvn - jnp.mean(dvn, axis=-1, keepdims=True) - vn * jnp.mean(dvn * vn, axis=-1, keepdims=True))
        dp_ref[:, 2048:4096] = (dv * gv).astype(dp_ref.dtype)

    return _call(
        body, plan, name="sgu_backward", grid=(L // tl,),
        in_specs=[pl.BlockSpec((tl, ODD_IN), lambda i: (i, 0)), pl.BlockSpec((tl, 2048), lambda i: (i, 0)),
                  _full((1, 2048)), _full(wm.shape), _full(bt.shape)],
        out_specs=[pl.BlockSpec((tl, ODD_IN), lambda i: (i, 0)), _full((1, 2048)), _full(wm.shape), _full(bt.shape)],
        out_shape=[jax.ShapeDtypeStruct((L, ODD_IN), MXU_DTYPE), jax.ShapeDtypeStruct((1, 2048), F32),
                   jax.ShapeDtypeStruct(wm.shape, F32), jax.ShapeDtypeStruct(bt.shape, F32)],
        sem=("arbitrary",),
    )(p, dy, gain, wm, bt)


def cast_shards(mats):
    n = len(mats)
    steps = 8

    def body(*refs):
        for p in range(n):
            refs[n + p][...] = refs[p][...].astype(MXU_DTYPE)

    specs = [pl.BlockSpec((m.shape[0] // steps, m.shape[1]), lambda i: (i, 0)) for m in mats]
    return pl.pallas_call(
        body, name="cast_shards", grid=(steps,), in_specs=specs, out_specs=specs,
        out_shape=[jax.ShapeDtypeStruct(m.shape, MXU_DTYPE) for m in mats],
        compiler_params=_cparams(("arbitrary",)),
    )(*mats)


def local_grads(x, tgt, w):
    L = x.shape[0]
    ne, gf = w["norm_even"], w["final_norm"].reshape(1, D_MODEL)
    sh = dict(zip(MATRICES, cast_shards([w[n][0] for n in MATRICES])))
    lam_re, lam_im = w["s5_lam_re"][0], w["s5_lam_im"][0]
    log_dt = w["s5_log_dt"].reshape(S5_GROUPS, 1)
    bt_re = jnp.transpose(w["s5_b_re"][0], (2, 0, 1))
    bt_im = jnp.transpose(w["s5_b_im"][0], (2, 0, 1))
    c_re, c_im = w["s5_c_re"][0], w["s5_c_im"][0]
    wm = w["sgu_w_spatial"][0]
    bt = jnp.transpose(w["sgu_b_spatial"][0])

    tl5 = min(TL_S5, L)
    ab_re, ab_im, bb_re, bb_im, at_re, at_im = s5_params_fwd(lam_re, lam_im, log_dt, bt_re, bt_im, tl5 // 8)
    atab = jnp.stack([ab_re.reshape(S5_LANES), ab_im.reshape(S5_LANES),
                      at_re.reshape(S5_LANES), at_im.reshape(S5_LANES)])
    wbd = jnp.concatenate([_block_diag(jnp.transpose(bb_re, (1, 0, 2)), True),
                           _block_diag(jnp.transpose(bb_im, (1, 0, 2)), True)], axis=2).astype(MXU_DTYPE)
    cre = _block_diag(jnp.transpose(c_re, (0, 2, 1)), True).astype(MXU_DTYPE)
    cim = _block_diag(jnp.transpose(c_im, (0, 2, 1)), True).astype(MXU_DTYPE)
    cos, sin = _rope_tables(L)

    s5_cols = 2 * S5_WIDTH
    me = (2 * lax.axis_index("x") + lax.axis_index("y")).astype(jnp.int32)
    xs = stream_order(x, tl5)
    slab = lambda d: jnp.stack([me ^ d])
    zero = jnp.zeros((1,), jnp.int32)
    shards = [sh["w_in_even"][None]]
    (p1, h0s, h0), (got,) = even_in_slabs(x, xs, ne, shards[0], slab(0), zero, "even_in_0",
                                          plan=gather_plan([sh["w_in_even"]], only=1))
    for d in (1, 2, 3):
        shards.append(got)
        plan = gather_plan([sh["w_in_even"]], only=d + 1) if d < 3 else gather_plan([sh["s5_w_glu"]])
        (p1,), (got,) = even_in_slabs(x, xs, ne, shards[d], slab(d), zero, "even_in_%d" % d, p_in=p1, plan=plan)
    w_glu = got
    by_xor = jnp.concatenate(shards)
    w_in_e = [lax.dynamic_index_in_dim(by_xor, me ^ j, 0, keepdims=False) for j in range(N_CHIPS)]
    w_s5 = jnp.concatenate([w_in_e[0], w_in_e[1][:, :s5_cols - EVEN_IN // N_CHIPS]], axis=1)
    w_kvzq = jnp.concatenate([w_in_e[2], w_in_e[3], w_in_e[1][:, s5_cols - EVEN_IN // N_CHIPS:]], axis=1)
    w_glu = w_glu.reshape(S5_WIDTH, S5_WIDTH)
    (ya, st_re, st_im, sv_re, sv_im), (w_out_e, w_in_o, w_out_o, no, sg_gain) = s5_forward(
        p1, wbd, cre, cim, atab, w["s5_d"], w_glu, w["s5_b_glu"],
        gather_plan([sh["w_out_even"], sh["w_in_odd"], sh["w_out_odd"], w["norm_odd"], w["sgu_norm_gain"]]))
    w_out_e = w_out_e.reshape(2 * S5_WIDTH, D_MODEL)
    w_out_o = w_out_o.reshape(SGU_WIDTH, D_MODEL)
    no, sg_gain = no.reshape(1, D_MODEL), sg_gain.reshape(1, SGU_WIDTH)
    yb, prevs = retention_forward(p1, cos, sin, w["ret_gn_gain"])
    ya = token_order(ya, tl5)
    x1 = matmul_residual([ya, yb], w_out_e, x, "even_out")
    (p2, h1), _ = norm_matmul(x1, no, w_in_o, "odd_in")
    y2 = sgu_forward(p2, sg_gain, wm, bt)
    dx2, loss, dgf = out_proj_loss(y2, w_out_o, x1, gf, tgt, "odd_out_loss")

    g, landed = {}, {}
    shard_major = lambda a, n: a.reshape((N_CHIPS,) + w[n].shape[1:])
    dy2, g_w_out_o = out_proj_bwd(dx2, w_out_o, [y2], "odd_out_bwd")
    (dp2, g["sgu_norm_gain"], dwm, dbt), (landed["w_out_odd"],) = sgu_backward(
        p2, dy2, sg_gain, wm, bt, reduce_plan([shard_major(g_w_out_o, "w_out_odd")]))
    g_w_in_o, _ = in_proj_bwd_dw(h1, dp2, "odd_in_dw", ODD_IN // N_CHIPS)
    (dx1, g["norm_odd"]), _ = in_proj_bwd_dx(x1, no, [dp2], [w_in_o], dx2, "odd_in_dx")
    dya, dyb, g_w_out_e = out_proj_bwd(dx1, w_out_e, [ya, yb], "even_out_bwd")
    ((dpa, dwbd, dcre, dcim, dab_re, dab_im, g["s5_d"], g_w_glu, g["s5_b_glu"]),
     (landed["w_in_odd"], landed["w_out_even"])) = s5_backward(
        p1, stream_order(dya, tl5), st_re, st_im, sv_re, sv_im, wbd, cre, cim, atab, w["s5_d"], w_glu,
        w["s5_b_glu"], reduce_plan([g_w_in_o, shard_major(g_w_out_e, "w_out_even")]))

    dbb_re = jnp.transpose(_block_diag_extract(dwbd[:, :, :512], S5_GROUP, S5_STATE), (1, 0, 2))
    dbb_im = jnp.transpose(_block_diag_extract(dwbd[:, :, 512:], S5_GROUP, S5_STATE), (1, 0, 2))
    dlr, dli, ddt, dbt_re, dbt_im = s5_params_bwd(
        lam_re, lam_im, log_dt, bt_re, bt_im, dab_re.reshape(8, S5_GROUPS, S5_STATE),
        dab_im.reshape(8, S5_GROUPS, S5_STATE), dbb_re, dbb_im)
    g["s5_lam_re"], g["s5_lam_im"] = dlr[None], dli[None]
    g["s5_log_dt"] = ddt.reshape(1, S5_GROUPS)
    g["s5_b_re"], g["s5_b_im"] = dbt_re, dbt_im
    g["s5_c_re"] = _block_diag_extract(dcre, S5_GROUP, S5_STATE)[None]
    g["s5_c_im"] = _block_diag_extract(dcim, S5_GROUP, S5_STATE)[None]
    g["sgu_w_spatial"] = dwm[None]
    g["sgu_b_spatial"] = jnp.transpose(dbt)[None]
    g["final_norm"] = dgf.reshape(D_MODEL)
    g["loss"] = loss

    big_small = ("s5_b_re", "s5_b_im", "s5_c_re")
    (dpb, g["ret_gn_gain"]), recv = retention_backward(
        p1, dyb, prevs, cos, sin, w["ret_gn_gain"],
        reduce_plan([shard_major(g_w_glu, "s5_w_glu")], [g[n] for n in big_small]))
    landed.update(zip(("s5_w_glu",) + big_small, recv))
    done = tuple(n for n in MATRICES if n != "w_in_even")
    part = {n: sum_slabs(landed[n], "sum_" + n) for n in done}
    g_w_in_e, recv = in_proj_bwd_dw(h0s, dpa, "even_in_dw_s5", 512, dtype=MXU_DTYPE,
                                    plan=_SiblingPlan([part[n] for n in done]))
    other = dict(zip(done, recv))
    small = tuple(n for n in SMALL if n != "norm_even" and n not in big_small) + ("loss",)
    wb = EVEN_IN // N_CHIPS
    g_w_in_e, _ = in_proj_bwd_dw(h0, dpb, "even_in_dw_q", 512, first=s5_cols // 512, into=g_w_in_e,
                                 dtype=MXU_DTYPE, dp_first=2 * wb // 512, count=RET_HEADS * RET_DK // 512)
    g_w_in_e, recv = in_proj_bwd_dw(h0, dpb, "even_in_dw_kvz", wb, first=2, into=g_w_in_e, dtype=MXU_DTYPE,
                                    count=2, plan=reduce_plan([], [g[n] for n in small]))
    landed.update(zip(small, recv))
    (dx0, g["norm_even"]), (landed["w_in_even"],) = in_proj_bwd_dx(
        x, ne, [token_order(dpa, tl5), dpb], [w_s5, w_kvzq], dx1, "even_in_dx", reduce_plan([g_w_in_e]))
    (landed["norm_even"],) = run_plan(reduce_plan([], [g["norm_even"]]), "exchange_norm_even")
    return dx0, landed, part, other


def _row_block(rows):
    return 128 if rows % 128 == 0 else rows


def sum_slabs(r, name):
    _, R, C = r.shape
    tr = _row_block(R)

    def body(r_ref, o_ref):
        a, b, c, d = (r_ref[k].astype(F32) for k in range(N_CHIPS))
        o_ref[...] = (a + b) + (c + d)

    return pl.pallas_call(
        body, name=name, grid=(R // tr,),
        in_specs=[pl.BlockSpec((N_CHIPS, tr, C), lambda i: (0, i, 0))],
        out_specs=pl.BlockSpec((tr, C), lambda i: (i, 0)),
        out_shape=jax.ShapeDtypeStruct((R, C), F32),
        compiler_params=_cparams(("arbitrary",)),
    )(r)


def _adam(w, m, v, g):
    mn = ADAM_B1 * m + (1.0 - ADAM_B1) * g
    vn = ADAM_B2 * v + (1.0 - ADAM_B2) * (g * g)
    m_hat = mn / (1.0 - ADAM_B1 ** ADAM_STEP)
    v_hat = vn / (1.0 - ADAM_B2 ** ADAM_STEP)
    return -ADAM_LR * (m_hat / (jnp.sqrt(v_hat) + ADAM_EPS) + ADAM_WD * w), mn, vn


def adam_update(w, m, v, ga, gb, name):
    R, C = w.shape
    tr = _row_block(R)

    def body(w_ref, m_ref, v_ref, ga_ref, gb_ref, g_out, d_out, m_out, v_out):
        g = ga_ref[...] + gb_ref[...]
        g_out[...] = g
        d_out[...], m_out[...], v_out[...] = _adam(w_ref[...], m_ref[...], v_ref[...], g)

    blk = pl.BlockSpec((tr, C), lambda i: (i, 0))
    return pl.pallas_call(
        body, name=name, grid=(R // tr,),
        in_specs=[blk] * 5, out_specs=[blk] * 4,
        out_shape=[jax.ShapeDtypeStruct((R, C), F32)] * 4,
        compiler_params=_cparams(("arbitrary",)),
    )(w, m, v, ga, gb)


WIDE_ROWS = ("s5_b_re", "s5_b_im")


def sum_small(landed):
    def body(*refs):
        k = len(refs) // 2
        for i in range(k):
            r = refs[i]
            refs[k + i][...] = (r[0] + r[1]) + (r[2] + r[3])

    names = list(landed)
    res = pl.pallas_call(
        body, name="sum_small", out_shape=[jax.ShapeDtypeStruct(landed[n].shape[1:], F32) for n in names],
        compiler_params=pltpu.CompilerParams(vmem_limit_bytes=VMEM_LIMIT),
    )(*[landed[n] for n in names])
    return dict(zip(names, res))


def adam_small(names, w, m, v, ga, gb):
    def body(*refs):
        k = len(refs) // 9
        me = 2 * lax.axis_index("x") + lax.axis_index("y")
        for i in range(k):
            w_ref, m_ref, v_ref, ga_ref, gb_ref = refs[i], refs[k + i], refs[2 * k + i], refs[3 * k + i], refs[4 * k + i]
            size = w_ref.shape[-1]
            if ga_ref.shape != w_ref.shape:
                part = pl.ds(pl.multiple_of(me * size, LANES), size)
                g = ga_ref[:, part] + gb_ref[:, part]
            else:
                g = ga_ref[...] + gb_ref[...]
            refs[5 * k + i][...] = g
            refs[6 * k + i][...], refs[7 * k + i][...], refs[8 * k + i][...] = _adam(w_ref[...], m_ref[...], v_ref[...], g)

    ins = [d[n] for d in (w, m, v, ga, gb) for n in names]
    outs = [jax.ShapeDtypeStruct(w[n].shape, F32) for _ in range(4) for n in names]
    res = pl.pallas_call(body, name="adam_small", out_shape=outs,
                         compiler_params=pltpu.CompilerParams(vmem_limit_bytes=VMEM_LIMIT))(*ins)
    k = len(names)
    return [dict(zip(names, res[j * k:(j + 1) * k])) for j in range(4)]


WEIGHTS = ("norm_even", "w_in_even", "s5_lam_re", "s5_lam_im", "s5_log_dt", "s5_b_re", "s5_b_im", "s5_c_re",
           "s5_c_im", "s5_d", "s5_w_glu", "s5_b_glu", "ret_gn_gain", "w_out_even", "norm_odd", "w_in_odd",
           "sgu_norm_gain", "sgu_w_spatial", "sgu_b_spatial", "w_out_odd", "final_norm")
MATRICES = ("w_in_even", "s5_w_glu", "w_out_even", "w_in_odd", "w_out_odd")
SHARDED_VECS = ("norm_odd", "sgu_norm_gain")
REPLICATED = tuple(n for n in WEIGHTS if n not in MATRICES and n not in SHARDED_VECS)
SMALL = tuple(n for n in WEIGHTS if n not in MATRICES)
LANES = 128


def kernel(x, norm_even, w_in_even, s5_lam_re, s5_lam_im, s5_log_dt, s5_b_re, s5_b_im, s5_c_re, s5_c_im, s5_d, s5_w_glu, s5_b_glu, ret_gn_gain, w_out_even, norm_odd, w_in_odd, sgu_norm_gain, sgu_w_spatial, sgu_b_spatial, w_out_odd, final_norm, loss_target, m_norm_even, m_w_in_even, m_s5_lam_re, m_s5_lam_im, m_s5_log_dt, m_s5_b_re, m_s5_b_im, m_s5_c_re, m_s5_c_im, m_s5_d, m_s5_w_glu, m_s5_b_glu, m_ret_gn_gain, m_w_out_even, m_norm_odd, m_w_in_odd, m_sgu_norm_gain, m_sgu_w_spatial, m_sgu_b_spatial, m_w_out_odd, m_final_norm, v_norm_even, v_w_in_even, v_s5_lam_re, v_s5_lam_im, v_s5_log_dt, v_s5_b_re, v_s5_b_im, v_s5_c_re, v_s5_c_im, v_s5_d, v_s5_w_glu, v_s5_b_glu, v_ret_gn_gain, v_w_out_even, v_norm_odd, v_w_in_odd, v_sgu_norm_gain, v_sgu_w_spatial, v_sgu_b_spatial, v_w_out_odd, v_final_norm):
    w = dict(norm_even=norm_even, w_in_even=w_in_even, s5_lam_re=s5_lam_re, s5_lam_im=s5_lam_im, s5_log_dt=s5_log_dt, s5_b_re=s5_b_re, s5_b_im=s5_b_im, s5_c_re=s5_c_re, s5_c_im=s5_c_im, s5_d=s5_d, s5_w_glu=s5_w_glu, s5_b_glu=s5_b_glu, ret_gn_gain=ret_gn_gain, w_out_even=w_out_even, norm_odd=norm_odd, w_in_odd=w_in_odd, sgu_norm_gain=sgu_norm_gain, sgu_w_spatial=sgu_w_spatial, sgu_b_spatial=sgu_b_spatial, w_out_odd=w_out_odd, final_norm=final_norm)
    m = dict(norm_even=m_norm_even, w_in_even=m_w_in_even, s5_lam_re=m_s5_lam_re, s5_lam_im=m_s5_lam_im, s5_log_dt=m_s5_log_dt, s5_b_re=m_s5_b_re, s5_b_im=m_s5_b_im, s5_c_re=m_s5_c_re, s5_c_im=m_s5_c_im, s5_d=m_s5_d, s5_w_glu=m_s5_w_glu, s5_b_glu=m_s5_b_glu, ret_gn_gain=m_ret_gn_gain, w_out_even=m_w_out_even, norm_odd=m_norm_odd, w_in_odd=m_w_in_odd, sgu_norm_gain=m_sgu_norm_gain, sgu_w_spatial=m_sgu_w_spatial, sgu_b_spatial=m_sgu_b_spatial, w_out_odd=m_w_out_odd, final_norm=m_final_norm)
    v = dict(norm_even=v_norm_even, w_in_even=v_w_in_even, s5_lam_re=v_s5_lam_re, s5_lam_im=v_s5_lam_im, s5_log_dt=v_s5_log_dt, s5_b_re=v_s5_b_re, s5_b_im=v_s5_b_im, s5_c_re=v_s5_c_re, s5_c_im=v_s5_c_im, s5_d=v_s5_d, s5_w_glu=v_s5_w_glu, s5_b_glu=v_s5_b_glu, ret_gn_gain=v_ret_gn_gain, w_out_even=v_w_out_even, norm_odd=v_norm_odd, w_in_odd=v_w_in_odd, sgu_norm_gain=v_sgu_norm_gain, sgu_w_spatial=v_sgu_w_spatial, sgu_b_spatial=v_sgu_b_spatial, w_out_odd=v_w_out_odd, final_norm=v_final_norm)

    grad_x, landed, part, other = local_grads(x[0], loss_target[0], w)

    small = SMALL + ("loss",)
    part["w_in_even"] = sum_slabs(landed["w_in_even"], "sum_w_in_even")
    part.update(sum_small({n: landed[n] for n in small}))
    names = ("w_in_even",) + small
    other.update(zip(names, run_plan(_SiblingPlan([part[n] for n in names]), "sibling_exchange")))

    wt, mt, vt = dict(w), dict(m), dict(v)
    for n in WIDE_ROWS:
        wt[n], mt[n], vt[n] = (jnp.transpose(a[n][0], (2, 0, 1)) for a in (w, m, v))
    out_g, out_d, out_m, out_v = adam_small(SMALL, wt, mt, vt, part, other)
    for n in WIDE_ROWS:
        for out in (out_g, out_d, out_m, out_v):
            out[n] = jnp.transpose(out[n], (1, 2, 0))[None]
    for n in MATRICES:
        res = adam_update(w[n][0], m[n][0], v[n][0], part[n], other[n], "adam_" + n)
        out_g[n], out_d[n], out_m[n], out_v[n] = (r[None] for r in res)
    total_loss = (part["loss"] + other["loss"])[0, 0]

    return (total_loss, grad_x[None], *[out_g[n] for n in WEIGHTS], *[out_d[n] for n in WEIGHTS],
            *[out_m[n] for n in WEIGHTS], *[out_v[n] for n in WEIGHTS])
```

```python
import functools
import math

import numpy as np
import jax
import jax.numpy as jnp
from jax import lax
from jax.experimental import pallas as pl
from jax.experimental.pallas import tpu as pltpu

F32 = jnp.float32
MXU_DTYPE = jnp.bfloat16
NORM_EPS = 1e-6
D_MODEL = 1024
S5_WIDTH = 1024
S5_GROUP = 16
S5_GROUPS = 64
S5_STATE = 64
S5_LANES = S5_GROUPS * S5_STATE
S5_KBLK = 8
RET_HEADS = 4
RET_DK = 256
RET_CHUNK = 128
ROPE_BASE = 10000.0
SGU_WIDTH = 2048
SGU_GROUPS = 4
SGU_GDIM = 512
SGU_CHUNK = 128
EVEN_IN = 6144
ODD_IN = 6144
ADAM_LR = 0.001
ADAM_B1 = 0.9
ADAM_B2 = 0.999
ADAM_EPS = 1e-08
ADAM_WD = 0.01
ADAM_STEP = 10
N_CHIPS = 4
VMEM_LIMIT = 56 * 1024 * 1024

TL_PROJ = 512
TL_DW = 1024
TL_S5 = 256
TL_SGU = 256


def _cparams(sem, **kw):
    return pltpu.CompilerParams(dimension_semantics=sem, vmem_limit_bytes=VMEM_LIMIT, **kw)


def _mm(a, b):
    return jnp.dot(a.astype(MXU_DTYPE), b.astype(MXU_DTYPE), preferred_element_type=F32)


def _mm_nt(a, b):
    return lax.dot_general(a.astype(MXU_DTYPE), b.astype(MXU_DTYPE),
                           (((1,), (1,)), ((), ())), preferred_element_type=F32)


def _mm_tn(a, b):
    return lax.dot_general(a.astype(MXU_DTYPE), b.astype(MXU_DTYPE),
                           (((0,), (0,)), ((), ())), preferred_element_type=F32)


_GELU_C = math.sqrt(2.0 / math.pi)


def _gelu_parts(x):
    x2 = x * x
    th = jnp.tanh(x * (_GELU_C + (_GELU_C * 0.044715) * x2))
    hx = 0.5 * x
    return hx + hx * th, th, x2, hx


def _gelu(x):
    return _gelu_parts(x)[0]


def _gelu_and_grad(x):
    g, th, x2, hx = _gelu_parts(x)
    return g, (0.5 + 0.5 * th) + hx * (1.0 - th * th) * (_GELU_C + (3.0 * _GELU_C * 0.044715) * x2)


def _gelu_grad(x):
    return _gelu_and_grad(x)[1]


def _sigmoid(x):
    return 1.0 / (1.0 + jnp.exp(-x))


def _silu_and_grad(x):
    s = _sigmoid(x)
    return x * s, s * (1.0 + x * (1.0 - s))


def _rms(x):
    return lax.rsqrt(jnp.mean(x * x, axis=-1, keepdims=True) + NORM_EPS)


def _full(shape):
    nd = len(shape)
    return pl.BlockSpec(shape, lambda *_: (0,) * nd)


MESH = pl.DeviceIdType.MESH
ANY = pl.BlockSpec(memory_space=pl.ANY)


def _place():
    return lax.axis_index("x"), lax.axis_index("y"), lax.axis_index("c")


def _chip_peer(x, y, c, d):
    return (1 - x if d >= 2 else x, 1 - y if d % 2 else y, c)


class _Plan:
    def __init__(self, inputs, out_shape, build):
        self.inputs, self.out_shape, self._build = list(inputs), list(out_shape), build
        n = len(self.inputs)
        self.sems = [pltpu.SemaphoreType.DMA((n, 3)), pltpu.SemaphoreType.DMA((n, 3)), pltpu.SemaphoreType.DMA((n,))]

    def start(self, in_refs, out_refs, sems):
        send, recv, local = self._build(in_refs, out_refs, sems)
        for p in range(len(self.inputs)):
            local[p].start()
            for cp in send[p]:
                cp.start()

    def wait(self, in_refs, out_refs, sems):
        send, recv, local = self._build(in_refs, out_refs, sems)
        for p in range(len(self.inputs)):
            for cp in recv[p]:
                cp.wait_recv()
        for p in range(len(self.inputs)):
            for cp in send[p]:
                cp.wait_send()
            local[p].wait()


class _GatherPlan:
    def __init__(self, shards, only=None):
        n = len(shards)
        self.n, self.only = n, only
        self.peers = (1, 2, 3) if only is None else (only,)
        self.inputs = list(shards)
        slabs = N_CHIPS if only is None else 1
        self.out_shape = [jax.ShapeDtypeStruct((slabs,) + s.shape, s.dtype) for s in shards]
        self.halved = [s.shape[0] % 32 == 0 for s in shards]
        self.sems = [pltpu.SemaphoreType.DMA((n, 3)) for _ in range(4)] + [pltpu.SemaphoreType.DMA((n,))]

    def _copies(self, in_refs, out_refs, sems):
        ici_s, ici_r, d2d_s, d2d_r, loc = sems
        x, y, c = _place()
        me = 2 * x + y

        def rows(p, core):
            if not self.halved[p]:
                return slice(None)
            half = self.inputs[p].shape[0] // 2
            return pl.ds(pl.multiple_of(core * half, 16), half)

        def slab(chip):
            return chip if self.only is None else 0

        def ici(p, d, chip, core):
            return pltpu.make_async_remote_copy(
                src_ref=in_refs[p].at[rows(p, core)], dst_ref=out_refs[p].at[slab(chip), rows(p, core)],
                send_sem=ici_s.at[p, d - 1], recv_sem=ici_r.at[p, d - 1],
                device_id=_chip_peer(x, y, c, d), device_id_type=MESH)

        def d2d(p, d, core):
            part = out_refs[p].at[slab(me ^ d), rows(p, core)]
            return pltpu.make_async_remote_copy(
                src_ref=part, dst_ref=part, send_sem=d2d_s.at[p, d - 1], recv_sem=d2d_r.at[p, d - 1],
                device_id=(x, y, 1 - c), device_id_type=MESH)

        local = [pltpu.make_async_copy(in_refs[p], out_refs[p].at[slab(me)], loc.at[p]) for p in range(self.n)]
        return me, c, ici, d2d, local

    def start(self, in_refs, out_refs, sems):
        me, c, ici, d2d, local = self._copies(in_refs, out_refs, sems)
        for p in range(self.n):
            if self.only is None:
                local[p].start()
            for d in self.peers:
                ici(p, d, me, c).start()

    def wait(self, in_refs, out_refs, sems):
        me, c, ici, d2d, local = self._copies(in_refs, out_refs, sems)
        for p in range(self.n):
            for d in self.peers:
                ici(p, d, me ^ d, c).wait_recv()
                if self.halved[p]:
                    d2d(p, d, c).start()
        for p in range(self.n):
            for d in self.peers:
                if self.halved[p]:
                    d2d(p, d, 1 - c).wait_recv()
                    d2d(p, d, c).wait_send()
                ici(p, d, me, c).wait_send()
            if self.only is None:
                local[p].wait()


def gather_plan(shards, only=None):
    return _GatherPlan(shards, only)


def reduce_plan(shards, whole=()):
    n_s = len(shards)

    def build(in_refs, out_refs, sems):
        send_sems, recv_sems, loc_sems = sems
        x, y, c = _place()
        me = 2 * x + y

        def src(p, slab):
            return in_refs[p].at[slab] if p < n_s else in_refs[p]

        def remote(p, d):
            return pltpu.make_async_remote_copy(
                src_ref=src(p, me ^ d), dst_ref=out_refs[p].at[d], send_sem=send_sems.at[p, d - 1],
                recv_sem=recv_sems.at[p, d - 1], device_id=_chip_peer(x, y, c, d), device_id_type=MESH)

        n = len(in_refs)
        send = [[remote(p, d) for d in (1, 2, 3)] for p in range(n)]
        local = [pltpu.make_async_copy(src(p, me), out_refs[p].at[0], loc_sems.at[p]) for p in range(n)]
        return send, send, local

    outs = [jax.ShapeDtypeStruct(s.shape, s.dtype) for s in shards]
    outs += [jax.ShapeDtypeStruct((N_CHIPS,) + a.shape, a.dtype) for a in whole]
    return _Plan(list(shards) + list(whole), outs, build)


class _SiblingPlan:
    def __init__(self, arrs):
        self.inputs = list(arrs)
        self.out_shape = [jax.ShapeDtypeStruct(a.shape, a.dtype) for a in arrs]
        n = len(arrs)
        self.sems = [pltpu.SemaphoreType.DMA((n,)), pltpu.SemaphoreType.DMA((n,))]

    def _copies(self, in_refs, out_refs, sems):
        x, y, c = _place()
        return [pltpu.make_async_remote_copy(
            src_ref=in_refs[p], dst_ref=out_refs[p], send_sem=sems[0].at[p], recv_sem=sems[1].at[p],
            device_id=(x, y, 1 - c), device_id_type=MESH) for p in range(len(self.inputs))]

    def start(self, in_refs, out_refs, sems):
        for cp in self._copies(in_refs, out_refs, sems):
            cp.start()

    def wait(self, in_refs, out_refs, sems):
        copies = self._copies(in_refs, out_refs, sems)
        for cp in copies:
            cp.wait_recv()
        for cp in copies:
            cp.wait_send()


def run_plan(plan, name):
    n = len(plan.inputs)

    def body(*refs):
        plan.start(refs[:n], refs[n:2 * n], refs[2 * n:])
        plan.wait(refs[:n], refs[n:2 * n], refs[2 * n:])

    return pl.pallas_call(body, name=name, in_specs=[ANY] * n, out_specs=[ANY] * n, out_shape=plan.out_shape,
                          scratch_shapes=plan.sems)(*plan.inputs)


def _call(body, plan, *, name, grid, in_specs, out_specs, out_shape, sem, scratch_shapes=(), aliases=None,
          n_prefetch=0):
    aliases = {} if aliases is None else aliases
    single = not isinstance(out_shape, (list, tuple))
    out_specs = [out_specs] if single else list(out_specs)
    out_shape = [out_shape] if single else list(out_shape)
    n_in, n_out, n_scr = len(in_specs), len(out_specs), len(scratch_shapes)
    ci = 0 if plan is None else len(plan.inputs)
    co = 0 if plan is None else len(plan.out_shape)

    def hosted(*refs):
        pre, refs = refs[:n_prefetch], refs[n_prefetch:]
        ins, cins = refs[:n_in], refs[n_in:n_in + ci]
        k = n_in + ci
        outs, couts = refs[k:k + n_out], refs[k + n_out:k + n_out + co]
        k += n_out + co
        scr, sems = refs[k:k + n_scr], refs[k + n_scr:]
        ids = [pl.program_id(a) for a in range(len(grid))]
        first = functools.reduce(jnp.logical_and, [i == 0 for i in ids])
        last = functools.reduce(jnp.logical_and, [i == g - 1 for i, g in zip(ids, grid)])

        @pl.when(first)
        def _():
            plan.start(cins, couts, sems)

        body(*pre, *ins, *outs, *scr)

        @pl.when(last)
        def _():
            plan.wait(cins, couts, sems)

    def run(*args):
        hosting = plan is not None
        spec = pltpu.PrefetchScalarGridSpec(
            num_scalar_prefetch=n_prefetch, grid=grid,
            in_specs=list(in_specs) + ([ANY] * ci if hosting else []),
            out_specs=out_specs + ([ANY] * co if hosting else []),
            scratch_shapes=list(scratch_shapes) + (plan.sems if hosting else []))
        res = pl.pallas_call(hosted if hosting else body, name=name, grid_spec=spec,
                             out_shape=out_shape + (plan.out_shape if hosting else []),
                             input_output_aliases=aliases, compiler_params=_cparams(sem),
                             )(*args, *(plan.inputs if hosting else []))
        return (res[0] if single else res[:n_out]), list(res[n_out:])

    return run


def norm_matmul(x, g, w, name, plan=None, tn=None):
    L, D = x.shape
    tl = min(TL_DW, L)
    if w.ndim == 3:
        nt, _, tn = w.shape
        w_spec = pl.BlockSpec((1, D, tn), lambda i, n: (n, 0, 0))
    else:
        nt = w.shape[1] // tn
        w_spec = pl.BlockSpec((D, tn), lambda i, n: (0, n))

    def body(x_ref, g_ref, w_ref, o_ref, h_ref):
        xv = x_ref[...]
        h = (xv * _rms(xv) * g_ref[...]).astype(h_ref.dtype)
        h_ref[...] = h
        o_ref[...] = _mm(h, w_ref[0] if w.ndim == 3 else w_ref[...])

    return _call(
        body, plan, name=name, grid=(L // tl, nt),
        in_specs=[pl.BlockSpec((tl, D), lambda i, n: (i, 0)), _full((1, D)), w_spec],
        out_specs=[pl.BlockSpec((tl, tn), lambda i, n: (i, n)), pl.BlockSpec((tl, D), lambda i, n: (i, 0))],
        out_shape=[jax.ShapeDtypeStruct((L, nt * tn), F32), jax.ShapeDtypeStruct((L, D), MXU_DTYPE)],
        sem=("arbitrary", "arbitrary"),
    )(x, g, w)


def even_in_slabs(x, xs, g, w, slabs, wsel, name, p_in=None, plan=None):
    L, D = x.shape
    tl = min(TL_PROJ, L)
    wb = EVEN_IN // N_CHIPS
    n = slabs.shape[0]
    s5_cols = 2 * S5_WIDTH - wb
    first = p_in is None

    def body(slabs_ref, wsel_ref, xs_ref, x_ref, g_ref, w_ref, *rest):
        o_ref = rest[-3] if first else rest[-1]
        j = slabs_ref[pl.program_id(0)]
        hs = (xs_ref[...] * _rms(xs_ref[...]) * g_ref[...]).astype(MXU_DTYPE)
        h = (x_ref[...] * _rms(x_ref[...]) * g_ref[...]).astype(MXU_DTYPE)
        if first:
            rest[-2][...] = hs
            rest[-1][...] = h
        o_ref[:, :s5_cols] = _mm(jnp.where(j <= 1, hs, h), w_ref[0, :, :s5_cols])
        o_ref[:, s5_cols:] = _mm(jnp.where(j == 0, hs, h), w_ref[0, :, s5_cols:])

    row = pl.BlockSpec((tl, D), lambda s, i, slabs_ref, wsel_ref: (i, 0))
    in_specs = [row if first else
                pl.BlockSpec((tl, D), lambda s, i, slabs_ref, wsel_ref: (jnp.where(slabs_ref[s] <= 1, i, 0), 0)),
                row if first else
                pl.BlockSpec((tl, D), lambda s, i, slabs_ref, wsel_ref: (jnp.where(slabs_ref[s] >= 1, i, 0), 0)),
                pl.BlockSpec((1, D), lambda s, i, slabs_ref, wsel_ref: (0, 0)),
                pl.BlockSpec((1, D, wb), lambda s, i, slabs_ref, wsel_ref: (wsel_ref[s], 0, 0))]
    out_specs = [pl.BlockSpec((tl, wb), lambda s, i, slabs_ref, wsel_ref: (i, slabs_ref[s]))]
    out_shape = [jax.ShapeDtypeStruct((L, EVEN_IN), F32)]
    args = [slabs, wsel, xs, x, g, w]
    if first:
        out_specs += [row, row]
        out_shape += [jax.ShapeDtypeStruct((L, D), MXU_DTYPE)] * 2
    else:
        in_specs.append(ANY)
        args.append(p_in)
    return _call(body, plan, name=name, grid=(n, L // tl), in_specs=in_specs, out_specs=out_specs,
                 out_shape=out_shape, sem=("arbitrary", "arbitrary"), n_prefetch=2,
                 aliases={} if first else {6: 0})(*args)


def matmul_residual(ys, w, x, name):
    L, D = x.shape
    tl = min(TL_PROJ, L)
    n = len(ys)
    offs = np.cumsum([0] + [y.shape[1] for y in ys])

    def body(*refs):
        y_refs, w_ref, x_ref, o_ref = refs[:n], refs[n], refs[n + 1], refs[n + 2]
        acc = x_ref[...]
        for k in range(n):
            acc = acc + _mm(y_refs[k][...], w_ref[offs[k]:offs[k + 1], :])
        o_ref[...] = acc

    return pl.pallas_call(
        body, name=name, grid=(L // tl,),
        in_specs=[pl.BlockSpec((tl, y.shape[1]), lambda i: (i, 0)) for y in ys]
        + [_full(w.shape), pl.BlockSpec((tl, D), lambda i: (i, 0))],
        out_specs=pl.BlockSpec((tl, D), lambda i: (i, 0)),
        out_shape=jax.ShapeDtypeStruct((L, D), F32),
        compiler_params=_cparams(("arbitrary",)),
    )(*ys, w, x)


def out_proj_loss(y, w, x, gf, tgt, name):
    L, K = y.shape
    D = w.shape[1]
    tl = min(TL_PROJ, L)

    def body(y_ref, w_ref, x_ref, gf_ref, t_ref, dx_ref, loss_ref, dg_ref):
        @pl.when(pl.program_id(0) == 0)
        def _():
            loss_ref[...] = jnp.zeros_like(loss_ref)
            dg_ref[...] = jnp.zeros_like(dg_ref)

        x2 = x_ref[...] + _mm(y_ref[...], w_ref[...])
        r = _rms(x2)
        xn = x2 * r
        e = xn * gf_ref[...] - t_ref[...]
        loss_ref[...] += (0.5 / D) * jnp.sum(e * e)
        dout = e * (1.0 / D)
        dg_ref[...] += jnp.sum(dout * xn, axis=0, keepdims=True)
        dxn = dout * gf_ref[...]
        dx_ref[...] = r * (dxn - xn * jnp.mean(dxn * xn, axis=-1, keepdims=True))

    return pl.pallas_call(
        body, name=name, grid=(L // tl,),
        in_specs=[pl.BlockSpec((tl, K), lambda i: (i, 0)), _full((K, D)),
                  pl.BlockSpec((tl, D), lambda i: (i, 0)), _full((1, D)),
                  pl.BlockSpec((tl, D), lambda i: (i, 0))],
        out_specs=[pl.BlockSpec((tl, D), lambda i: (i, 0)), _full((8, 128)), _full((1, D))],
        out_shape=[jax.ShapeDtypeStruct((L, D), F32), jax.ShapeDtypeStruct((8, 128), F32),
                   jax.ShapeDtypeStruct((1, D), F32)],
        compiler_params=_cparams(("arbitrary",)),
    )(y, w, x, gf, tgt)


def out_proj_bwd(dx, w, ys, name):
    L, D = dx.shape
    K = w.shape[0]
    tl = min(TL_PROJ, L)
    n = len(ys)
    offs = np.cumsum([0] + [y.shape[1] for y in ys])

    def body(*refs):
        dx_ref, w_ref, y_refs = refs[0], refs[1], refs[2:2 + n]
        dy_refs, dw_ref = refs[2 + n:2 + 2 * n], refs[2 + 2 * n]

        @pl.when(pl.program_id(0) == 0)
        def _():
            dw_ref[...] = jnp.zeros_like(dw_ref)

        dxv = dx_ref[...]
        for k in range(n):
            dy_refs[k][...] = _mm_nt(dxv, w_ref[offs[k]:offs[k + 1], :])
            dw_ref[offs[k]:offs[k + 1], :] += _mm_tn(y_refs[k][...], dxv)

    y_specs = [pl.BlockSpec((tl, y.shape[1]), lambda i: (i, 0)) for y in ys]
    return pl.pallas_call(
        body, name=name, grid=(L // tl,),
        in_specs=[pl.BlockSpec((tl, D), lambda i: (i, 0)), _full((K, D))] + y_specs,
        out_specs=y_specs + [_full((K, D))],
        out_shape=[jax.ShapeDtypeStruct(y.shape, F32) for y in ys] + [jax.ShapeDtypeStruct((K, D), F32)],
        compiler_params=_cparams(("arbitrary",)),
    )(dx, w, *ys)


def in_proj_bwd_dx(x, g, dps, ws, dres, name, plan=None):
    L, D = x.shape
    tl = min(TL_PROJ, L)
    n = len(dps)

    def body(*refs):
        x_ref, g_ref, dres_ref = refs[:3]
        dp_refs, w_refs = refs[3:3 + n], refs[3 + n:3 + 2 * n]
        dx_ref, dg_ref = refs[3 + 2 * n:]

        @pl.when(pl.program_id(0) == 0)
        def _():
            dg_ref[...] = jnp.zeros_like(dg_ref)

        dh = None
        for dp_ref, w_ref, w in zip(dp_refs, w_refs, ws):
            if w.ndim == 3:
                tn = w.shape[2]
                parts = [_mm_nt(dp_ref[:, tn * k:tn * (k + 1)], w_ref[k]) for k in range(w.shape[0])]
            else:
                parts = [_mm_nt(dp_ref[...], w_ref[...])]
            for part in parts:
                dh = part if dh is None else dh + part
        xv = x_ref[...]
        r = _rms(xv)
        xn = xv * r
        dg_ref[...] += jnp.sum(dh * xn, axis=0, keepdims=True)
        dxn = dh * g_ref[...]
        dx_ref[...] = dres_ref[...] + r * (dxn - xn * jnp.mean(dxn * xn, axis=-1, keepdims=True))

    return _call(
        body, plan, name=name, grid=(L // tl,),
        in_specs=[pl.BlockSpec((tl, D), lambda i: (i, 0)), _full((1, D)), pl.BlockSpec((tl, D), lambda i: (i, 0))]
        + [pl.BlockSpec((tl, dp.shape[1]), lambda i: (i, 0)) for dp in dps] + [_full(w.shape) for w in ws],
        out_specs=[pl.BlockSpec((tl, D), lambda i: (i, 0)), _full((1, D))],
        out_shape=[jax.ShapeDtypeStruct((L, D), F32), jax.ShapeDtypeStruct((1, D), F32)],
        sem=("arbitrary",),
    )(x, g, dres, *dps, *ws)


def in_proj_bwd_dw(h, dp, name, tn, first=0, into=None, dtype=F32, plan=None, dp_first=0, count=None):
    L, D = h.shape
    tl = min(TL_DW, L)
    wb = EVEN_IN // N_CHIPS
    per = wb // tn
    count = dp.shape[1] // tn if count is None else count
    last = L // tl - 1

    def body(*refs):
        h_ref, dp_ref, dw_ref, acc = refs[0], refs[1], refs[-2], refs[-1]

        @pl.when(pl.program_id(1) == 0)
        def _():
            acc[...] = jnp.zeros_like(acc)

        acc[...] += _mm_tn(h_ref[...], dp_ref[...])

        @pl.when(pl.program_id(1) == last)
        def _():
            dw_ref[0] = acc[...].astype(dw_ref.dtype)

    ins = [h, dp] + ([] if into is None else [into])
    return _call(
        body, plan, name=name, grid=(count, L // tl),
        in_specs=[pl.BlockSpec((tl, D), lambda n, i: (i, 0)), pl.BlockSpec((tl, tn), lambda n, i: (i, n + dp_first))]
        + ([] if into is None else [ANY]),
        out_specs=pl.BlockSpec((1, D, tn), lambda n, i: ((n + first) // per, 0, (n + first) % per)),
        out_shape=jax.ShapeDtypeStruct((N_CHIPS, D, wb), dtype),
        scratch_shapes=[pltpu.VMEM((D, tn), F32)],
        aliases={} if into is None else {2: 0},
        sem=("arbitrary", "arbitrary"),
    )(*ins)


def _s5_param_fn(lam_re, lam_im, log_dt, b_re, b_im):
    lr = jnp.minimum(lam_re, -1e-4)
    li = lam_im
    dt = jnp.exp(log_dt)
    mag = jnp.exp(lr * dt)
    ab_re = mag * jnp.cos(li * dt)
    ab_im = mag * jnp.sin(li * dt)
    den = lr * lr + li * li
    n_re = ab_re - 1.0
    n_im = ab_im
    z_re = (n_re * lr + n_im * li) / den
    z_im = (n_im * lr - n_re * li) / den
    bb_re = z_re[None] * b_re - z_im[None] * b_im
    bb_im = z_re[None] * b_im + z_im[None] * b_re
    return ab_re, ab_im, bb_re, bb_im


def s5_params_fwd(lam_re, lam_im, log_dt, b_re, b_im, span):
    G, P = lam_re.shape
    H = b_re.shape[0]
    assert span & (span - 1) == 0

    def body(lr_ref, li_ref, dt_ref, br_ref, bi_ref, abr_ref, abi_ref, bbr_ref, bbi_ref, pr_ref, pi_ref):
        ab_re, ab_im, bb_re, bb_im = _s5_param_fn(lr_ref[...], li_ref[...], dt_ref[...], br_ref[...], bi_ref[...])
        abr_ref[...] = ab_re
        abi_ref[...] = ab_im
        bbr_ref[...] = bb_re
        bbi_ref[...] = bb_im
        cr, ci = ab_re, ab_im
        for _ in range(span.bit_length() - 1):
            cr, ci = cr * cr - ci * ci, 2.0 * cr * ci
        pr_ref[...] = cr
        pi_ref[...] = ci

    shp = lambda *s: jax.ShapeDtypeStruct(s, F32)
    return pl.pallas_call(
        body, name="s5_params_fwd",
        out_shape=[shp(G, P), shp(G, P), shp(H, G, P), shp(H, G, P), shp(G, P), shp(G, P)],
    )(lam_re, lam_im, log_dt, b_re, b_im)


def s5_params_bwd(lam_re, lam_im, log_dt, b_re, b_im, d_ab_re, d_ab_im, d_bb_re, d_bb_im):
    G, P = lam_re.shape
    H = b_re.shape[0]

    def body(lr_ref, li_ref, dt_ref, br_ref, bi_ref, g0, g1, g2, g3, o0, o1, o2, o3, o4):
        prim = (lr_ref[...], li_ref[...], dt_ref[...], br_ref[...], bi_ref[...])
        _, vjp = jax.vjp(_s5_param_fn, *prim)
        d = vjp((jnp.sum(g0[...], axis=0), jnp.sum(g1[...], axis=0), g2[...], g3[...]))
        o0[...], o1[...], o2[...], o3[...], o4[...] = d

    shp = lambda *s: jax.ShapeDtypeStruct(s, F32)
    return pl.pallas_call(
        body, name="s5_params_bwd",
        out_shape=[shp(G, P), shp(G, P), shp(G, 1), shp(H, G, P), shp(H, G, P)],
    )(lam_re, lam_im, log_dt, b_re, b_im, d_ab_re, d_ab_im, d_bb_re, d_bb_im)


def stream_order(a, tl):
    L, C = a.shape
    return a.reshape(L // tl, 8, tl // 8, C).transpose(0, 2, 1, 3).reshape(L, C)


def token_order(a, tl):
    L, C = a.shape
    return a.reshape(L // tl, tl // 8, 8, C).transpose(0, 2, 1, 3).reshape(L, C)


_LANE_BLK = 1024
_LANE_BLK_BWD = 1024


def _cmul_add(ar, ai, xr, xi, br, bi):
    return br + (ar * xr - ai * xi), bi + (ar * xi + ai * xr)


def _cmulc_add(ar, ai, xr, xi, br, bi):
    return br + (ar * xr + ai * xi), bi + (ar * xi - ai * xr)


def _s5_states(u, wbd_ref, a_re, a_im, at_re, at_im, s_re, s_im, e_re, e_im, c0_re, c0_im, tl):
    t8 = tl // 8
    for k in range(S5_KBLK):
        bu = _mm(u[:, 128 * k:128 * (k + 1)], wbd_ref[k])
        s_re[:, 512 * k:512 * (k + 1)] = bu[:, :512]
        s_im[:, 512 * k:512 * (k + 1)] = bu[:, 512:]
    outs_re, outs_im = [], []
    for b in range(S5_LANES // _LANE_BLK):
        lanes = slice(_LANE_BLK * b, _LANE_BLK * (b + 1))
        ar = jnp.broadcast_to(a_re[:, lanes], (8, _LANE_BLK))
        ai = jnp.broadcast_to(a_im[:, lanes], (8, _LANE_BLK))

        def local(i, carry, lanes=lanes, ar=ar, ai=ai):
            r = pl.multiple_of(i * 8, 8)
            sr, si = _cmul_add(ar, ai, carry[0], carry[1], s_re[pl.ds(r, 8), lanes], s_im[pl.ds(r, 8), lanes])
            s_re[pl.ds(r, 8), lanes] = sr
            s_im[pl.ds(r, 8), lanes] = si
            return sr, si

        zero = jnp.zeros((8, _LANE_BLK), F32)
        fr, fi = lax.fori_loop(0, t8, local, (zero, zero), unroll=True)
        tr, ti = at_re[:, lanes], at_im[:, lanes]
        er, ei = c0_re[:, lanes], c0_im[:, lanes]
        ers, eis = [er], [ei]
        for j in range(8):
            er, ei = _cmul_add(tr, ti, er, ei, fr[j:j + 1], fi[j:j + 1])
            ers.append(er)
            eis.append(ei)
        outs_re.append(ers[8])
        outs_im.append(eis[8])
        ent_r, ent_i = jnp.concatenate(ers[:8], axis=0), jnp.concatenate(eis[:8], axis=0)
        e_re[:, lanes] = ent_r
        e_im[:, lanes] = ent_i

        def fix(i, carry, lanes=lanes, ar=ar, ai=ai):
            r = pl.multiple_of(i * 8, 8)
            zr, zi = ar * carry[0] - ai * carry[1], ar * carry[1] + ai * carry[0]
            s_re[pl.ds(r, 8), lanes] = s_re[pl.ds(r, 8), lanes] + zr
            s_im[pl.ds(r, 8), lanes] = s_im[pl.ds(r, 8), lanes] + zi
            return zr, zi

        lax.fori_loop(0, t8, fix, (ent_r, ent_i), unroll=True)
    return jnp.concatenate(outs_re, axis=1), jnp.concatenate(outs_im, axis=1)


def _s5_readout(s_re, s_im, cre_ref, cim_ref):
    ys = []
    for k in range(S5_KBLK):
        lanes = slice(512 * k, 512 * (k + 1))
        ys.append(_mm(s_re[:, lanes], cre_ref[k]) - _mm(s_im[:, lanes], cim_ref[k]))
    return jnp.concatenate(ys, axis=1)


def s5_forward(p, wbd, cre, cim, atab, d_skip, w_glu, b_glu, plan=None):
    L = p.shape[0]
    tl = min(TL_S5, L)
    nch = L // tl

    def body(u_ref, z_ref, wbd_ref, cre_ref, cim_ref, at_ref, d_ref, wg_ref, bg_ref,
             ya_ref, st_re_ref, st_im_ref, sv_re_ref, sv_im_ref, s_re, s_im, e_re, e_im, car_re, car_im):
        @pl.when(pl.program_id(0) == 0)
        def _():
            car_re[...] = jnp.zeros_like(car_re)
            car_im[...] = jnp.zeros_like(car_im)

        c0_re, c0_im = car_re[...], car_im[...]
        st_re_ref[0] = c0_re
        st_im_ref[0] = c0_im
        u = u_ref[...]
        x_re, x_im = _s5_states(u, wbd_ref, at_ref[0:1], at_ref[1:2], at_ref[2:3], at_ref[3:4],
                                s_re, s_im, e_re, e_im, c0_re, c0_im, tl)
        car_re[...] = x_re
        car_im[...] = x_im
        sv_re_ref[...] = s_re[...].astype(sv_re_ref.dtype)
        sv_im_ref[...] = s_im[...].astype(sv_im_ref.dtype)
        y = _s5_readout(sv_re_ref, sv_im_ref, cre_ref, cim_ref) + d_ref[...] * u
        yg = _gelu(y)
        gate = _sigmoid(_mm(yg, wg_ref[...]) + bg_ref[...])
        sz, _ = _silu_and_grad(z_ref[...])
        ya_ref[...] = (yg * gate * sz).astype(ya_ref.dtype)

    return _call(
        body, plan, name="s5_forward", grid=(nch,),
        in_specs=[pl.BlockSpec((tl, 1024), lambda i: (i, 0)), pl.BlockSpec((tl, 1024), lambda i: (i, 1)),
                  _full(wbd.shape), _full(cre.shape), _full(cim.shape), _full(atab.shape),
                  _full((1, 1024)), _full((1024, 1024)), _full((1, 1024))],
        out_specs=[pl.BlockSpec((tl, 1024), lambda i: (i, 0)),
                   pl.BlockSpec((1, 1, S5_LANES), lambda i: (i, 0, 0)),
                   pl.BlockSpec((1, 1, S5_LANES), lambda i: (i, 0, 0)),
                   pl.BlockSpec((tl, S5_LANES), lambda i: (i, 0)), pl.BlockSpec((tl, S5_LANES), lambda i: (i, 0))],
        out_shape=[jax.ShapeDtypeStruct((L, 1024), MXU_DTYPE),
                   jax.ShapeDtypeStruct((nch, 1, S5_LANES), F32), jax.ShapeDtypeStruct((nch, 1, S5_LANES), F32),
                   jax.ShapeDtypeStruct((L, S5_LANES), MXU_DTYPE), jax.ShapeDtypeStruct((L, S5_LANES), MXU_DTYPE)],
        scratch_shapes=[pltpu.VMEM((tl, S5_LANES), F32), pltpu.VMEM((tl, S5_LANES), F32),
                        pltpu.VMEM((8, S5_LANES), F32), pltpu.VMEM((8, S5_LANES), F32),
                        pltpu.VMEM((1, S5_LANES), F32), pltpu.VMEM((1, S5_LANES), F32)],
        sem=("arbitrary",),
    )(p, p, wbd, cre, cim, atab, d_skip, w_glu, b_glu)


def s5_backward(p, dya, st_re, st_im, sv_re, sv_im, wbd, cre, cim, atab, d_skip, w_glu, b_glu, plan=None):
    L = p.shape[0]
    tl = min(TL_S5, L)
    t8 = tl // 8
    nch = L // tl
    rev = lambda i: (nch - 1 - i, 0)
    rev1 = lambda i: (nch - 1 - i, 1)
    rev3 = lambda i: (nch - 1 - i, 0, 0)
    ct_shape = (S5_KBLK, cre.shape[2], cre.shape[1])

    def body(u_ref, z_ref, dya_ref, str_ref, sti_ref, s_re, s_im, wbd_ref, cre_ref, cim_ref, at_ref,
             d_ref, wg_ref, bg_ref,
             dp_ref, dwbd_ref, dcre_ref, dcim_ref, dabr_ref, dabi_ref, dd_ref, dwg_ref, dbg_ref,
             g_re, g_im, car_re, car_im):
        @pl.when(pl.program_id(0) == 0)
        def _():
            car_re[...] = jnp.zeros_like(car_re)
            car_im[...] = jnp.zeros_like(car_im)
            for r in (dwbd_ref, dcre_ref, dcim_ref, dabr_ref, dabi_ref, dd_ref, dwg_ref, dbg_ref):
                r[...] = jnp.zeros_like(r)

        u = u_ref[...]
        a_re, a_im, at_re, at_im = at_ref[0:1], at_ref[1:2], at_ref[2:3], at_ref[3:4]
        y = _s5_readout(s_re, s_im, cre_ref, cim_ref) + d_ref[...] * u
        yg, dyg = _gelu_and_grad(y)
        gate = _sigmoid(_mm(yg, wg_ref[...]) + bg_ref[...])
        sz, dsz = _silu_and_grad(z_ref[...])
        dya = dya_ref[...]
        s5out = yg * gate
        dp_ref[:, 1024:] = (dya * s5out * dsz).astype(dp_ref.dtype)
        ds5 = dya * sz
        dt = ds5 * yg * gate * (1.0 - gate)
        dwg_ref[...] += _mm_tn(yg, dt)
        dbg_ref[...] += jnp.sum(dt, axis=0, keepdims=True)
        dyv = (ds5 * gate + _mm_nt(dt, wg_ref[...])) * dyg
        dd_ref[...] += jnp.sum(dyv * u, axis=0, keepdims=True)

        for k in range(S5_KBLK):
            lanes = slice(512 * k, 512 * (k + 1))
            dyk = dyv[:, 128 * k:128 * (k + 1)]
            g_re[:, lanes] = _mm_nt(dyk, cre_ref[k])
            g_im[:, lanes] = -_mm_nt(dyk, cim_ref[k])
            dcre_ref[k] += _mm_tn(dyk, s_re[:, lanes])
            dcim_ref[k] -= _mm_tn(dyk, s_im[:, lanes])

        blk = _LANE_BLK_BWD
        for b in range(S5_LANES // blk):
            lanes = slice(blk * b, blk * (b + 1))
            ar = jnp.broadcast_to(a_re[:, lanes], (8, blk))
            ai = jnp.broadcast_to(a_im[:, lanes], (8, blk))

            def local(j, carry, lanes=lanes, ar=ar, ai=ai):
                r = pl.multiple_of((t8 - 1 - j) * 8, 8)
                gr, gi = _cmulc_add(ar, ai, carry[0], carry[1], g_re[pl.ds(r, 8), lanes], g_im[pl.ds(r, 8), lanes])
                g_re[pl.ds(r, 8), lanes] = gr
                g_im[pl.ds(r, 8), lanes] = gi
                return gr, gi

            zero = jnp.zeros((8, blk), F32)
            fr, fi = lax.fori_loop(0, t8, local, (zero, zero), unroll=True)
            tr, ti = at_re[:, lanes], at_im[:, lanes]
            hr, hi = car_re[:, lanes], car_im[:, lanes]
            hrs, his = [hr], [hi]
            for j in range(7, -1, -1):
                hr, hi = _cmulc_add(tr, ti, hr, hi, fr[j:j + 1], fi[j:j + 1])
                hrs.append(hr)
                his.append(hi)
            car_re[:, lanes] = hrs[8]
            car_im[:, lanes] = his[8]
            in_r = jnp.concatenate(hrs[7::-1], axis=0)
            in_i = jnp.concatenate(his[7::-1], axis=0)

            wr, wi, nr, ni, accr, acci = in_r, in_i, zero, zero, zero, zero
            for pair in range(t8 // 2 - 1, -1, -1):
                rows = slice(16 * pair, 16 * pair + 16)
                s16r, s16i = s_re[rows, lanes].astype(F32), s_im[rows, lanes].astype(F32)
                for half in (1, 0):
                    r = 16 * pair + 8 * half
                    sr, si = s16r[8 * half:8 * half + 8], s16i[8 * half:8 * half + 8]
                    accr, acci = accr + (sr * nr + si * ni), acci + (sr * ni - si * nr)
                    wr, wi = ar * wr + ai * wi, ar * wi - ai * wr
                    nr, ni = g_re[r:r + 8, lanes] + wr, g_im[r:r + 8, lanes] + wi
                    g_re[r:r + 8, lanes] = nr
                    g_im[r:r + 8, lanes] = ni
            lr, li = s_re[tl - 16:tl, lanes].astype(F32)[8:], s_im[tl - 16:tl, lanes].astype(F32)[8:]
            row0 = lax.broadcasted_iota(jnp.int32, (8, blk), 0) == 0
            sr = jnp.where(row0, jnp.broadcast_to(str_ref[0][:, lanes], (8, blk)), pltpu.roll(lr, 1, 0))
            si = jnp.where(row0, jnp.broadcast_to(sti_ref[0][:, lanes], (8, blk)), pltpu.roll(li, 1, 0))
            dabr_ref[:, lanes] += accr + (sr * nr + si * ni)
            dabi_ref[:, lanes] += acci + (sr * ni - si * nr)

        dus = []
        for k in range(S5_KBLK):
            lanes = slice(512 * k, 512 * (k + 1))
            g = jnp.concatenate([g_re[:, lanes], g_im[:, lanes]], axis=1)
            dwbd_ref[k] += _mm_tn(u[:, 128 * k:128 * (k + 1)], g)
            dus.append(_mm_nt(g, wbd_ref[k]))
        du = jnp.concatenate(dus, axis=1) + dyv * d_ref[...]
        dp_ref[:, :1024] = du.astype(dp_ref.dtype)

    shp = lambda *s: jax.ShapeDtypeStruct(s, F32)
    return _call(
        body, plan, name="s5_backward", grid=(nch,),
        in_specs=[pl.BlockSpec((tl, 1024), rev), pl.BlockSpec((tl, 1024), rev1), pl.BlockSpec((tl, 1024), rev),
                  pl.BlockSpec((1, 1, S5_LANES), rev3), pl.BlockSpec((1, 1, S5_LANES), rev3),
                  pl.BlockSpec((tl, S5_LANES), rev), pl.BlockSpec((tl, S5_LANES), rev),
                  _full(wbd.shape), _full(cre.shape), _full(cim.shape), _full(atab.shape),
                  _full((1, 1024)), _full((1024, 1024)), _full((1, 1024))],
        out_specs=[pl.BlockSpec((tl, 2048), rev), _full(wbd.shape), _full(ct_shape), _full(ct_shape),
                   _full((8, S5_LANES)), _full((8, S5_LANES)), _full((1, 1024)), _full((1024, 1024)), _full((1, 1024))],
        out_shape=[jax.ShapeDtypeStruct((L, 2048), MXU_DTYPE), shp(*wbd.shape), shp(*ct_shape), shp(*ct_shape),
                   shp(8, S5_LANES), shp(8, S5_LANES), shp(1, 1024), shp(1024, 1024), shp(1, 1024)],
        scratch_shapes=[pltpu.VMEM((tl, S5_LANES), F32), pltpu.VMEM((tl, S5_LANES), F32),
                        pltpu.VMEM((1, S5_LANES), F32), pltpu.VMEM((1, S5_LANES), F32)],
        sem=("arbitrary",),
    )(p, p, dya, st_re, st_im, sv_re, sv_im, wbd, cre, cim, atab, d_skip, w_glu, b_glu)


def _block_diag(w, rows_first):
    g8 = w.reshape(S5_KBLK, 8, w.shape[1], w.shape[2])
    eye = jnp.eye(8, dtype=w.dtype)
    out = jnp.einsum('kgab,fg->kfagb', g8, eye)
    return out.reshape(S5_KBLK, 8 * w.shape[1], 8 * w.shape[2])


def _block_diag_extract(wbd, a, b):
    w5 = wbd.reshape(S5_KBLK, 8, a, 8, b)
    idx = jnp.arange(8)
    return w5[:, idx, :, idx, :].transpose(1, 0, 2, 3).reshape(S5_GROUPS, a, b)


def _ret_constants():
    log_g = np.log1p(-np.exp2(-5.0 - np.arange(RET_HEADS, dtype=np.float32))).astype(np.float32)
    idx = np.arange(RET_CHUNK, dtype=np.float32)
    diff = idx[:, None] - idx[None, :]
    decay = np.where(diff >= 0, np.exp(log_g[:, None, None] * np.maximum(diff, 0.0)), 0.0).astype(np.float32)
    xi = np.exp(log_g[None, :] * (idx[:, None] + 1.0)).astype(np.float32)
    zeta = np.exp(log_g[None, :] * (RET_CHUNK - 1.0 - idx[:, None])).astype(np.float32)
    chunk_decay = np.exp(log_g * RET_CHUNK).astype(np.float32)
    return decay, xi, zeta, chunk_decay


def _rope_tables(L):
    half = RET_DK // 2
    inv = ROPE_BASE ** (-jnp.arange(half, dtype=F32) / half)
    ang = jnp.arange(L, dtype=F32)[:, None] * inv[None, :]
    return jnp.cos(ang), jnp.sin(ang)


def _rot(xh, cos, sin):
    x1, x2 = xh[:, :128], xh[:, 128:]
    return jnp.concatenate([x1 * cos - x2 * sin, x1 * sin + x2 * cos], axis=1)


def _rot_t(dh, cos, sin):
    d1, d2 = dh[:, :128], dh[:, 128:]
    return jnp.concatenate([d1 * cos + d2 * sin, d2 * cos - d1 * sin], axis=1)


RET_PER_STEP = 4


def _ret_setup(L):
    nc = L // RET_CHUNK
    per = RET_PER_STEP if nc % RET_PER_STEP == 0 else 1
    decay_np, xi_np, zeta_np, cd_np = _ret_constants()
    tables = (jnp.asarray(decay_np), jnp.asarray(np.tile(xi_np, (per, 1))), jnp.asarray(np.tile(zeta_np, (per, 1))))
    return nc // per, per, tables, [float(c) for c in cd_np]


def _ret_rows(q_ref, k_ref, v_ref, cos_ref, sin_ref, xi_ref, zeta_ref):
    H = range(RET_HEADS)
    hs = [slice(RET_DK * h, RET_DK * (h + 1)) for h in H]
    cs, sn = cos_ref[...], sin_ref[...]
    qh = [_rot(q_ref[:, hs[h]], cs, sn) for h in H]
    kh = [_rot(k_ref[:, hs[h]], cs, sn) * (RET_DK ** -0.5) for h in H]
    vh = [v_ref[:, hs[h]] for h in H]
    qx = [qh[h] * xi_ref[:, h:h + 1] for h in H]
    kz = [kh[h] * zeta_ref[:, h:h + 1] for h in H]
    return hs, cs, sn, qh, kh, vh, qx, kz


def _ret_normed(qh, kh, vh, qx, dec_ref, prevs, per):
    H, C = range(RET_HEADS), range(per)
    rs = [slice(RET_CHUNK * c, RET_CHUNK * (c + 1)) for c in C]
    sc = [[_mm_nt(qh[h][rs[c]], kh[h][rs[c]]) * dec_ref[h] for h in H] for c in C]
    inner = [[_mm(sc[c][h], vh[h][rs[c]]) for h in H] for c in C]
    cross = [[_mm(qx[h][rs[c]], prevs[c][h]) for h in H] for c in C]
    o = [jnp.concatenate([inner[c][h] + cross[c][h] for c in C], axis=0) for h in H]
    oc = [o[h] - jnp.mean(o[h], axis=-1, keepdims=True) for h in H]
    rstd = [lax.rsqrt(jnp.mean(oc[h] * oc[h], axis=-1, keepdims=True) + NORM_EPS) for h in H]
    on = [oc[h] * rstd[h] for h in H]
    return rs, sc, rstd, on


def retention_forward(p, cos, sin, gain):
    L = p.shape[0]
    steps, per, (decay, xi, zeta), cd = _ret_setup(L)
    rows = RET_CHUNK * per

    def body(q_ref, k_ref, v_ref, z_ref, cos_ref, sin_ref, dec_ref, xi_ref, zeta_ref, gain_ref,
             yb_ref, prev_ref, state):
        @pl.when(pl.program_id(0) == 0)
        def _():
            state[...] = jnp.zeros_like(state)

        H, C = range(RET_HEADS), range(per)
        hs, cs, sn, qh, kh, vh, qx, kz = _ret_rows(q_ref, k_ref, v_ref, cos_ref, sin_ref, xi_ref, zeta_ref)
        prevs = [[state[h] for h in H]]
        for c in C:
            rs_c = slice(RET_CHUNK * c, RET_CHUNK * (c + 1))
            prevs.append([prevs[c][h] * cd[h] + _mm_tn(kz[h][rs_c], vh[h][rs_c]) for h in H])
        _, _, _, on = _ret_normed(qh, kh, vh, qx, dec_ref, prevs, per)
        sz, _ = _silu_and_grad(z_ref[...])
        for h in H:
            for c in C:
                prev_ref[c, h] = prevs[c][h].astype(prev_ref.dtype)
            state[h] = prevs[per][h]
            yb_ref[:, hs[h]] = (on[h] * gain_ref[:, hs[h]] * sz[:, hs[h]]).astype(yb_ref.dtype)

    col0 = p.shape[1] // 1024 - 4
    blk = lambda c: pl.BlockSpec((rows, 1024), lambda i, c=c: (i, c + col0))
    return pl.pallas_call(
        body, name="retention_forward", grid=(steps,),
        in_specs=[blk(0), blk(1), blk(2), blk(3),
                  pl.BlockSpec((rows, 128), lambda i: (i, 0)), pl.BlockSpec((rows, 128), lambda i: (i, 0)),
                  _full(decay.shape), _full(xi.shape), _full(zeta.shape), _full((1, 1024))],
        out_specs=[pl.BlockSpec((rows, 1024), lambda i: (i, 0)),
                   pl.BlockSpec((per, RET_HEADS, RET_DK, RET_DK), lambda i: (i, 0, 0, 0))],
        out_shape=[jax.ShapeDtypeStruct((L, 1024), MXU_DTYPE),
                   jax.ShapeDtypeStruct((steps * per, RET_HEADS, RET_DK, RET_DK), MXU_DTYPE)],
        scratch_shapes=[pltpu.VMEM((RET_HEADS, RET_DK, RET_DK), F32)],
        compiler_params=_cparams(("arbitrary",)),
    )(p, p, p, p, cos, sin, decay, xi, zeta, gain)


def retention_backward(p, dy, prevs, cos, sin, gain, plan=None):
    L = p.shape[0]
    steps, per, (decay, xi, zeta), cd = _ret_setup(L)
    rows = RET_CHUNK * per
    scale = RET_DK ** -0.5

    def body(q_ref, k_ref, v_ref, z_ref, dyb_ref, prev_ref, cos_ref, sin_ref, dec_ref, xi_ref, zeta_ref, gain_ref,
             dp_ref, dgain_ref, dstate):
        @pl.when(pl.program_id(0) == 0)
        def _():
            dstate[...] = jnp.zeros_like(dstate)
            dgain_ref[...] = jnp.zeros_like(dgain_ref)

        H, C = range(RET_HEADS), range(per)
        hs, cs, sn, qh, kh, vh, qx, kz = _ret_rows(q_ref, k_ref, v_ref, cos_ref, sin_ref, xi_ref, zeta_ref)
        prevs = [[prev_ref[c, h] for h in H] for c in C]
        rs, sc, rstd, on = _ret_normed(qh, kh, vh, qx, dec_ref, prevs, per)
        sz, dsz = _silu_and_grad(z_ref[...])
        dyb = dyb_ref[...]
        dong = [dyb[:, hs[h]] * sz[:, hs[h]] for h in H]
        don = [dong[h] * gain_ref[:, hs[h]] for h in H]
        do = [rstd[h] * (don[h] - jnp.mean(don[h], axis=-1, keepdims=True)
                         - on[h] * jnp.mean(don[h] * on[h], axis=-1, keepdims=True)) for h in H]
        dsc = [[_mm_nt(do[h][rs[c]], vh[h][rs[c]]) * dec_ref[h] for h in H] for c in C]
        dq_st = [[_mm_nt(do[h][rs[c]], prevs[c][h]) for h in H] for c in C]
        dnew = [[_mm_tn(qx[h][rs[c]], do[h][rs[c]]) for h in H] for c in C]
        dsts = [None] * per + [[dstate[h] for h in H]]
        for c in reversed(C):
            dsts[c] = [dsts[c + 1][h] * cd[h] + dnew[c][h] for h in H]
        dk_st = [[_mm_nt(vh[h][rs[c]], dsts[c + 1][h]) for h in H] for c in C]
        dv_st = [[_mm(kz[h][rs[c]], dsts[c + 1][h]) for h in H] for c in C]
        rows_of = lambda parts: jnp.concatenate(parts, axis=0)
        dqh = [rows_of([_mm(dsc[c][h], kh[h][rs[c]]) for c in C])
               + rows_of([dq_st[c][h] for c in C]) * xi_ref[:, h:h + 1] for h in H]
        dkh = [rows_of([_mm_tn(dsc[c][h], qh[h][rs[c]]) for c in C])
               + rows_of([dk_st[c][h] for c in C]) * zeta_ref[:, h:h + 1] for h in H]
        dvh = [rows_of([_mm_tn(sc[c][h], do[h][rs[c]]) + dv_st[c][h] for c in C]) for h in H]
        for h in H:
            dstate[h] = dsts[0][h]
            dgain_ref[:, hs[h]] += jnp.sum(dong[h] * on[h], axis=0, keepdims=True)
            dp_ref[:, hs[h]] = (_rot_t(dkh[h], cs, sn) * scale).astype(dp_ref.dtype)
            dp_ref[:, 1024 + RET_DK * h:1024 + RET_DK * (h + 1)] = dvh[h].astype(dp_ref.dtype)
            dp_ref[:, 2048 + RET_DK * h:2048 + RET_DK * (h + 1)] = (
                dyb[:, hs[h]] * on[h] * gain_ref[:, hs[h]] * dsz[:, hs[h]]).astype(dp_ref.dtype)
            dp_ref[:, 3072 + RET_DK * h:3072 + RET_DK * (h + 1)] = _rot_t(dqh[h], cs, sn).astype(dp_ref.dtype)

    col0 = p.shape[1] // 1024 - 4
    blk = lambda c: pl.BlockSpec((rows, 1024), lambda i, c=c: (steps - 1 - i, c + col0))
    tab = pl.BlockSpec((rows, 128), lambda i: (steps - 1 - i, 0))
    return _call(
        body, plan, name="retention_backward", grid=(steps,),
        in_specs=[blk(0), blk(1), blk(2), blk(3), pl.BlockSpec((rows, 1024), lambda i: (steps - 1 - i, 0)),
                  pl.BlockSpec((per, RET_HEADS, RET_DK, RET_DK), lambda i: (steps - 1 - i, 0, 0, 0)),
                  tab, tab, _full(decay.shape), _full(xi.shape), _full(zeta.shape), _full((1, 1024))],
        out_specs=[pl.BlockSpec((rows, 4096), lambda i: (steps - 1 - i, 0)), _full((1, 1024))],
        out_shape=[jax.ShapeDtypeStruct((L, 4096), MXU_DTYPE), jax.ShapeDtypeStruct((1, 1024), F32)],
        scratch_shapes=[pltpu.VMEM((RET_HEADS, RET_DK, RET_DK), F32)],
        sem=("arbitrary",),
    )(p, p, p, p, dy, prevs, cos, sin, decay, xi, zeta, gain)


def _sgu_mix(p_ref, gain_ref, wm_ref, bt_ref, tl):
    pu, pv, z = p_ref[:, :2048], p_ref[:, 2048:4096], p_ref[:, 4096:]
    (u, du), (v, dv) = _gelu_and_grad(pu), _gelu_and_grad(pv)
    mu = jnp.mean(v, axis=-1, keepdims=True)
    vc = v - mu
    rstd = lax.rsqrt(jnp.mean(vc * vc, axis=-1, keepdims=True) + NORM_EPS)
    vn = vc * rstd
    vg = vn * gain_ref[...]
    mask = (lax.broadcasted_iota(jnp.int32, (SGU_CHUNK, SGU_CHUNK), 0)
            >= lax.broadcasted_iota(jnp.int32, (SGU_CHUNK, SGU_CHUNK), 1))
    wms = [jnp.where(mask, wm_ref[g], 0.0) for g in range(SGU_GROUPS)]
    rows = []
    for c in range(tl // SGU_CHUNK):
        rs = slice(SGU_CHUNK * c, SGU_CHUNK * (c + 1))
        cols = []
        for g in range(SGU_GROUPS):
            gs = slice(SGU_GDIM * g, SGU_GDIM * (g + 1))
            cols.append(_mm(wms[g], vg[rs, gs]) + bt_ref[:, g:g + 1])
        rows.append(jnp.concatenate(cols, axis=1))
    s = rows[0] if len(rows) == 1 else jnp.concatenate(rows, axis=0)
    return du, dv, z, u, vn, rstd, vg, wms, mask, s


def sgu_forward(p, gain, wm, bt):
    L = p.shape[0]
    tl = min(TL_SGU, L)

    def body(p_ref, gain_ref, wm_ref, bt_ref, y_ref):
        _, _, z, u, _, _, _, _, _, s = _sgu_mix(p_ref, gain_ref, wm_ref, bt_ref, tl)
        sz, _ = _silu_and_grad(z)
        y_ref[...] = (u * s * sz).astype(y_ref.dtype)

    return pl.pallas_call(
        body, name="sgu_forward", grid=(L // tl,),
        in_specs=[pl.BlockSpec((tl, ODD_IN), lambda i: (i, 0)), _full((1, 2048)), _full(wm.shape), _full(bt.shape)],
        out_specs=pl.BlockSpec((tl, 2048), lambda i: (i, 0)),
        out_shape=jax.ShapeDtypeStruct((L, 2048), MXU_DTYPE),
        compiler_params=_cparams(("arbitrary",)),
    )(p, gain, wm, bt)


def sgu_backward(p, dy, gain, wm, bt, plan=None):
    L = p.shape[0]
    tl = min(TL_SGU, L)

    def body(p_ref, dy_ref, gain_ref, wm_ref, bt_ref, dp_ref, dgain_ref, dwm_ref, dbt_ref):
        @pl.when(pl.program_id(0) == 0)
        def _():
            dgain_ref[...] = jnp.zeros_like(dgain_ref)
            dwm_ref[...] = jnp.zeros_like(dwm_ref)
            dbt_ref[...] = jnp.zeros_like(dbt_ref)

        gu, gv, z, u, vn, rstd, vg, wms, mask, s = _sgu_mix(p_ref, gain_ref, wm_ref, bt_ref, tl)
        sz, dsz = _silu_and_grad(z)
        dyv = dy_ref[...]
        dp_ref[:, 4096:] = (dyv * u * s * dsz).astype(dp_ref.dtype)
        dsg = dyv * sz
        dp_ref[:, :2048] = (dsg * s * gu).astype(dp_ref.dtype)
        ds = dsg * u
        rows = []
        dbs = [jnp.zeros((SGU_CHUNK, 1), F32) for _ in range(SGU_GROUPS)]
        for c in range(tl // SGU_CHUNK):
            rs = slice(SGU_CHUNK * c, SGU_CHUNK * (c + 1))
            cols = []
            for g in range(SGU_GROUPS):
                gs = slice(SGU_GDIM * g, SGU_GDIM * (g + 1))
                dsg_c = ds[rs, gs]
                dbs[g] = dbs[g] + jnp.sum(dsg_c, axis=1, keepdims=True)
                dwm_ref[g] += jnp.where(mask, _mm_nt(dsg_c, vg[rs, gs]), 0.0)
                cols.append(_mm_tn(wms[g], dsg_c))
            rows.append(jnp.concatenate(cols, axis=1))
        dbt_ref[...] += jnp.concatenate(dbs, axis=1)
        dvg = rows[0] if len(rows) == 1 else jnp.concatenate(rows, axis=0)
        dgain_ref[...] += jnp.sum(dvg * vn, axis=0, keepdims=True)
        dvn = dvg * gain_ref[...]
        dv = rstd * (dvn - jnp.mean(dvn, axis=-1, keepdims=True) - vn * jnp.mean(dvn * vn, axis=-1, keepdims=True))
        dp_ref[:, 2048:4096] = (dv * gv).astype(dp_ref.dtype)

    return _call(
        body, plan, name="sgu_backward", grid=(L // tl,),
        in_specs=[pl.BlockSpec((tl, ODD_IN), lambda i: (i, 0)), pl.BlockSpec((tl, 2048), lambda i: (i, 0)),
                  _full((1, 2048)), _full(wm.shape), _full(bt.shape)],
        out_specs=[pl.BlockSpec((tl, ODD_IN), lambda i: (i, 0)), _full((1, 2048)), _full(wm.shape), _full(bt.shape)],
        out_shape=[jax.ShapeDtypeStruct((L, ODD_IN), MXU_DTYPE), jax.ShapeDtypeStruct((1, 2048), F32),
                   jax.ShapeDtypeStruct(wm.shape, F32), jax.ShapeDtypeStruct(bt.shape, F32)],
        sem=("arbitrary",),
    )(p, dy, gain, wm, bt)


def cast_shards(mats):
    n = len(mats)
    steps = 8

    def body(*refs):
        for p in range(n):
            refs[n + p][...] = refs[p][...].astype(MXU_DTYPE)

    specs = [pl.BlockSpec((m.shape[0] // steps, m.shape[1]), lambda i: (i, 0)) for m in mats]
    return pl.pallas_call(
        body, name="cast_shards", grid=(steps,), in_specs=specs, out_specs=specs,
        out_shape=[jax.ShapeDtypeStruct(m.shape, MXU_DTYPE) for m in mats],
        compiler_params=_cparams(("arbitrary",)),
    )(*mats)


def local_grads(x, tgt, w):
    L = x.shape[0]
    ne, gf = w["norm_even"], w["final_norm"].reshape(1, D_MODEL)
    sh = dict(zip(MATRICES, cast_shards([w[n][0] for n in MATRICES])))
    lam_re, lam_im = w["s5_lam_re"][0], w["s5_lam_im"][0]
    log_dt = w["s5_log_dt"].reshape(S5_GROUPS, 1)
    bt_re = jnp.transpose(w["s5_b_re"][0], (2, 0, 1))
    bt_im = jnp.transpose(w["s5_b_im"][0], (2, 0, 1))
    c_re, c_im = w["s5_c_re"][0], w["s5_c_im"][0]
    wm = w["sgu_w_spatial"][0]
    bt = jnp.transpose(w["sgu_b_spatial"][0])

    tl5 = min(TL_S5, L)
    ab_re, ab_im, bb_re, bb_im, at_re, at_im = s5_params_fwd(lam_re, lam_im, log_dt, bt_re, bt_im, tl5 // 8)
    atab = jnp.stack([ab_re.reshape(S5_LANES), ab_im.reshape(S5_LANES),
                      at_re.reshape(S5_LANES), at_im.reshape(S5_LANES)])
    wbd = jnp.concatenate([_block_diag(jnp.transpose(bb_re, (1, 0, 2)), True),
                           _block_diag(jnp.transpose(bb_im, (1, 0, 2)), True)], axis=2).astype(MXU_DTYPE)
    cre = _block_diag(jnp.transpose(c_re, (0, 2, 1)), True).astype(MXU_DTYPE)
    cim = _block_diag(jnp.transpose(c_im, (0, 2, 1)), True).astype(MXU_DTYPE)
    cos, sin = _rope_tables(L)

    s5_cols = 2 * S5_WIDTH
    me = (2 * lax.axis_index("x") + lax.axis_index("y")).astype(jnp.int32)
    xs = stream_order(x, tl5)
    slab = lambda d: jnp.stack([me ^ d])
    zero = jnp.zeros((1,), jnp.int32)
    shards = [sh["w_in_even"][None]]
    (p1, h0s, h0), (got,) = even_in_slabs(x, xs, ne, shards[0], slab(0), zero, "even_in_0",
                                          plan=gather_plan([sh["w_in_even"]], only=1))
    for d in (1, 2, 3):
        shards.append(got)
        plan = gather_plan([sh["w_in_even"]], only=d + 1) if d < 3 else gather_plan([sh["s5_w_glu"]])
        (p1,), (got,) = even_in_slabs(x, xs, ne, shards[d], slab(d), zero, "even_in_%d" % d, p_in=p1, plan=plan)
    w_glu = got
    by_xor = jnp.concatenate(shards)
    w_in_e = [lax.dynamic_index_in_dim(by_xor, me ^ j, 0, keepdims=False) for j in range(N_CHIPS)]
    w_s5 = jnp.concatenate([w_in_e[0], w_in_e[1][:, :s5_cols - EVEN_IN // N_CHIPS]], axis=1)
    w_kvzq = jnp.concatenate([w_in_e[2], w_in_e[3], w_in_e[1][:, s5_cols - EVEN_IN // N_CHIPS:]], axis=1)
    w_glu = w_glu.reshape(S5_WIDTH, S5_WIDTH)
    (ya, st_re, st_im, sv_re, sv_im), (w_out_e, w_in_o, w_out_o, no, sg_gain) = s5_forward(
        p1, wbd, cre, cim, atab, w["s5_d"], w_glu, w["s5_b_glu"],
        gather_plan([sh["w_out_even"], sh["w_in_odd"], sh["w_out_odd"], w["norm_odd"], w["sgu_norm_gain"]]))
    w_out_e = w_out_e.reshape(2 * S5_WIDTH, D_MODEL)
    w_out_o = w_out_o.reshape(SGU_WIDTH, D_MODEL)
    no, sg_gain = no.reshape(1, D_MODEL), sg_gain.reshape(1, SGU_WIDTH)
    yb, prevs = retention_forward(p1, cos, sin, w["ret_gn_gain"])
    ya = token_order(ya, tl5)
    x1 = matmul_residual([ya, yb], w_out_e, x, "even_out")
    (p2, h1), _ = norm_matmul(x1, no, w_in_o, "odd_in")
    y2 = sgu_forward(p2, sg_gain, wm, bt)
    dx2, loss, dgf = out_proj_loss(y2, w_out_o, x1, gf, tgt, "odd_out_loss")

    g, landed = {}, {}
    shard_major = lambda a, n: a.reshape((N_CHIPS,) + w[n].shape[1:])
    dy2, g_w_out_o = out_proj_bwd(dx2, w_out_o, [y2], "odd_out_bwd")
    (dp2, g["sgu_norm_gain"], dwm, dbt), (landed["w_out_odd"],) = sgu_backward(
        p2, dy2, sg_gain, wm, bt, reduce_plan([shard_major(g_w_out_o, "w_out_odd")]))
    g_w_in_o, _ = in_proj_bwd_dw(h1, dp2, "odd_in_dw", ODD_IN // N_CHIPS)
    (dx1, g["norm_odd"]), _ = in_proj_bwd_dx(x1, no, [dp2], [w_in_o], dx2, "odd_in_dx")
    dya, dyb, g_w_out_e = out_proj_bwd(dx1, w_out_e, [ya, yb], "even_out_bwd")
    ((dpa, dwbd, dcre, dcim, dab_re, dab_im, g["s5_d"], g_w_glu, g["s5_b_glu"]),
     (landed["w_in_odd"], landed["w_out_even"])) = s5_backward(
        p1, stream_order(dya, tl5), st_re, st_im, sv_re, sv_im, wbd, cre, cim, atab, w["s5_d"], w_glu,
        w["s5_b_glu"], reduce_plan([g_w_in_o, shard_major(g_w_out_e, "w_out_even")]))

    dbb_re = jnp.transpose(_block_diag_extract(dwbd[:, :, :512], S5_GROUP, S5_STATE), (1, 0, 2))
    dbb_im = jnp.transpose(_block_diag_extract(dwbd[:, :, 512:], S5_GROUP, S5_STATE), (1, 0, 2))
    dlr, dli, ddt, dbt_re, dbt_im = s5_params_bwd(
        lam_re, lam_im, log_dt, bt_re, bt_im, dab_re.reshape(8, S5_GROUPS, S5_STATE),
        dab_im.reshape(8, S5_GROUPS, S5_STATE), dbb_re, dbb_im)
    g["s5_lam_re"], g["s5_lam_im"] = dlr[None], dli[None]
    g["s5_log_dt"] = ddt.reshape(1, S5_GROUPS)
    g["s5_b_re"], g["s5_b_im"] = dbt_re, dbt_im
    g["s5_c_re"] = _block_diag_extract(dcre, S5_GROUP, S5_STATE)[None]
    g["s5_c_im"] = _block_diag_extract(dcim, S5_GROUP, S5_STATE)[None]
    g["sgu_w_spatial"] = dwm[None]
    g["sgu_b_spatial"] = jnp.transpose(dbt)[None]
    g["final_norm"] = dgf.reshape(D_MODEL)
    g["loss"] = loss

    big_small = ("s5_b_re", "s5_b_im")
    mid_small = ("s5_c_re",)
    (dpb, g["ret_gn_gain"]), recv = retention_backward(
        p1, dyb, prevs, cos, sin, w["ret_gn_gain"],
        reduce_plan([shard_major(g_w_glu, "s5_w_glu")], [g[n] for n in big_small]))
    landed.update(zip(("s5_w_glu",) + big_small, recv))
    done = tuple(n for n in MATRICES if n != "w_in_even")
    part = {n: sum_slabs(landed[n], "sum_" + n) for n in done}
    g_w_in_e, recv = in_proj_bwd_dw(h0s, dpa, "even_in_dw_s5", 512, dtype=MXU_DTYPE,
                                    plan=_SiblingPlan([part[n] for n in done]))
    other = dict(zip(done, recv))
    small = tuple(n for n in SMALL if n != "norm_even" and n not in big_small + mid_small) + ("loss",)
    wb = EVEN_IN // N_CHIPS
    g_w_in_e, recv = in_proj_bwd_dw(h0, dpb, "even_in_dw_q", 512, first=s5_cols // 512, into=g_w_in_e,
                                    dtype=MXU_DTYPE, dp_first=2 * wb // 512, count=RET_HEADS * RET_DK // 512,
                                    plan=reduce_plan([], [g[n] for n in mid_small]))
    landed.update(zip(mid_small, recv))
    g_w_in_e, recv = in_proj_bwd_dw(h0, dpb, "even_in_dw_kvz", wb, first=2, into=g_w_in_e, dtype=MXU_DTYPE,
                                    count=2, plan=reduce_plan([], [g[n] for n in small]))
    landed.update(zip(small, recv))
    (dx0, g["norm_even"]), (landed["w_in_even"],) = in_proj_bwd_dx(
        x, ne, [token_order(dpa, tl5), dpb], [w_s5, w_kvzq], dx1, "even_in_dx", reduce_plan([g_w_in_e]))
    (landed["norm_even"],) = run_plan(reduce_plan([], [g["norm_even"]]), "exchange_norm_even")
    return dx0, landed, part, other


def _row_block(rows):
    return 128 if rows % 128 == 0 else rows


def sum_slabs(r, name):
    _, R, C = r.shape
    tr = _row_block(R)

    def body(r_ref, o_ref):
        a, b, c, d = (r_ref[k].astype(F32) for k in range(N_CHIPS))
        o_ref[...] = (a + b) + (c + d)

    return pl.pallas_call(
        body, name=name, grid=(R // tr,),
        in_specs=[pl.BlockSpec((N_CHIPS, tr, C), lambda i: (0, i, 0))],
        out_specs=pl.BlockSpec((tr, C), lambda i: (i, 0)),
        out_shape=jax.ShapeDtypeStruct((R, C), F32),
        compiler_params=_cparams(("arbitrary",)),
    )(r)


def _adam(w, m, v, g):
    mn = ADAM_B1 * m + (1.0 - ADAM_B1) * g
    vn = ADAM_B2 * v + (1.0 - ADAM_B2) * (g * g)
    m_hat = mn / (1.0 - ADAM_B1 ** ADAM_STEP)
    v_hat = vn / (1.0 - ADAM_B2 ** ADAM_STEP)
    return -ADAM_LR * (m_hat / (jnp.sqrt(v_hat) + ADAM_EPS) + ADAM_WD * w), mn, vn


def adam_update(w, m, v, ga, gb, name):
    R, C = w.shape
    tr = _row_block(R)

    def body(w_ref, m_ref, v_ref, ga_ref, gb_ref, g_out, d_out, m_out, v_out):
        g = ga_ref[...] + gb_ref[...]
        g_out[...] = g
        d_out[...], m_out[...], v_out[...] = _adam(w_ref[...], m_ref[...], v_ref[...], g)

    blk = pl.BlockSpec((tr, C), lambda i: (i, 0))
    return pl.pallas_call(
        body, name=name, grid=(R // tr,),
        in_specs=[blk] * 5, out_specs=[blk] * 4,
        out_shape=[jax.ShapeDtypeStruct((R, C), F32)] * 4,
        compiler_params=_cparams(("arbitrary",)),
    )(w, m, v, ga, gb)


WIDE_ROWS = ("s5_b_re", "s5_b_im")


def sum_small(landed):
    def body(*refs):
        k = len(refs) // 2
        for i in range(k):
            r = refs[i]
            refs[k + i][...] = (r[0] + r[1]) + (r[2] + r[3])

    names = list(landed)
    res = pl.pallas_call(
        body, name="sum_small", out_shape=[jax.ShapeDtypeStruct(landed[n].shape[1:], F32) for n in names],
        compiler_params=pltpu.CompilerParams(vmem_limit_bytes=VMEM_LIMIT),
    )(*[landed[n] for n in names])
    return dict(zip(names, res))


def adam_small(names, w, m, v, ga, gb):
    def body(*refs):
        k = len(refs) // 9
        me = 2 * lax.axis_index("x") + lax.axis_index("y")
        for i in range(k):
            w_ref, m_ref, v_ref, ga_ref, gb_ref = refs[i], refs[k + i], refs[2 * k + i], refs[3 * k + i], refs[4 * k + i]
            size = w_ref.shape[-1]
            if ga_ref.shape != w_ref.shape:
                part = pl.ds(pl.multiple_of(me * size, LANES), size)
                g = ga_ref[:, part] + gb_ref[:, part]
            else:
                g = ga_ref[...] + gb_ref[...]
            refs[5 * k + i][...] = g
            refs[6 * k + i][...], refs[7 * k + i][...], refs[8 * k + i][...] = _adam(w_ref[...], m_ref[...], v_ref[...], g)

    ins = [d[n] for d in (w, m, v, ga, gb) for n in names]
    outs = [jax.ShapeDtypeStruct(w[n].shape, F32) for _ in range(4) for n in names]
    res = pl.pallas_call(body, name="adam_small", out_shape=outs,
                         compiler_params=pltpu.CompilerParams(vmem_limit_bytes=VMEM_LIMIT))(*ins)
    k = len(names)
    return [dict(zip(names, res[j * k:(j + 1) * k])) for j in range(4)]


WEIGHTS = ("norm_even", "w_in_even", "s5_lam_re", "s5_lam_im", "s5_log_dt", "s5_b_re", "s5_b_im", "s5_c_re",
           "s5_c_im", "s5_d", "s5_w_glu", "s5_b_glu", "ret_gn_gain", "w_out_even", "norm_odd", "w_in_odd",
           "sgu_norm_gain", "sgu_w_spatial", "sgu_b_spatial", "w_out_odd", "final_norm")
MATRICES = ("w_in_even", "s5_w_glu", "w_out_even", "w_in_odd", "w_out_odd")
SHARDED_VECS = ("norm_odd", "sgu_norm_gain")
REPLICATED = tuple(n for n in WEIGHTS if n not in MATRICES and n not in SHARDED_VECS)
SMALL = tuple(n for n in WEIGHTS if n not in MATRICES)
LANES = 128


def kernel(x, norm_even, w_in_even, s5_lam_re, s5_lam_im, s5_log_dt, s5_b_re, s5_b_im, s5_c_re, s5_c_im, s5_d, s5_w_glu, s5_b_glu, ret_gn_gain, w_out_even, norm_odd, w_in_odd, sgu_norm_gain, sgu_w_spatial, sgu_b_spatial, w_out_odd, final_norm, loss_target, m_norm_even, m_w_in_even, m_s5_lam_re, m_s5_lam_im, m_s5_log_dt, m_s5_b_re, m_s5_b_im, m_s5_c_re, m_s5_c_im, m_s5_d, m_s5_w_glu, m_s5_b_glu, m_ret_gn_gain, m_w_out_even, m_norm_odd, m_w_in_odd, m_sgu_norm_gain, m_sgu_w_spatial, m_sgu_b_spatial, m_w_out_odd, m_final_norm, v_norm_even, v_w_in_even, v_s5_lam_re, v_s5_lam_im, v_s5_log_dt, v_s5_b_re, v_s5_b_im, v_s5_c_re, v_s5_c_im, v_s5_d, v_s5_w_glu, v_s5_b_glu, v_ret_gn_gain, v_w_out_even, v_norm_odd, v_w_in_odd, v_sgu_norm_gain, v_sgu_w_spatial, v_sgu_b_spatial, v_w_out_odd, v_final_norm):
    w = dict(norm_even=norm_even, w_in_even=w_in_even, s5_lam_re=s5_lam_re, s5_lam_im=s5_lam_im, s5_log_dt=s5_log_dt, s5_b_re=s5_b_re, s5_b_im=s5_b_im, s5_c_re=s5_c_re, s5_c_im=s5_c_im, s5_d=s5_d, s5_w_glu=s5_w_glu, s5_b_glu=s5_b_glu, ret_gn_gain=ret_gn_gain, w_out_even=w_out_even, norm_odd=norm_odd, w_in_odd=w_in_odd, sgu_norm_gain=sgu_norm_gain, sgu_w_spatial=sgu_w_spatial, sgu_b_spatial=sgu_b_spatial, w_out_odd=w_out_odd, final_norm=final_norm)
    m = dict(norm_even=m_norm_even, w_in_even=m_w_in_even, s5_lam_re=m_s5_lam_re, s5_lam_im=m_s5_lam_im, s5_log_dt=m_s5_log_dt, s5_b_re=m_s5_b_re, s5_b_im=m_s5_b_im, s5_c_re=m_s5_c_re, s5_c_im=m_s5_c_im, s5_d=m_s5_d, s5_w_glu=m_s5_w_glu, s5_b_glu=m_s5_b_glu, ret_gn_gain=m_ret_gn_gain, w_out_even=m_w_out_even, norm_odd=m_norm_odd, w_in_odd=m_w_in_odd, sgu_norm_gain=m_sgu_norm_gain, sgu_w_spatial=m_sgu_w_spatial, sgu_b_spatial=m_sgu_b_spatial, w_out_odd=m_w_out_odd, final_norm=m_final_norm)
    v = dict(norm_even=v_norm_even, w_in_even=v_w_in_even, s5_lam_re=v_s5_lam_re, s5_lam_im=v_s5_lam_im, s5_log_dt=v_s5_log_dt, s5_b_re=v_s5_b_re, s5_b_im=v_s5_b_im, s5_c_re=v_s5_c_re, s5_c_im=v_s5_c_im, s5_d=v_s5_d, s5_w_glu=v_s5_w_glu, s5_b_glu=v_s5_b_glu, ret_gn_gain=v_ret_gn_gain, w_out_even=v_w_out_even, norm_odd=v_norm_odd, w_in_odd=v_w_in_odd, sgu_norm_gain=v_sgu_norm_gain, sgu_w_spatial=v_sgu_w_spatial, sgu_b_spatial=v_sgu_b_spatial, w_out_odd=v_w_out_odd, final_norm=v_final_norm)

    grad_x, landed, part, other = local_grads(x[0], loss_target[0], w)

    small = SMALL + ("loss",)
    part["w_in_even"] = sum_slabs(landed["w_in_even"], "sum_w_in_even")
    part.update(sum_small({n: landed[n] for n in small}))
    names = ("w_in_even",) + small
    other.update(zip(names, run_plan(_SiblingPlan([part[n] for n in names]), "sibling_exchange")))

    wt, mt, vt = dict(w), dict(m), dict(v)
    for n in WIDE_ROWS:
        wt[n], mt[n], vt[n] = (jnp.transpose(a[n][0], (2, 0, 1)) for a in (w, m, v))
    out_g, out_d, out_m, out_v = adam_small(SMALL, wt, mt, vt, part, other)
    for n in WIDE_ROWS:
        for out in (out_g, out_d, out_m, out_v):
            out[n] = jnp.transpose(out[n], (1, 2, 0))[None]
    for n in MATRICES:
        res = adam_update(w[n][0], m[n][0], v[n][0], part[n], other[n], "adam_" + n)
        out_g[n], out_d[n], out_m[n], out_v[n] = (r[None] for r in res)
    total_loss = (part["loss"] + other["loss"])[0, 0]

    return (total_loss, grad_x[None], *[out_g[n] for n in WEIGHTS], *[out_d[n] for n in WEIGHTS],
            *[out_m[n] for n in WEIGHTS], *[out_v[n] for n in WEIGHTS])
```

```python
import functools
import math

import numpy as np
import jax
import jax.numpy as jnp
from jax import lax
from jax.experimental import pallas as pl
from jax.experimental.pallas import tpu as pltpu

F32 = jnp.float32
MXU_DTYPE = jnp.bfloat16
NORM_EPS = 1e-6
D_MODEL = 1024
S5_WIDTH = 1024
S5_GROUP = 16
S5_GROUPS = 64
S5_STATE = 64
S5_LANES = S5_GROUPS * S5_STATE
S5_KBLK = 8
RET_HEADS = 4
RET_DK = 256
RET_CHUNK = 128
ROPE_BASE = 10000.0
SGU_WIDTH = 2048
SGU_GROUPS = 4
SGU_GDIM = 512
SGU_CHUNK = 128
EVEN_IN = 6144
ODD_IN = 6144
ADAM_LR = 0.001
ADAM_B1 = 0.9
ADAM_B2 = 0.999
ADAM_EPS = 1e-08
ADAM_WD = 0.01
ADAM_STEP = 10
N_CHIPS = 4
VMEM_LIMIT = 56 * 1024 * 1024

TL_PROJ = 512
TL_DW = 1024
TL_S5 = 256
TL_SGU = 256


def _cparams(sem, **kw):
    return pltpu.CompilerParams(dimension_semantics=sem, vmem_limit_bytes=VMEM_LIMIT, **kw)


def _mm(a, b):
    return jnp.dot(a.astype(MXU_DTYPE), b.astype(MXU_DTYPE), preferred_element_type=F32)


def _mm_nt(a, b):
    return lax.dot_general(a.astype(MXU_DTYPE), b.astype(MXU_DTYPE),
                           (((1,), (1,)), ((), ())), preferred_element_type=F32)


def _mm_tn(a, b):
    return lax.dot_general(a.astype(MXU_DTYPE), b.astype(MXU_DTYPE),
                           (((0,), (0,)), ((), ())), preferred_element_type=F32)


_GELU_C = math.sqrt(2.0 / math.pi)


def _gelu_parts(x):
    x2 = x * x
    th = jnp.tanh(x * (_GELU_C + (_GELU_C * 0.044715) * x2))
    hx = 0.5 * x
    return hx + hx * th, th, x2, hx


def _gelu(x):
    return _gelu_parts(x)[0]


def _gelu_and_grad(x):
    g, th, x2, hx = _gelu_parts(x)
    return g, (0.5 + 0.5 * th) + hx * (1.0 - th * th) * (_GELU_C + (3.0 * _GELU_C * 0.044715) * x2)


def _gelu_grad(x):
    return _gelu_and_grad(x)[1]


def _sigmoid(x):
    return 1.0 / (1.0 + jnp.exp(-x))


def _silu_and_grad(x):
    s = _sigmoid(x)
    return x * s, s * (1.0 + x * (1.0 - s))


def _rms(x):
    return lax.rsqrt(jnp.mean(x * x, axis=-1, keepdims=True) + NORM_EPS)


def _full(shape):
    nd = len(shape)
    return pl.BlockSpec(shape, lambda *_: (0,) * nd)


MESH = pl.DeviceIdType.MESH
ANY = pl.BlockSpec(memory_space=pl.ANY)


def _place():
    return lax.axis_index("x"), lax.axis_index("y"), lax.axis_index("c")


def _chip_peer(x, y, c, d):
    return (1 - x if d >= 2 else x, 1 - y if d % 2 else y, c)


class _Plan:
    def __init__(self, inputs, out_shape, build):
        self.inputs, self.out_shape, self._build = list(inputs), list(out_shape), build
        n = len(self.inputs)
        self.sems = [pltpu.SemaphoreType.DMA((n, 3)), pltpu.SemaphoreType.DMA((n, 3)), pltpu.SemaphoreType.DMA((n,))]

    def start(self, in_refs, out_refs, sems):
        send, recv, local = self._build(in_refs, out_refs, sems)
        for p in range(len(self.inputs)):
            local[p].start()
            for cp in send[p]:
                cp.start()

    def wait(self, in_refs, out_refs, sems):
        send, recv, local = self._build(in_refs, out_refs, sems)
        for p in range(len(self.inputs)):
            for cp in recv[p]:
                cp.wait_recv()
        for p in range(len(self.inputs)):
            for cp in send[p]:
                cp.wait_send()
            local[p].wait()


class _GatherPlan:
    def __init__(self, shards, only=None):
        n = len(shards)
        self.n, self.only = n, only
        self.peers = (1, 2, 3) if only is None else (only,)
        self.inputs = list(shards)
        slabs = N_CHIPS if only is None else 1
        self.out_shape = [jax.ShapeDtypeStruct((slabs,) + s.shape, s.dtype) for s in shards]
        self.halved = [s.shape[0] % 32 == 0 for s in shards]
        self.sems = [pltpu.SemaphoreType.DMA((n, 3)) for _ in range(4)] + [pltpu.SemaphoreType.DMA((n,))]

    def _copies(self, in_refs, out_refs, sems):
        ici_s, ici_r, d2d_s, d2d_r, loc = sems
        x, y, c = _place()
        me = 2 * x + y

        def rows(p, core):
            if not self.halved[p]:
                return slice(None)
            half = self.inputs[p].shape[0] // 2
            return pl.ds(pl.multiple_of(core * half, 16), half)

        def slab(chip):
            return chip if self.only is None else 0

        def ici(p, d, chip, core):
            return pltpu.make_async_remote_copy(
                src_ref=in_refs[p].at[rows(p, core)], dst_ref=out_refs[p].at[slab(chip), rows(p, core)],
                send_sem=ici_s.at[p, d - 1], recv_sem=ici_r.at[p, d - 1],
                device_id=_chip_peer(x, y, c, d), device_id_type=MESH)

        def d2d(p, d, core):
            part = out_refs[p].at[slab(me ^ d), rows(p, core)]
            return pltpu.make_async_remote_copy(
                src_ref=part, dst_ref=part, send_sem=d2d_s.at[p, d - 1], recv_sem=d2d_r.at[p, d - 1],
                device_id=(x, y, 1 - c), device_id_type=MESH)

        local = [pltpu.make_async_copy(in_refs[p], out_refs[p].at[slab(me)], loc.at[p]) for p in range(self.n)]
        return me, c, ici, d2d, local

    def start(self, in_refs, out_refs, sems):
        me, c, ici, d2d, local = self._copies(in_refs, out_refs, sems)
        for p in range(self.n):
            if self.only is None:
                local[p].start()
            for d in self.peers:
                ici(p, d, me, c).start()

    def wait(self, in_refs, out_refs, sems):
        me, c, ici, d2d, local = self._copies(in_refs, out_refs, sems)
        for p in range(self.n):
            for d in self.peers:
                ici(p, d, me ^ d, c).wait_recv()
                if self.halved[p]:
                    d2d(p, d, c).start()
        for p in range(self.n):
            for d in self.peers:
                if self.halved[p]:
                    d2d(p, d, 1 - c).wait_recv()
                    d2d(p, d, c).wait_send()
                ici(p, d, me, c).wait_send()
            if self.only is None:
                local[p].wait()


def gather_plan(shards, only=None):
    return _GatherPlan(shards, only)


def reduce_plan(shards, whole=()):
    n_s = len(shards)

    def build(in_refs, out_refs, sems):
        send_sems, recv_sems, loc_sems = sems
        x, y, c = _place()
        me = 2 * x + y

        def src(p, slab):
            return in_refs[p].at[slab] if p < n_s else in_refs[p]

        def remote(p, d):
            return pltpu.make_async_remote_copy(
                src_ref=src(p, me ^ d), dst_ref=out_refs[p].at[d], send_sem=send_sems.at[p, d - 1],
                recv_sem=recv_sems.at[p, d - 1], device_id=_chip_peer(x, y, c, d), device_id_type=MESH)

        n = len(in_refs)
        send = [[remote(p, d) for d in (1, 2, 3)] for p in range(n)]
        local = [pltpu.make_async_copy(src(p, me), out_refs[p].at[0], loc_sems.at[p]) for p in range(n)]
        return send, send, local

    outs = [jax.ShapeDtypeStruct(s.shape, s.dtype) for s in shards]
    outs += [jax.ShapeDtypeStruct((N_CHIPS,) + a.shape, a.dtype) for a in whole]
    return _Plan(list(shards) + list(whole), outs, build)


class _SiblingPlan:
    def __init__(self, arrs):
        self.inputs = list(arrs)
        self.out_shape = [jax.ShapeDtypeStruct(a.shape, a.dtype) for a in arrs]
        n = len(arrs)
        self.sems = [pltpu.SemaphoreType.DMA((n,)), pltpu.SemaphoreType.DMA((n,))]

    def _copies(self, in_refs, out_refs, sems):
        x, y, c = _place()
        return [pltpu.make_async_remote_copy(
            src_ref=in_refs[p], dst_ref=out_refs[p], send_sem=sems[0].at[p], recv_sem=sems[1].at[p],
            device_id=(x, y, 1 - c), device_id_type=MESH) for p in range(len(self.inputs))]

    def start(self, in_refs, out_refs, sems):
        for cp in self._copies(in_refs, out_refs, sems):
            cp.start()

    def wait(self, in_refs, out_refs, sems):
        copies = self._copies(in_refs, out_refs, sems)
        for cp in copies:
            cp.wait_recv()
        for cp in copies:
            cp.wait_send()


def run_plan(plan, name):
    n = len(plan.inputs)

    def body(*refs):
        plan.start(refs[:n], refs[n:2 * n], refs[2 * n:])
        plan.wait(refs[:n], refs[n:2 * n], refs[2 * n:])

    return pl.pallas_call(body, name=name, in_specs=[ANY] * n, out_specs=[ANY] * n, out_shape=plan.out_shape,
                          scratch_shapes=plan.sems)(*plan.inputs)


def _call(body, plan, *, name, grid, in_specs, out_specs, out_shape, sem, scratch_shapes=(), aliases=None,
          n_prefetch=0):
    aliases = {} if aliases is None else aliases
    single = not isinstance(out_shape, (list, tuple))
    out_specs = [out_specs] if single else list(out_specs)
    out_shape = [out_shape] if single else list(out_shape)
    n_in, n_out, n_scr = len(in_specs), len(out_specs), len(scratch_shapes)
    ci = 0 if plan is None else len(plan.inputs)
    co = 0 if plan is None else len(plan.out_shape)

    def hosted(*refs):
        pre, refs = refs[:n_prefetch], refs[n_prefetch:]
        ins, cins = refs[:n_in], refs[n_in:n_in + ci]
        k = n_in + ci
        outs, couts = refs[k:k + n_out], refs[k + n_out:k + n_out + co]
        k += n_out + co
        scr, sems = refs[k:k + n_scr], refs[k + n_scr:]
        ids = [pl.program_id(a) for a in range(len(grid))]
        first = functools.reduce(jnp.logical_and, [i == 0 for i in ids])
        last = functools.reduce(jnp.logical_and, [i == g - 1 for i, g in zip(ids, grid)])

        @pl.when(first)
        def _():
            plan.start(cins, couts, sems)

        body(*pre, *ins, *outs, *scr)

        @pl.when(last)
        def _():
            plan.wait(cins, couts, sems)

    def run(*args):
        hosting = plan is not None
        spec = pltpu.PrefetchScalarGridSpec(
            num_scalar_prefetch=n_prefetch, grid=grid,
            in_specs=list(in_specs) + ([ANY] * ci if hosting else []),
            out_specs=out_specs + ([ANY] * co if hosting else []),
            scratch_shapes=list(scratch_shapes) + (plan.sems if hosting else []))
        res = pl.pallas_call(hosted if hosting else body, name=name, grid_spec=spec,
                             out_shape=out_shape + (plan.out_shape if hosting else []),
                             input_output_aliases=aliases, compiler_params=_cparams(sem),
                             )(*args, *(plan.inputs if hosting else []))
        return (res[0] if single else res[:n_out]), list(res[n_out:])

    return run


def norm_matmul(x, g, w, name, plan=None, tn=None):
    L, D = x.shape
    tl = min(TL_DW, L)
    if w.ndim == 3:
        nt, _, tn = w.shape
        w_spec = pl.BlockSpec((1, D, tn), lambda i, n: (n, 0, 0))
    else:
        nt = w.shape[1] // tn
        w_spec = pl.BlockSpec((D, tn), lambda i, n: (0, n))

    def body(x_ref, g_ref, w_ref, o_ref, h_ref):
        xv = x_ref[...]
        h = (xv * _rms(xv) * g_ref[...]).astype(h_ref.dtype)
        h_ref[...] = h
        o_ref[...] = _mm(h, w_ref[0] if w.ndim == 3 else w_ref[...])

    return _call(
        body, plan, name=name, grid=(L // tl, nt),
        in_specs=[pl.BlockSpec((tl, D), lambda i, n: (i, 0)), _full((1, D)), w_spec],
        out_specs=[pl.BlockSpec((tl, tn), lambda i, n: (i, n)), pl.BlockSpec((tl, D), lambda i, n: (i, 0))],
        out_shape=[jax.ShapeDtypeStruct((L, nt * tn), F32), jax.ShapeDtypeStruct((L, D), MXU_DTYPE)],
        sem=("arbitrary", "arbitrary"),
    )(x, g, w)


def even_in_slabs(x, xs, g, w, slabs, wsel, name, p_in=None, plan=None):
    L, D = x.shape
    tl = min(TL_DW, L)
    wb = EVEN_IN // N_CHIPS
    n = slabs.shape[0]
    s5_cols = 2 * S5_WIDTH - wb
    first = p_in is None

    def body(slabs_ref, wsel_ref, xs_ref, x_ref, g_ref, w_ref, *rest):
        o_ref = rest[-3] if first else rest[-1]
        j = slabs_ref[pl.program_id(0)]
        hs = (xs_ref[...] * _rms(xs_ref[...]) * g_ref[...]).astype(MXU_DTYPE)
        h = (x_ref[...] * _rms(x_ref[...]) * g_ref[...]).astype(MXU_DTYPE)
        if first:
            rest[-2][...] = hs
            rest[-1][...] = h
        o_ref[:, :s5_cols] = _mm(jnp.where(j <= 1, hs, h), w_ref[0, :, :s5_cols])
        o_ref[:, s5_cols:] = _mm(jnp.where(j == 0, hs, h), w_ref[0, :, s5_cols:])

    row = pl.BlockSpec((tl, D), lambda s, i, slabs_ref, wsel_ref: (i, 0))
    in_specs = [row if first else
                pl.BlockSpec((tl, D), lambda s, i, slabs_ref, wsel_ref: (jnp.where(slabs_ref[s] <= 1, i, 0), 0)),
                row if first else
                pl.BlockSpec((tl, D), lambda s, i, slabs_ref, wsel_ref: (jnp.where(slabs_ref[s] >= 1, i, 0), 0)),
                pl.BlockSpec((1, D), lambda s, i, slabs_ref, wsel_ref: (0, 0)),
                pl.BlockSpec((1, D, wb), lambda s, i, slabs_ref, wsel_ref: (wsel_ref[s], 0, 0))]
    out_specs = [pl.BlockSpec((tl, wb), lambda s, i, slabs_ref, wsel_ref: (i, slabs_ref[s]))]
    out_shape = [jax.ShapeDtypeStruct((L, EVEN_IN), F32)]
    args = [slabs, wsel, xs, x, g, w]
    if first:
        out_specs += [row, row]
        out_shape += [jax.ShapeDtypeStruct((L, D), MXU_DTYPE)] * 2
    else:
        in_specs.append(ANY)
        args.append(p_in)
    return _call(body, plan, name=name, grid=(n, L // tl), in_specs=in_specs, out_specs=out_specs,
                 out_shape=out_shape, sem=("arbitrary", "arbitrary"), n_prefetch=2,
                 aliases={} if first else {6: 0})(*args)


def matmul_residual(ys, w, x, name):
    L, D = x.shape
    tl = min(TL_PROJ, L)
    n = len(ys)
    offs = np.cumsum([0] + [y.shape[1] for y in ys])

    def body(*refs):
        y_refs, w_ref, x_ref, o_ref = refs[:n], refs[n], refs[n + 1], refs[n + 2]
        acc = x_ref[...]
        for k in range(n):
            acc = acc + _mm(y_refs[k][...], w_ref[offs[k]:offs[k + 1], :])
        o_ref[...] = acc

    return pl.pallas_call(
        body, name=name, grid=(L // tl,),
        in_specs=[pl.BlockSpec((tl, y.shape[1]), lambda i: (i, 0)) for y in ys]
        + [_full(w.shape), pl.BlockSpec((tl, D), lambda i: (i, 0))],
        out_specs=pl.BlockSpec((tl, D), lambda i: (i, 0)),
        out_shape=jax.ShapeDtypeStruct((L, D), F32),
        compiler_params=_cparams(("arbitrary",)),
    )(*ys, w, x)


def out_proj_loss(y, w, x, gf, tgt, name):
    L, K = y.shape
    D = w.shape[1]
    tl = min(TL_PROJ, L)

    def body(y_ref, w_ref, x_ref, gf_ref, t_ref, dx_ref, loss_ref, dg_ref):
        @pl.when(pl.program_id(0) == 0)
        def _():
            loss_ref[...] = jnp.zeros_like(loss_ref)
            dg_ref[...] = jnp.zeros_like(dg_ref)

        x2 = x_ref[...] + _mm(y_ref[...], w_ref[...])
        r = _rms(x2)
        xn = x2 * r
        e = xn * gf_ref[...] - t_ref[...]
        loss_ref[...] += (0.5 / D) * jnp.sum(e * e)
        dout = e * (1.0 / D)
        dg_ref[...] += jnp.sum(dout * xn, axis=0, keepdims=True)
        dxn = dout * gf_ref[...]
        dx_ref[...] = r * (dxn - xn * jnp.mean(dxn * xn, axis=-1, keepdims=True))

    return pl.pallas_call(
        body, name=name, grid=(L // tl,),
        in_specs=[pl.BlockSpec((tl, K), lambda i: (i, 0)), _full((K, D)),
                  pl.BlockSpec((tl, D), lambda i: (i, 0)), _full((1, D)),
                  pl.BlockSpec((tl, D), lambda i: (i, 0))],
        out_specs=[pl.BlockSpec((tl, D), lambda i: (i, 0)), _full((8, 128)), _full((1, D))],
        out_shape=[jax.ShapeDtypeStruct((L, D), F32), jax.ShapeDtypeStruct((8, 128), F32),
                   jax.ShapeDtypeStruct((1, D), F32)],
        compiler_params=_cparams(("arbitrary",)),
    )(y, w, x, gf, tgt)


def out_proj_bwd(dx, w, ys, name):
    L, D = dx.shape
    K = w.shape[0]
    tl = min(TL_PROJ, L)
    n = len(ys)
    offs = np.cumsum([0] + [y.shape[1] for y in ys])

    def body(*refs):
        dx_ref, w_ref, y_refs = refs[0], refs[1], refs[2:2 + n]
        dy_refs, dw_ref = refs[2 + n:2 + 2 * n], refs[2 + 2 * n]

        @pl.when(pl.program_id(0) == 0)
        def _():
            dw_ref[...] = jnp.zeros_like(dw_ref)

        dxv = dx_ref[...]
        for k in range(n):
            dy_refs[k][...] = _mm_nt(dxv, w_ref[offs[k]:offs[k + 1], :])
            dw_ref[offs[k]:offs[k + 1], :] += _mm_tn(y_refs[k][...], dxv)

    y_specs = [pl.BlockSpec((tl, y.shape[1]), lambda i: (i, 0)) for y in ys]
    return pl.pallas_call(
        body, name=name, grid=(L // tl,),
        in_specs=[pl.BlockSpec((tl, D), lambda i: (i, 0)), _full((K, D))] + y_specs,
        out_specs=y_specs + [_full((K, D))],
        out_shape=[jax.ShapeDtypeStruct(y.shape, F32) for y in ys] + [jax.ShapeDtypeStruct((K, D), F32)],
        compiler_params=_cparams(("arbitrary",)),
    )(dx, w, *ys)


def in_proj_bwd_dx(x, g, dps, ws, dres, name, plan=None):
    L, D = x.shape
    tl = min(TL_PROJ, L)
    n = len(dps)

    def body(*refs):
        x_ref, g_ref, dres_ref = refs[:3]
        dp_refs, w_refs = refs[3:3 + n], refs[3 + n:3 + 2 * n]
        dx_ref, dg_ref = refs[3 + 2 * n:]

        @pl.when(pl.program_id(0) == 0)
        def _():
            dg_ref[...] = jnp.zeros_like(dg_ref)

        dh = None
        for dp_ref, w_ref, w in zip(dp_refs, w_refs, ws):
            if w.ndim == 3:
                tn = w.shape[2]
                parts = [_mm_nt(dp_ref[:, tn * k:tn * (k + 1)], w_ref[k]) for k in range(w.shape[0])]
            else:
                parts = [_mm_nt(dp_ref[...], w_ref[...])]
            for part in parts:
                dh = part if dh is None else dh + part
        xv = x_ref[...]
        r = _rms(xv)
        xn = xv * r
        dg_ref[...] += jnp.sum(dh * xn, axis=0, keepdims=True)
        dxn = dh * g_ref[...]
        dx_ref[...] = dres_ref[...] + r * (dxn - xn * jnp.mean(dxn * xn, axis=-1, keepdims=True))

    return _call(
        body, plan, name=name, grid=(L // tl,),
        in_specs=[pl.BlockSpec((tl, D), lambda i: (i, 0)), _full((1, D)), pl.BlockSpec((tl, D), lambda i: (i, 0))]
        + [pl.BlockSpec((tl, dp.shape[1]), lambda i: (i, 0)) for dp in dps] + [_full(w.shape) for w in ws],
        out_specs=[pl.BlockSpec((tl, D), lambda i: (i, 0)), _full((1, D))],
        out_shape=[jax.ShapeDtypeStruct((L, D), F32), jax.ShapeDtypeStruct((1, D), F32)],
        sem=("arbitrary",),
    )(x, g, dres, *dps, *ws)


def in_proj_bwd_dw(h, dp, name, tn, first=0, into=None, dtype=F32, plan=None, dp_first=0, count=None):
    L, D = h.shape
    tl = min(TL_DW, L)
    wb = EVEN_IN // N_CHIPS
    per = wb // tn
    count = dp.shape[1] // tn if count is None else count
    last = L // tl - 1

    def body(*refs):
        h_ref, dp_ref, dw_ref, acc = refs[0], refs[1], refs[-2], refs[-1]

        @pl.when(pl.program_id(1) == 0)
        def _():
            acc[...] = jnp.zeros_like(acc)

        acc[...] += _mm_tn(h_ref[...], dp_ref[...])

        @pl.when(pl.program_id(1) == last)
        def _():
            dw_ref[0] = acc[...].astype(dw_ref.dtype)

    ins = [h, dp] + ([] if into is None else [into])
    return _call(
        body, plan, name=name, grid=(count, L // tl),
        in_specs=[pl.BlockSpec((tl, D), lambda n, i: (i, 0)), pl.BlockSpec((tl, tn), lambda n, i: (i, n + dp_first))]
        + ([] if into is None else [ANY]),
        out_specs=pl.BlockSpec((1, D, tn), lambda n, i: ((n + first) // per, 0, (n + first) % per)),
        out_shape=jax.ShapeDtypeStruct((N_CHIPS, D, wb), dtype),
        scratch_shapes=[pltpu.VMEM((D, tn), F32)],
        aliases={} if into is None else {2: 0},
        sem=("arbitrary", "arbitrary"),
    )(*ins)


def _s5_param_fn(lam_re, lam_im, log_dt, b_re, b_im):
    lr = jnp.minimum(lam_re, -1e-4)
    li = lam_im
    dt = jnp.exp(log_dt)
    mag = jnp.exp(lr * dt)
    ab_re = mag * jnp.cos(li * dt)
    ab_im = mag * jnp.sin(li * dt)
    den = lr * lr + li * li
    n_re = ab_re - 1.0
    n_im = ab_im
    z_re = (n_re * lr + n_im * li) / den
    z_im = (n_im * lr - n_re * li) / den
    bb_re = z_re[None] * b_re - z_im[None] * b_im
    bb_im = z_re[None] * b_im + z_im[None] * b_re
    return ab_re, ab_im, bb_re, bb_im


def s5_params_fwd(lam_re, lam_im, log_dt, b_re, b_im, span):
    G, P = lam_re.shape
    H = b_re.shape[0]
    assert span & (span - 1) == 0

    def body(lr_ref, li_ref, dt_ref, br_ref, bi_ref, abr_ref, abi_ref, bbr_ref, bbi_ref, pr_ref, pi_ref):
        ab_re, ab_im, bb_re, bb_im = _s5_param_fn(lr_ref[...], li_ref[...], dt_ref[...], br_ref[...], bi_ref[...])
        abr_ref[...] = ab_re
        abi_ref[...] = ab_im
        bbr_ref[...] = bb_re
        bbi_ref[...] = bb_im
        cr, ci = ab_re, ab_im
        for _ in range(span.bit_length() - 1):
            cr, ci = cr * cr - ci * ci, 2.0 * cr * ci
        pr_ref[...] = cr
        pi_ref[...] = ci

    shp = lambda *s: jax.ShapeDtypeStruct(s, F32)
    return pl.pallas_call(
        body, name="s5_params_fwd",
        out_shape=[shp(G, P), shp(G, P), shp(H, G, P), shp(H, G, P), shp(G, P), shp(G, P)],
    )(lam_re, lam_im, log_dt, b_re, b_im)


def s5_params_bwd(lam_re, lam_im, log_dt, b_re, b_im, d_ab_re, d_ab_im, d_bb_re, d_bb_im):
    G, P = lam_re.shape
    H = b_re.shape[0]

    def body(lr_ref, li_ref, dt_ref, br_ref, bi_ref, g0, g1, g2, g3, o0, o1, o2, o3, o4):
        prim = (lr_ref[...], li_ref[...], dt_ref[...], br_ref[...], bi_ref[...])
        _, vjp = jax.vjp(_s5_param_fn, *prim)
        d = vjp((jnp.sum(g0[...], axis=0), jnp.sum(g1[...], axis=0), g2[...], g3[...]))
        o0[...], o1[...], o2[...], o3[...], o4[...] = d

    shp = lambda *s: jax.ShapeDtypeStruct(s, F32)
    return pl.pallas_call(
        body, name="s5_params_bwd",
        out_shape=[shp(G, P), shp(G, P), shp(G, 1), shp(H, G, P), shp(H, G, P)],
    )(lam_re, lam_im, log_dt, b_re, b_im, d_ab_re, d_ab_im, d_bb_re, d_bb_im)


def stream_order(a, tl):
    L, C = a.shape
    return a.reshape(L // tl, 8, tl // 8, C).transpose(0, 2, 1, 3).reshape(L, C)


def token_order(a, tl):
    L, C = a.shape
    return a.reshape(L // tl, tl // 8, 8, C).transpose(0, 2, 1, 3).reshape(L, C)


_LANE_BLK = 1024
_LANE_BLK_BWD = 1024


def _cmul_add(ar, ai, xr, xi, br, bi):
    return br + (ar * xr - ai * xi), bi + (ar * xi + ai * xr)


def _cmulc_add(ar, ai, xr, xi, br, bi):
    return br + (ar * xr + ai * xi), bi + (ar * xi - ai * xr)


def _s5_states(u, wbd_ref, a_re, a_im, at_re, at_im, s_re, s_im, e_re, e_im, c0_re, c0_im, tl):
    t8 = tl // 8
    for k in range(S5_KBLK):
        bu = _mm(u[:, 128 * k:128 * (k + 1)], wbd_ref[k])
        s_re[:, 512 * k:512 * (k + 1)] = bu[:, :512]
        s_im[:, 512 * k:512 * (k + 1)] = bu[:, 512:]
    outs_re, outs_im = [], []
    for b in range(S5_LANES // _LANE_BLK):
        lanes = slice(_LANE_BLK * b, _LANE_BLK * (b + 1))
        ar = jnp.broadcast_to(a_re[:, lanes], (8, _LANE_BLK))
        ai = jnp.broadcast_to(a_im[:, lanes], (8, _LANE_BLK))

        def local(i, carry, lanes=lanes, ar=ar, ai=ai):
            r = pl.multiple_of(i * 8, 8)
            sr, si = _cmul_add(ar, ai, carry[0], carry[1], s_re[pl.ds(r, 8), lanes], s_im[pl.ds(r, 8), lanes])
            s_re[pl.ds(r, 8), lanes] = sr
            s_im[pl.ds(r, 8), lanes] = si
            return sr, si

        zero = jnp.zeros((8, _LANE_BLK), F32)
        fr, fi = lax.fori_loop(0, t8, local, (zero, zero), unroll=True)
        tr, ti = at_re[:, lanes], at_im[:, lanes]
        er, ei = c0_re[:, lanes], c0_im[:, lanes]
        ers, eis = [er], [ei]
        for j in range(8):
            er, ei = _cmul_add(tr, ti, er, ei, fr[j:j + 1], fi[j:j + 1])
            ers.append(er)
            eis.append(ei)
        outs_re.append(ers[8])
        outs_im.append(eis[8])
        ent_r, ent_i = jnp.concatenate(ers[:8], axis=0), jnp.concatenate(eis[:8], axis=0)
        e_re[:, lanes] = ent_r
        e_im[:, lanes] = ent_i

        def fix(i, carry, lanes=lanes, ar=ar, ai=ai):
            r = pl.multiple_of(i * 8, 8)
            zr, zi = ar * carry[0] - ai * carry[1], ar * carry[1] + ai * carry[0]
            s_re[pl.ds(r, 8), lanes] = s_re[pl.ds(r, 8), lanes] + zr
            s_im[pl.ds(r, 8), lanes] = s_im[pl.ds(r, 8), lanes] + zi
            return zr, zi

        lax.fori_loop(0, t8, fix, (ent_r, ent_i), unroll=True)
    return jnp.concatenate(outs_re, axis=1), jnp.concatenate(outs_im, axis=1)


def _s5_readout(s_re, s_im, cre_ref, cim_ref):
    ys = []
    for k in range(S5_KBLK):
        lanes = slice(512 * k, 512 * (k + 1))
        ys.append(_mm(s_re[:, lanes], cre_ref[k]) - _mm(s_im[:, lanes], cim_ref[k]))
    return jnp.concatenate(ys, axis=1)


def s5_forward(p, wbd, cre, cim, atab, d_skip, w_glu, b_glu, plan=None):
    L = p.shape[0]
    tl = min(TL_S5, L)
    nch = L // tl

    def body(u_ref, z_ref, wbd_ref, cre_ref, cim_ref, at_ref, d_ref, wg_ref, bg_ref,
             ya_ref, st_re_ref, st_im_ref, sv_re_ref, sv_im_ref, s_re, s_im, e_re, e_im, car_re, car_im):
        @pl.when(pl.program_id(0) == 0)
        def _():
            car_re[...] = jnp.zeros_like(car_re)
            car_im[...] = jnp.zeros_like(car_im)

        c0_re, c0_im = car_re[...], car_im[...]
        st_re_ref[0] = c0_re
        st_im_ref[0] = c0_im
        u = u_ref[...]
        x_re, x_im = _s5_states(u, wbd_ref, at_ref[0:1], at_ref[1:2], at_ref[2:3], at_ref[3:4],
                                s_re, s_im, e_re, e_im, c0_re, c0_im, tl)
        car_re[...] = x_re
        car_im[...] = x_im
        sv_re_ref[...] = s_re[...].astype(sv_re_ref.dtype)
        sv_im_ref[...] = s_im[...].astype(sv_im_ref.dtype)
        y = _s5_readout(sv_re_ref, sv_im_ref, cre_ref, cim_ref) + d_ref[...] * u
        yg = _gelu(y)
        gate = _sigmoid(_mm(yg, wg_ref[...]) + bg_ref[...])
        sz, _ = _silu_and_grad(z_ref[...])
        ya_ref[...] = (yg * gate * sz).astype(ya_ref.dtype)

    return _call(
        body, plan, name="s5_forward", grid=(nch,),
        in_specs=[pl.BlockSpec((tl, 1024), lambda i: (i, 0)), pl.BlockSpec((tl, 1024), lambda i: (i, 1)),
                  _full(wbd.shape), _full(cre.shape), _full(cim.shape), _full(atab.shape),
                  _full((1, 1024)), _full((1024, 1024)), _full((1, 1024))],
        out_specs=[pl.BlockSpec((tl, 1024), lambda i: (i, 0)),
                   pl.BlockSpec((1, 1, S5_LANES), lambda i: (i, 0, 0)),
                   pl.BlockSpec((1, 1, S5_LANES), lambda i: (i, 0, 0)),
                   pl.BlockSpec((tl, S5_LANES), lambda i: (i, 0)), pl.BlockSpec((tl, S5_LANES), lambda i: (i, 0))],
        out_shape=[jax.ShapeDtypeStruct((L, 1024), MXU_DTYPE),
                   jax.ShapeDtypeStruct((nch, 1, S5_LANES), F32), jax.ShapeDtypeStruct((nch, 1, S5_LANES), F32),
                   jax.ShapeDtypeStruct((L, S5_LANES), MXU_DTYPE), jax.ShapeDtypeStruct((L, S5_LANES), MXU_DTYPE)],
        scratch_shapes=[pltpu.VMEM((tl, S5_LANES), F32), pltpu.VMEM((tl, S5_LANES), F32),
                        pltpu.VMEM((8, S5_LANES), F32), pltpu.VMEM((8, S5_LANES), F32),
                        pltpu.VMEM((1, S5_LANES), F32), pltpu.VMEM((1, S5_LANES), F32)],
        sem=("arbitrary",),
    )(p, p, wbd, cre, cim, atab, d_skip, w_glu, b_glu)


def s5_backward(p, dya, st_re, st_im, sv_re, sv_im, wbd, cre, cim, atab, d_skip, w_glu, b_glu, plan=None):
    L = p.shape[0]
    tl = min(TL_S5, L)
    t8 = tl // 8
    nch = L // tl
    rev = lambda i: (nch - 1 - i, 0)
    rev1 = lambda i: (nch - 1 - i, 1)
    rev3 = lambda i: (nch - 1 - i, 0, 0)
    ct_shape = (S5_KBLK, cre.shape[2], cre.shape[1])

    def body(u_ref, z_ref, dya_ref, str_ref, sti_ref, s_re, s_im, wbd_ref, cre_ref, cim_ref, at_ref,
             d_ref, wg_ref, bg_ref,
             dp_ref, dwbd_ref, dcre_ref, dcim_ref, dabr_ref, dabi_ref, dd_ref, dwg_ref, dbg_ref,
             g_re, g_im, car_re, car_im):
        @pl.when(pl.program_id(0) == 0)
        def _():
            car_re[...] = jnp.zeros_like(car_re)
            car_im[...] = jnp.zeros_like(car_im)
            for r in (dwbd_ref, dcre_ref, dcim_ref, dabr_ref, dabi_ref, dd_ref, dwg_ref, dbg_ref):
                r[...] = jnp.zeros_like(r)

        u = u_ref[...]
        a_re, a_im, at_re, at_im = at_ref[0:1], at_ref[1:2], at_ref[2:3], at_ref[3:4]
        y = _s5_readout(s_re, s_im, cre_ref, cim_ref) + d_ref[...] * u
        yg, dyg = _gelu_and_grad(y)
        gate = _sigmoid(_mm(yg, wg_ref[...]) + bg_ref[...])
        sz, dsz = _silu_and_grad(z_ref[...])
        dya = dya_ref[...]
        s5out = yg * gate
        dp_ref[:, 1024:] = (dya * s5out * dsz).astype(dp_ref.dtype)
        ds5 = dya * sz
        dt = ds5 * yg * gate * (1.0 - gate)
        dwg_ref[...] += _mm_tn(yg, dt)
        dbg_ref[...] += jnp.sum(dt, axis=0, keepdims=True)
        dyv = (ds5 * gate + _mm_nt(dt, wg_ref[...])) * dyg
        dd_ref[...] += jnp.sum(dyv * u, axis=0, keepdims=True)

        for k in range(S5_KBLK):
            lanes = slice(512 * k, 512 * (k + 1))
            dyk = dyv[:, 128 * k:128 * (k + 1)]
            g_re[:, lanes] = _mm_nt(dyk, cre_ref[k])
            g_im[:, lanes] = -_mm_nt(dyk, cim_ref[k])
            dcre_ref[k] += _mm_tn(dyk, s_re[:, lanes])
            dcim_ref[k] -= _mm_tn(dyk, s_im[:, lanes])

        blk = _LANE_BLK_BWD
        for b in range(S5_LANES // blk):
            lanes = slice(blk * b, blk * (b + 1))
            ar = jnp.broadcast_to(a_re[:, lanes], (8, blk))
            ai = jnp.broadcast_to(a_im[:, lanes], (8, blk))

            def local(j, carry, lanes=lanes, ar=ar, ai=ai):
                r = pl.multiple_of((t8 - 1 - j) * 8, 8)
                gr, gi = _cmulc_add(ar, ai, carry[0], carry[1], g_re[pl.ds(r, 8), lanes], g_im[pl.ds(r, 8), lanes])
                g_re[pl.ds(r, 8), lanes] = gr
                g_im[pl.ds(r, 8), lanes] = gi
                return gr, gi

            zero = jnp.zeros((8, blk), F32)
            fr, fi = lax.fori_loop(0, t8, local, (zero, zero), unroll=True)
            tr, ti = at_re[:, lanes], at_im[:, lanes]
            hr, hi = car_re[:, lanes], car_im[:, lanes]
            hrs, his = [hr], [hi]
            for j in range(7, -1, -1):
                hr, hi = _cmulc_add(tr, ti, hr, hi, fr[j:j + 1], fi[j:j + 1])
                hrs.append(hr)
                his.append(hi)
            car_re[:, lanes] = hrs[8]
            car_im[:, lanes] = his[8]
            in_r = jnp.concatenate(hrs[7::-1], axis=0)
            in_i = jnp.concatenate(his[7::-1], axis=0)

            wr, wi, nr, ni, accr, acci = in_r, in_i, zero, zero, zero, zero
            for pair in range(t8 // 2 - 1, -1, -1):
                rows = slice(16 * pair, 16 * pair + 16)
                s16r, s16i = s_re[rows, lanes].astype(F32), s_im[rows, lanes].astype(F32)
                for half in (1, 0):
                    r = 16 * pair + 8 * half
                    sr, si = s16r[8 * half:8 * half + 8], s16i[8 * half:8 * half + 8]
                    accr, acci = accr + (sr * nr + si * ni), acci + (sr * ni - si * nr)
                    wr, wi = ar * wr + ai * wi, ar * wi - ai * wr
                    nr, ni = g_re[r:r + 8, lanes] + wr, g_im[r:r + 8, lanes] + wi
                    g_re[r:r + 8, lanes] = nr
                    g_im[r:r + 8, lanes] = ni
            lr, li = s_re[tl - 16:tl, lanes].astype(F32)[8:], s_im[tl - 16:tl, lanes].astype(F32)[8:]
            row0 = lax.broadcasted_iota(jnp.int32, (8, blk), 0) == 0
            sr = jnp.where(row0, jnp.broadcast_to(str_ref[0][:, lanes], (8, blk)), pltpu.roll(lr, 1, 0))
            si = jnp.where(row0, jnp.broadcast_to(sti_ref[0][:, lanes], (8, blk)), pltpu.roll(li, 1, 0))
            dabr_ref[:, lanes] += accr + (sr * nr + si * ni)
            dabi_ref[:, lanes] += acci + (sr * ni - si * nr)

        dus = []
        for k in range(S5_KBLK):
            lanes = slice(512 * k, 512 * (k + 1))
            g = jnp.concatenate([g_re[:, lanes], g_im[:, lanes]], axis=1)
            dwbd_ref[k] += _mm_tn(u[:, 128 * k:128 * (k + 1)], g)
            dus.append(_mm_nt(g, wbd_ref[k]))
        du = jnp.concatenate(dus, axis=1) + dyv * d_ref[...]
        dp_ref[:, :1024] = du.astype(dp_ref.dtype)

    shp = lambda *s: jax.ShapeDtypeStruct(s, F32)
    return _call(
        body, plan, name="s5_backward", grid=(nch,),
        in_specs=[pl.BlockSpec((tl, 1024), rev), pl.BlockSpec((tl, 1024), rev1), pl.BlockSpec((tl, 1024), rev),
                  pl.BlockSpec((1, 1, S5_LANES), rev3), pl.BlockSpec((1, 1, S5_LANES), rev3),
                  pl.BlockSpec((tl, S5_LANES), rev), pl.BlockSpec((tl, S5_LANES), rev),
                  _full(wbd.shape), _full(cre.shape), _full(cim.shape), _full(atab.shape),
                  _full((1, 1024)), _full((1024, 1024)), _full((1, 1024))],
        out_specs=[pl.BlockSpec((tl, 2048), rev), _full(wbd.shape), _full(ct_shape), _full(ct_shape),
                   _full((8, S5_LANES)), _full((8, S5_LANES)), _full((1, 1024)), _full((1024, 1024)), _full((1, 1024))],
        out_shape=[jax.ShapeDtypeStruct((L, 2048), MXU_DTYPE), shp(*wbd.shape), shp(*ct_shape), shp(*ct_shape),
                   shp(8, S5_LANES), shp(8, S5_LANES), shp(1, 1024), shp(1024, 1024), shp(1, 1024)],
        scratch_shapes=[pltpu.VMEM((tl, S5_LANES), F32), pltpu.VMEM((tl, S5_LANES), F32),
                        pltpu.VMEM((1, S5_LANES), F32), pltpu.VMEM((1, S5_LANES), F32)],
        sem=("arbitrary",),
    )(p, p, dya, st_re, st_im, sv_re, sv_im, wbd, cre, cim, atab, d_skip, w_glu, b_glu)


def _block_diag(w, rows_first):
    g8 = w.reshape(S5_KBLK, 8, w.shape[1], w.shape[2])
    eye = jnp.eye(8, dtype=w.dtype)
    out = jnp.einsum('kgab,fg->kfagb', g8, eye)
    return out.reshape(S5_KBLK, 8 * w.shape[1], 8 * w.shape[2])


def _block_diag_extract(wbd, a, b):
    w5 = wbd.reshape(S5_KBLK, 8, a, 8, b)
    idx = jnp.arange(8)
    return w5[:, idx, :, idx, :].transpose(1, 0, 2, 3).reshape(S5_GROUPS, a, b)


def _ret_constants():
    log_g = np.log1p(-np.exp2(-5.0 - np.arange(RET_HEADS, dtype=np.float32))).astype(np.float32)
    idx = np.arange(RET_CHUNK, dtype=np.float32)
    diff = idx[:, None] - idx[None, :]
    decay = np.where(diff >= 0, np.exp(log_g[:, None, None] * np.maximum(diff, 0.0)), 0.0).astype(np.float32)
    xi = np.exp(log_g[None, :] * (idx[:, None] + 1.0)).astype(np.float32)
    zeta = np.exp(log_g[None, :] * (RET_CHUNK - 1.0 - idx[:, None])).astype(np.float32)
    chunk_decay = np.exp(log_g * RET_CHUNK).astype(np.float32)
    return decay, xi, zeta, chunk_decay


def _rope_tables(L):
    half = RET_DK // 2
    inv = ROPE_BASE ** (-jnp.arange(half, dtype=F32) / half)
    ang = jnp.arange(L, dtype=F32)[:, None] * inv[None, :]
    return jnp.cos(ang), jnp.sin(ang)


def _rot(xh, cos, sin):
    x1, x2 = xh[:, :128], xh[:, 128:]
    return jnp.concatenate([x1 * cos - x2 * sin, x1 * sin + x2 * cos], axis=1)


def _rot_t(dh, cos, sin):
    d1, d2 = dh[:, :128], dh[:, 128:]
    return jnp.concatenate([d1 * cos + d2 * sin, d2 * cos - d1 * sin], axis=1)


RET_PER_STEP = 4


def _ret_setup(L):
    nc = L // RET_CHUNK
    per = RET_PER_STEP if nc % RET_PER_STEP == 0 else 1
    decay_np, xi_np, zeta_np, cd_np = _ret_constants()
    tables = (jnp.asarray(decay_np), jnp.asarray(np.tile(xi_np, (per, 1))), jnp.asarray(np.tile(zeta_np, (per, 1))))
    return nc // per, per, tables, [float(c) for c in cd_np]


def _ret_rows(q_ref, k_ref, v_ref, cos_ref, sin_ref, xi_ref, zeta_ref):
    H = range(RET_HEADS)
    hs = [slice(RET_DK * h, RET_DK * (h + 1)) for h in H]
    cs, sn = cos_ref[...], sin_ref[...]
    qh = [_rot(q_ref[:, hs[h]], cs, sn) for h in H]
    kh = [_rot(k_ref[:, hs[h]], cs, sn) * (RET_DK ** -0.5) for h in H]
    vh = [v_ref[:, hs[h]] for h in H]
    qx = [qh[h] * xi_ref[:, h:h + 1] for h in H]
    kz = [kh[h] * zeta_ref[:, h:h + 1] for h in H]
    return hs, cs, sn, qh, kh, vh, qx, kz


def _ret_normed(qh, kh, vh, qx, dec_ref, prevs, per):
    H, C = range(RET_HEADS), range(per)
    rs = [slice(RET_CHUNK * c, RET_CHUNK * (c + 1)) for c in C]
    sc = [[_mm_nt(qh[h][rs[c]], kh[h][rs[c]]) * dec_ref[h] for h in H] for c in C]
    inner = [[_mm(sc[c][h], vh[h][rs[c]]) for h in H] for c in C]
    cross = [[_mm(qx[h][rs[c]], prevs[c][h]) for h in H] for c in C]
    o = [jnp.concatenate([inner[c][h] + cross[c][h] for c in C], axis=0) for h in H]
    oc = [o[h] - jnp.mean(o[h], axis=-1, keepdims=True) for h in H]
    rstd = [lax.rsqrt(jnp.mean(oc[h] * oc[h], axis=-1, keepdims=True) + NORM_EPS) for h in H]
    on = [oc[h] * rstd[h] for h in H]
    return rs, sc, rstd, on


def retention_forward(p, cos, sin, gain):
    L = p.shape[0]
    steps, per, (decay, xi, zeta), cd = _ret_setup(L)
    rows = RET_CHUNK * per

    def body(q_ref, k_ref, v_ref, z_ref, cos_ref, sin_ref, dec_ref, xi_ref, zeta_ref, gain_ref,
             yb_ref, prev_ref, state):
        @pl.when(pl.program_id(0) == 0)
        def _():
            state[...] = jnp.zeros_like(state)

        H, C = range(RET_HEADS), range(per)
        hs, cs, sn, qh, kh, vh, qx, kz = _ret_rows(q_ref, k_ref, v_ref, cos_ref, sin_ref, xi_ref, zeta_ref)
        prevs = [[state[h] for h in H]]
        for c in C:
            rs_c = slice(RET_CHUNK * c, RET_CHUNK * (c + 1))
            prevs.append([prevs[c][h] * cd[h] + _mm_tn(kz[h][rs_c], vh[h][rs_c]) for h in H])
        _, _, _, on = _ret_normed(qh, kh, vh, qx, dec_ref, prevs, per)
        sz, _ = _silu_and_grad(z_ref[...])
        for h in H:
            for c in C:
                prev_ref[c, h] = prevs[c][h].astype(prev_ref.dtype)
            state[h] = prevs[per][h]
            yb_ref[:, hs[h]] = (on[h] * gain_ref[:, hs[h]] * sz[:, hs[h]]).astype(yb_ref.dtype)

    col0 = p.shape[1] // 1024 - 4
    blk = lambda c: pl.BlockSpec((rows, 1024), lambda i, c=c: (i, c + col0))
    return pl.pallas_call(
        body, name="retention_forward", grid=(steps,),
        in_specs=[blk(0), blk(1), blk(2), blk(3),
                  pl.BlockSpec((rows, 128), lambda i: (i, 0)), pl.BlockSpec((rows, 128), lambda i: (i, 0)),
                  _full(decay.shape), _full(xi.shape), _full(zeta.shape), _full((1, 1024))],
        out_specs=[pl.BlockSpec((rows, 1024), lambda i: (i, 0)),
                   pl.BlockSpec((per, RET_HEADS, RET_DK, RET_DK), lambda i: (i, 0, 0, 0))],
        out_shape=[jax.ShapeDtypeStruct((L, 1024), MXU_DTYPE),
                   jax.ShapeDtypeStruct((steps * per, RET_HEADS, RET_DK, RET_DK), MXU_DTYPE)],
        scratch_shapes=[pltpu.VMEM((RET_HEADS, RET_DK, RET_DK), F32)],
        compiler_params=_cparams(("arbitrary",)),
    )(p, p, p, p, cos, sin, decay, xi, zeta, gain)


def retention_backward(p, dy, prevs, cos, sin, gain, plan=None):
    L = p.shape[0]
    steps, per, (decay, xi, zeta), cd = _ret_setup(L)
    rows = RET_CHUNK * per
    scale = RET_DK ** -0.5

    def body(q_ref, k_ref, v_ref, z_ref, dyb_ref, prev_ref, cos_ref, sin_ref, dec_ref, xi_ref, zeta_ref, gain_ref,
             dp_ref, dgain_ref, dstate):
        @pl.when(pl.program_id(0) == 0)
        def _():
            dstate[...] = jnp.zeros_like(dstate)
            dgain_ref[...] = jnp.zeros_like(dgain_ref)

        H, C = range(RET_HEADS), range(per)
        hs, cs, sn, qh, kh, vh, qx, kz = _ret_rows(q_ref, k_ref, v_ref, cos_ref, sin_ref, xi_ref, zeta_ref)
        prevs = [[prev_ref[c, h] for h in H] for c in C]
        rs, sc, rstd, on = _ret_normed(qh, kh, vh, qx, dec_ref, prevs, per)
        sz, dsz = _silu_and_grad(z_ref[...])
        dyb = dyb_ref[...]
        dong = [dyb[:, hs[h]] * sz[:, hs[h]] for h in H]
        don = [dong[h] * gain_ref[:, hs[h]] for h in H]
        do = [rstd[h] * (don[h] - jnp.mean(don[h], axis=-1, keepdims=True)
                         - on[h] * jnp.mean(don[h] * on[h], axis=-1, keepdims=True)) for h in H]
        dsc = [[_mm_nt(do[h][rs[c]], vh[h][rs[c]]) * dec_ref[h] for h in H] for c in C]
        dq_st = [[_mm_nt(do[h][rs[c]], prevs[c][h]) for h in H] for c in C]
        dnew = [[_mm_tn(qx[h][rs[c]], do[h][rs[c]]) for h in H] for c in C]
        dsts = [None] * per + [[dstate[h] for h in H]]
        for c in reversed(C):
            dsts[c] = [dsts[c + 1][h] * cd[h] + dnew[c][h] for h in H]
        dk_st = [[_mm_nt(vh[h][rs[c]], dsts[c + 1][h]) for h in H] for c in C]
        dv_st = [[_mm(kz[h][rs[c]], dsts[c + 1][h]) for h in H] for c in C]
        rows_of = lambda parts: jnp.concatenate(parts, axis=0)
        dqh = [rows_of([_mm(dsc[c][h], kh[h][rs[c]]) for c in C])
               + rows_of([dq_st[c][h] for c in C]) * xi_ref[:, h:h + 1] for h in H]
        dkh = [rows_of([_mm_tn(dsc[c][h], qh[h][rs[c]]) for c in C])
               + rows_of([dk_st[c][h] for c in C]) * zeta_ref[:, h:h + 1] for h in H]
        dvh = [rows_of([_mm_tn(sc[c][h], do[h][rs[c]]) + dv_st[c][h] for c in C]) for h in H]
        for h in H:
            dstate[h] = dsts[0][h]
            dgain_ref[:, hs[h]] += jnp.sum(dong[h] * on[h], axis=0, keepdims=True)
            dp_ref[:, hs[h]] = (_rot_t(dkh[h], cs, sn) * scale).astype(dp_ref.dtype)
            dp_ref[:, 1024 + RET_DK * h:1024 + RET_DK * (h + 1)] = dvh[h].astype(dp_ref.dtype)
            dp_ref[:, 2048 + RET_DK * h:2048 + RET_DK * (h + 1)] = (
                dyb[:, hs[h]] * on[h] * gain_ref[:, hs[h]] * dsz[:, hs[h]]).astype(dp_ref.dtype)
            dp_ref[:, 3072 + RET_DK * h:3072 + RET_DK * (h + 1)] = _rot_t(dqh[h], cs, sn).astype(dp_ref.dtype)

    col0 = p.shape[1] // 1024 - 4
    blk = lambda c: pl.BlockSpec((rows, 1024), lambda i, c=c: (steps - 1 - i, c + col0))
    tab = pl.BlockSpec((rows, 128), lambda i: (steps - 1 - i, 0))
    return _call(
        body, plan, name="retention_backward", grid=(steps,),
        in_specs=[blk(0), blk(1), blk(2), blk(3), pl.BlockSpec((rows, 1024), lambda i: (steps - 1 - i, 0)),
                  pl.BlockSpec((per, RET_HEADS, RET_DK, RET_DK), lambda i: (steps - 1 - i, 0, 0, 0)),
                  tab, tab, _full(decay.shape), _full(xi.shape), _full(zeta.shape), _full((1, 1024))],
        out_specs=[pl.BlockSpec((rows, 4096), lambda i: (steps - 1 - i, 0)), _full((1, 1024))],
        out_shape=[jax.ShapeDtypeStruct((L, 4096), MXU_DTYPE), jax.ShapeDtypeStruct((1, 1024), F32)],
        scratch_shapes=[pltpu.VMEM((RET_HEADS, RET_DK, RET_DK), F32)],
        sem=("arbitrary",),
    )(p, p, p, p, dy, prevs, cos, sin, decay, xi, zeta, gain)


def _sgu_mix(p_ref, gain_ref, wm_ref, bt_ref, tl):
    pu, pv, z = p_ref[:, :2048], p_ref[:, 2048:4096], p_ref[:, 4096:]
    (u, du), (v, dv) = _gelu_and_grad(pu), _gelu_and_grad(pv)
    mu = jnp.mean(v, axis=-1, keepdims=True)
    vc = v - mu
    rstd = lax.rsqrt(jnp.mean(vc * vc, axis=-1, keepdims=True) + NORM_EPS)
    vn = vc * rstd
    vg = vn * gain_ref[...]
    mask = (lax.broadcasted_iota(jnp.int32, (SGU_CHUNK, SGU_CHUNK), 0)
            >= lax.broadcasted_iota(jnp.int32, (SGU_CHUNK, SGU_CHUNK), 1))
    wms = [jnp.where(mask, wm_ref[g], 0.0) for g in range(SGU_GROUPS)]
    rows = []
    for c in range(tl // SGU_CHUNK):
        rs = slice(SGU_CHUNK * c, SGU_CHUNK * (c + 1))
        cols = []
        for g in range(SGU_GROUPS):
            gs = slice(SGU_GDIM * g, SGU_GDIM * (g + 1))
            cols.append(_mm(wms[g], vg[rs, gs]) + bt_ref[:, g:g + 1])
        rows.append(jnp.concatenate(cols, axis=1))
    s = rows[0] if len(rows) == 1 else jnp.concatenate(rows, axis=0)
    return du, dv, z, u, vn, rstd, vg, wms, mask, s


def sgu_forward(p, gain, wm, bt):
    L = p.shape[0]
    tl = min(TL_SGU, L)

    def body(p_ref, gain_ref, wm_ref, bt_ref, y_ref):
        _, _, z, u, _, _, _, _, _, s = _sgu_mix(p_ref, gain_ref, wm_ref, bt_ref, tl)
        sz, _ = _silu_and_grad(z)
        y_ref[...] = (u * s * sz).astype(y_ref.dtype)

    return pl.pallas_call(
        body, name="sgu_forward", grid=(L // tl,),
        in_specs=[pl.BlockSpec((tl, ODD_IN), lambda i: (i, 0)), _full((1, 2048)), _full(wm.shape), _full(bt.shape)],
        out_specs=pl.BlockSpec((tl, 2048), lambda i: (i, 0)),
        out_shape=jax.ShapeDtypeStruct((L, 2048), MXU_DTYPE),
        compiler_params=_cparams(("arbitrary",)),
    )(p, gain, wm, bt)


def sgu_backward(p, dy, gain, wm, bt, plan=None):
    L = p.shape[0]
    tl = min(TL_SGU, L)

    def body(p_ref, dy_ref, gain_ref, wm_ref, bt_ref, dp_ref, dgain_ref, dwm_ref, dbt_ref):
        @pl.when(pl.program_id(0) == 0)
        def _():
            dgain_ref[...] = jnp.zeros_like(dgain_ref)
            dwm_ref[...] = jnp.zeros_like(dwm_ref)
            dbt_ref[...] = jnp.zeros_like(dbt_ref)

        gu, gv, z, u, vn, rstd, vg, wms, mask, s = _sgu_mix(p_ref, gain_ref, wm_ref, bt_ref, tl)
        sz, dsz = _silu_and_grad(z)
        dyv = dy_ref[...]
        dp_ref[:, 4096:] = (dyv * u * s * dsz).astype(dp_ref.dtype)
        dsg = dyv * sz
        dp_ref[:, :2048] = (dsg * s * gu).astype(dp_ref.dtype)
        ds = dsg * u
        rows = []
        dbs = [jnp.zeros((SGU_CHUNK, 1), F32) for _ in range(SGU_GROUPS)]
        for c in range(tl // SGU_CHUNK):
            rs = slice(SGU_CHUNK * c, SGU_CHUNK * (c + 1))
            cols = []
            for g in range(SGU_GROUPS):
                gs = slice(SGU_GDIM * g, SGU_GDIM * (g + 1))
                dsg_c = ds[rs, gs]
                dbs[g] = dbs[g] + jnp.sum(dsg_c, axis=1, keepdims=True)
                dwm_ref[g] += jnp.where(mask, _mm_nt(dsg_c, vg[rs, gs]), 0.0)
                cols.append(_mm_tn(wms[g], dsg_c))
            rows.append(jnp.concatenate(cols, axis=1))
        dbt_ref[...] += jnp.concatenate(dbs, axis=1)
        dvg = rows[0] if len(rows) == 1 else jnp.concatenate(rows, axis=0)
        dgain_ref[...] += jnp.sum(dvg * vn, axis=0, keepdims=True)
        dvn = dvg * gain_ref[...]
        dv = rstd * (dvn - jnp.mean(dvn, axis=-1, keepdims=True) - vn * jnp.mean(dvn * vn, axis=-1, keepdims=True))
        dp_ref[:, 2048:4096] = (dv * gv).astype(dp_ref.dtype)

    return _call(
        body, plan, name="sgu_backward", grid=(L // tl,),
        in_specs=[pl.BlockSpec((tl, ODD_IN), lambda i: (i, 0)), pl.BlockSpec((tl, 2048), lambda i: (i, 0)),
                  _full((1, 2048)), _full(wm.shape), _full(bt.shape)],
        out_specs=[pl.BlockSpec((tl, ODD_IN), lambda i: (i, 0)), _full((1, 2048)), _full(wm.shape), _full(bt.shape)],
        out_shape=[jax.ShapeDtypeStruct((L, ODD_IN), MXU_DTYPE), jax.ShapeDtypeStruct((1, 2048), F32),
                   jax.ShapeDtypeStruct(wm.shape, F32), jax.ShapeDtypeStruct(bt.shape, F32)],
        sem=("arbitrary",),
    )(p, dy, gain, wm, bt)


def cast_shards(mats):
    n = len(mats)
    steps = 8

    def body(*refs):
        for p in range(n):
            refs[n + p][...] = refs[p][...].astype(MXU_DTYPE)

    specs = [pl.BlockSpec((m.shape[0] // steps, m.shape[1]), lambda i: (i, 0)) for m in mats]
    return pl.pallas_call(
        body, name="cast_shards", grid=(steps,), in_specs=specs, out_specs=specs,
        out_shape=[jax.ShapeDtypeStruct(m.shape, MXU_DTYPE) for m in mats],
        compiler_params=_cparams(("arbitrary",)),
    )(*mats)


def local_grads(x, tgt, w):
    L = x.shape[0]
    ne, gf = w["norm_even"], w["final_norm"].reshape(1, D_MODEL)
    sh = dict(zip(MATRICES, cast_shards([w[n][0] for n in MATRICES])))
    lam_re, lam_im = w["s5_lam_re"][0], w["s5_lam_im"][0]
    log_dt = w["s5_log_dt"].reshape(S5_GROUPS, 1)
    bt_re = jnp.transpose(w["s5_b_re"][0], (2, 0, 1))
    bt_im = jnp.transpose(w["s5_b_im"][0], (2, 0, 1))
    c_re, c_im = w["s5_c_re"][0], w["s5_c_im"][0]
    wm = w["sgu_w_spatial"][0]
    bt = jnp.transpose(w["sgu_b_spatial"][0])

    tl5 = min(TL_S5, L)
    ab_re, ab_im, bb_re, bb_im, at_re, at_im = s5_params_fwd(lam_re, lam_im, log_dt, bt_re, bt_im, tl5 // 8)
    atab = jnp.stack([ab_re.reshape(S5_LANES), ab_im.reshape(S5_LANES),
                      at_re.reshape(S5_LANES), at_im.reshape(S5_LANES)])
    wbd = jnp.concatenate([_block_diag(jnp.transpose(bb_re, (1, 0, 2)), True),
                           _block_diag(jnp.transpose(bb_im, (1, 0, 2)), True)], axis=2).astype(MXU_DTYPE)
    cre = _block_diag(jnp.transpose(c_re, (0, 2, 1)), True).astype(MXU_DTYPE)
    cim = _block_diag(jnp.transpose(c_im, (0, 2, 1)), True).astype(MXU_DTYPE)
    cos, sin = _rope_tables(L)

    s5_cols = 2 * S5_WIDTH
    me = (2 * lax.axis_index("x") + lax.axis_index("y")).astype(jnp.int32)
    xs = stream_order(x, tl5)
    slab = lambda d: jnp.stack([me ^ d])
    zero = jnp.zeros((1,), jnp.int32)
    shards = [sh["w_in_even"][None]]
    (p1, h0s, h0), (got,) = even_in_slabs(x, xs, ne, shards[0], slab(0), zero, "even_in_0",
                                          plan=gather_plan([sh["w_in_even"]], only=1))
    for d in (1, 2, 3):
        shards.append(got)
        plan = gather_plan([sh["w_in_even"]], only=d + 1) if d < 3 else gather_plan([sh["s5_w_glu"]])
        (p1,), (got,) = even_in_slabs(x, xs, ne, shards[d], slab(d), zero, "even_in_%d" % d, p_in=p1, plan=plan)
    w_glu = got
    by_xor = jnp.concatenate(shards)
    w_in_e = [lax.dynamic_index_in_dim(by_xor, me ^ j, 0, keepdims=False) for j in range(N_CHIPS)]
    w_s5 = jnp.concatenate([w_in_e[0], w_in_e[1][:, :s5_cols - EVEN_IN // N_CHIPS]], axis=1)
    w_kvzq = jnp.concatenate([w_in_e[2], w_in_e[3], w_in_e[1][:, s5_cols - EVEN_IN // N_CHIPS:]], axis=1)
    w_glu = w_glu.reshape(S5_WIDTH, S5_WIDTH)
    (ya, st_re, st_im, sv_re, sv_im), (w_out_e, w_in_o, w_out_o, no, sg_gain) = s5_forward(
        p1, wbd, cre, cim, atab, w["s5_d"], w_glu, w["s5_b_glu"],
        gather_plan([sh["w_out_even"], sh["w_in_odd"], sh["w_out_odd"], w["norm_odd"], w["sgu_norm_gain"]]))
    w_out_e = w_out_e.reshape(2 * S5_WIDTH, D_MODEL)
    w_out_o = w_out_o.reshape(SGU_WIDTH, D_MODEL)
    no, sg_gain = no.reshape(1, D_MODEL), sg_gain.reshape(1, SGU_WIDTH)
    yb, prevs = retention_forward(p1, cos, sin, w["ret_gn_gain"])
    ya = token_order(ya, tl5)
    x1 = matmul_residual([ya, yb], w_out_e, x, "even_out")
    (p2, h1), _ = norm_matmul(x1, no, w_in_o, "odd_in")
    y2 = sgu_forward(p2, sg_gain, wm, bt)
    dx2, loss, dgf = out_proj_loss(y2, w_out_o, x1, gf, tgt, "odd_out_loss")

    g, landed = {}, {}
    shard_major = lambda a, n: a.reshape((N_CHIPS,) + w[n].shape[1:])
    dy2, g_w_out_o = out_proj_bwd(dx2, w_out_o, [y2], "odd_out_bwd")
    (dp2, g["sgu_norm_gain"], dwm, dbt), (landed["w_out_odd"],) = sgu_backward(
        p2, dy2, sg_gain, wm, bt, reduce_plan([shard_major(g_w_out_o, "w_out_odd")]))
    g_w_in_o, _ = in_proj_bwd_dw(h1, dp2, "odd_in_dw", ODD_IN // N_CHIPS)
    (dx1, g["norm_odd"]), _ = in_proj_bwd_dx(x1, no, [dp2], [w_in_o], dx2, "odd_in_dx")
    dya, dyb, g_w_out_e = out_proj_bwd(dx1, w_out_e, [ya, yb], "even_out_bwd")
    ((dpa, dwbd, dcre, dcim, dab_re, dab_im, g["s5_d"], g_w_glu, g["s5_b_glu"]),
     (landed["w_in_odd"], landed["w_out_even"])) = s5_backward(
        p1, stream_order(dya, tl5), st_re, st_im, sv_re, sv_im, wbd, cre, cim, atab, w["s5_d"], w_glu,
        w["s5_b_glu"], reduce_plan([g_w_in_o, shard_major(g_w_out_e, "w_out_even")]))

    dbb_re = jnp.transpose(_block_diag_extract(dwbd[:, :, :512], S5_GROUP, S5_STATE), (1, 0, 2))
    dbb_im = jnp.transpose(_block_diag_extract(dwbd[:, :, 512:], S5_GROUP, S5_STATE), (1, 0, 2))
    dlr, dli, ddt, dbt_re, dbt_im = s5_params_bwd(
        lam_re, lam_im, log_dt, bt_re, bt_im, dab_re.reshape(8, S5_GROUPS, S5_STATE),
        dab_im.reshape(8, S5_GROUPS, S5_STATE), dbb_re, dbb_im)
    g["s5_lam_re"], g["s5_lam_im"] = dlr[None], dli[None]
    g["s5_log_dt"] = ddt.reshape(1, S5_GROUPS)
    g["s5_b_re"], g["s5_b_im"] = dbt_re, dbt_im
    g["s5_c_re"] = _block_diag_extract(dcre, S5_GROUP, S5_STATE)[None]
    g["s5_c_im"] = _block_diag_extract(dcim, S5_GROUP, S5_STATE)[None]
    g["sgu_w_spatial"] = dwm[None]
    g["sgu_b_spatial"] = jnp.transpose(dbt)[None]
    g["final_norm"] = dgf.reshape(D_MODEL)
    g["loss"] = loss

    big_small = ("s5_b_re", "s5_b_im")
    mid_small = ("s5_c_re",)
    (dpb, g["ret_gn_gain"]), recv = retention_backward(
        p1, dyb, prevs, cos, sin, w["ret_gn_gain"],
        reduce_plan([shard_major(g_w_glu, "s5_w_glu")], [g[n] for n in big_small]))
    landed.update(zip(("s5_w_glu",) + big_small, recv))
    done = tuple(n for n in MATRICES if n != "w_in_even")
    part = {n: sum_slabs(landed[n], "sum_" + n) for n in done}
    g_w_in_e, recv = in_proj_bwd_dw(h0s, dpa, "even_in_dw_s5", 512, dtype=MXU_DTYPE,
                                    plan=_SiblingPlan([part[n] for n in done]))
    other = dict(zip(done, recv))
    small = tuple(n for n in SMALL if n != "norm_even" and n not in big_small + mid_small) + ("loss",)
    wb = EVEN_IN // N_CHIPS
    g_w_in_e, recv = in_proj_bwd_dw(h0, dpb, "even_in_dw_q", 512, first=s5_cols // 512, into=g_w_in_e,
                                    dtype=MXU_DTYPE, dp_first=2 * wb // 512, count=RET_HEADS * RET_DK // 512,
                                    plan=reduce_plan([], [g[n] for n in mid_small]))
    landed.update(zip(mid_small, recv))
    g_w_in_e, recv = in_proj_bwd_dw(h0, dpb, "even_in_dw_kvz", wb, first=2, into=g_w_in_e, dtype=MXU_DTYPE,
                                    count=2, plan=reduce_plan([], [g[n] for n in small]))
    landed.update(zip(small, recv))
    (dx0, g["norm_even"]), (landed["w_in_even"],) = in_proj_bwd_dx(
        x, ne, [token_order(dpa, tl5), dpb], [w_s5, w_kvzq], dx1, "even_in_dx", reduce_plan([g_w_in_e]))
    (landed["norm_even"],) = run_plan(reduce_plan([], [g["norm_even"]]), "exchange_norm_even")
    return dx0, landed, part, other


def _row_block(rows):
    return 128 if rows % 128 == 0 else rows


def sum_slabs(r, name):
    _, R, C = r.shape
    tr = _row_block(R)

    def body(r_ref, o_ref):
        a, b, c, d = (r_ref[k].astype(F32) for k in range(N_CHIPS))
        o_ref[...] = (a + b) + (c + d)

    return pl.pallas_call(
        body, name=name, grid=(R // tr,),
        in_specs=[pl.BlockSpec((N_CHIPS, tr, C), lambda i: (0, i, 0))],
        out_specs=pl.BlockSpec((tr, C), lambda i: (i, 0)),
        out_shape=jax.ShapeDtypeStruct((R, C), F32),
        compiler_params=_cparams(("arbitrary",)),
    )(r)


def _adam(w, m, v, g):
    mn = ADAM_B1 * m + (1.0 - ADAM_B1) * g
    vn = ADAM_B2 * v + (1.0 - ADAM_B2) * (g * g)
    m_hat = mn / (1.0 - ADAM_B1 ** ADAM_STEP)
    v_hat = vn / (1.0 - ADAM_B2 ** ADAM_STEP)
    return -ADAM_LR * (m_hat / (jnp.sqrt(v_hat) + ADAM_EPS) + ADAM_WD * w), mn, vn


def adam_update(w, m, v, ga, gb, name):
    R, C = w.shape
    tr = _row_block(R)

    def body(w_ref, m_ref, v_ref, ga_ref, gb_ref, g_out, d_out, m_out, v_out):
        g = ga_ref[...] + gb_ref[...]
        g_out[...] = g
        d_out[...], m_out[...], v_out[...] = _adam(w_ref[...], m_ref[...], v_ref[...], g)

    blk = pl.BlockSpec((tr, C), lambda i: (i, 0))
    return pl.pallas_call(
        body, name=name, grid=(R // tr,),
        in_specs=[blk] * 5, out_specs=[blk] * 4,
        out_shape=[jax.ShapeDtypeStruct((R, C), F32)] * 4,
        compiler_params=_cparams(("arbitrary",)),
    )(w, m, v, ga, gb)


WIDE_ROWS = ("s5_b_re", "s5_b_im")


def sum_small(landed):
    def body(*refs):
        k = len(refs) // 2
        for i in range(k):
            r = refs[i]
            refs[k + i][...] = (r[0] + r[1]) + (r[2] + r[3])

    names = list(landed)
    res = pl.pallas_call(
        body, name="sum_small", out_shape=[jax.ShapeDtypeStruct(landed[n].shape[1:], F32) for n in names],
        compiler_params=pltpu.CompilerParams(vmem_limit_bytes=VMEM_LIMIT),
    )(*[landed[n] for n in names])
    return dict(zip(names, res))


def adam_small(names, w, m, v, ga, gb):
    def body(*refs):
        k = len(refs) // 9
        me = 2 * lax.axis_index("x") + lax.axis_index("y")
        for i in range(k):
            w_ref, m_ref, v_ref, ga_ref, gb_ref = refs[i], refs[k + i], refs[2 * k + i], refs[3 * k + i], refs[4 * k + i]
            size = w_ref.shape[-1]
            if ga_ref.shape != w_ref.shape:
                part = pl.ds(pl.multiple_of(me * size, LANES), size)
                g = ga_ref[:, part] + gb_ref[:, part]
            else:
                g = ga_ref[...] + gb_ref[...]
            refs[5 * k + i][...] = g
            refs[6 * k + i][...], refs[7 * k + i][...], refs[8 * k + i][...] = _adam(w_ref[...], m_ref[...], v_ref[...], g)

    ins = [d[n] for d in (w, m, v, ga, gb) for n in names]
    outs = [jax.ShapeDtypeStruct(w[n].shape, F32) for _ in range(4) for n in names]
    res = pl.pallas_call(body, name="adam_small", out_shape=outs,
                         compiler_params=pltpu.CompilerParams(vmem_limit_bytes=VMEM_LIMIT))(*ins)
    k = len(names)
    return [dict(zip(names, res[j * k:(j + 1) * k])) for j in range(4)]


WEIGHTS = ("norm_even", "w_in_even", "s5_lam_re", "s5_lam_im", "s5_log_dt", "s5_b_re", "s5_b_im", "s5_c_re",
           "s5_c_im", "s5_d", "s5_w_glu", "s5_b_glu", "ret_gn_gain", "w_out_even", "norm_odd", "w_in_odd",
           "sgu_norm_gain", "sgu_w_spatial", "sgu_b_spatial", "w_out_odd", "final_norm")
MATRICES = ("w_in_even", "s5_w_glu", "w_out_even", "w_in_odd", "w_out_odd")
SHARDED_VECS = ("norm_odd", "sgu_norm_gain")
REPLICATED = tuple(n for n in WEIGHTS if n not in MATRICES and n not in SHARDED_VECS)
SMALL = tuple(n for n in WEIGHTS if n not in MATRICES)
LANES = 128


def kernel(x, norm_even, w_in_even, s5_lam_re, s5_lam_im, s5_log_dt, s5_b_re, s5_b_im, s5_c_re, s5_c_im, s5_d, s5_w_glu, s5_b_glu, ret_gn_gain, w_out_even, norm_odd, w_in_odd, sgu_norm_gain, sgu_w_spatial, sgu_b_spatial, w_out_odd, final_norm, loss_target, m_norm_even, m_w_in_even, m_s5_lam_re, m_s5_lam_im, m_s5_log_dt, m_s5_b_re, m_s5_b_im, m_s5_c_re, m_s5_c_im, m_s5_d, m_s5_w_glu, m_s5_b_glu, m_ret_gn_gain, m_w_out_even, m_norm_odd, m_w_in_odd, m_sgu_norm_gain, m_sgu_w_spatial, m_sgu_b_spatial, m_w_out_odd, m_final_norm, v_norm_even, v_w_in_even, v_s5_lam_re, v_s5_lam_im, v_s5_log_dt, v_s5_b_re, v_s5_b_im, v_s5_c_re, v_s5_c_im, v_s5_d, v_s5_w_glu, v_s5_b_glu, v_ret_gn_gain, v_w_out_even, v_norm_odd, v_w_in_odd, v_sgu_norm_gain, v_sgu_w_spatial, v_sgu_b_spatial, v_w_out_odd, v_final_norm):
    w = dict(norm_even=norm_even, w_in_even=w_in_even, s5_lam_re=s5_lam_re, s5_lam_im=s5_lam_im, s5_log_dt=s5_log_dt, s5_b_re=s5_b_re, s5_b_im=s5_b_im, s5_c_re=s5_c_re, s5_c_im=s5_c_im, s5_d=s5_d, s5_w_glu=s5_w_glu, s5_b_glu=s5_b_glu, ret_gn_gain=ret_gn_gain, w_out_even=w_out_even, norm_odd=norm_odd, w_in_odd=w_in_odd, sgu_norm_gain=sgu_norm_gain, sgu_w_spatial=sgu_w_spatial, sgu_b_spatial=sgu_b_spatial, w_out_odd=w_out_odd, final_norm=final_norm)
    m = dict(norm_even=m_norm_even, w_in_even=m_w_in_even, s5_lam_re=m_s5_lam_re, s5_lam_im=m_s5_lam_im, s5_log_dt=m_s5_log_dt, s5_b_re=m_s5_b_re, s5_b_im=m_s5_b_im, s5_c_re=m_s5_c_re, s5_c_im=m_s5_c_im, s5_d=m_s5_d, s5_w_glu=m_s5_w_glu, s5_b_glu=m_s5_b_glu, ret_gn_gain=m_ret_gn_gain, w_out_even=m_w_out_even, norm_odd=m_norm_odd, w_in_odd=m_w_in_odd, sgu_norm_gain=m_sgu_norm_gain, sgu_w_spatial=m_sgu_w_spatial, sgu_b_spatial=m_sgu_b_spatial, w_out_odd=m_w_out_odd, final_norm=m_final_norm)
    v = dict(norm_even=v_norm_even, w_in_even=v_w_in_even, s5_lam_re=v_s5_lam_re, s5_lam_im=v_s5_lam_im, s5_log_dt=v_s5_log_dt, s5_b_re=v_s5_b_re, s5_b_im=v_s5_b_im, s5_c_re=v_s5_c_re, s5_c_im=v_s5_c_im, s5_d=v_s5_d, s5_w_glu=v_s5_w_glu, s5_b_glu=v_s5_b_glu, ret_gn_gain=v_ret_gn_gain, w_out_even=v_w_out_even, norm_odd=v_norm_odd, w_in_odd=v_w_in_odd, sgu_norm_gain=v_sgu_norm_gain, sgu_w_spatial=v_sgu_w_spatial, sgu_b_spatial=v_sgu_b_spatial, w_out_odd=v_w_out_odd, final_norm=v_final_norm)

    grad_x, landed, part, other = local_grads(x[0], loss_target[0], w)

    small = SMALL + ("loss",)
    part["w_in_even"] = sum_slabs(landed["w_in_even"], "sum_w_in_even")
    part.update(sum_small({n: landed[n] for n in small}))
    names = ("w_in_even",) + small
    other.update(zip(names, run_plan(_SiblingPlan([part[n] for n in names]), "sibling_exchange")))

    wt, mt, vt = dict(w), dict(m), dict(v)
    for n in WIDE_ROWS:
        wt[n], mt[n], vt[n] = (jnp.transpose(a[n][0], (2, 0, 1)) for a in (w, m, v))
    out_g, out_d, out_m, out_v = adam_small(SMALL, wt, mt, vt, part, other)
    for n in WIDE_ROWS:
        for out in (out_g, out_d, out_m, out_v):
            out[n] = jnp.transpose(out[n], (1, 2, 0))[None]
    for n in MATRICES:
        res = adam_update(w[n][0], m[n][0], v[n][0], part[n], other[n], "adam_" + n)
        out_g[n], out_d[n], out_m[n], out_v[n] = (r[None] for r in res)
    total_loss = (part["loss"] + other["loss"])[0, 0]

    return (total_loss, grad_x[None], *[out_g[n] for n in WEIGHTS], *[out_d[n] for n in WEIGHTS],
            *[out_m[n] for n in WEIGHTS], *[out_v[n] for n in WEIGHTS])
```

```python
import functools
import math

import numpy as np
import jax
import jax.numpy as jnp
from jax import lax
from jax.experimental import pallas as pl
from jax.experimental.pallas import tpu as pltpu

F32 = jnp.float32
MXU_DTYPE = jnp.bfloat16
NORM_EPS = 1e-6
D_MODEL = 1024
S5_WIDTH = 1024
S5_GROUP = 16
S5_GROUPS = 64
S5_STATE = 64
S5_LANES = S5_GROUPS * S5_STATE
S5_KBLK = 8
RET_HEADS = 4
RET_DK = 256
RET_CHUNK = 128
ROPE_BASE = 10000.0
SGU_WIDTH = 2048
SGU_GROUPS = 4
SGU_GDIM = 512
SGU_CHUNK = 128
EVEN_IN = 6144
ODD_IN = 6144
ADAM_LR = 0.001
ADAM_B1 = 0.9
ADAM_B2 = 0.999
ADAM_EPS = 1e-08
ADAM_WD = 0.01
ADAM_STEP = 10
N_CHIPS = 4
VMEM_LIMIT = 56 * 1024 * 1024

TL_PROJ = 512
TL_DW = 1024
TL_S5 = 256
TL_SGU = 256


def _cparams(sem, **kw):
    return pltpu.CompilerParams(dimension_semantics=sem, vmem_limit_bytes=VMEM_LIMIT, **kw)


def _mm(a, b):
    return jnp.dot(a.astype(MXU_DTYPE), b.astype(MXU_DTYPE), preferred_element_type=F32)


def _mm_nt(a, b):
    return lax.dot_general(a.astype(MXU_DTYPE), b.astype(MXU_DTYPE),
                           (((1,), (1,)), ((), ())), preferred_element_type=F32)


def _mm_tn(a, b):
    return lax.dot_general(a.astype(MXU_DTYPE), b.astype(MXU_DTYPE),
                           (((0,), (0,)), ((), ())), preferred_element_type=F32)


_GELU_C = math.sqrt(2.0 / math.pi)


def _gelu_parts(x):
    x2 = x * x
    th = jnp.tanh(x * (_GELU_C + (_GELU_C * 0.044715) * x2))
    hx = 0.5 * x
    return hx + hx * th, th, x2, hx


def _gelu(x):
    return _gelu_parts(x)[0]


def _gelu_and_grad(x):
    g, th, x2, hx = _gelu_parts(x)
    return g, (0.5 + 0.5 * th) + hx * (1.0 - th * th) * (_GELU_C + (3.0 * _GELU_C * 0.044715) * x2)


def _gelu_grad(x):
    return _gelu_and_grad(x)[1]


def _sigmoid(x):
    return 1.0 / (1.0 + jnp.exp(-x))


def _silu_and_grad(x):
    s = _sigmoid(x)
    return x * s, s * (1.0 + x * (1.0 - s))


def _rms(x):
    return lax.rsqrt(jnp.mean(x * x, axis=-1, keepdims=True) + NORM_EPS)


def _full(shape):
    nd = len(shape)
    return pl.BlockSpec(shape, lambda *_: (0,) * nd)


MESH = pl.DeviceIdType.MESH
ANY = pl.BlockSpec(memory_space=pl.ANY)


def _place():
    return lax.axis_index("x"), lax.axis_index("y"), lax.axis_index("c")


def _chip_peer(x, y, c, d):
    return (1 - x if d >= 2 else x, 1 - y if d % 2 else y, c)


class _Plan:
    def __init__(self, inputs, out_shape, build):
        self.inputs, self.out_shape, self._build = list(inputs), list(out_shape), build
        n = len(self.inputs)
        self.sems = [pltpu.SemaphoreType.DMA((n, 3)), pltpu.SemaphoreType.DMA((n, 3)), pltpu.SemaphoreType.DMA((n,))]

    def start(self, in_refs, out_refs, sems):
        send, recv, local = self._build(in_refs, out_refs, sems)
        for p in range(len(self.inputs)):
            local[p].start()
            for cp in send[p]:
                cp.start()

    def wait(self, in_refs, out_refs, sems):
        send, recv, local = self._build(in_refs, out_refs, sems)
        for p in range(len(self.inputs)):
            for cp in recv[p]:
                cp.wait_recv()
        for p in range(len(self.inputs)):
            for cp in send[p]:
                cp.wait_send()
            local[p].wait()


class _GatherPlan:
    def __init__(self, shards, only=None):
        n = len(shards)
        self.n, self.only = n, only
        self.peers = (1, 2, 3) if only is None else (only,)
        self.inputs = list(shards)
        slabs = N_CHIPS if only is None else 1
        self.out_shape = [jax.ShapeDtypeStruct((slabs,) + s.shape, s.dtype) for s in shards]
        self.halved = [s.shape[0] % 32 == 0 for s in shards]
        self.sems = [pltpu.SemaphoreType.DMA((n, 3)) for _ in range(4)] + [pltpu.SemaphoreType.DMA((n,))]

    def _copies(self, in_refs, out_refs, sems):
        ici_s, ici_r, d2d_s, d2d_r, loc = sems
        x, y, c = _place()
        me = 2 * x + y

        def rows(p, core):
            if not self.halved[p]:
                return slice(None)
            half = self.inputs[p].shape[0] // 2
            return pl.ds(pl.multiple_of(core * half, 16), half)

        def slab(chip):
            return chip if self.only is None else 0

        def ici(p, d, chip, core):
            return pltpu.make_async_remote_copy(
                src_ref=in_refs[p].at[rows(p, core)], dst_ref=out_refs[p].at[slab(chip), rows(p, core)],
                send_sem=ici_s.at[p, d - 1], recv_sem=ici_r.at[p, d - 1],
                device_id=_chip_peer(x, y, c, d), device_id_type=MESH)

        def d2d(p, d, core):
            part = out_refs[p].at[slab(me ^ d), rows(p, core)]
            return pltpu.make_async_remote_copy(
                src_ref=part, dst_ref=part, send_sem=d2d_s.at[p, d - 1], recv_sem=d2d_r.at[p, d - 1],
                device_id=(x, y, 1 - c), device_id_type=MESH)

        local = [pltpu.make_async_copy(in_refs[p], out_refs[p].at[slab(me)], loc.at[p]) for p in range(self.n)]
        return me, c, ici, d2d, local

    def start(self, in_refs, out_refs, sems):
        me, c, ici, d2d, local = self._copies(in_refs, out_refs, sems)
        for p in range(self.n):
            if self.only is None:
                local[p].start()
            for d in self.peers:
                ici(p, d, me, c).start()

    def wait(self, in_refs, out_refs, sems):
        me, c, ici, d2d, local = self._copies(in_refs, out_refs, sems)
        for p in range(self.n):
            for d in self.peers:
                ici(p, d, me ^ d, c).wait_recv()
                if self.halved[p]:
                    d2d(p, d, c).start()
        for p in range(self.n):
            for d in self.peers:
                if self.halved[p]:
                    d2d(p, d, 1 - c).wait_recv()
                    d2d(p, d, c).wait_send()
                ici(p, d, me, c).wait_send()
            if self.only is None:
                local[p].wait()


def gather_plan(shards, only=None):
    return _GatherPlan(shards, only)


def reduce_plan(shards, whole=()):
    n_s = len(shards)

    def build(in_refs, out_refs, sems):
        send_sems, recv_sems, loc_sems = sems
        x, y, c = _place()
        me = 2 * x + y

        def src(p, slab):
            return in_refs[p].at[slab] if p < n_s else in_refs[p]

        def remote(p, d):
            return pltpu.make_async_remote_copy(
                src_ref=src(p, me ^ d), dst_ref=out_refs[p].at[d], send_sem=send_sems.at[p, d - 1],
                recv_sem=recv_sems.at[p, d - 1], device_id=_chip_peer(x, y, c, d), device_id_type=MESH)

        n = len(in_refs)
        send = [[remote(p, d) for d in (1, 2, 3)] for p in range(n)]
        local = [pltpu.make_async_copy(src(p, me), out_refs[p].at[0], loc_sems.at[p]) for p in range(n)]
        return send, send, local

    outs = [jax.ShapeDtypeStruct(s.shape, s.dtype) for s in shards]
    outs += [jax.ShapeDtypeStruct((N_CHIPS,) + a.shape, a.dtype) for a in whole]
    return _Plan(list(shards) + list(whole), outs, build)


class _SiblingPlan:
    def __init__(self, arrs):
        self.inputs = list(arrs)
        self.out_shape = [jax.ShapeDtypeStruct(a.shape, a.dtype) for a in arrs]
        n = len(arrs)
        self.sems = [pltpu.SemaphoreType.DMA((n,)), pltpu.SemaphoreType.DMA((n,))]

    def _copies(self, in_refs, out_refs, sems):
        x, y, c = _place()
        return [pltpu.make_async_remote_copy(
            src_ref=in_refs[p], dst_ref=out_refs[p], send_sem=sems[0].at[p], recv_sem=sems[1].at[p],
            device_id=(x, y, 1 - c), device_id_type=MESH) for p in range(len(self.inputs))]

    def start(self, in_refs, out_refs, sems):
        for cp in self._copies(in_refs, out_refs, sems):
            cp.start()

    def wait(self, in_refs, out_refs, sems):
        copies = self._copies(in_refs, out_refs, sems)
        for cp in copies:
            cp.wait_recv()
        for cp in copies:
            cp.wait_send()


def run_plan(plan, name):
    n = len(plan.inputs)

    def body(*refs):
        plan.start(refs[:n], refs[n:2 * n], refs[2 * n:])
        plan.wait(refs[:n], refs[n:2 * n], refs[2 * n:])

    return pl.pallas_call(body, name=name, in_specs=[ANY] * n, out_specs=[ANY] * n, out_shape=plan.out_shape,
                          scratch_shapes=plan.sems)(*plan.inputs)


def _call(body, plan, *, name, grid, in_specs, out_specs, out_shape, sem, scratch_shapes=(), aliases=None,
          n_prefetch=0):
    aliases = {} if aliases is None else aliases
    single = not isinstance(out_shape, (list, tuple))
    out_specs = [out_specs] if single else list(out_specs)
    out_shape = [out_shape] if single else list(out_shape)
    n_in, n_out, n_scr = len(in_specs), len(out_specs), len(scratch_shapes)
    ci = 0 if plan is None else len(plan.inputs)
    co = 0 if plan is None else len(plan.out_shape)

    def hosted(*refs):
        pre, refs = refs[:n_prefetch], refs[n_prefetch:]
        ins, cins = refs[:n_in], refs[n_in:n_in + ci]
        k = n_in + ci
        outs, couts = refs[k:k + n_out], refs[k + n_out:k + n_out + co]
        k += n_out + co
        scr, sems = refs[k:k + n_scr], refs[k + n_scr:]
        ids = [pl.program_id(a) for a in range(len(grid))]
        first = functools.reduce(jnp.logical_and, [i == 0 for i in ids])
        last = functools.reduce(jnp.logical_and, [i == g - 1 for i, g in zip(ids, grid)])

        @pl.when(first)
        def _():
            plan.start(cins, couts, sems)

        body(*pre, *ins, *outs, *scr)

        @pl.when(last)
        def _():
            plan.wait(cins, couts, sems)

    def run(*args):
        hosting = plan is not None
        spec = pltpu.PrefetchScalarGridSpec(
            num_scalar_prefetch=n_prefetch, grid=grid,
            in_specs=list(in_specs) + ([ANY] * ci if hosting else []),
            out_specs=out_specs + ([ANY] * co if hosting else []),
            scratch_shapes=list(scratch_shapes) + (plan.sems if hosting else []))
        res = pl.pallas_call(hosted if hosting else body, name=name, grid_spec=spec,
                             out_shape=out_shape + (plan.out_shape if hosting else []),
                             input_output_aliases=aliases, compiler_params=_cparams(sem),
                             )(*args, *(plan.inputs if hosting else []))
        return (res[0] if single else res[:n_out]), list(res[n_out:])

    return run


def norm_matmul(x, g, w, name, plan=None, tn=None):
    L, D = x.shape
    tl = min(TL_DW, L)
    if w.ndim == 3:
        nt, _, tn = w.shape
        w_spec = pl.BlockSpec((1, D, tn), lambda i, n: (n, 0, 0))
    else:
        nt = w.shape[1] // tn
        w_spec = pl.BlockSpec((D, tn), lambda i, n: (0, n))

    def body(x_ref, g_ref, w_ref, o_ref, h_ref):
        xv = x_ref[...]
        h = (xv * _rms(xv) * g_ref[...]).astype(h_ref.dtype)
        h_ref[...] = h
        o_ref[...] = _mm(h, w_ref[0] if w.ndim == 3 else w_ref[...])

    return _call(
        body, plan, name=name, grid=(L // tl, nt),
        in_specs=[pl.BlockSpec((tl, D), lambda i, n: (i, 0)), _full((1, D)), w_spec],
        out_specs=[pl.BlockSpec((tl, tn), lambda i, n: (i, n)), pl.BlockSpec((tl, D), lambda i, n: (i, 0))],
        out_shape=[jax.ShapeDtypeStruct((L, nt * tn), F32), jax.ShapeDtypeStruct((L, D), MXU_DTYPE)],
        sem=("arbitrary", "arbitrary"),
    )(x, g, w)


def even_in_slabs(x, xs, g, w, slabs, wsel, name, p_in=None, plan=None):
    L, D = x.shape
    tl = min(TL_DW, L)
    wb = EVEN_IN // N_CHIPS
    n = slabs.shape[0]
    s5_cols = 2 * S5_WIDTH - wb
    first = p_in is None

    def body(slabs_ref, wsel_ref, xs_ref, x_ref, g_ref, w_ref, *rest):
        o_ref = rest[-3] if first else rest[-1]
        j = slabs_ref[pl.program_id(0)]
        hs = (xs_ref[...] * _rms(xs_ref[...]) * g_ref[...]).astype(MXU_DTYPE)
        h = (x_ref[...] * _rms(x_ref[...]) * g_ref[...]).astype(MXU_DTYPE)
        if first:
            rest[-2][...] = hs
            rest[-1][...] = h
        o_ref[:, :s5_cols] = _mm(jnp.where(j <= 1, hs, h), w_ref[0, :, :s5_cols])
        o_ref[:, s5_cols:] = _mm(jnp.where(j == 0, hs, h), w_ref[0, :, s5_cols:])

    row = pl.BlockSpec((tl, D), lambda s, i, slabs_ref, wsel_ref: (i, 0))
    in_specs = [row if first else
                pl.BlockSpec((tl, D), lambda s, i, slabs_ref, wsel_ref: (jnp.where(slabs_ref[s] <= 1, i, 0), 0)),
                row if first else
                pl.BlockSpec((tl, D), lambda s, i, slabs_ref, wsel_ref: (jnp.where(slabs_ref[s] >= 1, i, 0), 0)),
                pl.BlockSpec((1, D), lambda s, i, slabs_ref, wsel_ref: (0, 0)),
                pl.BlockSpec((1, D, wb), lambda s, i, slabs_ref, wsel_ref: (wsel_ref[s], 0, 0))]
    out_specs = [pl.BlockSpec((tl, wb), lambda s, i, slabs_ref, wsel_ref: (i, slabs_ref[s]))]
    out_shape = [jax.ShapeDtypeStruct((L, EVEN_IN), F32)]
    args = [slabs, wsel, xs, x, g, w]
    if first:
        out_specs += [row, row]
        out_shape += [jax.ShapeDtypeStruct((L, D), MXU_DTYPE)] * 2
    else:
        in_specs.append(ANY)
        args.append(p_in)
    return _call(body, plan, name=name, grid=(n, L // tl), in_specs=in_specs, out_specs=out_specs,
                 out_shape=out_shape, sem=("arbitrary", "arbitrary"), n_prefetch=2,
                 aliases={} if first else {6: 0})(*args)


def matmul_residual(ys, w, x, name):
    L, D = x.shape
    tl = min(TL_PROJ, L)
    n = len(ys)
    offs = np.cumsum([0] + [y.shape[1] for y in ys])

    def body(*refs):
        y_refs, w_ref, x_ref, o_ref = refs[:n], refs[n], refs[n + 1], refs[n + 2]
        acc = x_ref[...]
        for k in range(n):
            acc = acc + _mm(y_refs[k][...], w_ref[offs[k]:offs[k + 1], :])
        o_ref[...] = acc

    return pl.pallas_call(
        body, name=name, grid=(L // tl,),
        in_specs=[pl.BlockSpec((tl, y.shape[1]), lambda i: (i, 0)) for y in ys]
        + [_full(w.shape), pl.BlockSpec((tl, D), lambda i: (i, 0))],
        out_specs=pl.BlockSpec((tl, D), lambda i: (i, 0)),
        out_shape=jax.ShapeDtypeStruct((L, D), F32),
        compiler_params=_cparams(("arbitrary",)),
    )(*ys, w, x)


def out_proj_loss(y, w, x, gf, tgt, name):
    L, K = y.shape
    D = w.shape[1]
    tl = min(TL_PROJ, L)

    def body(y_ref, w_ref, x_ref, gf_ref, t_ref, dx_ref, loss_ref, dg_ref):
        @pl.when(pl.program_id(0) == 0)
        def _():
            loss_ref[...] = jnp.zeros_like(loss_ref)
            dg_ref[...] = jnp.zeros_like(dg_ref)

        x2 = x_ref[...] + _mm(y_ref[...], w_ref[...])
        r = _rms(x2)
        xn = x2 * r
        e = xn * gf_ref[...] - t_ref[...]
        loss_ref[...] += (0.5 / D) * jnp.sum(e * e)
        dout = e * (1.0 / D)
        dg_ref[...] += jnp.sum(dout * xn, axis=0, keepdims=True)
        dxn = dout * gf_ref[...]
        dx_ref[...] = r * (dxn - xn * jnp.mean(dxn * xn, axis=-1, keepdims=True))

    return pl.pallas_call(
        body, name=name, grid=(L // tl,),
        in_specs=[pl.BlockSpec((tl, K), lambda i: (i, 0)), _full((K, D)),
                  pl.BlockSpec((tl, D), lambda i: (i, 0)), _full((1, D)),
                  pl.BlockSpec((tl, D), lambda i: (i, 0))],
        out_specs=[pl.BlockSpec((tl, D), lambda i: (i, 0)), _full((8, 128)), _full((1, D))],
        out_shape=[jax.ShapeDtypeStruct((L, D), F32), jax.ShapeDtypeStruct((8, 128), F32),
                   jax.ShapeDtypeStruct((1, D), F32)],
        compiler_params=_cparams(("arbitrary",)),
    )(y, w, x, gf, tgt)


def out_proj_bwd(dx, w, ys, name):
    L, D = dx.shape
    K = w.shape[0]
    tl = min(TL_PROJ, L)
    n = len(ys)
    offs = np.cumsum([0] + [y.shape[1] for y in ys])

    def body(*refs):
        dx_ref, w_ref, y_refs = refs[0], refs[1], refs[2:2 + n]
        dy_refs, dw_ref = refs[2 + n:2 + 2 * n], refs[2 + 2 * n]

        @pl.when(pl.program_id(0) == 0)
        def _():
            dw_ref[...] = jnp.zeros_like(dw_ref)

        dxv = dx_ref[...]
        for k in range(n):
            dy_refs[k][...] = _mm_nt(dxv, w_ref[offs[k]:offs[k + 1], :])
            dw_ref[offs[k]:offs[k + 1], :] += _mm_tn(y_refs[k][...], dxv)

    y_specs = [pl.BlockSpec((tl, y.shape[1]), lambda i: (i, 0)) for y in ys]
    return pl.pallas_call(
        body, name=name, grid=(L // tl,),
        in_specs=[pl.BlockSpec((tl, D), lambda i: (i, 0)), _full((K, D))] + y_specs,
        out_specs=y_specs + [_full((K, D))],
        out_shape=[jax.ShapeDtypeStruct(y.shape, F32) for y in ys] + [jax.ShapeDtypeStruct((K, D), F32)],
        compiler_params=_cparams(("arbitrary",)),
    )(dx, w, *ys)


def out_proj_dw(dx, y, name):
    L, D = dx.shape
    K = y.shape[1]
    tl = min(TL_DW, L)

    def body(dx_ref, y_ref, dw_ref):
        @pl.when(pl.program_id(0) == 0)
        def _():
            dw_ref[...] = jnp.zeros_like(dw_ref)

        dw_ref[...] += _mm_tn(y_ref[...], dx_ref[...])

    return pl.pallas_call(
        body, name=name, grid=(L // tl,),
        in_specs=[pl.BlockSpec((tl, D), lambda i: (i, 0)), pl.BlockSpec((tl, K), lambda i: (i, 0))],
        out_specs=_full((K, D)), out_shape=jax.ShapeDtypeStruct((K, D), F32),
        compiler_params=_cparams(("arbitrary",)),
    )(dx, y)


def in_proj_bwd_dx(x, g, dps, ws, dres, name, plan=None):
    L, D = x.shape
    tl = min(TL_PROJ, L)
    n = len(dps)

    def body(*refs):
        x_ref, g_ref, dres_ref = refs[:3]
        dp_refs, w_refs = refs[3:3 + n], refs[3 + n:3 + 2 * n]
        dx_ref, dg_ref = refs[3 + 2 * n:]

        @pl.when(pl.program_id(0) == 0)
        def _():
            dg_ref[...] = jnp.zeros_like(dg_ref)

        dh = None
        for dp_ref, w_ref, w in zip(dp_refs, w_refs, ws):
            if w.ndim == 3:
                tn = w.shape[2]
                parts = [_mm_nt(dp_ref[:, tn * k:tn * (k + 1)], w_ref[k]) for k in range(w.shape[0])]
            else:
                parts = [_mm_nt(dp_ref[...], w_ref[...])]
            for part in parts:
                dh = part if dh is None else dh + part
        xv = x_ref[...]
        r = _rms(xv)
        xn = xv * r
        dg_ref[...] += jnp.sum(dh * xn, axis=0, keepdims=True)
        dxn = dh * g_ref[...]
        dx_ref[...] = dres_ref[...] + r * (dxn - xn * jnp.mean(dxn * xn, axis=-1, keepdims=True))

    return _call(
        body, plan, name=name, grid=(L // tl,),
        in_specs=[pl.BlockSpec((tl, D), lambda i: (i, 0)), _full((1, D)), pl.BlockSpec((tl, D), lambda i: (i, 0))]
        + [pl.BlockSpec((tl, dp.shape[1]), lambda i: (i, 0)) for dp in dps] + [_full(w.shape) for w in ws],
        out_specs=[pl.BlockSpec((tl, D), lambda i: (i, 0)), _full((1, D))],
        out_shape=[jax.ShapeDtypeStruct((L, D), F32), jax.ShapeDtypeStruct((1, D), F32)],
        sem=("arbitrary",),
    )(x, g, dres, *dps, *ws)


def in_proj_bwd_dw(h, dp, name, tn, first=0, into=None, dtype=F32, plan=None, dp_first=0, count=None):
    L, D = h.shape
    tl = min(TL_DW, L)
    wb = EVEN_IN // N_CHIPS
    per = wb // tn
    count = dp.shape[1] // tn if count is None else count
    last = L // tl - 1

    def body(*refs):
        h_ref, dp_ref, dw_ref, acc = refs[0], refs[1], refs[-2], refs[-1]

        @pl.when(pl.program_id(1) == 0)
        def _():
            acc[...] = jnp.zeros_like(acc)

        acc[...] += _mm_tn(h_ref[...], dp_ref[...])

        @pl.when(pl.program_id(1) == last)
        def _():
            dw_ref[0] = acc[...].astype(dw_ref.dtype)

    ins = [h, dp] + ([] if into is None else [into])
    return _call(
        body, plan, name=name, grid=(count, L // tl),
        in_specs=[pl.BlockSpec((tl, D), lambda n, i: (i, 0)), pl.BlockSpec((tl, tn), lambda n, i: (i, n + dp_first))]
        + ([] if into is None else [ANY]),
        out_specs=pl.BlockSpec((1, D, tn), lambda n, i: ((n + first) // per, 0, (n + first) % per)),
        out_shape=jax.ShapeDtypeStruct((N_CHIPS, D, wb), dtype),
        scratch_shapes=[pltpu.VMEM((D, tn), F32)],
        aliases={} if into is None else {2: 0},
        sem=("arbitrary", "arbitrary"),
    )(*ins)


def _s5_param_fn(lam_re, lam_im, log_dt, b_re, b_im):
    lr = jnp.minimum(lam_re, -1e-4)
    li = lam_im
    dt = jnp.exp(log_dt)
    mag = jnp.exp(lr * dt)
    ab_re = mag * jnp.cos(li * dt)
    ab_im = mag * jnp.sin(li * dt)
    den = lr * lr + li * li
    n_re = ab_re - 1.0
    n_im = ab_im
    z_re = (n_re * lr + n_im * li) / den
    z_im = (n_im * lr - n_re * li) / den
    bb_re = z_re[None] * b_re - z_im[None] * b_im
    bb_im = z_re[None] * b_im + z_im[None] * b_re
    return ab_re, ab_im, bb_re, bb_im


def s5_params_fwd(lam_re, lam_im, log_dt, b_re, b_im, span):
    G, P = lam_re.shape
    H = b_re.shape[0]
    assert span & (span - 1) == 0

    def body(lr_ref, li_ref, dt_ref, br_ref, bi_ref, abr_ref, abi_ref, bbr_ref, bbi_ref, pr_ref, pi_ref):
        ab_re, ab_im, bb_re, bb_im = _s5_param_fn(lr_ref[...], li_ref[...], dt_ref[...], br_ref[...], bi_ref[...])
        abr_ref[...] = ab_re
        abi_ref[...] = ab_im
        bbr_ref[...] = bb_re
        bbi_ref[...] = bb_im
        cr, ci = ab_re, ab_im
        for _ in range(span.bit_length() - 1):
            cr, ci = cr * cr - ci * ci, 2.0 * cr * ci
        pr_ref[...] = cr
        pi_ref[...] = ci

    shp = lambda *s: jax.ShapeDtypeStruct(s, F32)
    return pl.pallas_call(
        body, name="s5_params_fwd",
        out_shape=[shp(G, P), shp(G, P), shp(H, G, P), shp(H, G, P), shp(G, P), shp(G, P)],
    )(lam_re, lam_im, log_dt, b_re, b_im)


def s5_params_bwd(lam_re, lam_im, log_dt, b_re, b_im, d_ab_re, d_ab_im, d_bb_re, d_bb_im):
    G, P = lam_re.shape
    H = b_re.shape[0]

    def body(lr_ref, li_ref, dt_ref, br_ref, bi_ref, g0, g1, g2, g3, o0, o1, o2, o3, o4):
        prim = (lr_ref[...], li_ref[...], dt_ref[...], br_ref[...], bi_ref[...])
        _, vjp = jax.vjp(_s5_param_fn, *prim)
        d = vjp((jnp.sum(g0[...], axis=0), jnp.sum(g1[...], axis=0), g2[...], g3[...]))
        o0[...], o1[...], o2[...], o3[...], o4[...] = d

    shp = lambda *s: jax.ShapeDtypeStruct(s, F32)
    return pl.pallas_call(
        body, name="s5_params_bwd",
        out_shape=[shp(G, P), shp(G, P), shp(G, 1), shp(H, G, P), shp(H, G, P)],
    )(lam_re, lam_im, log_dt, b_re, b_im, d_ab_re, d_ab_im, d_bb_re, d_bb_im)


def stream_order(a, tl):
    L, C = a.shape
    return a.reshape(L // tl, 8, tl // 8, C).transpose(0, 2, 1, 3).reshape(L, C)


def token_order(a, tl):
    L, C = a.shape
    return a.reshape(L // tl, tl // 8, 8, C).transpose(0, 2, 1, 3).reshape(L, C)


_LANE_BLK = 1024
_LANE_BLK_BWD = 1024


def _cmul_add(ar, ai, xr, xi, br, bi):
    return br + (ar * xr - ai * xi), bi + (ar * xi + ai * xr)


def _cmulc_add(ar, ai, xr, xi, br, bi):
    return br + (ar * xr + ai * xi), bi + (ar * xi - ai * xr)


def _s5_states(u, wbd_ref, a_re, a_im, at_re, at_im, s_re, s_im, e_re, e_im, c0_re, c0_im, tl):
    t8 = tl // 8
    for k in range(S5_KBLK):
        bu = _mm(u[:, 128 * k:128 * (k + 1)], wbd_ref[k])
        s_re[:, 512 * k:512 * (k + 1)] = bu[:, :512]
        s_im[:, 512 * k:512 * (k + 1)] = bu[:, 512:]
    outs_re, outs_im = [], []
    for b in range(S5_LANES // _LANE_BLK):
        lanes = slice(_LANE_BLK * b, _LANE_BLK * (b + 1))
        ar = jnp.broadcast_to(a_re[:, lanes], (8, _LANE_BLK))
        ai = jnp.broadcast_to(a_im[:, lanes], (8, _LANE_BLK))

        def local(i, carry, lanes=lanes, ar=ar, ai=ai):
            r = pl.multiple_of(i * 8, 8)
            sr, si = _cmul_add(ar, ai, carry[0], carry[1], s_re[pl.ds(r, 8), lanes], s_im[pl.ds(r, 8), lanes])
            s_re[pl.ds(r, 8), lanes] = sr
            s_im[pl.ds(r, 8), lanes] = si
            return sr, si

        zero = jnp.zeros((8, _LANE_BLK), F32)
        fr, fi = lax.fori_loop(0, t8, local, (zero, zero), unroll=True)
        tr, ti = at_re[:, lanes], at_im[:, lanes]
        er, ei = c0_re[:, lanes], c0_im[:, lanes]
        ers, eis = [er], [ei]
        for j in range(8):
            er, ei = _cmul_add(tr, ti, er, ei, fr[j:j + 1], fi[j:j + 1])
            ers.append(er)
            eis.append(ei)
        outs_re.append(ers[8])
        outs_im.append(eis[8])
        ent_r, ent_i = jnp.concatenate(ers[:8], axis=0), jnp.concatenate(eis[:8], axis=0)
        e_re[:, lanes] = ent_r
        e_im[:, lanes] = ent_i

        def fix(i, carry, lanes=lanes, ar=ar, ai=ai):
            r = pl.multiple_of(i * 8, 8)
            zr, zi = ar * carry[0] - ai * carry[1], ar * carry[1] + ai * carry[0]
            s_re[pl.ds(r, 8), lanes] = s_re[pl.ds(r, 8), lanes] + zr
            s_im[pl.ds(r, 8), lanes] = s_im[pl.ds(r, 8), lanes] + zi
            return zr, zi

        lax.fori_loop(0, t8, fix, (ent_r, ent_i), unroll=True)
    return jnp.concatenate(outs_re, axis=1), jnp.concatenate(outs_im, axis=1)


def _s5_readout(s_re, s_im, cre_ref, cim_ref):
    ys = []
    for k in range(S5_KBLK):
        lanes = slice(512 * k, 512 * (k + 1))
        ys.append(_mm(s_re[:, lanes], cre_ref[k]) - _mm(s_im[:, lanes], cim_ref[k]))
    return jnp.concatenate(ys, axis=1)


def s5_forward(p, wbd, cre, cim, atab, d_skip, w_glu, b_glu, plan=None):
    L = p.shape[0]
    tl = min(TL_S5, L)
    nch = L // tl

    def body(u_ref, z_ref, wbd_ref, cre_ref, cim_ref, at_ref, d_ref, wg_ref, bg_ref,
             ya_ref, st_re_ref, st_im_ref, sv_re_ref, sv_im_ref, s_re, s_im, e_re, e_im, car_re, car_im):
        @pl.when(pl.program_id(0) == 0)
        def _():
            car_re[...] = jnp.zeros_like(car_re)
            car_im[...] = jnp.zeros_like(car_im)

        c0_re, c0_im = car_re[...], car_im[...]
        st_re_ref[0] = c0_re
        st_im_ref[0] = c0_im
        u = u_ref[...]
        x_re, x_im = _s5_states(u, wbd_ref, at_ref[0:1], at_ref[1:2], at_ref[2:3], at_ref[3:4],
                                s_re, s_im, e_re, e_im, c0_re, c0_im, tl)
        car_re[...] = x_re
        car_im[...] = x_im
        sv_re_ref[...] = s_re[...].astype(sv_re_ref.dtype)
        sv_im_ref[...] = s_im[...].astype(sv_im_ref.dtype)
        y = _s5_readout(sv_re_ref, sv_im_ref, cre_ref, cim_ref) + d_ref[...] * u
        yg = _gelu(y)
        gate = _sigmoid(_mm(yg, wg_ref[...]) + bg_ref[...])
        sz, _ = _silu_and_grad(z_ref[...])
        ya_ref[...] = (yg * gate * sz).astype(ya_ref.dtype)

    return _call(
        body, plan, name="s5_forward", grid=(nch,),
        in_specs=[pl.BlockSpec((tl, 1024), lambda i: (i, 0)), pl.BlockSpec((tl, 1024), lambda i: (i, 1)),
                  _full(wbd.shape), _full(cre.shape), _full(cim.shape), _full(atab.shape),
                  _full((1, 1024)), _full((1024, 1024)), _full((1, 1024))],
        out_specs=[pl.BlockSpec((tl, 1024), lambda i: (i, 0)),
                   pl.BlockSpec((1, 1, S5_LANES), lambda i: (i, 0, 0)),
                   pl.BlockSpec((1, 1, S5_LANES), lambda i: (i, 0, 0)),
                   pl.BlockSpec((tl, S5_LANES), lambda i: (i, 0)), pl.BlockSpec((tl, S5_LANES), lambda i: (i, 0))],
        out_shape=[jax.ShapeDtypeStruct((L, 1024), MXU_DTYPE),
                   jax.ShapeDtypeStruct((nch, 1, S5_LANES), F32), jax.ShapeDtypeStruct((nch, 1, S5_LANES), F32),
                   jax.ShapeDtypeStruct((L, S5_LANES), MXU_DTYPE), jax.ShapeDtypeStruct((L, S5_LANES), MXU_DTYPE)],
        scratch_shapes=[pltpu.VMEM((tl, S5_LANES), F32), pltpu.VMEM((tl, S5_LANES), F32),
                        pltpu.VMEM((8, S5_LANES), F32), pltpu.VMEM((8, S5_LANES), F32),
                        pltpu.VMEM((1, S5_LANES), F32), pltpu.VMEM((1, S5_LANES), F32)],
        sem=("arbitrary",),
    )(p, p, wbd, cre, cim, atab, d_skip, w_glu, b_glu)


def s5_backward(p, dya, st_re, st_im, sv_re, sv_im, wbd, cre, cim, atab, d_skip, w_glu, b_glu, plan=None):
    L = p.shape[0]
    tl = min(TL_S5, L)
    t8 = tl // 8
    nch = L // tl
    rev = lambda i: (nch - 1 - i, 0)
    rev1 = lambda i: (nch - 1 - i, 1)
    rev3 = lambda i: (nch - 1 - i, 0, 0)
    ct_shape = (S5_KBLK, cre.shape[2], cre.shape[1])

    def body(u_ref, z_ref, dya_ref, str_ref, sti_ref, s_re, s_im, wbd_ref, cre_ref, cim_ref, at_ref,
             d_ref, wg_ref, bg_ref,
             dp_ref, dwbd_ref, dcre_ref, dcim_ref, dabr_ref, dabi_ref, dd_ref, dwg_ref, dbg_ref,
             g_re, g_im, car_re, car_im):
        @pl.when(pl.program_id(0) == 0)
        def _():
            car_re[...] = jnp.zeros_like(car_re)
            car_im[...] = jnp.zeros_like(car_im)
            for r in (dwbd_ref, dcre_ref, dcim_ref, dabr_ref, dabi_ref, dd_ref, dwg_ref, dbg_ref):
                r[...] = jnp.zeros_like(r)

        u = u_ref[...]
        a_re, a_im, at_re, at_im = at_ref[0:1], at_ref[1:2], at_ref[2:3], at_ref[3:4]
        y = _s5_readout(s_re, s_im, cre_ref, cim_ref) + d_ref[...] * u
        yg, dyg = _gelu_and_grad(y)
        gate = _sigmoid(_mm(yg, wg_ref[...]) + bg_ref[...])
        sz, dsz = _silu_and_grad(z_ref[...])
        dya = dya_ref[...]
        s5out = yg * gate
        dp_ref[:, 1024:] = (dya * s5out * dsz).astype(dp_ref.dtype)
        ds5 = dya * sz
        dt = ds5 * yg * gate * (1.0 - gate)
        dwg_ref[...] += _mm_tn(yg, dt)
        dbg_ref[...] += jnp.sum(dt, axis=0, keepdims=True)
        dyv = (ds5 * gate + _mm_nt(dt, wg_ref[...])) * dyg
        dd_ref[...] += jnp.sum(dyv * u, axis=0, keepdims=True)

        for k in range(S5_KBLK):
            lanes = slice(512 * k, 512 * (k + 1))
            dyk = dyv[:, 128 * k:128 * (k + 1)]
            g_re[:, lanes] = _mm_nt(dyk, cre_ref[k])
            g_im[:, lanes] = -_mm_nt(dyk, cim_ref[k])
            dcre_ref[k] += _mm_tn(dyk, s_re[:, lanes])
            dcim_ref[k] -= _mm_tn(dyk, s_im[:, lanes])

        blk = _LANE_BLK_BWD
        for b in range(S5_LANES // blk):
            lanes = slice(blk * b, blk * (b + 1))
            ar = jnp.broadcast_to(a_re[:, lanes], (8, blk))
            ai = jnp.broadcast_to(a_im[:, lanes], (8, blk))

            def local(j, carry, lanes=lanes, ar=ar, ai=ai):
                r = pl.multiple_of((t8 - 1 - j) * 8, 8)
                gr, gi = _cmulc_add(ar, ai, carry[0], carry[1], g_re[pl.ds(r, 8), lanes], g_im[pl.ds(r, 8), lanes])
                g_re[pl.ds(r, 8), lanes] = gr
                g_im[pl.ds(r, 8), lanes] = gi
                return gr, gi

            zero = jnp.zeros((8, blk), F32)
            fr, fi = lax.fori_loop(0, t8, local, (zero, zero), unroll=True)
            tr, ti = at_re[:, lanes], at_im[:, lanes]
            hr, hi = car_re[:, lanes], car_im[:, lanes]
            hrs, his = [hr], [hi]
            for j in range(7, -1, -1):
                hr, hi = _cmulc_add(tr, ti, hr, hi, fr[j:j + 1], fi[j:j + 1])
                hrs.append(hr)
                his.append(hi)
            car_re[:, lanes] = hrs[8]
            car_im[:, lanes] = his[8]
            in_r = jnp.concatenate(hrs[7::-1], axis=0)
            in_i = jnp.concatenate(his[7::-1], axis=0)

            wr, wi, nr, ni, accr, acci = in_r, in_i, zero, zero, zero, zero
            for pair in range(t8 // 2 - 1, -1, -1):
                rows = slice(16 * pair, 16 * pair + 16)
                s16r, s16i = s_re[rows, lanes].astype(F32), s_im[rows, lanes].astype(F32)
                for half in (1, 0):
                    r = 16 * pair + 8 * half
                    sr, si = s16r[8 * half:8 * half + 8], s16i[8 * half:8 * half + 8]
                    accr, acci = accr + (sr * nr + si * ni), acci + (sr * ni - si * nr)
                    wr, wi = ar * wr + ai * wi, ar * wi - ai * wr
                    nr, ni = g_re[r:r + 8, lanes] + wr, g_im[r:r + 8, lanes] + wi
                    g_re[r:r + 8, lanes] = nr
                    g_im[r:r + 8, lanes] = ni
            lr, li = s_re[tl - 16:tl, lanes].astype(F32)[8:], s_im[tl - 16:tl, lanes].astype(F32)[8:]
            row0 = lax.broadcasted_iota(jnp.int32, (8, blk), 0) == 0
            sr = jnp.where(row0, jnp.broadcast_to(str_ref[0][:, lanes], (8, blk)), pltpu.roll(lr, 1, 0))
            si = jnp.where(row0, jnp.broadcast_to(sti_ref[0][:, lanes], (8, blk)), pltpu.roll(li, 1, 0))
            dabr_ref[:, lanes] += accr + (sr * nr + si * ni)
            dabi_ref[:, lanes] += acci + (sr * ni - si * nr)

        dus = []
        for k in range(S5_KBLK):
            lanes = slice(512 * k, 512 * (k + 1))
            g = jnp.concatenate([g_re[:, lanes], g_im[:, lanes]], axis=1)
            dwbd_ref[k] += _mm_tn(u[:, 128 * k:128 * (k + 1)], g)
            dus.append(_mm_nt(g, wbd_ref[k]))
        du = jnp.concatenate(dus, axis=1) + dyv * d_ref[...]
        dp_ref[:, :1024] = du.astype(dp_ref.dtype)

    shp = lambda *s: jax.ShapeDtypeStruct(s, F32)
    return _call(
        body, plan, name="s5_backward", grid=(nch,),
        in_specs=[pl.BlockSpec((tl, 1024), rev), pl.BlockSpec((tl, 1024), rev1), pl.BlockSpec((tl, 1024), rev),
                  pl.BlockSpec((1, 1, S5_LANES), rev3), pl.BlockSpec((1, 1, S5_LANES), rev3),
                  pl.BlockSpec((tl, S5_LANES), rev), pl.BlockSpec((tl, S5_LANES), rev),
                  _full(wbd.shape), _full(cre.shape), _full(cim.shape), _full(atab.shape),
                  _full((1, 1024)), _full((1024, 1024)), _full((1, 1024))],
        out_specs=[pl.BlockSpec((tl, 2048), rev), _full(wbd.shape), _full(ct_shape), _full(ct_shape),
                   _full((8, S5_LANES)), _full((8, S5_LANES)), _full((1, 1024)), _full((1024, 1024)), _full((1, 1024))],
        out_shape=[jax.ShapeDtypeStruct((L, 2048), MXU_DTYPE), shp(*wbd.shape), shp(*ct_shape), shp(*ct_shape),
                   shp(8, S5_LANES), shp(8, S5_LANES), shp(1, 1024), shp(1024, 1024), shp(1, 1024)],
        scratch_shapes=[pltpu.VMEM((tl, S5_LANES), F32), pltpu.VMEM((tl, S5_LANES), F32),
                        pltpu.VMEM((1, S5_LANES), F32), pltpu.VMEM((1, S5_LANES), F32)],
        sem=("arbitrary",),
    )(p, p, dya, st_re, st_im, sv_re, sv_im, wbd, cre, cim, atab, d_skip, w_glu, b_glu)


def _block_diag(w, rows_first):
    g8 = w.reshape(S5_KBLK, 8, w.shape[1], w.shape[2])
    eye = jnp.eye(8, dtype=w.dtype)
    out = jnp.einsum('kgab,fg->kfagb', g8, eye)
    return out.reshape(S5_KBLK, 8 * w.shape[1], 8 * w.shape[2])


def _block_diag_extract(wbd, a, b):
    w5 = wbd.reshape(S5_KBLK, 8, a, 8, b)
    idx = jnp.arange(8)
    return w5[:, idx, :, idx, :].transpose(1, 0, 2, 3).reshape(S5_GROUPS, a, b)


def _ret_constants():
    log_g = np.log1p(-np.exp2(-5.0 - np.arange(RET_HEADS, dtype=np.float32))).astype(np.float32)
    idx = np.arange(RET_CHUNK, dtype=np.float32)
    diff = idx[:, None] - idx[None, :]
    decay = np.where(diff >= 0, np.exp(log_g[:, None, None] * np.maximum(diff, 0.0)), 0.0).astype(np.float32)
    xi = np.exp(log_g[None, :] * (idx[:, None] + 1.0)).astype(np.float32)
    zeta = np.exp(log_g[None, :] * (RET_CHUNK - 1.0 - idx[:, None])).astype(np.float32)
    chunk_decay = np.exp(log_g * RET_CHUNK).astype(np.float32)
    return decay, xi, zeta, chunk_decay


def _rope_tables(L):
    half = RET_DK // 2
    inv = ROPE_BASE ** (-jnp.arange(half, dtype=F32) / half)
    ang = jnp.arange(L, dtype=F32)[:, None] * inv[None, :]
    return jnp.cos(ang), jnp.sin(ang)


def _rot(xh, cos, sin):
    x1, x2 = xh[:, :128], xh[:, 128:]
    return jnp.concatenate([x1 * cos - x2 * sin, x1 * sin + x2 * cos], axis=1)


def _rot_t(dh, cos, sin):
    d1, d2 = dh[:, :128], dh[:, 128:]
    return jnp.concatenate([d1 * cos + d2 * sin, d2 * cos - d1 * sin], axis=1)


RET_PER_STEP = 4


def _ret_setup(L):
    nc = L // RET_CHUNK
    per = RET_PER_STEP if nc % RET_PER_STEP == 0 else 1
    decay_np, xi_np, zeta_np, cd_np = _ret_constants()
    tables = (jnp.asarray(decay_np), jnp.asarray(np.tile(xi_np, (per, 1))), jnp.asarray(np.tile(zeta_np, (per, 1))))
    return nc // per, per, tables, [float(c) for c in cd_np]


def _ret_rows(q_ref, k_ref, v_ref, cos_ref, sin_ref, xi_ref, zeta_ref):
    H = range(RET_HEADS)
    hs = [slice(RET_DK * h, RET_DK * (h + 1)) for h in H]
    cs, sn = cos_ref[...], sin_ref[...]
    qh = [_rot(q_ref[:, hs[h]], cs, sn) for h in H]
    kh = [_rot(k_ref[:, hs[h]], cs, sn) * (RET_DK ** -0.5) for h in H]
    vh = [v_ref[:, hs[h]] for h in H]
    qx = [qh[h] * xi_ref[:, h:h + 1] for h in H]
    kz = [kh[h] * zeta_ref[:, h:h + 1] for h in H]
    return hs, cs, sn, qh, kh, vh, qx, kz


def _ret_normed(qh, kh, vh, qx, dec_ref, prevs, per):
    H, C = range(RET_HEADS), range(per)
    rs = [slice(RET_CHUNK * c, RET_CHUNK * (c + 1)) for c in C]
    sc = [[_mm_nt(qh[h][rs[c]], kh[h][rs[c]]) * dec_ref[h] for h in H] for c in C]
    inner = [[_mm(sc[c][h], vh[h][rs[c]]) for h in H] for c in C]
    cross = [[_mm(qx[h][rs[c]], prevs[c][h]) for h in H] for c in C]
    o = [jnp.concatenate([inner[c][h] + cross[c][h] for c in C], axis=0) for h in H]
    oc = [o[h] - jnp.mean(o[h], axis=-1, keepdims=True) for h in H]
    rstd = [lax.rsqrt(jnp.mean(oc[h] * oc[h], axis=-1, keepdims=True) + NORM_EPS) for h in H]
    on = [oc[h] * rstd[h] for h in H]
    return rs, sc, rstd, on


def retention_forward(p, cos, sin, gain):
    L = p.shape[0]
    steps, per, (decay, xi, zeta), cd = _ret_setup(L)
    rows = RET_CHUNK * per

    def body(q_ref, k_ref, v_ref, z_ref, cos_ref, sin_ref, dec_ref, xi_ref, zeta_ref, gain_ref,
             yb_ref, prev_ref, state):
        @pl.when(pl.program_id(0) == 0)
        def _():
            state[...] = jnp.zeros_like(state)

        H, C = range(RET_HEADS), range(per)
        hs, cs, sn, qh, kh, vh, qx, kz = _ret_rows(q_ref, k_ref, v_ref, cos_ref, sin_ref, xi_ref, zeta_ref)
        prevs = [[state[h] for h in H]]
        for c in C:
            rs_c = slice(RET_CHUNK * c, RET_CHUNK * (c + 1))
            prevs.append([prevs[c][h] * cd[h] + _mm_tn(kz[h][rs_c], vh[h][rs_c]) for h in H])
        _, _, _, on = _ret_normed(qh, kh, vh, qx, dec_ref, prevs, per)
        sz, _ = _silu_and_grad(z_ref[...])
        for h in H:
            for c in C:
                prev_ref[c, h] = prevs[c][h].astype(prev_ref.dtype)
            state[h] = prevs[per][h]
            yb_ref[:, hs[h]] = (on[h] * gain_ref[:, hs[h]] * sz[:, hs[h]]).astype(yb_ref.dtype)

    col0 = p.shape[1] // 1024 - 4
    blk = lambda c: pl.BlockSpec((rows, 1024), lambda i, c=c: (i, c + col0))
    return pl.pallas_call(
        body, name="retention_forward", grid=(steps,),
        in_specs=[blk(0), blk(1), blk(2), blk(3),
                  pl.BlockSpec((rows, 128), lambda i: (i, 0)), pl.BlockSpec((rows, 128), lambda i: (i, 0)),
                  _full(decay.shape), _full(xi.shape), _full(zeta.shape), _full((1, 1024))],
        out_specs=[pl.BlockSpec((rows, 1024), lambda i: (i, 0)),
                   pl.BlockSpec((per, RET_HEADS, RET_DK, RET_DK), lambda i: (i, 0, 0, 0))],
        out_shape=[jax.ShapeDtypeStruct((L, 1024), MXU_DTYPE),
                   jax.ShapeDtypeStruct((steps * per, RET_HEADS, RET_DK, RET_DK), MXU_DTYPE)],
        scratch_shapes=[pltpu.VMEM((RET_HEADS, RET_DK, RET_DK), F32)],
        compiler_params=_cparams(("arbitrary",)),
    )(p, p, p, p, cos, sin, decay, xi, zeta, gain)


def retention_backward(p, dy, prevs, cos, sin, gain, plan=None):
    L = p.shape[0]
    steps, per, (decay, xi, zeta), cd = _ret_setup(L)
    rows = RET_CHUNK * per
    scale = RET_DK ** -0.5

    def body(q_ref, k_ref, v_ref, z_ref, dyb_ref, prev_ref, cos_ref, sin_ref, dec_ref, xi_ref, zeta_ref, gain_ref,
             dp_ref, dgain_ref, dstate):
        @pl.when(pl.program_id(0) == 0)
        def _():
            dstate[...] = jnp.zeros_like(dstate)
            dgain_ref[...] = jnp.zeros_like(dgain_ref)

        H, C = range(RET_HEADS), range(per)
        hs, cs, sn, qh, kh, vh, qx, kz = _ret_rows(q_ref, k_ref, v_ref, cos_ref, sin_ref, xi_ref, zeta_ref)
        prevs = [[prev_ref[c, h] for h in H] for c in C]
        rs, sc, rstd, on = _ret_normed(qh, kh, vh, qx, dec_ref, prevs, per)
        sz, dsz = _silu_and_grad(z_ref[...])
        dyb = dyb_ref[...]
        dong = [dyb[:, hs[h]] * sz[:, hs[h]] for h in H]
        don = [dong[h] * gain_ref[:, hs[h]] for h in H]
        do = [rstd[h] * (don[h] - jnp.mean(don[h], axis=-1, keepdims=True)
                         - on[h] * jnp.mean(don[h] * on[h], axis=-1, keepdims=True)) for h in H]
        dsc = [[_mm_nt(do[h][rs[c]], vh[h][rs[c]]) * dec_ref[h] for h in H] for c in C]
        dq_st = [[_mm_nt(do[h][rs[c]], prevs[c][h]) for h in H] for c in C]
        dnew = [[_mm_tn(qx[h][rs[c]], do[h][rs[c]]) for h in H] for c in C]
        dsts = [None] * per + [[dstate[h] for h in H]]
        for c in reversed(C):
            dsts[c] = [dsts[c + 1][h] * cd[h] + dnew[c][h] for h in H]
        dk_st = [[_mm_nt(vh[h][rs[c]], dsts[c + 1][h]) for h in H] for c in C]
        dv_st = [[_mm(kz[h][rs[c]], dsts[c + 1][h]) for h in H] for c in C]
        rows_of = lambda parts: jnp.concatenate(parts, axis=0)
        dqh = [rows_of([_mm(dsc[c][h], kh[h][rs[c]]) for c in C])
               + rows_of([dq_st[c][h] for c in C]) * xi_ref[:, h:h + 1] for h in H]
        dkh = [rows_of([_mm_tn(dsc[c][h], qh[h][rs[c]]) for c in C])
               + rows_of([dk_st[c][h] for c in C]) * zeta_ref[:, h:h + 1] for h in H]
        dvh = [rows_of([_mm_tn(sc[c][h], do[h][rs[c]]) + dv_st[c][h] for c in C]) for h in H]
        for h in H:
            dstate[h] = dsts[0][h]
            dgain_ref[:, hs[h]] += jnp.sum(dong[h] * on[h], axis=0, keepdims=True)
            dp_ref[:, hs[h]] = (_rot_t(dkh[h], cs, sn) * scale).astype(dp_ref.dtype)
            dp_ref[:, 1024 + RET_DK * h:1024 + RET_DK * (h + 1)] = dvh[h].astype(dp_ref.dtype)
            dp_ref[:, 2048 + RET_DK * h:2048 + RET_DK * (h + 1)] = (
                dyb[:, hs[h]] * on[h] * gain_ref[:, hs[h]] * dsz[:, hs[h]]).astype(dp_ref.dtype)
            dp_ref[:, 3072 + RET_DK * h:3072 + RET_DK * (h + 1)] = _rot_t(dqh[h], cs, sn).astype(dp_ref.dtype)

    col0 = p.shape[1] // 1024 - 4
    blk = lambda c: pl.BlockSpec((rows, 1024), lambda i, c=c: (steps - 1 - i, c + col0))
    tab = pl.BlockSpec((rows, 128), lambda i: (steps - 1 - i, 0))
    return _call(
        body, plan, name="retention_backward", grid=(steps,),
        in_specs=[blk(0), blk(1), blk(2), blk(3), pl.BlockSpec((rows, 1024), lambda i: (steps - 1 - i, 0)),
                  pl.BlockSpec((per, RET_HEADS, RET_DK, RET_DK), lambda i: (steps - 1 - i, 0, 0, 0)),
                  tab, tab, _full(decay.shape), _full(xi.shape), _full(zeta.shape), _full((1, 1024))],
        out_specs=[pl.BlockSpec((rows, 4096), lambda i: (steps - 1 - i, 0)), _full((1, 1024))],
        out_shape=[jax.ShapeDtypeStruct((L, 4096), MXU_DTYPE), jax.ShapeDtypeStruct((1, 1024), F32)],
        scratch_shapes=[pltpu.VMEM((RET_HEADS, RET_DK, RET_DK), F32)],
        sem=("arbitrary",),
    )(p, p, p, p, dy, prevs, cos, sin, decay, xi, zeta, gain)


def _sgu_mix(p_ref, gain_ref, wm_ref, bt_ref, tl):
    pu, pv, z = p_ref[:, :2048], p_ref[:, 2048:4096], p_ref[:, 4096:]
    (u, du), (v, dv) = _gelu_and_grad(pu), _gelu_and_grad(pv)
    mu = jnp.mean(v, axis=-1, keepdims=True)
    vc = v - mu
    rstd = lax.rsqrt(jnp.mean(vc * vc, axis=-1, keepdims=True) + NORM_EPS)
    vn = vc * rstd
    vg = vn * gain_ref[...]
    mask = (lax.broadcasted_iota(jnp.int32, (SGU_CHUNK, SGU_CHUNK), 0)
            >= lax.broadcasted_iota(jnp.int32, (SGU_CHUNK, SGU_CHUNK), 1))
    wms = [jnp.where(mask, wm_ref[g], 0.0) for g in range(SGU_GROUPS)]
    rows = []
    for c in range(tl // SGU_CHUNK):
        rs = slice(SGU_CHUNK * c, SGU_CHUNK * (c + 1))
        cols = []
        for g in range(SGU_GROUPS):
            gs = slice(SGU_GDIM * g, SGU_GDIM * (g + 1))
            cols.append(_mm(wms[g], vg[rs, gs]) + bt_ref[:, g:g + 1])
        rows.append(jnp.concatenate(cols, axis=1))
    s = rows[0] if len(rows) == 1 else jnp.concatenate(rows, axis=0)
    return du, dv, z, u, vn, rstd, vg, wms, mask, s


def sgu_forward(p, gain, wm, bt):
    L = p.shape[0]
    tl = min(TL_SGU, L)

    def body(p_ref, gain_ref, wm_ref, bt_ref, y_ref):
        _, _, z, u, _, _, _, _, _, s = _sgu_mix(p_ref, gain_ref, wm_ref, bt_ref, tl)
        sz, _ = _silu_and_grad(z)
        y_ref[...] = (u * s * sz).astype(y_ref.dtype)

    return pl.pallas_call(
        body, name="sgu_forward", grid=(L // tl,),
        in_specs=[pl.BlockSpec((tl, ODD_IN), lambda i: (i, 0)), _full((1, 2048)), _full(wm.shape), _full(bt.shape)],
        out_specs=pl.BlockSpec((tl, 2048), lambda i: (i, 0)),
        out_shape=jax.ShapeDtypeStruct((L, 2048), MXU_DTYPE),
        compiler_params=_cparams(("arbitrary",)),
    )(p, gain, wm, bt)


def sgu_backward(p, dx, w_out, gain, wm, bt, plan=None):
    L = p.shape[0]
    tl = min(TL_SGU, L)

    def body(p_ref, dx_ref, w_ref, gain_ref, wm_ref, bt_ref, dp_ref, dgain_ref, dwm_ref, dbt_ref):
        @pl.when(pl.program_id(0) == 0)
        def _():
            dgain_ref[...] = jnp.zeros_like(dgain_ref)
            dwm_ref[...] = jnp.zeros_like(dwm_ref)
            dbt_ref[...] = jnp.zeros_like(dbt_ref)

        gu, gv, z, u, vn, rstd, vg, wms, mask, s = _sgu_mix(p_ref, gain_ref, wm_ref, bt_ref, tl)
        sz, dsz = _silu_and_grad(z)
        dyv = _mm_nt(dx_ref[...], w_ref[...])
        dp_ref[:, 4096:] = (dyv * u * s * dsz).astype(dp_ref.dtype)
        dsg = dyv * sz
        dp_ref[:, :2048] = (dsg * s * gu).astype(dp_ref.dtype)
        ds = dsg * u
        rows = []
        dbs = [jnp.zeros((SGU_CHUNK, 1), F32) for _ in range(SGU_GROUPS)]
        for c in range(tl // SGU_CHUNK):
            rs = slice(SGU_CHUNK * c, SGU_CHUNK * (c + 1))
            cols = []
            for g in range(SGU_GROUPS):
                gs = slice(SGU_GDIM * g, SGU_GDIM * (g + 1))
                dsg_c = ds[rs, gs]
                dbs[g] = dbs[g] + jnp.sum(dsg_c, axis=1, keepdims=True)
                dwm_ref[g] += jnp.where(mask, _mm_nt(dsg_c, vg[rs, gs]), 0.0)
                cols.append(_mm_tn(wms[g], dsg_c))
            rows.append(jnp.concatenate(cols, axis=1))
        dbt_ref[...] += jnp.concatenate(dbs, axis=1)
        dvg = rows[0] if len(rows) == 1 else jnp.concatenate(rows, axis=0)
        dgain_ref[...] += jnp.sum(dvg * vn, axis=0, keepdims=True)
        dvn = dvg * gain_ref[...]
        dv = rstd * (dvn - jnp.mean(dvn, axis=-1, keepdims=True) - vn * jnp.mean(dvn * vn, axis=-1, keepdims=True))
        dp_ref[:, 2048:4096] = (dv * gv).astype(dp_ref.dtype)

    return _call(
        body, plan, name="sgu_backward", grid=(L // tl,),
        in_specs=[pl.BlockSpec((tl, ODD_IN), lambda i: (i, 0)), pl.BlockSpec((tl, dx.shape[1]), lambda i: (i, 0)),
                  _full(w_out.shape), _full((1, 2048)), _full(wm.shape), _full(bt.shape)],
        out_specs=[pl.BlockSpec((tl, ODD_IN), lambda i: (i, 0)), _full((1, 2048)), _full(wm.shape), _full(bt.shape)],
        out_shape=[jax.ShapeDtypeStruct((L, ODD_IN), MXU_DTYPE), jax.ShapeDtypeStruct((1, 2048), F32),
                   jax.ShapeDtypeStruct(wm.shape, F32), jax.ShapeDtypeStruct(bt.shape, F32)],
        sem=("arbitrary",),
    )(p, dx, w_out, gain, wm, bt)


def cast_shards(mats):
    n = len(mats)
    steps = 8

    def body(*refs):
        for p in range(n):
            refs[n + p][...] = refs[p][...].astype(MXU_DTYPE)

    specs = [pl.BlockSpec((m.shape[0] // steps, m.shape[1]), lambda i: (i, 0)) for m in mats]
    return pl.pallas_call(
        body, name="cast_shards", grid=(steps,), in_specs=specs, out_specs=specs,
        out_shape=[jax.ShapeDtypeStruct(m.shape, MXU_DTYPE) for m in mats],
        compiler_params=_cparams(("arbitrary",)),
    )(*mats)


def local_grads(x, tgt, w):
    L = x.shape[0]
    ne, gf = w["norm_even"], w["final_norm"].reshape(1, D_MODEL)
    sh = dict(zip(MATRICES, cast_shards([w[n][0] for n in MATRICES])))
    lam_re, lam_im = w["s5_lam_re"][0], w["s5_lam_im"][0]
    log_dt = w["s5_log_dt"].reshape(S5_GROUPS, 1)
    bt_re = jnp.transpose(w["s5_b_re"][0], (2, 0, 1))
    bt_im = jnp.transpose(w["s5_b_im"][0], (2, 0, 1))
    c_re, c_im = w["s5_c_re"][0], w["s5_c_im"][0]
    wm = w["sgu_w_spatial"][0]
    bt = jnp.transpose(w["sgu_b_spatial"][0])

    tl5 = min(TL_S5, L)
    ab_re, ab_im, bb_re, bb_im, at_re, at_im = s5_params_fwd(lam_re, lam_im, log_dt, bt_re, bt_im, tl5 // 8)
    atab = jnp.stack([ab_re.reshape(S5_LANES), ab_im.reshape(S5_LANES),
                      at_re.reshape(S5_LANES), at_im.reshape(S5_LANES)])
    wbd = jnp.concatenate([_block_diag(jnp.transpose(bb_re, (1, 0, 2)), True),
                           _block_diag(jnp.transpose(bb_im, (1, 0, 2)), True)], axis=2).astype(MXU_DTYPE)
    cre = _block_diag(jnp.transpose(c_re, (0, 2, 1)), True).astype(MXU_DTYPE)
    cim = _block_diag(jnp.transpose(c_im, (0, 2, 1)), True).astype(MXU_DTYPE)
    cos, sin = _rope_tables(L)

    s5_cols = 2 * S5_WIDTH
    me = (2 * lax.axis_index("x") + lax.axis_index("y")).astype(jnp.int32)
    xs = stream_order(x, tl5)
    slab = lambda d: jnp.stack([me ^ d])
    zero = jnp.zeros((1,), jnp.int32)
    shards = [sh["w_in_even"][None]]
    (p1, h0s, h0), (got,) = even_in_slabs(x, xs, ne, shards[0], slab(0), zero, "even_in_0",
                                          plan=gather_plan([sh["w_in_even"]], only=1))
    for d in (1, 2, 3):
        shards.append(got)
        plan = gather_plan([sh["w_in_even"]], only=d + 1) if d < 3 else gather_plan([sh["s5_w_glu"]])
        (p1,), (got,) = even_in_slabs(x, xs, ne, shards[d], slab(d), zero, "even_in_%d" % d, p_in=p1, plan=plan)
    w_glu = got
    by_xor = jnp.concatenate(shards)
    w_in_e = [lax.dynamic_index_in_dim(by_xor, me ^ j, 0, keepdims=False) for j in range(N_CHIPS)]
    w_s5 = jnp.concatenate([w_in_e[0], w_in_e[1][:, :s5_cols - EVEN_IN // N_CHIPS]], axis=1)
    w_kvzq = jnp.concatenate([w_in_e[2], w_in_e[3], w_in_e[1][:, s5_cols - EVEN_IN // N_CHIPS:]], axis=1)
    w_glu = w_glu.reshape(S5_WIDTH, S5_WIDTH)
    (ya, st_re, st_im, sv_re, sv_im), (w_out_e, w_in_o, w_out_o, no, sg_gain) = s5_forward(
        p1, wbd, cre, cim, atab, w["s5_d"], w_glu, w["s5_b_glu"],
        gather_plan([sh["w_out_even"], sh["w_in_odd"], sh["w_out_odd"], w["norm_odd"], w["sgu_norm_gain"]]))
    w_out_e = w_out_e.reshape(2 * S5_WIDTH, D_MODEL)
    w_out_o = w_out_o.reshape(SGU_WIDTH, D_MODEL)
    no, sg_gain = no.reshape(1, D_MODEL), sg_gain.reshape(1, SGU_WIDTH)
    yb, prevs = retention_forward(p1, cos, sin, w["ret_gn_gain"])
    ya = token_order(ya, tl5)
    x1 = matmul_residual([ya, yb], w_out_e, x, "even_out")
    (p2, h1), _ = norm_matmul(x1, no, w_in_o, "odd_in")
    y2 = sgu_forward(p2, sg_gain, wm, bt)
    dx2, loss, dgf = out_proj_loss(y2, w_out_o, x1, gf, tgt, "odd_out_loss")

    g, landed = {}, {}
    shard_major = lambda a, n: a.reshape((N_CHIPS,) + w[n].shape[1:])
    g_w_out_o = out_proj_dw(dx2, y2, "odd_out_dw")
    (dp2, g["sgu_norm_gain"], dwm, dbt), (landed["w_out_odd"],) = sgu_backward(
        p2, dx2, w_out_o, sg_gain, wm, bt, reduce_plan([shard_major(g_w_out_o, "w_out_odd")]))
    g_w_in_o, _ = in_proj_bwd_dw(h1, dp2, "odd_in_dw", ODD_IN // N_CHIPS)
    (dx1, g["norm_odd"]), _ = in_proj_bwd_dx(x1, no, [dp2], [w_in_o], dx2, "odd_in_dx")
    dya, dyb, g_w_out_e = out_proj_bwd(dx1, w_out_e, [ya, yb], "even_out_bwd")
    ((dpa, dwbd, dcre, dcim, dab_re, dab_im, g["s5_d"], g_w_glu, g["s5_b_glu"]),
     (landed["w_in_odd"], landed["w_out_even"])) = s5_backward(
        p1, stream_order(dya, tl5), st_re, st_im, sv_re, sv_im, wbd, cre, cim, atab, w["s5_d"], w_glu,
        w["s5_b_glu"], reduce_plan([g_w_in_o, shard_major(g_w_out_e, "w_out_even")]))

    dbb_re = jnp.transpose(_block_diag_extract(dwbd[:, :, :512], S5_GROUP, S5_STATE), (1, 0, 2))
    dbb_im = jnp.transpose(_block_diag_extract(dwbd[:, :, 512:], S5_GROUP, S5_STATE), (1, 0, 2))
    dlr, dli, ddt, dbt_re, dbt_im = s5_params_bwd(
        lam_re, lam_im, log_dt, bt_re, bt_im, dab_re.reshape(8, S5_GROUPS, S5_STATE),
        dab_im.reshape(8, S5_GROUPS, S5_STATE), dbb_re, dbb_im)
    g["s5_lam_re"], g["s5_lam_im"] = dlr[None], dli[None]
    g["s5_log_dt"] = ddt.reshape(1, S5_GROUPS)
    g["s5_b_re"], g["s5_b_im"] = dbt_re, dbt_im
    g["s5_c_re"] = _block_diag_extract(dcre, S5_GROUP, S5_STATE)[None]
    g["s5_c_im"] = _block_diag_extract(dcim, S5_GROUP, S5_STATE)[None]
    g["sgu_w_spatial"] = dwm[None]
    g["sgu_b_spatial"] = jnp.transpose(dbt)[None]
    g["final_norm"] = dgf.reshape(D_MODEL)
    g["loss"] = loss

    big_small = ("s5_b_re", "s5_b_im")
    mid_small = ("s5_c_re",)
    (dpb, g["ret_gn_gain"]), recv = retention_backward(
        p1, dyb, prevs, cos, sin, w["ret_gn_gain"],
        reduce_plan([shard_major(g_w_glu, "s5_w_glu")], [g[n] for n in big_small]))
    landed.update(zip(("s5_w_glu",) + big_small, recv))
    done = tuple(n for n in MATRICES if n != "w_in_even")
    part = {n: sum_slabs(landed[n], "sum_" + n) for n in done}
    g_w_in_e, recv = in_proj_bwd_dw(h0s, dpa, "even_in_dw_s5", 512, dtype=MXU_DTYPE,
                                    plan=_SiblingPlan([part[n] for n in done]))
    other = dict(zip(done, recv))
    small = tuple(n for n in SMALL if n != "norm_even" and n not in big_small + mid_small) + ("loss",)
    wb = EVEN_IN // N_CHIPS
    g_w_in_e, recv = in_proj_bwd_dw(h0, dpb, "even_in_dw_q", 512, first=s5_cols // 512, into=g_w_in_e,
                                    dtype=MXU_DTYPE, dp_first=2 * wb // 512, count=RET_HEADS * RET_DK // 512,
                                    plan=reduce_plan([], [g[n] for n in mid_small]))
    landed.update(zip(mid_small, recv))
    g_w_in_e, recv = in_proj_bwd_dw(h0, dpb, "even_in_dw_kvz", wb, first=2, into=g_w_in_e, dtype=MXU_DTYPE,
                                    count=2, plan=reduce_plan([], [g[n] for n in small]))
    landed.update(zip(small, recv))
    (dx0, g["norm_even"]), (landed["w_in_even"],) = in_proj_bwd_dx(
        x, ne, [token_order(dpa, tl5), dpb], [w_s5, w_kvzq], dx1, "even_in_dx", reduce_plan([g_w_in_e]))
    (landed["norm_even"],) = run_plan(reduce_plan([], [g["norm_even"]]), "exchange_norm_even")
    return dx0, landed, part, other


def _row_block(rows):
    return 128 if rows % 128 == 0 else rows


def sum_slabs(r, name):
    _, R, C = r.shape
    tr = _row_block(R)

    def body(r_ref, o_ref):
        a, b, c, d = (r_ref[k].astype(F32) for k in range(N_CHIPS))
        o_ref[...] = (a + b) + (c + d)

    return pl.pallas_call(
        body, name=name, grid=(R // tr,),
        in_specs=[pl.BlockSpec((N_CHIPS, tr, C), lambda i: (0, i, 0))],
        out_specs=pl.BlockSpec((tr, C), lambda i: (i, 0)),
        out_shape=jax.ShapeDtypeStruct((R, C), F32),
        compiler_params=_cparams(("arbitrary",)),
    )(r)


def _adam(w, m, v, g):
    mn = ADAM_B1 * m + (1.0 - ADAM_B1) * g
    vn = ADAM_B2 * v + (1.0 - ADAM_B2) * (g * g)
    m_hat = mn / (1.0 - ADAM_B1 ** ADAM_STEP)
    v_hat = vn / (1.0 - ADAM_B2 ** ADAM_STEP)
    return -ADAM_LR * (m_hat / (jnp.sqrt(v_hat) + ADAM_EPS) + ADAM_WD * w), mn, vn


def adam_update(w, m, v, ga, gb, name):
    R, C = w.shape
    tr = _row_block(R)

    def body(w_ref, m_ref, v_ref, ga_ref, gb_ref, g_out, d_out, m_out, v_out):
        g = ga_ref[...] + gb_ref[...]
        g_out[...] = g
        d_out[...], m_out[...], v_out[...] = _adam(w_ref[...], m_ref[...], v_ref[...], g)

    blk = pl.BlockSpec((tr, C), lambda i: (i, 0))
    return pl.pallas_call(
        body, name=name, grid=(R // tr,),
        in_specs=[blk] * 5, out_specs=[blk] * 4,
        out_shape=[jax.ShapeDtypeStruct((R, C), F32)] * 4,
        compiler_params=_cparams(("arbitrary",)),
    )(w, m, v, ga, gb)


WIDE_ROWS = ("s5_b_re", "s5_b_im")


def sum_small(landed):
    def body(*refs):
        k = len(refs) // 2
        for i in range(k):
            r = refs[i]
            refs[k + i][...] = (r[0] + r[1]) + (r[2] + r[3])

    names = list(landed)
    res = pl.pallas_call(
        body, name="sum_small", out_shape=[jax.ShapeDtypeStruct(landed[n].shape[1:], F32) for n in names],
        compiler_params=pltpu.CompilerParams(vmem_limit_bytes=VMEM_LIMIT),
    )(*[landed[n] for n in names])
    return dict(zip(names, res))


def adam_small(names, w, m, v, ga, gb):
    def body(*refs):
        k = len(refs) // 9
        me = 2 * lax.axis_index("x") + lax.axis_index("y")
        for i in range(k):
            w_ref, m_ref, v_ref, ga_ref, gb_ref = refs[i], refs[k + i], refs[2 * k + i], refs[3 * k + i], refs[4 * k + i]
            size = w_ref.shape[-1]
            if ga_ref.shape != w_ref.shape:
                part = pl.ds(pl.multiple_of(me * size, LANES), size)
                g = ga_ref[:, part] + gb_ref[:, part]
            else:
                g = ga_ref[...] + gb_ref[...]
            refs[5 * k + i][...] = g
            refs[6 * k + i][...], refs[7 * k + i][...], refs[8 * k + i][...] = _adam(w_ref[...], m_ref[...], v_ref[...], g)

    ins = [d[n] for d in (w, m, v, ga, gb) for n in names]
    outs = [jax.ShapeDtypeStruct(w[n].shape, F32) for _ in range(4) for n in names]
    res = pl.pallas_call(body, name="adam_small", out_shape=outs,
                         compiler_params=pltpu.CompilerParams(vmem_limit_bytes=VMEM_LIMIT))(*ins)
    k = len(names)
    return [dict(zip(names, res[j * k:(j + 1) * k])) for j in range(4)]


WEIGHTS = ("norm_even", "w_in_even", "s5_lam_re", "s5_lam_im", "s5_log_dt", "s5_b_re", "s5_b_im", "s5_c_re",
           "s5_c_im", "s5_d", "s5_w_glu", "s5_b_glu", "ret_gn_gain", "w_out_even", "norm_odd", "w_in_odd",
           "sgu_norm_gain", "sgu_w_spatial", "sgu_b_spatial", "w_out_odd", "final_norm")
MATRICES = ("w_in_even", "s5_w_glu", "w_out_even", "w_in_odd", "w_out_odd")
SHARDED_VECS = ("norm_odd", "sgu_norm_gain")
REPLICATED = tuple(n for n in WEIGHTS if n not in MATRICES and n not in SHARDED_VECS)
SMALL = tuple(n for n in WEIGHTS if n not in MATRICES)
LANES = 128


def kernel(x, norm_even, w_in_even, s5_lam_re, s5_lam_im, s5_log_dt, s5_b_re, s5_b_im, s5_c_re, s5_c_im, s5_d, s5_w_glu, s5_b_glu, ret_gn_gain, w_out_even, norm_odd, w_in_odd, sgu_norm_gain, sgu_w_spatial, sgu_b_spatial, w_out_odd, final_norm, loss_target, m_norm_even, m_w_in_even, m_s5_lam_re, m_s5_lam_im, m_s5_log_dt, m_s5_b_re, m_s5_b_im, m_s5_c_re, m_s5_c_im, m_s5_d, m_s5_w_glu, m_s5_b_glu, m_ret_gn_gain, m_w_out_even, m_norm_odd, m_w_in_odd, m_sgu_norm_gain, m_sgu_w_spatial, m_sgu_b_spatial, m_w_out_odd, m_final_norm, v_norm_even, v_w_in_even, v_s5_lam_re, v_s5_lam_im, v_s5_log_dt, v_s5_b_re, v_s5_b_im, v_s5_c_re, v_s5_c_im, v_s5_d, v_s5_w_glu, v_s5_b_glu, v_ret_gn_gain, v_w_out_even, v_norm_odd, v_w_in_odd, v_sgu_norm_gain, v_sgu_w_spatial, v_sgu_b_spatial, v_w_out_odd, v_final_norm):
    w = dict(norm_even=norm_even, w_in_even=w_in_even, s5_lam_re=s5_lam_re, s5_lam_im=s5_lam_im, s5_log_dt=s5_log_dt, s5_b_re=s5_b_re, s5_b_im=s5_b_im, s5_c_re=s5_c_re, s5_c_im=s5_c_im, s5_d=s5_d, s5_w_glu=s5_w_glu, s5_b_glu=s5_b_glu, ret_gn_gain=ret_gn_gain, w_out_even=w_out_even, norm_odd=norm_odd, w_in_odd=w_in_odd, sgu_norm_gain=sgu_norm_gain, sgu_w_spatial=sgu_w_spatial, sgu_b_spatial=sgu_b_spatial, w_out_odd=w_out_odd, final_norm=final_norm)
    m = dict(norm_even=m_norm_even, w_in_even=m_w_in_even, s5_lam_re=m_s5_lam_re, s5_lam_im=m_s5_lam_im, s5_log_dt=m_s5_log_dt, s5_b_re=m_s5_b_re, s5_b_im=m_s5_b_im, s5_c_re=m_s5_c_re, s5_c_im=m_s5_c_im, s5_d=m_s5_d, s5_w_glu=m_s5_w_glu, s5_b_glu=m_s5_b_glu, ret_gn_gain=m_ret_gn_gain, w_out_even=m_w_out_even, norm_odd=m_norm_odd, w_in_odd=m_w_in_odd, sgu_norm_gain=m_sgu_norm_gain, sgu_w_spatial=m_sgu_w_spatial, sgu_b_spatial=m_sgu_b_spatial, w_out_odd=m_w_out_odd, final_norm=m_final_norm)
    v = dict(norm_even=v_norm_even, w_in_even=v_w_in_even, s5_lam_re=v_s5_lam_re, s5_lam_im=v_s5_lam_im, s5_log_dt=v_s5_log_dt, s5_b_re=v_s5_b_re, s5_b_im=v_s5_b_im, s5_c_re=v_s5_c_re, s5_c_im=v_s5_c_im, s5_d=v_s5_d, s5_w_glu=v_s5_w_glu, s5_b_glu=v_s5_b_glu, ret_gn_gain=v_ret_gn_gain, w_out_even=v_w_out_even, norm_odd=v_norm_odd, w_in_odd=v_w_in_odd, sgu_norm_gain=v_sgu_norm_gain, sgu_w_spatial=v_sgu_w_spatial, sgu_b_spatial=v_sgu_b_spatial, w_out_odd=v_w_out_odd, final_norm=v_final_norm)

    grad_x, landed, part, other = local_grads(x[0], loss_target[0], w)

    small = SMALL + ("loss",)
    part["w_in_even"] = sum_slabs(landed["w_in_even"], "sum_w_in_even")
    part.update(sum_small({n: landed[n] for n in small}))
    names = ("w_in_even",) + small
    other.update(zip(names, run_plan(_SiblingPlan([part[n] for n in names]), "sibling_exchange")))

    wt, mt, vt = dict(w), dict(m), dict(v)
    for n in WIDE_ROWS:
        wt[n], mt[n], vt[n] = (jnp.transpose(a[n][0], (2, 0, 1)) for a in (w, m, v))
    out_g, out_d, out_m, out_v = adam_small(SMALL, wt, mt, vt, part, other)
    for n in WIDE_ROWS:
        for out in (out_g, out_d, out_m, out_v):
            out[n] = jnp.transpose(out[n], (1, 2, 0))[None]
    for n in MATRICES:
        res = adam_update(w[n][0], m[n][0], v[n][0], part[n], other[n], "adam_" + n)
        out_g[n], out_d[n], out_m[n], out_v[n] = (r[None] for r in res)
    total_loss = (part["loss"] + other["loss"])[0, 0]

    return (total_loss, grad_x[None], *[out_g[n] for n in WEIGHTS], *[out_d[n] for n in WEIGHTS],
            *[out_m[n] for n in WEIGHTS], *[out_v[n] for n in WEIGHTS])
```

```python
import functools
import math

import numpy as np
import jax
import jax.numpy as jnp
from jax import lax
from jax.experimental import pallas as pl
from jax.experimental.pallas import tpu as pltpu

F32 = jnp.float32
MXU_DTYPE = jnp.bfloat16
NORM_EPS = 1e-6
D_MODEL = 1024
S5_WIDTH = 1024
S5_GROUP = 16
S5_GROUPS = 64
S5_STATE = 64
S5_LANES = S5_GROUPS * S5_STATE
S5_KBLK = 8
RET_HEADS = 4
RET_DK = 256
RET_CHUNK = 128
ROPE_BASE = 10000.0
SGU_WIDTH = 2048
SGU_GROUPS = 4
SGU_GDIM = 512
SGU_CHUNK = 128
EVEN_IN = 6144
ODD_IN = 6144
ADAM_LR = 0.001
ADAM_B1 = 0.9
ADAM_B2 = 0.999
ADAM_EPS = 1e-08
ADAM_WD = 0.01
ADAM_STEP = 10
N_CHIPS = 4
VMEM_LIMIT = 56 * 1024 * 1024

TL_PROJ = 512
TL_DW = 1024
TL_S5 = 256
TL_SGU = 256


def _cparams(sem, **kw):
    return pltpu.CompilerParams(dimension_semantics=sem, vmem_limit_bytes=VMEM_LIMIT, **kw)


def _mm(a, b):
    return jnp.dot(a.astype(MXU_DTYPE), b.astype(MXU_DTYPE), preferred_element_type=F32)


def _mm_nt(a, b):
    return lax.dot_general(a.astype(MXU_DTYPE), b.astype(MXU_DTYPE),
                           (((1,), (1,)), ((), ())), preferred_element_type=F32)


def _mm_tn(a, b):
    return lax.dot_general(a.astype(MXU_DTYPE), b.astype(MXU_DTYPE),
                           (((0,), (0,)), ((), ())), preferred_element_type=F32)


_GELU_C = math.sqrt(2.0 / math.pi)


def _gelu_parts(x):
    x2 = x * x
    th = jnp.tanh(x * (_GELU_C + (_GELU_C * 0.044715) * x2))
    hx = 0.5 * x
    return hx + hx * th, th, x2, hx


def _gelu(x):
    return _gelu_parts(x)[0]


def _gelu_and_grad(x):
    g, th, x2, hx = _gelu_parts(x)
    return g, (0.5 + 0.5 * th) + hx * (1.0 - th * th) * (_GELU_C + (3.0 * _GELU_C * 0.044715) * x2)


def _gelu_grad(x):
    return _gelu_and_grad(x)[1]


def _sigmoid(x):
    return 1.0 / (1.0 + jnp.exp(-x))


def _silu_and_grad(x):
    s = _sigmoid(x)
    return x * s, s * (1.0 + x * (1.0 - s))


def _rms(x):
    return lax.rsqrt(jnp.mean(x * x, axis=-1, keepdims=True) + NORM_EPS)


def _full(shape):
    nd = len(shape)
    return pl.BlockSpec(shape, lambda *_: (0,) * nd)


MESH = pl.DeviceIdType.MESH
ANY = pl.BlockSpec(memory_space=pl.ANY)


def _place():
    return lax.axis_index("x"), lax.axis_index("y"), lax.axis_index("c")


def _chip_peer(x, y, c, d):
    return (1 - x if d >= 2 else x, 1 - y if d % 2 else y, c)


class _Plan:
    def __init__(self, inputs, out_shape, build):
        self.inputs, self.out_shape, self._build = list(inputs), list(out_shape), build
        n = len(self.inputs)
        self.sems = [pltpu.SemaphoreType.DMA((n, 3)), pltpu.SemaphoreType.DMA((n, 3)), pltpu.SemaphoreType.DMA((n,))]

    def start(self, in_refs, out_refs, sems):
        send, recv, local = self._build(in_refs, out_refs, sems)
        for p in range(len(self.inputs)):
            local[p].start()
            for cp in send[p]:
                cp.start()

    def wait(self, in_refs, out_refs, sems):
        send, recv, local = self._build(in_refs, out_refs, sems)
        for p in range(len(self.inputs)):
            for cp in recv[p]:
                cp.wait_recv()
        for p in range(len(self.inputs)):
            for cp in send[p]:
                cp.wait_send()
            local[p].wait()


class _GatherPlan:
    def __init__(self, shards, only=None):
        n = len(shards)
        self.n, self.only = n, only
        self.peers = (1, 2, 3) if only is None else (only,)
        self.inputs = list(shards)
        slabs = N_CHIPS if only is None else 1
        self.out_shape = [jax.ShapeDtypeStruct((slabs,) + s.shape, s.dtype) for s in shards]
        self.halved = [s.shape[0] % 32 == 0 for s in shards]
        self.sems = [pltpu.SemaphoreType.DMA((n, 3)) for _ in range(4)] + [pltpu.SemaphoreType.DMA((n,))]

    def _copies(self, in_refs, out_refs, sems):
        ici_s, ici_r, d2d_s, d2d_r, loc = sems
        x, y, c = _place()
        me = 2 * x + y

        def rows(p, core):
            if not self.halved[p]:
                return slice(None)
            half = self.inputs[p].shape[0] // 2
            return pl.ds(pl.multiple_of(core * half, 16), half)

        def slab(chip):
            return chip if self.only is None else 0

        def ici(p, d, chip, core):
            return pltpu.make_async_remote_copy(
                src_ref=in_refs[p].at[rows(p, core)], dst_ref=out_refs[p].at[slab(chip), rows(p, core)],
                send_sem=ici_s.at[p, d - 1], recv_sem=ici_r.at[p, d - 1],
                device_id=_chip_peer(x, y, c, d), device_id_type=MESH)

        def d2d(p, d, core):
            part = out_refs[p].at[slab(me ^ d), rows(p, core)]
            return pltpu.make_async_remote_copy(
                src_ref=part, dst_ref=part, send_sem=d2d_s.at[p, d - 1], recv_sem=d2d_r.at[p, d - 1],
                device_id=(x, y, 1 - c), device_id_type=MESH)

        local = [pltpu.make_async_copy(in_refs[p], out_refs[p].at[slab(me)], loc.at[p]) for p in range(self.n)]
        return me, c, ici, d2d, local

    def start(self, in_refs, out_refs, sems):
        me, c, ici, d2d, local = self._copies(in_refs, out_refs, sems)
        for p in range(self.n):
            if self.only is None:
                local[p].start()
            for d in self.peers:
                ici(p, d, me, c).start()

    def wait(self, in_refs, out_refs, sems):
        me, c, ici, d2d, local = self._copies(in_refs, out_refs, sems)
        for p in range(self.n):
            for d in self.peers:
                ici(p, d, me ^ d, c).wait_recv()
                if self.halved[p]:
                    d2d(p, d, c).start()
        for p in range(self.n):
            for d in self.peers:
                if self.halved[p]:
                    d2d(p, d, 1 - c).wait_recv()
                    d2d(p, d, c).wait_send()
                ici(p, d, me, c).wait_send()
            if self.only is None:
                local[p].wait()


def gather_plan(shards, only=None):
    return _GatherPlan(shards, only)


def reduce_plan(shards, whole=()):
    n_s = len(shards)

    def build(in_refs, out_refs, sems):
        send_sems, recv_sems, loc_sems = sems
        x, y, c = _place()
        me = 2 * x + y

        def src(p, slab):
            return in_refs[p].at[slab] if p < n_s else in_refs[p]

        def remote(p, d):
            return pltpu.make_async_remote_copy(
                src_ref=src(p, me ^ d), dst_ref=out_refs[p].at[d], send_sem=send_sems.at[p, d - 1],
                recv_sem=recv_sems.at[p, d - 1], device_id=_chip_peer(x, y, c, d), device_id_type=MESH)

        n = len(in_refs)
        send = [[remote(p, d) for d in (1, 2, 3)] for p in range(n)]
        local = [pltpu.make_async_copy(src(p, me), out_refs[p].at[0], loc_sems.at[p]) for p in range(n)]
        return send, send, local

    outs = [jax.ShapeDtypeStruct(s.shape, s.dtype) for s in shards]
    outs += [jax.ShapeDtypeStruct((N_CHIPS,) + a.shape, a.dtype) for a in whole]
    return _Plan(list(shards) + list(whole), outs, build)


class _SiblingPlan:
    def __init__(self, arrs):
        self.inputs = list(arrs)
        self.out_shape = [jax.ShapeDtypeStruct(a.shape, a.dtype) for a in arrs]
        n = len(arrs)
        self.sems = [pltpu.SemaphoreType.DMA((n,)), pltpu.SemaphoreType.DMA((n,))]

    def _copies(self, in_refs, out_refs, sems):
        x, y, c = _place()
        return [pltpu.make_async_remote_copy(
            src_ref=in_refs[p], dst_ref=out_refs[p], send_sem=sems[0].at[p], recv_sem=sems[1].at[p],
            device_id=(x, y, 1 - c), device_id_type=MESH) for p in range(len(self.inputs))]

    def start(self, in_refs, out_refs, sems):
        for cp in self._copies(in_refs, out_refs, sems):
            cp.start()

    def wait(self, in_refs, out_refs, sems):
        copies = self._copies(in_refs, out_refs, sems)
        for cp in copies:
            cp.wait_recv()
        for cp in copies:
            cp.wait_send()


def run_plan(plan, name):
    n = len(plan.inputs)

    def body(*refs):
        plan.start(refs[:n], refs[n:2 * n], refs[2 * n:])
        plan.wait(refs[:n], refs[n:2 * n], refs[2 * n:])

    return pl.pallas_call(body, name=name, in_specs=[ANY] * n, out_specs=[ANY] * n, out_shape=plan.out_shape,
                          scratch_shapes=plan.sems)(*plan.inputs)


def _call(body, plan, *, name, grid, in_specs, out_specs, out_shape, sem, scratch_shapes=(), aliases=None,
          n_prefetch=0):
    aliases = {} if aliases is None else aliases
    single = not isinstance(out_shape, (list, tuple))
    out_specs = [out_specs] if single else list(out_specs)
    out_shape = [out_shape] if single else list(out_shape)
    n_in, n_out, n_scr = len(in_specs), len(out_specs), len(scratch_shapes)
    ci = 0 if plan is None else len(plan.inputs)
    co = 0 if plan is None else len(plan.out_shape)

    def hosted(*refs):
        pre, refs = refs[:n_prefetch], refs[n_prefetch:]
        ins, cins = refs[:n_in], refs[n_in:n_in + ci]
        k = n_in + ci
        outs, couts = refs[k:k + n_out], refs[k + n_out:k + n_out + co]
        k += n_out + co
        scr, sems = refs[k:k + n_scr], refs[k + n_scr:]
        ids = [pl.program_id(a) for a in range(len(grid))]
        first = functools.reduce(jnp.logical_and, [i == 0 for i in ids])
        last = functools.reduce(jnp.logical_and, [i == g - 1 for i, g in zip(ids, grid)])

        @pl.when(first)
        def _():
            plan.start(cins, couts, sems)

        body(*pre, *ins, *outs, *scr)

        @pl.when(last)
        def _():
            plan.wait(cins, couts, sems)

    def run(*args):
        hosting = plan is not None
        spec = pltpu.PrefetchScalarGridSpec(
            num_scalar_prefetch=n_prefetch, grid=grid,
            in_specs=list(in_specs) + ([ANY] * ci if hosting else []),
            out_specs=out_specs + ([ANY] * co if hosting else []),
            scratch_shapes=list(scratch_shapes) + (plan.sems if hosting else []))
        res = pl.pallas_call(hosted if hosting else body, name=name, grid_spec=spec,
                             out_shape=out_shape + (plan.out_shape if hosting else []),
                             input_output_aliases=aliases, compiler_params=_cparams(sem),
                             )(*args, *(plan.inputs if hosting else []))
        return (res[0] if single else res[:n_out]), list(res[n_out:])

    return run


def norm_matmul(x, g, w, name, plan=None, tn=None):
    L, D = x.shape
    tl = min(TL_DW, L)
    if w.ndim == 3:
        nt, _, tn = w.shape
        w_spec = pl.BlockSpec((1, D, tn), lambda i, n: (n, 0, 0))
    else:
        nt = w.shape[1] // tn
        w_spec = pl.BlockSpec((D, tn), lambda i, n: (0, n))

    def body(x_ref, g_ref, w_ref, o_ref, h_ref):
        xv = x_ref[...]
        h = (xv * _rms(xv) * g_ref[...]).astype(h_ref.dtype)
        h_ref[...] = h
        o_ref[...] = _mm(h, w_ref[0] if w.ndim == 3 else w_ref[...])

    return _call(
        body, plan, name=name, grid=(L // tl, nt),
        in_specs=[pl.BlockSpec((tl, D), lambda i, n: (i, 0)), _full((1, D)), w_spec],
        out_specs=[pl.BlockSpec((tl, tn), lambda i, n: (i, n)), pl.BlockSpec((tl, D), lambda i, n: (i, 0))],
        out_shape=[jax.ShapeDtypeStruct((L, nt * tn), F32), jax.ShapeDtypeStruct((L, D), MXU_DTYPE)],
        sem=("arbitrary", "arbitrary"),
    )(x, g, w)


def even_in_slabs(x, xs, g, w, slabs, wsel, name, p_in=None, plan=None):
    L, D = x.shape
    tl = min(TL_DW, L)
    wb = EVEN_IN // N_CHIPS
    n = slabs.shape[0]
    s5_cols = 2 * S5_WIDTH - wb
    first = p_in is None

    def body(slabs_ref, wsel_ref, xs_ref, x_ref, g_ref, w_ref, *rest):
        o_ref = rest[-3] if first else rest[-1]
        j = slabs_ref[pl.program_id(0)]
        hs = (xs_ref[...] * _rms(xs_ref[...]) * g_ref[...]).astype(MXU_DTYPE)
        h = (x_ref[...] * _rms(x_ref[...]) * g_ref[...]).astype(MXU_DTYPE)
        if first:
            rest[-2][...] = hs
            rest[-1][...] = h
        o_ref[:, :s5_cols] = _mm(jnp.where(j <= 1, hs, h), w_ref[0, :, :s5_cols])
        o_ref[:, s5_cols:] = _mm(jnp.where(j == 0, hs, h), w_ref[0, :, s5_cols:])

    row = pl.BlockSpec((tl, D), lambda s, i, slabs_ref, wsel_ref: (i, 0))
    in_specs = [row if first else
                pl.BlockSpec((tl, D), lambda s, i, slabs_ref, wsel_ref: (jnp.where(slabs_ref[s] <= 1, i, 0), 0)),
                row if first else
                pl.BlockSpec((tl, D), lambda s, i, slabs_ref, wsel_ref: (jnp.where(slabs_ref[s] >= 1, i, 0), 0)),
                pl.BlockSpec((1, D), lambda s, i, slabs_ref, wsel_ref: (0, 0)),
                pl.BlockSpec((1, D, wb), lambda s, i, slabs_ref, wsel_ref: (wsel_ref[s], 0, 0))]
    out_specs = [pl.BlockSpec((tl, wb), lambda s, i, slabs_ref, wsel_ref: (i, slabs_ref[s]))]
    out_shape = [jax.ShapeDtypeStruct((L, EVEN_IN), F32)]
    args = [slabs, wsel, xs, x, g, w]
    if first:
        out_specs += [row, row]
        out_shape += [jax.ShapeDtypeStruct((L, D), MXU_DTYPE)] * 2
    else:
        in_specs.append(ANY)
        args.append(p_in)
    return _call(body, plan, name=name, grid=(n, L // tl), in_specs=in_specs, out_specs=out_specs,
                 out_shape=out_shape, sem=("arbitrary", "arbitrary"), n_prefetch=2,
                 aliases={} if first else {6: 0})(*args)


def matmul_residual(ys, w, x, name):
    L, D = x.shape
    tl = min(TL_PROJ, L)
    n = len(ys)
    offs = np.cumsum([0] + [y.shape[1] for y in ys])

    def body(*refs):
        y_refs, w_ref, x_ref, o_ref = refs[:n], refs[n], refs[n + 1], refs[n + 2]
        acc = x_ref[...]
        for k in range(n):
            acc = acc + _mm(y_refs[k][...], w_ref[offs[k]:offs[k + 1], :])
        o_ref[...] = acc

    return pl.pallas_call(
        body, name=name, grid=(L // tl,),
        in_specs=[pl.BlockSpec((tl, y.shape[1]), lambda i: (i, 0)) for y in ys]
        + [_full(w.shape), pl.BlockSpec((tl, D), lambda i: (i, 0))],
        out_specs=pl.BlockSpec((tl, D), lambda i: (i, 0)),
        out_shape=jax.ShapeDtypeStruct((L, D), F32),
        compiler_params=_cparams(("arbitrary",)),
    )(*ys, w, x)


def out_proj_loss(y, w, x, gf, tgt, name):
    L, K = y.shape
    D = w.shape[1]
    tl = min(TL_PROJ, L)

    def body(y_ref, w_ref, x_ref, gf_ref, t_ref, dx_ref, loss_ref, dg_ref):
        @pl.when(pl.program_id(0) == 0)
        def _():
            loss_ref[...] = jnp.zeros_like(loss_ref)
            dg_ref[...] = jnp.zeros_like(dg_ref)

        x2 = x_ref[...] + _mm(y_ref[...], w_ref[...])
        r = _rms(x2)
        xn = x2 * r
        e = xn * gf_ref[...] - t_ref[...]
        loss_ref[...] += (0.5 / D) * jnp.sum(e * e)
        dout = e * (1.0 / D)
        dg_ref[...] += jnp.sum(dout * xn, axis=0, keepdims=True)
        dxn = dout * gf_ref[...]
        dx_ref[...] = r * (dxn - xn * jnp.mean(dxn * xn, axis=-1, keepdims=True))

    return pl.pallas_call(
        body, name=name, grid=(L // tl,),
        in_specs=[pl.BlockSpec((tl, K), lambda i: (i, 0)), _full((K, D)),
                  pl.BlockSpec((tl, D), lambda i: (i, 0)), _full((1, D)),
                  pl.BlockSpec((tl, D), lambda i: (i, 0))],
        out_specs=[pl.BlockSpec((tl, D), lambda i: (i, 0)), _full((8, 128)), _full((1, D))],
        out_shape=[jax.ShapeDtypeStruct((L, D), F32), jax.ShapeDtypeStruct((8, 128), F32),
                   jax.ShapeDtypeStruct((1, D), F32)],
        compiler_params=_cparams(("arbitrary",)),
    )(y, w, x, gf, tgt)


def out_proj_bwd(dx, w, ys, name):
    L, D = dx.shape
    K = w.shape[0]
    tl = min(TL_PROJ, L)
    n = len(ys)
    offs = np.cumsum([0] + [y.shape[1] for y in ys])

    def body(*refs):
        dx_ref, w_ref, y_refs = refs[0], refs[1], refs[2:2 + n]
        dy_refs, dw_ref = refs[2 + n:2 + 2 * n], refs[2 + 2 * n]

        @pl.when(pl.program_id(0) == 0)
        def _():
            dw_ref[...] = jnp.zeros_like(dw_ref)

        dxv = dx_ref[...]
        for k in range(n):
            dy_refs[k][...] = _mm_nt(dxv, w_ref[offs[k]:offs[k + 1], :])
            dw_ref[offs[k]:offs[k + 1], :] += _mm_tn(y_refs[k][...], dxv)

    y_specs = [pl.BlockSpec((tl, y.shape[1]), lambda i: (i, 0)) for y in ys]
    return pl.pallas_call(
        body, name=name, grid=(L // tl,),
        in_specs=[pl.BlockSpec((tl, D), lambda i: (i, 0)), _full((K, D))] + y_specs,
        out_specs=y_specs + [_full((K, D))],
        out_shape=[jax.ShapeDtypeStruct(y.shape, F32) for y in ys] + [jax.ShapeDtypeStruct((K, D), F32)],
        compiler_params=_cparams(("arbitrary",)),
    )(dx, w, *ys)


def in_proj_bwd_dx(x, g, dps, ws, dres, name, plan=None):
    L, D = x.shape
    tl = min(TL_PROJ, L)
    n = len(dps)

    def body(*refs):
        x_ref, g_ref, dres_ref = refs[:3]
        dp_refs, w_refs = refs[3:3 + n], refs[3 + n:3 + 2 * n]
        dx_ref, dg_ref = refs[3 + 2 * n:]

        @pl.when(pl.program_id(0) == 0)
        def _():
            dg_ref[...] = jnp.zeros_like(dg_ref)

        dh = None
        for dp_ref, w_ref, w in zip(dp_refs, w_refs, ws):
            if w.ndim == 3:
                tn = w.shape[2]
                parts = [_mm_nt(dp_ref[:, tn * k:tn * (k + 1)], w_ref[k]) for k in range(w.shape[0])]
            else:
                parts = [_mm_nt(dp_ref[...], w_ref[...])]
            for part in parts:
                dh = part if dh is None else dh + part
        xv = x_ref[...]
        r = _rms(xv)
        xn = xv * r
        dg_ref[...] += jnp.sum(dh * xn, axis=0, keepdims=True)
        dxn = dh * g_ref[...]
        dx_ref[...] = dres_ref[...] + r * (dxn - xn * jnp.mean(dxn * xn, axis=-1, keepdims=True))

    return _call(
        body, plan, name=name, grid=(L // tl,),
        in_specs=[pl.BlockSpec((tl, D), lambda i: (i, 0)), _full((1, D)), pl.BlockSpec((tl, D), lambda i: (i, 0))]
        + [pl.BlockSpec((tl, dp.shape[1]), lambda i: (i, 0)) for dp in dps] + [_full(w.shape) for w in ws],
        out_specs=[pl.BlockSpec((tl, D), lambda i: (i, 0)), _full((1, D))],
        out_shape=[jax.ShapeDtypeStruct((L, D), F32), jax.ShapeDtypeStruct((1, D), F32)],
        sem=("arbitrary",),
    )(x, g, dres, *dps, *ws)


def in_proj_bwd_dw(h, dp, name, tn, first=0, into=None, dtype=F32, plan=None, dp_first=0, count=None):
    L, D = h.shape
    tl = min(TL_DW, L)
    wb = EVEN_IN // N_CHIPS
    per = wb // tn
    count = dp.shape[1] // tn if count is None else count
    last = L // tl - 1

    def body(*refs):
        h_ref, dp_ref, dw_ref, acc = refs[0], refs[1], refs[-2], refs[-1]

        @pl.when(pl.program_id(1) == 0)
        def _():
            acc[...] = jnp.zeros_like(acc)

        acc[...] += _mm_tn(h_ref[...], dp_ref[...])

        @pl.when(pl.program_id(1) == last)
        def _():
            dw_ref[0] = acc[...].astype(dw_ref.dtype)

    ins = [h, dp] + ([] if into is None else [into])
    return _call(
        body, plan, name=name, grid=(count, L // tl),
        in_specs=[pl.BlockSpec((tl, D), lambda n, i: (i, 0)), pl.BlockSpec((tl, tn), lambda n, i: (i, n + dp_first))]
        + ([] if into is None else [ANY]),
        out_specs=pl.BlockSpec((1, D, tn), lambda n, i: ((n + first) // per, 0, (n + first) % per)),
        out_shape=jax.ShapeDtypeStruct((N_CHIPS, D, wb), dtype),
        scratch_shapes=[pltpu.VMEM((D, tn), F32)],
        aliases={} if into is None else {2: 0},
        sem=("arbitrary", "arbitrary"),
    )(*ins)


def _s5_param_fn(lam_re, lam_im, log_dt, b_re, b_im):
    lr = jnp.minimum(lam_re, -1e-4)
    li = lam_im
    dt = jnp.exp(log_dt)
    mag = jnp.exp(lr * dt)
    ab_re = mag * jnp.cos(li * dt)
    ab_im = mag * jnp.sin(li * dt)
    den = lr * lr + li * li
    n_re = ab_re - 1.0
    n_im = ab_im
    z_re = (n_re * lr + n_im * li) / den
    z_im = (n_im * lr - n_re * li) / den
    bb_re = z_re[None] * b_re - z_im[None] * b_im
    bb_im = z_re[None] * b_im + z_im[None] * b_re
    return ab_re, ab_im, bb_re, bb_im


def s5_params_fwd(lam_re, lam_im, log_dt, b_re, b_im, span):
    G, P = lam_re.shape
    H = b_re.shape[0]
    assert span & (span - 1) == 0

    def body(lr_ref, li_ref, dt_ref, br_ref, bi_ref, abr_ref, abi_ref, bbr_ref, bbi_ref, pr_ref, pi_ref):
        ab_re, ab_im, bb_re, bb_im = _s5_param_fn(lr_ref[...], li_ref[...], dt_ref[...], br_ref[...], bi_ref[...])
        abr_ref[...] = ab_re
        abi_ref[...] = ab_im
        bbr_ref[...] = bb_re
        bbi_ref[...] = bb_im
        cr, ci = ab_re, ab_im
        for _ in range(span.bit_length() - 1):
            cr, ci = cr * cr - ci * ci, 2.0 * cr * ci
        pr_ref[...] = cr
        pi_ref[...] = ci

    shp = lambda *s: jax.ShapeDtypeStruct(s, F32)
    return pl.pallas_call(
        body, name="s5_params_fwd",
        out_shape=[shp(G, P), shp(G, P), shp(H, G, P), shp(H, G, P), shp(G, P), shp(G, P)],
    )(lam_re, lam_im, log_dt, b_re, b_im)


def s5_params_bwd(lam_re, lam_im, log_dt, b_re, b_im, d_ab_re, d_ab_im, d_bb_re, d_bb_im):
    G, P = lam_re.shape
    H = b_re.shape[0]

    def body(lr_ref, li_ref, dt_ref, br_ref, bi_ref, g0, g1, g2, g3, o0, o1, o2, o3, o4):
        prim = (lr_ref[...], li_ref[...], dt_ref[...], br_ref[...], bi_ref[...])
        _, vjp = jax.vjp(_s5_param_fn, *prim)
        d = vjp((jnp.sum(g0[...], axis=0), jnp.sum(g1[...], axis=0), g2[...], g3[...]))
        o0[...], o1[...], o2[...], o3[...], o4[...] = d

    shp = lambda *s: jax.ShapeDtypeStruct(s, F32)
    return pl.pallas_call(
        body, name="s5_params_bwd",
        out_shape=[shp(G, P), shp(G, P), shp(G, 1), shp(H, G, P), shp(H, G, P)],
    )(lam_re, lam_im, log_dt, b_re, b_im, d_ab_re, d_ab_im, d_bb_re, d_bb_im)


def stream_order(a, tl):
    L, C = a.shape
    return a.reshape(L // tl, 8, tl // 8, C).transpose(0, 2, 1, 3).reshape(L, C)


def token_order(a, tl):
    L, C = a.shape
    return a.reshape(L // tl, tl // 8, 8, C).transpose(0, 2, 1, 3).reshape(L, C)


_LANE_BLK = 1024
_LANE_BLK_BWD = 1024


def _cmul_add(ar, ai, xr, xi, br, bi):
    return br + (ar * xr - ai * xi), bi + (ar * xi + ai * xr)


def _cmulc_add(ar, ai, xr, xi, br, bi):
    return br + (ar * xr + ai * xi), bi + (ar * xi - ai * xr)


def _s5_states(u, wbd_ref, a_re, a_im, at_re, at_im, s_re, s_im, e_re, e_im, c0_re, c0_im, tl):
    t8 = tl // 8
    for k in range(S5_KBLK):
        bu = _mm(u[:, 128 * k:128 * (k + 1)], wbd_ref[k])
        s_re[:, 512 * k:512 * (k + 1)] = bu[:, :512]
        s_im[:, 512 * k:512 * (k + 1)] = bu[:, 512:]
    outs_re, outs_im = [], []
    for b in range(S5_LANES // _LANE_BLK):
        lanes = slice(_LANE_BLK * b, _LANE_BLK * (b + 1))
        ar = jnp.broadcast_to(a_re[:, lanes], (8, _LANE_BLK))
        ai = jnp.broadcast_to(a_im[:, lanes], (8, _LANE_BLK))

        def local(i, carry, lanes=lanes, ar=ar, ai=ai):
            r = pl.multiple_of(i * 8, 8)
            sr, si = _cmul_add(ar, ai, carry[0], carry[1], s_re[pl.ds(r, 8), lanes], s_im[pl.ds(r, 8), lanes])
            s_re[pl.ds(r, 8), lanes] = sr
            s_im[pl.ds(r, 8), lanes] = si
            return sr, si

        zero = jnp.zeros((8, _LANE_BLK), F32)
        fr, fi = lax.fori_loop(0, t8, local, (zero, zero), unroll=True)
        tr, ti = at_re[:, lanes], at_im[:, lanes]
        er, ei = c0_re[:, lanes], c0_im[:, lanes]
        ers, eis = [er], [ei]
        for j in range(8):
            er, ei = _cmul_add(tr, ti, er, ei, fr[j:j + 1], fi[j:j + 1])
            ers.append(er)
            eis.append(ei)
        outs_re.append(ers[8])
        outs_im.append(eis[8])
        ent_r, ent_i = jnp.concatenate(ers[:8], axis=0), jnp.concatenate(eis[:8], axis=0)
        e_re[:, lanes] = ent_r
        e_im[:, lanes] = ent_i

        def fix(i, carry, lanes=lanes, ar=ar, ai=ai):
            r = pl.multiple_of(i * 8, 8)
            zr, zi = ar * carry[0] - ai * carry[1], ar * carry[1] + ai * carry[0]
            s_re[pl.ds(r, 8), lanes] = s_re[pl.ds(r, 8), lanes] + zr
            s_im[pl.ds(r, 8), lanes] = s_im[pl.ds(r, 8), lanes] + zi
            return zr, zi

        lax.fori_loop(0, t8, fix, (ent_r, ent_i), unroll=True)
    return jnp.concatenate(outs_re, axis=1), jnp.concatenate(outs_im, axis=1)


def _s5_readout(s_re, s_im, cre_ref, cim_ref):
    ys = []
    for k in range(S5_KBLK):
        lanes = slice(512 * k, 512 * (k + 1))
        ys.append(_mm(s_re[:, lanes], cre_ref[k]) - _mm(s_im[:, lanes], cim_ref[k]))
    return jnp.concatenate(ys, axis=1)


def s5_forward(p, wbd, cre, cim, atab, d_skip, w_glu, b_glu, plan=None):
    L = p.shape[0]
    tl = min(TL_S5, L)
    nch = L // tl

    def body(u_ref, z_ref, wbd_ref, cre_ref, cim_ref, at_ref, d_ref, wg_ref, bg_ref,
             ya_ref, st_re_ref, st_im_ref, sv_re_ref, sv_im_ref, s_re, s_im, e_re, e_im, car_re, car_im):
        @pl.when(pl.program_id(0) == 0)
        def _():
            car_re[...] = jnp.zeros_like(car_re)
            car_im[...] = jnp.zeros_like(car_im)

        c0_re, c0_im = car_re[...], car_im[...]
        st_re_ref[0] = c0_re
        st_im_ref[0] = c0_im
        u = u_ref[...]
        x_re, x_im = _s5_states(u, wbd_ref, at_ref[0:1], at_ref[1:2], at_ref[2:3], at_ref[3:4],
                                s_re, s_im, e_re, e_im, c0_re, c0_im, tl)
        car_re[...] = x_re
        car_im[...] = x_im
        sv_re_ref[...] = s_re[...].astype(sv_re_ref.dtype)
        sv_im_ref[...] = s_im[...].astype(sv_im_ref.dtype)
        y = _s5_readout(sv_re_ref, sv_im_ref, cre_ref, cim_ref) + d_ref[...] * u
        yg = _gelu(y)
        gate = _sigmoid(_mm(yg, wg_ref[...]) + bg_ref[...])
        sz, _ = _silu_and_grad(z_ref[...])
        ya_ref[...] = (yg * gate * sz).astype(ya_ref.dtype)

    return _call(
        body, plan, name="s5_forward", grid=(nch,),
        in_specs=[pl.BlockSpec((tl, 1024), lambda i: (i, 0)), pl.BlockSpec((tl, 1024), lambda i: (i, 1)),
                  _full(wbd.shape), _full(cre.shape), _full(cim.shape), _full(atab.shape),
                  _full((1, 1024)), _full((1024, 1024)), _full((1, 1024))],
        out_specs=[pl.BlockSpec((tl, 1024), lambda i: (i, 0)),
                   pl.BlockSpec((1, 1, S5_LANES), lambda i: (i, 0, 0)),
                   pl.BlockSpec((1, 1, S5_LANES), lambda i: (i, 0, 0)),
                   pl.BlockSpec((tl, S5_LANES), lambda i: (i, 0)), pl.BlockSpec((tl, S5_LANES), lambda i: (i, 0))],
        out_shape=[jax.ShapeDtypeStruct((L, 1024), MXU_DTYPE),
                   jax.ShapeDtypeStruct((nch, 1, S5_LANES), F32), jax.ShapeDtypeStruct((nch, 1, S5_LANES), F32),
                   jax.ShapeDtypeStruct((L, S5_LANES), MXU_DTYPE), jax.ShapeDtypeStruct((L, S5_LANES), MXU_DTYPE)],
        scratch_shapes=[pltpu.VMEM((tl, S5_LANES), F32), pltpu.VMEM((tl, S5_LANES), F32),
                        pltpu.VMEM((8, S5_LANES), F32), pltpu.VMEM((8, S5_LANES), F32),
                        pltpu.VMEM((1, S5_LANES), F32), pltpu.VMEM((1, S5_LANES), F32)],
        sem=("arbitrary",),
    )(p, p, wbd, cre, cim, atab, d_skip, w_glu, b_glu)


def s5_backward(p, dya, st_re, st_im, sv_re, sv_im, wbd, cre, cim, atab, d_skip, w_glu, b_glu, plan=None):
    L = p.shape[0]
    tl = min(TL_S5, L)
    t8 = tl // 8
    nch = L // tl
    rev = lambda i: (nch - 1 - i, 0)
    rev1 = lambda i: (nch - 1 - i, 1)
    rev3 = lambda i: (nch - 1 - i, 0, 0)
    ct_shape = (S5_KBLK, cre.shape[2], cre.shape[1])

    def body(u_ref, z_ref, dya_ref, str_ref, sti_ref, s_re, s_im, wbd_ref, cre_ref, cim_ref, at_ref,
             d_ref, wg_ref, bg_ref,
             dp_ref, dwbd_ref, dcre_ref, dcim_ref, dabr_ref, dabi_ref, dd_ref, dwg_ref, dbg_ref,
             g_re, g_im, car_re, car_im):
        @pl.when(pl.program_id(0) == 0)
        def _():
            car_re[...] = jnp.zeros_like(car_re)
            car_im[...] = jnp.zeros_like(car_im)
            for r in (dwbd_ref, dcre_ref, dcim_ref, dabr_ref, dabi_ref, dd_ref, dwg_ref, dbg_ref):
                r[...] = jnp.zeros_like(r)

        u = u_ref[...]
        a_re, a_im, at_re, at_im = at_ref[0:1], at_ref[1:2], at_ref[2:3], at_ref[3:4]
        y = _s5_readout(s_re, s_im, cre_ref, cim_ref) + d_ref[...] * u
        yg, dyg = _gelu_and_grad(y)
        gate = _sigmoid(_mm(yg, wg_ref[...]) + bg_ref[...])
        sz, dsz = _silu_and_grad(z_ref[...])
        dya = dya_ref[...]
        s5out = yg * gate
        dp_ref[:, 1024:] = (dya * s5out * dsz).astype(dp_ref.dtype)
        ds5 = dya * sz
        dt = ds5 * yg * gate * (1.0 - gate)
        dwg_ref[...] += _mm_tn(yg, dt)
        dbg_ref[...] += jnp.sum(dt, axis=0, keepdims=True)
        dyv = (ds5 * gate + _mm_nt(dt, wg_ref[...])) * dyg
        dd_ref[...] += jnp.sum(dyv * u, axis=0, keepdims=True)

        for k in range(S5_KBLK):
            lanes = slice(512 * k, 512 * (k + 1))
            dyk = dyv[:, 128 * k:128 * (k + 1)]
            g_re[:, lanes] = _mm_nt(dyk, cre_ref[k])
            g_im[:, lanes] = -_mm_nt(dyk, cim_ref[k])
            dcre_ref[k] += _mm_tn(dyk, s_re[:, lanes])
            dcim_ref[k] -= _mm_tn(dyk, s_im[:, lanes])

        blk = _LANE_BLK_BWD
        for b in range(S5_LANES // blk):
            lanes = slice(blk * b, blk * (b + 1))
            ar = jnp.broadcast_to(a_re[:, lanes], (8, blk))
            ai = jnp.broadcast_to(a_im[:, lanes], (8, blk))

            def local(j, carry, lanes=lanes, ar=ar, ai=ai):
                r = pl.multiple_of((t8 - 1 - j) * 8, 8)
                gr, gi = _cmulc_add(ar, ai, carry[0], carry[1], g_re[pl.ds(r, 8), lanes], g_im[pl.ds(r, 8), lanes])
                g_re[pl.ds(r, 8), lanes] = gr
                g_im[pl.ds(r, 8), lanes] = gi
                return gr, gi

            zero = jnp.zeros((8, blk), F32)
            fr, fi = lax.fori_loop(0, t8, local, (zero, zero), unroll=True)
            tr, ti = at_re[:, lanes], at_im[:, lanes]
            hr, hi = car_re[:, lanes], car_im[:, lanes]
            hrs, his = [hr], [hi]
            for j in range(7, -1, -1):
                hr, hi = _cmulc_add(tr, ti, hr, hi, fr[j:j + 1], fi[j:j + 1])
                hrs.append(hr)
                his.append(hi)
            car_re[:, lanes] = hrs[8]
            car_im[:, lanes] = his[8]
            in_r = jnp.concatenate(hrs[7::-1], axis=0)
            in_i = jnp.concatenate(his[7::-1], axis=0)

            wr, wi, nr, ni, accr, acci = in_r, in_i, zero, zero, zero, zero
            for pair in range(t8 // 2 - 1, -1, -1):
                rows = slice(16 * pair, 16 * pair + 16)
                s16r, s16i = s_re[rows, lanes].astype(F32), s_im[rows, lanes].astype(F32)
                for half in (1, 0):
                    r = 16 * pair + 8 * half
                    sr, si = s16r[8 * half:8 * half + 8], s16i[8 * half:8 * half + 8]
                    accr, acci = accr + (sr * nr + si * ni), acci + (sr * ni - si * nr)
                    wr, wi = ar * wr + ai * wi, ar * wi - ai * wr
                    nr, ni = g_re[r:r + 8, lanes] + wr, g_im[r:r + 8, lanes] + wi
                    g_re[r:r + 8, lanes] = nr
                    g_im[r:r + 8, lanes] = ni
            lr, li = s_re[tl - 16:tl, lanes].astype(F32)[8:], s_im[tl - 16:tl, lanes].astype(F32)[8:]
            row0 = lax.broadcasted_iota(jnp.int32, (8, blk), 0) == 0
            sr = jnp.where(row0, jnp.broadcast_to(str_ref[0][:, lanes], (8, blk)), pltpu.roll(lr, 1, 0))
            si = jnp.where(row0, jnp.broadcast_to(sti_ref[0][:, lanes], (8, blk)), pltpu.roll(li, 1, 0))
            dabr_ref[:, lanes] += accr + (sr * nr + si * ni)
            dabi_ref[:, lanes] += acci + (sr * ni - si * nr)

        dus = []
        for k in range(S5_KBLK):
            lanes = slice(512 * k, 512 * (k + 1))
            g = jnp.concatenate([g_re[:, lanes], g_im[:, lanes]], axis=1)
            dwbd_ref[k] += _mm_tn(u[:, 128 * k:128 * (k + 1)], g)
            dus.append(_mm_nt(g, wbd_ref[k]))
        du = jnp.concatenate(dus, axis=1) + dyv * d_ref[...]
        dp_ref[:, :1024] = du.astype(dp_ref.dtype)

    shp = lambda *s: jax.ShapeDtypeStruct(s, F32)
    return _call(
        body, plan, name="s5_backward", grid=(nch,),
        in_specs=[pl.BlockSpec((tl, 1024), rev), pl.BlockSpec((tl, 1024), rev1), pl.BlockSpec((tl, 1024), rev),
                  pl.BlockSpec((1, 1, S5_LANES), rev3), pl.BlockSpec((1, 1, S5_LANES), rev3),
                  pl.BlockSpec((tl, S5_LANES), rev), pl.BlockSpec((tl, S5_LANES), rev),
                  _full(wbd.shape), _full(cre.shape), _full(cim.shape), _full(atab.shape),
                  _full((1, 1024)), _full((1024, 1024)), _full((1, 1024))],
        out_specs=[pl.BlockSpec((tl, 2048), rev), _full(wbd.shape), _full(ct_shape), _full(ct_shape),
                   _full((8, S5_LANES)), _full((8, S5_LANES)), _full((1, 1024)), _full((1024, 1024)), _full((1, 1024))],
        out_shape=[jax.ShapeDtypeStruct((L, 2048), MXU_DTYPE), shp(*wbd.shape), shp(*ct_shape), shp(*ct_shape),
                   shp(8, S5_LANES), shp(8, S5_LANES), shp(1, 1024), shp(1024, 1024), shp(1, 1024)],
        scratch_shapes=[pltpu.VMEM((tl, S5_LANES), F32), pltpu.VMEM((tl, S5_LANES), F32),
                        pltpu.VMEM((1, S5_LANES), F32), pltpu.VMEM((1, S5_LANES), F32)],
        sem=("arbitrary",),
    )(p, p, dya, st_re, st_im, sv_re, sv_im, wbd, cre, cim, atab, d_skip, w_glu, b_glu)


def _block_diag(w, rows_first):
    g8 = w.reshape(S5_KBLK, 8, w.shape[1], w.shape[2])
    eye = jnp.eye(8, dtype=w.dtype)
    out = jnp.einsum('kgab,fg->kfagb', g8, eye)
    return out.reshape(S5_KBLK, 8 * w.shape[1], 8 * w.shape[2])


def _block_diag_extract(wbd, a, b):
    w5 = wbd.reshape(S5_KBLK, 8, a, 8, b)
    idx = jnp.arange(8)
    return w5[:, idx, :, idx, :].transpose(1, 0, 2, 3).reshape(S5_GROUPS, a, b)


def _ret_constants():
    log_g = np.log1p(-np.exp2(-5.0 - np.arange(RET_HEADS, dtype=np.float32))).astype(np.float32)
    idx = np.arange(RET_CHUNK, dtype=np.float32)
    diff = idx[:, None] - idx[None, :]
    decay = np.where(diff >= 0, np.exp(log_g[:, None, None] * np.maximum(diff, 0.0)), 0.0).astype(np.float32)
    xi = np.exp(log_g[None, :] * (idx[:, None] + 1.0)).astype(np.float32)
    zeta = np.exp(log_g[None, :] * (RET_CHUNK - 1.0 - idx[:, None])).astype(np.float32)
    chunk_decay = np.exp(log_g * RET_CHUNK).astype(np.float32)
    return decay, xi, zeta, chunk_decay


def _rope_tables(L):
    half = RET_DK // 2
    inv = ROPE_BASE ** (-jnp.arange(half, dtype=F32) / half)
    ang = jnp.arange(L, dtype=F32)[:, None] * inv[None, :]
    return jnp.cos(ang), jnp.sin(ang)


def _rot(xh, cos, sin):
    x1, x2 = xh[:, :128], xh[:, 128:]
    return jnp.concatenate([x1 * cos - x2 * sin, x1 * sin + x2 * cos], axis=1)


def _rot_t(dh, cos, sin):
    d1, d2 = dh[:, :128], dh[:, 128:]
    return jnp.concatenate([d1 * cos + d2 * sin, d2 * cos - d1 * sin], axis=1)


RET_PER_STEP = 4


def _ret_setup(L):
    nc = L // RET_CHUNK
    per = RET_PER_STEP if nc % RET_PER_STEP == 0 else 1
    decay_np, xi_np, zeta_np, cd_np = _ret_constants()
    tables = (jnp.asarray(decay_np), jnp.asarray(np.tile(xi_np, (per, 1))), jnp.asarray(np.tile(zeta_np, (per, 1))))
    return nc // per, per, tables, [float(c) for c in cd_np]


def _ret_rows(q_ref, k_ref, v_ref, cos_ref, sin_ref, xi_ref, zeta_ref):
    H = range(RET_HEADS)
    hs = [slice(RET_DK * h, RET_DK * (h + 1)) for h in H]
    cs, sn = cos_ref[...], sin_ref[...]
    qh = [_rot(q_ref[:, hs[h]], cs, sn) for h in H]
    kh = [_rot(k_ref[:, hs[h]], cs, sn) * (RET_DK ** -0.5) for h in H]
    vh = [v_ref[:, hs[h]] for h in H]
    qx = [qh[h] * xi_ref[:, h:h + 1] for h in H]
    kz = [kh[h] * zeta_ref[:, h:h + 1] for h in H]
    return hs, cs, sn, qh, kh, vh, qx, kz


def _ret_normed(qh, kh, vh, qx, dec_ref, prevs, per):
    H, C = range(RET_HEADS), range(per)
    rs = [slice(RET_CHUNK * c, RET_CHUNK * (c + 1)) for c in C]
    sc = [[_mm_nt(qh[h][rs[c]], kh[h][rs[c]]) * dec_ref[h] for h in H] for c in C]
    inner = [[_mm(sc[c][h], vh[h][rs[c]]) for h in H] for c in C]
    cross = [[_mm(qx[h][rs[c]], prevs[c][h]) for h in H] for c in C]
    o = [jnp.concatenate([inner[c][h] + cross[c][h] for c in C], axis=0) for h in H]
    oc = [o[h] - jnp.mean(o[h], axis=-1, keepdims=True) for h in H]
    rstd = [lax.rsqrt(jnp.mean(oc[h] * oc[h], axis=-1, keepdims=True) + NORM_EPS) for h in H]
    on = [oc[h] * rstd[h] for h in H]
    return rs, sc, rstd, on


def retention_forward(p, cos, sin, gain):
    L = p.shape[0]
    steps, per, (decay, xi, zeta), cd = _ret_setup(L)
    rows = RET_CHUNK * per

    def body(q_ref, k_ref, v_ref, z_ref, cos_ref, sin_ref, dec_ref, xi_ref, zeta_ref, gain_ref,
             yb_ref, prev_ref, state):
        @pl.when(pl.program_id(0) == 0)
        def _():
            state[...] = jnp.zeros_like(state)

        H, C = range(RET_HEADS), range(per)
        hs, cs, sn, qh, kh, vh, qx, kz = _ret_rows(q_ref, k_ref, v_ref, cos_ref, sin_ref, xi_ref, zeta_ref)
        prevs = [[state[h] for h in H]]
        for c in C:
            rs_c = slice(RET_CHUNK * c, RET_CHUNK * (c + 1))
            prevs.append([prevs[c][h] * cd[h] + _mm_tn(kz[h][rs_c], vh[h][rs_c]) for h in H])
        _, _, _, on = _ret_normed(qh, kh, vh, qx, dec_ref, prevs, per)
        sz, _ = _silu_and_grad(z_ref[...])
        for h in H:
            for c in C:
                prev_ref[c, h] = prevs[c][h].astype(prev_ref.dtype)
            state[h] = prevs[per][h]
            yb_ref[:, hs[h]] = (on[h] * gain_ref[:, hs[h]] * sz[:, hs[h]]).astype(yb_ref.dtype)

    col0 = p.shape[1] // 1024 - 4
    blk = lambda c: pl.BlockSpec((rows, 1024), lambda i, c=c: (i, c + col0))
    return pl.pallas_call(
        body, name="retention_forward", grid=(steps,),
        in_specs=[blk(0), blk(1), blk(2), blk(3),
                  pl.BlockSpec((rows, 128), lambda i: (i, 0)), pl.BlockSpec((rows, 128), lambda i: (i, 0)),
                  _full(decay.shape), _full(xi.shape), _full(zeta.shape), _full((1, 1024))],
        out_specs=[pl.BlockSpec((rows, 1024), lambda i: (i, 0)),
                   pl.BlockSpec((per, RET_HEADS, RET_DK, RET_DK), lambda i: (i, 0, 0, 0))],
        out_shape=[jax.ShapeDtypeStruct((L, 1024), MXU_DTYPE),
                   jax.ShapeDtypeStruct((steps * per, RET_HEADS, RET_DK, RET_DK), MXU_DTYPE)],
        scratch_shapes=[pltpu.VMEM((RET_HEADS, RET_DK, RET_DK), F32)],
        compiler_params=_cparams(("arbitrary",)),
    )(p, p, p, p, cos, sin, decay, xi, zeta, gain)


def retention_backward(p, dy, prevs, cos, sin, gain, plan=None):
    L = p.shape[0]
    steps, per, (decay, xi, zeta), cd = _ret_setup(L)
    rows = RET_CHUNK * per
    scale = RET_DK ** -0.5

    def body(q_ref, k_ref, v_ref, z_ref, dyb_ref, prev_ref, cos_ref, sin_ref, dec_ref, xi_ref, zeta_ref, gain_ref,
             dp_ref, dgain_ref, dstate):
        @pl.when(pl.program_id(0) == 0)
        def _():
            dstate[...] = jnp.zeros_like(dstate)
            dgain_ref[...] = jnp.zeros_like(dgain_ref)

        H, C = range(RET_HEADS), range(per)
        hs, cs, sn, qh, kh, vh, qx, kz = _ret_rows(q_ref, k_ref, v_ref, cos_ref, sin_ref, xi_ref, zeta_ref)
        prevs = [[prev_ref[c, h] for h in H] for c in C]
        rs, sc, rstd, on = _ret_normed(qh, kh, vh, qx, dec_ref, prevs, per)
        sz, dsz = _silu_and_grad(z_ref[...])
        dyb = dyb_ref[...]
        dong = [dyb[:, hs[h]] * sz[:, hs[h]] for h in H]
        don = [dong[h] * gain_ref[:, hs[h]] for h in H]
        do = [rstd[h] * (don[h] - jnp.mean(don[h], axis=-1, keepdims=True)
                         - on[h] * jnp.mean(don[h] * on[h], axis=-1, keepdims=True)) for h in H]
        dsc = [[_mm_nt(do[h][rs[c]], vh[h][rs[c]]) * dec_ref[h] for h in H] for c in C]
        dq_st = [[_mm_nt(do[h][rs[c]], prevs[c][h]) for h in H] for c in C]
        dnew = [[_mm_tn(qx[h][rs[c]], do[h][rs[c]]) for h in H] for c in C]
        dsts = [None] * per + [[dstate[h] for h in H]]
        for c in reversed(C):
            dsts[c] = [dsts[c + 1][h] * cd[h] + dnew[c][h] for h in H]
        dk_st = [[_mm_nt(vh[h][rs[c]], dsts[c + 1][h]) for h in H] for c in C]
        dv_st = [[_mm(kz[h][rs[c]], dsts[c + 1][h]) for h in H] for c in C]
        rows_of = lambda parts: jnp.concatenate(parts, axis=0)
        dqh = [rows_of([_mm(dsc[c][h], kh[h][rs[c]]) for c in C])
               + rows_of([dq_st[c][h] for c in C]) * xi_ref[:, h:h + 1] for h in H]
        dkh = [rows_of([_mm_tn(dsc[c][h], qh[h][rs[c]]) for c in C])
               + rows_of([dk_st[c][h] for c in C]) * zeta_ref[:, h:h + 1] for h in H]
        dvh = [rows_of([_mm_tn(sc[c][h], do[h][rs[c]]) + dv_st[c][h] for c in C]) for h in H]
        for h in H:
            dstate[h] = dsts[0][h]
            dgain_ref[:, hs[h]] += jnp.sum(dong[h] * on[h], axis=0, keepdims=True)
            dp_ref[:, hs[h]] = (_rot_t(dkh[h], cs, sn) * scale).astype(dp_ref.dtype)
            dp_ref[:, 1024 + RET_DK * h:1024 + RET_DK * (h + 1)] = dvh[h].astype(dp_ref.dtype)
            dp_ref[:, 2048 + RET_DK * h:2048 + RET_DK * (h + 1)] = (
                dyb[:, hs[h]] * on[h] * gain_ref[:, hs[h]] * dsz[:, hs[h]]).astype(dp_ref.dtype)
            dp_ref[:, 3072 + RET_DK * h:3072 + RET_DK * (h + 1)] = _rot_t(dqh[h], cs, sn).astype(dp_ref.dtype)

    col0 = p.shape[1] // 1024 - 4
    blk = lambda c: pl.BlockSpec((rows, 1024), lambda i, c=c: (steps - 1 - i, c + col0))
    tab = pl.BlockSpec((rows, 128), lambda i: (steps - 1 - i, 0))
    return _call(
        body, plan, name="retention_backward", grid=(steps,),
        in_specs=[blk(0), blk(1), blk(2), blk(3), pl.BlockSpec((rows, 1024), lambda i: (steps - 1 - i, 0)),
                  pl.BlockSpec((per, RET_HEADS, RET_DK, RET_DK), lambda i: (steps - 1 - i, 0, 0, 0)),
                  tab, tab, _full(decay.shape), _full(xi.shape), _full(zeta.shape), _full((1, 1024))],
        out_specs=[pl.BlockSpec((rows, 4096), lambda i: (steps - 1 - i, 0)), _full((1, 1024))],
        out_shape=[jax.ShapeDtypeStruct((L, 4096), MXU_DTYPE), jax.ShapeDtypeStruct((1, 1024), F32)],
        scratch_shapes=[pltpu.VMEM((RET_HEADS, RET_DK, RET_DK), F32)],
        sem=("arbitrary",),
    )(p, p, p, p, dy, prevs, cos, sin, decay, xi, zeta, gain)


def _sgu_mix(p_ref, gain_ref, wm_ref, bt_ref, tl):
    pu, pv, z = p_ref[:, :2048], p_ref[:, 2048:4096], p_ref[:, 4096:]
    (u, du), (v, dv) = _gelu_and_grad(pu), _gelu_and_grad(pv)
    mu = jnp.mean(v, axis=-1, keepdims=True)
    vc = v - mu
    rstd = lax.rsqrt(jnp.mean(vc * vc, axis=-1, keepdims=True) + NORM_EPS)
    vn = vc * rstd
    vg = vn * gain_ref[...]
    mask = (lax.broadcasted_iota(jnp.int32, (SGU_CHUNK, SGU_CHUNK), 0)
            >= lax.broadcasted_iota(jnp.int32, (SGU_CHUNK, SGU_CHUNK), 1))
    wms = [jnp.where(mask, wm_ref[g], 0.0) for g in range(SGU_GROUPS)]
    rows = []
    for c in range(tl // SGU_CHUNK):
        rs = slice(SGU_CHUNK * c, SGU_CHUNK * (c + 1))
        cols = []
        for g in range(SGU_GROUPS):
            gs = slice(SGU_GDIM * g, SGU_GDIM * (g + 1))
            cols.append(_mm(wms[g], vg[rs, gs]) + bt_ref[:, g:g + 1])
        rows.append(jnp.concatenate(cols, axis=1))
    s = rows[0] if len(rows) == 1 else jnp.concatenate(rows, axis=0)
    return du, dv, z, u, vn, rstd, vg, wms, mask, s


def sgu_forward(p, gain, wm, bt):
    L = p.shape[0]
    tl = min(TL_SGU, L)

    def body(p_ref, gain_ref, wm_ref, bt_ref, y_ref):
        _, _, z, u, _, _, _, _, _, s = _sgu_mix(p_ref, gain_ref, wm_ref, bt_ref, tl)
        sz, _ = _silu_and_grad(z)
        y_ref[...] = (u * s * sz).astype(y_ref.dtype)

    return pl.pallas_call(
        body, name="sgu_forward", grid=(L // tl,),
        in_specs=[pl.BlockSpec((tl, ODD_IN), lambda i: (i, 0)), _full((1, 2048)), _full(wm.shape), _full(bt.shape)],
        out_specs=pl.BlockSpec((tl, 2048), lambda i: (i, 0)),
        out_shape=jax.ShapeDtypeStruct((L, 2048), MXU_DTYPE),
        compiler_params=_cparams(("arbitrary",)),
    )(p, gain, wm, bt)


def sgu_backward(p, dy, gain, wm, bt, plan=None):
    L = p.shape[0]
    tl = min(TL_SGU, L)

    def body(p_ref, dy_ref, gain_ref, wm_ref, bt_ref, dp_ref, dgain_ref, dwm_ref, dbt_ref):
        @pl.when(pl.program_id(0) == 0)
        def _():
            dgain_ref[...] = jnp.zeros_like(dgain_ref)
            dwm_ref[...] = jnp.zeros_like(dwm_ref)
            dbt_ref[...] = jnp.zeros_like(dbt_ref)

        gu, gv, z, u, vn, rstd, vg, wms, mask, s = _sgu_mix(p_ref, gain_ref, wm_ref, bt_ref, tl)
        sz, dsz = _silu_and_grad(z)
        dyv = dy_ref[...]
        dp_ref[:, 4096:] = (dyv * u * s * dsz).astype(dp_ref.dtype)
        dsg = dyv * sz
        dp_ref[:, :2048] = (dsg * s * gu).astype(dp_ref.dtype)
        ds = dsg * u
        rows = []
        dbs = [jnp.zeros((SGU_CHUNK, 1), F32) for _ in range(SGU_GROUPS)]
        for c in range(tl // SGU_CHUNK):
            rs = slice(SGU_CHUNK * c, SGU_CHUNK * (c + 1))
            cols = []
            for g in range(SGU_GROUPS):
                gs = slice(SGU_GDIM * g, SGU_GDIM * (g + 1))
                dsg_c = ds[rs, gs]
                dbs[g] = dbs[g] + jnp.sum(dsg_c, axis=1, keepdims=True)
                dwm_ref[g] += jnp.where(mask, _mm_nt(dsg_c, vg[rs, gs]), 0.0)
                cols.append(_mm_tn(wms[g], dsg_c))
            rows.append(jnp.concatenate(cols, axis=1))
        dbt_ref[...] += jnp.concatenate(dbs, axis=1)
        dvg = rows[0] if len(rows) == 1 else jnp.concatenate(rows, axis=0)
        dgain_ref[...] += jnp.sum(dvg * vn, axis=0, keepdims=True)
        dvn = dvg * gain_ref[...]
        dv = rstd * (dvn - jnp.mean(dvn, axis=-1, keepdims=True) - vn * jnp.mean(dvn * vn, axis=-1, keepdims=True))
        dp_ref[:, 2048:4096] = (dv * gv).astype(dp_ref.dtype)

    return _call(
        body, plan, name="sgu_backward", grid=(L // tl,),
        in_specs=[pl.BlockSpec((tl, ODD_IN), lambda i: (i, 0)), pl.BlockSpec((tl, 2048), lambda i: (i, 0)),
                  _full((1, 2048)), _full(wm.shape), _full(bt.shape)],
        out_specs=[pl.BlockSpec((tl, ODD_IN), lambda i: (i, 0)), _full((1, 2048)), _full(wm.shape), _full(bt.shape)],
        out_shape=[jax.ShapeDtypeStruct((L, ODD_IN), MXU_DTYPE), jax.ShapeDtypeStruct((1, 2048), F32),
                   jax.ShapeDtypeStruct(wm.shape, F32), jax.ShapeDtypeStruct(bt.shape, F32)],
        sem=("arbitrary",),
    )(p, dy, gain, wm, bt)


def cast_shards(mats):
    n = len(mats)
    steps = 8

    def body(*refs):
        for p in range(n):
            refs[n + p][...] = refs[p][...].astype(MXU_DTYPE)

    specs = [pl.BlockSpec((m.shape[0] // steps, m.shape[1]), lambda i: (i, 0)) for m in mats]
    return pl.pallas_call(
        body, name="cast_shards", grid=(steps,), in_specs=specs, out_specs=specs,
        out_shape=[jax.ShapeDtypeStruct(m.shape, MXU_DTYPE) for m in mats],
        compiler_params=_cparams(("arbitrary",)),
    )(*mats)


def local_grads(x, tgt, w):
    L = x.shape[0]
    ne, gf = w["norm_even"], w["final_norm"].reshape(1, D_MODEL)
    sh = dict(zip(MATRICES, cast_shards([w[n][0] for n in MATRICES])))
    lam_re, lam_im = w["s5_lam_re"][0], w["s5_lam_im"][0]
    log_dt = w["s5_log_dt"].reshape(S5_GROUPS, 1)
    bt_re = jnp.transpose(w["s5_b_re"][0], (2, 0, 1))
    bt_im = jnp.transpose(w["s5_b_im"][0], (2, 0, 1))
    c_re, c_im = w["s5_c_re"][0], w["s5_c_im"][0]
    wm = w["sgu_w_spatial"][0]
    bt = jnp.transpose(w["sgu_b_spatial"][0])

    tl5 = min(TL_S5, L)
    ab_re, ab_im, bb_re, bb_im, at_re, at_im = s5_params_fwd(lam_re, lam_im, log_dt, bt_re, bt_im, tl5 // 8)
    atab = jnp.stack([ab_re.reshape(S5_LANES), ab_im.reshape(S5_LANES),
                      at_re.reshape(S5_LANES), at_im.reshape(S5_LANES)])
    wbd = jnp.concatenate([_block_diag(jnp.transpose(bb_re, (1, 0, 2)), True),
                           _block_diag(jnp.transpose(bb_im, (1, 0, 2)), True)], axis=2).astype(MXU_DTYPE)
    cre = _block_diag(jnp.transpose(c_re, (0, 2, 1)), True).astype(MXU_DTYPE)
    cim = _block_diag(jnp.transpose(c_im, (0, 2, 1)), True).astype(MXU_DTYPE)
    cos, sin = _rope_tables(L)

    s5_cols = 2 * S5_WIDTH
    me = (2 * lax.axis_index("x") + lax.axis_index("y")).astype(jnp.int32)
    xs = stream_order(x, tl5)
    slab = lambda d: jnp.stack([me ^ d])
    zero = jnp.zeros((1,), jnp.int32)
    shards = [sh["w_in_even"][None]]
    (p1, h0s, h0), (got,) = even_in_slabs(x, xs, ne, shards[0], slab(0), zero, "even_in_0",
                                          plan=gather_plan([sh["w_in_even"]], only=1))
    for d in (1, 2, 3):
        shards.append(got)
        plan = gather_plan([sh["w_in_even"]], only=d + 1) if d < 3 else gather_plan([sh["s5_w_glu"]])
        (p1,), (got,) = even_in_slabs(x, xs, ne, shards[d], slab(d), zero, "even_in_%d" % d, p_in=p1, plan=plan)
    w_glu = got
    by_xor = jnp.concatenate(shards)
    w_in_e = [lax.dynamic_index_in_dim(by_xor, me ^ j, 0, keepdims=False) for j in range(N_CHIPS)]
    w_s5 = jnp.concatenate([w_in_e[0], w_in_e[1][:, :s5_cols - EVEN_IN // N_CHIPS]], axis=1)
    w_kvzq = jnp.concatenate([w_in_e[2], w_in_e[3], w_in_e[1][:, s5_cols - EVEN_IN // N_CHIPS:]], axis=1)
    w_glu = w_glu.reshape(S5_WIDTH, S5_WIDTH)
    (ya, st_re, st_im, sv_re, sv_im), (w_out_e, w_in_o, w_out_o, no, sg_gain) = s5_forward(
        p1, wbd, cre, cim, atab, w["s5_d"], w_glu, w["s5_b_glu"],
        gather_plan([sh["w_out_even"], sh["w_in_odd"], sh["w_out_odd"], w["norm_odd"], w["sgu_norm_gain"]]))
    w_out_e = w_out_e.reshape(2 * S5_WIDTH, D_MODEL)
    w_out_o = w_out_o.reshape(SGU_WIDTH, D_MODEL)
    no, sg_gain = no.reshape(1, D_MODEL), sg_gain.reshape(1, SGU_WIDTH)
    yb, prevs = retention_forward(p1, cos, sin, w["ret_gn_gain"])
    ya = token_order(ya, tl5)
    x1 = matmul_residual([ya, yb], w_out_e, x, "even_out")
    (p2, h1), _ = norm_matmul(x1, no, w_in_o, "odd_in")
    y2 = sgu_forward(p2, sg_gain, wm, bt)
    dx2, loss, dgf = out_proj_loss(y2, w_out_o, x1, gf, tgt, "odd_out_loss")

    g, landed = {}, {}
    shard_major = lambda a, n: a.reshape((N_CHIPS,) + w[n].shape[1:])
    dy2, g_w_out_o = out_proj_bwd(dx2, w_out_o, [y2], "odd_out_bwd")
    (dp2, g["sgu_norm_gain"], dwm, dbt), (landed["w_out_odd"],) = sgu_backward(
        p2, dy2, sg_gain, wm, bt, reduce_plan([shard_major(g_w_out_o, "w_out_odd")]))
    g_w_in_o, _ = in_proj_bwd_dw(h1, dp2, "odd_in_dw", ODD_IN // N_CHIPS)
    (dx1, g["norm_odd"]), _ = in_proj_bwd_dx(x1, no, [dp2], [w_in_o], dx2, "odd_in_dx")
    dya, dyb, g_w_out_e = out_proj_bwd(dx1, w_out_e, [ya, yb], "even_out_bwd")
    ((dpa, dwbd, dcre, dcim, dab_re, dab_im, g["s5_d"], g_w_glu, g["s5_b_glu"]),
     (landed["w_in_odd"], landed["w_out_even"])) = s5_backward(
        p1, stream_order(dya, tl5), st_re, st_im, sv_re, sv_im, wbd, cre, cim, atab, w["s5_d"], w_glu,
        w["s5_b_glu"], reduce_plan([g_w_in_o, shard_major(g_w_out_e, "w_out_even")]))

    dbb_re = jnp.transpose(_block_diag_extract(dwbd[:, :, :512], S5_GROUP, S5_STATE), (1, 0, 2))
    dbb_im = jnp.transpose(_block_diag_extract(dwbd[:, :, 512:], S5_GROUP, S5_STATE), (1, 0, 2))
    dlr, dli, ddt, dbt_re, dbt_im = s5_params_bwd(
        lam_re, lam_im, log_dt, bt_re, bt_im, dab_re.reshape(8, S5_GROUPS, S5_STATE),
        dab_im.reshape(8, S5_GROUPS, S5_STATE), dbb_re, dbb_im)
    g["s5_lam_re"], g["s5_lam_im"] = dlr[None], dli[None]
    g["s5_log_dt"] = ddt.reshape(1, S5_GROUPS)
    g["s5_b_re"], g["s5_b_im"] = dbt_re, dbt_im
    g["s5_c_re"] = _block_diag_extract(dcre, S5_GROUP, S5_STATE)[None]
    g["s5_c_im"] = _block_diag_extract(dcim, S5_GROUP, S5_STATE)[None]
    g["sgu_w_spatial"] = dwm[None]
    g["sgu_b_spatial"] = jnp.transpose(dbt)[None]
    g["final_norm"] = dgf.reshape(D_MODEL)
    g["loss"] = loss

    big_small = ("s5_b_re", "s5_b_im")
    mid_small = ("s5_c_re",)
    (dpb, g["ret_gn_gain"]), recv = retention_backward(
        p1, dyb, prevs, cos, sin, w["ret_gn_gain"],
        reduce_plan([shard_major(g_w_glu, "s5_w_glu")], [g[n] for n in big_small]))
    landed.update(zip(("s5_w_glu",) + big_small, recv))
    done = tuple(n for n in MATRICES if n != "w_in_even")
    part = {n: sum_slabs(landed[n], "sum_" + n) for n in done}
    g_w_in_e, recv = in_proj_bwd_dw(h0s, dpa, "even_in_dw_s5", 512, dtype=MXU_DTYPE,
                                    plan=_SiblingPlan([part[n] for n in done]))
    other = dict(zip(done, recv))
    small = tuple(n for n in SMALL if n != "norm_even" and n not in big_small + mid_small) + ("loss",)
    wb = EVEN_IN // N_CHIPS
    g_w_in_e, recv = in_proj_bwd_dw(h0, dpb, "even_in_dw_q", 512, first=s5_cols // 512, into=g_w_in_e,
                                    dtype=MXU_DTYPE, dp_first=2 * wb // 512, count=RET_HEADS * RET_DK // 512,
                                    plan=reduce_plan([], [g[n] for n in mid_small]))
    landed.update(zip(mid_small, recv))
    g_w_in_e, recv = in_proj_bwd_dw(h0, dpb, "even_in_dw_kvz", wb, first=2, into=g_w_in_e, dtype=MXU_DTYPE,
                                    count=2, plan=reduce_plan([], [g[n] for n in small]))
    landed.update(zip(small, recv))
    (dx0, g["norm_even"]), (landed["w_in_even"],) = in_proj_bwd_dx(
        x, ne, [token_order(dpa, tl5), dpb], [w_s5, w_kvzq], dx1, "even_in_dx", reduce_plan([g_w_in_e]))
    (landed["norm_even"],) = run_plan(reduce_plan([], [g["norm_even"]]), "exchange_norm_even")
    return dx0, landed, part, other


def _row_block(rows):
    return 128 if rows % 128 == 0 else rows


def sum_slabs(r, name):
    _, R, C = r.shape
    tr = _row_block(R)

    def body(r_ref, o_ref):
        a, b, c, d = (r_ref[k].astype(F32) for k in range(N_CHIPS))
        o_ref[...] = (a + b) + (c + d)

    return pl.pallas_call(
        body, name=name, grid=(R // tr,),
        in_specs=[pl.BlockSpec((N_CHIPS, tr, C), lambda i: (0, i, 0))],
        out_specs=pl.BlockSpec((tr, C), lambda i: (i, 0)),
        out_shape=jax.ShapeDtypeStruct((R, C), F32),
        compiler_params=_cparams(("arbitrary",)),
    )(r)


def _adam(w, m, v, g):
    mn = ADAM_B1 * m + (1.0 - ADAM_B1) * g
    vn = ADAM_B2 * v + (1.0 - ADAM_B2) * (g * g)
    m_hat = mn / (1.0 - ADAM_B1 ** ADAM_STEP)
    v_hat = vn / (1.0 - ADAM_B2 ** ADAM_STEP)
    return -ADAM_LR * (m_hat / (jnp.sqrt(v_hat) + ADAM_EPS) + ADAM_WD * w), mn, vn


def adam_update(w, m, v, ga, gb, name, plan=None):
    R, C = w.shape
    tr = _row_block(R)

    def body(w_ref, m_ref, v_ref, ga_ref, gb_ref, g_out, d_out, m_out, v_out):
        g = ga_ref[...] + gb_ref[...]
        g_out[...] = g
        d_out[...], m_out[...], v_out[...] = _adam(w_ref[...], m_ref[...], v_ref[...], g)

    blk = pl.BlockSpec((tr, C), lambda i: (i, 0))
    return _call(
        body, plan, name=name, grid=(R // tr,),
        in_specs=[blk] * 5, out_specs=[blk] * 4,
        out_shape=[jax.ShapeDtypeStruct((R, C), F32)] * 4,
        sem=("arbitrary",),
    )(w, m, v, ga, gb)


WIDE_ROWS = ("s5_b_re", "s5_b_im")


def sum_small(landed):
    def body(*refs):
        k = len(refs) // 2
        for i in range(k):
            r = refs[i]
            refs[k + i][...] = (r[0] + r[1]) + (r[2] + r[3])

    names = list(landed)
    res = pl.pallas_call(
        body, name="sum_small", out_shape=[jax.ShapeDtypeStruct(landed[n].shape[1:], F32) for n in names],
        compiler_params=pltpu.CompilerParams(vmem_limit_bytes=VMEM_LIMIT),
    )(*[landed[n] for n in names])
    return dict(zip(names, res))


def adam_small(names, w, m, v, ga, gb):
    def body(*refs):
        k = len(refs) // 9
        me = 2 * lax.axis_index("x") + lax.axis_index("y")
        for i in range(k):
            w_ref, m_ref, v_ref, ga_ref, gb_ref = refs[i], refs[k + i], refs[2 * k + i], refs[3 * k + i], refs[4 * k + i]
            size = w_ref.shape[-1]
            if ga_ref.shape != w_ref.shape:
                part = pl.ds(pl.multiple_of(me * size, LANES), size)
                g = ga_ref[:, part] + gb_ref[:, part]
            else:
                g = ga_ref[...] + gb_ref[...]
            refs[5 * k + i][...] = g
            refs[6 * k + i][...], refs[7 * k + i][...], refs[8 * k + i][...] = _adam(w_ref[...], m_ref[...], v_ref[...], g)

    ins = [d[n] for d in (w, m, v, ga, gb) for n in names]
    outs = [jax.ShapeDtypeStruct(w[n].shape, F32) for _ in range(4) for n in names]
    res = pl.pallas_call(body, name="adam_small", out_shape=outs,
                         compiler_params=pltpu.CompilerParams(vmem_limit_bytes=VMEM_LIMIT))(*ins)
    k = len(names)
    return [dict(zip(names, res[j * k:(j + 1) * k])) for j in range(4)]


WEIGHTS = ("norm_even", "w_in_even", "s5_lam_re", "s5_lam_im", "s5_log_dt", "s5_b_re", "s5_b_im", "s5_c_re",
           "s5_c_im", "s5_d", "s5_w_glu", "s5_b_glu", "ret_gn_gain", "w_out_even", "norm_odd", "w_in_odd",
           "sgu_norm_gain", "sgu_w_spatial", "sgu_b_spatial", "w_out_odd", "final_norm")
MATRICES = ("w_in_even", "s5_w_glu", "w_out_even", "w_in_odd", "w_out_odd")
SHARDED_VECS = ("norm_odd", "sgu_norm_gain")
REPLICATED = tuple(n for n in WEIGHTS if n not in MATRICES and n not in SHARDED_VECS)
SMALL = tuple(n for n in WEIGHTS if n not in MATRICES)
LANES = 128


def kernel(x, norm_even, w_in_even, s5_lam_re, s5_lam_im, s5_log_dt, s5_b_re, s5_b_im, s5_c_re, s5_c_im, s5_d, s5_w_glu, s5_b_glu, ret_gn_gain, w_out_even, norm_odd, w_in_odd, sgu_norm_gain, sgu_w_spatial, sgu_b_spatial, w_out_odd, final_norm, loss_target, m_norm_even, m_w_in_even, m_s5_lam_re, m_s5_lam_im, m_s5_log_dt, m_s5_b_re, m_s5_b_im, m_s5_c_re, m_s5_c_im, m_s5_d, m_s5_w_glu, m_s5_b_glu, m_ret_gn_gain, m_w_out_even, m_norm_odd, m_w_in_odd, m_sgu_norm_gain, m_sgu_w_spatial, m_sgu_b_spatial, m_w_out_odd, m_final_norm, v_norm_even, v_w_in_even, v_s5_lam_re, v_s5_lam_im, v_s5_log_dt, v_s5_b_re, v_s5_b_im, v_s5_c_re, v_s5_c_im, v_s5_d, v_s5_w_glu, v_s5_b_glu, v_ret_gn_gain, v_w_out_even, v_norm_odd, v_w_in_odd, v_sgu_norm_gain, v_sgu_w_spatial, v_sgu_b_spatial, v_w_out_odd, v_final_norm):
    w = dict(norm_even=norm_even, w_in_even=w_in_even, s5_lam_re=s5_lam_re, s5_lam_im=s5_lam_im, s5_log_dt=s5_log_dt, s5_b_re=s5_b_re, s5_b_im=s5_b_im, s5_c_re=s5_c_re, s5_c_im=s5_c_im, s5_d=s5_d, s5_w_glu=s5_w_glu, s5_b_glu=s5_b_glu, ret_gn_gain=ret_gn_gain, w_out_even=w_out_even, norm_odd=norm_odd, w_in_odd=w_in_odd, sgu_norm_gain=sgu_norm_gain, sgu_w_spatial=sgu_w_spatial, sgu_b_spatial=sgu_b_spatial, w_out_odd=w_out_odd, final_norm=final_norm)
    m = dict(norm_even=m_norm_even, w_in_even=m_w_in_even, s5_lam_re=m_s5_lam_re, s5_lam_im=m_s5_lam_im, s5_log_dt=m_s5_log_dt, s5_b_re=m_s5_b_re, s5_b_im=m_s5_b_im, s5_c_re=m_s5_c_re, s5_c_im=m_s5_c_im, s5_d=m_s5_d, s5_w_glu=m_s5_w_glu, s5_b_glu=m_s5_b_glu, ret_gn_gain=m_ret_gn_gain, w_out_even=m_w_out_even, norm_odd=m_norm_odd, w_in_odd=m_w_in_odd, sgu_norm_gain=m_sgu_norm_gain, sgu_w_spatial=m_sgu_w_spatial, sgu_b_spatial=m_sgu_b_spatial, w_out_odd=m_w_out_odd, final_norm=m_final_norm)
    v = dict(norm_even=v_norm_even, w_in_even=v_w_in_even, s5_lam_re=v_s5_lam_re, s5_lam_im=v_s5_lam_im, s5_log_dt=v_s5_log_dt, s5_b_re=v_s5_b_re, s5_b_im=v_s5_b_im, s5_c_re=v_s5_c_re, s5_c_im=v_s5_c_im, s5_d=v_s5_d, s5_w_glu=v_s5_w_glu, s5_b_glu=v_s5_b_glu, ret_gn_gain=v_ret_gn_gain, w_out_even=v_w_out_even, norm_odd=v_norm_odd, w_in_odd=v_w_in_odd, sgu_norm_gain=v_sgu_norm_gain, sgu_w_spatial=v_sgu_w_spatial, sgu_b_spatial=v_sgu_b_spatial, w_out_odd=v_w_out_odd, final_norm=v_final_norm)

    grad_x, landed, part, other = local_grads(x[0], loss_target[0], w)

    small = SMALL + ("loss",)
    part["w_in_even"] = sum_slabs(landed["w_in_even"], "sum_w_in_even")
    part.update(sum_small({n: landed[n] for n in small}))
    names = ("w_in_even",) + small
    swap = _SiblingPlan([part[n] for n in names])

    host = "w_in_odd"
    res_host, swapped = adam_update(w[host][0], m[host][0], v[host][0], part[host], other[host], "adam_" + host, swap)
    other.update(zip(names, swapped))
    wt, mt, vt = dict(w), dict(m), dict(v)
    for n in WIDE_ROWS:
        wt[n], mt[n], vt[n] = (jnp.transpose(a[n][0], (2, 0, 1)) for a in (w, m, v))
    out_g, out_d, out_m, out_v = adam_small(SMALL, wt, mt, vt, part, other)
    for n in WIDE_ROWS:
        for out in (out_g, out_d, out_m, out_v):
            out[n] = jnp.transpose(out[n], (1, 2, 0))[None]
    for n in MATRICES:
        res = res_host if n == host else adam_update(w[n][0], m[n][0], v[n][0], part[n], other[n], "adam_" + n)[0]
        out_g[n], out_d[n], out_m[n], out_v[n] = (r[None] for r in res)
    total_loss = (part["loss"] + other["loss"])[0, 0]

    return (total_loss, grad_x[None], *[out_g[n] for n in WEIGHTS], *[out_d[n] for n in WEIGHTS],
            *[out_m[n] for n in WEIGHTS], *[out_v[n] for n in WEIGHTS])
```

```python
import functools
import math

import numpy as np
import jax
import jax.numpy as jnp
from jax import lax
from jax.experimental import pallas as pl
from jax.experimental.pallas import tpu as pltpu

F32 = jnp.float32
MXU_DTYPE = jnp.bfloat16
NORM_EPS = 1e-6
D_MODEL = 1024
S5_WIDTH = 1024
S5_GROUP = 16
S5_GROUPS = 64
S5_STATE = 64
S5_LANES = S5_GROUPS * S5_STATE
S5_KBLK = 8
RET_HEADS = 4
RET_DK = 256
RET_CHUNK = 128
ROPE_BASE = 10000.0
SGU_WIDTH = 2048
SGU_GROUPS = 4
SGU_GDIM = 512
SGU_CHUNK = 128
EVEN_IN = 6144
ODD_IN = 6144
ADAM_LR = 0.001
ADAM_B1 = 0.9
ADAM_B2 = 0.999
ADAM_EPS = 1e-08
ADAM_WD = 0.01
ADAM_STEP = 10
N_CHIPS = 4
VMEM_LIMIT = 56 * 1024 * 1024

TL_PROJ = 512
TL_DW = 1024
TL_S5 = 256
TL_SGU = 256


def _cparams(sem, **kw):
    return pltpu.CompilerParams(dimension_semantics=sem, vmem_limit_bytes=VMEM_LIMIT, **kw)


def _mm(a, b):
    return jnp.dot(a.astype(MXU_DTYPE), b.astype(MXU_DTYPE), preferred_element_type=F32)


def _mm_nt(a, b):
    return lax.dot_general(a.astype(MXU_DTYPE), b.astype(MXU_DTYPE),
                           (((1,), (1,)), ((), ())), preferred_element_type=F32)


def _mm_tn(a, b):
    return lax.dot_general(a.astype(MXU_DTYPE), b.astype(MXU_DTYPE),
                           (((0,), (0,)), ((), ())), preferred_element_type=F32)


_GELU_C = math.sqrt(2.0 / math.pi)


def _gelu_parts(x):
    x2 = x * x
    th = jnp.tanh(x * (_GELU_C + (_GELU_C * 0.044715) * x2))
    hx = 0.5 * x
    return hx + hx * th, th, x2, hx


def _gelu(x):
    return _gelu_parts(x)[0]


def _gelu_and_grad(x):
    g, th, x2, hx = _gelu_parts(x)
    return g, (0.5 + 0.5 * th) + hx * (1.0 - th * th) * (_GELU_C + (3.0 * _GELU_C * 0.044715) * x2)


def _gelu_grad(x):
    return _gelu_and_grad(x)[1]


def _sigmoid(x):
    return 1.0 / (1.0 + jnp.exp(-x))


def _silu_and_grad(x):
    s = _sigmoid(x)
    return x * s, s * (1.0 + x * (1.0 - s))


def _rms(x):
    return lax.rsqrt(jnp.mean(x * x, axis=-1, keepdims=True) + NORM_EPS)


def _full(shape):
    nd = len(shape)
    return pl.BlockSpec(shape, lambda *_: (0,) * nd)


MESH = pl.DeviceIdType.MESH
ANY = pl.BlockSpec(memory_space=pl.ANY)


def _place():
    return lax.axis_index("x"), lax.axis_index("y"), lax.axis_index("c")


def _chip_peer(x, y, c, d):
    return (1 - x if d >= 2 else x, 1 - y if d % 2 else y, c)


class _Plan:
    def __init__(self, inputs, out_shape, build):
        self.inputs, self.out_shape, self._build = list(inputs), list(out_shape), build
        n = len(self.inputs)
        self.sems = [pltpu.SemaphoreType.DMA((n, 3)), pltpu.SemaphoreType.DMA((n, 3)), pltpu.SemaphoreType.DMA((n,))]

    def start(self, in_refs, out_refs, sems):
        send, recv, local = self._build(in_refs, out_refs, sems)
        for p in range(len(self.inputs)):
            local[p].start()
            for cp in send[p]:
                cp.start()

    def wait(self, in_refs, out_refs, sems):
        send, recv, local = self._build(in_refs, out_refs, sems)
        for p in range(len(self.inputs)):
            for cp in recv[p]:
                cp.wait_recv()
        for p in range(len(self.inputs)):
            for cp in send[p]:
                cp.wait_send()
            local[p].wait()


class _GatherPlan:
    def __init__(self, shards, only=None):
        n = len(shards)
        self.n, self.only = n, only
        self.peers = (1, 2, 3) if only is None else (only,)
        self.inputs = list(shards)
        slabs = N_CHIPS if only is None else 1
        self.out_shape = [jax.ShapeDtypeStruct((slabs,) + s.shape, s.dtype) for s in shards]
        self.halved = [s.shape[0] % 32 == 0 for s in shards]
        self.sems = [pltpu.SemaphoreType.DMA((n, 3)) for _ in range(4)] + [pltpu.SemaphoreType.DMA((n,))]

    def _copies(self, in_refs, out_refs, sems):
        ici_s, ici_r, d2d_s, d2d_r, loc = sems
        x, y, c = _place()
        me = 2 * x + y

        def rows(p, core):
            if not self.halved[p]:
                return slice(None)
            half = self.inputs[p].shape[0] // 2
            return pl.ds(pl.multiple_of(core * half, 16), half)

        def slab(chip):
            return chip if self.only is None else 0

        def ici(p, d, chip, core):
            return pltpu.make_async_remote_copy(
                src_ref=in_refs[p].at[rows(p, core)], dst_ref=out_refs[p].at[slab(chip), rows(p, core)],
                send_sem=ici_s.at[p, d - 1], recv_sem=ici_r.at[p, d - 1],
                device_id=_chip_peer(x, y, c, d), device_id_type=MESH)

        def d2d(p, d, core):
            part = out_refs[p].at[slab(me ^ d), rows(p, core)]
            return pltpu.make_async_remote_copy(
                src_ref=part, dst_ref=part, send_sem=d2d_s.at[p, d - 1], recv_sem=d2d_r.at[p, d - 1],
                device_id=(x, y, 1 - c), device_id_type=MESH)

        local = [pltpu.make_async_copy(in_refs[p], out_refs[p].at[slab(me)], loc.at[p]) for p in range(self.n)]
        return me, c, ici, d2d, local

    def start(self, in_refs, out_refs, sems):
        me, c, ici, d2d, local = self._copies(in_refs, out_refs, sems)
        for p in range(self.n):
            if self.only is None:
                local[p].start()
            for d in self.peers:
                ici(p, d, me, c).start()

    def wait(self, in_refs, out_refs, sems):
        me, c, ici, d2d, local = self._copies(in_refs, out_refs, sems)
        for p in range(self.n):
            for d in self.peers:
                ici(p, d, me ^ d, c).wait_recv()
                if self.halved[p]:
                    d2d(p, d, c).start()
        for p in range(self.n):
            for d in self.peers:
                if self.halved[p]:
                    d2d(p, d, 1 - c).wait_recv()
                    d2d(p, d, c).wait_send()
                ici(p, d, me, c).wait_send()
            if self.only is None:
                local[p].wait()


def gather_plan(shards, only=None):
    return _GatherPlan(shards, only)


def reduce_plan(shards, whole=()):
    n_s = len(shards)

    def build(in_refs, out_refs, sems):
        send_sems, recv_sems, loc_sems = sems
        x, y, c = _place()
        me = 2 * x + y

        def src(p, slab):
            return in_refs[p].at[slab] if p < n_s else in_refs[p]

        def remote(p, d):
            return pltpu.make_async_remote_copy(
                src_ref=src(p, me ^ d), dst_ref=out_refs[p].at[d], send_sem=send_sems.at[p, d - 1],
                recv_sem=recv_sems.at[p, d - 1], device_id=_chip_peer(x, y, c, d), device_id_type=MESH)

        n = len(in_refs)
        send = [[remote(p, d) for d in (1, 2, 3)] for p in range(n)]
        local = [pltpu.make_async_copy(src(p, me), out_refs[p].at[0], loc_sems.at[p]) for p in range(n)]
        return send, send, local

    outs = [jax.ShapeDtypeStruct(s.shape, s.dtype) for s in shards]
    outs += [jax.ShapeDtypeStruct((N_CHIPS,) + a.shape, a.dtype) for a in whole]
    return _Plan(list(shards) + list(whole), outs, build)


class _SiblingPlan:
    def __init__(self, arrs):
        self.inputs = list(arrs)
        self.out_shape = [jax.ShapeDtypeStruct(a.shape, a.dtype) for a in arrs]
        n = len(arrs)
        self.sems = [pltpu.SemaphoreType.DMA((n,)), pltpu.SemaphoreType.DMA((n,))]

    def _copies(self, in_refs, out_refs, sems):
        x, y, c = _place()
        return [pltpu.make_async_remote_copy(
            src_ref=in_refs[p], dst_ref=out_refs[p], send_sem=sems[0].at[p], recv_sem=sems[1].at[p],
            device_id=(x, y, 1 - c), device_id_type=MESH) for p in range(len(self.inputs))]

    def start(self, in_refs, out_refs, sems):
        for cp in self._copies(in_refs, out_refs, sems):
            cp.start()

    def wait(self, in_refs, out_refs, sems):
        copies = self._copies(in_refs, out_refs, sems)
        for cp in copies:
            cp.wait_recv()
        for cp in copies:
            cp.wait_send()


def run_plan(plan, name):
    n = len(plan.inputs)

    def body(*refs):
        plan.start(refs[:n], refs[n:2 * n], refs[2 * n:])
        plan.wait(refs[:n], refs[n:2 * n], refs[2 * n:])

    return pl.pallas_call(body, name=name, in_specs=[ANY] * n, out_specs=[ANY] * n, out_shape=plan.out_shape,
                          scratch_shapes=plan.sems)(*plan.inputs)


def _call(body, plan, *, name, grid, in_specs, out_specs, out_shape, sem, scratch_shapes=(), aliases=None,
          n_prefetch=0):
    aliases = {} if aliases is None else aliases
    single = not isinstance(out_shape, (list, tuple))
    out_specs = [out_specs] if single else list(out_specs)
    out_shape = [out_shape] if single else list(out_shape)
    n_in, n_out, n_scr = len(in_specs), len(out_specs), len(scratch_shapes)
    ci = 0 if plan is None else len(plan.inputs)
    co = 0 if plan is None else len(plan.out_shape)

    def hosted(*refs):
        pre, refs = refs[:n_prefetch], refs[n_prefetch:]
        ins, cins = refs[:n_in], refs[n_in:n_in + ci]
        k = n_in + ci
        outs, couts = refs[k:k + n_out], refs[k + n_out:k + n_out + co]
        k += n_out + co
        scr, sems = refs[k:k + n_scr], refs[k + n_scr:]
        ids = [pl.program_id(a) for a in range(len(grid))]
        first = functools.reduce(jnp.logical_and, [i == 0 for i in ids])
        last = functools.reduce(jnp.logical_and, [i == g - 1 for i, g in zip(ids, grid)])

        @pl.when(first)
        def _():
            plan.start(cins, couts, sems)

        body(*pre, *ins, *outs, *scr)

        @pl.when(last)
        def _():
            plan.wait(cins, couts, sems)

    def run(*args):
        hosting = plan is not None
        spec = pltpu.PrefetchScalarGridSpec(
            num_scalar_prefetch=n_prefetch, grid=grid,
            in_specs=list(in_specs) + ([ANY] * ci if hosting else []),
            out_specs=out_specs + ([ANY] * co if hosting else []),
            scratch_shapes=list(scratch_shapes) + (plan.sems if hosting else []))
        res = pl.pallas_call(hosted if hosting else body, name=name, grid_spec=spec,
                             out_shape=out_shape + (plan.out_shape if hosting else []),
                             input_output_aliases=aliases, compiler_params=_cparams(sem),
                             )(*args, *(plan.inputs if hosting else []))
        return (res[0] if single else res[:n_out]), list(res[n_out:])

    return run


def norm_matmul(x, g, w, name, plan=None, tn=None):
    L, D = x.shape
    tl = min(TL_DW, L)
    if w.ndim == 3:
        nt, _, tn = w.shape
        w_spec = pl.BlockSpec((1, D, tn), lambda i, n: (n, 0, 0))
    else:
        nt = w.shape[1] // tn
        w_spec = pl.BlockSpec((D, tn), lambda i, n: (0, n))

    def body(x_ref, g_ref, w_ref, o_ref, h_ref):
        xv = x_ref[...]
        h = (xv * _rms(xv) * g_ref[...]).astype(h_ref.dtype)
        h_ref[...] = h
        o_ref[...] = _mm(h, w_ref[0] if w.ndim == 3 else w_ref[...])

    return _call(
        body, plan, name=name, grid=(L // tl, nt),
        in_specs=[pl.BlockSpec((tl, D), lambda i, n: (i, 0)), _full((1, D)), w_spec],
        out_specs=[pl.BlockSpec((tl, tn), lambda i, n: (i, n)), pl.BlockSpec((tl, D), lambda i, n: (i, 0))],
        out_shape=[jax.ShapeDtypeStruct((L, nt * tn), F32), jax.ShapeDtypeStruct((L, D), MXU_DTYPE)],
        sem=("arbitrary", "arbitrary"),
    )(x, g, w)


def even_in_slabs(x, xs, g, w, slabs, wsel, name, p_in=None, plan=None):
    L, D = x.shape
    tl = min(TL_DW, L)
    wb = EVEN_IN // N_CHIPS
    n = slabs.shape[0]
    s5_cols = 2 * S5_WIDTH - wb
    first = p_in is None

    def body(slabs_ref, wsel_ref, xs_ref, x_ref, g_ref, w_ref, *rest):
        o_ref = rest[-3] if first else rest[-1]
        j = slabs_ref[pl.program_id(0)]
        hs = (xs_ref[...] * _rms(xs_ref[...]) * g_ref[...]).astype(MXU_DTYPE)
        h = (x_ref[...] * _rms(x_ref[...]) * g_ref[...]).astype(MXU_DTYPE)
        if first:
            rest[-2][...] = hs
            rest[-1][...] = h
        o_ref[:, :s5_cols] = _mm(jnp.where(j <= 1, hs, h), w_ref[0, :, :s5_cols])
        o_ref[:, s5_cols:] = _mm(jnp.where(j == 0, hs, h), w_ref[0, :, s5_cols:])

    row = pl.BlockSpec((tl, D), lambda s, i, slabs_ref, wsel_ref: (i, 0))
    in_specs = [row if first else
                pl.BlockSpec((tl, D), lambda s, i, slabs_ref, wsel_ref: (jnp.where(slabs_ref[s] <= 1, i, 0), 0)),
                row if first else
                pl.BlockSpec((tl, D), lambda s, i, slabs_ref, wsel_ref: (jnp.where(slabs_ref[s] >= 1, i, 0), 0)),
                pl.BlockSpec((1, D), lambda s, i, slabs_ref, wsel_ref: (0, 0)),
                pl.BlockSpec((1, D, wb), lambda s, i, slabs_ref, wsel_ref: (wsel_ref[s], 0, 0))]
    out_specs = [pl.BlockSpec((tl, wb), lambda s, i, slabs_ref, wsel_ref: (i, slabs_ref[s]))]
    out_shape = [jax.ShapeDtypeStruct((L, EVEN_IN), F32)]
    args = [slabs, wsel, xs, x, g, w]
    if first:
        out_specs += [row, row]
        out_shape += [jax.ShapeDtypeStruct((L, D), MXU_DTYPE)] * 2
    else:
        in_specs.append(ANY)
        args.append(p_in)
    return _call(body, plan, name=name, grid=(n, L // tl), in_specs=in_specs, out_specs=out_specs,
                 out_shape=out_shape, sem=("arbitrary", "arbitrary"), n_prefetch=2,
                 aliases={} if first else {6: 0})(*args)


def matmul_residual(ys, w, x, name):
    L, D = x.shape
    tl = min(TL_PROJ, L)
    n = len(ys)
    offs = np.cumsum([0] + [y.shape[1] for y in ys])

    def body(*refs):
        y_refs, w_ref, x_ref, o_ref = refs[:n], refs[n], refs[n + 1], refs[n + 2]
        acc = x_ref[...]
        for k in range(n):
            acc = acc + _mm(y_refs[k][...], w_ref[offs[k]:offs[k + 1], :])
        o_ref[...] = acc

    return pl.pallas_call(
        body, name=name, grid=(L // tl,),
        in_specs=[pl.BlockSpec((tl, y.shape[1]), lambda i: (i, 0)) for y in ys]
        + [_full(w.shape), pl.BlockSpec((tl, D), lambda i: (i, 0))],
        out_specs=pl.BlockSpec((tl, D), lambda i: (i, 0)),
        out_shape=jax.ShapeDtypeStruct((L, D), F32),
        compiler_params=_cparams(("arbitrary",)),
    )(*ys, w, x)


def out_proj_loss(y, w, x, gf, tgt, name):
    L, K = y.shape
    D = w.shape[1]
    tl = min(TL_PROJ, L)

    def body(y_ref, w_ref, x_ref, gf_ref, t_ref, dx_ref, loss_ref, dg_ref):
        @pl.when(pl.program_id(0) == 0)
        def _():
            loss_ref[...] = jnp.zeros_like(loss_ref)
            dg_ref[...] = jnp.zeros_like(dg_ref)

        x2 = x_ref[...] + _mm(y_ref[...], w_ref[...])
        r = _rms(x2)
        xn = x2 * r
        e = xn * gf_ref[...] - t_ref[...]
        loss_ref[...] += (0.5 / D) * jnp.sum(e * e)
        dout = e * (1.0 / D)
        dg_ref[...] += jnp.sum(dout * xn, axis=0, keepdims=True)
        dxn = dout * gf_ref[...]
        dx_ref[...] = r * (dxn - xn * jnp.mean(dxn * xn, axis=-1, keepdims=True))

    return pl.pallas_call(
        body, name=name, grid=(L // tl,),
        in_specs=[pl.BlockSpec((tl, K), lambda i: (i, 0)), _full((K, D)),
                  pl.BlockSpec((tl, D), lambda i: (i, 0)), _full((1, D)),
                  pl.BlockSpec((tl, D), lambda i: (i, 0))],
        out_specs=[pl.BlockSpec((tl, D), lambda i: (i, 0)), _full((8, 128)), _full((1, D))],
        out_shape=[jax.ShapeDtypeStruct((L, D), F32), jax.ShapeDtypeStruct((8, 128), F32),
                   jax.ShapeDtypeStruct((1, D), F32)],
        compiler_params=_cparams(("arbitrary",)),
    )(y, w, x, gf, tgt)


def out_proj_bwd(dx, w, ys, name):
    L, D = dx.shape
    K = w.shape[0]
    tl = min(TL_PROJ, L)
    n = len(ys)
    offs = np.cumsum([0] + [y.shape[1] for y in ys])

    def body(*refs):
        dx_ref, w_ref, y_refs = refs[0], refs[1], refs[2:2 + n]
        dy_refs, dw_ref = refs[2 + n:2 + 2 * n], refs[2 + 2 * n]

        @pl.when(pl.program_id(0) == 0)
        def _():
            dw_ref[...] = jnp.zeros_like(dw_ref)

        dxv = dx_ref[...]
        for k in range(n):
            dy_refs[k][...] = _mm_nt(dxv, w_ref[offs[k]:offs[k + 1], :])
            dw_ref[offs[k]:offs[k + 1], :] += _mm_tn(y_refs[k][...], dxv)

    y_specs = [pl.BlockSpec((tl, y.shape[1]), lambda i: (i, 0)) for y in ys]
    return pl.pallas_call(
        body, name=name, grid=(L // tl,),
        in_specs=[pl.BlockSpec((tl, D), lambda i: (i, 0)), _full((K, D))] + y_specs,
        out_specs=y_specs + [_full((K, D))],
        out_shape=[jax.ShapeDtypeStruct(y.shape, F32) for y in ys] + [jax.ShapeDtypeStruct((K, D), F32)],
        compiler_params=_cparams(("arbitrary",)),
    )(dx, w, *ys)


def in_proj_bwd_dx(x, g, dps, ws, dres, name, plan=None):
    L, D = x.shape
    tl = min(TL_PROJ, L)
    n = len(dps)

    def body(*refs):
        x_ref, g_ref, dres_ref = refs[:3]
        dp_refs, w_refs = refs[3:3 + n], refs[3 + n:3 + 2 * n]
        dx_ref, dg_ref = refs[3 + 2 * n:]

        @pl.when(pl.program_id(0) == 0)
        def _():
            dg_ref[...] = jnp.zeros_like(dg_ref)

        dh = None
        for dp_ref, w_ref, w in zip(dp_refs, w_refs, ws):
            if w.ndim == 3:
                tn = w.shape[2]
                parts = [_mm_nt(dp_ref[:, tn * k:tn * (k + 1)], w_ref[k]) for k in range(w.shape[0])]
            else:
                parts = [_mm_nt(dp_ref[...], w_ref[...])]
            for part in parts:
                dh = part if dh is None else dh + part
        xv = x_ref[...]
        r = _rms(xv)
        xn = xv * r
        dg_ref[...] += jnp.sum(dh * xn, axis=0, keepdims=True)
        dxn = dh * g_ref[...]
        dx_ref[...] = dres_ref[...] + r * (dxn - xn * jnp.mean(dxn * xn, axis=-1, keepdims=True))

    return _call(
        body, plan, name=name, grid=(L // tl,),
        in_specs=[pl.BlockSpec((tl, D), lambda i: (i, 0)), _full((1, D)), pl.BlockSpec((tl, D), lambda i: (i, 0))]
        + [pl.BlockSpec((tl, dp.shape[1]), lambda i: (i, 0)) for dp in dps] + [_full(w.shape) for w in ws],
        out_specs=[pl.BlockSpec((tl, D), lambda i: (i, 0)), _full((1, D))],
        out_shape=[jax.ShapeDtypeStruct((L, D), F32), jax.ShapeDtypeStruct((1, D), F32)],
        sem=("arbitrary",),
    )(x, g, dres, *dps, *ws)


def even_in_bwd_dx(x, g, dpa, dpb, wsh, sel, dres, name, plan=None):
    L, D = x.shape
    tl = min(TL_PROJ, L)
    wb = wsh.shape[2]
    na, nb = dpa.shape[1], dpb.shape[1]

    def body(sel_ref, x_ref, g_ref, dres_ref, dpa_ref, dpb_ref, w_ref, dx_ref, dg_ref):
        @pl.when(pl.program_id(0) == 0)
        def _():
            dg_ref[...] = jnp.zeros_like(dg_ref)

        w1 = w_ref.at[sel_ref[1]]
        dh = _mm_nt(dpa_ref[:, :wb], w_ref[sel_ref[0]])
        dh += _mm_nt(dpa_ref[:, wb:], w1[:, :na - wb])
        dh += _mm_nt(dpb_ref[:, :wb], w_ref[sel_ref[2]])
        dh += _mm_nt(dpb_ref[:, wb:2 * wb], w_ref[sel_ref[3]])
        dh += _mm_nt(dpb_ref[:, 2 * wb:], w1[:, na - wb:])
        xv = x_ref[...]
        r = _rms(xv)
        xn = xv * r
        dg_ref[...] += jnp.sum(dh * xn, axis=0, keepdims=True)
        dxn = dh * g_ref[...]
        dx_ref[...] = dres_ref[...] + r * (dxn - xn * jnp.mean(dxn * xn, axis=-1, keepdims=True))

    row = lambda n: pl.BlockSpec((tl, n), lambda i, sel_ref: (i, 0))
    whole = lambda shape: pl.BlockSpec(shape, lambda i, sel_ref: (0,) * len(shape))
    return _call(
        body, plan, name=name, grid=(L // tl,),
        in_specs=[row(D), whole((1, D)), row(D), row(na), row(nb), whole(wsh.shape)],
        out_specs=[row(D), whole((1, D))],
        out_shape=[jax.ShapeDtypeStruct((L, D), F32), jax.ShapeDtypeStruct((1, D), F32)],
        sem=("arbitrary",), n_prefetch=1,
    )(sel, x, g, dres, dpa, dpb, wsh)


def in_proj_bwd_dw(h, dp, name, tn, first=0, into=None, dtype=F32, plan=None, dp_first=0, count=None):
    L, D = h.shape
    tl = min(TL_DW, L)
    wb = EVEN_IN // N_CHIPS
    per = wb // tn
    count = dp.shape[1] // tn if count is None else count
    last = L // tl - 1

    def body(*refs):
        h_ref, dp_ref, dw_ref, acc = refs[0], refs[1], refs[-2], refs[-1]

        @pl.when(pl.program_id(1) == 0)
        def _():
            acc[...] = jnp.zeros_like(acc)

        acc[...] += _mm_tn(h_ref[...], dp_ref[...])

        @pl.when(pl.program_id(1) == last)
        def _():
            dw_ref[0] = acc[...].astype(dw_ref.dtype)

    ins = [h, dp] + ([] if into is None else [into])
    return _call(
        body, plan, name=name, grid=(count, L // tl),
        in_specs=[pl.BlockSpec((tl, D), lambda n, i: (i, 0)), pl.BlockSpec((tl, tn), lambda n, i: (i, n + dp_first))]
        + ([] if into is None else [ANY]),
        out_specs=pl.BlockSpec((1, D, tn), lambda n, i: ((n + first) // per, 0, (n + first) % per)),
        out_shape=jax.ShapeDtypeStruct((N_CHIPS, D, wb), dtype),
        scratch_shapes=[pltpu.VMEM((D, tn), F32)],
        aliases={} if into is None else {2: 0},
        sem=("arbitrary", "arbitrary"),
    )(*ins)


def _s5_param_fn(lam_re, lam_im, log_dt, b_re, b_im):
    lr = jnp.minimum(lam_re, -1e-4)
    li = lam_im
    dt = jnp.exp(log_dt)
    mag = jnp.exp(lr * dt)
    ab_re = mag * jnp.cos(li * dt)
    ab_im = mag * jnp.sin(li * dt)
    den = lr * lr + li * li
    n_re = ab_re - 1.0
    n_im = ab_im
    z_re = (n_re * lr + n_im * li) / den
    z_im = (n_im * lr - n_re * li) / den
    bb_re = z_re[None] * b_re - z_im[None] * b_im
    bb_im = z_re[None] * b_im + z_im[None] * b_re
    return ab_re, ab_im, bb_re, bb_im


def s5_params_fwd(lam_re, lam_im, log_dt, b_re, b_im, span):
    G, P = lam_re.shape
    H = b_re.shape[0]
    assert span & (span - 1) == 0

    def body(lr_ref, li_ref, dt_ref, br_ref, bi_ref, abr_ref, abi_ref, bbr_ref, bbi_ref, pr_ref, pi_ref):
        ab_re, ab_im, bb_re, bb_im = _s5_param_fn(lr_ref[...], li_ref[...], dt_ref[...], br_ref[...], bi_ref[...])
        abr_ref[...] = ab_re
        abi_ref[...] = ab_im
        bbr_ref[...] = bb_re
        bbi_ref[...] = bb_im
        cr, ci = ab_re, ab_im
        for _ in range(span.bit_length() - 1):
            cr, ci = cr * cr - ci * ci, 2.0 * cr * ci
        pr_ref[...] = cr
        pi_ref[...] = ci

    shp = lambda *s: jax.ShapeDtypeStruct(s, F32)
    return pl.pallas_call(
        body, name="s5_params_fwd",
        out_shape=[shp(G, P), shp(G, P), shp(H, G, P), shp(H, G, P), shp(G, P), shp(G, P)],
    )(lam_re, lam_im, log_dt, b_re, b_im)


def s5_params_bwd(lam_re, lam_im, log_dt, b_re, b_im, d_ab_re, d_ab_im, d_bb_re, d_bb_im):
    G, P = lam_re.shape
    H = b_re.shape[0]

    def body(lr_ref, li_ref, dt_ref, br_ref, bi_ref, g0, g1, g2, g3, o0, o1, o2, o3, o4):
        prim = (lr_ref[...], li_ref[...], dt_ref[...], br_ref[...], bi_ref[...])
        _, vjp = jax.vjp(_s5_param_fn, *prim)
        d = vjp((jnp.sum(g0[...], axis=0), jnp.sum(g1[...], axis=0), g2[...], g3[...]))
        o0[...], o1[...], o2[...], o3[...], o4[...] = d

    shp = lambda *s: jax.ShapeDtypeStruct(s, F32)
    return pl.pallas_call(
        body, name="s5_params_bwd",
        out_shape=[shp(G, P), shp(G, P), shp(G, 1), shp(H, G, P), shp(H, G, P)],
    )(lam_re, lam_im, log_dt, b_re, b_im, d_ab_re, d_ab_im, d_bb_re, d_bb_im)


def stream_order(a, tl):
    L, C = a.shape
    return a.reshape(L // tl, 8, tl // 8, C).transpose(0, 2, 1, 3).reshape(L, C)


def token_order(a, tl):
    L, C = a.shape
    return a.reshape(L // tl, tl // 8, 8, C).transpose(0, 2, 1, 3).reshape(L, C)


_LANE_BLK = 1024
_LANE_BLK_BWD = 1024


def _cmul_add(ar, ai, xr, xi, br, bi):
    return br + (ar * xr - ai * xi), bi + (ar * xi + ai * xr)


def _cmulc_add(ar, ai, xr, xi, br, bi):
    return br + (ar * xr + ai * xi), bi + (ar * xi - ai * xr)


def _s5_states(u, wbd_ref, a_re, a_im, at_re, at_im, s_re, s_im, e_re, e_im, c0_re, c0_im, tl):
    t8 = tl // 8
    for k in range(S5_KBLK):
        bu = _mm(u[:, 128 * k:128 * (k + 1)], wbd_ref[k])
        s_re[:, 512 * k:512 * (k + 1)] = bu[:, :512]
        s_im[:, 512 * k:512 * (k + 1)] = bu[:, 512:]
    outs_re, outs_im = [], []
    for b in range(S5_LANES // _LANE_BLK):
        lanes = slice(_LANE_BLK * b, _LANE_BLK * (b + 1))
        ar = jnp.broadcast_to(a_re[:, lanes], (8, _LANE_BLK))
        ai = jnp.broadcast_to(a_im[:, lanes], (8, _LANE_BLK))

        def local(i, carry, lanes=lanes, ar=ar, ai=ai):
            r = pl.multiple_of(i * 8, 8)
            sr, si = _cmul_add(ar, ai, carry[0], carry[1], s_re[pl.ds(r, 8), lanes], s_im[pl.ds(r, 8), lanes])
            s_re[pl.ds(r, 8), lanes] = sr
            s_im[pl.ds(r, 8), lanes] = si
            return sr, si

        zero = jnp.zeros((8, _LANE_BLK), F32)
        fr, fi = lax.fori_loop(0, t8, local, (zero, zero), unroll=True)
        tr, ti = at_re[:, lanes], at_im[:, lanes]
        er, ei = c0_re[:, lanes], c0_im[:, lanes]
        ers, eis = [er], [ei]
        for j in range(8):
            er, ei = _cmul_add(tr, ti, er, ei, fr[j:j + 1], fi[j:j + 1])
            ers.append(er)
            eis.append(ei)
        outs_re.append(ers[8])
        outs_im.append(eis[8])
        ent_r, ent_i = jnp.concatenate(ers[:8], axis=0), jnp.concatenate(eis[:8], axis=0)
        e_re[:, lanes] = ent_r
        e_im[:, lanes] = ent_i

        def fix(i, carry, lanes=lanes, ar=ar, ai=ai):
            r = pl.multiple_of(i * 8, 8)
            zr, zi = ar * carry[0] - ai * carry[1], ar * carry[1] + ai * carry[0]
            s_re[pl.ds(r, 8), lanes] = s_re[pl.ds(r, 8), lanes] + zr
            s_im[pl.ds(r, 8), lanes] = s_im[pl.ds(r, 8), lanes] + zi
            return zr, zi

        lax.fori_loop(0, t8, fix, (ent_r, ent_i), unroll=True)
    return jnp.concatenate(outs_re, axis=1), jnp.concatenate(outs_im, axis=1)


def _s5_readout(s_re, s_im, cre_ref, cim_ref):
    ys = []
    for k in range(S5_KBLK):
        lanes = slice(512 * k, 512 * (k + 1))
        ys.append(_mm(s_re[:, lanes], cre_ref[k]) - _mm(s_im[:, lanes], cim_ref[k]))
    return jnp.concatenate(ys, axis=1)


def s5_forward(p, wbd, cre, cim, atab, d_skip, w_glu, b_glu, plan=None):
    L = p.shape[0]
    tl = min(TL_S5, L)
    nch = L // tl

    def body(u_ref, z_ref, wbd_ref, cre_ref, cim_ref, at_ref, d_ref, wg_ref, bg_ref,
             ya_ref, st_re_ref, st_im_ref, sv_re_ref, sv_im_ref, s_re, s_im, e_re, e_im, car_re, car_im):
        @pl.when(pl.program_id(0) == 0)
        def _():
            car_re[...] = jnp.zeros_like(car_re)
            car_im[...] = jnp.zeros_like(car_im)

        c0_re, c0_im = car_re[...], car_im[...]
        st_re_ref[0] = c0_re
        st_im_ref[0] = c0_im
        u = u_ref[...]
        x_re, x_im = _s5_states(u, wbd_ref, at_ref[0:1], at_ref[1:2], at_ref[2:3], at_ref[3:4],
                                s_re, s_im, e_re, e_im, c0_re, c0_im, tl)
        car_re[...] = x_re
        car_im[...] = x_im
        sv_re_ref[...] = s_re[...].astype(sv_re_ref.dtype)
        sv_im_ref[...] = s_im[...].astype(sv_im_ref.dtype)
        y = _s5_readout(sv_re_ref, sv_im_ref, cre_ref, cim_ref) + d_ref[...] * u
        yg = _gelu(y)
        gate = _sigmoid(_mm(yg, wg_ref[...]) + bg_ref[...])
        sz, _ = _silu_and_grad(z_ref[...])
        ya_ref[...] = (yg * gate * sz).astype(ya_ref.dtype)

    return _call(
        body, plan, name="s5_forward", grid=(nch,),
        in_specs=[pl.BlockSpec((tl, 1024), lambda i: (i, 0)), pl.BlockSpec((tl, 1024), lambda i: (i, 1)),
                  _full(wbd.shape), _full(cre.shape), _full(cim.shape), _full(atab.shape),
                  _full((1, 1024)), _full((1024, 1024)), _full((1, 1024))],
        out_specs=[pl.BlockSpec((tl, 1024), lambda i: (i, 0)),
                   pl.BlockSpec((1, 1, S5_LANES), lambda i: (i, 0, 0)),
                   pl.BlockSpec((1, 1, S5_LANES), lambda i: (i, 0, 0)),
                   pl.BlockSpec((tl, S5_LANES), lambda i: (i, 0)), pl.BlockSpec((tl, S5_LANES), lambda i: (i, 0))],
        out_shape=[jax.ShapeDtypeStruct((L, 1024), MXU_DTYPE),
                   jax.ShapeDtypeStruct((nch, 1, S5_LANES), F32), jax.ShapeDtypeStruct((nch, 1, S5_LANES), F32),
                   jax.ShapeDtypeStruct((L, S5_LANES), MXU_DTYPE), jax.ShapeDtypeStruct((L, S5_LANES), MXU_DTYPE)],
        scratch_shapes=[pltpu.VMEM((tl, S5_LANES), F32), pltpu.VMEM((tl, S5_LANES), F32),
                        pltpu.VMEM((8, S5_LANES), F32), pltpu.VMEM((8, S5_LANES), F32),
                        pltpu.VMEM((1, S5_LANES), F32), pltpu.VMEM((1, S5_LANES), F32)],
        sem=("arbitrary",),
    )(p, p, wbd, cre, cim, atab, d_skip, w_glu, b_glu)


def s5_backward(p, dya, st_re, st_im, sv_re, sv_im, wbd, cre, cim, atab, d_skip, w_glu, b_glu, plan=None):
    L = p.shape[0]
    tl = min(TL_S5, L)
    t8 = tl // 8
    nch = L // tl
    rev = lambda i: (nch - 1 - i, 0)
    rev1 = lambda i: (nch - 1 - i, 1)
    rev3 = lambda i: (nch - 1 - i, 0, 0)
    ct_shape = (S5_KBLK, cre.shape[2], cre.shape[1])

    def body(u_ref, z_ref, dya_ref, str_ref, sti_ref, s_re, s_im, wbd_ref, cre_ref, cim_ref, at_ref,
             d_ref, wg_ref, bg_ref,
             dp_ref, dwbd_ref, dcre_ref, dcim_ref, dabr_ref, dabi_ref, dd_ref, dwg_ref, dbg_ref,
             g_re, g_im, car_re, car_im):
        @pl.when(pl.program_id(0) == 0)
        def _():
            car_re[...] = jnp.zeros_like(car_re)
            car_im[...] = jnp.zeros_like(car_im)
            for r in (dwbd_ref, dcre_ref, dcim_ref, dabr_ref, dabi_ref, dd_ref, dwg_ref, dbg_ref):
                r[...] = jnp.zeros_like(r)

        u = u_ref[...]
        a_re, a_im, at_re, at_im = at_ref[0:1], at_ref[1:2], at_ref[2:3], at_ref[3:4]
        y = _s5_readout(s_re, s_im, cre_ref, cim_ref) + d_ref[...] * u
        yg, dyg = _gelu_and_grad(y)
        gate = _sigmoid(_mm(yg, wg_ref[...]) + bg_ref[...])
        sz, dsz = _silu_and_grad(z_ref[...])
        dya = dya_ref[...]
        s5out = yg * gate
        dp_ref[:, 1024:] = (dya * s5out * dsz).astype(dp_ref.dtype)
        ds5 = dya * sz
        dt = ds5 * yg * gate * (1.0 - gate)
        dwg_ref[...] += _mm_tn(yg, dt)
        dbg_ref[...] += jnp.sum(dt, axis=0, keepdims=True)
        dyv = (ds5 * gate + _mm_nt(dt, wg_ref[...])) * dyg
        dd_ref[...] += jnp.sum(dyv * u, axis=0, keepdims=True)

        for k in range(S5_KBLK):
            lanes = slice(512 * k, 512 * (k + 1))
            dyk = dyv[:, 128 * k:128 * (k + 1)]
            g_re[:, lanes] = _mm_nt(dyk, cre_ref[k])
            g_im[:, lanes] = -_mm_nt(dyk, cim_ref[k])
            dcre_ref[k] += _mm_tn(dyk, s_re[:, lanes])
            dcim_ref[k] -= _mm_tn(dyk, s_im[:, lanes])

        blk = _LANE_BLK_BWD
        for b in range(S5_LANES // blk):
            lanes = slice(blk * b, blk * (b + 1))
            ar = jnp.broadcast_to(a_re[:, lanes], (8, blk))
            ai = jnp.broadcast_to(a_im[:, lanes], (8, blk))

            def local(j, carry, lanes=lanes, ar=ar, ai=ai):
                r = pl.multiple_of((t8 - 1 - j) * 8, 8)
                gr, gi = _cmulc_add(ar, ai, carry[0], carry[1], g_re[pl.ds(r, 8), lanes], g_im[pl.ds(r, 8), lanes])
                g_re[pl.ds(r, 8), lanes] = gr
                g_im[pl.ds(r, 8), lanes] = gi
                return gr, gi

            zero = jnp.zeros((8, blk), F32)
            fr, fi = lax.fori_loop(0, t8, local, (zero, zero), unroll=True)
            tr, ti = at_re[:, lanes], at_im[:, lanes]
            hr, hi = car_re[:, lanes], car_im[:, lanes]
            hrs, his = [hr], [hi]
            for j in range(7, -1, -1):
                hr, hi = _cmulc_add(tr, ti, hr, hi, fr[j:j + 1], fi[j:j + 1])
                hrs.append(hr)
                his.append(hi)
            car_re[:, lanes] = hrs[8]
            car_im[:, lanes] = his[8]
            in_r = jnp.concatenate(hrs[7::-1], axis=0)
            in_i = jnp.concatenate(his[7::-1], axis=0)

            wr, wi, nr, ni, accr, acci = in_r, in_i, zero, zero, zero, zero
            for pair in range(t8 // 2 - 1, -1, -1):
                rows = slice(16 * pair, 16 * pair + 16)
                s16r, s16i = s_re[rows, lanes].astype(F32), s_im[rows, lanes].astype(F32)
                for half in (1, 0):
                    r = 16 * pair + 8 * half
                    sr, si = s16r[8 * half:8 * half + 8], s16i[8 * half:8 * half + 8]
                    accr, acci = accr + (sr * nr + si * ni), acci + (sr * ni - si * nr)
                    wr, wi = ar * wr + ai * wi, ar * wi - ai * wr
                    nr, ni = g_re[r:r + 8, lanes] + wr, g_im[r:r + 8, lanes] + wi
                    g_re[r:r + 8, lanes] = nr
                    g_im[r:r + 8, lanes] = ni
            lr, li = s_re[tl - 16:tl, lanes].astype(F32)[8:], s_im[tl - 16:tl, lanes].astype(F32)[8:]
            row0 = lax.broadcasted_iota(jnp.int32, (8, blk), 0) == 0
            sr = jnp.where(row0, jnp.broadcast_to(str_ref[0][:, lanes], (8, blk)), pltpu.roll(lr, 1, 0))
            si = jnp.where(row0, jnp.broadcast_to(sti_ref[0][:, lanes], (8, blk)), pltpu.roll(li, 1, 0))
            dabr_ref[:, lanes] += accr + (sr * nr + si * ni)
            dabi_ref[:, lanes] += acci + (sr * ni - si * nr)

        dus = []
        for k in range(S5_KBLK):
            lanes = slice(512 * k, 512 * (k + 1))
            g = jnp.concatenate([g_re[:, lanes], g_im[:, lanes]], axis=1)
            dwbd_ref[k] += _mm_tn(u[:, 128 * k:128 * (k + 1)], g)
            dus.append(_mm_nt(g, wbd_ref[k]))
        du = jnp.concatenate(dus, axis=1) + dyv * d_ref[...]
        dp_ref[:, :1024] = du.astype(dp_ref.dtype)

    shp = lambda *s: jax.ShapeDtypeStruct(s, F32)
    return _call(
        body, plan, name="s5_backward", grid=(nch,),
        in_specs=[pl.BlockSpec((tl, 1024), rev), pl.BlockSpec((tl, 1024), rev1), pl.BlockSpec((tl, 1024), rev),
                  pl.BlockSpec((1, 1, S5_LANES), rev3), pl.BlockSpec((1, 1, S5_LANES), rev3),
                  pl.BlockSpec((tl, S5_LANES), rev), pl.BlockSpec((tl, S5_LANES), rev),
                  _full(wbd.shape), _full(cre.shape), _full(cim.shape), _full(atab.shape),
                  _full((1, 1024)), _full((1024, 1024)), _full((1, 1024))],
        out_specs=[pl.BlockSpec((tl, 2048), rev), _full(wbd.shape), _full(ct_shape), _full(ct_shape),
                   _full((8, S5_LANES)), _full((8, S5_LANES)), _full((1, 1024)), _full((1024, 1024)), _full((1, 1024))],
        out_shape=[jax.ShapeDtypeStruct((L, 2048), MXU_DTYPE), shp(*wbd.shape), shp(*ct_shape), shp(*ct_shape),
                   shp(8, S5_LANES), shp(8, S5_LANES), shp(1, 1024), shp(1024, 1024), shp(1, 1024)],
        scratch_shapes=[pltpu.VMEM((tl, S5_LANES), F32), pltpu.VMEM((tl, S5_LANES), F32),
                        pltpu.VMEM((1, S5_LANES), F32), pltpu.VMEM((1, S5_LANES), F32)],
        sem=("arbitrary",),
    )(p, p, dya, st_re, st_im, sv_re, sv_im, wbd, cre, cim, atab, d_skip, w_glu, b_glu)


def _block_diag(w, rows_first):
    g8 = w.reshape(S5_KBLK, 8, w.shape[1], w.shape[2])
    eye = jnp.eye(8, dtype=w.dtype)
    out = jnp.einsum('kgab,fg->kfagb', g8, eye)
    return out.reshape(S5_KBLK, 8 * w.shape[1], 8 * w.shape[2])


def _block_diag_extract(wbd, a, b):
    w5 = wbd.reshape(S5_KBLK, 8, a, 8, b)
    idx = jnp.arange(8)
    return w5[:, idx, :, idx, :].transpose(1, 0, 2, 3).reshape(S5_GROUPS, a, b)


def _ret_constants():
    log_g = np.log1p(-np.exp2(-5.0 - np.arange(RET_HEADS, dtype=np.float32))).astype(np.float32)
    idx = np.arange(RET_CHUNK, dtype=np.float32)
    diff = idx[:, None] - idx[None, :]
    decay = np.where(diff >= 0, np.exp(log_g[:, None, None] * np.maximum(diff, 0.0)), 0.0).astype(np.float32)
    xi = np.exp(log_g[None, :] * (idx[:, None] + 1.0)).astype(np.float32)
    zeta = np.exp(log_g[None, :] * (RET_CHUNK - 1.0 - idx[:, None])).astype(np.float32)
    chunk_decay = np.exp(log_g * RET_CHUNK).astype(np.float32)
    return decay, xi, zeta, chunk_decay


def _rope_tables(L):
    half = RET_DK // 2
    inv = ROPE_BASE ** (-jnp.arange(half, dtype=F32) / half)
    ang = jnp.arange(L, dtype=F32)[:, None] * inv[None, :]
    return jnp.cos(ang), jnp.sin(ang)


def _rot(xh, cos, sin):
    x1, x2 = xh[:, :128], xh[:, 128:]
    return jnp.concatenate([x1 * cos - x2 * sin, x1 * sin + x2 * cos], axis=1)


def _rot_t(dh, cos, sin):
    d1, d2 = dh[:, :128], dh[:, 128:]
    return jnp.concatenate([d1 * cos + d2 * sin, d2 * cos - d1 * sin], axis=1)


RET_PER_STEP = 4


def _ret_setup(L):
    nc = L // RET_CHUNK
    per = RET_PER_STEP if nc % RET_PER_STEP == 0 else 1
    decay_np, xi_np, zeta_np, cd_np = _ret_constants()
    tables = (jnp.asarray(decay_np), jnp.asarray(np.tile(xi_np, (per, 1))), jnp.asarray(np.tile(zeta_np, (per, 1))))
    return nc // per, per, tables, [float(c) for c in cd_np]


def _ret_rows(q_ref, k_ref, v_ref, cos_ref, sin_ref, xi_ref, zeta_ref):
    H = range(RET_HEADS)
    hs = [slice(RET_DK * h, RET_DK * (h + 1)) for h in H]
    cs, sn = cos_ref[...], sin_ref[...]
    qh = [_rot(q_ref[:, hs[h]], cs, sn) for h in H]
    kh = [_rot(k_ref[:, hs[h]], cs, sn) * (RET_DK ** -0.5) for h in H]
    vh = [v_ref[:, hs[h]] for h in H]
    qx = [qh[h] * xi_ref[:, h:h + 1] for h in H]
    kz = [kh[h] * zeta_ref[:, h:h + 1] for h in H]
    return hs, cs, sn, qh, kh, vh, qx, kz


def _ret_normed(qh, kh, vh, qx, dec_ref, prevs, per):
    H, C = range(RET_HEADS), range(per)
    rs = [slice(RET_CHUNK * c, RET_CHUNK * (c + 1)) for c in C]
    sc = [[_mm_nt(qh[h][rs[c]], kh[h][rs[c]]) * dec_ref[h] for h in H] for c in C]
    inner = [[_mm(sc[c][h], vh[h][rs[c]]) for h in H] for c in C]
    cross = [[_mm(qx[h][rs[c]], prevs[c][h]) for h in H] for c in C]
    o = [jnp.concatenate([inner[c][h] + cross[c][h] for c in C], axis=0) for h in H]
    oc = [o[h] - jnp.mean(o[h], axis=-1, keepdims=True) for h in H]
    rstd = [lax.rsqrt(jnp.mean(oc[h] * oc[h], axis=-1, keepdims=True) + NORM_EPS) for h in H]
    on = [oc[h] * rstd[h] for h in H]
    return rs, sc, rstd, on


def retention_forward(p, cos, sin, gain):
    L = p.shape[0]
    steps, per, (decay, xi, zeta), cd = _ret_setup(L)
    rows = RET_CHUNK * per

    def body(q_ref, k_ref, v_ref, z_ref, cos_ref, sin_ref, dec_ref, xi_ref, zeta_ref, gain_ref,
             yb_ref, prev_ref, state):
        @pl.when(pl.program_id(0) == 0)
        def _():
            state[...] = jnp.zeros_like(state)

        H, C = range(RET_HEADS), range(per)
        hs, cs, sn, qh, kh, vh, qx, kz = _ret_rows(q_ref, k_ref, v_ref, cos_ref, sin_ref, xi_ref, zeta_ref)
        prevs = [[state[h] for h in H]]
        for c in C:
            rs_c = slice(RET_CHUNK * c, RET_CHUNK * (c + 1))
            prevs.append([prevs[c][h] * cd[h] + _mm_tn(kz[h][rs_c], vh[h][rs_c]) for h in H])
        _, _, _, on = _ret_normed(qh, kh, vh, qx, dec_ref, prevs, per)
        sz, _ = _silu_and_grad(z_ref[...])
        for h in H:
            for c in C:
                prev_ref[c, h] = prevs[c][h].astype(prev_ref.dtype)
            state[h] = prevs[per][h]
            yb_ref[:, hs[h]] = (on[h] * gain_ref[:, hs[h]] * sz[:, hs[h]]).astype(yb_ref.dtype)

    col0 = p.shape[1] // 1024 - 4
    blk = lambda c: pl.BlockSpec((rows, 1024), lambda i, c=c: (i, c + col0))
    return pl.pallas_call(
        body, name="retention_forward", grid=(steps,),
        in_specs=[blk(0), blk(1), blk(2), blk(3),
                  pl.BlockSpec((rows, 128), lambda i: (i, 0)), pl.BlockSpec((rows, 128), lambda i: (i, 0)),
                  _full(decay.shape), _full(xi.shape), _full(zeta.shape), _full((1, 1024))],
        out_specs=[pl.BlockSpec((rows, 1024), lambda i: (i, 0)),
                   pl.BlockSpec((per, RET_HEADS, RET_DK, RET_DK), lambda i: (i, 0, 0, 0))],
        out_shape=[jax.ShapeDtypeStruct((L, 1024), MXU_DTYPE),
                   jax.ShapeDtypeStruct((steps * per, RET_HEADS, RET_DK, RET_DK), MXU_DTYPE)],
        scratch_shapes=[pltpu.VMEM((RET_HEADS, RET_DK, RET_DK), F32)],
        compiler_params=_cparams(("arbitrary",)),
    )(p, p, p, p, cos, sin, decay, xi, zeta, gain)


def retention_backward(p, dy, prevs, cos, sin, gain, plan=None):
    L = p.shape[0]
    steps, per, (decay, xi, zeta), cd = _ret_setup(L)
    rows = RET_CHUNK * per
    scale = RET_DK ** -0.5

    def body(q_ref, k_ref, v_ref, z_ref, dyb_ref, prev_ref, cos_ref, sin_ref, dec_ref, xi_ref, zeta_ref, gain_ref,
             dp_ref, dgain_ref, dstate):
        @pl.when(pl.program_id(0) == 0)
        def _():
            dstate[...] = jnp.zeros_like(dstate)
            dgain_ref[...] = jnp.zeros_like(dgain_ref)

        H, C = range(RET_HEADS), range(per)
        hs, cs, sn, qh, kh, vh, qx, kz = _ret_rows(q_ref, k_ref, v_ref, cos_ref, sin_ref, xi_ref, zeta_ref)
        prevs = [[prev_ref[c, h] for h in H] for c in C]
        rs, sc, rstd, on = _ret_normed(qh, kh, vh, qx, dec_ref, prevs, per)
        sz, dsz = _silu_and_grad(z_ref[...])
        dyb = dyb_ref[...]
        dong = [dyb[:, hs[h]] * sz[:, hs[h]] for h in H]
        don = [dong[h] * gain_ref[:, hs[h]] for h in H]
        do = [rstd[h] * (don[h] - jnp.mean(don[h], axis=-1, keepdims=True)
                         - on[h] * jnp.mean(don[h] * on[h], axis=-1, keepdims=True)) for h in H]
        dsc = [[_mm_nt(do[h][rs[c]], vh[h][rs[c]]) * dec_ref[h] for h in H] for c in C]
        dq_st = [[_mm_nt(do[h][rs[c]], prevs[c][h]) for h in H] for c in C]
        dnew = [[_mm_tn(qx[h][rs[c]], do[h][rs[c]]) for h in H] for c in C]
        dsts = [None] * per + [[dstate[h] for h in H]]
        for c in reversed(C):
            dsts[c] = [dsts[c + 1][h] * cd[h] + dnew[c][h] for h in H]
        dk_st = [[_mm_nt(vh[h][rs[c]], dsts[c + 1][h]) for h in H] for c in C]
        dv_st = [[_mm(kz[h][rs[c]], dsts[c + 1][h]) for h in H] for c in C]
        rows_of = lambda parts: jnp.concatenate(parts, axis=0)
        dqh = [rows_of([_mm(dsc[c][h], kh[h][rs[c]]) for c in C])
               + rows_of([dq_st[c][h] for c in C]) * xi_ref[:, h:h + 1] for h in H]
        dkh = [rows_of([_mm_tn(dsc[c][h], qh[h][rs[c]]) for c in C])
               + rows_of([dk_st[c][h] for c in C]) * zeta_ref[:, h:h + 1] for h in H]
        dvh = [rows_of([_mm_tn(sc[c][h], do[h][rs[c]]) + dv_st[c][h] for c in C]) for h in H]
        for h in H:
            dstate[h] = dsts[0][h]
            dgain_ref[:, hs[h]] += jnp.sum(dong[h] * on[h], axis=0, keepdims=True)
            dp_ref[:, hs[h]] = (_rot_t(dkh[h], cs, sn) * scale).astype(dp_ref.dtype)
            dp_ref[:, 1024 + RET_DK * h:1024 + RET_DK * (h + 1)] = dvh[h].astype(dp_ref.dtype)
            dp_ref[:, 2048 + RET_DK * h:2048 + RET_DK * (h + 1)] = (
                dyb[:, hs[h]] * on[h] * gain_ref[:, hs[h]] * dsz[:, hs[h]]).astype(dp_ref.dtype)
            dp_ref[:, 3072 + RET_DK * h:3072 + RET_DK * (h + 1)] = _rot_t(dqh[h], cs, sn).astype(dp_ref.dtype)

    col0 = p.shape[1] // 1024 - 4
    blk = lambda c: pl.BlockSpec((rows, 1024), lambda i, c=c: (steps - 1 - i, c + col0))
    tab = pl.BlockSpec((rows, 128), lambda i: (steps - 1 - i, 0))
    return _call(
        body, plan, name="retention_backward", grid=(steps,),
        in_specs=[blk(0), blk(1), blk(2), blk(3), pl.BlockSpec((rows, 1024), lambda i: (steps - 1 - i, 0)),
                  pl.BlockSpec((per, RET_HEADS, RET_DK, RET_DK), lambda i: (steps - 1 - i, 0, 0, 0)),
                  tab, tab, _full(decay.shape), _full(xi.shape), _full(zeta.shape), _full((1, 1024))],
        out_specs=[pl.BlockSpec((rows, 4096), lambda i: (steps - 1 - i, 0)), _full((1, 1024))],
        out_shape=[jax.ShapeDtypeStruct((L, 4096), MXU_DTYPE), jax.ShapeDtypeStruct((1, 1024), F32)],
        scratch_shapes=[pltpu.VMEM((RET_HEADS, RET_DK, RET_DK), F32)],
        sem=("arbitrary",),
    )(p, p, p, p, dy, prevs, cos, sin, decay, xi, zeta, gain)


def _sgu_mix(p_ref, gain_ref, wm_ref, bt_ref, tl):
    pu, pv, z = p_ref[:, :2048], p_ref[:, 2048:4096], p_ref[:, 4096:]
    (u, du), (v, dv) = _gelu_and_grad(pu), _gelu_and_grad(pv)
    mu = jnp.mean(v, axis=-1, keepdims=True)
    vc = v - mu
    rstd = lax.rsqrt(jnp.mean(vc * vc, axis=-1, keepdims=True) + NORM_EPS)
    vn = vc * rstd
    vg = vn * gain_ref[...]
    mask = (lax.broadcasted_iota(jnp.int32, (SGU_CHUNK, SGU_CHUNK), 0)
            >= lax.broadcasted_iota(jnp.int32, (SGU_CHUNK, SGU_CHUNK), 1))
    wms = [jnp.where(mask, wm_ref[g], 0.0) for g in range(SGU_GROUPS)]
    rows = []
    for c in range(tl // SGU_CHUNK):
        rs = slice(SGU_CHUNK * c, SGU_CHUNK * (c + 1))
        cols = []
        for g in range(SGU_GROUPS):
            gs = slice(SGU_GDIM * g, SGU_GDIM * (g + 1))
            cols.append(_mm(wms[g], vg[rs, gs]) + bt_ref[:, g:g + 1])
        rows.append(jnp.concatenate(cols, axis=1))
    s = rows[0] if len(rows) == 1 else jnp.concatenate(rows, axis=0)
    return du, dv, z, u, vn, rstd, vg, wms, mask, s


def sgu_forward(p, gain, wm, bt):
    L = p.shape[0]
    tl = min(TL_SGU, L)

    def body(p_ref, gain_ref, wm_ref, bt_ref, y_ref):
        _, _, z, u, _, _, _, _, _, s = _sgu_mix(p_ref, gain_ref, wm_ref, bt_ref, tl)
        sz, _ = _silu_and_grad(z)
        y_ref[...] = (u * s * sz).astype(y_ref.dtype)

    return pl.pallas_call(
        body, name="sgu_forward", grid=(L // tl,),
        in_specs=[pl.BlockSpec((tl, ODD_IN), lambda i: (i, 0)), _full((1, 2048)), _full(wm.shape), _full(bt.shape)],
        out_specs=pl.BlockSpec((tl, 2048), lambda i: (i, 0)),
        out_shape=jax.ShapeDtypeStruct((L, 2048), MXU_DTYPE),
        compiler_params=_cparams(("arbitrary",)),
    )(p, gain, wm, bt)


def sgu_backward(p, dy, gain, wm, bt, plan=None):
    L = p.shape[0]
    tl = min(TL_SGU, L)

    def body(p_ref, dy_ref, gain_ref, wm_ref, bt_ref, dp_ref, dgain_ref, dwm_ref, dbt_ref):
        @pl.when(pl.program_id(0) == 0)
        def _():
            dgain_ref[...] = jnp.zeros_like(dgain_ref)
            dwm_ref[...] = jnp.zeros_like(dwm_ref)
            dbt_ref[...] = jnp.zeros_like(dbt_ref)

        gu, gv, z, u, vn, rstd, vg, wms, mask, s = _sgu_mix(p_ref, gain_ref, wm_ref, bt_ref, tl)
        sz, dsz = _silu_and_grad(z)
        dyv = dy_ref[...]
        dp_ref[:, 4096:] = (dyv * u * s * dsz).astype(dp_ref.dtype)
        dsg = dyv * sz
        dp_ref[:, :2048] = (dsg * s * gu).astype(dp_ref.dtype)
        ds = dsg * u
        rows = []
        dbs = [jnp.zeros((SGU_CHUNK, 1), F32) for _ in range(SGU_GROUPS)]
        for c in range(tl // SGU_CHUNK):
            rs = slice(SGU_CHUNK * c, SGU_CHUNK * (c + 1))
            cols = []
            for g in range(SGU_GROUPS):
                gs = slice(SGU_GDIM * g, SGU_GDIM * (g + 1))
                dsg_c = ds[rs, gs]
                dbs[g] = dbs[g] + jnp.sum(dsg_c, axis=1, keepdims=True)
                dwm_ref[g] += jnp.where(mask, _mm_nt(dsg_c, vg[rs, gs]), 0.0)
                cols.append(_mm_tn(wms[g], dsg_c))
            rows.append(jnp.concatenate(cols, axis=1))
        dbt_ref[...] += jnp.concatenate(dbs, axis=1)
        dvg = rows[0] if len(rows) == 1 else jnp.concatenate(rows, axis=0)
        dgain_ref[...] += jnp.sum(dvg * vn, axis=0, keepdims=True)
        dvn = dvg * gain_ref[...]
        dv = rstd * (dvn - jnp.mean(dvn, axis=-1, keepdims=True) - vn * jnp.mean(dvn * vn, axis=-1, keepdims=True))
        dp_ref[:, 2048:4096] = (dv * gv).astype(dp_ref.dtype)

    return _call(
        body, plan, name="sgu_backward", grid=(L // tl,),
        in_specs=[pl.BlockSpec((tl, ODD_IN), lambda i: (i, 0)), pl.BlockSpec((tl, 2048), lambda i: (i, 0)),
                  _full((1, 2048)), _full(wm.shape), _full(bt.shape)],
        out_specs=[pl.BlockSpec((tl, ODD_IN), lambda i: (i, 0)), _full((1, 2048)), _full(wm.shape), _full(bt.shape)],
        out_shape=[jax.ShapeDtypeStruct((L, ODD_IN), MXU_DTYPE), jax.ShapeDtypeStruct((1, 2048), F32),
                   jax.ShapeDtypeStruct(wm.shape, F32), jax.ShapeDtypeStruct(bt.shape, F32)],
        sem=("arbitrary",),
    )(p, dy, gain, wm, bt)


def cast_shards(mats):
    n = len(mats)
    steps = 8

    def body(*refs):
        for p in range(n):
            refs[n + p][...] = refs[p][...].astype(MXU_DTYPE)

    specs = [pl.BlockSpec((m.shape[0] // steps, m.shape[1]), lambda i: (i, 0)) for m in mats]
    return pl.pallas_call(
        body, name="cast_shards", grid=(steps,), in_specs=specs, out_specs=specs,
        out_shape=[jax.ShapeDtypeStruct(m.shape, MXU_DTYPE) for m in mats],
        compiler_params=_cparams(("arbitrary",)),
    )(*mats)


def local_grads(x, tgt, w):
    L = x.shape[0]
    ne, gf = w["norm_even"], w["final_norm"].reshape(1, D_MODEL)
    sh = dict(zip(MATRICES, cast_shards([w[n][0] for n in MATRICES])))
    lam_re, lam_im = w["s5_lam_re"][0], w["s5_lam_im"][0]
    log_dt = w["s5_log_dt"].reshape(S5_GROUPS, 1)
    bt_re = jnp.transpose(w["s5_b_re"][0], (2, 0, 1))
    bt_im = jnp.transpose(w["s5_b_im"][0], (2, 0, 1))
    c_re, c_im = w["s5_c_re"][0], w["s5_c_im"][0]
    wm = w["sgu_w_spatial"][0]
    bt = jnp.transpose(w["sgu_b_spatial"][0])

    tl5 = min(TL_S5, L)
    ab_re, ab_im, bb_re, bb_im, at_re, at_im = s5_params_fwd(lam_re, lam_im, log_dt, bt_re, bt_im, tl5 // 8)
    atab = jnp.stack([ab_re.reshape(S5_LANES), ab_im.reshape(S5_LANES),
                      at_re.reshape(S5_LANES), at_im.reshape(S5_LANES)])
    wbd = jnp.concatenate([_block_diag(jnp.transpose(bb_re, (1, 0, 2)), True),
                           _block_diag(jnp.transpose(bb_im, (1, 0, 2)), True)], axis=2).astype(MXU_DTYPE)
    cre = _block_diag(jnp.transpose(c_re, (0, 2, 1)), True).astype(MXU_DTYPE)
    cim = _block_diag(jnp.transpose(c_im, (0, 2, 1)), True).astype(MXU_DTYPE)
    cos, sin = _rope_tables(L)

    s5_cols = 2 * S5_WIDTH
    me = (2 * lax.axis_index("x") + lax.axis_index("y")).astype(jnp.int32)
    xs = stream_order(x, tl5)
    slab = lambda d: jnp.stack([me ^ d])
    zero = jnp.zeros((1,), jnp.int32)
    shards = [sh["w_in_even"][None]]
    (p1, h0s, h0), (got,) = even_in_slabs(x, xs, ne, shards[0], slab(0), zero, "even_in_0",
                                          plan=gather_plan([sh["w_in_even"]], only=1))
    for d in (1, 2, 3):
        shards.append(got)
        plan = gather_plan([sh["w_in_even"]], only=d + 1) if d < 3 else gather_plan([sh["s5_w_glu"]])
        (p1,), (got,) = even_in_slabs(x, xs, ne, shards[d], slab(d), zero, "even_in_%d" % d, p_in=p1, plan=plan)
    w_glu = got
    by_xor = jnp.concatenate(shards)
    w_glu = w_glu.reshape(S5_WIDTH, S5_WIDTH)
    (ya, st_re, st_im, sv_re, sv_im), (w_out_e, w_in_o, w_out_o, no, sg_gain) = s5_forward(
        p1, wbd, cre, cim, atab, w["s5_d"], w_glu, w["s5_b_glu"],
        gather_plan([sh["w_out_even"], sh["w_in_odd"], sh["w_out_odd"], w["norm_odd"], w["sgu_norm_gain"]]))
    w_out_e = w_out_e.reshape(2 * S5_WIDTH, D_MODEL)
    w_out_o = w_out_o.reshape(SGU_WIDTH, D_MODEL)
    no, sg_gain = no.reshape(1, D_MODEL), sg_gain.reshape(1, SGU_WIDTH)
    yb, prevs = retention_forward(p1, cos, sin, w["ret_gn_gain"])
    ya = token_order(ya, tl5)
    x1 = matmul_residual([ya, yb], w_out_e, x, "even_out")
    (p2, h1), _ = norm_matmul(x1, no, w_in_o, "odd_in")
    y2 = sgu_forward(p2, sg_gain, wm, bt)
    dx2, loss, dgf = out_proj_loss(y2, w_out_o, x1, gf, tgt, "odd_out_loss")

    g, landed = {}, {}
    shard_major = lambda a, n: a.reshape((N_CHIPS,) + w[n].shape[1:])
    dy2, g_w_out_o = out_proj_bwd(dx2, w_out_o, [y2], "odd_out_bwd")
    (dp2, g["sgu_norm_gain"], dwm, dbt), (landed["w_out_odd"],) = sgu_backward(
        p2, dy2, sg_gain, wm, bt, reduce_plan([shard_major(g_w_out_o, "w_out_odd")]))
    g_w_in_o, _ = in_proj_bwd_dw(h1, dp2, "odd_in_dw", ODD_IN // N_CHIPS)
    (dx1, g["norm_odd"]), _ = in_proj_bwd_dx(x1, no, [dp2], [w_in_o], dx2, "odd_in_dx")
    dya, dyb, g_w_out_e = out_proj_bwd(dx1, w_out_e, [ya, yb], "even_out_bwd")
    ((dpa, dwbd, dcre, dcim, dab_re, dab_im, g["s5_d"], g_w_glu, g["s5_b_glu"]),
     (landed["w_in_odd"], landed["w_out_even"])) = s5_backward(
        p1, stream_order(dya, tl5), st_re, st_im, sv_re, sv_im, wbd, cre, cim, atab, w["s5_d"], w_glu,
        w["s5_b_glu"], reduce_plan([g_w_in_o, shard_major(g_w_out_e, "w_out_even")]))

    dbb_re = jnp.transpose(_block_diag_extract(dwbd[:, :, :512], S5_GROUP, S5_STATE), (1, 0, 2))
    dbb_im = jnp.transpose(_block_diag_extract(dwbd[:, :, 512:], S5_GROUP, S5_STATE), (1, 0, 2))
    dlr, dli, ddt, dbt_re, dbt_im = s5_params_bwd(
        lam_re, lam_im, log_dt, bt_re, bt_im, dab_re.reshape(8, S5_GROUPS, S5_STATE),
        dab_im.reshape(8, S5_GROUPS, S5_STATE), dbb_re, dbb_im)
    g["s5_lam_re"], g["s5_lam_im"] = dlr[None], dli[None]
    g["s5_log_dt"] = ddt.reshape(1, S5_GROUPS)
    g["s5_b_re"], g["s5_b_im"] = dbt_re, dbt_im
    g["s5_c_re"] = _block_diag_extract(dcre, S5_GROUP, S5_STATE)[None]
    g["s5_c_im"] = _block_diag_extract(dcim, S5_GROUP, S5_STATE)[None]
    g["sgu_w_spatial"] = dwm[None]
    g["sgu_b_spatial"] = jnp.transpose(dbt)[None]
    g["final_norm"] = dgf.reshape(D_MODEL)
    g["loss"] = loss

    big_small = ("s5_b_re", "s5_b_im")
    mid_small = ("s5_c_re",)
    (dpb, g["ret_gn_gain"]), recv = retention_backward(
        p1, dyb, prevs, cos, sin, w["ret_gn_gain"],
        reduce_plan([shard_major(g_w_glu, "s5_w_glu")], [g[n] for n in big_small]))
    landed.update(zip(("s5_w_glu",) + big_small, recv))
    done = tuple(n for n in MATRICES if n != "w_in_even")
    part = {n: sum_slabs(landed[n], "sum_" + n) for n in done}
    g_w_in_e, recv = in_proj_bwd_dw(h0s, dpa, "even_in_dw_s5", 512, dtype=MXU_DTYPE,
                                    plan=_SiblingPlan([part[n] for n in done]))
    other = dict(zip(done, recv))
    small = tuple(n for n in SMALL if n != "norm_even" and n not in big_small + mid_small) + ("loss",)
    wb = EVEN_IN // N_CHIPS
    g_w_in_e, recv = in_proj_bwd_dw(h0, dpb, "even_in_dw_q", 512, first=s5_cols // 512, into=g_w_in_e,
                                    dtype=MXU_DTYPE, dp_first=2 * wb // 512, count=RET_HEADS * RET_DK // 512,
                                    plan=reduce_plan([], [g[n] for n in mid_small]))
    landed.update(zip(mid_small, recv))
    g_w_in_e, recv = in_proj_bwd_dw(h0, dpb, "even_in_dw_kvz", wb, first=2, into=g_w_in_e, dtype=MXU_DTYPE,
                                    count=2, plan=reduce_plan([], [g[n] for n in small]))
    landed.update(zip(small, recv))
    (dx0, g["norm_even"]), (landed["w_in_even"],) = even_in_bwd_dx(
        x, ne, token_order(dpa, tl5), dpb, by_xor, me ^ jnp.arange(N_CHIPS, dtype=jnp.int32), dx1, "even_in_dx",
        reduce_plan([g_w_in_e]))
    (landed["norm_even"],) = run_plan(reduce_plan([], [g["norm_even"]]), "exchange_norm_even")
    return dx0, landed, part, other


def _row_block(rows):
    return 128 if rows % 128 == 0 else rows


def sum_slabs(r, name):
    _, R, C = r.shape
    tr = _row_block(R)

    def body(r_ref, o_ref):
        a, b, c, d = (r_ref[k].astype(F32) for k in range(N_CHIPS))
        o_ref[...] = (a + b) + (c + d)

    return pl.pallas_call(
        body, name=name, grid=(R // tr,),
        in_specs=[pl.BlockSpec((N_CHIPS, tr, C), lambda i: (0, i, 0))],
        out_specs=pl.BlockSpec((tr, C), lambda i: (i, 0)),
        out_shape=jax.ShapeDtypeStruct((R, C), F32),
        compiler_params=_cparams(("arbitrary",)),
    )(r)


def _adam(w, m, v, g):
    mn = ADAM_B1 * m + (1.0 - ADAM_B1) * g
    vn = ADAM_B2 * v + (1.0 - ADAM_B2) * (g * g)
    m_hat = mn / (1.0 - ADAM_B1 ** ADAM_STEP)
    v_hat = vn / (1.0 - ADAM_B2 ** ADAM_STEP)
    return -ADAM_LR * (m_hat / (jnp.sqrt(v_hat) + ADAM_EPS) + ADAM_WD * w), mn, vn


def adam_update(w, m, v, ga, gb, name, plan=None):
    R, C = w.shape
    tr = _row_block(R)

    def body(w_ref, m_ref, v_ref, ga_ref, gb_ref, g_out, d_out, m_out, v_out):
        g = ga_ref[...] + gb_ref[...]
        g_out[...] = g
        d_out[...], m_out[...], v_out[...] = _adam(w_ref[...], m_ref[...], v_ref[...], g)

    blk = pl.BlockSpec((tr, C), lambda i: (i, 0))
    return _call(
        body, plan, name=name, grid=(R // tr,),
        in_specs=[blk] * 5, out_specs=[blk] * 4,
        out_shape=[jax.ShapeDtypeStruct((R, C), F32)] * 4,
        sem=("arbitrary",),
    )(w, m, v, ga, gb)


WIDE_ROWS = ("s5_b_re", "s5_b_im")


def sum_small(landed):
    def body(*refs):
        k = len(refs) // 2
        for i in range(k):
            r = refs[i]
            refs[k + i][...] = (r[0] + r[1]) + (r[2] + r[3])

    names = list(landed)
    res = pl.pallas_call(
        body, name="sum_small", out_shape=[jax.ShapeDtypeStruct(landed[n].shape[1:], F32) for n in names],
        compiler_params=pltpu.CompilerParams(vmem_limit_bytes=VMEM_LIMIT),
    )(*[landed[n] for n in names])
    return dict(zip(names, res))


def adam_small(names, w, m, v, ga, gb):
    def body(*refs):
        k = len(refs) // 9
        me = 2 * lax.axis_index("x") + lax.axis_index("y")
        for i in range(k):
            w_ref, m_ref, v_ref, ga_ref, gb_ref = refs[i], refs[k + i], refs[2 * k + i], refs[3 * k + i], refs[4 * k + i]
            size = w_ref.shape[-1]
            if ga_ref.shape != w_ref.shape:
                part = pl.ds(pl.multiple_of(me * size, LANES), size)
                g = ga_ref[:, part] + gb_ref[:, part]
            else:
                g = ga_ref[...] + gb_ref[...]
            refs[5 * k + i][...] = g
            refs[6 * k + i][...], refs[7 * k + i][...], refs[8 * k + i][...] = _adam(w_ref[...], m_ref[...], v_ref[...], g)

    ins = [d[n] for d in (w, m, v, ga, gb) for n in names]
    outs = [jax.ShapeDtypeStruct(w[n].shape, F32) for _ in range(4) for n in names]
    res = pl.pallas_call(body, name="adam_small", out_shape=outs,
                         compiler_params=pltpu.CompilerParams(vmem_limit_bytes=VMEM_LIMIT))(*ins)
    k = len(names)
    return [dict(zip(names, res[j * k:(j + 1) * k])) for j in range(4)]


WEIGHTS = ("norm_even", "w_in_even", "s5_lam_re", "s5_lam_im", "s5_log_dt", "s5_b_re", "s5_b_im", "s5_c_re",
           "s5_c_im", "s5_d", "s5_w_glu", "s5_b_glu", "ret_gn_gain", "w_out_even", "norm_odd", "w_in_odd",
           "sgu_norm_gain", "sgu_w_spatial", "sgu_b_spatial", "w_out_odd", "final_norm")
MATRICES = ("w_in_even", "s5_w_glu", "w_out_even", "w_in_odd", "w_out_odd")
SHARDED_VECS = ("norm_odd", "sgu_norm_gain")
REPLICATED = tuple(n for n in WEIGHTS if n not in MATRICES and n not in SHARDED_VECS)
SMALL = tuple(n for n in WEIGHTS if n not in MATRICES)
LANES = 128


def kernel(x, norm_even, w_in_even, s5_lam_re, s5_lam_im, s5_log_dt, s5_b_re, s5_b_im, s5_c_re, s5_c_im, s5_d, s5_w_glu, s5_b_glu, ret_gn_gain, w_out_even, norm_odd, w_in_odd, sgu_norm_gain, sgu_w_spatial, sgu_b_spatial, w_out_odd, final_norm, loss_target, m_norm_even, m_w_in_even, m_s5_lam_re, m_s5_lam_im, m_s5_log_dt, m_s5_b_re, m_s5_b_im, m_s5_c_re, m_s5_c_im, m_s5_d, m_s5_w_glu, m_s5_b_glu, m_ret_gn_gain, m_w_out_even, m_norm_odd, m_w_in_odd, m_sgu_norm_gain, m_sgu_w_spatial, m_sgu_b_spatial, m_w_out_odd, m_final_norm, v_norm_even, v_w_in_even, v_s5_lam_re, v_s5_lam_im, v_s5_log_dt, v_s5_b_re, v_s5_b_im, v_s5_c_re, v_s5_c_im, v_s5_d, v_s5_w_glu, v_s5_b_glu, v_ret_gn_gain, v_w_out_even, v_norm_odd, v_w_in_odd, v_sgu_norm_gain, v_sgu_w_spatial, v_sgu_b_spatial, v_w_out_odd, v_final_norm):
    w = dict(norm_even=norm_even, w_in_even=w_in_even, s5_lam_re=s5_lam_re, s5_lam_im=s5_lam_im, s5_log_dt=s5_log_dt, s5_b_re=s5_b_re, s5_b_im=s5_b_im, s5_c_re=s5_c_re, s5_c_im=s5_c_im, s5_d=s5_d, s5_w_glu=s5_w_glu, s5_b_glu=s5_b_glu, ret_gn_gain=ret_gn_gain, w_out_even=w_out_even, norm_odd=norm_odd, w_in_odd=w_in_odd, sgu_norm_gain=sgu_norm_gain, sgu_w_spatial=sgu_w_spatial, sgu_b_spatial=sgu_b_spatial, w_out_odd=w_out_odd, final_norm=final_norm)
    m = dict(norm_even=m_norm_even, w_in_even=m_w_in_even, s5_lam_re=m_s5_lam_re, s5_lam_im=m_s5_lam_im, s5_log_dt=m_s5_log_dt, s5_b_re=m_s5_b_re, s5_b_im=m_s5_b_im, s5_c_re=m_s5_c_re, s5_c_im=m_s5_c_im, s5_d=m_s5_d, s5_w_glu=m_s5_w_glu, s5_b_glu=m_s5_b_glu, ret_gn_gain=m_ret_gn_gain, w_out_even=m_w_out_even, norm_odd=m_norm_odd, w_in_odd=m_w_in_odd, sgu_norm_gain=m_sgu_norm_gain, sgu_w_spatial=m_sgu_w_spatial, sgu_b_spatial=m_sgu_b_spatial, w_out_odd=m_w_out_odd, final_norm=m_final_norm)
    v = dict(norm_even=v_norm_even, w_in_even=v_w_in_even, s5_lam_re=v_s5_lam_re, s5_lam_im=v_s5_lam_im, s5_log_dt=v_s5_log_dt, s5_b_re=v_s5_b_re, s5_b_im=v_s5_b_im, s5_c_re=v_s5_c_re, s5_c_im=v_s5_c_im, s5_d=v_s5_d, s5_w_glu=v_s5_w_glu, s5_b_glu=v_s5_b_glu, ret_gn_gain=v_ret_gn_gain, w_out_even=v_w_out_even, norm_odd=v_norm_odd, w_in_odd=v_w_in_odd, sgu_norm_gain=v_sgu_norm_gain, sgu_w_spatial=v_sgu_w_spatial, sgu_b_spatial=v_sgu_b_spatial, w_out_odd=v_w_out_odd, final_norm=v_final_norm)

    grad_x, landed, part, other = local_grads(x[0], loss_target[0], w)

    small = SMALL + ("loss",)
    part["w_in_even"] = sum_slabs(landed["w_in_even"], "sum_w_in_even")
    part.update(sum_small({n: landed[n] for n in small}))
    names = ("w_in_even",) + small
    swap = _SiblingPlan([part[n] for n in names])

    host = "w_in_odd"
    res_host, swapped = adam_update(w[host][0], m[host][0], v[host][0], part[host], other[host], "adam_" + host, swap)
    other.update(zip(names, swapped))
    wt, mt, vt = dict(w), dict(m), dict(v)
    for n in WIDE_ROWS:
        wt[n], mt[n], vt[n] = (jnp.transpose(a[n][0], (2, 0, 1)) for a in (w, m, v))
    out_g, out_d, out_m, out_v = adam_small(SMALL, wt, mt, vt, part, other)
    for n in WIDE_ROWS:
        for out in (out_g, out_d, out_m, out_v):
            out[n] = jnp.transpose(out[n], (1, 2, 0))[None]
    for n in MATRICES:
        res = res_host if n == host else adam_update(w[n][0], m[n][0], v[n][0], part[n], other[n], "adam_" + n)[0]
        out_g[n], out_d[n], out_m[n], out_v[n] = (r[None] for r in res)
    total_loss = (part["loss"] + other["loss"])[0, 0]

    return (total_loss, grad_x[None], *[out_g[n] for n in WEIGHTS], *[out_d[n] for n in WEIGHTS],
            *[out_m[n] for n in WEIGHTS], *[out_v[n] for n in WEIGHTS])
```

```python
import functools
import math

import numpy as np
import jax
import jax.numpy as jnp
from jax import lax
from jax.experimental import pallas as pl
from jax.experimental.pallas import tpu as pltpu

F32 = jnp.float32
MXU_DTYPE = jnp.bfloat16
NORM_EPS = 1e-6
D_MODEL = 1024
S5_WIDTH = 1024
S5_GROUP = 16
S5_GROUPS = 64
S5_STATE = 64
S5_LANES = S5_GROUPS * S5_STATE
S5_KBLK = 8
RET_HEADS = 4
RET_DK = 256
RET_CHUNK = 128
ROPE_BASE = 10000.0
SGU_WIDTH = 2048
SGU_GROUPS = 4
SGU_GDIM = 512
SGU_CHUNK = 128
EVEN_IN = 6144
ODD_IN = 6144
ADAM_LR = 0.001
ADAM_B1 = 0.9
ADAM_B2 = 0.999
ADAM_EPS = 1e-08
ADAM_WD = 0.01
ADAM_STEP = 10
N_CHIPS = 4
VMEM_LIMIT = 56 * 1024 * 1024

TL_PROJ = 512
TL_DW = 1024
TL_S5 = 256
TL_SGU = 256


def _cparams(sem, **kw):
    return pltpu.CompilerParams(dimension_semantics=sem, vmem_limit_bytes=VMEM_LIMIT, **kw)


def _mm(a, b):
    return jnp.dot(a.astype(MXU_DTYPE), b.astype(MXU_DTYPE), preferred_element_type=F32)


def _mm_nt(a, b):
    return lax.dot_general(a.astype(MXU_DTYPE), b.astype(MXU_DTYPE),
                           (((1,), (1,)), ((), ())), preferred_element_type=F32)


def _mm_tn(a, b):
    return lax.dot_general(a.astype(MXU_DTYPE), b.astype(MXU_DTYPE),
                           (((0,), (0,)), ((), ())), preferred_element_type=F32)


_GELU_C = math.sqrt(2.0 / math.pi)


def _gelu_parts(x):
    x2 = x * x
    th = jnp.tanh(x * (_GELU_C + (_GELU_C * 0.044715) * x2))
    hx = 0.5 * x
    return hx + hx * th, th, x2, hx


def _gelu(x):
    return _gelu_parts(x)[0]


def _gelu_and_grad(x):
    g, th, x2, hx = _gelu_parts(x)
    return g, (0.5 + 0.5 * th) + hx * (1.0 - th * th) * (_GELU_C + (3.0 * _GELU_C * 0.044715) * x2)


def _gelu_grad(x):
    return _gelu_and_grad(x)[1]


def _sigmoid(x):
    return 1.0 / (1.0 + jnp.exp(-x))


def _silu_and_grad(x):
    s = _sigmoid(x)
    return x * s, s * (1.0 + x * (1.0 - s))


def _rms(x):
    return lax.rsqrt(jnp.mean(x * x, axis=-1, keepdims=True) + NORM_EPS)


def _full(shape):
    nd = len(shape)
    return pl.BlockSpec(shape, lambda *_: (0,) * nd)


MESH = pl.DeviceIdType.MESH
ANY = pl.BlockSpec(memory_space=pl.ANY)


def _place():
    return lax.axis_index("x"), lax.axis_index("y"), lax.axis_index("c")


def _chip_peer(x, y, c, d):
    return (1 - x if d >= 2 else x, 1 - y if d % 2 else y, c)


class _Plan:
    def __init__(self, inputs, out_shape, build):
        self.inputs, self.out_shape, self._build = list(inputs), list(out_shape), build
        n = len(self.inputs)
        self.sems = [pltpu.SemaphoreType.DMA((n, 3)), pltpu.SemaphoreType.DMA((n, 3)), pltpu.SemaphoreType.DMA((n,))]

    def start(self, in_refs, out_refs, sems):
        send, recv, local = self._build(in_refs, out_refs, sems)
        for p in range(len(self.inputs)):
            local[p].start()
            for cp in send[p]:
                cp.start()

    def wait(self, in_refs, out_refs, sems):
        send, recv, local = self._build(in_refs, out_refs, sems)
        for p in range(len(self.inputs)):
            for cp in recv[p]:
                cp.wait_recv()
        for p in range(len(self.inputs)):
            for cp in send[p]:
                cp.wait_send()
            local[p].wait()


class _GatherPlan:
    def __init__(self, shards, only=None):
        n = len(shards)
        self.n, self.only = n, only
        self.peers = (1, 2, 3) if only is None else (only,)
        self.inputs = list(shards)
        slabs = N_CHIPS if only is None else 1
        self.out_shape = [jax.ShapeDtypeStruct((slabs,) + s.shape, s.dtype) for s in shards]
        self.halved = [s.shape[0] % 32 == 0 for s in shards]
        self.sems = [pltpu.SemaphoreType.DMA((n, 3)) for _ in range(4)] + [pltpu.SemaphoreType.DMA((n,))]

    def _copies(self, in_refs, out_refs, sems):
        ici_s, ici_r, d2d_s, d2d_r, loc = sems
        x, y, c = _place()
        me = 2 * x + y

        def rows(p, core):
            if not self.halved[p]:
                return slice(None)
            half = self.inputs[p].shape[0] // 2
            return pl.ds(pl.multiple_of(core * half, 16), half)

        def slab(chip):
            return chip if self.only is None else 0

        def ici(p, d, chip, core):
            return pltpu.make_async_remote_copy(
                src_ref=in_refs[p].at[rows(p, core)], dst_ref=out_refs[p].at[slab(chip), rows(p, core)],
                send_sem=ici_s.at[p, d - 1], recv_sem=ici_r.at[p, d - 1],
                device_id=_chip_peer(x, y, c, d), device_id_type=MESH)

        def d2d(p, d, core):
            part = out_refs[p].at[slab(me ^ d), rows(p, core)]
            return pltpu.make_async_remote_copy(
                src_ref=part, dst_ref=part, send_sem=d2d_s.at[p, d - 1], recv_sem=d2d_r.at[p, d - 1],
                device_id=(x, y, 1 - c), device_id_type=MESH)

        local = [pltpu.make_async_copy(in_refs[p], out_refs[p].at[slab(me)], loc.at[p]) for p in range(self.n)]
        return me, c, ici, d2d, local

    def start(self, in_refs, out_refs, sems):
        me, c, ici, d2d, local = self._copies(in_refs, out_refs, sems)
        for p in range(self.n):
            if self.only is None:
                local[p].start()
            for d in self.peers:
                ici(p, d, me, c).start()

    def wait(self, in_refs, out_refs, sems):
        me, c, ici, d2d, local = self._copies(in_refs, out_refs, sems)
        for p in range(self.n):
            for d in self.peers:
                ici(p, d, me ^ d, c).wait_recv()
                if self.halved[p]:
                    d2d(p, d, c).start()
        for p in range(self.n):
            for d in self.peers:
                if self.halved[p]:
                    d2d(p, d, 1 - c).wait_recv()
                    d2d(p, d, c).wait_send()
                ici(p, d, me, c).wait_send()
            if self.only is None:
                local[p].wait()


def gather_plan(shards, only=None):
    return _GatherPlan(shards, only)


def reduce_plan(shards, whole=()):
    n_s = len(shards)

    def build(in_refs, out_refs, sems):
        send_sems, recv_sems, loc_sems = sems
        x, y, c = _place()
        me = 2 * x + y

        def src(p, slab):
            return in_refs[p].at[slab] if p < n_s else in_refs[p]

        def remote(p, d):
            return pltpu.make_async_remote_copy(
                src_ref=src(p, me ^ d), dst_ref=out_refs[p].at[d], send_sem=send_sems.at[p, d - 1],
                recv_sem=recv_sems.at[p, d - 1], device_id=_chip_peer(x, y, c, d), device_id_type=MESH)

        n = len(in_refs)
        send = [[remote(p, d) for d in (1, 2, 3)] for p in range(n)]
        local = [pltpu.make_async_copy(src(p, me), out_refs[p].at[0], loc_sems.at[p]) for p in range(n)]
        return send, send, local

    outs = [jax.ShapeDtypeStruct(s.shape, s.dtype) for s in shards]
    outs += [jax.ShapeDtypeStruct((N_CHIPS,) + a.shape, a.dtype) for a in whole]
    return _Plan(list(shards) + list(whole), outs, build)


class _SiblingPlan:
    def __init__(self, arrs):
        self.inputs = list(arrs)
        self.out_shape = [jax.ShapeDtypeStruct(a.shape, a.dtype) for a in arrs]
        n = len(arrs)
        self.sems = [pltpu.SemaphoreType.DMA((n,)), pltpu.SemaphoreType.DMA((n,))]

    def _copies(self, in_refs, out_refs, sems):
        x, y, c = _place()
        return [pltpu.make_async_remote_copy(
            src_ref=in_refs[p], dst_ref=out_refs[p], send_sem=sems[0].at[p], recv_sem=sems[1].at[p],
            device_id=(x, y, 1 - c), device_id_type=MESH) for p in range(len(self.inputs))]

    def start(self, in_refs, out_refs, sems):
        for cp in self._copies(in_refs, out_refs, sems):
            cp.start()

    def wait(self, in_refs, out_refs, sems):
        copies = self._copies(in_refs, out_refs, sems)
        for cp in copies:
            cp.wait_recv()
        for cp in copies:
            cp.wait_send()


def run_plan(plan, name):
    n = len(plan.inputs)

    def body(*refs):
        plan.start(refs[:n], refs[n:2 * n], refs[2 * n:])
        plan.wait(refs[:n], refs[n:2 * n], refs[2 * n:])

    return pl.pallas_call(body, name=name, in_specs=[ANY] * n, out_specs=[ANY] * n, out_shape=plan.out_shape,
                          scratch_shapes=plan.sems)(*plan.inputs)


def _call(body, plan, *, name, grid, in_specs, out_specs, out_shape, sem, scratch_shapes=(), aliases=None,
          n_prefetch=0):
    aliases = {} if aliases is None else aliases
    single = not isinstance(out_shape, (list, tuple))
    out_specs = [out_specs] if single else list(out_specs)
    out_shape = [out_shape] if single else list(out_shape)
    n_in, n_out, n_scr = len(in_specs), len(out_specs), len(scratch_shapes)
    ci = 0 if plan is None else len(plan.inputs)
    co = 0 if plan is None else len(plan.out_shape)

    def hosted(*refs):
        pre, refs = refs[:n_prefetch], refs[n_prefetch:]
        ins, cins = refs[:n_in], refs[n_in:n_in + ci]
        k = n_in + ci
        outs, couts = refs[k:k + n_out], refs[k + n_out:k + n_out + co]
        k += n_out + co
        scr, sems = refs[k:k + n_scr], refs[k + n_scr:]
        ids = [pl.program_id(a) for a in range(len(grid))]
        first = functools.reduce(jnp.logical_and, [i == 0 for i in ids])
        last = functools.reduce(jnp.logical_and, [i == g - 1 for i, g in zip(ids, grid)])

        @pl.when(first)
        def _():
            plan.start(cins, couts, sems)

        body(*pre, *ins, *outs, *scr)

        @pl.when(last)
        def _():
            plan.wait(cins, couts, sems)

    def run(*args):
        hosting = plan is not None
        spec = pltpu.PrefetchScalarGridSpec(
            num_scalar_prefetch=n_prefetch, grid=grid,
            in_specs=list(in_specs) + ([ANY] * ci if hosting else []),
            out_specs=out_specs + ([ANY] * co if hosting else []),
            scratch_shapes=list(scratch_shapes) + (plan.sems if hosting else []))
        res = pl.pallas_call(hosted if hosting else body, name=name, grid_spec=spec,
                             out_shape=out_shape + (plan.out_shape if hosting else []),
                             input_output_aliases=aliases, compiler_params=_cparams(sem),
                             )(*args, *(plan.inputs if hosting else []))
        return (res[0] if single else res[:n_out]), list(res[n_out:])

    return run


def norm_matmul(x, g, w, name, plan=None, tn=None):
    L, D = x.shape
    tl = min(TL_DW, L)
    if w.ndim == 3:
        nt, _, tn = w.shape
        w_spec = pl.BlockSpec((1, D, tn), lambda i, n: (n, 0, 0))
    else:
        nt = w.shape[1] // tn
        w_spec = pl.BlockSpec((D, tn), lambda i, n: (0, n))

    def body(x_ref, g_ref, w_ref, o_ref, h_ref):
        xv = x_ref[...]
        h = (xv * _rms(xv) * g_ref[...]).astype(h_ref.dtype)
        h_ref[...] = h
        o_ref[...] = _mm(h, w_ref[0] if w.ndim == 3 else w_ref[...])

    return _call(
        body, plan, name=name, grid=(L // tl, nt),
        in_specs=[pl.BlockSpec((tl, D), lambda i, n: (i, 0)), _full((1, D)), w_spec],
        out_specs=[pl.BlockSpec((tl, tn), lambda i, n: (i, n)), pl.BlockSpec((tl, D), lambda i, n: (i, 0))],
        out_shape=[jax.ShapeDtypeStruct((L, nt * tn), F32), jax.ShapeDtypeStruct((L, D), MXU_DTYPE)],
        sem=("arbitrary", "arbitrary"),
    )(x, g, w)


def even_in_slabs(x, xs, g, w, slabs, wsel, name, p_in=None, plan=None):
    L, D = x.shape
    tl = min(TL_DW, L)
    wb = EVEN_IN // N_CHIPS
    n = slabs.shape[0]
    s5_cols = 2 * S5_WIDTH - wb
    first = p_in is None

    def body(slabs_ref, wsel_ref, xs_ref, x_ref, g_ref, w_ref, *rest):
        o_ref = rest[-3] if first else rest[-1]
        j = slabs_ref[pl.program_id(0)]
        hs = (xs_ref[...] * _rms(xs_ref[...]) * g_ref[...]).astype(MXU_DTYPE)
        h = (x_ref[...] * _rms(x_ref[...]) * g_ref[...]).astype(MXU_DTYPE)
        if first:
            rest[-2][...] = hs
            rest[-1][...] = h
        o_ref[:, :s5_cols] = _mm(jnp.where(j <= 1, hs, h), w_ref[0, :, :s5_cols])
        o_ref[:, s5_cols:] = _mm(jnp.where(j == 0, hs, h), w_ref[0, :, s5_cols:])

    row = pl.BlockSpec((tl, D), lambda s, i, slabs_ref, wsel_ref: (i, 0))
    in_specs = [row if first else
                pl.BlockSpec((tl, D), lambda s, i, slabs_ref, wsel_ref: (jnp.where(slabs_ref[s] <= 1, i, 0), 0)),
                row if first else
                pl.BlockSpec((tl, D), lambda s, i, slabs_ref, wsel_ref: (jnp.where(slabs_ref[s] >= 1, i, 0), 0)),
                pl.BlockSpec((1, D), lambda s, i, slabs_ref, wsel_ref: (0, 0)),
                pl.BlockSpec((1, D, wb), lambda s, i, slabs_ref, wsel_ref: (wsel_ref[s], 0, 0))]
    out_specs = [pl.BlockSpec((tl, wb), lambda s, i, slabs_ref, wsel_ref: (i, slabs_ref[s]))]
    out_shape = [jax.ShapeDtypeStruct((L, EVEN_IN), F32)]
    args = [slabs, wsel, xs, x, g, w]
    if first:
        out_specs += [row, row]
        out_shape += [jax.ShapeDtypeStruct((L, D), MXU_DTYPE)] * 2
    else:
        in_specs.append(ANY)
        args.append(p_in)
    return _call(body, plan, name=name, grid=(n, L // tl), in_specs=in_specs, out_specs=out_specs,
                 out_shape=out_shape, sem=("arbitrary", "arbitrary"), n_prefetch=2,
                 aliases={} if first else {6: 0})(*args)


def matmul_residual(ys, w, x, name):
    L, D = x.shape
    tl = min(TL_PROJ, L)
    n = len(ys)
    offs = np.cumsum([0] + [y.shape[1] for y in ys])

    def body(*refs):
        y_refs, w_ref, x_ref, o_ref = refs[:n], refs[n], refs[n + 1], refs[n + 2]
        acc = x_ref[...]
        for k in range(n):
            acc = acc + _mm(y_refs[k][...], w_ref[offs[k]:offs[k + 1], :])
        o_ref[...] = acc

    return pl.pallas_call(
        body, name=name, grid=(L // tl,),
        in_specs=[pl.BlockSpec((tl, y.shape[1]), lambda i: (i, 0)) for y in ys]
        + [_full(w.shape), pl.BlockSpec((tl, D), lambda i: (i, 0))],
        out_specs=pl.BlockSpec((tl, D), lambda i: (i, 0)),
        out_shape=jax.ShapeDtypeStruct((L, D), F32),
        compiler_params=_cparams(("arbitrary",)),
    )(*ys, w, x)


def out_proj_loss(y, w, x, gf, tgt, name):
    L, K = y.shape
    D = w.shape[1]
    tl = min(TL_PROJ, L)

    def body(y_ref, w_ref, x_ref, gf_ref, t_ref, dx_ref, loss_ref, dg_ref):
        @pl.when(pl.program_id(0) == 0)
        def _():
            loss_ref[...] = jnp.zeros_like(loss_ref)
            dg_ref[...] = jnp.zeros_like(dg_ref)

        x2 = x_ref[...] + _mm(y_ref[...], w_ref[...])
        r = _rms(x2)
        xn = x2 * r
        e = xn * gf_ref[...] - t_ref[...]
        loss_ref[...] += (0.5 / D) * jnp.sum(e * e)
        dout = e * (1.0 / D)
        dg_ref[...] += jnp.sum(dout * xn, axis=0, keepdims=True)
        dxn = dout * gf_ref[...]
        dx_ref[...] = r * (dxn - xn * jnp.mean(dxn * xn, axis=-1, keepdims=True))

    return pl.pallas_call(
        body, name=name, grid=(L // tl,),
        in_specs=[pl.BlockSpec((tl, K), lambda i: (i, 0)), _full((K, D)),
                  pl.BlockSpec((tl, D), lambda i: (i, 0)), _full((1, D)),
                  pl.BlockSpec((tl, D), lambda i: (i, 0))],
        out_specs=[pl.BlockSpec((tl, D), lambda i: (i, 0)), _full((8, 128)), _full((1, D))],
        out_shape=[jax.ShapeDtypeStruct((L, D), F32), jax.ShapeDtypeStruct((8, 128), F32),
                   jax.ShapeDtypeStruct((1, D), F32)],
        compiler_params=_cparams(("arbitrary",)),
    )(y, w, x, gf, tgt)


def out_proj_bwd(dx, w, ys, name):
    L, D = dx.shape
    K = w.shape[0]
    tl = min(TL_PROJ, L)
    n = len(ys)
    offs = np.cumsum([0] + [y.shape[1] for y in ys])

    def body(*refs):
        dx_ref, w_ref, y_refs = refs[0], refs[1], refs[2:2 + n]
        dy_refs, dw_ref = refs[2 + n:2 + 2 * n], refs[2 + 2 * n]

        @pl.when(pl.program_id(0) == 0)
        def _():
            dw_ref[...] = jnp.zeros_like(dw_ref)

        dxv = dx_ref[...]
        for k in range(n):
            dy_refs[k][...] = _mm_nt(dxv, w_ref[offs[k]:offs[k + 1], :])
            dw_ref[offs[k]:offs[k + 1], :] += _mm_tn(y_refs[k][...], dxv)

    y_specs = [pl.BlockSpec((tl, y.shape[1]), lambda i: (i, 0)) for y in ys]
    return pl.pallas_call(
        body, name=name, grid=(L // tl,),
        in_specs=[pl.BlockSpec((tl, D), lambda i: (i, 0)), _full((K, D))] + y_specs,
        out_specs=y_specs + [_full((K, D))],
        out_shape=[jax.ShapeDtypeStruct(y.shape, F32) for y in ys] + [jax.ShapeDtypeStruct((K, D), F32)],
        compiler_params=_cparams(("arbitrary",)),
    )(dx, w, *ys)


def in_proj_bwd_dx(x, g, dps, ws, dres, name, plan=None):
    L, D = x.shape
    tl = min(TL_PROJ, L)
    n = len(dps)

    def body(*refs):
        x_ref, g_ref, dres_ref = refs[:3]
        dp_refs, w_refs = refs[3:3 + n], refs[3 + n:3 + 2 * n]
        dx_ref, dg_ref = refs[3 + 2 * n:]

        @pl.when(pl.program_id(0) == 0)
        def _():
            dg_ref[...] = jnp.zeros_like(dg_ref)

        dh = None
        for dp_ref, w_ref, w in zip(dp_refs, w_refs, ws):
            if w.ndim == 3:
                tn = w.shape[2]
                parts = [_mm_nt(dp_ref[:, tn * k:tn * (k + 1)], w_ref[k]) for k in range(w.shape[0])]
            else:
                parts = [_mm_nt(dp_ref[...], w_ref[...])]
            for part in parts:
                dh = part if dh is None else dh + part
        xv = x_ref[...]
        r = _rms(xv)
        xn = xv * r
        dg_ref[...] += jnp.sum(dh * xn, axis=0, keepdims=True)
        dxn = dh * g_ref[...]
        dx_ref[...] = dres_ref[...] + r * (dxn - xn * jnp.mean(dxn * xn, axis=-1, keepdims=True))

    return _call(
        body, plan, name=name, grid=(L // tl,),
        in_specs=[pl.BlockSpec((tl, D), lambda i: (i, 0)), _full((1, D)), pl.BlockSpec((tl, D), lambda i: (i, 0))]
        + [pl.BlockSpec((tl, dp.shape[1]), lambda i: (i, 0)) for dp in dps] + [_full(w.shape) for w in ws],
        out_specs=[pl.BlockSpec((tl, D), lambda i: (i, 0)), _full((1, D))],
        out_shape=[jax.ShapeDtypeStruct((L, D), F32), jax.ShapeDtypeStruct((1, D), F32)],
        sem=("arbitrary",),
    )(x, g, dres, *dps, *ws)


def even_in_bwd_dx(x, g, dpa, dpb, wsh, sel, dres, name, plan=None):
    L, D = x.shape
    tl = min(TL_PROJ, L)
    wb = wsh.shape[2]
    na, nb = dpa.shape[1], dpb.shape[1]

    def body(sel_ref, x_ref, g_ref, dres_ref, dpa_ref, dpb_ref, w_ref, dx_ref, dg_ref):
        @pl.when(pl.program_id(0) == 0)
        def _():
            dg_ref[...] = jnp.zeros_like(dg_ref)

        w1 = w_ref.at[sel_ref[1]]
        dh = _mm_nt(dpa_ref[:, :wb], w_ref[sel_ref[0]])
        dh += _mm_nt(dpa_ref[:, wb:], w1[:, :na - wb])
        dh += _mm_nt(dpb_ref[:, :wb], w_ref[sel_ref[2]])
        dh += _mm_nt(dpb_ref[:, wb:2 * wb], w_ref[sel_ref[3]])
        dh += _mm_nt(dpb_ref[:, 2 * wb:], w1[:, na - wb:])
        xv = x_ref[...]
        r = _rms(xv)
        xn = xv * r
        dg_ref[...] += jnp.sum(dh * xn, axis=0, keepdims=True)
        dxn = dh * g_ref[...]
        dx_ref[...] = dres_ref[...] + r * (dxn - xn * jnp.mean(dxn * xn, axis=-1, keepdims=True))

    row = lambda n: pl.BlockSpec((tl, n), lambda i, sel_ref: (i, 0))
    whole = lambda shape: pl.BlockSpec(shape, lambda i, sel_ref: (0,) * len(shape))
    return _call(
        body, plan, name=name, grid=(L // tl,),
        in_specs=[row(D), whole((1, D)), row(D), row(na), row(nb), whole(wsh.shape)],
        out_specs=[row(D), whole((1, D))],
        out_shape=[jax.ShapeDtypeStruct((L, D), F32), jax.ShapeDtypeStruct((1, D), F32)],
        sem=("arbitrary",), n_prefetch=1,
    )(sel, x, g, dres, dpa, dpb, wsh)


def in_proj_bwd_dw(h, dp, name, tn, first=0, into=None, dtype=F32, plan=None, dp_first=0, count=None):
    L, D = h.shape
    tl = min(TL_DW, L)
    wb = EVEN_IN // N_CHIPS
    per = wb // tn
    count = dp.shape[1] // tn if count is None else count
    last = L // tl - 1

    def body(*refs):
        h_ref, dp_ref, dw_ref, acc = refs[0], refs[1], refs[-2], refs[-1]

        @pl.when(pl.program_id(1) == 0)
        def _():
            acc[...] = jnp.zeros_like(acc)

        acc[...] += _mm_tn(h_ref[...], dp_ref[...])

        @pl.when(pl.program_id(1) == last)
        def _():
            dw_ref[0] = acc[...].astype(dw_ref.dtype)

    ins = [h, dp] + ([] if into is None else [into])
    return _call(
        body, plan, name=name, grid=(count, L // tl),
        in_specs=[pl.BlockSpec((tl, D), lambda n, i: (i, 0)), pl.BlockSpec((tl, tn), lambda n, i: (i, n + dp_first))]
        + ([] if into is None else [ANY]),
        out_specs=pl.BlockSpec((1, D, tn), lambda n, i: ((n + first) // per, 0, (n + first) % per)),
        out_shape=jax.ShapeDtypeStruct((N_CHIPS, D, wb), dtype),
        scratch_shapes=[pltpu.VMEM((D, tn), F32)],
        aliases={} if into is None else {2: 0},
        sem=("arbitrary", "arbitrary"),
    )(*ins)


def _s5_param_fn(lam_re, lam_im, log_dt, b_re, b_im):
    lr = jnp.minimum(lam_re, -1e-4)
    li = lam_im
    dt = jnp.exp(log_dt)
    mag = jnp.exp(lr * dt)
    ab_re = mag * jnp.cos(li * dt)
    ab_im = mag * jnp.sin(li * dt)
    den = lr * lr + li * li
    n_re = ab_re - 1.0
    n_im = ab_im
    z_re = (n_re * lr + n_im * li) / den
    z_im = (n_im * lr - n_re * li) / den
    bb_re = z_re[None] * b_re - z_im[None] * b_im
    bb_im = z_re[None] * b_im + z_im[None] * b_re
    return ab_re, ab_im, bb_re, bb_im


def s5_params_fwd(lam_re, lam_im, log_dt, b_re, b_im, span):
    G, P = lam_re.shape
    H = b_re.shape[0]
    assert span & (span - 1) == 0

    def body(lr_ref, li_ref, dt_ref, br_ref, bi_ref, abr_ref, abi_ref, bbr_ref, bbi_ref, pr_ref, pi_ref):
        ab_re, ab_im, bb_re, bb_im = _s5_param_fn(lr_ref[...], li_ref[...], dt_ref[...], br_ref[...], bi_ref[...])
        abr_ref[...] = ab_re
        abi_ref[...] = ab_im
        bbr_ref[...] = bb_re
        bbi_ref[...] = bb_im
        cr, ci = ab_re, ab_im
        for _ in range(span.bit_length() - 1):
            cr, ci = cr * cr - ci * ci, 2.0 * cr * ci
        pr_ref[...] = cr
        pi_ref[...] = ci

    shp = lambda *s: jax.ShapeDtypeStruct(s, F32)
    return pl.pallas_call(
        body, name="s5_params_fwd",
        out_shape=[shp(G, P), shp(G, P), shp(H, G, P), shp(H, G, P), shp(G, P), shp(G, P)],
    )(lam_re, lam_im, log_dt, b_re, b_im)


def s5_params_bwd(lam_re, lam_im, log_dt, b_re, b_im, d_ab_re, d_ab_im, d_bb_re, d_bb_im):
    G, P = lam_re.shape
    H = b_re.shape[0]

    def body(lr_ref, li_ref, dt_ref, br_ref, bi_ref, g0, g1, g2, g3, o0, o1, o2, o3, o4):
        prim = (lr_ref[...], li_ref[...], dt_ref[...], br_ref[...], bi_ref[...])
        _, vjp = jax.vjp(_s5_param_fn, *prim)
        d = vjp((jnp.sum(g0[...], axis=0), jnp.sum(g1[...], axis=0), g2[...], g3[...]))
        o0[...], o1[...], o2[...], o3[...], o4[...] = d

    shp = lambda *s: jax.ShapeDtypeStruct(s, F32)
    return pl.pallas_call(
        body, name="s5_params_bwd",
        out_shape=[shp(G, P), shp(G, P), shp(G, 1), shp(H, G, P), shp(H, G, P)],
    )(lam_re, lam_im, log_dt, b_re, b_im, d_ab_re, d_ab_im, d_bb_re, d_bb_im)


def stream_order(a, tl):
    L, C = a.shape
    return a.reshape(L // tl, 8, tl // 8, C).transpose(0, 2, 1, 3).reshape(L, C)


def token_order(a, tl):
    L, C = a.shape
    return a.reshape(L // tl, tl // 8, 8, C).transpose(0, 2, 1, 3).reshape(L, C)


_LANE_BLK = 1024
_LANE_BLK_BWD = 1024


def _cmul_add(ar, ai, xr, xi, br, bi):
    return br + (ar * xr - ai * xi), bi + (ar * xi + ai * xr)


def _cmulc_add(ar, ai, xr, xi, br, bi):
    return br + (ar * xr + ai * xi), bi + (ar * xi - ai * xr)


def _s5_states(u, wbd_ref, a_re, a_im, at_re, at_im, s_re, s_im, e_re, e_im, c0_re, c0_im, tl):
    t8 = tl // 8
    for k in range(S5_KBLK):
        bu = _mm(u[:, 128 * k:128 * (k + 1)], wbd_ref[k])
        s_re[:, 512 * k:512 * (k + 1)] = bu[:, :512]
        s_im[:, 512 * k:512 * (k + 1)] = bu[:, 512:]
    outs_re, outs_im = [], []
    for b in range(S5_LANES // _LANE_BLK):
        lanes = slice(_LANE_BLK * b, _LANE_BLK * (b + 1))
        ar = jnp.broadcast_to(a_re[:, lanes], (8, _LANE_BLK))
        ai = jnp.broadcast_to(a_im[:, lanes], (8, _LANE_BLK))

        def local(i, carry, lanes=lanes, ar=ar, ai=ai):
            r = pl.multiple_of(i * 8, 8)
            sr, si = _cmul_add(ar, ai, carry[0], carry[1], s_re[pl.ds(r, 8), lanes], s_im[pl.ds(r, 8), lanes])
            s_re[pl.ds(r, 8), lanes] = sr
            s_im[pl.ds(r, 8), lanes] = si
            return sr, si

        zero = jnp.zeros((8, _LANE_BLK), F32)
        fr, fi = lax.fori_loop(0, t8, local, (zero, zero), unroll=True)
        tr, ti = at_re[:, lanes], at_im[:, lanes]
        er, ei = c0_re[:, lanes], c0_im[:, lanes]
        ers, eis = [er], [ei]
        for j in range(8):
            er, ei = _cmul_add(tr, ti, er, ei, fr[j:j + 1], fi[j:j + 1])
            ers.append(er)
            eis.append(ei)
        outs_re.append(ers[8])
        outs_im.append(eis[8])
        ent_r, ent_i = jnp.concatenate(ers[:8], axis=0), jnp.concatenate(eis[:8], axis=0)
        e_re[:, lanes] = ent_r
        e_im[:, lanes] = ent_i

        def fix(i, carry, lanes=lanes, ar=ar, ai=ai):
            r = pl.multiple_of(i * 8, 8)
            zr, zi = ar * carry[0] - ai * carry[1], ar * carry[1] + ai * carry[0]
            s_re[pl.ds(r, 8), lanes] = s_re[pl.ds(r, 8), lanes] + zr
            s_im[pl.ds(r, 8), lanes] = s_im[pl.ds(r, 8), lanes] + zi
            return zr, zi

        lax.fori_loop(0, t8, fix, (ent_r, ent_i), unroll=True)
    return jnp.concatenate(outs_re, axis=1), jnp.concatenate(outs_im, axis=1)


def _s5_readout(s_re, s_im, cre_ref, cim_ref):
    ys = []
    for k in range(S5_KBLK):
        lanes = slice(512 * k, 512 * (k + 1))
        ys.append(_mm(s_re[:, lanes], cre_ref[k]) - _mm(s_im[:, lanes], cim_ref[k]))
    return jnp.concatenate(ys, axis=1)


def s5_forward(p, wbd, cre, cim, atab, d_skip, w_glu, b_glu, plan=None):
    L = p.shape[0]
    tl = min(TL_S5, L)
    nch = L // tl

    def body(u_ref, z_ref, wbd_ref, cre_ref, cim_ref, at_ref, d_ref, wg_ref, bg_ref,
             ya_ref, st_re_ref, st_im_ref, sv_re_ref, sv_im_ref, s_re, s_im, e_re, e_im, car_re, car_im):
        @pl.when(pl.program_id(0) == 0)
        def _():
            car_re[...] = jnp.zeros_like(car_re)
            car_im[...] = jnp.zeros_like(car_im)

        c0_re, c0_im = car_re[...], car_im[...]
        st_re_ref[0] = c0_re
        st_im_ref[0] = c0_im
        u = u_ref[...]
        x_re, x_im = _s5_states(u, wbd_ref, at_ref[0:1], at_ref[1:2], at_ref[2:3], at_ref[3:4],
                                s_re, s_im, e_re, e_im, c0_re, c0_im, tl)
        car_re[...] = x_re
        car_im[...] = x_im
        sv_re_ref[...] = s_re[...].astype(sv_re_ref.dtype)
        sv_im_ref[...] = s_im[...].astype(sv_im_ref.dtype)
        y = _s5_readout(sv_re_ref, sv_im_ref, cre_ref, cim_ref) + d_ref[...] * u
        yg = _gelu(y)
        gate = _sigmoid(_mm(yg, wg_ref[...]) + bg_ref[...])
        sz, _ = _silu_and_grad(z_ref[...])
        ya_ref[...] = (yg * gate * sz).astype(ya_ref.dtype)

    return _call(
        body, plan, name="s5_forward", grid=(nch,),
        in_specs=[pl.BlockSpec((tl, 1024), lambda i: (i, 0)), pl.BlockSpec((tl, 1024), lambda i: (i, 1)),
                  _full(wbd.shape), _full(cre.shape), _full(cim.shape), _full(atab.shape),
                  _full((1, 1024)), _full((1024, 1024)), _full((1, 1024))],
        out_specs=[pl.BlockSpec((tl, 1024), lambda i: (i, 0)),
                   pl.BlockSpec((1, 1, S5_LANES), lambda i: (i, 0, 0)),
                   pl.BlockSpec((1, 1, S5_LANES), lambda i: (i, 0, 0)),
                   pl.BlockSpec((tl, S5_LANES), lambda i: (i, 0)), pl.BlockSpec((tl, S5_LANES), lambda i: (i, 0))],
        out_shape=[jax.ShapeDtypeStruct((L, 1024), MXU_DTYPE),
                   jax.ShapeDtypeStruct((nch, 1, S5_LANES), F32), jax.ShapeDtypeStruct((nch, 1, S5_LANES), F32),
                   jax.ShapeDtypeStruct((L, S5_LANES), MXU_DTYPE), jax.ShapeDtypeStruct((L, S5_LANES), MXU_DTYPE)],
        scratch_shapes=[pltpu.VMEM((tl, S5_LANES), F32), pltpu.VMEM((tl, S5_LANES), F32),
                        pltpu.VMEM((8, S5_LANES), F32), pltpu.VMEM((8, S5_LANES), F32),
                        pltpu.VMEM((1, S5_LANES), F32), pltpu.VMEM((1, S5_LANES), F32)],
        sem=("arbitrary",),
    )(p, p, wbd, cre, cim, atab, d_skip, w_glu, b_glu)


def s5_backward(p, dya, st_re, st_im, sv_re, sv_im, wbd, cre, cim, atab, d_skip, w_glu, b_glu, plan=None):
    L = p.shape[0]
    tl = min(TL_S5, L)
    t8 = tl // 8
    nch = L // tl
    rev = lambda i: (nch - 1 - i, 0)
    rev1 = lambda i: (nch - 1 - i, 1)
    rev3 = lambda i: (nch - 1 - i, 0, 0)
    ct_shape = (S5_KBLK, cre.shape[2], cre.shape[1])

    def body(u_ref, z_ref, dya_ref, str_ref, sti_ref, s_re, s_im, wbd_ref, cre_ref, cim_ref, at_ref,
             d_ref, wg_ref, bg_ref,
             dp_ref, dwbd_ref, dcre_ref, dcim_ref, dabr_ref, dabi_ref, dd_ref, dwg_ref, dbg_ref,
             g_re, g_im, car_re, car_im):
        @pl.when(pl.program_id(0) == 0)
        def _():
            car_re[...] = jnp.zeros_like(car_re)
            car_im[...] = jnp.zeros_like(car_im)
            for r in (dwbd_ref, dcre_ref, dcim_ref, dabr_ref, dabi_ref, dd_ref, dwg_ref, dbg_ref):
                r[...] = jnp.zeros_like(r)

        u = u_ref[...]
        a_re, a_im, at_re, at_im = at_ref[0:1], at_ref[1:2], at_ref[2:3], at_ref[3:4]
        y = _s5_readout(s_re, s_im, cre_ref, cim_ref) + d_ref[...] * u
        yg, dyg = _gelu_and_grad(y)
        gate = _sigmoid(_mm(yg, wg_ref[...]) + bg_ref[...])
        sz, dsz = _silu_and_grad(z_ref[...])
        dya = dya_ref[...]
        s5out = yg * gate
        dp_ref[:, 1024:] = (dya * s5out * dsz).astype(dp_ref.dtype)
        ds5 = dya * sz
        dt = ds5 * yg * gate * (1.0 - gate)
        dwg_ref[...] += _mm_tn(yg, dt)
        dbg_ref[...] += jnp.sum(dt, axis=0, keepdims=True)
        dyv = (ds5 * gate + _mm_nt(dt, wg_ref[...])) * dyg
        dd_ref[...] += jnp.sum(dyv * u, axis=0, keepdims=True)

        for k in range(S5_KBLK):
            lanes = slice(512 * k, 512 * (k + 1))
            dyk = dyv[:, 128 * k:128 * (k + 1)]
            g_re[:, lanes] = _mm_nt(dyk, cre_ref[k])
            g_im[:, lanes] = -_mm_nt(dyk, cim_ref[k])
            dcre_ref[k] += _mm_tn(dyk, s_re[:, lanes])
            dcim_ref[k] -= _mm_tn(dyk, s_im[:, lanes])

        blk = _LANE_BLK_BWD
        for b in range(S5_LANES // blk):
            lanes = slice(blk * b, blk * (b + 1))
            ar = jnp.broadcast_to(a_re[:, lanes], (8, blk))
            ai = jnp.broadcast_to(a_im[:, lanes], (8, blk))

            def local(j, carry, lanes=lanes, ar=ar, ai=ai):
                r = pl.multiple_of((t8 - 1 - j) * 8, 8)
                gr, gi = _cmulc_add(ar, ai, carry[0], carry[1], g_re[pl.ds(r, 8), lanes], g_im[pl.ds(r, 8), lanes])
                g_re[pl.ds(r, 8), lanes] = gr
                g_im[pl.ds(r, 8), lanes] = gi
                return gr, gi

            zero = jnp.zeros((8, blk), F32)
            fr, fi = lax.fori_loop(0, t8, local, (zero, zero), unroll=True)
            tr, ti = at_re[:, lanes], at_im[:, lanes]
            hr, hi = car_re[:, lanes], car_im[:, lanes]
            hrs, his = [hr], [hi]
            for j in range(7, -1, -1):
                hr, hi = _cmulc_add(tr, ti, hr, hi, fr[j:j + 1], fi[j:j + 1])
                hrs.append(hr)
                his.append(hi)
            car_re[:, lanes] = hrs[8]
            car_im[:, lanes] = his[8]
            in_r = jnp.concatenate(hrs[7::-1], axis=0)
            in_i = jnp.concatenate(his[7::-1], axis=0)

            wr, wi, nr, ni, accr, acci = in_r, in_i, zero, zero, zero, zero
            for pair in range(t8 // 2 - 1, -1, -1):
                rows = slice(16 * pair, 16 * pair + 16)
                s16r, s16i = s_re[rows, lanes].astype(F32), s_im[rows, lanes].astype(F32)
                for half in (1, 0):
                    r = 16 * pair + 8 * half
                    sr, si = s16r[8 * half:8 * half + 8], s16i[8 * half:8 * half + 8]
                    accr, acci = accr + (sr * nr + si * ni), acci + (sr * ni - si * nr)
                    wr, wi = ar * wr + ai * wi, ar * wi - ai * wr
                    nr, ni = g_re[r:r + 8, lanes] + wr, g_im[r:r + 8, lanes] + wi
                    g_re[r:r + 8, lanes] = nr
                    g_im[r:r + 8, lanes] = ni
            lr, li = s_re[tl - 16:tl, lanes].astype(F32)[8:], s_im[tl - 16:tl, lanes].astype(F32)[8:]
            row0 = lax.broadcasted_iota(jnp.int32, (8, blk), 0) == 0
            sr = jnp.where(row0, jnp.broadcast_to(str_ref[0][:, lanes], (8, blk)), pltpu.roll(lr, 1, 0))
            si = jnp.where(row0, jnp.broadcast_to(sti_ref[0][:, lanes], (8, blk)), pltpu.roll(li, 1, 0))
            dabr_ref[:, lanes] += accr + (sr * nr + si * ni)
            dabi_ref[:, lanes] += acci + (sr * ni - si * nr)

        dus = []
        for k in range(S5_KBLK):
            lanes = slice(512 * k, 512 * (k + 1))
            g = jnp.concatenate([g_re[:, lanes], g_im[:, lanes]], axis=1)
            dwbd_ref[k] += _mm_tn(u[:, 128 * k:128 * (k + 1)], g)
            dus.append(_mm_nt(g, wbd_ref[k]))
        du = jnp.concatenate(dus, axis=1) + dyv * d_ref[...]
        dp_ref[:, :1024] = du.astype(dp_ref.dtype)

    shp = lambda *s: jax.ShapeDtypeStruct(s, F32)
    return _call(
        body, plan, name="s5_backward", grid=(nch,),
        in_specs=[pl.BlockSpec((tl, 1024), rev), pl.BlockSpec((tl, 1024), rev1), pl.BlockSpec((tl, 1024), rev),
                  pl.BlockSpec((1, 1, S5_LANES), rev3), pl.BlockSpec((1, 1, S5_LANES), rev3),
                  pl.BlockSpec((tl, S5_LANES), rev), pl.BlockSpec((tl, S5_LANES), rev),
                  _full(wbd.shape), _full(cre.shape), _full(cim.shape), _full(atab.shape),
                  _full((1, 1024)), _full((1024, 1024)), _full((1, 1024))],
        out_specs=[pl.BlockSpec((tl, 2048), rev), _full(wbd.shape), _full(ct_shape), _full(ct_shape),
                   _full((8, S5_LANES)), _full((8, S5_LANES)), _full((1, 1024)), _full((1024, 1024)), _full((1, 1024))],
        out_shape=[jax.ShapeDtypeStruct((L, 2048), MXU_DTYPE), shp(*wbd.shape), shp(*ct_shape), shp(*ct_shape),
                   shp(8, S5_LANES), shp(8, S5_LANES), shp(1, 1024), shp(1024, 1024), shp(1, 1024)],
        scratch_shapes=[pltpu.VMEM((tl, S5_LANES), F32), pltpu.VMEM((tl, S5_LANES), F32),
                        pltpu.VMEM((1, S5_LANES), F32), pltpu.VMEM((1, S5_LANES), F32)],
        sem=("arbitrary",),
    )(p, p, dya, st_re, st_im, sv_re, sv_im, wbd, cre, cim, atab, d_skip, w_glu, b_glu)


def _block_diag(w, rows_first):
    g8 = w.reshape(S5_KBLK, 8, w.shape[1], w.shape[2])
    eye = jnp.eye(8, dtype=w.dtype)
    out = jnp.einsum('kgab,fg->kfagb', g8, eye)
    return out.reshape(S5_KBLK, 8 * w.shape[1], 8 * w.shape[2])


def _block_diag_extract(wbd, a, b):
    w5 = wbd.reshape(S5_KBLK, 8, a, 8, b)
    idx = jnp.arange(8)
    return w5[:, idx, :, idx, :].transpose(1, 0, 2, 3).reshape(S5_GROUPS, a, b)


def _ret_constants():
    log_g = np.log1p(-np.exp2(-5.0 - np.arange(RET_HEADS, dtype=np.float32))).astype(np.float32)
    idx = np.arange(RET_CHUNK, dtype=np.float32)
    diff = idx[:, None] - idx[None, :]
    decay = np.where(diff >= 0, np.exp(log_g[:, None, None] * np.maximum(diff, 0.0)), 0.0).astype(np.float32)
    xi = np.exp(log_g[None, :] * (idx[:, None] + 1.0)).astype(np.float32)
    zeta = np.exp(log_g[None, :] * (RET_CHUNK - 1.0 - idx[:, None])).astype(np.float32)
    chunk_decay = np.exp(log_g * RET_CHUNK).astype(np.float32)
    return decay, xi, zeta, chunk_decay


def _rope_tables(L):
    half = RET_DK // 2
    inv = ROPE_BASE ** (-jnp.arange(half, dtype=F32) / half)
    ang = jnp.arange(L, dtype=F32)[:, None] * inv[None, :]
    return jnp.cos(ang), jnp.sin(ang)


def _rot(xh, cos, sin):
    x1, x2 = xh[:, :128], xh[:, 128:]
    return jnp.concatenate([x1 * cos - x2 * sin, x1 * sin + x2 * cos], axis=1)


def _rot_t(dh, cos, sin):
    d1, d2 = dh[:, :128], dh[:, 128:]
    return jnp.concatenate([d1 * cos + d2 * sin, d2 * cos - d1 * sin], axis=1)


RET_PER_STEP = 4


def _ret_setup(L):
    nc = L // RET_CHUNK
    per = RET_PER_STEP if nc % RET_PER_STEP == 0 else 1
    decay_np, xi_np, zeta_np, cd_np = _ret_constants()
    tables = (jnp.asarray(decay_np), jnp.asarray(np.tile(xi_np, (per, 1))), jnp.asarray(np.tile(zeta_np, (per, 1))))
    return nc // per, per, tables, [float(c) for c in cd_np]


def _ret_rows(q_ref, k_ref, v_ref, cos_ref, sin_ref, xi_ref, zeta_ref):
    H = range(RET_HEADS)
    hs = [slice(RET_DK * h, RET_DK * (h + 1)) for h in H]
    cs, sn = cos_ref[...], sin_ref[...]
    qh = [_rot(q_ref[:, hs[h]], cs, sn) for h in H]
    kh = [_rot(k_ref[:, hs[h]], cs, sn) * (RET_DK ** -0.5) for h in H]
    vh = [v_ref[:, hs[h]] for h in H]
    qx = [qh[h] * xi_ref[:, h:h + 1] for h in H]
    kz = [kh[h] * zeta_ref[:, h:h + 1] for h in H]
    return hs, cs, sn, qh, kh, vh, qx, kz


def _ret_normed(qh, kh, vh, qx, dec_ref, prevs, per):
    H, C = range(RET_HEADS), range(per)
    rs = [slice(RET_CHUNK * c, RET_CHUNK * (c + 1)) for c in C]
    sc = [[_mm_nt(qh[h][rs[c]], kh[h][rs[c]]) * dec_ref[h] for h in H] for c in C]
    inner = [[_mm(sc[c][h], vh[h][rs[c]]) for h in H] for c in C]
    cross = [[_mm(qx[h][rs[c]], prevs[c][h]) for h in H] for c in C]
    o = [jnp.concatenate([inner[c][h] + cross[c][h] for c in C], axis=0) for h in H]
    oc = [o[h] - jnp.mean(o[h], axis=-1, keepdims=True) for h in H]
    rstd = [lax.rsqrt(jnp.mean(oc[h] * oc[h], axis=-1, keepdims=True) + NORM_EPS) for h in H]
    on = [oc[h] * rstd[h] for h in H]
    return rs, sc, rstd, on


def retention_forward(p, cos, sin, gain):
    L = p.shape[0]
    steps, per, (decay, xi, zeta), cd = _ret_setup(L)
    rows = RET_CHUNK * per

    def body(q_ref, k_ref, v_ref, z_ref, cos_ref, sin_ref, dec_ref, xi_ref, zeta_ref, gain_ref,
             yb_ref, prev_ref, state):
        @pl.when(pl.program_id(0) == 0)
        def _():
            state[...] = jnp.zeros_like(state)

        H, C = range(RET_HEADS), range(per)
        hs, cs, sn, qh, kh, vh, qx, kz = _ret_rows(q_ref, k_ref, v_ref, cos_ref, sin_ref, xi_ref, zeta_ref)
        prevs = [[state[h] for h in H]]
        for c in C:
            rs_c = slice(RET_CHUNK * c, RET_CHUNK * (c + 1))
            prevs.append([prevs[c][h] * cd[h] + _mm_tn(kz[h][rs_c], vh[h][rs_c]) for h in H])
        _, _, _, on = _ret_normed(qh, kh, vh, qx, dec_ref, prevs, per)
        sz, _ = _silu_and_grad(z_ref[...])
        for h in H:
            for c in C:
                prev_ref[c, h] = prevs[c][h].astype(prev_ref.dtype)
            state[h] = prevs[per][h]
            yb_ref[:, hs[h]] = (on[h] * gain_ref[:, hs[h]] * sz[:, hs[h]]).astype(yb_ref.dtype)

    col0 = p.shape[1] // 1024 - 4
    blk = lambda c: pl.BlockSpec((rows, 1024), lambda i, c=c: (i, c + col0))
    return pl.pallas_call(
        body, name="retention_forward", grid=(steps,),
        in_specs=[blk(0), blk(1), blk(2), blk(3),
                  pl.BlockSpec((rows, 128), lambda i: (i, 0)), pl.BlockSpec((rows, 128), lambda i: (i, 0)),
                  _full(decay.shape), _full(xi.shape), _full(zeta.shape), _full((1, 1024))],
        out_specs=[pl.BlockSpec((rows, 1024), lambda i: (i, 0)),
                   pl.BlockSpec((per, RET_HEADS, RET_DK, RET_DK), lambda i: (i, 0, 0, 0))],
        out_shape=[jax.ShapeDtypeStruct((L, 1024), MXU_DTYPE),
                   jax.ShapeDtypeStruct((steps * per, RET_HEADS, RET_DK, RET_DK), MXU_DTYPE)],
        scratch_shapes=[pltpu.VMEM((RET_HEADS, RET_DK, RET_DK), F32)],
        compiler_params=_cparams(("arbitrary",)),
    )(p, p, p, p, cos, sin, decay, xi, zeta, gain)


def retention_backward(p, dy, prevs, cos, sin, gain, plan=None):
    L = p.shape[0]
    steps, per, (decay, xi, zeta), cd = _ret_setup(L)
    rows = RET_CHUNK * per
    scale = RET_DK ** -0.5

    def body(q_ref, k_ref, v_ref, z_ref, dyb_ref, prev_ref, cos_ref, sin_ref, dec_ref, xi_ref, zeta_ref, gain_ref,
             dp_ref, dgain_ref, dstate):
        @pl.when(pl.program_id(0) == 0)
        def _():
            dstate[...] = jnp.zeros_like(dstate)
            dgain_ref[...] = jnp.zeros_like(dgain_ref)

        H, C = range(RET_HEADS), range(per)
        hs, cs, sn, qh, kh, vh, qx, kz = _ret_rows(q_ref, k_ref, v_ref, cos_ref, sin_ref, xi_ref, zeta_ref)
        prevs = [[prev_ref[c, h] for h in H] for c in C]
        rs, sc, rstd, on = _ret_normed(qh, kh, vh, qx, dec_ref, prevs, per)
        sz, dsz = _silu_and_grad(z_ref[...])
        dyb = dyb_ref[...]
        dong = [dyb[:, hs[h]] * sz[:, hs[h]] for h in H]
        don = [dong[h] * gain_ref[:, hs[h]] for h in H]
        do = [rstd[h] * (don[h] - jnp.mean(don[h], axis=-1, keepdims=True)
                         - on[h] * jnp.mean(don[h] * on[h], axis=-1, keepdims=True)) for h in H]
        dsc = [[_mm_nt(do[h][rs[c]], vh[h][rs[c]]) * dec_ref[h] for h in H] for c in C]
        dq_st = [[_mm_nt(do[h][rs[c]], prevs[c][h]) for h in H] for c in C]
        dnew = [[_mm_tn(qx[h][rs[c]], do[h][rs[c]]) for h in H] for c in C]
        dsts = [None] * per + [[dstate[h] for h in H]]
        for c in reversed(C):
            dsts[c] = [dsts[c + 1][h] * cd[h] + dnew[c][h] for h in H]
        dk_st = [[_mm_nt(vh[h][rs[c]], dsts[c + 1][h]) for h in H] for c in C]
        dv_st = [[_mm(kz[h][rs[c]], dsts[c + 1][h]) for h in H] for c in C]
        rows_of = lambda parts: jnp.concatenate(parts, axis=0)
        dqh = [rows_of([_mm(dsc[c][h], kh[h][rs[c]]) for c in C])
               + rows_of([dq_st[c][h] for c in C]) * xi_ref[:, h:h + 1] for h in H]
        dkh = [rows_of([_mm_tn(dsc[c][h], qh[h][rs[c]]) for c in C])
               + rows_of([dk_st[c][h] for c in C]) * zeta_ref[:, h:h + 1] for h in H]
        dvh = [rows_of([_mm_tn(sc[c][h], do[h][rs[c]]) + dv_st[c][h] for c in C]) for h in H]
        for h in H:
            dstate[h] = dsts[0][h]
            dgain_ref[:, hs[h]] += jnp.sum(dong[h] * on[h], axis=0, keepdims=True)
            dp_ref[:, hs[h]] = (_rot_t(dkh[h], cs, sn) * scale).astype(dp_ref.dtype)
            dp_ref[:, 1024 + RET_DK * h:1024 + RET_DK * (h + 1)] = dvh[h].astype(dp_ref.dtype)
            dp_ref[:, 2048 + RET_DK * h:2048 + RET_DK * (h + 1)] = (
                dyb[:, hs[h]] * on[h] * gain_ref[:, hs[h]] * dsz[:, hs[h]]).astype(dp_ref.dtype)
            dp_ref[:, 3072 + RET_DK * h:3072 + RET_DK * (h + 1)] = _rot_t(dqh[h], cs, sn).astype(dp_ref.dtype)

    col0 = p.shape[1] // 1024 - 4
    blk = lambda c: pl.BlockSpec((rows, 1024), lambda i, c=c: (steps - 1 - i, c + col0))
    tab = pl.BlockSpec((rows, 128), lambda i: (steps - 1 - i, 0))
    return _call(
        body, plan, name="retention_backward", grid=(steps,),
        in_specs=[blk(0), blk(1), blk(2), blk(3), pl.BlockSpec((rows, 1024), lambda i: (steps - 1 - i, 0)),
                  pl.BlockSpec((per, RET_HEADS, RET_DK, RET_DK), lambda i: (steps - 1 - i, 0, 0, 0)),
                  tab, tab, _full(decay.shape), _full(xi.shape), _full(zeta.shape), _full((1, 1024))],
        out_specs=[pl.BlockSpec((rows, 4096), lambda i: (steps - 1 - i, 0)), _full((1, 1024))],
        out_shape=[jax.ShapeDtypeStruct((L, 4096), MXU_DTYPE), jax.ShapeDtypeStruct((1, 1024), F32)],
        scratch_shapes=[pltpu.VMEM((RET_HEADS, RET_DK, RET_DK), F32)],
        sem=("arbitrary",),
    )(p, p, p, p, dy, prevs, cos, sin, decay, xi, zeta, gain)


def _sgu_mix(p_ref, gain_ref, wm_ref, bt_ref, tl):
    pu, pv, z = p_ref[:, :2048], p_ref[:, 2048:4096], p_ref[:, 4096:]
    (u, du), (v, dv) = _gelu_and_grad(pu), _gelu_and_grad(pv)
    mu = jnp.mean(v, axis=-1, keepdims=True)
    vc = v - mu
    rstd = lax.rsqrt(jnp.mean(vc * vc, axis=-1, keepdims=True) + NORM_EPS)
    vn = vc * rstd
    vg = vn * gain_ref[...]
    mask = (lax.broadcasted_iota(jnp.int32, (SGU_CHUNK, SGU_CHUNK), 0)
            >= lax.broadcasted_iota(jnp.int32, (SGU_CHUNK, SGU_CHUNK), 1))
    wms = [jnp.where(mask, wm_ref[g], 0.0) for g in range(SGU_GROUPS)]
    rows = []
    for c in range(tl // SGU_CHUNK):
        rs = slice(SGU_CHUNK * c, SGU_CHUNK * (c + 1))
        cols = []
        for g in range(SGU_GROUPS):
            gs = slice(SGU_GDIM * g, SGU_GDIM * (g + 1))
            cols.append(_mm(wms[g], vg[rs, gs]) + bt_ref[:, g:g + 1])
        rows.append(jnp.concatenate(cols, axis=1))
    s = rows[0] if len(rows) == 1 else jnp.concatenate(rows, axis=0)
    return du, dv, z, u, vn, rstd, vg, wms, mask, s


def sgu_forward(p, gain, wm, bt):
    L = p.shape[0]
    tl = min(TL_SGU, L)

    def body(p_ref, gain_ref, wm_ref, bt_ref, y_ref):
        _, _, z, u, _, _, _, _, _, s = _sgu_mix(p_ref, gain_ref, wm_ref, bt_ref, tl)
        sz, _ = _silu_and_grad(z)
        y_ref[...] = (u * s * sz).astype(y_ref.dtype)

    return pl.pallas_call(
        body, name="sgu_forward", grid=(L // tl,),
        in_specs=[pl.BlockSpec((tl, ODD_IN), lambda i: (i, 0)), _full((1, 2048)), _full(wm.shape), _full(bt.shape)],
        out_specs=pl.BlockSpec((tl, 2048), lambda i: (i, 0)),
        out_shape=jax.ShapeDtypeStruct((L, 2048), MXU_DTYPE),
        compiler_params=_cparams(("arbitrary",)),
    )(p, gain, wm, bt)


def sgu_backward(p, dy, gain, wm, bt, plan=None):
    L = p.shape[0]
    tl = min(TL_SGU, L)

    def body(p_ref, dy_ref, gain_ref, wm_ref, bt_ref, dp_ref, dgain_ref, dwm_ref, dbt_ref):
        @pl.when(pl.program_id(0) == 0)
        def _():
            dgain_ref[...] = jnp.zeros_like(dgain_ref)
            dwm_ref[...] = jnp.zeros_like(dwm_ref)
            dbt_ref[...] = jnp.zeros_like(dbt_ref)

        gu, gv, z, u, vn, rstd, vg, wms, mask, s = _sgu_mix(p_ref, gain_ref, wm_ref, bt_ref, tl)
        sz, dsz = _silu_and_grad(z)
        dyv = dy_ref[...]
        dp_ref[:, 4096:] = (dyv * u * s * dsz).astype(dp_ref.dtype)
        dsg = dyv * sz
        dp_ref[:, :2048] = (dsg * s * gu).astype(dp_ref.dtype)
        ds = dsg * u
        rows = []
        dbs = [jnp.zeros((SGU_CHUNK, 1), F32) for _ in range(SGU_GROUPS)]
        for c in range(tl // SGU_CHUNK):
            rs = slice(SGU_CHUNK * c, SGU_CHUNK * (c + 1))
            cols = []
            for g in range(SGU_GROUPS):
                gs = slice(SGU_GDIM * g, SGU_GDIM * (g + 1))
                dsg_c = ds[rs, gs]
                dbs[g] = dbs[g] + jnp.sum(dsg_c, axis=1, keepdims=True)
                dwm_ref[g] += jnp.where(mask, _mm_nt(dsg_c, vg[rs, gs]), 0.0)
                cols.append(_mm_tn(wms[g], dsg_c))
            rows.append(jnp.concatenate(cols, axis=1))
        dbt_ref[...] += jnp.concatenate(dbs, axis=1)
        dvg = rows[0] if len(rows) == 1 else jnp.concatenate(rows, axis=0)
        dgain_ref[...] += jnp.sum(dvg * vn, axis=0, keepdims=True)
        dvn = dvg * gain_ref[...]
        dv = rstd * (dvn - jnp.mean(dvn, axis=-1, keepdims=True) - vn * jnp.mean(dvn * vn, axis=-1, keepdims=True))
        dp_ref[:, 2048:4096] = (dv * gv).astype(dp_ref.dtype)

    return _call(
        body, plan, name="sgu_backward", grid=(L // tl,),
        in_specs=[pl.BlockSpec((tl, ODD_IN), lambda i: (i, 0)), pl.BlockSpec((tl, 2048), lambda i: (i, 0)),
                  _full((1, 2048)), _full(wm.shape), _full(bt.shape)],
        out_specs=[pl.BlockSpec((tl, ODD_IN), lambda i: (i, 0)), _full((1, 2048)), _full(wm.shape), _full(bt.shape)],
        out_shape=[jax.ShapeDtypeStruct((L, ODD_IN), MXU_DTYPE), jax.ShapeDtypeStruct((1, 2048), F32),
                   jax.ShapeDtypeStruct(wm.shape, F32), jax.ShapeDtypeStruct(bt.shape, F32)],
        sem=("arbitrary",),
    )(p, dy, gain, wm, bt)


def cast_shards(mats):
    n = len(mats)
    steps = 8

    def body(*refs):
        for p in range(n):
            refs[n + p][...] = refs[p][...].astype(MXU_DTYPE)

    specs = [pl.BlockSpec((m.shape[0] // steps, m.shape[1]), lambda i: (i, 0)) for m in mats]
    return pl.pallas_call(
        body, name="cast_shards", grid=(steps,), in_specs=specs, out_specs=specs,
        out_shape=[jax.ShapeDtypeStruct(m.shape, MXU_DTYPE) for m in mats],
        compiler_params=_cparams(("arbitrary",)),
    )(*mats)


def local_grads(x, tgt, w):
    L = x.shape[0]
    ne, gf = w["norm_even"], w["final_norm"].reshape(1, D_MODEL)
    sh = dict(zip(MATRICES, cast_shards([w[n][0] for n in MATRICES])))
    lam_re, lam_im = w["s5_lam_re"][0], w["s5_lam_im"][0]
    log_dt = w["s5_log_dt"].reshape(S5_GROUPS, 1)
    bt_re = jnp.transpose(w["s5_b_re"][0], (2, 0, 1))
    bt_im = jnp.transpose(w["s5_b_im"][0], (2, 0, 1))
    c_re, c_im = w["s5_c_re"][0], w["s5_c_im"][0]
    wm = w["sgu_w_spatial"][0]
    bt = jnp.transpose(w["sgu_b_spatial"][0])

    tl5 = min(TL_S5, L)
    ab_re, ab_im, bb_re, bb_im, at_re, at_im = s5_params_fwd(lam_re, lam_im, log_dt, bt_re, bt_im, tl5 // 8)
    atab = jnp.stack([ab_re.reshape(S5_LANES), ab_im.reshape(S5_LANES),
                      at_re.reshape(S5_LANES), at_im.reshape(S5_LANES)])
    wbd = jnp.concatenate([_block_diag(jnp.transpose(bb_re, (1, 0, 2)), True),
                           _block_diag(jnp.transpose(bb_im, (1, 0, 2)), True)], axis=2).astype(MXU_DTYPE)
    cre = _block_diag(jnp.transpose(c_re, (0, 2, 1)), True).astype(MXU_DTYPE)
    cim = _block_diag(jnp.transpose(c_im, (0, 2, 1)), True).astype(MXU_DTYPE)
    cos, sin = _rope_tables(L)

    s5_cols = 2 * S5_WIDTH
    me = (2 * lax.axis_index("x") + lax.axis_index("y")).astype(jnp.int32)
    xs = stream_order(x, tl5)
    slab = lambda d: jnp.stack([me ^ d])
    zero = jnp.zeros((1,), jnp.int32)
    shards = [sh["w_in_even"][None]]
    (p1, h0s, h0), (got,) = even_in_slabs(x, xs, ne, shards[0], slab(0), zero, "even_in_0",
                                          plan=gather_plan([sh["w_in_even"]], only=1))
    for d in (1, 2, 3):
        shards.append(got)
        plan = gather_plan([sh["w_in_even"]], only=d + 1) if d < 3 else gather_plan([sh["s5_w_glu"]])
        (p1,), (got,) = even_in_slabs(x, xs, ne, shards[d], slab(d), zero, "even_in_%d" % d, p_in=p1, plan=plan)
    w_glu = got
    by_xor = jnp.concatenate(shards)
    w_glu = w_glu.reshape(S5_WIDTH, S5_WIDTH)
    (ya, st_re, st_im, sv_re, sv_im), (w_out_e, w_in_o, w_out_o, no, sg_gain) = s5_forward(
        p1, wbd, cre, cim, atab, w["s5_d"], w_glu, w["s5_b_glu"],
        gather_plan([sh["w_out_even"], sh["w_in_odd"], sh["w_out_odd"], w["norm_odd"], w["sgu_norm_gain"]]))
    w_out_e = w_out_e.reshape(2 * S5_WIDTH, D_MODEL)
    w_out_o = w_out_o.reshape(SGU_WIDTH, D_MODEL)
    no, sg_gain = no.reshape(1, D_MODEL), sg_gain.reshape(1, SGU_WIDTH)
    yb, prevs = retention_forward(p1, cos, sin, w["ret_gn_gain"])
    ya = token_order(ya, tl5)
    x1 = matmul_residual([ya, yb], w_out_e, x, "even_out")
    (p2, h1), _ = norm_matmul(x1, no, w_in_o, "odd_in")
    y2 = sgu_forward(p2, sg_gain, wm, bt)
    dx2, loss, dgf = out_proj_loss(y2, w_out_o, x1, gf, tgt, "odd_out_loss")

    g, landed = {}, {}
    shard_major = lambda a, n: a.reshape((N_CHIPS,) + w[n].shape[1:])
    dy2, g_w_out_o = out_proj_bwd(dx2, w_out_o, [y2], "odd_out_bwd")
    (dp2, g["sgu_norm_gain"], dwm, dbt), (landed["w_out_odd"],) = sgu_backward(
        p2, dy2, sg_gain, wm, bt, reduce_plan([shard_major(g_w_out_o, "w_out_odd")]))
    g_w_in_o, _ = in_proj_bwd_dw(h1, dp2, "odd_in_dw", ODD_IN // N_CHIPS)
    (dx1, g["norm_odd"]), _ = in_proj_bwd_dx(x1, no, [dp2], [w_in_o], dx2, "odd_in_dx")
    dya, dyb, g_w_out_e = out_proj_bwd(dx1, w_out_e, [ya, yb], "even_out_bwd")
    ((dpa, dwbd, dcre, dcim, dab_re, dab_im, g["s5_d"], g_w_glu, g["s5_b_glu"]),
     (landed["w_in_odd"], landed["w_out_even"])) = s5_backward(
        p1, stream_order(dya, tl5), st_re, st_im, sv_re, sv_im, wbd, cre, cim, atab, w["s5_d"], w_glu,
        w["s5_b_glu"], reduce_plan([g_w_in_o, shard_major(g_w_out_e, "w_out_even")]))

    dbb_re = jnp.transpose(_block_diag_extract(dwbd[:, :, :512], S5_GROUP, S5_STATE), (1, 0, 2))
    dbb_im = jnp.transpose(_block_diag_extract(dwbd[:, :, 512:], S5_GROUP, S5_STATE), (1, 0, 2))
    dlr, dli, ddt, dbt_re, dbt_im = s5_params_bwd(
        lam_re, lam_im, log_dt, bt_re, bt_im, dab_re.reshape(8, S5_GROUPS, S5_STATE),
        dab_im.reshape(8, S5_GROUPS, S5_STATE), dbb_re, dbb_im)
    g["s5_lam_re"], g["s5_lam_im"] = dlr[None], dli[None]
    g["s5_log_dt"] = ddt.reshape(1, S5_GROUPS)
    g["s5_b_re"], g["s5_b_im"] = dbt_re, dbt_im
    g["s5_c_re"] = _block_diag_extract(dcre, S5_GROUP, S5_STATE)[None]
    g["s5_c_im"] = _block_diag_extract(dcim, S5_GROUP, S5_STATE)[None]
    g["sgu_w_spatial"] = dwm[None]
    g["sgu_b_spatial"] = jnp.transpose(dbt)[None]
    g["final_norm"] = dgf.reshape(D_MODEL)
    g["loss"] = loss

    big_small = ("s5_b_re", "s5_b_im")
    mid_small = ("s5_c_re",)
    (dpb, g["ret_gn_gain"]), recv = retention_backward(
        p1, dyb, prevs, cos, sin, w["ret_gn_gain"],
        reduce_plan([shard_major(g_w_glu, "s5_w_glu")], [g[n] for n in big_small]))
    landed.update(zip(("s5_w_glu",) + big_small, recv))
    done = tuple(n for n in MATRICES if n != "w_in_even")
    part = {n: sum_slabs(landed[n], "sum_" + n) for n in done}
    g_w_in_e, recv = in_proj_bwd_dw(h0s, dpa, "even_in_dw_s5", 512, dtype=MXU_DTYPE,
                                    plan=_SiblingPlan([part[n] for n in done]))
    other = dict(zip(done, recv))
    small = tuple(n for n in SMALL if n != "norm_even" and n not in big_small + mid_small) + ("loss",)
    wb = EVEN_IN // N_CHIPS
    g_w_in_e, recv = in_proj_bwd_dw(h0, dpb, "even_in_dw_q", 512, first=s5_cols // 512, into=g_w_in_e,
                                    dtype=MXU_DTYPE, dp_first=2 * wb // 512, count=RET_HEADS * RET_DK // 512,
                                    plan=reduce_plan([], [g[n] for n in mid_small]))
    landed.update(zip(mid_small, recv))
    g_w_in_e, recv = in_proj_bwd_dw(h0, dpb, "even_in_dw_kvz", wb, first=2, into=g_w_in_e, dtype=MXU_DTYPE,
                                    count=2, plan=reduce_plan([], [g[n] for n in small]))
    landed.update(zip(small, recv))
    (dx0, g["norm_even"]), (landed["w_in_even"],) = even_in_bwd_dx(
        x, ne, token_order(dpa, tl5), dpb, by_xor, me ^ jnp.arange(N_CHIPS, dtype=jnp.int32), dx1, "even_in_dx",
        reduce_plan([g_w_in_e]))
    landed["norm_even"] = reduce_plan([], [g["norm_even"]])
    return dx0, landed, part, other


def _row_block(rows):
    return 128 if rows % 128 == 0 else rows


def sum_slabs(r, name, plan=None):
    _, R, C = r.shape
    tr = _row_block(R)

    def body(r_ref, o_ref):
        a, b, c, d = (r_ref[k].astype(F32) for k in range(N_CHIPS))
        o_ref[...] = (a + b) + (c + d)

    res = _call(
        body, plan, name=name, grid=(R // tr,),
        in_specs=[pl.BlockSpec((N_CHIPS, tr, C), lambda i: (0, i, 0))],
        out_specs=pl.BlockSpec((tr, C), lambda i: (i, 0)),
        out_shape=jax.ShapeDtypeStruct((R, C), F32),
        sem=("arbitrary",),
    )(r)
    return res[0] if plan is None else res


def _adam(w, m, v, g):
    mn = ADAM_B1 * m + (1.0 - ADAM_B1) * g
    vn = ADAM_B2 * v + (1.0 - ADAM_B2) * (g * g)
    m_hat = mn / (1.0 - ADAM_B1 ** ADAM_STEP)
    v_hat = vn / (1.0 - ADAM_B2 ** ADAM_STEP)
    return -ADAM_LR * (m_hat / (jnp.sqrt(v_hat) + ADAM_EPS) + ADAM_WD * w), mn, vn


def adam_update(w, m, v, ga, gb, name, plan=None):
    R, C = w.shape
    tr = _row_block(R)

    def body(w_ref, m_ref, v_ref, ga_ref, gb_ref, g_out, d_out, m_out, v_out):
        g = ga_ref[...] + gb_ref[...]
        g_out[...] = g
        d_out[...], m_out[...], v_out[...] = _adam(w_ref[...], m_ref[...], v_ref[...], g)

    blk = pl.BlockSpec((tr, C), lambda i: (i, 0))
    return _call(
        body, plan, name=name, grid=(R // tr,),
        in_specs=[blk] * 5, out_specs=[blk] * 4,
        out_shape=[jax.ShapeDtypeStruct((R, C), F32)] * 4,
        sem=("arbitrary",),
    )(w, m, v, ga, gb)


WIDE_ROWS = ("s5_b_re", "s5_b_im")


def sum_small(landed):
    def body(*refs):
        k = len(refs) // 2
        for i in range(k):
            r = refs[i]
            refs[k + i][...] = (r[0] + r[1]) + (r[2] + r[3])

    names = list(landed)
    res = pl.pallas_call(
        body, name="sum_small", out_shape=[jax.ShapeDtypeStruct(landed[n].shape[1:], F32) for n in names],
        compiler_params=pltpu.CompilerParams(vmem_limit_bytes=VMEM_LIMIT),
    )(*[landed[n] for n in names])
    return dict(zip(names, res))


def adam_small(names, w, m, v, ga, gb):
    def body(*refs):
        k = len(refs) // 9
        me = 2 * lax.axis_index("x") + lax.axis_index("y")
        for i in range(k):
            w_ref, m_ref, v_ref, ga_ref, gb_ref = refs[i], refs[k + i], refs[2 * k + i], refs[3 * k + i], refs[4 * k + i]
            size = w_ref.shape[-1]
            if ga_ref.shape != w_ref.shape:
                part = pl.ds(pl.multiple_of(me * size, LANES), size)
                g = ga_ref[:, part] + gb_ref[:, part]
            else:
                g = ga_ref[...] + gb_ref[...]
            refs[5 * k + i][...] = g
            refs[6 * k + i][...], refs[7 * k + i][...], refs[8 * k + i][...] = _adam(w_ref[...], m_ref[...], v_ref[...], g)

    ins = [d[n] for d in (w, m, v, ga, gb) for n in names]
    outs = [jax.ShapeDtypeStruct(w[n].shape, F32) for _ in range(4) for n in names]
    res = pl.pallas_call(body, name="adam_small", out_shape=outs,
                         compiler_params=pltpu.CompilerParams(vmem_limit_bytes=VMEM_LIMIT))(*ins)
    k = len(names)
    return [dict(zip(names, res[j * k:(j + 1) * k])) for j in range(4)]


WEIGHTS = ("norm_even", "w_in_even", "s5_lam_re", "s5_lam_im", "s5_log_dt", "s5_b_re", "s5_b_im", "s5_c_re",
           "s5_c_im", "s5_d", "s5_w_glu", "s5_b_glu", "ret_gn_gain", "w_out_even", "norm_odd", "w_in_odd",
           "sgu_norm_gain", "sgu_w_spatial", "sgu_b_spatial", "w_out_odd", "final_norm")
MATRICES = ("w_in_even", "s5_w_glu", "w_out_even", "w_in_odd", "w_out_odd")
SHARDED_VECS = ("norm_odd", "sgu_norm_gain")
REPLICATED = tuple(n for n in WEIGHTS if n not in MATRICES and n not in SHARDED_VECS)
SMALL = tuple(n for n in WEIGHTS if n not in MATRICES)
LANES = 128


def kernel(x, norm_even, w_in_even, s5_lam_re, s5_lam_im, s5_log_dt, s5_b_re, s5_b_im, s5_c_re, s5_c_im, s5_d, s5_w_glu, s5_b_glu, ret_gn_gain, w_out_even, norm_odd, w_in_odd, sgu_norm_gain, sgu_w_spatial, sgu_b_spatial, w_out_odd, final_norm, loss_target, m_norm_even, m_w_in_even, m_s5_lam_re, m_s5_lam_im, m_s5_log_dt, m_s5_b_re, m_s5_b_im, m_s5_c_re, m_s5_c_im, m_s5_d, m_s5_w_glu, m_s5_b_glu, m_ret_gn_gain, m_w_out_even, m_norm_odd, m_w_in_odd, m_sgu_norm_gain, m_sgu_w_spatial, m_sgu_b_spatial, m_w_out_odd, m_final_norm, v_norm_even, v_w_in_even, v_s5_lam_re, v_s5_lam_im, v_s5_log_dt, v_s5_b_re, v_s5_b_im, v_s5_c_re, v_s5_c_im, v_s5_d, v_s5_w_glu, v_s5_b_glu, v_ret_gn_gain, v_w_out_even, v_norm_odd, v_w_in_odd, v_sgu_norm_gain, v_sgu_w_spatial, v_sgu_b_spatial, v_w_out_odd, v_final_norm):
    w = dict(norm_even=norm_even, w_in_even=w_in_even, s5_lam_re=s5_lam_re, s5_lam_im=s5_lam_im, s5_log_dt=s5_log_dt, s5_b_re=s5_b_re, s5_b_im=s5_b_im, s5_c_re=s5_c_re, s5_c_im=s5_c_im, s5_d=s5_d, s5_w_glu=s5_w_glu, s5_b_glu=s5_b_glu, ret_gn_gain=ret_gn_gain, w_out_even=w_out_even, norm_odd=norm_odd, w_in_odd=w_in_odd, sgu_norm_gain=sgu_norm_gain, sgu_w_spatial=sgu_w_spatial, sgu_b_spatial=sgu_b_spatial, w_out_odd=w_out_odd, final_norm=final_norm)
    m = dict(norm_even=m_norm_even, w_in_even=m_w_in_even, s5_lam_re=m_s5_lam_re, s5_lam_im=m_s5_lam_im, s5_log_dt=m_s5_log_dt, s5_b_re=m_s5_b_re, s5_b_im=m_s5_b_im, s5_c_re=m_s5_c_re, s5_c_im=m_s5_c_im, s5_d=m_s5_d, s5_w_glu=m_s5_w_glu, s5_b_glu=m_s5_b_glu, ret_gn_gain=m_ret_gn_gain, w_out_even=m_w_out_even, norm_odd=m_norm_odd, w_in_odd=m_w_in_odd, sgu_norm_gain=m_sgu_norm_gain, sgu_w_spatial=m_sgu_w_spatial, sgu_b_spatial=m_sgu_b_spatial, w_out_odd=m_w_out_odd, final_norm=m_final_norm)
    v = dict(norm_even=v_norm_even, w_in_even=v_w_in_even, s5_lam_re=v_s5_lam_re, s5_lam_im=v_s5_lam_im, s5_log_dt=v_s5_log_dt, s5_b_re=v_s5_b_re, s5_b_im=v_s5_b_im, s5_c_re=v_s5_c_re, s5_c_im=v_s5_c_im, s5_d=v_s5_d, s5_w_glu=v_s5_w_glu, s5_b_glu=v_s5_b_glu, ret_gn_gain=v_ret_gn_gain, w_out_even=v_w_out_even, norm_odd=v_norm_odd, w_in_odd=v_w_in_odd, sgu_norm_gain=v_sgu_norm_gain, sgu_w_spatial=v_sgu_w_spatial, sgu_b_spatial=v_sgu_b_spatial, w_out_odd=v_w_out_odd, final_norm=v_final_norm)

    grad_x, landed, part, other = local_grads(x[0], loss_target[0], w)

    small = SMALL + ("loss",)
    part["w_in_even"], (landed["norm_even"],) = sum_slabs(landed["w_in_even"], "sum_w_in_even", landed["norm_even"])
    part.update(sum_small({n: landed[n] for n in small}))
    names = ("w_in_even",) + small
    swap = _SiblingPlan([part[n] for n in names])

    host = "w_in_odd"
    res_host, swapped = adam_update(w[host][0], m[host][0], v[host][0], part[host], other[host], "adam_" + host, swap)
    other.update(zip(names, swapped))
    wt, mt, vt = dict(w), dict(m), dict(v)
    for n in WIDE_ROWS:
        wt[n], mt[n], vt[n] = (jnp.transpose(a[n][0], (2, 0, 1)) for a in (w, m, v))
    out_g, out_d, out_m, out_v = adam_small(SMALL, wt, mt, vt, part, other)
    for n in WIDE_ROWS:
        for out in (out_g, out_d, out_m, out_v):
            out[n] = jnp.transpose(out[n], (1, 2, 0))[None]
    for n in MATRICES:
        res = res_host if n == host else adam_update(w[n][0], m[n][0], v[n][0], part[n], other[n], "adam_" + n)[0]
        out_g[n], out_d[n], out_m[n], out_v[n] = (r[None] for r in res)
    total_loss = (part["loss"] + other["loss"])[0, 0]

    return (total_loss, grad_x[None], *[out_g[n] for n in WEIGHTS], *[out_d[n] for n in WEIGHTS],
            *[out_m[n] for n in WEIGHTS], *[out_v[n] for n in WEIGHTS])
```
